```python
import math
import jax, jax.numpy as jnp
from jax import lax
import numpy as np

D_MODEL = 1024
BATCH = 8
SEQ = 2048
DEPTH = 2

D_MIX = D_MODEL
DA_HEADS = 4
DA_HEAD_DIM = 64
DA_V_DIM = 2 * DA_HEAD_DIM
DA_WIDTH = DA_HEADS * DA_V_DIM
DA_QK_WIDTH = DA_HEADS * 2 * DA_HEAD_DIM
Q_BLOCK = 128
ROPE_THETA = 10000.0
ML_HEADS = 4
ML_HEAD_DIM = 64
ML_WIDTH = ML_HEADS * ML_HEAD_DIM
ML_CHUNK = 64
ML_CONV = 4
S5_GROUP = 16
S5_STATE = 64
S5_WIDTH = D_MIX - DA_WIDTH - ML_WIDTH
S5_GROUPS = S5_WIDTH // S5_GROUP
N_EXPERTS = 32
TOP_K = 4
D_EXPERT = D_MODEL
SWIGLU_LIMIT = 7.0
SWIGLU_ALPHA = 1.702
DN_ALPHA = (2 * DEPTH) ** 0.25
DN_BETA = (8 * DEPTH) ** -0.25
LN_EPS = 1e-5
NEG = -1e30

OFF_DA_K = DA_QK_WIDTH
OFF_DA_V = 2 * DA_QK_WIDTH
OFF_ML_X = OFF_DA_V + DA_WIDTH
OFF_ML_V = OFF_ML_X + ML_WIDTH
OFF_ML_O = OFF_ML_V + ML_WIDTH
OFF_ML_I = OFF_ML_O + ML_WIDTH
OFF_ML_F = OFF_ML_I + ML_HEADS
OFF_S5_U = OFF_ML_F + ML_HEADS
N_IN = OFF_S5_U + S5_WIDTH
IN_SPLITS = (OFF_DA_K, OFF_DA_V, OFF_ML_X, OFF_ML_V, OFF_ML_O, OFF_ML_I, OFF_ML_F, OFF_S5_U)

kernel_name = "hybrid_diffattn_mlstm_s5_moe_block"


def layer_norm(x, g, b):
    xf = x.astype(jnp.float32)
    mu = jnp.mean(xf, axis=-1, keepdims=True)
    var = jnp.mean(jnp.square(xf - mu), axis=-1, keepdims=True)
    return (xf - mu) * lax.rsqrt(var + LN_EPS) * g + b


def rms_norm(t, g):
    tf = t.astype(jnp.float32)
    return tf * lax.rsqrt(jnp.mean(jnp.square(tf), axis=-1, keepdims=True) + LN_EPS) * g


def rope_tables(positions):
    inv = ROPE_THETA ** (-jnp.arange(0, DA_HEAD_DIM, 2, dtype=jnp.float32) / DA_HEAD_DIM)
    ang = positions.astype(jnp.float32)[..., None] * inv
    return jnp.cos(ang), jnp.sin(ang)


def apply_rope(t, cos, sin):
    t = t.astype(jnp.float32)
    half = DA_HEAD_DIM // 2
    t1, t2 = t[..., :half], t[..., half:]
    cs = cos[:, :, None, None, :]
    sn = sin[:, :, None, None, :]
    return jnp.concatenate([t1 * cs - t2 * sn, t2 * cs + t1 * sn], axis=-1)


def diff_attention(q, k, v, cos, sin, lam, lam_init, gain):
    B, S, _ = q.shape
    q = apply_rope(q.reshape(B, S, DA_HEADS, 2, DA_HEAD_DIM), cos, sin)
    k = apply_rope(k.reshape(B, S, DA_HEADS, 2, DA_HEAD_DIM), cos, sin)
    q = q.transpose(0, 2, 3, 1, 4)
    k = k.transpose(0, 2, 3, 1, 4)
    v = v.astype(jnp.float32).reshape(B, S, DA_HEADS, DA_V_DIM).transpose(0, 2, 1, 3)
    nb = S // Q_BLOCK
    qb = jnp.moveaxis(q.reshape(B, DA_HEADS, 2, nb, Q_BLOCK, DA_HEAD_DIM), 3, 0)
    kpos = jnp.arange(S)
    scale = DA_HEAD_DIM ** -0.5

    def attend(args):
        qi, bi = args
        s = jnp.einsum('bhcqd,bhckd->bhcqk', qi, k) * scale
        qpos = bi * Q_BLOCK + jnp.arange(Q_BLOCK)
        s = jnp.where(kpos[None, :] <= qpos[:, None], s, NEG)
        p = jax.nn.softmax(s, axis=-1)
        a = p[:, :, 0] - lam * p[:, :, 1]
        return jnp.einsum('bhqk,bhkv->bhqv', a, v)

    o = lax.map(attend, (qb, jnp.arange(nb)))
    o = o.transpose(1, 0, 3, 2, 4).reshape(B, S, DA_HEADS, DA_V_DIM)
    o = rms_norm(o, gain) * (1.0 - lam_init)
    return o.reshape(B, S, DA_WIDTH)


def mlstm(x_m, v, o_pre, i_pre, f_pre, conv_w, conv_b, w_q, w_k, gate_b, norm_g):
    B, S, _ = x_m.shape
    H, d, L = ML_HEADS, ML_HEAD_DIM, ML_CHUNK
    nc = S // L
    x_m = x_m.astype(jnp.float32)
    xc = lax.conv_general_dilated(
        x_m, conv_w.astype(jnp.float32)[:, None, :], window_strides=(1,),
        padding=[(ML_CONV - 1, 0)], dimension_numbers=('NWC', 'WIO', 'NWC'),
        feature_group_count=ML_WIDTH)
    xc = jax.nn.silu(xc + conv_b).reshape(B, S, H, d)
    q = jnp.einsum('bshd,hde->bhse', xc, w_q)
    k = jnp.einsum('bshd,hde->bhse', xc, w_k) * (d ** -0.5)
    v = v.astype(jnp.float32).reshape(B, S, H, d).transpose(0, 2, 1, 3)
    ig = (i_pre + gate_b[:H]).astype(jnp.float32).transpose(0, 2, 1)
    lf = jax.nn.log_sigmoid((f_pre + gate_b[H:]).astype(jnp.float32)).transpose(0, 2, 1)

    qc = q.reshape(B, H, nc, L, d)
    kc = k.reshape(B, H, nc, L, d)
    vc = v.reshape(B, H, nc, L, d)
    ic = ig.reshape(B, H, nc, L)
    bcum = jnp.cumsum(lf.reshape(B, H, nc, L), axis=-1)
    g = bcum[..., -1]
    a = g[..., None] - bcum + ic

    def step(carry, xs):
        C, n, m = carry
        k_, v_, a_, g_ = xs
        m_new = jnp.maximum(g_ + m, jnp.max(a_, axis=-1))
        dec = jnp.exp(g_ + m - m_new)
        w = jnp.exp(a_ - m_new[..., None])
        C_new = dec[..., None, None] * C + jnp.einsum('bhl,bhlv,bhlk->bhvk', w, v_, k_)
        n_new = dec[..., None] * n + jnp.einsum('bhl,bhlk->bhk', w, k_)
        return (C_new, n_new, m_new), (C, n, m)

    init = (jnp.zeros((B, H, d, d), jnp.float32), jnp.zeros((B, H, d), jnp.float32),
            jnp.full((B, H), NEG, jnp.float32))
    xs = (jnp.moveaxis(kc, 2, 0), jnp.moveaxis(vc, 2, 0), jnp.moveaxis(a, 2, 0), jnp.moveaxis(g, 2, 0))
    _, (C_prev, n_prev, m_prev) = lax.scan(step, init, xs)
    C_prev = jnp.moveaxis(C_prev, 0, 2)
    n_prev = jnp.moveaxis(n_prev, 0, 2)
    m_prev = jnp.moveaxis(m_prev, 0, 2)

    causal = jnp.tril(jnp.ones((L, L), dtype=bool))
    log_d = jnp.where(causal, bcum[..., :, None] - bcum[..., None, :] + ic[..., None, :], NEG)
    inter = bcum + m_prev[..., None]
    m = jnp.maximum(inter, jnp.max(log_d, axis=-1))
    s = jnp.einsum('bhcld,bhcsd->bhcls', qc, kc) * jnp.exp(log_d - m[..., None])
    dec = jnp.exp(inter - m)
    num = jnp.einsum('bhcls,bhcsd->bhcld', s, vc) + dec[..., None] * jnp.einsum('bhcvk,bhclk->bhclv', C_prev, qc)
    den = jnp.sum(s, axis=-1) + dec * jnp.einsum('bhck,bhclk->bhcl', n_prev, qc)
    h = num / jnp.maximum(jnp.abs(den), jnp.exp(-m))[..., None]
    h = h.reshape(B, H, S, d).transpose(0, 2, 1, 3)
    h = rms_norm(h, norm_g.reshape(H, d)).reshape(B, S, ML_WIDTH)
    return h * jax.nn.sigmoid(o_pre.astype(jnp.float32))


def complex_affine_combine(e1, e2):
    a1r, a1i, b1r, b1i = e1
    a2r, a2i, b2r, b2i = e2
    return (a1r * a2r - a1i * a2i, a1r * a2i + a1i * a2r,
            a2r * b1r - a2i * b1i + b2r, a2r * b1i + a2i * b1r + b2i)


def s5_layer(u, a_re, a_im, log_dt, b_re, b_im, c_re, c_im, d_skip, w_glu):
    B, S, _ = u.shape
    G, Hc, P = S5_GROUPS, S5_GROUP, S5_STATE
    u = u.astype(jnp.float32).reshape(B, S, G, Hc)
    a_re = a_re.astype(jnp.float32)
    a_im = a_im.astype(jnp.float32)
    dt = jnp.exp(log_dt.astype(jnp.float32))[:, None]
    mag = jnp.exp(a_re * dt)
    ab_re = mag * jnp.cos(a_im * dt)
    ab_im = mag * jnp.sin(a_im * dt)
    nr, ni = ab_re - 1.0, ab_im
    den = a_re * a_re + a_im * a_im
    fr = (nr * a_re + ni * a_im) / den
    fi = (ni * a_re - nr * a_im) / den
    bb_re = fr[..., None] * b_re - fi[..., None] * b_im
    bb_im = fr[..., None] * b_im + fi[..., None] * b_re
    bu_re = jnp.einsum('bsgh,gph->bsgp', u, bb_re)
    bu_im = jnp.einsum('bsgh,gph->bsgp', u, bb_im)
    A_re = jnp.broadcast_to(ab_re, (1, S, G, P))
    A_im = jnp.broadcast_to(ab_im, (1, S, G, P))
    _, _, x_re, x_im = lax.associative_scan(complex_affine_combine, (A_re, A_im, bu_re, bu_im), axis=1)
    y = (jnp.einsum('bsgp,ghp->bsgh', x_re, c_re) - jnp.einsum('bsgp,ghp->bsgh', x_im, c_im)
         + d_skip.reshape(G, Hc) * u)
    z = jnp.einsum('bsgh,ghj->bsgj', jax.nn.gelu(y), w_glu)
    out = z[..., :Hc] * jax.nn.sigmoid(z[..., Hc:])
    return out.reshape(B, S, S5_WIDTH)


def hybrid_mixer(h, cos, sin, layer, w_in, w_out, lam_q1, lam_k1, lam_q2, lam_k2, da_norm_g,
                 ml_conv_w, ml_conv_b, ml_w_q, ml_w_k, ml_gate_b, ml_norm_g,
                 s5_a_re, s5_a_im, s5_log_dt, s5_b_re, s5_b_im, s5_c_re, s5_c_im, s5_d, s5_w_glu):
    proj = jnp.einsum('bsd,de->bse', h, w_in)
    da_q, da_k, da_v, ml_x, ml_v, ml_o, ml_i, ml_f, s5_u = jnp.split(proj, IN_SPLITS, axis=-1)
    lam_init = 0.8 - 0.6 * math.exp(-0.3 * layer)
    lam = (jnp.exp(jnp.sum(lam_q1 * lam_k1).astype(jnp.float32))
           - jnp.exp(jnp.sum(lam_q2 * lam_k2).astype(jnp.float32)) + lam_init)
    y_da = diff_attention(da_q, da_k, da_v, cos, sin, lam, lam_init, da_norm_g)
    y_ml = mlstm(ml_x, ml_v, ml_o, ml_i, ml_f, ml_conv_w, ml_conv_b, ml_w_q, ml_w_k, ml_gate_b, ml_norm_g)
    y_s5 = s5_layer(s5_u, s5_a_re, s5_a_im, s5_log_dt, s5_b_re, s5_b_im, s5_c_re, s5_c_im, s5_d, s5_w_glu)
    y = jnp.concatenate([y_da, y_ml, y_s5], axis=-1)
    return jnp.einsum('bse,ed->bsd', y, w_out)


def moe_ffn(h, w_router, b_router, w_up, b_up, w_down, b_down):
    B, S, D = h.shape
    t = h.reshape(B * S, D)
    logits = (t @ w_router + b_router).astype(jnp.float32)
    top_v, top_i = lax.top_k(logits, TOP_K)
    probs = jax.nn.softmax(top_v, axis=-1)
    combine = jnp.einsum('tk,tke->te', probs, jax.nn.one_hot(top_i, N_EXPERTS, dtype=jnp.float32))

    def expert(acc, xs):
        w1, b1, w2, b2, gcol = xs
        z = t @ w1 + b1
        x_glu = jnp.minimum(z[:, :D_EXPERT], SWIGLU_LIMIT)
        x_lin = jnp.clip(z[:, D_EXPERT:], -SWIGLU_LIMIT, SWIGLU_LIMIT)
        act = x_glu * jax.nn.sigmoid(SWIGLU_ALPHA * x_glu) * (x_lin + 1.0)
        return acc + gcol[:, None] * (act @ w2 + b2).astype(jnp.float32), None

    acc0 = jnp.zeros((B * S, D), jnp.float32)
    y, _ = lax.scan(expert, acc0, (w_up, b_up, w_down, b_down, combine.T))
    return y.reshape(B, S, D)


def setup_inputs(seed: int = 0) -> dict:
    key = jax.random.key(seed)
    ks = iter(jax.random.split(key, 48))

    def nrm(shape, s):
        return jax.random.normal(next(ks), shape, jnp.float32) * s

    G, Hc, P = S5_GROUPS, S5_GROUP, S5_STATE
    x = nrm((BATCH, SEQ, D_MODEL), 1.0)
    c = nrm((BATCH, D_MODEL), 1.0)
    positions = jnp.broadcast_to(jnp.arange(SEQ, dtype=jnp.int32), (BATCH, SEQ))
    ada_w = nrm((DEPTH, 2, D_MODEL, 3 * D_MODEL), 0.5 * D_MODEL ** -0.5)
    ada_b = nrm((DEPTH, 2, 3 * D_MODEL), 0.02)
    w_in = nrm((DEPTH, D_MODEL, N_IN), D_MODEL ** -0.5)
    lam_q1 = nrm((DEPTH, DA_HEAD_DIM), 0.1)
    lam_k1 = nrm((DEPTH, DA_HEAD_DIM), 0.1)
    lam_q2 = nrm((DEPTH, DA_HEAD_DIM), 0.1)
    lam_k2 = nrm((DEPTH, DA_HEAD_DIM), 0.1)
    da_norm_g = 1.0 + nrm((DEPTH, DA_V_DIM), 0.02)
    ml_conv_w = nrm((DEPTH, ML_CONV, ML_WIDTH), ML_CONV ** -0.5)
    ml_conv_b = nrm((DEPTH, ML_WIDTH), 0.02)
    ml_w_q = nrm((DEPTH, ML_HEADS, ML_HEAD_DIM, ML_HEAD_DIM), ML_HEAD_DIM ** -0.5)
    ml_w_k = nrm((DEPTH, ML_HEADS, ML_HEAD_DIM, ML_HEAD_DIM), ML_HEAD_DIM ** -0.5)
    ml_gate_b = jnp.concatenate(
        [nrm((DEPTH, ML_HEADS), 0.1),
         jnp.linspace(3.0, 6.0, ML_HEADS, dtype=jnp.float32)[None, :] + nrm((DEPTH, ML_HEADS), 0.1)], axis=-1)
    ml_norm_g = 1.0 + nrm((DEPTH, ML_WIDTH), 0.02)
    s5_a_re = -0.5 + nrm((DEPTH, G, P), 0.01)
    s5_a_im = jnp.pi * jnp.arange(P, dtype=jnp.float32)[None, None, :] + nrm((DEPTH, G, P), 0.01)
    s5_log_dt = math.log(0.001) + jax.random.uniform(next(ks), (DEPTH, G), jnp.float32) * (math.log(0.1) - math.log(0.001))
    s5_b_re = nrm((DEPTH, G, P, Hc), (2 * Hc) ** -0.5)
    s5_b_im = nrm((DEPTH, G, P, Hc), (2 * Hc) ** -0.5)
    s5_c_re = nrm((DEPTH, G, Hc, P), P ** -0.5)
    s5_c_im = nrm((DEPTH, G, Hc, P), P ** -0.5)
    s5_d = nrm((DEPTH, S5_WIDTH), 0.5)
    s5_w_glu = nrm((DEPTH, G, Hc, 2 * Hc), Hc ** -0.5)
    w_out = nrm((DEPTH, D_MIX, D_MODEL), DN_BETA * D_MIX ** -0.5)
    ln_g = 1.0 + nrm((DEPTH, 2, D_MODEL), 0.02)
    ln_b = nrm((DEPTH, 2, D_MODEL), 0.02)
    w_router = nrm((DEPTH, D_MODEL, N_EXPERTS), D_MODEL ** -0.5)
    b_router = nrm((DEPTH, N_EXPERTS), 0.01)
    w_up = nrm((DEPTH, N_EXPERTS, D_MODEL, 2 * D_EXPERT), D_MODEL ** -0.5)
    b_up = nrm((DEPTH, N_EXPERTS, 2 * D_EXPERT), 0.02)
    w_down = nrm((DEPTH, N_EXPERTS, D_EXPERT, D_MODEL), DN_BETA * D_EXPERT ** -0.5)
    b_down = nrm((DEPTH, N_EXPERTS, D_MODEL), 0.02)
    return {"x": x, "c": c, "positions": positions, "ada_w": ada_w, "ada_b": ada_b, "w_in": w_in,
            "lam_q1": lam_q1, "lam_k1": lam_k1, "lam_q2": lam_q2, "lam_k2": lam_k2, "da_norm_g": da_norm_g,
            "ml_conv_w": ml_conv_w, "ml_conv_b": ml_conv_b, "ml_w_q": ml_w_q, "ml_w_k": ml_w_k,
            "ml_gate_b": ml_gate_b, "ml_norm_g": ml_norm_g,
            "s5_a_re": s5_a_re, "s5_a_im": s5_a_im, "s5_log_dt": s5_log_dt, "s5_b_re": s5_b_re,
            "s5_b_im": s5_b_im, "s5_c_re": s5_c_re, "s5_c_im": s5_c_im, "s5_d": s5_d, "s5_w_glu": s5_w_glu,
            "w_out": w_out, "ln_g": ln_g, "ln_b": ln_b, "w_router": w_router, "b_router": b_router,
            "w_up": w_up, "b_up": b_up, "w_down": w_down, "b_down": b_down}


def reference(x, c, positions, ada_w, ada_b, w_in, lam_q1, lam_k1, lam_q2, lam_k2, da_norm_g,
              ml_conv_w, ml_conv_b, ml_w_q, ml_w_k, ml_gate_b, ml_norm_g,
              s5_a_re, s5_a_im, s5_log_dt, s5_b_re, s5_b_im, s5_c_re, s5_c_im, s5_d, s5_w_glu,
              w_out, ln_g, ln_b, w_router, b_router, w_up, b_up, w_down, b_down):
    cos, sin = rope_tables(positions)
    c_act = jax.nn.silu(c)
    for l in range(DEPTH):
        mod = jnp.einsum('bd,jde->jbe', c_act, ada_w[l]) + ada_b[l][:, None, :]
        shift, scale, gate = jnp.split(mod[0], 3, axis=-1)
        h = x * (1.0 + scale[:, None, :]) + shift[:, None, :]
        y = hybrid_mixer(h, cos, sin, l, w_in[l], w_out[l], lam_q1[l], lam_k1[l], lam_q2[l], lam_k2[l],
                         da_norm_g[l], ml_conv_w[l], ml_conv_b[l], ml_w_q[l], ml_w_k[l], ml_gate_b[l],
                         ml_norm_g[l], s5_a_re[l], s5_a_im[l], s5_log_dt[l], s5_b_re[l], s5_b_im[l],
                         s5_c_re[l], s5_c_im[l], s5_d[l], s5_w_glu[l])
        x = layer_norm(DN_ALPHA * x + (1.0 + gate[:, None, :]) * y, ln_g[l, 0], ln_b[l, 0])
        shift, scale, gate = jnp.split(mod[1], 3, axis=-1)
        h = x * (1.0 + scale[:, None, :]) + shift[:, None, :]
        y = moe_ffn(h, w_router[l], b_router[l], w_up[l], b_up[l], w_down[l], b_down[l])
        x = layer_norm(DN_ALPHA * x + (1.0 + gate[:, None, :]) * y, ln_g[l, 1], ln_b[l, 1])
    return x
```

```python
import functools
import math

import jax
import jax.numpy as jnp
from jax import lax
from jax.experimental import pallas as pl
from jax.experimental.pallas import tpu as pltpu

F32 = jnp.float32
BF16 = jnp.bfloat16
HIGHEST = lax.Precision.HIGHEST

D_MODEL = 1024
DEPTH = 2
DA_HEADS = 4
DA_HEAD_DIM = 64
DA_V_DIM = 2 * DA_HEAD_DIM
DA_WIDTH = DA_HEADS * DA_V_DIM
DA_QK_WIDTH = DA_HEADS * 2 * DA_HEAD_DIM
ROPE_THETA = 10000.0
ML_HEADS = 4
ML_HEAD_DIM = 64
ML_WIDTH = ML_HEADS * ML_HEAD_DIM
ML_CONV = 4
S5_GROUP = 16
S5_STATE = 64
S5_WIDTH = D_MODEL - DA_WIDTH - ML_WIDTH
S5_GROUPS = S5_WIDTH // S5_GROUP
S5_NSTATE = S5_GROUPS * S5_STATE
N_EXPERTS = 32
TOP_K = 4
D_EXPERT = D_MODEL
SWIGLU_LIMIT = 7.0
SWIGLU_ALPHA = 1.702
DN_ALPHA = (2 * DEPTH) ** 0.25
LN_EPS = 1e-5
NEG = -1e30

OFF_DA_K = DA_QK_WIDTH
OFF_DA_V = 2 * DA_QK_WIDTH
OFF_ML_X = OFF_DA_V + DA_WIDTH
OFF_ML_V = OFF_ML_X + ML_WIDTH
OFF_ML_O = OFF_ML_V + ML_WIDTH
OFF_ML_I = OFF_ML_O + ML_WIDTH
OFF_ML_F = OFF_ML_I + ML_HEADS
OFF_S5_U = OFF_ML_F + ML_HEADS
N_IN = OFF_S5_U + S5_WIDTH

LANES = 128
SUBLANES = 8
VMEM_LIMIT_BYTES = 56 * 1024 * 1024

TM_PROJ = 512
TQ = 256
ML_CHUNK = 256
S5_TC = 128
S5_UNROLL = 8
TB_RANK = 512
TM_MOE = 256
TM_DISP = 256
GATE_PAD = 8


def _cparams(sem, vmem=VMEM_LIMIT_BYTES):
    return pltpu.CompilerParams(dimension_semantics=sem, vmem_limit_bytes=vmem)


def _sigmoid(x):
    return 1.0 / (1.0 + jnp.exp(-x))


def _mod_kernel(c_ref, w_ref, b_ref, o_ref):
    c = c_ref[...]
    ca = (c * _sigmoid(c)).astype(BF16)
    w = w_ref[0].astype(BF16)
    o_ref[0] = jnp.dot(ca, w, preferred_element_type=F32) + b_ref[0]


def _modulation(c, ada_w, ada_b):
    nmod = ada_w.shape[0] * ada_w.shape[1]
    bsz, d = c.shape
    e = ada_w.shape[-1]
    tn = 1024
    w = ada_w.reshape(nmod, d, e)
    b = ada_b.reshape(nmod, 1, e)
    return pl.pallas_call(
        _mod_kernel,
        grid=(nmod, e // tn),
        in_specs=[pl.BlockSpec((bsz, d), lambda n, j: (0, 0)),
                  pl.BlockSpec((1, d, tn), lambda n, j: (n, 0, j)),
                  pl.BlockSpec((1, 1, tn), lambda n, j: (n, 0, j))],
        out_specs=pl.BlockSpec((1, bsz, tn), lambda n, j: (n, 0, j)),
        out_shape=jax.ShapeDtypeStruct((nmod, bsz, e), F32),
        compiler_params=_cparams(("arbitrary", "arbitrary")),
        name="modulation",
    )(c, w, b)


def _rope_kernel(pos_ref, cos_ref, sin_ref):
    pos = pos_ref[0].astype(F32)
    lane = lax.broadcasted_iota(jnp.int32, (1, LANES), 1)
    fidx = (lane % (DA_HEAD_DIM // 2)).astype(F32)
    inv = jnp.exp(fidx * (-2.0 * math.log(ROPE_THETA) / DA_HEAD_DIM))
    ang = pos * inv
    sign = jnp.where((lane % DA_HEAD_DIM) < DA_HEAD_DIM // 2, -1.0, 1.0)
    cos_ref[0] = jnp.cos(ang)
    sin_ref[0] = jnp.sin(ang) * sign


def _rope_tables(positions):
    bsz, s = positions.shape
    ts = 512
    pos3 = positions.reshape(bsz, s, 1)
    return pl.pallas_call(
        _rope_kernel,
        grid=(bsz, s // ts),
        in_specs=[pl.BlockSpec((1, ts, 1), lambda b, i: (b, i, 0))],
        out_specs=[pl.BlockSpec((1, ts, LANES), lambda b, i: (b, i, 0))] * 2,
        out_shape=[jax.ShapeDtypeStruct((bsz, s, LANES), F32)] * 2,
        compiler_params=_cparams(("arbitrary", "arbitrary")),
        name="rope_tables",
    )(pos3)


def _in_proj_kernel(x_ref, shift_ref, scale_ref, cos_ref, sin_ref, wqk_ref, wrest_ref, wgt_ref, wgc_ref,
                    q_ref, k_ref, v_ref, mlx_ref, mlv_ref, mlo_ref, gt_ref, gc_ref, s5u_ref):
    h = (x_ref[0] * (1.0 + scale_ref[0]) + shift_ref[0]).astype(BF16)
    cos = cos_ref[0]
    sin = sin_ref[0]
    lane = lax.broadcasted_iota(jnp.int32, (1, LANES), 1)
    lo_half = (lane % DA_HEAD_DIM) < DA_HEAD_DIM // 2
    half = DA_HEAD_DIM // 2

    def rope(t):
        fwd = pltpu.roll(t, half, 1)
        bwd = pltpu.roll(t, LANES - half, 1)
        partner = jnp.where(lo_half, bwd, fwd)
        return t * cos + partner * sin

    qk = jnp.dot(h, wqk_ref[...], preferred_element_type=F32)
    nslab = DA_QK_WIDTH // LANES
    for c in range(nslab):
        q_ref[0, :, c * LANES:(c + 1) * LANES] = (
            rope(qk[:, c * LANES:(c + 1) * LANES]) * (DA_HEAD_DIM ** -0.5)).astype(BF16)
        k_ref[0, :, c * LANES:(c + 1) * LANES] = rope(
            qk[:, DA_QK_WIDTH + c * LANES:DA_QK_WIDTH + (c + 1) * LANES]).astype(BF16)

    r = jnp.dot(h, wrest_ref[...], preferred_element_type=F32)
    o = 0
    v_ref[0] = r[:, o:o + DA_WIDTH].astype(BF16); o += DA_WIDTH
    mlx_ref[0] = r[:, o:o + ML_WIDTH].astype(BF16); o += ML_WIDTH
    mlv_ref[0] = r[:, o:o + ML_WIDTH].astype(BF16); o += ML_WIDTH
    mlo_ref[0] = r[:, o:o + ML_WIDTH].astype(BF16); o += ML_WIDTH
    s5u_ref[...] = r[:, o:o + S5_WIDTH]
    gt_ref[0] = lax.dot_general(wgt_ref[...], h, (((1,), (1,)), ((), ())), preferred_element_type=F32)
    gc_ref[0] = jnp.dot(h, wgc_ref[...], preferred_element_type=F32)


def _in_proj(x, shift, scale, cos_t, sin_t, w_in_l):
    bsz, s, d = x.shape
    tm = TM_PROJ
    w = w_in_l.astype(BF16)
    wqk = w[:, :OFF_DA_V]
    wrest = jnp.concatenate([w[:, OFF_DA_V:OFF_ML_I], w[:, OFF_S5_U:]], axis=1)
    wg = w[:, OFF_ML_I:OFF_S5_U]
    wgt = wg.T
    wgc = jnp.pad(wg, ((0, 0), (0, LANES - GATE_PAD)))
    nrest = wrest.shape[1]
    shift3 = shift.reshape(bsz, 1, d)
    scale3 = scale.reshape(bsz, 1, d)
    tok = lambda b, i: (b, i, 0)
    per_b = lambda b, i: (b, 0, 0)
    const2 = lambda b, i: (0, 0)
    out_shapes = [
        jax.ShapeDtypeStruct((bsz, s, DA_QK_WIDTH), BF16),
        jax.ShapeDtypeStruct((bsz, s, DA_QK_WIDTH), BF16),
        jax.ShapeDtypeStruct((bsz, s, DA_WIDTH), BF16),
        jax.ShapeDtypeStruct((bsz, s, ML_WIDTH), BF16),
        jax.ShapeDtypeStruct((bsz, s, ML_WIDTH), BF16),
        jax.ShapeDtypeStruct((bsz, s, ML_WIDTH), BF16),
        jax.ShapeDtypeStruct((bsz, GATE_PAD, s), F32),
        jax.ShapeDtypeStruct((bsz, s, LANES), F32),
        jax.ShapeDtypeStruct((s, bsz * S5_WIDTH), F32),
    ]
    out_specs = [
        pl.BlockSpec((1, tm, DA_QK_WIDTH), tok),
        pl.BlockSpec((1, tm, DA_QK_WIDTH), tok),
        pl.BlockSpec((1, tm, DA_WIDTH), tok),
        pl.BlockSpec((1, tm, ML_WIDTH), tok),
        pl.BlockSpec((1, tm, ML_WIDTH), tok),
        pl.BlockSpec((1, tm, ML_WIDTH), tok),
        pl.BlockSpec((1, GATE_PAD, tm), lambda b, i: (b, 0, i)),
        pl.BlockSpec((1, tm, LANES), tok),
        pl.BlockSpec((tm, S5_WIDTH), lambda b, i: (i, b)),
    ]
    return pl.pallas_call(
        _in_proj_kernel,
        grid=(bsz, s // tm),
        in_specs=[pl.BlockSpec((1, tm, d), tok),
                  pl.BlockSpec((1, 1, d), per_b),
                  pl.BlockSpec((1, 1, d), per_b),
                  pl.BlockSpec((1, tm, LANES), tok),
                  pl.BlockSpec((1, tm, LANES), tok),
                  pl.BlockSpec((d, OFF_DA_V), const2),
                  pl.BlockSpec((d, nrest), const2),
                  pl.BlockSpec((GATE_PAD, d), const2),
                  pl.BlockSpec((d, LANES), const2)],
        out_specs=out_specs,
        out_shape=out_shapes,
        compiler_params=_cparams(("arbitrary", "arbitrary")),
        name="in_proj",
    )(x, shift3, scale3, cos_t, sin_t, wqk, wrest, wgt, wgc)


def _diff_attn_kernel(lam_init, lamv_ref, gain_ref, q_ref, k_ref, v_ref, o_ref):
    qi = pl.program_id(2)
    tq = q_ref.shape[1]
    lane = lax.broadcasted_iota(jnp.int32, (1, LANES), 1)
    first = lane < DA_HEAD_DIM
    q = q_ref[0]
    zero = jnp.zeros_like(q)
    qs = (jnp.where(first, q, zero), jnp.where(first, zero, q))
    ones_col = jnp.ones((tq, LANES), BF16)

    def block(j, carry, masked):
        kb = k_ref[0, pl.ds(pl.multiple_of(j * tq, tq), tq), :]
        vb = v_ref[0, pl.ds(pl.multiple_of(j * tq, tq), tq), :]
        vaug = jnp.concatenate([vb, ones_col], axis=1)
        out = []
        for c in range(2):
            m_prev, acc = carry[c]
            s = lax.dot_general(qs[c], kb, (((1,), (1,)), ((), ())), preferred_element_type=F32)
            if masked:
                row = lax.broadcasted_iota(jnp.int32, (tq, tq), 0)
                col = lax.broadcasted_iota(jnp.int32, (tq, tq), 1)
                s = jnp.where(col <= row, s, NEG)
            m_new = jnp.maximum(m_prev, jnp.max(s, axis=1, keepdims=True))
            alpha = jnp.exp(m_prev - m_new)
            p = jnp.exp(s - m_new).astype(BF16)
            acc = alpha * acc + jnp.dot(p, vaug, preferred_element_type=F32)
            out.append((m_new, acc))
        return tuple(out)

    init = tuple((jnp.full((tq, 1), NEG, F32), jnp.zeros((tq, 2 * LANES), F32)) for _ in range(2))
    carry = lax.fori_loop(0, qi, lambda j, c: block(j, c, False), init)
    carry = block(qi, carry, True)

    lamv = lamv_ref[...]
    lam = (jnp.exp(jnp.sum(lamv[0:1] * lamv[1:2], axis=1, keepdims=True))
           - jnp.exp(jnp.sum(lamv[2:3] * lamv[3:4], axis=1, keepdims=True)) + lam_init)
    outs = []
    for c in range(2):
        acc = carry[c][1]
        outs.append(acc[:, :LANES] / acc[:, LANES:LANES + 1])
    o = outs[0] - lam * outs[1]
    ms = jnp.mean(o * o, axis=1, keepdims=True)
    o = o * lax.rsqrt(ms + LN_EPS) * gain_ref[...] * (1.0 - lam_init)
    o_ref[0] = o.astype(o_ref.dtype)


def _diff_attn(q, k, v, lamv, gain, lam_init):
    bsz, s, _ = q.shape
    tq = TQ
    return pl.pallas_call(
        functools.partial(_diff_attn_kernel, lam_init),
        grid=(bsz, DA_HEADS, s // tq),
        in_specs=[pl.BlockSpec((4, DA_HEAD_DIM), lambda b, h, i: (0, 0)),
                  pl.BlockSpec((1, DA_V_DIM), lambda b, h, i: (0, 0)),
                  pl.BlockSpec((1, tq, DA_V_DIM), lambda b, h, i: (b, i, h)),
                  pl.BlockSpec((1, s, DA_V_DIM), lambda b, h, i: (b, 0, h)),
                  pl.BlockSpec((1, s, DA_V_DIM), lambda b, h, i: (b, 0, h))],
        out_specs=pl.BlockSpec((1, tq, DA_V_DIM), lambda b, h, i: (b, i, h)),
        out_shape=jax.ShapeDtypeStruct((bsz, s, DA_WIDTH), BF16),
        compiler_params=_cparams(("arbitrary", "arbitrary", "arbitrary")),
        name="diff_attn",
    )(lamv, gain.reshape(1, DA_V_DIM), q, k, v)


def _log_sigmoid(x):
    return jnp.minimum(x, 0.0) - jnp.log(1.0 + jnp.exp(-jnp.abs(x)))


def _mlstm_kernel(x_ref, v_ref, o_ref, gt_ref, gc_ref, cw_ref, cb_ref, wq_ref, wkt_ref, gbt_ref, gbc_ref,
                  ng_ref, hmean_ref, y_ref, xc_s, c_s, m_s):
    s = x_ref.shape[1]
    L = ML_CHUNK
    H, dh = ML_HEADS, ML_HEAD_DIM
    nc = s // L
    x = x_ref[0].astype(F32)
    cw = cw_ref[...]
    row = lax.broadcasted_iota(jnp.int32, (s, 1), 0)
    xc = x * cw[ML_CONV - 1:ML_CONV]
    for j in range(1, ML_CONV):
        xs = jnp.where(row >= j, pltpu.roll(x, j, 0), 0.0)
        xc = xc + xs * cw[ML_CONV - 1 - j:ML_CONV - j]
    xc = xc + cb_ref[...]
    xc_s[...] = (xc * _sigmoid(xc)).astype(BF16)

    c_s[...] = jnp.zeros_like(c_s)
    m_s[...] = jnp.full(m_s.shape, NEG, F32)

    ri = lax.broadcasted_iota(jnp.int32, (L, L), 0)
    ci = lax.broadcasted_iota(jnp.int32, (L, L), 1)
    causal = ci <= ri
    tril = causal.astype(F32)
    triu = (ri <= ci).astype(F32)
    lane = lax.broadcasted_iota(jnp.int32, (1, dh), 1)
    one_hot0 = jnp.broadcast_to((lane == 0).astype(BF16), (L, dh))

    def chunk(ci_, _):
        t0 = pl.multiple_of(ci_ * L, L)
        xcc = xc_s[pl.ds(t0, L), :]
        qc = jnp.dot(xcc, wq_ref[...], preferred_element_type=F32).astype(BF16)
        ktc = lax.dot_general(wkt_ref[...], xcc, (((1,), (1,)), ((), ())),
                              preferred_element_type=F32)
        g_rows = gt_ref[0, ci_] + gbt_ref[...]
        g_cols = gc_ref[0, pl.ds(t0, L), :] + gbc_ref[...]
        lf_rows = _log_sigmoid(g_rows)
        lf_cols = _log_sigmoid(g_cols)
        b_rows = jnp.dot(lf_rows, triu, preferred_element_type=F32, precision=HIGHEST)
        b_cols = jnp.dot(tril, lf_cols, preferred_element_type=F32, precision=HIGHEST)
        vch = v_ref[0, pl.ds(t0, L), :]
        och = o_ref[0, pl.ds(t0, L), :].astype(F32)
        hs = []
        for h in range(H):
            br = b_rows[H + h:H + h + 1, :]
            ir = g_rows[h:h + 1, :]
            bc = b_cols[:, H + h:H + h + 1]
            m_prev = m_s[h]
            log_d = jnp.where(causal, bc - br + ir, NEG)
            inter = bc + m_prev
            mx = jnp.maximum(inter, jnp.max(log_d, axis=1, keepdims=True))
            dmat = jnp.exp(log_d - mx)
            dec = jnp.exp(inter - mx)
            qh = qc[:, h * dh:(h + 1) * dh]
            kth = ktc[h * dh:(h + 1) * dh, :]
            vaug = jnp.concatenate([vch[:, h * dh:(h + 1) * dh], one_hot0], axis=1)
            sm = (jnp.dot(qh, kth.astype(BF16), preferred_element_type=F32) * dmat).astype(BF16)
            c_prev = c_s[h]
            na = (jnp.dot(sm, vaug, preferred_element_type=F32)
                  + dec * jnp.dot(qh, c_prev.astype(BF16), preferred_element_type=F32))
            den = na[:, dh:dh + 1]
            hs.append(na[:, :dh] / jnp.maximum(jnp.abs(den), jnp.exp(-mx)))
            g_tot = br[:, L - 1:L]
            a_row = g_tot - br + ir
            m_new = jnp.maximum(g_tot + m_prev, jnp.max(a_row, axis=1, keepdims=True))
            decay = jnp.exp(g_tot + m_prev - m_new)
            w_row = jnp.exp(a_row - m_new)
            kw = (kth * w_row).astype(BF16)
            c_s[h] = decay * c_prev + jnp.dot(kw, vaug, preferred_element_type=F32)
            m_s[h] = m_new
        hcat = jnp.concatenate(hs, axis=1)
        ms = jnp.dot(hcat * hcat, hmean_ref[...], preferred_element_type=F32, precision=HIGHEST)
        y = hcat * lax.rsqrt(ms + LN_EPS) * ng_ref[...] * _sigmoid(och)
        y_ref[0, pl.ds(t0, L), :] = y.astype(y_ref.dtype)
        return 0

    lax.fori_loop(0, nc, chunk, 0)


def _mlstm(mlx, mlv, mlo, g_t, g_c, conv_w, conv_b, w_q, w_k, gate_b, norm_g):
    bsz, s, _ = mlx.shape
    H, dh = ML_HEADS, ML_HEAD_DIM
    eye = jnp.eye(H, dtype=F32)
    wq_bd = jnp.einsum('hde,hg->hdge', w_q, eye).reshape(ML_WIDTH, ML_WIDTH).astype(BF16)
    wk_bd = jnp.einsum('hde,hg->hdge', w_k * (dh ** -0.5), eye).reshape(ML_WIDTH, ML_WIDTH)
    wkt_bd = wk_bd.T.astype(BF16)
    gbt = gate_b.reshape(GATE_PAD, 1)
    gbc = jnp.pad(gate_b.reshape(1, GATE_PAD), ((0, 0), (0, LANES - GATE_PAD)))
    hmean = jnp.kron(eye, jnp.full((dh, dh), 1.0 / dh, F32))
    nc = s // ML_CHUNK
    g_t4 = g_t.reshape(bsz, GATE_PAD, nc, ML_CHUNK).transpose(0, 2, 1, 3)
    tok = lambda b: (b, 0, 0)
    c2 = lambda b: (0, 0)
    return pl.pallas_call(
        _mlstm_kernel,
        grid=(bsz,),
        in_specs=[pl.BlockSpec((1, s, ML_WIDTH), tok),
                  pl.BlockSpec((1, s, ML_WIDTH), tok),
                  pl.BlockSpec((1, s, ML_WIDTH), tok),
                  pl.BlockSpec((1, nc, GATE_PAD, ML_CHUNK), lambda b: (b, 0, 0, 0)),
                  pl.BlockSpec((1, s, LANES), tok),
                  pl.BlockSpec((ML_CONV, ML_WIDTH), c2),
                  pl.BlockSpec((1, ML_WIDTH), c2),
                  pl.BlockSpec((ML_WIDTH, ML_WIDTH), c2),
                  pl.BlockSpec((ML_WIDTH, ML_WIDTH), c2),
                  pl.BlockSpec((GATE_PAD, 1), c2),
                  pl.BlockSpec((1, LANES), c2),
                  pl.BlockSpec((1, ML_WIDTH), c2),
                  pl.BlockSpec((ML_WIDTH, ML_WIDTH), c2)],
        out_specs=pl.BlockSpec((1, s, ML_WIDTH), tok),
        out_shape=jax.ShapeDtypeStruct((bsz, s, ML_WIDTH), BF16),
        scratch_shapes=[pltpu.VMEM((s, ML_WIDTH), BF16),
                        pltpu.VMEM((H, dh, LANES), F32),
                        pltpu.VMEM((H, 1, 1), F32)],
        compiler_params=_cparams(("arbitrary",)),
        name="mlstm",
    )(mlx, mlv, mlo, g_t4, g_c, conv_w, conv_b.reshape(1, ML_WIDTH), wq_bd, wkt_bd, gbt, gbc,
      norm_g.reshape(1, ML_WIDTH), hmean)


def _gelu_tanh(x):
    return 0.5 * x * (1.0 + jnp.tanh(math.sqrt(2.0 / math.pi) * (x + 0.044715 * (x * x * x))))


def _s5_kernel(u_ref, are_ref, aim_ref, bcat_ref, ccat_ref, d_ref, wglu_ref, y_ref, xs_s, st_s):
    tc, bsz, w = u_ref.shape
    n = S5_NSTATE

    @pl.when(pl.program_id(0) == 0)
    def _():
        st_s[...] = jnp.zeros_like(st_s)

    u = u_ref[...].reshape(tc * bsz, w)
    xs_s[...] = jnp.dot(u.astype(BF16), bcat_ref[...], preferred_element_type=F32).reshape(tc, bsz, 2 * n)
    a_re = jnp.broadcast_to(are_ref[...], (bsz, n))
    a_im = jnp.broadcast_to(aim_ref[...], (bsz, n))

    def step(t, carry):
        x_re, x_im = carry
        bu = xs_s[t]
        n_re = a_re * x_re - a_im * x_im + bu[:, :n]
        n_im = a_re * x_im + a_im * x_re + bu[:, n:]
        xs_s[t] = jnp.concatenate([n_re, n_im], axis=1)
        return n_re, n_im

    x_re, x_im = lax.fori_loop(0, tc, step, (st_s[0], st_s[1]), unroll=S5_UNROLL)
    st_s[0] = x_re
    st_s[1] = x_im

    xs = xs_s[...].reshape(tc * bsz, 2 * n).astype(BF16)
    y = jnp.dot(xs, ccat_ref[...], preferred_element_type=F32) + d_ref[...] * u
    z = jnp.dot(_gelu_tanh(y).astype(BF16), wglu_ref[...], preferred_element_type=F32)
    out = z[:, :w] * _sigmoid(z[:, w:])
    y_ref[...] = out.reshape(tc, bsz, w).astype(y_ref.dtype)


def _s5_params(a_re, a_im, log_dt, b_re, b_im, c_re, c_im, w_glu):
    G, P, Hc = S5_GROUPS, S5_STATE, S5_GROUP
    dt = jnp.exp(log_dt)[:, None]
    mag = jnp.exp(a_re * dt)
    ab_re = mag * jnp.cos(a_im * dt)
    ab_im = mag * jnp.sin(a_im * dt)
    nr, ni = ab_re - 1.0, ab_im
    den = a_re * a_re + a_im * a_im
    fr = (nr * a_re + ni * a_im) / den
    fi = (ni * a_re - nr * a_im) / den
    bb_re = fr[..., None] * b_re - fi[..., None] * b_im
    bb_im = fr[..., None] * b_im + fi[..., None] * b_re
    eye = jnp.eye(G, dtype=F32)
    bd = lambda t, sub: jnp.einsum(sub, t, eye)
    bre = bd(bb_re, 'gph,gk->ghkp').reshape(G * Hc, G * P)
    bim = bd(bb_im, 'gph,gk->ghkp').reshape(G * Hc, G * P)
    bcat = jnp.concatenate([bre, bim], axis=1).astype(BF16)
    cre = bd(c_re, 'ghp,gk->gpkh').reshape(G * P, G * Hc)
    cim = bd(c_im, 'ghp,gk->gpkh').reshape(G * P, G * Hc)
    ccat = jnp.concatenate([cre, -cim], axis=0).astype(BF16)
    wv = bd(w_glu[:, :, :Hc], 'ghj,gk->ghkj').reshape(G * Hc, G * Hc)
    wg = bd(w_glu[:, :, Hc:], 'ghj,gk->ghkj').reshape(G * Hc, G * Hc)
    wglu = jnp.concatenate([wv, wg], axis=1).astype(BF16)
    return ab_re.reshape(1, G * P), ab_im.reshape(1, G * P), bcat, ccat, wglu


def _s5(u_tm, bsz, params, d_skip):
    s = u_tm.shape[0]
    w = S5_WIDTH
    n = S5_NSTATE
    are, aim, bcat, ccat, wglu = params
    u3 = u_tm.reshape(s, bsz, w)
    tc = S5_TC
    c2 = lambda i: (0, 0)
    y = pl.pallas_call(
        _s5_kernel,
        grid=(s // tc,),
        in_specs=[pl.BlockSpec((tc, bsz, w), lambda i: (i, 0, 0)),
                  pl.BlockSpec((1, n), c2),
                  pl.BlockSpec((1, n), c2),
                  pl.BlockSpec((w, 2 * n), c2),
                  pl.BlockSpec((2 * n, w), c2),
                  pl.BlockSpec((1, w), c2),
                  pl.BlockSpec((w, 2 * w), c2)],
        out_specs=pl.BlockSpec((tc, bsz, w), lambda i: (i, 0, 0)),
        out_shape=jax.ShapeDtypeStruct((s, bsz, w), F32),
        scratch_shapes=[pltpu.VMEM((tc, bsz, 2 * n), F32),
                        pltpu.VMEM((2, bsz, n), F32)],
        compiler_params=_cparams(("arbitrary",)),
        name="s5",
    )(u3, are, aim, bcat, ccat, d_skip.reshape(1, w), wglu)
    return y.reshape(s, bsz * w)


def _layer_norm(z, g, b):
    mu = jnp.mean(z, axis=1, keepdims=True)
    zc = z - mu
    var = jnp.mean(zc * zc, axis=1, keepdims=True)
    return zc * lax.rsqrt(var + LN_EPS) * g + b


def _out_proj_kernel(yda_ref, yml_ref, ys5_ref, x_ref, gate_ref, lng_ref, lnb_ref, shift_ref, scale_ref,
                     wout_ref, wrt_ref, brt_ref, x1_ref, h2_ref, eid_ref, prob_ref):
    y = jnp.dot(yda_ref[0], wout_ref[0:DA_WIDTH, :], preferred_element_type=F32)
    y = y + jnp.dot(yml_ref[0], wout_ref[DA_WIDTH:DA_WIDTH + ML_WIDTH, :], preferred_element_type=F32)
    y = y + jnp.dot(ys5_ref[...].astype(BF16), wout_ref[DA_WIDTH + ML_WIDTH:, :], preferred_element_type=F32)
    x1 = _layer_norm(DN_ALPHA * x_ref[0] + (1.0 + gate_ref[0]) * y, lng_ref[...], lnb_ref[...])
    x1_ref[0] = x1
    h2 = x1 * (1.0 + scale_ref[0]) + shift_ref[0]
    h2_ref[...] = h2
    logits = lax.dot_general(wrt_ref[...], h2, (((1,), (1,)), ((), ())), preferred_element_type=F32,
                             precision=HIGHEST) + brt_ref[...]
    eidx = lax.broadcasted_iota(jnp.int32, logits.shape, 0)
    vals, ids = [], []
    for _ in range(TOP_K):
        mx = jnp.max(logits, axis=0, keepdims=True)
        sel = jnp.min(jnp.where(logits == mx, eidx, N_EXPERTS), axis=0, keepdims=True)
        vals.append(mx)
        ids.append(sel)
        logits = jnp.where(eidx == sel, -jnp.inf, logits)
    ex = [jnp.exp(v - vals[0]) for v in vals]
    tot = ex[0] + ex[1] + ex[2] + ex[3]
    zi = jnp.zeros_like(ids[0])
    zf = jnp.zeros_like(tot)
    eid_ref[...] = jnp.concatenate(ids + [zi] * (SUBLANES - TOP_K), axis=0)
    prob_ref[...] = jnp.concatenate([e / tot for e in ex] + [zf] * (SUBLANES - TOP_K), axis=0)


def _out_proj(y_da, y_ml, y_s5, x, gate, ln_g, ln_b, shift2, scale2, w_out_l, w_router_l, b_router_l):
    bsz, s, d = x.shape
    tm = TM_PROJ
    nt = s // tm
    tok = lambda b, i: (b, i, 0)
    per_b = lambda b, i: (b, 0, 0)
    c2 = lambda b, i: (0, 0)
    r3 = lambda a: a.reshape(bsz, 1, d)
    flat = lambda b, i: (0, b * nt + i)
    return pl.pallas_call(
        _out_proj_kernel,
        grid=(bsz, nt),
        in_specs=[pl.BlockSpec((1, tm, DA_WIDTH), tok),
                  pl.BlockSpec((1, tm, ML_WIDTH), tok),
                  pl.BlockSpec((tm, S5_WIDTH), lambda b, i: (i, b)),
                  pl.BlockSpec((1, tm, d), tok),
                  pl.BlockSpec((1, 1, d), per_b),
                  pl.BlockSpec((1, d), c2),
                  pl.BlockSpec((1, d), c2),
                  pl.BlockSpec((1, 1, d), per_b),
                  pl.BlockSpec((1, 1, d), per_b),
                  pl.BlockSpec((d, d), c2),
                  pl.BlockSpec((N_EXPERTS, d), c2),
                  pl.BlockSpec((N_EXPERTS, 1), c2)],
        out_specs=[pl.BlockSpec((1, tm, d), tok),
                   pl.BlockSpec((tm, d), lambda b, i: (b * nt + i, 0)),
                   pl.BlockSpec((SUBLANES, tm), flat),
                   pl.BlockSpec((SUBLANES, tm), flat)],
        out_shape=[jax.ShapeDtypeStruct((bsz, s, d), F32),
                   jax.ShapeDtypeStruct((bsz * s, d), F32),
                   jax.ShapeDtypeStruct((SUBLANES, bsz * s), jnp.int32),
                   jax.ShapeDtypeStruct((SUBLANES, bsz * s), F32)],
        compiler_params=_cparams(("arbitrary", "arbitrary")),
        name="out_proj",
    )(y_da, y_ml, y_s5, x, r3(gate), ln_g.reshape(1, d), ln_b.reshape(1, d), r3(shift2), r3(scale2),
      w_out_l.astype(BF16), w_router_l.T, b_router_l.reshape(N_EXPERTS, 1))


def _rank_kernel(eid_ref, rank_ref, cnt_ref, carry_s):
    tb = eid_ref.shape[1]

    @pl.when(pl.program_id(0) == 0)
    def _():
        carry_s[...] = jnp.zeros_like(carry_s)

    eid = eid_ref[...]
    eidx = lax.broadcasted_iota(jnp.int32, (N_EXPERTS, tb), 0)
    hot = [eidx == eid[k:k + 1, :] for k in range(TOP_K)]
    member = jnp.zeros((N_EXPERTS, tb), F32)
    for k in range(TOP_K):
        member = member + hot[k].astype(F32)
    ri = lax.broadcasted_iota(jnp.int32, (tb, tb), 0)
    ci = lax.broadcasted_iota(jnp.int32, (tb, tb), 1)
    triu = (ri <= ci).astype(BF16)
    incl = jnp.dot(member.astype(BF16), triu, preferred_element_type=F32)
    carry = carry_s[...]
    excl = incl - member + carry[:, 0:1]
    ranks = [jnp.sum(jnp.where(hot[k], excl, 0.0), axis=0, keepdims=True) for k in range(TOP_K)]
    zr = jnp.zeros_like(ranks[0])
    rank_ref[...] = jnp.concatenate(ranks + [zr] * (SUBLANES - TOP_K), axis=0).astype(jnp.int32)
    new_carry = carry + jnp.sum(member, axis=1, keepdims=True)
    carry_s[...] = new_carry
    cnt_ref[...] = new_carry


def _route_rank(eid):
    t = eid.shape[1]
    tb = TB_RANK
    return pl.pallas_call(
        _rank_kernel,
        grid=(t // tb,),
        in_specs=[pl.BlockSpec((SUBLANES, tb), lambda i: (0, i))],
        out_specs=[pl.BlockSpec((SUBLANES, tb), lambda i: (0, i)),
                   pl.BlockSpec((N_EXPERTS, LANES), lambda i: (0, 0))],
        out_shape=[jax.ShapeDtypeStruct((SUBLANES, t), jnp.int32),
                   jax.ShapeDtypeStruct((N_EXPERTS, LANES), F32)],
        scratch_shapes=[pltpu.VMEM((N_EXPERTS, LANES), F32)],
        compiler_params=_cparams(("arbitrary",)),
        name="route_rank",
    )(eid)


def _dispatch_kernel(pos_ref, h_ref, init_ref, xs_ref, sem):
    del init_ref
    tm = h_ref.shape[0]

    def issue(i, _):
        for k in range(TOP_K):
            dst = pos_ref[0, 0, i * TOP_K + k]
            pltpu.make_async_copy(h_ref.at[pl.ds(i, 1)], xs_ref.at[pl.ds(dst, 1)], sem).start()
        return 0

    lax.fori_loop(0, tm, issue, 0)
    for _ in range(TOP_K):
        pltpu.make_async_copy(h_ref, xs_ref.at[pl.ds(0, tm)], sem).wait()


def _dispatch(h2, pos, n_rows):
    t, d = h2.shape
    tm = TM_DISP
    pos3 = pos.reshape(t // tm, 1, tm * TOP_K)
    init = jnp.zeros((n_rows, d), h2.dtype)
    return pl.pallas_call(
        _dispatch_kernel,
        grid=(t // tm,),
        in_specs=[pl.BlockSpec((1, 1, tm * TOP_K), lambda i: (i, 0, 0), memory_space=pltpu.SMEM),
                  pl.BlockSpec((tm, d), lambda i: (i, 0)),
                  pl.BlockSpec(memory_space=pl.ANY)],
        out_specs=pl.BlockSpec(memory_space=pl.ANY),
        out_shape=jax.ShapeDtypeStruct((n_rows, d), h2.dtype),
        scratch_shapes=[pltpu.SemaphoreType.DMA(())],
        input_output_aliases={2: 0},
        compiler_params=_cparams(("arbitrary",)),
        name="dispatch",
    )(pos3, h2, init)


def _expert_kernel(te_ref, nt_ref, x_ref, wup_ref, bup_ref, wdn_ref, bdn_ref, o_ref, wup_s, wdn_s):
    j = pl.program_id(0)
    n_act = nt_ref[0]
    prev = te_ref[jnp.maximum(j - 1, 0)]
    fresh = jnp.logical_or(j == 0, te_ref[j] != prev)

    @pl.when(jnp.logical_and(j < n_act, fresh))
    def _():
        wup_s[...] = wup_ref[0, 0].astype(BF16)
        wdn_s[...] = wdn_ref[0, 0].astype(BF16)

    @pl.when(j < n_act)
    def _():
        x = x_ref[...].astype(BF16)
        z = jnp.dot(x, wup_s[...], preferred_element_type=F32) + bup_ref[0, 0]
        glu = jnp.minimum(z[:, :D_EXPERT], SWIGLU_LIMIT)
        lin = jnp.clip(z[:, D_EXPERT:], -SWIGLU_LIMIT, SWIGLU_LIMIT)
        act = (glu * _sigmoid(SWIGLU_ALPHA * glu) * (lin + 1.0)).astype(BF16)
        o_ref[...] = jnp.dot(act, wdn_s[...], preferred_element_type=F32) + bdn_ref[0, 0]

    @pl.when(j >= n_act)
    def _():
        o_ref[...] = jnp.zeros_like(o_ref)


def _expert_mlp(xs, tile_expert, n_tiles, layer, w_up, b_up, w_down, b_down):
    n_rows, d = xs.shape
    tm = TM_MOE
    nt_max = n_rows // tm
    f = w_up.shape[-1]
    b_up4 = b_up.reshape(DEPTH, N_EXPERTS, 1, f)
    b_dn4 = b_down.reshape(DEPTH, N_EXPERTS, 1, d)
    row = lambda j, te, nt: (jnp.minimum(j, nt[0] - 1), 0)
    wsel = lambda j, te, nt: (layer, te[j], 0, 0)
    grid_spec = pltpu.PrefetchScalarGridSpec(
        num_scalar_prefetch=2,
        grid=(nt_max,),
        in_specs=[pl.BlockSpec((tm, d), row),
                  pl.BlockSpec((1, 1, d, f), wsel),
                  pl.BlockSpec((1, 1, 1, f), wsel),
                  pl.BlockSpec((1, 1, f // 2, d), wsel),
                  pl.BlockSpec((1, 1, 1, d), wsel)],
        out_specs=pl.BlockSpec((tm, d), lambda j, te, nt: (j, 0)),
        scratch_shapes=[pltpu.VMEM((d, f), BF16), pltpu.VMEM((f // 2, d), BF16)],
    )
    return pl.pallas_call(
        _expert_kernel,
        grid_spec=grid_spec,
        out_shape=jax.ShapeDtypeStruct((n_rows, d), F32),
        compiler_params=_cparams(("arbitrary",)),
        name="expert_mlp",
    )(tile_expert, n_tiles, xs, w_up, b_up4, w_down, b_dn4)


def _combine_kernel(pos_ref, ys_ref, prob_ref, x_ref, gate_ref, lng_ref, lnb_ref, o_ref, rows_s, sem):
    tm = x_ref.shape[1]

    def issue(i, _):
        for k in range(TOP_K):
            src = pos_ref[0, 0, i * TOP_K + k]
            pltpu.make_async_copy(ys_ref.at[pl.ds(src, 1)], rows_s.at[k, pl.ds(i, 1)], sem).start()
        return 0

    lax.fori_loop(0, tm, issue, 0)
    for k in range(TOP_K):
        pltpu.make_async_copy(ys_ref.at[pl.ds(0, tm)], rows_s.at[k], sem).wait()
    p = prob_ref[...]
    y = p[:, 0:1] * rows_s[0]
    for k in range(1, TOP_K):
        y = y + p[:, k:k + 1] * rows_s[k]
    o_ref[0] = _layer_norm(DN_ALPHA * x_ref[0] + (1.0 + gate_ref[0]) * y, lng_ref[...], lnb_ref[...])


def _combine(ys, pos, prob_c, x1, gate, ln_g, ln_b):
    bsz, s, d = x1.shape
    tm = TM_DISP
    nt = s // tm
    pos3 = pos.reshape(bsz * nt, 1, tm * TOP_K)
    return pl.pallas_call(
        _combine_kernel,
        grid=(bsz, nt),
        in_specs=[pl.BlockSpec((1, 1, tm * TOP_K), lambda b, i: (b * nt + i, 0, 0), memory_space=pltpu.SMEM),
                  pl.BlockSpec(memory_space=pl.ANY),
                  pl.BlockSpec((tm, LANES), lambda b, i: (b * nt + i, 0)),
                  pl.BlockSpec((1, tm, d), lambda b, i: (b, i, 0)),
                  pl.BlockSpec((1, 1, d), lambda b, i: (b, 0, 0)),
                  pl.BlockSpec((1, d), lambda b, i: (0, 0)),
                  pl.BlockSpec((1, d), lambda b, i: (0, 0))],
        out_specs=pl.BlockSpec((1, tm, d), lambda b, i: (b, i, 0)),
        out_shape=jax.ShapeDtypeStruct((bsz, s, d), F32),
        scratch_shapes=[pltpu.VMEM((TOP_K, tm, d), F32), pltpu.SemaphoreType.DMA(())],
        compiler_params=_cparams(("arbitrary", "arbitrary")),
        name="combine",
    )(pos3, ys, prob_c, x1, gate.reshape(bsz, 1, d), ln_g.reshape(1, d), ln_b.reshape(1, d))


def _routing_tables(eid, rank, counts, n_tiles_max):
    tm = TM_MOE
    cnt = counts[:, 0].astype(jnp.int32)
    padded = ((cnt + tm - 1) // tm) * tm
    ends = jnp.cumsum(padded)
    starts = ends - padded
    pos = jnp.take(starts, eid[:TOP_K], axis=0) + rank[:TOP_K]
    pos = pos.T.reshape(-1)
    n_tiles = (ends[-1] // tm).astype(jnp.int32)
    tile_start = jnp.arange(n_tiles_max, dtype=jnp.int32) * tm
    te = jnp.searchsorted(ends, tile_start, side='right').astype(jnp.int32)
    last = jnp.take(te, jnp.maximum(n_tiles - 1, 0))
    te = jnp.where(jnp.arange(n_tiles_max) < n_tiles, te, last)
    return pos, te, n_tiles.reshape(1)


def kernel(x, c, positions, ada_w, ada_b, w_in, lam_q1, lam_k1, lam_q2, lam_k2, da_norm_g, ml_conv_w, ml_conv_b,
           ml_w_q, ml_w_k, ml_gate_b, ml_norm_g, s5_a_re, s5_a_im, s5_log_dt, s5_b_re, s5_b_im, s5_c_re, s5_c_im,
           s5_d, s5_w_glu, w_out, ln_g, ln_b, w_router, b_router, w_up, b_up, w_down, b_down):
    bsz, s, d = x.shape
    t = bsz * s
    n_tiles_max = (t * TOP_K) // TM_MOE + N_EXPERTS
    n_rows = n_tiles_max * TM_MOE
    mod = _modulation(c, ada_w, ada_b)
    cos_t, sin_t = _rope_tables(positions)
    for l in range(DEPTH):
        shift, scale, gate = jnp.split(mod[2 * l], 3, axis=-1)
        q, k, v, mlx, mlv, mlo, g_t, g_c, s5u = _in_proj(x, shift, scale, cos_t, sin_t, w_in[l])
        lam_init = 0.8 - 0.6 * math.exp(-0.3 * l)
        lamv = jnp.stack([lam_q1[l], lam_k1[l], lam_q2[l], lam_k2[l]])
        y_da = _diff_attn(q, k, v, lamv, da_norm_g[l], lam_init)
        y_ml = _mlstm(mlx, mlv, mlo, g_t, g_c, ml_conv_w[l], ml_conv_b[l], ml_w_q[l], ml_w_k[l],
                      ml_gate_b[l], ml_norm_g[l])
        s5p = _s5_params(s5_a_re[l], s5_a_im[l], s5_log_dt[l], s5_b_re[l], s5_b_im[l], s5_c_re[l], s5_c_im[l],
                         s5_w_glu[l])
        y_s5 = _s5(s5u, bsz, s5p, s5_d[l])
        shift2, scale2, gate2 = jnp.split(mod[2 * l + 1], 3, axis=-1)
        x1, h2, eid, prob = _out_proj(y_da, y_ml, y_s5, x, gate, ln_g[l, 0], ln_b[l, 0], shift2, scale2,
                                      w_out[l], w_router[l], b_router[l])
        rank, counts = _route_rank(eid)
        pos, tile_expert, n_tiles = _routing_tables(eid, rank, counts, n_tiles_max)
        xs = _dispatch(h2, pos, n_rows)
        ys = _expert_mlp(xs, tile_expert, n_tiles, l, w_up, b_up, w_down, b_down)
        prob_c = jnp.pad(prob[:TOP_K].T, ((0, 0), (0, LANES - TOP_K)))
        x = _combine(ys, pos, prob_c, x1, gate2, ln_g[l, 1], ln_b[l, 1])
    return x
```

```python
import functools
import math

import jax
import jax.numpy as jnp
from jax import lax
from jax.experimental import pallas as pl
from jax.experimental.pallas import tpu as pltpu

F32 = jnp.float32
BF16 = jnp.bfloat16
HIGHEST = lax.Precision.HIGHEST

D_MODEL = 1024
DEPTH = 2
DA_HEADS = 4
DA_HEAD_DIM = 64
DA_V_DIM = 2 * DA_HEAD_DIM
DA_WIDTH = DA_HEADS * DA_V_DIM
DA_QK_WIDTH = DA_HEADS * 2 * DA_HEAD_DIM
ROPE_THETA = 10000.0
ML_HEADS = 4
ML_HEAD_DIM = 64
ML_WIDTH = ML_HEADS * ML_HEAD_DIM
ML_CONV = 4
S5_GROUP = 16
S5_STATE = 64
S5_WIDTH = D_MODEL - DA_WIDTH - ML_WIDTH
S5_GROUPS = S5_WIDTH // S5_GROUP
S5_NSTATE = S5_GROUPS * S5_STATE
N_EXPERTS = 32
TOP_K = 4
D_EXPERT = D_MODEL
SWIGLU_LIMIT = 7.0
SWIGLU_ALPHA = 1.702
DN_ALPHA = (2 * DEPTH) ** 0.25
LN_EPS = 1e-5
NEG = -1e30

OFF_DA_K = DA_QK_WIDTH
OFF_DA_V = 2 * DA_QK_WIDTH
OFF_ML_X = OFF_DA_V + DA_WIDTH
OFF_ML_V = OFF_ML_X + ML_WIDTH
OFF_ML_O = OFF_ML_V + ML_WIDTH
OFF_ML_I = OFF_ML_O + ML_WIDTH
OFF_ML_F = OFF_ML_I + ML_HEADS
OFF_S5_U = OFF_ML_F + ML_HEADS
N_IN = OFF_S5_U + S5_WIDTH

LANES = 128
SUBLANES = 8
VMEM_LIMIT_BYTES = 56 * 1024 * 1024

TM_PROJ = 512
TQ = 512
ML_CHUNK = 256
S5_TC = 128
S5_UNROLL = 8
TB_RANK = 512
TM_MOE = 256
TM_DISP = 256
GATE_PAD = 8
VT_ROWS = DA_V_DIM + 16
Q_PRESCALE = DA_HEAD_DIM ** -0.5 * math.log2(math.e)


def _cparams(sem, vmem=VMEM_LIMIT_BYTES):
    return pltpu.CompilerParams(dimension_semantics=sem, vmem_limit_bytes=vmem)


def _sigmoid(x):
    return 1.0 / (1.0 + jnp.exp(-x))


def _mod_kernel(c_ref, w_ref, b_ref, o_ref):
    c = c_ref[...]
    ca = (c * _sigmoid(c)).astype(BF16)
    w = w_ref[0].astype(BF16)
    o_ref[0] = jnp.dot(ca, w, preferred_element_type=F32) + b_ref[0]


def _modulation(c, ada_w, ada_b):
    nmod = ada_w.shape[0] * ada_w.shape[1]
    bsz, d = c.shape
    e = ada_w.shape[-1]
    tn = 1024
    w = ada_w.reshape(nmod, d, e)
    b = ada_b.reshape(nmod, 1, e)
    return pl.pallas_call(
        _mod_kernel,
        grid=(nmod, e // tn),
        in_specs=[pl.BlockSpec((bsz, d), lambda n, j: (0, 0)),
                  pl.BlockSpec((1, d, tn), lambda n, j: (n, 0, j)),
                  pl.BlockSpec((1, 1, tn), lambda n, j: (n, 0, j))],
        out_specs=pl.BlockSpec((1, bsz, tn), lambda n, j: (n, 0, j)),
        out_shape=jax.ShapeDtypeStruct((nmod, bsz, e), F32),
        compiler_params=_cparams(("arbitrary", "arbitrary")),
        name="modulation",
    )(c, w, b)


def _rope_kernel(pos_ref, cos_ref, sin_ref):
    pos = pos_ref[0].astype(F32)
    lane = lax.broadcasted_iota(jnp.int32, (1, LANES), 1)
    fidx = (lane % (DA_HEAD_DIM // 2)).astype(F32)
    inv = jnp.exp(fidx * (-2.0 * math.log(ROPE_THETA) / DA_HEAD_DIM))
    ang = pos * inv
    sign = jnp.where((lane % DA_HEAD_DIM) < DA_HEAD_DIM // 2, -1.0, 1.0)
    cos_ref[0] = jnp.cos(ang)
    sin_ref[0] = jnp.sin(ang) * sign


def _rope_tables(positions):
    bsz, s = positions.shape
    ts = 512
    pos3 = positions.reshape(bsz, s, 1)
    return pl.pallas_call(
        _rope_kernel,
        grid=(bsz, s // ts),
        in_specs=[pl.BlockSpec((1, ts, 1), lambda b, i: (b, i, 0))],
        out_specs=[pl.BlockSpec((1, ts, LANES), lambda b, i: (b, i, 0))] * 2,
        out_shape=[jax.ShapeDtypeStruct((bsz, s, LANES), F32)] * 2,
        compiler_params=_cparams(("arbitrary", "arbitrary")),
        name="rope_tables",
    )(pos3)


def _in_proj_kernel(x_ref, shift_ref, scale_ref, cos_ref, sin_ref, wqk_ref, wvt_ref, wrest_ref, wgt_ref, wgc_ref,
                    q_ref, k_ref, vt_ref, mlx_ref, mlv_ref, mlo_ref, gt_ref, gc_ref, s5u_ref):
    h = (x_ref[0] * (1.0 + scale_ref[0]) + shift_ref[0]).astype(BF16)
    cos = cos_ref[0]
    sin = sin_ref[0]
    lane = lax.broadcasted_iota(jnp.int32, (1, LANES), 1)
    lo_half = (lane % DA_HEAD_DIM) < DA_HEAD_DIM // 2
    half = DA_HEAD_DIM // 2

    def rope(t):
        fwd = pltpu.roll(t, half, 1)
        bwd = pltpu.roll(t, LANES - half, 1)
        partner = jnp.where(lo_half, bwd, fwd)
        return t * cos + partner * sin

    qk = jnp.dot(h, wqk_ref[...], preferred_element_type=F32)
    nslab = DA_QK_WIDTH // LANES
    for c in range(nslab):
        q_ref[0, :, c * LANES:(c + 1) * LANES] = (
            rope(qk[:, c * LANES:(c + 1) * LANES]) * Q_PRESCALE).astype(BF16)
        k_ref[0, :, c * LANES:(c + 1) * LANES] = rope(
            qk[:, DA_QK_WIDTH + c * LANES:DA_QK_WIDTH + (c + 1) * LANES]).astype(BF16)

    vt = lax.dot_general(wvt_ref[...], h, (((1,), (1,)), ((), ())), preferred_element_type=F32)
    tm = h.shape[0]
    for hh in range(DA_HEADS):
        for jj in range(tm // TQ):
            vt_ref[0, hh, jj, 0:DA_V_DIM, :] = vt[hh * DA_V_DIM:(hh + 1) * DA_V_DIM,
                                                  jj * TQ:(jj + 1) * TQ].astype(BF16)
            vt_ref[0, hh, jj, DA_V_DIM:VT_ROWS, :] = jnp.ones((VT_ROWS - DA_V_DIM, TQ), BF16)

    r = jnp.dot(h, wrest_ref[...], preferred_element_type=F32)
    o = 0
    mlx_ref[0] = r[:, o:o + ML_WIDTH].astype(BF16); o += ML_WIDTH
    mlv_ref[0] = r[:, o:o + ML_WIDTH].astype(BF16); o += ML_WIDTH
    mlo_ref[0] = r[:, o:o + ML_WIDTH].astype(BF16); o += ML_WIDTH
    s5u_ref[...] = r[:, o:o + S5_WIDTH]
    gt_ref[0] = lax.dot_general(wgt_ref[...], h, (((1,), (1,)), ((), ())), preferred_element_type=F32)
    gc_ref[0] = jnp.dot(h, wgc_ref[...], preferred_element_type=F32)


def _in_proj(x, shift, scale, cos_t, sin_t, w_in_l):
    bsz, s, d = x.shape
    tm = TM_PROJ
    w = w_in_l.astype(BF16)
    wqk = w[:, :OFF_DA_V]
    wvt = w[:, OFF_DA_V:OFF_ML_X].T
    wrest = jnp.concatenate([w[:, OFF_ML_X:OFF_ML_I], w[:, OFF_S5_U:]], axis=1)
    wg = w[:, OFF_ML_I:OFF_S5_U]
    wgt = wg.T
    wgc = jnp.pad(wg, ((0, 0), (0, LANES - GATE_PAD)))
    nrest = wrest.shape[1]
    shift3 = shift.reshape(bsz, 1, d)
    scale3 = scale.reshape(bsz, 1, d)
    tok = lambda b, i: (b, i, 0)
    per_b = lambda b, i: (b, 0, 0)
    const2 = lambda b, i: (0, 0)
    out_shapes = [
        jax.ShapeDtypeStruct((bsz, s, DA_QK_WIDTH), BF16),
        jax.ShapeDtypeStruct((bsz, s, DA_QK_WIDTH), BF16),
        jax.ShapeDtypeStruct((bsz, DA_HEADS, s // TQ, VT_ROWS, TQ), BF16),
        jax.ShapeDtypeStruct((bsz, s, ML_WIDTH), BF16),
        jax.ShapeDtypeStruct((bsz, s, ML_WIDTH), BF16),
        jax.ShapeDtypeStruct((bsz, s, ML_WIDTH), BF16),
        jax.ShapeDtypeStruct((bsz, GATE_PAD, s), F32),
        jax.ShapeDtypeStruct((bsz, s, LANES), F32),
        jax.ShapeDtypeStruct((s, bsz * S5_WIDTH), F32),
    ]
    out_specs = [
        pl.BlockSpec((1, tm, DA_QK_WIDTH), tok),
        pl.BlockSpec((1, tm, DA_QK_WIDTH), tok),
        pl.BlockSpec((1, DA_HEADS, tm // TQ, VT_ROWS, TQ), lambda b, i: (b, 0, i, 0, 0)),
        pl.BlockSpec((1, tm, ML_WIDTH), tok),
        pl.BlockSpec((1, tm, ML_WIDTH), tok),
        pl.BlockSpec((1, tm, ML_WIDTH), tok),
        pl.BlockSpec((1, GATE_PAD, tm), lambda b, i: (b, 0, i)),
        pl.BlockSpec((1, tm, LANES), tok),
        pl.BlockSpec((tm, S5_WIDTH), lambda b, i: (i, b)),
    ]
    return pl.pallas_call(
        _in_proj_kernel,
        grid=(bsz, s // tm),
        in_specs=[pl.BlockSpec((1, tm, d), tok),
                  pl.BlockSpec((1, 1, d), per_b),
                  pl.BlockSpec((1, 1, d), per_b),
                  pl.BlockSpec((1, tm, LANES), tok),
                  pl.BlockSpec((1, tm, LANES), tok),
                  pl.BlockSpec((d, OFF_DA_V), const2),
                  pl.BlockSpec((DA_WIDTH, d), const2),
                  pl.BlockSpec((d, nrest), const2),
                  pl.BlockSpec((GATE_PAD, d), const2),
                  pl.BlockSpec((d, LANES), const2)],
        out_specs=out_specs,
        out_shape=out_shapes,
        compiler_params=_cparams(("arbitrary", "arbitrary")),
        name="in_proj",
    )(x, shift3, scale3, cos_t, sin_t, wqk, wvt, wrest, wgt, wgc)


def _diff_attn_kernel(lam_init, lamv_ref, gain_ref, q_ref, k_ref, vt_ref, o_ref, acc_s, m_s):
    qi = pl.program_id(2)
    tq = q_ref.shape[1]
    lane = lax.broadcasted_iota(jnp.int32, (1, LANES), 1)
    first = lane < DA_HEAD_DIM
    q = q_ref[0]
    zero = jnp.zeros_like(q)
    qm = (jnp.where(first, q, zero), jnp.where(first, zero, q))
    acc_s[...] = jnp.zeros_like(acc_s)
    m_s[...] = jnp.full(m_s.shape, NEG, F32)

    def step(j, masked):
        kb = k_ref[0, pl.ds(pl.multiple_of(j * tq, tq), tq), :]
        vtb = vt_ref[0, 0, j]
        for c in range(2):
            st = lax.dot_general(kb, qm[c], (((1,), (1,)), ((), ())), preferred_element_type=F32)
            if masked:
                key_i = lax.broadcasted_iota(jnp.int32, (tq, tq), 0)
                qry_i = lax.broadcasted_iota(jnp.int32, (tq, tq), 1)
                st = jnp.where(key_i <= qry_i, st, NEG)
            m_prev = m_s[c]
            m_new = jnp.maximum(m_prev, jnp.max(st, axis=0, keepdims=True))
            alpha = jnp.exp2(m_prev - m_new)
            p = jnp.exp2(st - m_new).astype(BF16)
            acc_s[c] = alpha * acc_s[c] + jnp.dot(vtb, p, preferred_element_type=F32)
            m_s[c] = m_new

    def body(j, carry):
        step(j, False)
        return carry

    lax.fori_loop(0, qi, body, 0)
    step(qi, True)
    outs = []
    for c in range(2):
        acc = acc_s[c]
        outs.append(acc[:DA_V_DIM] / acc[DA_V_DIM:DA_V_DIM + 1])

    lamv = lamv_ref[...]
    lam = (jnp.exp(jnp.sum(lamv[0:1] * lamv[1:2], axis=1, keepdims=True))
           - jnp.exp(jnp.sum(lamv[2:3] * lamv[3:4], axis=1, keepdims=True)) + lam_init)
    ot = outs[0] - lam * outs[1]
    ms = jnp.mean(ot * ot, axis=0, keepdims=True)
    ot = ot * (lax.rsqrt(ms + LN_EPS) * (1.0 - lam_init))
    o_ref[0] = (ot.T * gain_ref[...]).astype(o_ref.dtype)


def _diff_attn(q, k, vt, lamv, gain, lam_init):
    bsz, s, _ = q.shape
    tq = TQ
    nq = s // tq
    return pl.pallas_call(
        functools.partial(_diff_attn_kernel, lam_init),
        grid=(bsz, DA_HEADS, nq),
        in_specs=[pl.BlockSpec((4, DA_HEAD_DIM), lambda b, h, i: (0, 0)),
                  pl.BlockSpec((1, DA_V_DIM), lambda b, h, i: (0, 0)),
                  pl.BlockSpec((1, tq, DA_V_DIM), lambda b, h, i: (b, i, h)),
                  pl.BlockSpec((1, s, DA_V_DIM), lambda b, h, i: (b, 0, h)),
                  pl.BlockSpec((1, 1, nq, VT_ROWS, tq), lambda b, h, i: (b, h, 0, 0, 0))],
        out_specs=pl.BlockSpec((1, tq, DA_V_DIM), lambda b, h, i: (b, i, h)),
        out_shape=jax.ShapeDtypeStruct((bsz, s, DA_WIDTH), BF16),
        scratch_shapes=[pltpu.VMEM((2, VT_ROWS, tq), F32), pltpu.VMEM((2, 1, tq), F32)],
        compiler_params=_cparams(("arbitrary", "arbitrary", "arbitrary")),
        name="diff_attn",
    )(lamv, gain.reshape(1, DA_V_DIM), q, k, vt)


def _log_sigmoid(x):
    return jnp.minimum(x, 0.0) - jnp.log(1.0 + jnp.exp(-jnp.abs(x)))


def _mlstm_kernel(x_ref, v_ref, o_ref, gt_ref, gc_ref, cw_ref, cb_ref, wq_ref, wkt_ref, gbt_ref, gbc_ref,
                  ng_ref, hmean_ref, y_ref, xc_s, c_s, m_s):
    s = x_ref.shape[1]
    L = ML_CHUNK
    H, dh = ML_HEADS, ML_HEAD_DIM
    nc = s // L
    x = x_ref[0].astype(F32)
    cw = cw_ref[...]
    row = lax.broadcasted_iota(jnp.int32, (s, 1), 0)
    xc = x * cw[ML_CONV - 1:ML_CONV]
    for j in range(1, ML_CONV):
        xs = jnp.where(row >= j, pltpu.roll(x, j, 0), 0.0)
        xc = xc + xs * cw[ML_CONV - 1 - j:ML_CONV - j]
    xc = xc + cb_ref[...]
    xc_s[...] = (xc * _sigmoid(xc)).astype(BF16)

    c_s[...] = jnp.zeros_like(c_s)
    m_s[...] = jnp.full(m_s.shape, NEG, F32)

    ri = lax.broadcasted_iota(jnp.int32, (L, L), 0)
    ci = lax.broadcasted_iota(jnp.int32, (L, L), 1)
    causal = ci <= ri
    tril = causal.astype(F32)
    triu = (ri <= ci).astype(F32)
    lane = lax.broadcasted_iota(jnp.int32, (1, dh), 1)
    one_hot0 = jnp.broadcast_to((lane == 0).astype(BF16), (L, dh))

    def chunk(ci_, _):
        t0 = pl.multiple_of(ci_ * L, L)
        xcc = xc_s[pl.ds(t0, L), :]
        qc = jnp.dot(xcc, wq_ref[...], preferred_element_type=F32).astype(BF16)
        ktc = lax.dot_general(wkt_ref[...], xcc, (((1,), (1,)), ((), ())),
                              preferred_element_type=F32)
        g_rows = gt_ref[0, ci_] + gbt_ref[...]
        g_cols = gc_ref[0, pl.ds(t0, L), :] + gbc_ref[...]
        lf_rows = _log_sigmoid(g_rows)
        lf_cols = _log_sigmoid(g_cols)
        b_rows = jnp.dot(lf_rows, triu, preferred_element_type=F32, precision=HIGHEST)
        b_cols = jnp.dot(tril, lf_cols, preferred_element_type=F32, precision=HIGHEST)
        vch = v_ref[0, pl.ds(t0, L), :]
        och = o_ref[0, pl.ds(t0, L), :].astype(F32)
        hs = []
        for h in range(H):
            br = b_rows[H + h:H + h + 1, :]
            ir = g_rows[h:h + 1, :]
            bc = b_cols[:, H + h:H + h + 1]
            m_prev = m_s[h]
            log_d = jnp.where(causal, bc - br + ir, NEG)
            inter = bc + m_prev
            mx = jnp.maximum(inter, jnp.max(log_d, axis=1, keepdims=True))
            dmat = jnp.exp(log_d - mx)
            dec = jnp.exp(inter - mx)
            qh = qc[:, h * dh:(h + 1) * dh]
            kth = ktc[h * dh:(h + 1) * dh, :]
            vaug = jnp.concatenate([vch[:, h * dh:(h + 1) * dh], one_hot0], axis=1)
            sm = (jnp.dot(qh, kth.astype(BF16), preferred_element_type=F32) * dmat).astype(BF16)
            c_prev = c_s[h]
            na = (jnp.dot(sm, vaug, preferred_element_type=F32)
                  + dec * jnp.dot(qh, c_prev.astype(BF16), preferred_element_type=F32))
            den = na[:, dh:dh + 1]
            hs.append(na[:, :dh] / jnp.maximum(jnp.abs(den), jnp.exp(-mx)))
            g_tot = br[:, L - 1:L]
            a_row = g_tot - br + ir
            m_new = jnp.maximum(g_tot + m_prev, jnp.max(a_row, axis=1, keepdims=True))
            decay = jnp.exp(g_tot + m_prev - m_new)
            w_row = jnp.exp(a_row - m_new)
            kw = (kth * w_row).astype(BF16)
            c_s[h] = decay * c_prev + jnp.dot(kw, vaug, preferred_element_type=F32)
            m_s[h] = m_new
        hcat = jnp.concatenate(hs, axis=1)
        ms = jnp.dot(hcat * hcat, hmean_ref[...], preferred_element_type=F32, precision=HIGHEST)
        y = hcat * lax.rsqrt(ms + LN_EPS) * ng_ref[...] * _sigmoid(och)
        y_ref[0, pl.ds(t0, L), :] = y.astype(y_ref.dtype)
        return 0

    lax.fori_loop(0, nc, chunk, 0)


def _mlstm(mlx, mlv, mlo, g_t, g_c, conv_w, conv_b, w_q, w_k, gate_b, norm_g):
    bsz, s, _ = mlx.shape
    H, dh = ML_HEADS, ML_HEAD_DIM
    eye = jnp.eye(H, dtype=F32)
    wq_bd = jnp.einsum('hde,hg->hdge', w_q, eye).reshape(ML_WIDTH, ML_WIDTH).astype(BF16)
    wk_bd = jnp.einsum('hde,hg->hdge', w_k * (dh ** -0.5), eye).reshape(ML_WIDTH, ML_WIDTH)
    wkt_bd = wk_bd.T.astype(BF16)
    gbt = gate_b.reshape(GATE_PAD, 1)
    gbc = jnp.pad(gate_b.reshape(1, GATE_PAD), ((0, 0), (0, LANES - GATE_PAD)))
    hmean = jnp.kron(eye, jnp.full((dh, dh), 1.0 / dh, F32))
    nc = s // ML_CHUNK
    g_t4 = g_t.reshape(bsz, GATE_PAD, nc, ML_CHUNK).transpose(0, 2, 1, 3)
    tok = lambda b: (b, 0, 0)
    c2 = lambda b: (0, 0)
    return pl.pallas_call(
        _mlstm_kernel,
        grid=(bsz,),
        in_specs=[pl.BlockSpec((1, s, ML_WIDTH), tok),
                  pl.BlockSpec((1, s, ML_WIDTH), tok),
                  pl.BlockSpec((1, s, ML_WIDTH), tok),
                  pl.BlockSpec((1, nc, GATE_PAD, ML_CHUNK), lambda b: (b, 0, 0, 0)),
                  pl.BlockSpec((1, s, LANES), tok),
                  pl.BlockSpec((ML_CONV, ML_WIDTH), c2),
                  pl.BlockSpec((1, ML_WIDTH), c2),
                  pl.BlockSpec((ML_WIDTH, ML_WIDTH), c2),
                  pl.BlockSpec((ML_WIDTH, ML_WIDTH), c2),
                  pl.BlockSpec((GATE_PAD, 1), c2),
                  pl.BlockSpec((1, LANES), c2),
                  pl.BlockSpec((1, ML_WIDTH), c2),
                  pl.BlockSpec((ML_WIDTH, ML_WIDTH), c2)],
        out_specs=pl.BlockSpec((1, s, ML_WIDTH), tok),
        out_shape=jax.ShapeDtypeStruct((bsz, s, ML_WIDTH), BF16),
        scratch_shapes=[pltpu.VMEM((s, ML_WIDTH), BF16),
                        pltpu.VMEM((H, dh, LANES), F32),
                        pltpu.VMEM((H, 1, 1), F32)],
        compiler_params=_cparams(("arbitrary",)),
        name="mlstm",
    )(mlx, mlv, mlo, g_t4, g_c, conv_w, conv_b.reshape(1, ML_WIDTH), wq_bd, wkt_bd, gbt, gbc,
      norm_g.reshape(1, ML_WIDTH), hmean)


def _gelu_tanh(x):
    return 0.5 * x * (1.0 + jnp.tanh(math.sqrt(2.0 / math.pi) * (x + 0.044715 * (x * x * x))))


def _s5_kernel(u_ref, are_ref, aim_ref, bcat_ref, ccat_ref, d_ref, wglu_ref, y_ref, xs_s, st_s):
    tc, bsz, w = u_ref.shape
    n = S5_NSTATE

    @pl.when(pl.program_id(0) == 0)
    def _():
        st_s[...] = jnp.zeros_like(st_s)

    u = u_ref[...].reshape(tc * bsz, w)
    xs_s[...] = jnp.dot(u.astype(BF16), bcat_ref[...], preferred_element_type=F32).reshape(tc, bsz, 2 * n)
    a_re = jnp.broadcast_to(are_ref[...], (bsz, n))
    a_im = jnp.broadcast_to(aim_ref[...], (bsz, n))

    def step(t, carry):
        x_re, x_im = carry
        bu = xs_s[t]
        n_re = a_re * x_re - a_im * x_im + bu[:, :n]
        n_im = a_re * x_im + a_im * x_re + bu[:, n:]
        xs_s[t] = jnp.concatenate([n_re, n_im], axis=1)
        return n_re, n_im

    x_re, x_im = lax.fori_loop(0, tc, step, (st_s[0], st_s[1]), unroll=S5_UNROLL)
    st_s[0] = x_re
    st_s[1] = x_im

    xs = xs_s[...].reshape(tc * bsz, 2 * n).astype(BF16)
    y = jnp.dot(xs, ccat_ref[...], preferred_element_type=F32) + d_ref[...] * u
    z = jnp.dot(_gelu_tanh(y).astype(BF16), wglu_ref[...], preferred_element_type=F32)
    out = z[:, :w] * _sigmoid(z[:, w:])
    y_ref[...] = out.reshape(tc, bsz, w).astype(y_ref.dtype)


def _s5_params(a_re, a_im, log_dt, b_re, b_im, c_re, c_im, w_glu):
    G, P, Hc = S5_GROUPS, S5_STATE, S5_GROUP
    dt = jnp.exp(log_dt)[:, None]
    mag = jnp.exp(a_re * dt)
    ab_re = mag * jnp.cos(a_im * dt)
    ab_im = mag * jnp.sin(a_im * dt)
    nr, ni = ab_re - 1.0, ab_im
    den = a_re * a_re + a_im * a_im
    fr = (nr * a_re + ni * a_im) / den
    fi = (ni * a_re - nr * a_im) / den
    bb_re = fr[..., None] * b_re - fi[..., None] * b_im
    bb_im = fr[..., None] * b_im + fi[..., None] * b_re
    eye = jnp.eye(G, dtype=F32)
    bd = lambda t, sub: jnp.einsum(sub, t, eye)
    bre = bd(bb_re, 'gph,gk->ghkp').reshape(G * Hc, G * P)
    bim = bd(bb_im, 'gph,gk->ghkp').reshape(G * Hc, G * P)
    bcat = jnp.concatenate([bre, bim], axis=1).astype(BF16)
    cre = bd(c_re, 'ghp,gk->gpkh').reshape(G * P, G * Hc)
    cim = bd(c_im, 'ghp,gk->gpkh').reshape(G * P, G * Hc)
    ccat = jnp.concatenate([cre, -cim], axis=0).astype(BF16)
    wv = bd(w_glu[:, :, :Hc], 'ghj,gk->ghkj').reshape(G * Hc, G * Hc)
    wg = bd(w_glu[:, :, Hc:], 'ghj,gk->ghkj').reshape(G * Hc, G * Hc)
    wglu = jnp.concatenate([wv, wg], axis=1).astype(BF16)
    return ab_re.reshape(1, G * P), ab_im.reshape(1, G * P), bcat, ccat, wglu


def _s5(u_tm, bsz, params, d_skip):
    s = u_tm.shape[0]
    w = S5_WIDTH
    n = S5_NSTATE
    are, aim, bcat, ccat, wglu = params
    u3 = u_tm.reshape(s, bsz, w)
    tc = S5_TC
    c2 = lambda i: (0, 0)
    y = pl.pallas_call(
        _s5_kernel,
        grid=(s // tc,),
        in_specs=[pl.BlockSpec((tc, bsz, w), lambda i: (i, 0, 0)),
                  pl.BlockSpec((1, n), c2),
                  pl.BlockSpec((1, n), c2),
                  pl.BlockSpec((w, 2 * n), c2),
                  pl.BlockSpec((2 * n, w), c2),
                  pl.BlockSpec((1, w), c2),
                  pl.BlockSpec((w, 2 * w), c2)],
        out_specs=pl.BlockSpec((tc, bsz, w), lambda i: (i, 0, 0)),
        out_shape=jax.ShapeDtypeStruct((s, bsz, w), F32),
        scratch_shapes=[pltpu.VMEM((tc, bsz, 2 * n), F32),
                        pltpu.VMEM((2, bsz, n), F32)],
        compiler_params=_cparams(("arbitrary",)),
        name="s5",
    )(u3, are, aim, bcat, ccat, d_skip.reshape(1, w), wglu)
    return y.reshape(s, bsz * w)


def _layer_norm(z, g, b):
    mu = jnp.mean(z, axis=1, keepdims=True)
    zc = z - mu
    var = jnp.mean(zc * zc, axis=1, keepdims=True)
    return zc * lax.rsqrt(var + LN_EPS) * g + b


def _out_proj_kernel(yda_ref, yml_ref, ys5_ref, x_ref, gate_ref, lng_ref, lnb_ref, shift_ref, scale_ref,
                     wout_ref, wrt_ref, brt_ref, x1_ref, h2_ref, eid_ref, prob_ref):
    y = jnp.dot(yda_ref[0], wout_ref[0:DA_WIDTH, :], preferred_element_type=F32)
    y = y + jnp.dot(yml_ref[0], wout_ref[DA_WIDTH:DA_WIDTH + ML_WIDTH, :], preferred_element_type=F32)
    y = y + jnp.dot(ys5_ref[...].astype(BF16), wout_ref[DA_WIDTH + ML_WIDTH:, :], preferred_element_type=F32)
    x1 = _layer_norm(DN_ALPHA * x_ref[0] + (1.0 + gate_ref[0]) * y, lng_ref[...], lnb_ref[...])
    x1_ref[0] = x1
    h2 = x1 * (1.0 + scale_ref[0]) + shift_ref[0]
    h2_ref[...] = h2
    logits = lax.dot_general(wrt_ref[...], h2, (((1,), (1,)), ((), ())), preferred_element_type=F32,
                             precision=HIGHEST) + brt_ref[...]
    eidx = lax.broadcasted_iota(jnp.int32, logits.shape, 0)
    vals, ids = [], []
    for _ in range(TOP_K):
        mx = jnp.max(logits, axis=0, keepdims=True)
        sel = jnp.min(jnp.where(logits == mx, eidx, N_EXPERTS), axis=0, keepdims=True)
        vals.append(mx)
        ids.append(sel)
        logits = jnp.where(eidx == sel, -jnp.inf, logits)
    ex = [jnp.exp(v - vals[0]) for v in vals]
    tot = ex[0] + ex[1] + ex[2] + ex[3]
    zi = jnp.zeros_like(ids[0])
    zf = jnp.zeros_like(tot)
    eid_ref[...] = jnp.concatenate(ids + [zi] * (SUBLANES - TOP_K), axis=0)
    prob_ref[...] = jnp.concatenate([e / tot for e in ex] + [zf] * (SUBLANES - TOP_K), axis=0)


def _out_proj(y_da, y_ml, y_s5, x, gate, ln_g, ln_b, shift2, scale2, w_out_l, w_router_l, b_router_l):
    bsz, s, d = x.shape
    tm = TM_PROJ
    nt = s // tm
    tok = lambda b, i: (b, i, 0)
    per_b = lambda b, i: (b, 0, 0)
    c2 = lambda b, i: (0, 0)
    r3 = lambda a: a.reshape(bsz, 1, d)
    flat = lambda b, i: (0, b * nt + i)
    return pl.pallas_call(
        _out_proj_kernel,
        grid=(bsz, nt),
        in_specs=[pl.BlockSpec((1, tm, DA_WIDTH), tok),
                  pl.BlockSpec((1, tm, ML_WIDTH), tok),
                  pl.BlockSpec((tm, S5_WIDTH), lambda b, i: (i, b)),
                  pl.BlockSpec((1, tm, d), tok),
                  pl.BlockSpec((1, 1, d), per_b),
                  pl.BlockSpec((1, d), c2),
                  pl.BlockSpec((1, d), c2),
                  pl.BlockSpec((1, 1, d), per_b),
                  pl.BlockSpec((1, 1, d), per_b),
                  pl.BlockSpec((d, d), c2),
                  pl.BlockSpec((N_EXPERTS, d), c2),
                  pl.BlockSpec((N_EXPERTS, 1), c2)],
        out_specs=[pl.BlockSpec((1, tm, d), tok),
                   pl.BlockSpec((tm, d), lambda b, i: (b * nt + i, 0)),
                   pl.BlockSpec((SUBLANES, tm), flat),
                   pl.BlockSpec((SUBLANES, tm), flat)],
        out_shape=[jax.ShapeDtypeStruct((bsz, s, d), F32),
                   jax.ShapeDtypeStruct((bsz * s, d), F32),
                   jax.ShapeDtypeStruct((SUBLANES, bsz * s), jnp.int32),
                   jax.ShapeDtypeStruct((SUBLANES, bsz * s), F32)],
        compiler_params=_cparams(("arbitrary", "arbitrary")),
        name="out_proj",
    )(y_da, y_ml, y_s5, x, r3(gate), ln_g.reshape(1, d), ln_b.reshape(1, d), r3(shift2), r3(scale2),
      w_out_l.astype(BF16), w_router_l.T, b_router_l.reshape(N_EXPERTS, 1))


META_TE, META_NT, META_END, META_PAD = 0, 1, 2, 3


def _route_kernel(eid_ref, pos_ref, meta_ref, carry_s, start_s):
    phase = pl.program_id(0)
    i = pl.program_id(1)
    tb = eid_ref.shape[1]
    ntp = meta_ref.shape[1]
    tm = TM_MOE

    @pl.when(jnp.logical_and(phase == 0, i == 0))
    def _():
        carry_s[...] = jnp.zeros_like(carry_s)

    eid = eid_ref[...]
    eidx = lax.broadcasted_iota(jnp.int32, (N_EXPERTS, tb), 0)
    hot = [eidx == eid[k:k + 1, :] for k in range(TOP_K)]
    member = jnp.zeros((N_EXPERTS, tb), F32)
    for k in range(TOP_K):
        member = member + hot[k].astype(F32)
    total = jnp.sum(member, axis=1, keepdims=True)

    @pl.when(phase == 0)
    def _():
        carry_s[...] = carry_s[...] + total

    @pl.when(jnp.logical_and(phase == 1, i == 0))
    def _():
        cnt = carry_s[...]
        padded = jnp.floor((cnt + (tm - 1)) * (1.0 / tm)) * tm
        er = lax.broadcasted_iota(jnp.int32, (N_EXPERTS, N_EXPERTS), 0)
        ec = lax.broadcasted_iota(jnp.int32, (N_EXPERTS, N_EXPERTS), 1)
        ends = jnp.dot((ec <= er).astype(F32), padded, preferred_element_type=F32, precision=HIGHEST)
        start_s[...] = ends - padded
        carry_s[...] = jnp.zeros_like(carry_s)
        end_c = ends[:, 0:1]
        grand = end_c[N_EXPERTS - 1:N_EXPERTS, :]
        tile0 = lax.broadcasted_iota(jnp.int32, (1, ntp), 1).astype(F32) * tm
        te = jnp.sum((end_c <= tile0).astype(F32), axis=0, keepdims=True)
        te_last = jnp.sum((end_c <= grand - tm).astype(F32), axis=0, keepdims=True)
        te = jnp.where(tile0 < grand, te, te_last)
        lane = lax.broadcasted_iota(jnp.int32, (N_EXPERTS, ntp), 1)
        sub = lax.broadcasted_iota(jnp.int32, (N_EXPERTS, ntp), 0)
        diag = lane == sub
        end_r = jnp.sum(jnp.where(diag, end_c, 0.0), axis=0, keepdims=True)
        pad_r = jnp.sum(jnp.where(diag, padded[:, 0:1], 0.0), axis=0, keepdims=True)
        nt_r = jnp.broadcast_to(grand * (1.0 / tm), (1, ntp))
        zero = jnp.zeros((SUBLANES - 4, ntp), F32)
        meta_ref[...] = jnp.concatenate([te, nt_r, end_r, pad_r, zero], axis=0).astype(jnp.int32)

    @pl.when(phase == 1)
    def _():
        ri = lax.broadcasted_iota(jnp.int32, (tb, tb), 0)
        ci = lax.broadcasted_iota(jnp.int32, (tb, tb), 1)
        triu = (ri <= ci).astype(BF16)
        incl = jnp.dot(member.astype(BF16), triu, preferred_element_type=F32)
        slot = incl - member + carry_s[:, 0:1] + start_s[:, 0:1]
        rows = [jnp.sum(jnp.where(hot[k], slot, 0.0), axis=0, keepdims=True) for k in range(TOP_K)]
        zr = jnp.zeros_like(rows[0])
        pos_ref[...] = jnp.concatenate(rows + [zr] * (SUBLANES - TOP_K), axis=0).astype(jnp.int32)
        carry_s[...] = carry_s[...] + total


def _route(eid, n_tiles_max):
    t = eid.shape[1]
    tb = TB_RANK
    ntp = ((n_tiles_max + LANES - 1) // LANES) * LANES
    pos8, meta = pl.pallas_call(
        _route_kernel,
        grid=(2, t // tb),
        in_specs=[pl.BlockSpec((SUBLANES, tb), lambda p, i: (0, i))],
        out_specs=[pl.BlockSpec((SUBLANES, tb), lambda p, i: (0, i * p)),
                   pl.BlockSpec((SUBLANES, ntp), lambda p, i: (0, 0))],
        out_shape=[jax.ShapeDtypeStruct((SUBLANES, t), jnp.int32),
                   jax.ShapeDtypeStruct((SUBLANES, ntp), jnp.int32)],
        scratch_shapes=[pltpu.VMEM((N_EXPERTS, LANES), F32), pltpu.VMEM((N_EXPERTS, LANES), F32)],
        compiler_params=_cparams(("arbitrary", "arbitrary")),
        name="route",
    )(eid)
    pos = pos8[:TOP_K].T.reshape(-1)
    return pos, meta


def _dispatch_kernel(meta_ref, pos_ref, h_ref, xs_ref, zero_s, sem, zsem):
    tm = h_ref.shape[0]
    n_tiles_max = xs_ref.shape[0] // TM_MOE

    @pl.when(pl.program_id(0) == 0)
    def _():
        zero_s[...] = jnp.zeros_like(zero_s)

        def zero_tile(row0):
            return pltpu.make_async_copy(zero_s, xs_ref.at[pl.ds(pl.multiple_of(row0, TM_MOE), TM_MOE)], zsem)

        for phase in ("start", "wait"):
            for e in range(N_EXPERTS):
                @pl.when(meta_ref[META_PAD, e] > 0)
                def _():
                    getattr(zero_tile(meta_ref[META_END, e] - TM_MOE), phase)()

                idle = meta_ref[META_NT, 0] + e

                @pl.when(idle < n_tiles_max)
                def _():
                    getattr(zero_tile(idle * TM_MOE), phase)()

    def issue(i, _):
        for k in range(TOP_K):
            dst = pos_ref[0, 0, i * TOP_K + k]
            pltpu.make_async_copy(h_ref.at[pl.ds(i, 1)], xs_ref.at[pl.ds(dst, 1)], sem).start()
        return 0

    lax.fori_loop(0, tm, issue, 0)
    for _ in range(TOP_K):
        pltpu.make_async_copy(h_ref, xs_ref.at[pl.ds(0, tm)], sem).wait()


def _dispatch(h2, pos, meta, n_rows):
    t, d = h2.shape
    tm = TM_DISP
    pos3 = pos.reshape(t // tm, 1, tm * TOP_K)
    grid_spec = pltpu.PrefetchScalarGridSpec(
        num_scalar_prefetch=1,
        grid=(t // tm,),
        in_specs=[pl.BlockSpec((1, 1, tm * TOP_K), lambda i, m: (i, 0, 0), memory_space=pltpu.SMEM),
                  pl.BlockSpec((tm, d), lambda i, m: (i, 0))],
        out_specs=pl.BlockSpec(memory_space=pl.ANY),
        scratch_shapes=[pltpu.VMEM((TM_MOE, d), h2.dtype), pltpu.SemaphoreType.DMA(()),
                        pltpu.SemaphoreType.DMA(())],
    )
    return pl.pallas_call(
        _dispatch_kernel,
        grid_spec=grid_spec,
        out_shape=jax.ShapeDtypeStruct((n_rows, d), h2.dtype),
        compiler_params=_cparams(("arbitrary",)),
        name="dispatch",
    )(meta, pos3, h2)


def _expert_kernel(te_ref, nt_ref, x_ref, wup_ref, bup_ref, wdn_ref, bdn_ref, o_ref, wup_s, wdn_s):
    j = pl.program_id(0)
    n_act = nt_ref[0]
    prev = te_ref[jnp.maximum(j - 1, 0)]
    fresh = jnp.logical_or(j == 0, te_ref[j] != prev)

    @pl.when(jnp.logical_and(j < n_act, fresh))
    def _():
        wup_s[...] = wup_ref[0, 0].astype(BF16)
        wdn_s[...] = wdn_ref[0, 0].astype(BF16)

    @pl.when(j < n_act)
    def _():
        x = x_ref[...].astype(BF16)
        z = jnp.dot(x, wup_s[...], preferred_element_type=F32) + bup_ref[0, 0]
        glu = jnp.minimum(z[:, :D_EXPERT], SWIGLU_LIMIT)
        lin = jnp.clip(z[:, D_EXPERT:], -SWIGLU_LIMIT, SWIGLU_LIMIT)
        act = (glu * _sigmoid(SWIGLU_ALPHA * glu) * (lin + 1.0)).astype(BF16)
        o_ref[...] = jnp.dot(act, wdn_s[...], preferred_element_type=F32) + bdn_ref[0, 0]

    @pl.when(j >= n_act)
    def _():
        o_ref[...] = jnp.zeros_like(o_ref)


def _expert_mlp(xs, tile_expert, n_tiles, layer, w_up, b_up, w_down, b_down):
    n_rows, d = xs.shape
    tm = TM_MOE
    nt_max = n_rows // tm
    f = w_up.shape[-1]
    b_up4 = b_up.reshape(DEPTH, N_EXPERTS, 1, f)
    b_dn4 = b_down.reshape(DEPTH, N_EXPERTS, 1, d)
    row = lambda j, te, nt: (jnp.minimum(j, nt[0] - 1), 0)
    wsel = lambda j, te, nt: (layer, te[j], 0, 0)
    grid_spec = pltpu.PrefetchScalarGridSpec(
        num_scalar_prefetch=2,
        grid=(nt_max,),
        in_specs=[pl.BlockSpec((tm, d), row),
                  pl.BlockSpec((1, 1, d, f), wsel),
                  pl.BlockSpec((1, 1, 1, f), wsel),
                  pl.BlockSpec((1, 1, f // 2, d), wsel),
                  pl.BlockSpec((1, 1, 1, d), wsel)],
        out_specs=pl.BlockSpec((tm, d), lambda j, te, nt: (j, 0)),
        scratch_shapes=[pltpu.VMEM((d, f), BF16), pltpu.VMEM((f // 2, d), BF16)],
    )
    return pl.pallas_call(
        _expert_kernel,
        grid_spec=grid_spec,
        out_shape=jax.ShapeDtypeStruct((n_rows, d), F32),
        compiler_params=_cparams(("arbitrary",)),
        name="expert_mlp",
    )(tile_expert, n_tiles, xs, w_up, b_up4, w_down, b_dn4)


def _combine_kernel(pos_ref, ys_ref, prob_ref, x_ref, gate_ref, lng_ref, lnb_ref, o_ref, rows_s, sem):
    tm = x_ref.shape[1]

    def issue(i, _):
        for k in range(TOP_K):
            src = pos_ref[0, 0, i * TOP_K + k]
            pltpu.make_async_copy(ys_ref.at[pl.ds(src, 1)], rows_s.at[k, pl.ds(i, 1)], sem).start()
        return 0

    lax.fori_loop(0, tm, issue, 0)
    for k in range(TOP_K):
        pltpu.make_async_copy(ys_ref.at[pl.ds(0, tm)], rows_s.at[k], sem).wait()
    p = prob_ref[...]
    y = p[:, 0:1] * rows_s[0]
    for k in range(1, TOP_K):
        y = y + p[:, k:k + 1] * rows_s[k]
    o_ref[0] = _layer_norm(DN_ALPHA * x_ref[0] + (1.0 + gate_ref[0]) * y, lng_ref[...], lnb_ref[...])


def _combine(ys, pos, prob_c, x1, gate, ln_g, ln_b):
    bsz, s, d = x1.shape
    tm = TM_DISP
    nt = s // tm
    pos3 = pos.reshape(bsz * nt, 1, tm * TOP_K)
    return pl.pallas_call(
        _combine_kernel,
        grid=(bsz, nt),
        in_specs=[pl.BlockSpec((1, 1, tm * TOP_K), lambda b, i: (b * nt + i, 0, 0), memory_space=pltpu.SMEM),
                  pl.BlockSpec(memory_space=pl.ANY),
                  pl.BlockSpec((tm, LANES), lambda b, i: (b * nt + i, 0)),
                  pl.BlockSpec((1, tm, d), lambda b, i: (b, i, 0)),
                  pl.BlockSpec((1, 1, d), lambda b, i: (b, 0, 0)),
                  pl.BlockSpec((1, d), lambda b, i: (0, 0)),
                  pl.BlockSpec((1, d), lambda b, i: (0, 0))],
        out_specs=pl.BlockSpec((1, tm, d), lambda b, i: (b, i, 0)),
        out_shape=jax.ShapeDtypeStruct((bsz, s, d), F32),
        scratch_shapes=[pltpu.VMEM((TOP_K, tm, d), F32), pltpu.SemaphoreType.DMA(())],
        compiler_params=_cparams(("arbitrary", "arbitrary")),
        name="combine",
    )(pos3, ys, prob_c, x1, gate.reshape(bsz, 1, d), ln_g.reshape(1, d), ln_b.reshape(1, d))


def kernel(x, c, positions, ada_w, ada_b, w_in, lam_q1, lam_k1, lam_q2, lam_k2, da_norm_g, ml_conv_w, ml_conv_b,
           ml_w_q, ml_w_k, ml_gate_b, ml_norm_g, s5_a_re, s5_a_im, s5_log_dt, s5_b_re, s5_b_im, s5_c_re, s5_c_im,
           s5_d, s5_w_glu, w_out, ln_g, ln_b, w_router, b_router, w_up, b_up, w_down, b_down):
    bsz, s, d = x.shape
    t = bsz * s
    n_tiles_max = (t * TOP_K) // TM_MOE + N_EXPERTS
    n_rows = n_tiles_max * TM_MOE
    mod = _modulation(c, ada_w, ada_b)
    cos_t, sin_t = _rope_tables(positions)
    for l in range(DEPTH):
        shift, scale, gate = jnp.split(mod[2 * l], 3, axis=-1)
        q, k, v, mlx, mlv, mlo, g_t, g_c, s5u = _in_proj(x, shift, scale, cos_t, sin_t, w_in[l])
        lam_init = 0.8 - 0.6 * math.exp(-0.3 * l)
        lamv = jnp.stack([lam_q1[l], lam_k1[l], lam_q2[l], lam_k2[l]])
        y_da = _diff_attn(q, k, v, lamv, da_norm_g[l], lam_init)
        y_ml = _mlstm(mlx, mlv, mlo, g_t, g_c, ml_conv_w[l], ml_conv_b[l], ml_w_q[l], ml_w_k[l],
                      ml_gate_b[l], ml_norm_g[l])
        s5p = _s5_params(s5_a_re[l], s5_a_im[l], s5_log_dt[l], s5_b_re[l], s5_b_im[l], s5_c_re[l], s5_c_im[l],
                         s5_w_glu[l])
        y_s5 = _s5(s5u, bsz, s5p, s5_d[l])
        shift2, scale2, gate2 = jnp.split(mod[2 * l + 1], 3, axis=-1)
        x1, h2, eid, prob = _out_proj(y_da, y_ml, y_s5, x, gate, ln_g[l, 0], ln_b[l, 0], shift2, scale2,
                                      w_out[l], w_router[l], b_router[l])
        pos, meta = _route(eid, n_tiles_max)
        xs = _dispatch(h2, pos, meta, n_rows)
        ys = _expert_mlp(xs, meta[META_TE, :n_tiles_max], meta[META_NT, :1], l, w_up, b_up, w_down, b_down)
        prob_c = jnp.pad(prob[:TOP_K].T, ((0, 0), (0, LANES - TOP_K)))
        x = _combine(ys, pos, prob_c, x1, gate2, ln_g[l, 1], ln_b[l, 1])
    return x
```

```python
import functools
import math

import jax
import jax.numpy as jnp
from jax import lax
from jax.experimental import pallas as pl
from jax.experimental.pallas import tpu as pltpu
from jax.experimental.pallas import tpu_sc as plsc

F32 = jnp.float32
BF16 = jnp.bfloat16
HIGHEST = lax.Precision.HIGHEST

D_MODEL = 1024
DEPTH = 2
DA_HEADS = 4
DA_HEAD_DIM = 64
DA_V_DIM = 2 * DA_HEAD_DIM
DA_WIDTH = DA_HEADS * DA_V_DIM
DA_QK_WIDTH = DA_HEADS * 2 * DA_HEAD_DIM
ROPE_THETA = 10000.0
ML_HEADS = 4
ML_HEAD_DIM = 64
ML_WIDTH = ML_HEADS * ML_HEAD_DIM
ML_CONV = 4
S5_GROUP = 16
S5_STATE = 64
S5_WIDTH = D_MODEL - DA_WIDTH - ML_WIDTH
S5_GROUPS = S5_WIDTH // S5_GROUP
S5_NSTATE = S5_GROUPS * S5_STATE
N_EXPERTS = 32
TOP_K = 4
D_EXPERT = D_MODEL
SWIGLU_LIMIT = 7.0
SWIGLU_ALPHA = 1.702
DN_ALPHA = (2 * DEPTH) ** 0.25
LN_EPS = 1e-5
NEG = -1e30

OFF_DA_K = DA_QK_WIDTH
OFF_DA_V = 2 * DA_QK_WIDTH
OFF_ML_X = OFF_DA_V + DA_WIDTH
OFF_ML_V = OFF_ML_X + ML_WIDTH
OFF_ML_O = OFF_ML_V + ML_WIDTH
OFF_ML_I = OFF_ML_O + ML_WIDTH
OFF_ML_F = OFF_ML_I + ML_HEADS
OFF_S5_U = OFF_ML_F + ML_HEADS
N_IN = OFF_S5_U + S5_WIDTH

LANES = 128
SUBLANES = 8
VMEM_LIMIT_BYTES = 56 * 1024 * 1024

TM_PROJ = 512
TQ = 512
ML_CHUNK = 256
S5_TC = 128
S5_UNROLL = 8
TB_RANK = 512
TM_MOE = 256
TM_DISP = 256
SC_CHUNK = 64
GATE_PAD = 8
VT_ROWS = DA_V_DIM + 16
Q_PRESCALE = DA_HEAD_DIM ** -0.5 * math.log2(math.e)


def _cparams(sem, vmem=VMEM_LIMIT_BYTES):
    return pltpu.CompilerParams(dimension_semantics=sem, vmem_limit_bytes=vmem)


def _sigmoid(x):
    return 1.0 / (1.0 + jnp.exp(-x))


def _mod_kernel(c_ref, w_ref, b_ref, o_ref):
    c = c_ref[...]
    ca = (c * _sigmoid(c)).astype(BF16)
    w = w_ref[0].astype(BF16)
    o_ref[0] = jnp.dot(ca, w, preferred_element_type=F32) + b_ref[0]


def _modulation(c, ada_w, ada_b):
    nmod = ada_w.shape[0] * ada_w.shape[1]
    bsz, d = c.shape
    e = ada_w.shape[-1]
    tn = 1024
    w = ada_w.reshape(nmod, d, e)
    b = ada_b.reshape(nmod, 1, e)
    return pl.pallas_call(
        _mod_kernel,
        grid=(nmod, e // tn),
        in_specs=[pl.BlockSpec((bsz, d), lambda n, j: (0, 0)),
                  pl.BlockSpec((1, d, tn), lambda n, j: (n, 0, j)),
                  pl.BlockSpec((1, 1, tn), lambda n, j: (n, 0, j))],
        out_specs=pl.BlockSpec((1, bsz, tn), lambda n, j: (n, 0, j)),
        out_shape=jax.ShapeDtypeStruct((nmod, bsz, e), F32),
        compiler_params=_cparams(("arbitrary", "arbitrary")),
        name="modulation",
    )(c, w, b)


def _rope_kernel(pos_ref, cos_ref, sin_ref):
    pos = pos_ref[0].astype(F32)
    lane = lax.broadcasted_iota(jnp.int32, (1, LANES), 1)
    fidx = (lane % (DA_HEAD_DIM // 2)).astype(F32)
    inv = jnp.exp(fidx * (-2.0 * math.log(ROPE_THETA) / DA_HEAD_DIM))
    ang = pos * inv
    sign = jnp.where((lane % DA_HEAD_DIM) < DA_HEAD_DIM // 2, -1.0, 1.0)
    cos_ref[0] = jnp.cos(ang)
    sin_ref[0] = jnp.sin(ang) * sign


def _rope_tables(positions):
    bsz, s = positions.shape
    ts = 512
    pos3 = positions.reshape(bsz, s, 1)
    return pl.pallas_call(
        _rope_kernel,
        grid=(bsz, s // ts),
        in_specs=[pl.BlockSpec((1, ts, 1), lambda b, i: (b, i, 0))],
        out_specs=[pl.BlockSpec((1, ts, LANES), lambda b, i: (b, i, 0))] * 2,
        out_shape=[jax.ShapeDtypeStruct((bsz, s, LANES), F32)] * 2,
        compiler_params=_cparams(("arbitrary", "arbitrary")),
        name="rope_tables",
    )(pos3)


def _in_proj_kernel(x_ref, shift_ref, scale_ref, cos_ref, sin_ref, wqk_ref, wvt_ref, wrest_ref, wgt_ref, wgc_ref,
                    q_ref, k_ref, vt_ref, mlx_ref, mlv_ref, mlo_ref, gt_ref, gc_ref, s5u_ref):
    h = (x_ref[0] * (1.0 + scale_ref[0]) + shift_ref[0]).astype(BF16)
    cos = cos_ref[0]
    sin = sin_ref[0]
    lane = lax.broadcasted_iota(jnp.int32, (1, LANES), 1)
    lo_half = (lane % DA_HEAD_DIM) < DA_HEAD_DIM // 2
    half = DA_HEAD_DIM // 2

    def rope(t):
        fwd = pltpu.roll(t, half, 1)
        bwd = pltpu.roll(t, LANES - half, 1)
        partner = jnp.where(lo_half, bwd, fwd)
        return t * cos + partner * sin

    qk = jnp.dot(h, wqk_ref[...], preferred_element_type=F32)
    nslab = DA_QK_WIDTH // LANES
    for c in range(nslab):
        q_ref[0, :, c * LANES:(c + 1) * LANES] = (
            rope(qk[:, c * LANES:(c + 1) * LANES]) * Q_PRESCALE).astype(BF16)
        k_ref[0, :, c * LANES:(c + 1) * LANES] = rope(
            qk[:, DA_QK_WIDTH + c * LANES:DA_QK_WIDTH + (c + 1) * LANES]).astype(BF16)

    vt = lax.dot_general(wvt_ref[...], h, (((1,), (1,)), ((), ())), preferred_element_type=F32)
    tm = h.shape[0]
    for hh in range(DA_HEADS):
        for jj in range(tm // TQ):
            vt_ref[0, hh, jj, 0:DA_V_DIM, :] = vt[hh * DA_V_DIM:(hh + 1) * DA_V_DIM,
                                                  jj * TQ:(jj + 1) * TQ].astype(BF16)
            vt_ref[0, hh, jj, DA_V_DIM:VT_ROWS, :] = jnp.ones((VT_ROWS - DA_V_DIM, TQ), BF16)

    r = jnp.dot(h, wrest_ref[...], preferred_element_type=F32)
    o = 0
    mlx_ref[0] = r[:, o:o + ML_WIDTH].astype(BF16); o += ML_WIDTH
    mlv_ref[0] = r[:, o:o + ML_WIDTH].astype(BF16); o += ML_WIDTH
    mlo_ref[0] = r[:, o:o + ML_WIDTH].astype(BF16); o += ML_WIDTH
    s5u_ref[...] = r[:, o:o + S5_WIDTH]
    gt_ref[0] = lax.dot_general(wgt_ref[...], h, (((1,), (1,)), ((), ())), preferred_element_type=F32)
    gc_ref[0] = jnp.dot(h, wgc_ref[...], preferred_element_type=F32)


def _in_proj(x, shift, scale, cos_t, sin_t, w_in_l):
    bsz, s, d = x.shape
    tm = TM_PROJ
    w = w_in_l.astype(BF16)
    wqk = w[:, :OFF_DA_V]
    wvt = w[:, OFF_DA_V:OFF_ML_X].T
    wrest = jnp.concatenate([w[:, OFF_ML_X:OFF_ML_I], w[:, OFF_S5_U:]], axis=1)
    wg = w[:, OFF_ML_I:OFF_S5_U]
    wgt = wg.T
    wgc = jnp.pad(wg, ((0, 0), (0, LANES - GATE_PAD)))
    nrest = wrest.shape[1]
    shift3 = shift.reshape(bsz, 1, d)
    scale3 = scale.reshape(bsz, 1, d)
    tok = lambda b, i: (b, i, 0)
    per_b = lambda b, i: (b, 0, 0)
    const2 = lambda b, i: (0, 0)
    out_shapes = [
        jax.ShapeDtypeStruct((bsz, s, DA_QK_WIDTH), BF16),
        jax.ShapeDtypeStruct((bsz, s, DA_QK_WIDTH), BF16),
        jax.ShapeDtypeStruct((bsz, DA_HEADS, s // TQ, VT_ROWS, TQ), BF16),
        jax.ShapeDtypeStruct((bsz, s, ML_WIDTH), BF16),
        jax.ShapeDtypeStruct((bsz, s, ML_WIDTH), BF16),
        jax.ShapeDtypeStruct((bsz, s, ML_WIDTH), BF16),
        jax.ShapeDtypeStruct((bsz, GATE_PAD, s), F32),
        jax.ShapeDtypeStruct((bsz, s, LANES), F32),
        jax.ShapeDtypeStruct((s, bsz * S5_WIDTH), F32),
    ]
    out_specs = [
        pl.BlockSpec((1, tm, DA_QK_WIDTH), tok),
        pl.BlockSpec((1, tm, DA_QK_WIDTH), tok),
        pl.BlockSpec((1, DA_HEADS, tm // TQ, VT_ROWS, TQ), lambda b, i: (b, 0, i, 0, 0)),
        pl.BlockSpec((1, tm, ML_WIDTH), tok),
        pl.BlockSpec((1, tm, ML_WIDTH), tok),
        pl.BlockSpec((1, tm, ML_WIDTH), tok),
        pl.BlockSpec((1, GATE_PAD, tm), lambda b, i: (b, 0, i)),
        pl.BlockSpec((1, tm, LANES), tok),
        pl.BlockSpec((tm, S5_WIDTH), lambda b, i: (i, b)),
    ]
    return pl.pallas_call(
        _in_proj_kernel,
        grid=(bsz, s // tm),
        in_specs=[pl.BlockSpec((1, tm, d), tok),
                  pl.BlockSpec((1, 1, d), per_b),
                  pl.BlockSpec((1, 1, d), per_b),
                  pl.BlockSpec((1, tm, LANES), tok),
                  pl.BlockSpec((1, tm, LANES), tok),
                  pl.BlockSpec((d, OFF_DA_V), const2),
                  pl.BlockSpec((DA_WIDTH, d), const2),
                  pl.BlockSpec((d, nrest), const2),
                  pl.BlockSpec((GATE_PAD, d), const2),
                  pl.BlockSpec((d, LANES), const2)],
        out_specs=out_specs,
        out_shape=out_shapes,
        compiler_params=_cparams(("arbitrary", "arbitrary")),
        name="in_proj",
    )(x, shift3, scale3, cos_t, sin_t, wqk, wvt, wrest, wgt, wgc)


def _diff_attn_kernel(lam_init, lamv_ref, gain_ref, q_ref, k_ref, vt_ref, o_ref, acc_s, m_s):
    qi = pl.program_id(2)
    tq = q_ref.shape[1]
    lane = lax.broadcasted_iota(jnp.int32, (1, LANES), 1)
    first = lane < DA_HEAD_DIM
    q = q_ref[0]
    zero = jnp.zeros_like(q)
    qm = (jnp.where(first, q, zero), jnp.where(first, zero, q))
    acc_s[...] = jnp.zeros_like(acc_s)
    m_s[...] = jnp.full(m_s.shape, NEG, F32)

    def step(j, masked):
        kb = k_ref[0, pl.ds(pl.multiple_of(j * tq, tq), tq), :]
        vtb = vt_ref[0, 0, j]
        for c in range(2):
            st = lax.dot_general(kb, qm[c], (((1,), (1,)), ((), ())), preferred_element_type=F32)
            if masked:
                key_i = lax.broadcasted_iota(jnp.int32, (tq, tq), 0)
                qry_i = lax.broadcasted_iota(jnp.int32, (tq, tq), 1)
                st = jnp.where(key_i <= qry_i, st, NEG)
            m_prev = m_s[c]
            m_new = jnp.maximum(m_prev, jnp.max(st, axis=0, keepdims=True))
            alpha = jnp.exp2(m_prev - m_new)
            p = jnp.exp2(st - m_new).astype(BF16)
            acc_s[c] = alpha * acc_s[c] + jnp.dot(vtb, p, preferred_element_type=F32)
            m_s[c] = m_new

    def body(j, carry):
        step(j, False)
        return carry

    lax.fori_loop(0, qi, body, 0)
    step(qi, True)
    outs = []
    for c in range(2):
        acc = acc_s[c]
        outs.append(acc[:DA_V_DIM] / acc[DA_V_DIM:DA_V_DIM + 1])

    lamv = lamv_ref[...]
    lam = (jnp.exp(jnp.sum(lamv[0:1] * lamv[1:2], axis=1, keepdims=True))
           - jnp.exp(jnp.sum(lamv[2:3] * lamv[3:4], axis=1, keepdims=True)) + lam_init)
    ot = outs[0] - lam * outs[1]
    ms = jnp.mean(ot * ot, axis=0, keepdims=True)
    ot = ot * (lax.rsqrt(ms + LN_EPS) * (1.0 - lam_init))
    o_ref[0] = (ot.T * gain_ref[...]).astype(o_ref.dtype)


def _diff_attn(q, k, vt, lamv, gain, lam_init):
    bsz, s, _ = q.shape
    tq = TQ
    nq = s // tq
    return pl.pallas_call(
        functools.partial(_diff_attn_kernel, lam_init),
        grid=(bsz, DA_HEADS, nq),
        in_specs=[pl.BlockSpec((4, DA_HEAD_DIM), lambda b, h, i: (0, 0)),
                  pl.BlockSpec((1, DA_V_DIM), lambda b, h, i: (0, 0)),
                  pl.BlockSpec((1, tq, DA_V_DIM), lambda b, h, i: (b, i, h)),
                  pl.BlockSpec((1, s, DA_V_DIM), lambda b, h, i: (b, 0, h)),
                  pl.BlockSpec((1, 1, nq, VT_ROWS, tq), lambda b, h, i: (b, h, 0, 0, 0))],
        out_specs=pl.BlockSpec((1, tq, DA_V_DIM), lambda b, h, i: (b, i, h)),
        out_shape=jax.ShapeDtypeStruct((bsz, s, DA_WIDTH), BF16),
        scratch_shapes=[pltpu.VMEM((2, VT_ROWS, tq), F32), pltpu.VMEM((2, 1, tq), F32)],
        compiler_params=_cparams(("arbitrary", "arbitrary", "arbitrary")),
        name="diff_attn",
    )(lamv, gain.reshape(1, DA_V_DIM), q, k, vt)


def _log_sigmoid(x):
    return jnp.minimum(x, 0.0) - jnp.log(1.0 + jnp.exp(-jnp.abs(x)))


def _mlstm_kernel(x_ref, v_ref, o_ref, gt_ref, gc_ref, cw_ref, cb_ref, wq_ref, wkt_ref, gbt_ref, gbc_ref,
                  ng_ref, hmean_ref, y_ref, xc_s, c_s, m_s):
    s = x_ref.shape[1]
    L = ML_CHUNK
    H, dh = ML_HEADS, ML_HEAD_DIM
    nc = s // L
    x = x_ref[0].astype(F32)
    cw = cw_ref[...]
    row = lax.broadcasted_iota(jnp.int32, (s, 1), 0)
    xc = x * cw[ML_CONV - 1:ML_CONV]
    for j in range(1, ML_CONV):
        xs = jnp.where(row >= j, pltpu.roll(x, j, 0), 0.0)
        xc = xc + xs * cw[ML_CONV - 1 - j:ML_CONV - j]
    xc = xc + cb_ref[...]
    xc_s[...] = (xc * _sigmoid(xc)).astype(BF16)

    c_s[...] = jnp.zeros_like(c_s)
    m_s[...] = jnp.full(m_s.shape, NEG, F32)

    ri = lax.broadcasted_iota(jnp.int32, (L, L), 0)
    ci = lax.broadcasted_iota(jnp.int32, (L, L), 1)
    causal = ci <= ri
    tril = causal.astype(F32)
    triu = (ri <= ci).astype(F32)
    lane = lax.broadcasted_iota(jnp.int32, (1, dh), 1)
    one_hot0 = jnp.broadcast_to((lane == 0).astype(BF16), (L, dh))

    def chunk(ci_, _):
        t0 = pl.multiple_of(ci_ * L, L)
        xcc = xc_s[pl.ds(t0, L), :]
        qc = jnp.dot(xcc, wq_ref[...], preferred_element_type=F32).astype(BF16)
        ktc = lax.dot_general(wkt_ref[...], xcc, (((1,), (1,)), ((), ())),
                              preferred_element_type=F32)
        g_rows = gt_ref[0, ci_] + gbt_ref[...]
        g_cols = gc_ref[0, pl.ds(t0, L), :] + gbc_ref[...]
        lf_rows = _log_sigmoid(g_rows)
        lf_cols = _log_sigmoid(g_cols)
        b_rows = jnp.dot(lf_rows, triu, preferred_element_type=F32, precision=HIGHEST)
        b_cols = jnp.dot(tril, lf_cols, preferred_element_type=F32, precision=HIGHEST)
        vch = v_ref[0, pl.ds(t0, L), :]
        och = o_ref[0, pl.ds(t0, L), :].astype(F32)
        hs = []
        for h in range(H):
            br = b_rows[H + h:H + h + 1, :]
            ir = g_rows[h:h + 1, :]
            bc = b_cols[:, H + h:H + h + 1]
            m_prev = m_s[h]
            log_d = jnp.where(causal, bc - br + ir, NEG)
            inter = bc + m_prev
            mx = jnp.maximum(inter, jnp.max(log_d, axis=1, keepdims=True))
            dmat = jnp.exp(log_d - mx)
            dec = jnp.exp(inter - mx)
            qh = qc[:, h * dh:(h + 1) * dh]
            kth = ktc[h * dh:(h + 1) * dh, :]
            vaug = jnp.concatenate([vch[:, h * dh:(h + 1) * dh], one_hot0], axis=1)
            sm = (jnp.dot(qh, kth.astype(BF16), preferred_element_type=F32) * dmat).astype(BF16)
            c_prev = c_s[h]
            na = (jnp.dot(sm, vaug, preferred_element_type=F32)
                  + dec * jnp.dot(qh, c_prev.astype(BF16), preferred_element_type=F32))
            den = na[:, dh:dh + 1]
            hs.append(na[:, :dh] / jnp.maximum(jnp.abs(den), jnp.exp(-mx)))
            g_tot = br[:, L - 1:L]
            a_row = g_tot - br + ir
            m_new = jnp.maximum(g_tot + m_prev, jnp.max(a_row, axis=1, keepdims=True))
            decay = jnp.exp(g_tot + m_prev - m_new)
            w_row = jnp.exp(a_row - m_new)
            kw = (kth * w_row).astype(BF16)
            c_s[h] = decay * c_prev + jnp.dot(kw, vaug, preferred_element_type=F32)
            m_s[h] = m_new
        hcat = jnp.concatenate(hs, axis=1)
        ms = jnp.dot(hcat * hcat, hmean_ref[...], preferred_element_type=F32, precision=HIGHEST)
        y = hcat * lax.rsqrt(ms + LN_EPS) * ng_ref[...] * _sigmoid(och)
        y_ref[0, pl.ds(t0, L), :] = y.astype(y_ref.dtype)
        return 0

    lax.fori_loop(0, nc, chunk, 0)


def _mlstm(mlx, mlv, mlo, g_t, g_c, conv_w, conv_b, w_q, w_k, gate_b, norm_g):
    bsz, s, _ = mlx.shape
    H, dh = ML_HEADS, ML_HEAD_DIM
    eye = jnp.eye(H, dtype=F32)
    wq_bd = jnp.einsum('hde,hg->hdge', w_q, eye).reshape(ML_WIDTH, ML_WIDTH).astype(BF16)
    wk_bd = jnp.einsum('hde,hg->hdge', w_k * (dh ** -0.5), eye).reshape(ML_WIDTH, ML_WIDTH)
    wkt_bd = wk_bd.T.astype(BF16)
    gbt = gate_b.reshape(GATE_PAD, 1)
    gbc = jnp.pad(gate_b.reshape(1, GATE_PAD), ((0, 0), (0, LANES - GATE_PAD)))
    hmean = jnp.kron(eye, jnp.full((dh, dh), 1.0 / dh, F32))
    nc = s // ML_CHUNK
    g_t4 = g_t.reshape(bsz, GATE_PAD, nc, ML_CHUNK).transpose(0, 2, 1, 3)
    tok = lambda b: (b, 0, 0)
    c2 = lambda b: (0, 0)
    return pl.pallas_call(
        _mlstm_kernel,
        grid=(bsz,),
        in_specs=[pl.BlockSpec((1, s, ML_WIDTH), tok),
                  pl.BlockSpec((1, s, ML_WIDTH), tok),
                  pl.BlockSpec((1, s, ML_WIDTH), tok),
                  pl.BlockSpec((1, nc, GATE_PAD, ML_CHUNK), lambda b: (b, 0, 0, 0)),
                  pl.BlockSpec((1, s, LANES), tok),
                  pl.BlockSpec((ML_CONV, ML_WIDTH), c2),
                  pl.BlockSpec((1, ML_WIDTH), c2),
                  pl.BlockSpec((ML_WIDTH, ML_WIDTH), c2),
                  pl.BlockSpec((ML_WIDTH, ML_WIDTH), c2),
                  pl.BlockSpec((GATE_PAD, 1), c2),
                  pl.BlockSpec((1, LANES), c2),
                  pl.BlockSpec((1, ML_WIDTH), c2),
                  pl.BlockSpec((ML_WIDTH, ML_WIDTH), c2)],
        out_specs=pl.BlockSpec((1, s, ML_WIDTH), tok),
        out_shape=jax.ShapeDtypeStruct((bsz, s, ML_WIDTH), BF16),
        scratch_shapes=[pltpu.VMEM((s, ML_WIDTH), BF16),
                        pltpu.VMEM((H, dh, LANES), F32),
                        pltpu.VMEM((H, 1, 1), F32)],
        compiler_params=_cparams(("arbitrary",)),
        name="mlstm",
    )(mlx, mlv, mlo, g_t4, g_c, conv_w, conv_b.reshape(1, ML_WIDTH), wq_bd, wkt_bd, gbt, gbc,
      norm_g.reshape(1, ML_WIDTH), hmean)


def _gelu_tanh(x):
    return 0.5 * x * (1.0 + jnp.tanh(math.sqrt(2.0 / math.pi) * (x + 0.044715 * (x * x * x))))


def _s5_kernel(u_ref, are_ref, aim_ref, bcat_ref, ccat_ref, d_ref, wglu_ref, y_ref, xs_s, st_s):
    tc, bsz, w = u_ref.shape
    n = S5_NSTATE

    @pl.when(pl.program_id(0) == 0)
    def _():
        st_s[...] = jnp.zeros_like(st_s)

    u = u_ref[...].reshape(tc * bsz, w)
    xs_s[...] = jnp.dot(u.astype(BF16), bcat_ref[...], preferred_element_type=F32).reshape(tc, bsz, 2 * n)
    a_re = jnp.broadcast_to(are_ref[...], (bsz, n))
    a_im = jnp.broadcast_to(aim_ref[...], (bsz, n))

    def step(t, carry):
        x_re, x_im = carry
        bu = xs_s[t]
        n_re = a_re * x_re - a_im * x_im + bu[:, :n]
        n_im = a_re * x_im + a_im * x_re + bu[:, n:]
        xs_s[t] = jnp.concatenate([n_re, n_im], axis=1)
        return n_re, n_im

    x_re, x_im = lax.fori_loop(0, tc, step, (st_s[0], st_s[1]), unroll=S5_UNROLL)
    st_s[0] = x_re
    st_s[1] = x_im

    xs = xs_s[...].reshape(tc * bsz, 2 * n).astype(BF16)
    y = jnp.dot(xs, ccat_ref[...], preferred_element_type=F32) + d_ref[...] * u
    z = jnp.dot(_gelu_tanh(y).astype(BF16), wglu_ref[...], preferred_element_type=F32)
    out = z[:, :w] * _sigmoid(z[:, w:])
    y_ref[...] = out.reshape(tc, bsz, w).astype(y_ref.dtype)


def _s5_params(a_re, a_im, log_dt, b_re, b_im, c_re, c_im, w_glu):
    G, P, Hc = S5_GROUPS, S5_STATE, S5_GROUP
    dt = jnp.exp(log_dt)[:, None]
    mag = jnp.exp(a_re * dt)
    ab_re = mag * jnp.cos(a_im * dt)
    ab_im = mag * jnp.sin(a_im * dt)
    nr, ni = ab_re - 1.0, ab_im
    den = a_re * a_re + a_im * a_im
    fr = (nr * a_re + ni * a_im) / den
    fi = (ni * a_re - nr * a_im) / den
    bb_re = fr[..., None] * b_re - fi[..., None] * b_im
    bb_im = fr[..., None] * b_im + fi[..., None] * b_re
    eye = jnp.eye(G, dtype=F32)
    bd = lambda t, sub: jnp.einsum(sub, t, eye)
    bre = bd(bb_re, 'gph,gk->ghkp').reshape(G * Hc, G * P)
    bim = bd(bb_im, 'gph,gk->ghkp').reshape(G * Hc, G * P)
    bcat = jnp.concatenate([bre, bim], axis=1).astype(BF16)
    cre = bd(c_re, 'ghp,gk->gpkh').reshape(G * P, G * Hc)
    cim = bd(c_im, 'ghp,gk->gpkh').reshape(G * P, G * Hc)
    ccat = jnp.concatenate([cre, -cim], axis=0).astype(BF16)
    wv = bd(w_glu[:, :, :Hc], 'ghj,gk->ghkj').reshape(G * Hc, G * Hc)
    wg = bd(w_glu[:, :, Hc:], 'ghj,gk->ghkj').reshape(G * Hc, G * Hc)
    wglu = jnp.concatenate([wv, wg], axis=1).astype(BF16)
    return ab_re.reshape(1, G * P), ab_im.reshape(1, G * P), bcat, ccat, wglu


def _s5(u_tm, bsz, params, d_skip):
    s = u_tm.shape[0]
    w = S5_WIDTH
    n = S5_NSTATE
    are, aim, bcat, ccat, wglu = params
    u3 = u_tm.reshape(s, bsz, w)
    tc = S5_TC
    c2 = lambda i: (0, 0)
    y = pl.pallas_call(
        _s5_kernel,
        grid=(s // tc,),
        in_specs=[pl.BlockSpec((tc, bsz, w), lambda i: (i, 0, 0)),
                  pl.BlockSpec((1, n), c2),
                  pl.BlockSpec((1, n), c2),
                  pl.BlockSpec((w, 2 * n), c2),
                  pl.BlockSpec((2 * n, w), c2),
                  pl.BlockSpec((1, w), c2),
                  pl.BlockSpec((w, 2 * w), c2)],
        out_specs=pl.BlockSpec((tc, bsz, w), lambda i: (i, 0, 0)),
        out_shape=jax.ShapeDtypeStruct((s, bsz, w), F32),
        scratch_shapes=[pltpu.VMEM((tc, bsz, 2 * n), F32),
                        pltpu.VMEM((2, bsz, n), F32)],
        compiler_params=_cparams(("arbitrary",)),
        name="s5",
    )(u3, are, aim, bcat, ccat, d_skip.reshape(1, w), wglu)
    return y.reshape(s, bsz * w)


def _layer_norm(z, g, b):
    mu = jnp.mean(z, axis=1, keepdims=True)
    zc = z - mu
    var = jnp.mean(zc * zc, axis=1, keepdims=True)
    return zc * lax.rsqrt(var + LN_EPS) * g + b


def _out_proj_kernel(yda_ref, yml_ref, ys5_ref, x_ref, gate_ref, lng_ref, lnb_ref, shift_ref, scale_ref,
                     wout_ref, wrt_ref, brt_ref, x1_ref, h2_ref, eid_ref, prob_ref):
    y = jnp.dot(yda_ref[0], wout_ref[0:DA_WIDTH, :], preferred_element_type=F32)
    y = y + jnp.dot(yml_ref[0], wout_ref[DA_WIDTH:DA_WIDTH + ML_WIDTH, :], preferred_element_type=F32)
    y = y + jnp.dot(ys5_ref[...].astype(BF16), wout_ref[DA_WIDTH + ML_WIDTH:, :], preferred_element_type=F32)
    x1 = _layer_norm(DN_ALPHA * x_ref[0] + (1.0 + gate_ref[0]) * y, lng_ref[...], lnb_ref[...])
    x1_ref[0] = x1
    h2 = x1 * (1.0 + scale_ref[0]) + shift_ref[0]
    h2_ref[...] = h2
    logits = lax.dot_general(wrt_ref[...], h2, (((1,), (1,)), ((), ())), preferred_element_type=F32,
                             precision=HIGHEST) + brt_ref[...]
    eidx = lax.broadcasted_iota(jnp.int32, logits.shape, 0)
    vals, ids = [], []
    for _ in range(TOP_K):
        mx = jnp.max(logits, axis=0, keepdims=True)
        sel = jnp.min(jnp.where(logits == mx, eidx, N_EXPERTS), axis=0, keepdims=True)
        vals.append(mx)
        ids.append(sel)
        logits = jnp.where(eidx == sel, -jnp.inf, logits)
    ex = [jnp.exp(v - vals[0]) for v in vals]
    tot = ex[0] + ex[1] + ex[2] + ex[3]
    zi = jnp.zeros_like(ids[0])
    zf = jnp.zeros_like(tot)
    eid_ref[...] = jnp.concatenate(ids + [zi] * (SUBLANES - TOP_K), axis=0)
    prob_ref[...] = jnp.concatenate([e / tot for e in ex] + [zf] * (SUBLANES - TOP_K), axis=0)


def _out_proj(y_da, y_ml, y_s5, x, gate, ln_g, ln_b, shift2, scale2, w_out_l, w_router_l, b_router_l):
    bsz, s, d = x.shape
    tm = TM_PROJ
    nt = s // tm
    tok = lambda b, i: (b, i, 0)
    per_b = lambda b, i: (b, 0, 0)
    c2 = lambda b, i: (0, 0)
    r3 = lambda a: a.reshape(bsz, 1, d)
    flat = lambda b, i: (0, b * nt + i)
    return pl.pallas_call(
        _out_proj_kernel,
        grid=(bsz, nt),
        in_specs=[pl.BlockSpec((1, tm, DA_WIDTH), tok),
                  pl.BlockSpec((1, tm, ML_WIDTH), tok),
                  pl.BlockSpec((tm, S5_WIDTH), lambda b, i: (i, b)),
                  pl.BlockSpec((1, tm, d), tok),
                  pl.BlockSpec((1, 1, d), per_b),
                  pl.BlockSpec((1, d), c2),
                  pl.BlockSpec((1, d), c2),
                  pl.BlockSpec((1, 1, d), per_b),
                  pl.BlockSpec((1, 1, d), per_b),
                  pl.BlockSpec((d, d), c2),
                  pl.BlockSpec((N_EXPERTS, d), c2),
                  pl.BlockSpec((N_EXPERTS, 1), c2)],
        out_specs=[pl.BlockSpec((1, tm, d), tok),
                   pl.BlockSpec((tm, d), lambda b, i: (b * nt + i, 0)),
                   pl.BlockSpec((SUBLANES, tm), flat),
                   pl.BlockSpec((SUBLANES, tm), flat)],
        out_shape=[jax.ShapeDtypeStruct((bsz, s, d), F32),
                   jax.ShapeDtypeStruct((bsz * s, d), F32),
                   jax.ShapeDtypeStruct((SUBLANES, bsz * s), jnp.int32),
                   jax.ShapeDtypeStruct((SUBLANES, bsz * s), F32)],
        compiler_params=_cparams(("arbitrary", "arbitrary")),
        name="out_proj",
    )(y_da, y_ml, y_s5, x, r3(gate), ln_g.reshape(1, d), ln_b.reshape(1, d), r3(shift2), r3(scale2),
      w_out_l.astype(BF16), w_router_l.T, b_router_l.reshape(N_EXPERTS, 1))


META_TE, META_NT, META_END, META_PAD, META_TV = 0, 1, 2, 3, 4


def _route_kernel(eid_ref, pos_ref, meta_ref, carry_s, start_s):
    phase = pl.program_id(0)
    i = pl.program_id(1)
    tb = eid_ref.shape[1]
    ntp = meta_ref.shape[1]
    tm = TM_MOE

    @pl.when(jnp.logical_and(phase == 0, i == 0))
    def _():
        carry_s[...] = jnp.zeros_like(carry_s)

    eid = eid_ref[...]
    eidx = lax.broadcasted_iota(jnp.int32, (N_EXPERTS, tb), 0)
    hot = [eidx == eid[k:k + 1, :] for k in range(TOP_K)]
    member = jnp.zeros((N_EXPERTS, tb), F32)
    for k in range(TOP_K):
        member = member + hot[k].astype(F32)
    total = jnp.sum(member, axis=1, keepdims=True)

    @pl.when(phase == 0)
    def _():
        carry_s[...] = carry_s[...] + total

    @pl.when(jnp.logical_and(phase == 1, i == 0))
    def _():
        cnt = carry_s[...]
        padded = jnp.floor((cnt + (tm - 1)) * (1.0 / tm)) * tm
        er = lax.broadcasted_iota(jnp.int32, (N_EXPERTS, N_EXPERTS), 0)
        ec = lax.broadcasted_iota(jnp.int32, (N_EXPERTS, N_EXPERTS), 1)
        ends = jnp.dot((ec <= er).astype(F32), padded, preferred_element_type=F32, precision=HIGHEST)
        start_s[...] = ends - padded
        carry_s[...] = jnp.zeros_like(carry_s)
        end_c = ends[:, 0:1]
        grand = end_c[N_EXPERTS - 1:N_EXPERTS, :]
        tile0 = lax.broadcasted_iota(jnp.int32, (1, ntp), 1).astype(F32) * tm
        te = jnp.sum((end_c <= tile0).astype(F32), axis=0, keepdims=True)
        te_last = jnp.sum((end_c <= grand - tm).astype(F32), axis=0, keepdims=True)
        te = jnp.where(tile0 < grand, te, te_last)
        lane = lax.broadcasted_iota(jnp.int32, (N_EXPERTS, ntp), 1)
        sub = lax.broadcasted_iota(jnp.int32, (N_EXPERTS, ntp), 0)
        diag = lane == sub
        end_r = jnp.sum(jnp.where(diag, end_c, 0.0), axis=0, keepdims=True)
        pad_r = jnp.sum(jnp.where(diag, padded[:, 0:1], 0.0), axis=0, keepdims=True)
        nt_r = jnp.broadcast_to(grand * (1.0 / tm), (1, ntp))
        used_c = end_c - padded[:, 0:1] + cnt[:, 0:1]
        mine = jnp.logical_and(end_c - padded[:, 0:1] <= tile0, tile0 < end_c)
        tv = jnp.sum(jnp.where(mine, jnp.clip(used_c - tile0, 0.0, float(tm)), 0.0), axis=0, keepdims=True)
        zero = jnp.zeros((SUBLANES - 5, ntp), F32)
        meta_ref[...] = jnp.concatenate([te, nt_r, end_r, pad_r, tv, zero], axis=0).astype(jnp.int32)

    @pl.when(phase == 1)
    def _():
        ri = lax.broadcasted_iota(jnp.int32, (tb, tb), 0)
        ci = lax.broadcasted_iota(jnp.int32, (tb, tb), 1)
        triu = (ri <= ci).astype(BF16)
        incl = jnp.dot(member.astype(BF16), triu, preferred_element_type=F32)
        slot = incl - member + carry_s[:, 0:1] + start_s[:, 0:1]
        rows = [jnp.sum(jnp.where(hot[k], slot, 0.0), axis=0, keepdims=True) for k in range(TOP_K)]
        zr = jnp.zeros_like(rows[0])
        pos_ref[...] = jnp.concatenate(rows + [zr] * (SUBLANES - TOP_K), axis=0).astype(jnp.int32)
        carry_s[...] = carry_s[...] + total


def _route(eid, n_tiles_max):
    t = eid.shape[1]
    tb = TB_RANK
    ntp = ((n_tiles_max + LANES - 1) // LANES) * LANES
    pos8, meta = pl.pallas_call(
        _route_kernel,
        grid=(2, t // tb),
        in_specs=[pl.BlockSpec((SUBLANES, tb), lambda p, i: (0, i))],
        out_specs=[pl.BlockSpec((SUBLANES, tb), lambda p, i: (0, i * p)),
                   pl.BlockSpec((SUBLANES, ntp), lambda p, i: (0, 0))],
        out_shape=[jax.ShapeDtypeStruct((SUBLANES, t), jnp.int32),
                   jax.ShapeDtypeStruct((SUBLANES, ntp), jnp.int32)],
        scratch_shapes=[pltpu.VMEM((N_EXPERTS, LANES), F32), pltpu.VMEM((N_EXPERTS, LANES), F32)],
        compiler_params=_cparams(("arbitrary", "arbitrary")),
        name="route",
    )(eid)
    return pos8, meta


def _sc_workers():
    info = plsc.get_sparse_core_info()
    return info.num_cores, info.num_cores * info.num_subcores


def _dispatch(h2, pos8, n_rows):
    t, d = h2.shape
    n_cores, n_workers = _sc_workers()
    tpw = t // n_workers
    ch = SC_CHUNK
    mesh = plsc.VectorSubcoreMesh(core_axis_name="c", subcore_axis_name="s")

    @functools.partial(
        pl.kernel, mesh=mesh,
        out_type=jax.ShapeDtypeStruct((n_rows, d), h2.dtype),
        scratch_types=[pltpu.VMEM((ch,), jnp.int32)] * TOP_K + [pltpu.VMEM((ch, d), h2.dtype),
                                                                pltpu.SemaphoreType.DMA])
    def scatter_rows(h_hbm, pos_hbm, out_hbm, i0, i1, i2, i3, rows_v, sem):
        idx = (i0, i1, i2, i3)
        base = (lax.axis_index("s") * n_cores + lax.axis_index("c")) * tpw

        @pl.loop(0, tpw // ch)
        def _(i):
            off = base + i * ch
            pltpu.sync_copy(h_hbm.at[pl.ds(off, ch)], rows_v)
            for k in range(TOP_K):
                pltpu.sync_copy(pos_hbm.at[k, pl.ds(off, ch)], idx[k])
            copies = [pltpu.async_copy(rows_v, out_hbm.at[idx[k]], sem) for k in range(TOP_K)]
            for cp in copies:
                cp.wait()

    return scatter_rows(h2, pos8)


def _gather_expert_rows(ys, pos8):
    _, d = ys.shape
    t = pos8.shape[1]
    n_cores, n_workers = _sc_workers()
    tpw = t // n_workers
    ch = SC_CHUNK
    mesh = plsc.VectorSubcoreMesh(core_axis_name="c", subcore_axis_name="s")

    @functools.partial(
        pl.kernel, mesh=mesh,
        out_type=jax.ShapeDtypeStruct((TOP_K, t, d), ys.dtype),
        scratch_types=[pltpu.VMEM((ch,), jnp.int32), pltpu.VMEM((ch, d), ys.dtype), pltpu.SemaphoreType.DMA])
    def gather_rows(ys_hbm, pos_hbm, out_hbm, idx_v, rows_v, sem):
        base = (lax.axis_index("s") * n_cores + lax.axis_index("c")) * tpw

        @pl.loop(0, tpw // ch)
        def _(i):
            off = base + i * ch
            for k in range(TOP_K):
                pltpu.sync_copy(pos_hbm.at[k, pl.ds(off, ch)], idx_v)
                pltpu.async_copy(ys_hbm.at[idx_v], rows_v, sem).wait()
                pltpu.sync_copy(rows_v, out_hbm.at[k, pl.ds(off, ch)])

    return gather_rows(ys, pos8)


def _expert_kernel(te_ref, nt_ref, tv_ref, x_ref, wup_ref, bup_ref, wdn_ref, bdn_ref, o_ref, wup_s, wdn_s):
    j = pl.program_id(0)
    n_act = nt_ref[0]
    prev = te_ref[jnp.maximum(j - 1, 0)]
    fresh = jnp.logical_or(j == 0, te_ref[j] != prev)

    @pl.when(jnp.logical_and(j < n_act, fresh))
    def _():
        wup_s[...] = wup_ref[0, 0].astype(BF16)
        wdn_s[...] = wdn_ref[0, 0].astype(BF16)

    @pl.when(j < n_act)
    def _():
        row = lax.broadcasted_iota(jnp.int32, (x_ref.shape[0], 1), 0)
        x = jnp.where(row < tv_ref[j], x_ref[...], 0.0).astype(BF16)
        z = jnp.dot(x, wup_s[...], preferred_element_type=F32) + bup_ref[0, 0]
        glu = jnp.minimum(z[:, :D_EXPERT], SWIGLU_LIMIT)
        lin = jnp.clip(z[:, D_EXPERT:], -SWIGLU_LIMIT, SWIGLU_LIMIT)
        act = (glu * _sigmoid(SWIGLU_ALPHA * glu) * (lin + 1.0)).astype(BF16)
        o_ref[...] = jnp.dot(act, wdn_s[...], preferred_element_type=F32) + bdn_ref[0, 0]

    @pl.when(j >= n_act)
    def _():
        o_ref[...] = jnp.zeros_like(o_ref)


def _expert_mlp(xs, meta, layer, w_up, b_up, w_down, b_down):
    n_rows, d = xs.shape
    tm = TM_MOE
    nt_max = n_rows // tm
    f = w_up.shape[-1]
    b_up4 = b_up.reshape(DEPTH, N_EXPERTS, 1, f)
    b_dn4 = b_down.reshape(DEPTH, N_EXPERTS, 1, d)
    tile_expert, n_tiles, tile_valid = meta[META_TE, :nt_max], meta[META_NT, :1], meta[META_TV, :nt_max]
    row = lambda j, te, nt, tv: (jnp.minimum(j, nt[0] - 1), 0)
    wsel = lambda j, te, nt, tv: (layer, te[j], 0, 0)
    grid_spec = pltpu.PrefetchScalarGridSpec(
        num_scalar_prefetch=3,
        grid=(nt_max,),
        in_specs=[pl.BlockSpec((tm, d), row),
                  pl.BlockSpec((1, 1, d, f), wsel),
                  pl.BlockSpec((1, 1, 1, f), wsel),
                  pl.BlockSpec((1, 1, f // 2, d), wsel),
                  pl.BlockSpec((1, 1, 1, d), wsel)],
        out_specs=pl.BlockSpec((tm, d), lambda j, te, nt, tv: (j, 0)),
        scratch_shapes=[pltpu.VMEM((d, f), BF16), pltpu.VMEM((f // 2, d), BF16)],
    )
    return pl.pallas_call(
        _expert_kernel,
        grid_spec=grid_spec,
        out_shape=jax.ShapeDtypeStruct((n_rows, d), F32),
        compiler_params=_cparams(("arbitrary",)),
        name="expert_mlp",
    )(tile_expert, n_tiles, tile_valid, xs, w_up, b_up4, w_down, b_dn4)


def _combine_kernel(rows_ref, prob_ref, x_ref, gate_ref, lng_ref, lnb_ref, o_ref):
    p = prob_ref[...]
    y = p[:, 0:1] * rows_ref[0]
    for k in range(1, TOP_K):
        y = y + p[:, k:k + 1] * rows_ref[k]
    o_ref[0] = _layer_norm(DN_ALPHA * x_ref[0] + (1.0 + gate_ref[0]) * y, lng_ref[...], lnb_ref[...])


def _combine(rows, prob_c, x1, gate, ln_g, ln_b):
    bsz, s, d = x1.shape
    tm = TM_DISP
    nt = s // tm
    return pl.pallas_call(
        _combine_kernel,
        grid=(bsz, nt),
        in_specs=[pl.BlockSpec((TOP_K, tm, d), lambda b, i: (0, b * nt + i, 0)),
                  pl.BlockSpec((tm, LANES), lambda b, i: (b * nt + i, 0)),
                  pl.BlockSpec((1, tm, d), lambda b, i: (b, i, 0)),
                  pl.BlockSpec((1, 1, d), lambda b, i: (b, 0, 0)),
                  pl.BlockSpec((1, d), lambda b, i: (0, 0)),
                  pl.BlockSpec((1, d), lambda b, i: (0, 0))],
        out_specs=pl.BlockSpec((1, tm, d), lambda b, i: (b, i, 0)),
        out_shape=jax.ShapeDtypeStruct((bsz, s, d), F32),
        compiler_params=_cparams(("arbitrary", "arbitrary")),
        name="combine",
    )(rows, prob_c, x1, gate.reshape(bsz, 1, d), ln_g.reshape(1, d), ln_b.reshape(1, d))


def kernel(x, c, positions, ada_w, ada_b, w_in, lam_q1, lam_k1, lam_q2, lam_k2, da_norm_g, ml_conv_w, ml_conv_b,
           ml_w_q, ml_w_k, ml_gate_b, ml_norm_g, s5_a_re, s5_a_im, s5_log_dt, s5_b_re, s5_b_im, s5_c_re, s5_c_im,
           s5_d, s5_w_glu, w_out, ln_g, ln_b, w_router, b_router, w_up, b_up, w_down, b_down):
    bsz, s, d = x.shape
    t = bsz * s
    n_tiles_max = (t * TOP_K) // TM_MOE + N_EXPERTS
    n_rows = n_tiles_max * TM_MOE
    mod = _modulation(c, ada_w, ada_b)
    cos_t, sin_t = _rope_tables(positions)
    for l in range(DEPTH):
        shift, scale, gate = jnp.split(mod[2 * l], 3, axis=-1)
        q, k, v, mlx, mlv, mlo, g_t, g_c, s5u = _in_proj(x, shift, scale, cos_t, sin_t, w_in[l])
        lam_init = 0.8 - 0.6 * math.exp(-0.3 * l)
        lamv = jnp.stack([lam_q1[l], lam_k1[l], lam_q2[l], lam_k2[l]])
        y_da = _diff_attn(q, k, v, lamv, da_norm_g[l], lam_init)
        y_ml = _mlstm(mlx, mlv, mlo, g_t, g_c, ml_conv_w[l], ml_conv_b[l], ml_w_q[l], ml_w_k[l],
                      ml_gate_b[l], ml_norm_g[l])
        s5p = _s5_params(s5_a_re[l], s5_a_im[l], s5_log_dt[l], s5_b_re[l], s5_b_im[l], s5_c_re[l], s5_c_im[l],
                         s5_w_glu[l])
        y_s5 = _s5(s5u, bsz, s5p, s5_d[l])
        shift2, scale2, gate2 = jnp.split(mod[2 * l + 1], 3, axis=-1)
        x1, h2, eid, prob = _out_proj(y_da, y_ml, y_s5, x, gate, ln_g[l, 0], ln_b[l, 0], shift2, scale2,
                                      w_out[l], w_router[l], b_router[l])
        pos8, meta = _route(eid, n_tiles_max)
        xs = _dispatch(h2, pos8, n_rows)
        ys = _expert_mlp(xs, meta, l, w_up, b_up, w_down, b_down)
        rows = _gather_expert_rows(ys, pos8)
        prob_c = jnp.pad(prob[:TOP_K].T, ((0, 0), (0, LANES - TOP_K)))
        x = _combine(rows, prob_c, x1, gate2, ln_g[l, 1], ln_b[l, 1])
    return x
```

```python
import functools
import math

import jax
import jax.numpy as jnp
from jax import lax
from jax.experimental import pallas as pl
from jax.experimental.pallas import tpu as pltpu
from jax.experimental.pallas import tpu_sc as plsc

F32 = jnp.float32
BF16 = jnp.bfloat16
HIGHEST = lax.Precision.HIGHEST

D_MODEL = 1024
DEPTH = 2
DA_HEADS = 4
DA_HEAD_DIM = 64
DA_V_DIM = 2 * DA_HEAD_DIM
DA_WIDTH = DA_HEADS * DA_V_DIM
DA_QK_WIDTH = DA_HEADS * 2 * DA_HEAD_DIM
ROPE_THETA = 10000.0
ML_HEADS = 4
ML_HEAD_DIM = 64
ML_WIDTH = ML_HEADS * ML_HEAD_DIM
ML_CONV = 4
S5_GROUP = 16
S5_STATE = 64
S5_WIDTH = D_MODEL - DA_WIDTH - ML_WIDTH
S5_GROUPS = S5_WIDTH // S5_GROUP
S5_NSTATE = S5_GROUPS * S5_STATE
N_EXPERTS = 32
TOP_K = 4
D_EXPERT = D_MODEL
SWIGLU_LIMIT = 7.0
SWIGLU_ALPHA = 1.702
DN_ALPHA = (2 * DEPTH) ** 0.25
LN_EPS = 1e-5
NEG = -1e30

OFF_DA_K = DA_QK_WIDTH
OFF_DA_V = 2 * DA_QK_WIDTH
OFF_ML_X = OFF_DA_V + DA_WIDTH
OFF_ML_V = OFF_ML_X + ML_WIDTH
OFF_ML_O = OFF_ML_V + ML_WIDTH
OFF_ML_I = OFF_ML_O + ML_WIDTH
OFF_ML_F = OFF_ML_I + ML_HEADS
OFF_S5_U = OFF_ML_F + ML_HEADS
N_IN = OFF_S5_U + S5_WIDTH

LANES = 128
SUBLANES = 8
VMEM_LIMIT_BYTES = 56 * 1024 * 1024

TM_PROJ = 512
TQ = 512
ML_CHUNK = 256
S5_TC = 128
S5_UNROLL = 8
TB_RANK = 512
TM_MOE = 256
TM_DISP = 256
SC_CHUNK = 128
GATE_PAD = 8
VT_ROWS = DA_V_DIM + 16
Q_PRESCALE = DA_HEAD_DIM ** -0.5 * math.log2(math.e)


def _cparams(sem, vmem=VMEM_LIMIT_BYTES):
    return pltpu.CompilerParams(dimension_semantics=sem, vmem_limit_bytes=vmem)


def _sigmoid(x):
    return 1.0 / (1.0 + jnp.exp(-x))


def _mod_kernel(c_ref, w_ref, b_ref, o_ref):
    c = c_ref[...]
    ca = (c * _sigmoid(c)).astype(BF16)
    w = w_ref[0].astype(BF16)
    o_ref[0] = jnp.dot(ca, w, preferred_element_type=F32) + b_ref[0]


def _modulation(c, ada_w, ada_b):
    nmod = ada_w.shape[0] * ada_w.shape[1]
    bsz, d = c.shape
    e = ada_w.shape[-1]
    tn = 1024
    w = ada_w.reshape(nmod, d, e)
    b = ada_b.reshape(nmod, 1, e)
    return pl.pallas_call(
        _mod_kernel,
        grid=(nmod, e // tn),
        in_specs=[pl.BlockSpec((bsz, d), lambda n, j: (0, 0)),
                  pl.BlockSpec((1, d, tn), lambda n, j: (n, 0, j)),
                  pl.BlockSpec((1, 1, tn), lambda n, j: (n, 0, j))],
        out_specs=pl.BlockSpec((1, bsz, tn), lambda n, j: (n, 0, j)),
        out_shape=jax.ShapeDtypeStruct((nmod, bsz, e), F32),
        compiler_params=_cparams(("arbitrary", "arbitrary")),
        name="modulation",
    )(c, w, b)


def _rope_kernel(pos_ref, cos_ref, sin_ref):
    pos = pos_ref[0].astype(F32)
    lane = lax.broadcasted_iota(jnp.int32, (1, LANES), 1)
    fidx = (lane % (DA_HEAD_DIM // 2)).astype(F32)
    inv = jnp.exp(fidx * (-2.0 * math.log(ROPE_THETA) / DA_HEAD_DIM))
    ang = pos * inv
    sign = jnp.where((lane % DA_HEAD_DIM) < DA_HEAD_DIM // 2, -1.0, 1.0)
    cos_ref[0] = jnp.cos(ang)
    sin_ref[0] = jnp.sin(ang) * sign


def _rope_tables(positions):
    bsz, s = positions.shape
    ts = 512
    pos3 = positions.reshape(bsz, s, 1)
    return pl.pallas_call(
        _rope_kernel,
        grid=(bsz, s // ts),
        in_specs=[pl.BlockSpec((1, ts, 1), lambda b, i: (b, i, 0))],
        out_specs=[pl.BlockSpec((1, ts, LANES), lambda b, i: (b, i, 0))] * 2,
        out_shape=[jax.ShapeDtypeStruct((bsz, s, LANES), F32)] * 2,
        compiler_params=_cparams(("arbitrary", "arbitrary")),
        name="rope_tables",
    )(pos3)


def _in_proj_kernel(x_ref, shift_ref, scale_ref, cos_ref, sin_ref, wqk_ref, wvt_ref, wrest_ref, wgt_ref, wgc_ref,
                    q_ref, k_ref, vt_ref, mlx_ref, mlv_ref, mlo_ref, gt_ref, gc_ref, s5u_ref):
    h = (x_ref[0] * (1.0 + scale_ref[0]) + shift_ref[0]).astype(BF16)
    cos = cos_ref[0]
    sin = sin_ref[0]
    lane = lax.broadcasted_iota(jnp.int32, (1, LANES), 1)
    lo_half = (lane % DA_HEAD_DIM) < DA_HEAD_DIM // 2
    half = DA_HEAD_DIM // 2

    def rope(t):
        fwd = pltpu.roll(t, half, 1)
        bwd = pltpu.roll(t, LANES - half, 1)
        partner = jnp.where(lo_half, bwd, fwd)
        return t * cos + partner * sin

    qk = jnp.dot(h, wqk_ref[...], preferred_element_type=F32)
    nslab = DA_QK_WIDTH // LANES
    for c in range(nslab):
        q_ref[0, :, c * LANES:(c + 1) * LANES] = (
            rope(qk[:, c * LANES:(c + 1) * LANES]) * Q_PRESCALE).astype(BF16)
        k_ref[0, :, c * LANES:(c + 1) * LANES] = rope(
            qk[:, DA_QK_WIDTH + c * LANES:DA_QK_WIDTH + (c + 1) * LANES]).astype(BF16)

    vt = lax.dot_general(wvt_ref[...], h, (((1,), (1,)), ((), ())), preferred_element_type=F32)
    tm = h.shape[0]
    for hh in range(DA_HEADS):
        for jj in range(tm // TQ):
            vt_ref[0, hh, jj, 0:DA_V_DIM, :] = vt[hh * DA_V_DIM:(hh + 1) * DA_V_DIM,
                                                  jj * TQ:(jj + 1) * TQ].astype(BF16)
            vt_ref[0, hh, jj, DA_V_DIM:VT_ROWS, :] = jnp.ones((VT_ROWS - DA_V_DIM, TQ), BF16)

    r = jnp.dot(h, wrest_ref[...], preferred_element_type=F32)
    o = 0
    mlx_ref[0] = r[:, o:o + ML_WIDTH].astype(BF16); o += ML_WIDTH
    mlv_ref[0] = r[:, o:o + ML_WIDTH].astype(BF16); o += ML_WIDTH
    mlo_ref[0] = r[:, o:o + ML_WIDTH].astype(BF16); o += ML_WIDTH
    s5u_ref[...] = r[:, o:o + S5_WIDTH]
    gt_ref[0] = lax.dot_general(wgt_ref[...], h, (((1,), (1,)), ((), ())), preferred_element_type=F32)
    gc_ref[0] = jnp.dot(h, wgc_ref[...], preferred_element_type=F32)


def _in_proj(x, shift, scale, cos_t, sin_t, w_in_l):
    bsz, s, d = x.shape
    tm = TM_PROJ
    w = w_in_l.astype(BF16)
    wqk = w[:, :OFF_DA_V]
    wvt = w[:, OFF_DA_V:OFF_ML_X].T
    wrest = jnp.concatenate([w[:, OFF_ML_X:OFF_ML_I], w[:, OFF_S5_U:]], axis=1)
    wg = w[:, OFF_ML_I:OFF_S5_U]
    wgt = wg.T
    wgc = jnp.pad(wg, ((0, 0), (0, LANES - GATE_PAD)))
    nrest = wrest.shape[1]
    shift3 = shift.reshape(bsz, 1, d)
    scale3 = scale.reshape(bsz, 1, d)
    tok = lambda b, i: (b, i, 0)
    per_b = lambda b, i: (b, 0, 0)
    const2 = lambda b, i: (0, 0)
    out_shapes = [
        jax.ShapeDtypeStruct((bsz, s, DA_QK_WIDTH), BF16),
        jax.ShapeDtypeStruct((bsz, s, DA_QK_WIDTH), BF16),
        jax.ShapeDtypeStruct((bsz, DA_HEADS, s // TQ, VT_ROWS, TQ), BF16),
        jax.ShapeDtypeStruct((bsz, s, ML_WIDTH), BF16),
        jax.ShapeDtypeStruct((bsz, s, ML_WIDTH), BF16),
        jax.ShapeDtypeStruct((bsz, s, ML_WIDTH), BF16),
        jax.ShapeDtypeStruct((bsz, GATE_PAD, s), F32),
        jax.ShapeDtypeStruct((bsz, s, LANES), F32),
        jax.ShapeDtypeStruct((s, bsz * S5_WIDTH), F32),
    ]
    out_specs = [
        pl.BlockSpec((1, tm, DA_QK_WIDTH), tok),
        pl.BlockSpec((1, tm, DA_QK_WIDTH), tok),
        pl.BlockSpec((1, DA_HEADS, tm // TQ, VT_ROWS, TQ), lambda b, i: (b, 0, i, 0, 0)),
        pl.BlockSpec((1, tm, ML_WIDTH), tok),
        pl.BlockSpec((1, tm, ML_WIDTH), tok),
        pl.BlockSpec((1, tm, ML_WIDTH), tok),
        pl.BlockSpec((1, GATE_PAD, tm), lambda b, i: (b, 0, i)),
        pl.BlockSpec((1, tm, LANES), tok),
        pl.BlockSpec((tm, S5_WIDTH), lambda b, i: (i, b)),
    ]
    return pl.pallas_call(
        _in_proj_kernel,
        grid=(bsz, s // tm),
        in_specs=[pl.BlockSpec((1, tm, d), tok),
                  pl.BlockSpec((1, 1, d), per_b),
                  pl.BlockSpec((1, 1, d), per_b),
                  pl.BlockSpec((1, tm, LANES), tok),
                  pl.BlockSpec((1, tm, LANES), tok),
                  pl.BlockSpec((d, OFF_DA_V), const2),
                  pl.BlockSpec((DA_WIDTH, d), const2),
                  pl.BlockSpec((d, nrest), const2),
                  pl.BlockSpec((GATE_PAD, d), const2),
                  pl.BlockSpec((d, LANES), const2)],
        out_specs=out_specs,
        out_shape=out_shapes,
        compiler_params=_cparams(("arbitrary", "arbitrary")),
        name="in_proj",
    )(x, shift3, scale3, cos_t, sin_t, wqk, wvt, wrest, wgt, wgc)


def _diff_attn_kernel(lam_init, lamv_ref, gain_ref, q_ref, k_ref, vt_ref, o_ref, acc_s, m_s):
    qi = pl.program_id(2)
    tq = q_ref.shape[1]
    lane = lax.broadcasted_iota(jnp.int32, (1, LANES), 1)
    first = lane < DA_HEAD_DIM
    q = q_ref[0]
    zero = jnp.zeros_like(q)
    qm = (jnp.where(first, q, zero), jnp.where(first, zero, q))
    acc_s[...] = jnp.zeros_like(acc_s)
    m_s[...] = jnp.full(m_s.shape, NEG, F32)

    def step(j, masked):
        kb = k_ref[0, pl.ds(pl.multiple_of(j * tq, tq), tq), :]
        vtb = vt_ref[0, 0, j]
        for c in range(2):
            st = lax.dot_general(kb, qm[c], (((1,), (1,)), ((), ())), preferred_element_type=F32)
            if masked:
                key_i = lax.broadcasted_iota(jnp.int32, (tq, tq), 0)
                qry_i = lax.broadcasted_iota(jnp.int32, (tq, tq), 1)
                st = jnp.where(key_i <= qry_i, st, NEG)
            m_prev = m_s[c]
            m_new = jnp.maximum(m_prev, jnp.max(st, axis=0, keepdims=True))
            alpha = jnp.exp2(m_prev - m_new)
            p = jnp.exp2(st - m_new).astype(BF16)
            acc_s[c] = alpha * acc_s[c] + jnp.dot(vtb, p, preferred_element_type=F32)
            m_s[c] = m_new

    def body(j, carry):
        step(j, False)
        return carry

    lax.fori_loop(0, qi, body, 0)
    step(qi, True)
    outs = []
    for c in range(2):
        acc = acc_s[c]
        outs.append(acc[:DA_V_DIM] / acc[DA_V_DIM:DA_V_DIM + 1])

    lamv = lamv_ref[...]
    lam = (jnp.exp(jnp.sum(lamv[0:1] * lamv[1:2], axis=1, keepdims=True))
           - jnp.exp(jnp.sum(lamv[2:3] * lamv[3:4], axis=1, keepdims=True)) + lam_init)
    ot = outs[0] - lam * outs[1]
    ms = jnp.mean(ot * ot, axis=0, keepdims=True)
    ot = ot * (lax.rsqrt(ms + LN_EPS) * (1.0 - lam_init))
    o_ref[0] = (ot.T * gain_ref[...]).astype(o_ref.dtype)


def _diff_attn(q, k, vt, lamv, gain, lam_init):
    bsz, s, _ = q.shape
    tq = TQ
    nq = s // tq
    return pl.pallas_call(
        functools.partial(_diff_attn_kernel, lam_init),
        grid=(bsz, DA_HEADS, nq),
        in_specs=[pl.BlockSpec((4, DA_HEAD_DIM), lambda b, h, i: (0, 0)),
                  pl.BlockSpec((1, DA_V_DIM), lambda b, h, i: (0, 0)),
                  pl.BlockSpec((1, tq, DA_V_DIM), lambda b, h, i: (b, i, h)),
                  pl.BlockSpec((1, s, DA_V_DIM), lambda b, h, i: (b, 0, h)),
                  pl.BlockSpec((1, 1, nq, VT_ROWS, tq), lambda b, h, i: (b, h, 0, 0, 0))],
        out_specs=pl.BlockSpec((1, tq, DA_V_DIM), lambda b, h, i: (b, i, h)),
        out_shape=jax.ShapeDtypeStruct((bsz, s, DA_WIDTH), BF16),
        scratch_shapes=[pltpu.VMEM((2, VT_ROWS, tq), F32), pltpu.VMEM((2, 1, tq), F32)],
        compiler_params=_cparams(("arbitrary", "arbitrary", "arbitrary")),
        name="diff_attn",
    )(lamv, gain.reshape(1, DA_V_DIM), q, k, vt)


def _log_sigmoid(x):
    return jnp.minimum(x, 0.0) - jnp.log(1.0 + jnp.exp(-jnp.abs(x)))


def _mlstm_kernel(x_ref, v_ref, o_ref, gt_ref, gc_ref, cw_ref, cb_ref, wq_ref, wkt_ref, gbt_ref, gbc_ref,
                  ng_ref, hmean_ref, y_ref, xc_s, c_s, m_s):
    s = x_ref.shape[1]
    L = ML_CHUNK
    H, dh = ML_HEADS, ML_HEAD_DIM
    nc = s // L
    x = x_ref[0].astype(F32)
    cw = cw_ref[...]
    row = lax.broadcasted_iota(jnp.int32, (s, 1), 0)
    xc = x * cw[ML_CONV - 1:ML_CONV]
    for j in range(1, ML_CONV):
        xs = jnp.where(row >= j, pltpu.roll(x, j, 0), 0.0)
        xc = xc + xs * cw[ML_CONV - 1 - j:ML_CONV - j]
    xc = xc + cb_ref[...]
    xc_s[...] = (xc * _sigmoid(xc)).astype(BF16)

    c_s[...] = jnp.zeros_like(c_s)
    m_s[...] = jnp.full(m_s.shape, NEG, F32)

    ri = lax.broadcasted_iota(jnp.int32, (L, L), 0)
    ci = lax.broadcasted_iota(jnp.int32, (L, L), 1)
    causal = ci <= ri
    tril = causal.astype(F32)
    triu = (ri <= ci).astype(F32)
    lane = lax.broadcasted_iota(jnp.int32, (1, dh), 1)
    one_hot0 = jnp.broadcast_to((lane == 0).astype(BF16), (L, dh))

    def chunk(ci_, _):
        t0 = pl.multiple_of(ci_ * L, L)
        xcc = xc_s[pl.ds(t0, L), :]
        qc = jnp.dot(xcc, wq_ref[...], preferred_element_type=F32).astype(BF16)
        ktc = lax.dot_general(wkt_ref[...], xcc, (((1,), (1,)), ((), ())),
                              preferred_element_type=F32)
        g_rows = gt_ref[0, ci_] + gbt_ref[...]
        g_cols = gc_ref[0, pl.ds(t0, L), :] + gbc_ref[...]
        lf_rows = _log_sigmoid(g_rows)
        lf_cols = _log_sigmoid(g_cols)
        b_rows = jnp.dot(lf_rows, triu, preferred_element_type=F32, precision=HIGHEST)
        b_cols = jnp.dot(tril, lf_cols, preferred_element_type=F32, precision=HIGHEST)
        vch = v_ref[0, pl.ds(t0, L), :]
        och = o_ref[0, pl.ds(t0, L), :].astype(F32)
        hs = []
        for h in range(H):
            br = b_rows[H + h:H + h + 1, :]
            ir = g_rows[h:h + 1, :]
            bc = b_cols[:, H + h:H + h + 1]
            m_prev = m_s[h]
            log_d = jnp.where(causal, bc - br + ir, NEG)
            inter = bc + m_prev
            mx = jnp.maximum(inter, jnp.max(log_d, axis=1, keepdims=True))
            dmat = jnp.exp(log_d - mx)
            dec = jnp.exp(inter - mx)
            qh = qc[:, h * dh:(h + 1) * dh]
            kth = ktc[h * dh:(h + 1) * dh, :]
            vaug = jnp.concatenate([vch[:, h * dh:(h + 1) * dh], one_hot0], axis=1)
            sm = (jnp.dot(qh, kth.astype(BF16), preferred_element_type=F32) * dmat).astype(BF16)
            c_prev = c_s[h]
            na = (jnp.dot(sm, vaug, preferred_element_type=F32)
                  + dec * jnp.dot(qh, c_prev.astype(BF16), preferred_element_type=F32))
            den = na[:, dh:dh + 1]
            hs.append(na[:, :dh] / jnp.maximum(jnp.abs(den), jnp.exp(-mx)))
            g_tot = br[:, L - 1:L]
            a_row = g_tot - br + ir
            m_new = jnp.maximum(g_tot + m_prev, jnp.max(a_row, axis=1, keepdims=True))
            decay = jnp.exp(g_tot + m_prev - m_new)
            w_row = jnp.exp(a_row - m_new)
            kw = (kth * w_row).astype(BF16)
            c_s[h] = decay * c_prev + jnp.dot(kw, vaug, preferred_element_type=F32)
            m_s[h] = m_new
        hcat = jnp.concatenate(hs, axis=1)
        ms = jnp.dot(hcat * hcat, hmean_ref[...], preferred_element_type=F32, precision=HIGHEST)
        y = hcat * lax.rsqrt(ms + LN_EPS) * ng_ref[...] * _sigmoid(och)
        y_ref[0, pl.ds(t0, L), :] = y.astype(y_ref.dtype)
        return 0

    lax.fori_loop(0, nc, chunk, 0)


def _mlstm(mlx, mlv, mlo, g_t, g_c, conv_w, conv_b, w_q, w_k, gate_b, norm_g):
    bsz, s, _ = mlx.shape
    H, dh = ML_HEADS, ML_HEAD_DIM
    eye = jnp.eye(H, dtype=F32)
    wq_bd = jnp.einsum('hde,hg->hdge', w_q, eye).reshape(ML_WIDTH, ML_WIDTH).astype(BF16)
    wk_bd = jnp.einsum('hde,hg->hdge', w_k * (dh ** -0.5), eye).reshape(ML_WIDTH, ML_WIDTH)
    wkt_bd = wk_bd.T.astype(BF16)
    gbt = gate_b.reshape(GATE_PAD, 1)
    gbc = jnp.pad(gate_b.reshape(1, GATE_PAD), ((0, 0), (0, LANES - GATE_PAD)))
    hmean = jnp.kron(eye, jnp.full((dh, dh), 1.0 / dh, F32))
    nc = s // ML_CHUNK
    g_t4 = g_t.reshape(bsz, GATE_PAD, nc, ML_CHUNK).transpose(0, 2, 1, 3)
    tok = lambda b: (b, 0, 0)
    c2 = lambda b: (0, 0)
    return pl.pallas_call(
        _mlstm_kernel,
        grid=(bsz,),
        in_specs=[pl.BlockSpec((1, s, ML_WIDTH), tok),
                  pl.BlockSpec((1, s, ML_WIDTH), tok),
                  pl.BlockSpec((1, s, ML_WIDTH), tok),
                  pl.BlockSpec((1, nc, GATE_PAD, ML_CHUNK), lambda b: (b, 0, 0, 0)),
                  pl.BlockSpec((1, s, LANES), tok),
                  pl.BlockSpec((ML_CONV, ML_WIDTH), c2),
                  pl.BlockSpec((1, ML_WIDTH), c2),
                  pl.BlockSpec((ML_WIDTH, ML_WIDTH), c2),
                  pl.BlockSpec((ML_WIDTH, ML_WIDTH), c2),
                  pl.BlockSpec((GATE_PAD, 1), c2),
                  pl.BlockSpec((1, LANES), c2),
                  pl.BlockSpec((1, ML_WIDTH), c2),
                  pl.BlockSpec((ML_WIDTH, ML_WIDTH), c2)],
        out_specs=pl.BlockSpec((1, s, ML_WIDTH), tok),
        out_shape=jax.ShapeDtypeStruct((bsz, s, ML_WIDTH), BF16),
        scratch_shapes=[pltpu.VMEM((s, ML_WIDTH), BF16),
                        pltpu.VMEM((H, dh, LANES), F32),
                        pltpu.VMEM((H, 1, 1), F32)],
        compiler_params=_cparams(("arbitrary",)),
        name="mlstm",
    )(mlx, mlv, mlo, g_t4, g_c, conv_w, conv_b.reshape(1, ML_WIDTH), wq_bd, wkt_bd, gbt, gbc,
      norm_g.reshape(1, ML_WIDTH), hmean)


def _gelu_tanh(x):
    return 0.5 * x * (1.0 + jnp.tanh(math.sqrt(2.0 / math.pi) * (x + 0.044715 * (x * x * x))))


def _s5_kernel(u_ref, are_ref, aim_ref, bcat_ref, ccat_ref, d_ref, wglu_ref, y_ref, xs_s, st_s):
    tc, bsz, w = u_ref.shape
    n = S5_NSTATE

    @pl.when(pl.program_id(0) == 0)
    def _():
        st_s[...] = jnp.zeros_like(st_s)

    u = u_ref[...].reshape(tc * bsz, w)
    xs_s[...] = jnp.dot(u.astype(BF16), bcat_ref[...], preferred_element_type=F32).reshape(tc, bsz, 2 * n)
    a_re = jnp.broadcast_to(are_ref[...], (bsz, n))
    a_im = jnp.broadcast_to(aim_ref[...], (bsz, n))

    def step(t, carry):
        x_re, x_im = carry
        bu = xs_s[t]
        n_re = a_re * x_re - a_im * x_im + bu[:, :n]
        n_im = a_re * x_im + a_im * x_re + bu[:, n:]
        xs_s[t] = jnp.concatenate([n_re, n_im], axis=1)
        return n_re, n_im

    x_re, x_im = lax.fori_loop(0, tc, step, (st_s[0], st_s[1]), unroll=S5_UNROLL)
    st_s[0] = x_re
    st_s[1] = x_im

    xs = xs_s[...].reshape(tc * bsz, 2 * n).astype(BF16)
    y = jnp.dot(xs, ccat_ref[...], preferred_element_type=F32) + d_ref[...] * u
    z = jnp.dot(_gelu_tanh(y).astype(BF16), wglu_ref[...], preferred_element_type=F32)
    out = z[:, :w] * _sigmoid(z[:, w:])
    y_ref[...] = out.reshape(tc, bsz, w).astype(y_ref.dtype)


def _s5_params(a_re, a_im, log_dt, b_re, b_im, c_re, c_im, w_glu):
    G, P, Hc = S5_GROUPS, S5_STATE, S5_GROUP
    dt = jnp.exp(log_dt)[:, None]
    mag = jnp.exp(a_re * dt)
    ab_re = mag * jnp.cos(a_im * dt)
    ab_im = mag * jnp.sin(a_im * dt)
    nr, ni = ab_re - 1.0, ab_im
    den = a_re * a_re + a_im * a_im
    fr = (nr * a_re + ni * a_im) / den
    fi = (ni * a_re - nr * a_im) / den
    bb_re = fr[..., None] * b_re - fi[..., None] * b_im
    bb_im = fr[..., None] * b_im + fi[..., None] * b_re
    eye = jnp.eye(G, dtype=F32)
    bd = lambda t, sub: jnp.einsum(sub, t, eye)
    bre = bd(bb_re, 'gph,gk->ghkp').reshape(G * Hc, G * P)
    bim = bd(bb_im, 'gph,gk->ghkp').reshape(G * Hc, G * P)
    bcat = jnp.concatenate([bre, bim], axis=1).astype(BF16)
    cre = bd(c_re, 'ghp,gk->gpkh').reshape(G * P, G * Hc)
    cim = bd(c_im, 'ghp,gk->gpkh').reshape(G * P, G * Hc)
    ccat = jnp.concatenate([cre, -cim], axis=0).astype(BF16)
    wv = bd(w_glu[:, :, :Hc], 'ghj,gk->ghkj').reshape(G * Hc, G * Hc)
    wg = bd(w_glu[:, :, Hc:], 'ghj,gk->ghkj').reshape(G * Hc, G * Hc)
    wglu = jnp.concatenate([wv, wg], axis=1).astype(BF16)
    return ab_re.reshape(1, G * P), ab_im.reshape(1, G * P), bcat, ccat, wglu


def _s5(u_tm, bsz, params, d_skip):
    s = u_tm.shape[0]
    w = S5_WIDTH
    n = S5_NSTATE
    are, aim, bcat, ccat, wglu = params
    u3 = u_tm.reshape(s, bsz, w)
    tc = S5_TC
    c2 = lambda i: (0, 0)
    y = pl.pallas_call(
        _s5_kernel,
        grid=(s // tc,),
        in_specs=[pl.BlockSpec((tc, bsz, w), lambda i: (i, 0, 0)),
                  pl.BlockSpec((1, n), c2),
                  pl.BlockSpec((1, n), c2),
                  pl.BlockSpec((w, 2 * n), c2),
                  pl.BlockSpec((2 * n, w), c2),
                  pl.BlockSpec((1, w), c2),
                  pl.BlockSpec((w, 2 * w), c2)],
        out_specs=pl.BlockSpec((tc, bsz, w), lambda i: (i, 0, 0)),
        out_shape=jax.ShapeDtypeStruct((s, bsz, w), F32),
        scratch_shapes=[pltpu.VMEM((tc, bsz, 2 * n), F32),
                        pltpu.VMEM((2, bsz, n), F32)],
        compiler_params=_cparams(("arbitrary",)),
        name="s5",
    )(u3, are, aim, bcat, ccat, d_skip.reshape(1, w), wglu)
    return y.reshape(s, bsz * w)


def _layer_norm(z, g, b):
    mu = jnp.mean(z, axis=1, keepdims=True)
    zc = z - mu
    var = jnp.mean(zc * zc, axis=1, keepdims=True)
    return zc * lax.rsqrt(var + LN_EPS) * g + b


def _out_proj_kernel(yda_ref, yml_ref, ys5_ref, x_ref, gate_ref, lng_ref, lnb_ref, shift_ref, scale_ref,
                     wout_ref, wrt_ref, brt_ref, x1_ref, h2_ref, eid_ref, prob_ref):
    y = jnp.dot(yda_ref[0], wout_ref[0:DA_WIDTH, :], preferred_element_type=F32)
    y = y + jnp.dot(yml_ref[0], wout_ref[DA_WIDTH:DA_WIDTH + ML_WIDTH, :], preferred_element_type=F32)
    y = y + jnp.dot(ys5_ref[...].astype(BF16), wout_ref[DA_WIDTH + ML_WIDTH:, :], preferred_element_type=F32)
    x1 = _layer_norm(DN_ALPHA * x_ref[0] + (1.0 + gate_ref[0]) * y, lng_ref[...], lnb_ref[...])
    x1_ref[0] = x1
    h2 = x1 * (1.0 + scale_ref[0]) + shift_ref[0]
    h2_ref[...] = _pack_bf16_pairs(h2)
    logits = lax.dot_general(wrt_ref[...], h2, (((1,), (1,)), ((), ())), preferred_element_type=F32,
                             precision=HIGHEST) + brt_ref[...]
    eidx = lax.broadcasted_iota(jnp.int32, logits.shape, 0)
    vals, ids = [], []
    for _ in range(TOP_K):
        mx = jnp.max(logits, axis=0, keepdims=True)
        sel = jnp.min(jnp.where(logits == mx, eidx, N_EXPERTS), axis=0, keepdims=True)
        vals.append(mx)
        ids.append(sel)
        logits = jnp.where(eidx == sel, -jnp.inf, logits)
    ex = [jnp.exp(v - vals[0]) for v in vals]
    tot = ex[0] + ex[1] + ex[2] + ex[3]
    zi = jnp.zeros_like(ids[0])
    zf = jnp.zeros_like(tot)
    eid_ref[...] = jnp.concatenate(ids + [zi] * (SUBLANES - TOP_K), axis=0)
    prob_ref[...] = jnp.concatenate([e / tot for e in ex] + [zf] * (SUBLANES - TOP_K), axis=0)


def _out_proj(y_da, y_ml, y_s5, x, gate, ln_g, ln_b, shift2, scale2, w_out_l, w_router_l, b_router_l):
    bsz, s, d = x.shape
    tm = TM_PROJ
    nt = s // tm
    tok = lambda b, i: (b, i, 0)
    per_b = lambda b, i: (b, 0, 0)
    c2 = lambda b, i: (0, 0)
    r3 = lambda a: a.reshape(bsz, 1, d)
    flat = lambda b, i: (0, b * nt + i)
    return pl.pallas_call(
        _out_proj_kernel,
        grid=(bsz, nt),
        in_specs=[pl.BlockSpec((1, tm, DA_WIDTH), tok),
                  pl.BlockSpec((1, tm, ML_WIDTH), tok),
                  pl.BlockSpec((tm, S5_WIDTH), lambda b, i: (i, b)),
                  pl.BlockSpec((1, tm, d), tok),
                  pl.BlockSpec((1, 1, d), per_b),
                  pl.BlockSpec((1, d), c2),
                  pl.BlockSpec((1, d), c2),
                  pl.BlockSpec((1, 1, d), per_b),
                  pl.BlockSpec((1, 1, d), per_b),
                  pl.BlockSpec((d, d), c2),
                  pl.BlockSpec((N_EXPERTS, d), c2),
                  pl.BlockSpec((N_EXPERTS, 1), c2)],
        out_specs=[pl.BlockSpec((1, tm, d), tok),
                   pl.BlockSpec((tm, d // 2), lambda b, i: (b * nt + i, 0)),
                   pl.BlockSpec((SUBLANES, tm), flat),
                   pl.BlockSpec((SUBLANES, tm), flat)],
        out_shape=[jax.ShapeDtypeStruct((bsz, s, d), F32),
                   jax.ShapeDtypeStruct((bsz * s, d // 2), jnp.int32),
                   jax.ShapeDtypeStruct((SUBLANES, bsz * s), jnp.int32),
                   jax.ShapeDtypeStruct((SUBLANES, bsz * s), F32)],
        compiler_params=_cparams(("arbitrary", "arbitrary")),
        name="out_proj",
    )(y_da, y_ml, y_s5, x, r3(gate), ln_g.reshape(1, d), ln_b.reshape(1, d), r3(shift2), r3(scale2),
      w_out_l.astype(BF16), w_router_l.T, b_router_l.reshape(N_EXPERTS, 1))


META_END, META_PAD, META_CNT = 0, 1, 2


def _route_kernel(eid_ref, pos_ref, meta_ref, carry_s, start_s):
    phase = pl.program_id(0)
    i = pl.program_id(1)
    tb = eid_ref.shape[1]
    ntp = meta_ref.shape[1]
    tm = TM_MOE

    @pl.when(jnp.logical_and(phase == 0, i == 0))
    def _():
        carry_s[...] = jnp.zeros_like(carry_s)

    eid = eid_ref[...]
    eidx = lax.broadcasted_iota(jnp.int32, (N_EXPERTS, tb), 0)
    hot = [eidx == eid[k:k + 1, :] for k in range(TOP_K)]
    member = jnp.zeros((N_EXPERTS, tb), F32)
    for k in range(TOP_K):
        member = member + hot[k].astype(F32)
    total = jnp.sum(member, axis=1, keepdims=True)

    @pl.when(phase == 0)
    def _():
        carry_s[...] = carry_s[...] + total

    @pl.when(jnp.logical_and(phase == 1, i == 0))
    def _():
        cnt = carry_s[...]
        padded = jnp.floor((cnt + (tm - 1)) * (1.0 / tm)) * tm
        er = lax.broadcasted_iota(jnp.int32, (N_EXPERTS, N_EXPERTS), 0)
        ec = lax.broadcasted_iota(jnp.int32, (N_EXPERTS, N_EXPERTS), 1)
        ends = jnp.dot((ec <= er).astype(F32), padded, preferred_element_type=F32, precision=HIGHEST)
        start_s[...] = ends - padded
        carry_s[...] = jnp.zeros_like(carry_s)
        lane = lax.broadcasted_iota(jnp.int32, (N_EXPERTS, ntp), 1)
        sub = lax.broadcasted_iota(jnp.int32, (N_EXPERTS, ntp), 0)
        diag = lane == sub

        def as_row(col):
            return jnp.sum(jnp.where(diag, col, 0.0), axis=0, keepdims=True)

        zero = jnp.zeros((SUBLANES - 3, ntp), F32)
        meta_ref[...] = jnp.concatenate([as_row(ends[:, 0:1]), as_row(padded[:, 0:1]), as_row(cnt[:, 0:1]), zero],
                                        axis=0).astype(jnp.int32)

    @pl.when(phase == 1)
    def _():
        ri = lax.broadcasted_iota(jnp.int32, (tb, tb), 0)
        ci = lax.broadcasted_iota(jnp.int32, (tb, tb), 1)
        triu = (ri <= ci).astype(BF16)
        incl = jnp.dot(member.astype(BF16), triu, preferred_element_type=F32)
        slot = incl - member + carry_s[:, 0:1] + start_s[:, 0:1]
        rows = [jnp.sum(jnp.where(hot[k], slot, 0.0), axis=0, keepdims=True) for k in range(TOP_K)]
        zr = jnp.zeros_like(rows[0])
        pos_ref[...] = jnp.concatenate(rows + [zr] * (SUBLANES - TOP_K), axis=0).astype(jnp.int32)
        carry_s[...] = carry_s[...] + total


def _route(eid):
    t = eid.shape[1]
    tb = TB_RANK
    ntp = LANES
    pos8, meta = pl.pallas_call(
        _route_kernel,
        grid=(2, t // tb),
        in_specs=[pl.BlockSpec((SUBLANES, tb), lambda p, i: (0, i))],
        out_specs=[pl.BlockSpec((SUBLANES, tb), lambda p, i: (0, i * p)),
                   pl.BlockSpec((SUBLANES, ntp), lambda p, i: (0, 0))],
        out_shape=[jax.ShapeDtypeStruct((SUBLANES, t), jnp.int32),
                   jax.ShapeDtypeStruct((SUBLANES, ntp), jnp.int32)],
        scratch_shapes=[pltpu.VMEM((N_EXPERTS, LANES), F32), pltpu.VMEM((N_EXPERTS, LANES), F32)],
        compiler_params=_cparams(("arbitrary", "arbitrary")),
        name="route",
    )(eid)
    return pos8, meta


def _sc_workers():
    info = plsc.get_sparse_core_info()
    return info.num_cores, info.num_cores * info.num_subcores


def _dispatch(h2, pos8, n_rows):
    t, d = h2.shape
    n_cores, n_workers = _sc_workers()
    tpw = t // n_workers
    ch = SC_CHUNK
    mesh = plsc.VectorSubcoreMesh(core_axis_name="c", subcore_axis_name="s")

    @functools.partial(
        pl.kernel, mesh=mesh,
        out_type=jax.ShapeDtypeStruct((n_rows, d), h2.dtype),
        scratch_types=[pltpu.VMEM((ch,), jnp.int32)] * TOP_K + [pltpu.VMEM((ch, d), h2.dtype),
                                                                pltpu.SemaphoreType.DMA])
    def scatter_rows(h_hbm, pos_hbm, out_hbm, i0, i1, i2, i3, rows_v, sem):
        idx = (i0, i1, i2, i3)
        base = (lax.axis_index("s") * n_cores + lax.axis_index("c")) * tpw

        @pl.loop(0, tpw // ch)
        def _(i):
            off = base + i * ch
            pltpu.sync_copy(h_hbm.at[pl.ds(off, ch)], rows_v)
            for k in range(TOP_K):
                pltpu.sync_copy(pos_hbm.at[k, pl.ds(off, ch)], idx[k])
            copies = [pltpu.async_copy(rows_v, out_hbm.at[idx[k]], sem) for k in range(TOP_K)]
            for cp in copies:
                cp.wait()

    return scatter_rows(h2, pos8)


def _gather_expert_rows(ys, pos8):
    _, d = ys.shape
    t = pos8.shape[1]
    n_cores, n_workers = _sc_workers()
    tpw = t // n_workers
    ch = SC_CHUNK
    mesh = plsc.VectorSubcoreMesh(core_axis_name="c", subcore_axis_name="s")

    @functools.partial(
        pl.kernel, mesh=mesh,
        out_type=jax.ShapeDtypeStruct((TOP_K, t, d), ys.dtype),
        scratch_types=[pltpu.VMEM((ch,), jnp.int32), pltpu.VMEM((ch, d), ys.dtype), pltpu.SemaphoreType.DMA])
    def gather_rows(ys_hbm, pos_hbm, out_hbm, idx_v, rows_v, sem):
        base = (lax.axis_index("s") * n_cores + lax.axis_index("c")) * tpw

        @pl.loop(0, tpw // ch)
        def _(i):
            off = base + i * ch
            for k in range(TOP_K):
                pltpu.sync_copy(pos_hbm.at[k, pl.ds(off, ch)], idx_v)
                pltpu.async_copy(ys_hbm.at[idx_v], rows_v, sem).wait()
                pltpu.sync_copy(rows_v, out_hbm.at[k, pl.ds(off, ch)])

    return gather_rows(ys, pos8)


def _pack_bf16_pairs(a):
    n = a.shape[1] // 2
    lo = pltpu.bitcast(a[:, :n].astype(BF16).astype(F32), jnp.int32)
    hi = pltpu.bitcast(a[:, n:].astype(BF16).astype(F32), jnp.int32)
    return jnp.bitwise_or(jnp.bitwise_and(hi, -65536), jnp.bitwise_and(lax.shift_right_logical(lo, 16), 65535))


def _unpack_bf16_pairs(p):
    lo = pltpu.bitcast(lax.shift_left(p, 16), F32)
    hi = pltpu.bitcast(jnp.bitwise_and(p, -65536), F32)
    return lo, hi


def _expert_kernel(meta_ref, xs_ref, wup_ref, bup_ref, wdn_ref, bdn_ref, ys_ref,
                   wup_s, wdn_s, xbuf, obuf, in_sem, out_sem):
    e = pl.program_id(0)
    tm = TM_MOE
    half = D_MODEL // 2
    pad = meta_ref[META_PAD, e]
    n_t = pad // tm
    row0 = meta_ref[META_END, e] - pad
    cnt = meta_ref[META_CNT, e]

    def rows(i):
        return pl.ds(pl.multiple_of(row0 + i * tm, tm), tm)

    def x_copy(i, slot):
        return pltpu.make_async_copy(xs_ref.at[rows(i)], xbuf.at[slot], in_sem.at[slot])

    def y_copy(i, slot):
        return pltpu.make_async_copy(obuf.at[slot], ys_ref.at[rows(i)], out_sem.at[slot])

    @pl.when(n_t > 0)
    def _():
        x_copy(0, 0).start()
        wup_s[...] = wup_ref[0, 0].astype(BF16)
        wdn_s[...] = wdn_ref[0, 0].astype(BF16)

        def tile(i, carry):
            slot = lax.rem(i, 2)
            x_copy(i, slot).wait()

            @pl.when(i + 1 < n_t)
            def _():
                x_copy(i + 1, 1 - slot).start()

            @pl.when(i >= 2)
            def _():
                y_copy(i - 2, slot).wait()

            row = lax.broadcasted_iota(jnp.int32, (tm, 1), 0)
            lo, hi = _unpack_bf16_pairs(jnp.where(row < cnt - i * tm, xbuf[slot], 0))
            z = (jnp.dot(lo.astype(BF16), wup_s[0:half, :], preferred_element_type=F32)
                 + jnp.dot(hi.astype(BF16), wup_s[half:, :], preferred_element_type=F32) + bup_ref[0, 0])
            glu = jnp.minimum(z[:, :D_EXPERT], SWIGLU_LIMIT)
            lin = jnp.clip(z[:, D_EXPERT:], -SWIGLU_LIMIT, SWIGLU_LIMIT)
            act = (glu * _sigmoid(SWIGLU_ALPHA * glu) * (lin + 1.0)).astype(BF16)
            y = jnp.dot(act, wdn_s[...], preferred_element_type=F32) + bdn_ref[0, 0]
            obuf[slot] = _pack_bf16_pairs(y)
            y_copy(i, slot).start()
            return carry

        lax.fori_loop(0, n_t, tile, 0)

        @pl.when(n_t >= 2)
        def _():
            y_copy(n_t - 2, lax.rem(n_t, 2)).wait()

        y_copy(n_t - 1, lax.rem(n_t - 1, 2)).wait()

    @pl.when(e == N_EXPERTS - 1)
    def _():
        obuf[0] = jnp.zeros((tm, half), jnp.int32)

        def fill(i, carry):
            cp = pltpu.make_async_copy(obuf.at[0], ys_ref.at[pl.ds(pl.multiple_of(i * tm, tm), tm)], out_sem.at[0])
            cp.start()
            cp.wait()
            return carry

        lax.fori_loop(meta_ref[META_END, N_EXPERTS - 1] // tm, ys_ref.shape[0] // tm, fill, 0)


def _expert_mlp(xs, meta, layer, w_up, b_up, w_down, b_down):
    n_rows, half = xs.shape
    d = 2 * half
    tm = TM_MOE
    f = w_up.shape[-1]
    b_up4 = b_up.reshape(DEPTH, N_EXPERTS, 1, f)
    b_dn4 = b_down.reshape(DEPTH, N_EXPERTS, 1, d)
    wsel = lambda e, m: (layer, e, 0, 0)
    grid_spec = pltpu.PrefetchScalarGridSpec(
        num_scalar_prefetch=1,
        grid=(N_EXPERTS,),
        in_specs=[pl.BlockSpec(memory_space=pl.ANY),
                  pl.BlockSpec((1, 1, d, f), wsel),
                  pl.BlockSpec((1, 1, 1, f), wsel),
                  pl.BlockSpec((1, 1, f // 2, d), wsel),
                  pl.BlockSpec((1, 1, 1, d), wsel)],
        out_specs=pl.BlockSpec(memory_space=pl.ANY),
        scratch_shapes=[pltpu.VMEM((d, f), BF16), pltpu.VMEM((f // 2, d), BF16),
                        pltpu.VMEM((2, tm, half), jnp.int32), pltpu.VMEM((2, tm, half), jnp.int32),
                        pltpu.SemaphoreType.DMA((2,)), pltpu.SemaphoreType.DMA((2,))],
    )
    return pl.pallas_call(
        _expert_kernel,
        grid_spec=grid_spec,
        out_shape=jax.ShapeDtypeStruct((n_rows, half), jnp.int32),
        compiler_params=_cparams(("arbitrary",)),
        name="expert_mlp",
    )(meta, xs, w_up, b_up4, w_down, b_dn4)


def _combine_kernel(rows_ref, prob_ref, x_ref, gate_ref, lng_ref, lnb_ref, o_ref):
    p = prob_ref[...]
    y = None
    for k in range(TOP_K):
        yk = p[:, k:k + 1] * jnp.concatenate(_unpack_bf16_pairs(rows_ref[k]), axis=1)
        y = yk if y is None else y + yk
    o_ref[0] = _layer_norm(DN_ALPHA * x_ref[0] + (1.0 + gate_ref[0]) * y, lng_ref[...], lnb_ref[...])


def _combine(rows, prob_c, x1, gate, ln_g, ln_b):
    bsz, s, d = x1.shape
    tm = TM_DISP
    nt = s // tm
    return pl.pallas_call(
        _combine_kernel,
        grid=(bsz, nt),
        in_specs=[pl.BlockSpec((TOP_K, tm, d // 2), lambda b, i: (0, b * nt + i, 0)),
                  pl.BlockSpec((tm, LANES), lambda b, i: (b * nt + i, 0)),
                  pl.BlockSpec((1, tm, d), lambda b, i: (b, i, 0)),
                  pl.BlockSpec((1, 1, d), lambda b, i: (b, 0, 0)),
                  pl.BlockSpec((1, d), lambda b, i: (0, 0)),
                  pl.BlockSpec((1, d), lambda b, i: (0, 0))],
        out_specs=pl.BlockSpec((1, tm, d), lambda b, i: (b, i, 0)),
        out_shape=jax.ShapeDtypeStruct((bsz, s, d), F32),
        compiler_params=_cparams(("arbitrary", "arbitrary")),
        name="combine",
    )(rows, prob_c, x1, gate.reshape(bsz, 1, d), ln_g.reshape(1, d), ln_b.reshape(1, d))


def kernel(x, c, positions, ada_w, ada_b, w_in, lam_q1, lam_k1, lam_q2, lam_k2, da_norm_g, ml_conv_w, ml_conv_b,
           ml_w_q, ml_w_k, ml_gate_b, ml_norm_g, s5_a_re, s5_a_im, s5_log_dt, s5_b_re, s5_b_im, s5_c_re, s5_c_im,
           s5_d, s5_w_glu, w_out, ln_g, ln_b, w_router, b_router, w_up, b_up, w_down, b_down):
    bsz, s, d = x.shape
    t = bsz * s
    n_tiles_max = (t * TOP_K) // TM_MOE + N_EXPERTS
    n_rows = n_tiles_max * TM_MOE
    mod = _modulation(c, ada_w, ada_b)
    cos_t, sin_t = _rope_tables(positions)
    for l in range(DEPTH):
        shift, scale, gate = jnp.split(mod[2 * l], 3, axis=-1)
        q, k, v, mlx, mlv, mlo, g_t, g_c, s5u = _in_proj(x, shift, scale, cos_t, sin_t, w_in[l])
        lam_init = 0.8 - 0.6 * math.exp(-0.3 * l)
        lamv = jnp.stack([lam_q1[l], lam_k1[l], lam_q2[l], lam_k2[l]])
        y_da = _diff_attn(q, k, v, lamv, da_norm_g[l], lam_init)
        y_ml = _mlstm(mlx, mlv, mlo, g_t, g_c, ml_conv_w[l], ml_conv_b[l], ml_w_q[l], ml_w_k[l],
                      ml_gate_b[l], ml_norm_g[l])
        s5p = _s5_params(s5_a_re[l], s5_a_im[l], s5_log_dt[l], s5_b_re[l], s5_b_im[l], s5_c_re[l], s5_c_im[l],
                         s5_w_glu[l])
        y_s5 = _s5(s5u, bsz, s5p, s5_d[l])
        shift2, scale2, gate2 = jnp.split(mod[2 * l + 1], 3, axis=-1)
        x1, h2, eid, prob = _out_proj(y_da, y_ml, y_s5, x, gate, ln_g[l, 0], ln_b[l, 0], shift2, scale2,
                                      w_out[l], w_router[l], b_router[l])
        pos8, meta = _route(eid)
        xs = _dispatch(h2, pos8, n_rows)
        ys = _expert_mlp(xs, meta, l, w_up, b_up, w_down, b_down)
        rows = _gather_expert_rows(ys, pos8)
        prob_c = jnp.pad(prob[:TOP_K].T, ((0, 0), (0, LANES - TOP_K)))
        x = _combine(rows, prob_c, x1, gate2, ln_g[l, 1], ln_b[l, 1])
    return x
```

```python
import functools
import math

import jax
import jax.numpy as jnp
from jax import lax
from jax.experimental import pallas as pl
from jax.experimental.pallas import tpu as pltpu
from jax.experimental.pallas import tpu_sc as plsc

F32 = jnp.float32
BF16 = jnp.bfloat16
HIGHEST = lax.Precision.HIGHEST

D_MODEL = 1024
DEPTH = 2
DA_HEADS = 4
DA_HEAD_DIM = 64
DA_V_DIM = 2 * DA_HEAD_DIM
DA_WIDTH = DA_HEADS * DA_V_DIM
DA_QK_WIDTH = DA_HEADS * 2 * DA_HEAD_DIM
ROPE_THETA = 10000.0
ML_HEADS = 4
ML_HEAD_DIM = 64
ML_WIDTH = ML_HEADS * ML_HEAD_DIM
ML_CONV = 4
S5_GROUP = 16
S5_STATE = 64
S5_WIDTH = D_MODEL - DA_WIDTH - ML_WIDTH
S5_GROUPS = S5_WIDTH // S5_GROUP
S5_NSTATE = S5_GROUPS * S5_STATE
N_EXPERTS = 32
TOP_K = 4
D_EXPERT = D_MODEL
SWIGLU_LIMIT = 7.0
SWIGLU_ALPHA = 1.702
DN_ALPHA = (2 * DEPTH) ** 0.25
LN_EPS = 1e-5
NEG = -1e30

OFF_DA_K = DA_QK_WIDTH
OFF_DA_V = 2 * DA_QK_WIDTH
OFF_ML_X = OFF_DA_V + DA_WIDTH
OFF_ML_V = OFF_ML_X + ML_WIDTH
OFF_ML_O = OFF_ML_V + ML_WIDTH
OFF_ML_I = OFF_ML_O + ML_WIDTH
OFF_ML_F = OFF_ML_I + ML_HEADS
OFF_S5_U = OFF_ML_F + ML_HEADS
N_IN = OFF_S5_U + S5_WIDTH

LANES = 128
SUBLANES = 8
VMEM_LIMIT_BYTES = 56 * 1024 * 1024

TM_PROJ = 512
TQ = 512
ML_CHUNK = 256
S5_TC = 128
S5_UNROLL = 8
TB_RANK = 512
TM_MOE = 256
TM_DISP = 256
SC_CHUNK = 128
GATE_PAD = 8
VT_ROWS = DA_V_DIM + 16
Q_PRESCALE = DA_HEAD_DIM ** -0.5 * math.log2(math.e)


def _cparams(sem, vmem=VMEM_LIMIT_BYTES):
    return pltpu.CompilerParams(dimension_semantics=sem, vmem_limit_bytes=vmem)


def _sigmoid(x):
    return 1.0 / (1.0 + jnp.exp(-x))


def _mod_kernel(c_ref, w_ref, b_ref, o_ref):
    c = c_ref[...]
    ca = (c * _sigmoid(c)).astype(BF16)
    w = w_ref[0].astype(BF16)
    o_ref[0] = jnp.dot(ca, w, preferred_element_type=F32) + b_ref[0]


def _modulation(c, ada_w, ada_b):
    nmod = ada_w.shape[0] * ada_w.shape[1]
    bsz, d = c.shape
    e = ada_w.shape[-1]
    tn = 1024
    w = ada_w.reshape(nmod, d, e)
    b = ada_b.reshape(nmod, 1, e)
    return pl.pallas_call(
        _mod_kernel,
        grid=(nmod, e // tn),
        in_specs=[pl.BlockSpec((bsz, d), lambda n, j: (0, 0)),
                  pl.BlockSpec((1, d, tn), lambda n, j: (n, 0, j)),
                  pl.BlockSpec((1, 1, tn), lambda n, j: (n, 0, j))],
        out_specs=pl.BlockSpec((1, bsz, tn), lambda n, j: (n, 0, j)),
        out_shape=jax.ShapeDtypeStruct((nmod, bsz, e), F32),
        compiler_params=_cparams(("arbitrary", "arbitrary")),
        name="modulation",
    )(c, w, b)


def _rope_kernel(pos_ref, cos_ref, sin_ref):
    pos = pos_ref[0].astype(F32)
    lane = lax.broadcasted_iota(jnp.int32, (1, LANES), 1)
    fidx = (lane % (DA_HEAD_DIM // 2)).astype(F32)
    inv = jnp.exp(fidx * (-2.0 * math.log(ROPE_THETA) / DA_HEAD_DIM))
    ang = pos * inv
    sign = jnp.where((lane % DA_HEAD_DIM) < DA_HEAD_DIM // 2, -1.0, 1.0)
    cos_ref[0] = jnp.cos(ang)
    sin_ref[0] = jnp.sin(ang) * sign


def _rope_tables(positions):
    bsz, s = positions.shape
    ts = 512
    pos3 = positions.reshape(bsz, s, 1)
    return pl.pallas_call(
        _rope_kernel,
        grid=(bsz, s // ts),
        in_specs=[pl.BlockSpec((1, ts, 1), lambda b, i: (b, i, 0))],
        out_specs=[pl.BlockSpec((1, ts, LANES), lambda b, i: (b, i, 0))] * 2,
        out_shape=[jax.ShapeDtypeStruct((bsz, s, LANES), F32)] * 2,
        compiler_params=_cparams(("arbitrary", "arbitrary")),
        name="rope_tables",
    )(pos3)


def _in_proj_kernel(x_ref, shift_ref, scale_ref, cos_ref, sin_ref, wqk_ref, wvt_ref, wrest_ref, wgt_ref, wgc_ref,
                    q_ref, k_ref, vt_ref, mlx_ref, mlv_ref, mlo_ref, gt_ref, gc_ref, s5u_ref):
    h = (x_ref[0] * (1.0 + scale_ref[0]) + shift_ref[0]).astype(BF16)
    cos = cos_ref[0]
    sin = sin_ref[0]
    lane = lax.broadcasted_iota(jnp.int32, (1, LANES), 1)
    lo_half = (lane % DA_HEAD_DIM) < DA_HEAD_DIM // 2
    half = DA_HEAD_DIM // 2

    def rope(t):
        fwd = pltpu.roll(t, half, 1)
        bwd = pltpu.roll(t, LANES - half, 1)
        partner = jnp.where(lo_half, bwd, fwd)
        return t * cos + partner * sin

    qk = jnp.dot(h, wqk_ref[...], preferred_element_type=F32)
    nslab = DA_QK_WIDTH // LANES
    for c in range(nslab):
        q_ref[0, :, c * LANES:(c + 1) * LANES] = (
            rope(qk[:, c * LANES:(c + 1) * LANES]) * Q_PRESCALE).astype(BF16)
        k_ref[0, :, c * LANES:(c + 1) * LANES] = rope(
            qk[:, DA_QK_WIDTH + c * LANES:DA_QK_WIDTH + (c + 1) * LANES]).astype(BF16)

    vt = lax.dot_general(wvt_ref[...], h, (((1,), (1,)), ((), ())), preferred_element_type=F32)
    tm = h.shape[0]
    for hh in range(DA_HEADS):
        for jj in range(tm // TQ):
            vt_ref[0, hh, jj, 0:DA_V_DIM, :] = vt[hh * DA_V_DIM:(hh + 1) * DA_V_DIM,
                                                  jj * TQ:(jj + 1) * TQ].astype(BF16)
            vt_ref[0, hh, jj, DA_V_DIM:VT_ROWS, :] = jnp.ones((VT_ROWS - DA_V_DIM, TQ), BF16)

    r = jnp.dot(h, wrest_ref[...], preferred_element_type=F32)
    o = 0
    mlx_ref[0] = r[:, o:o + ML_WIDTH].astype(BF16); o += ML_WIDTH
    mlv_ref[0] = r[:, o:o + ML_WIDTH].astype(BF16); o += ML_WIDTH
    mlo_ref[0] = r[:, o:o + ML_WIDTH].astype(BF16); o += ML_WIDTH
    s5u_ref[...] = r[:, o:o + S5_WIDTH]
    gt_ref[0] = lax.dot_general(wgt_ref[...], h, (((1,), (1,)), ((), ())), preferred_element_type=F32)
    gc_ref[0] = jnp.dot(h, wgc_ref[...], preferred_element_type=F32)


def _in_proj(x, shift, scale, cos_t, sin_t, w_in_l):
    bsz, s, d = x.shape
    tm = TM_PROJ
    w = w_in_l.astype(BF16)
    wqk = w[:, :OFF_DA_V]
    wvt = w[:, OFF_DA_V:OFF_ML_X].T
    wrest = jnp.concatenate([w[:, OFF_ML_X:OFF_ML_I], w[:, OFF_S5_U:]], axis=1)
    wg = w[:, OFF_ML_I:OFF_S5_U]
    wgt = wg.T
    wgc = jnp.pad(wg, ((0, 0), (0, LANES - GATE_PAD)))
    nrest = wrest.shape[1]
    shift3 = shift.reshape(bsz, 1, d)
    scale3 = scale.reshape(bsz, 1, d)
    tok = lambda b, i: (b, i, 0)
    per_b = lambda b, i: (b, 0, 0)
    const2 = lambda b, i: (0, 0)
    out_shapes = [
        jax.ShapeDtypeStruct((bsz, s, DA_QK_WIDTH), BF16),
        jax.ShapeDtypeStruct((bsz, s, DA_QK_WIDTH), BF16),
        jax.ShapeDtypeStruct((bsz, DA_HEADS, s // TQ, VT_ROWS, TQ), BF16),
        jax.ShapeDtypeStruct((bsz, s, ML_WIDTH), BF16),
        jax.ShapeDtypeStruct((bsz, s, ML_WIDTH), BF16),
        jax.ShapeDtypeStruct((bsz, s, ML_WIDTH), BF16),
        jax.ShapeDtypeStruct((bsz, GATE_PAD, s), F32),
        jax.ShapeDtypeStruct((bsz, s, LANES), F32),
        jax.ShapeDtypeStruct((s, bsz * S5_WIDTH), F32),
    ]
    out_specs = [
        pl.BlockSpec((1, tm, DA_QK_WIDTH), tok),
        pl.BlockSpec((1, tm, DA_QK_WIDTH), tok),
        pl.BlockSpec((1, DA_HEADS, tm // TQ, VT_ROWS, TQ), lambda b, i: (b, 0, i, 0, 0)),
        pl.BlockSpec((1, tm, ML_WIDTH), tok),
        pl.BlockSpec((1, tm, ML_WIDTH), tok),
        pl.BlockSpec((1, tm, ML_WIDTH), tok),
        pl.BlockSpec((1, GATE_PAD, tm), lambda b, i: (b, 0, i)),
        pl.BlockSpec((1, tm, LANES), tok),
        pl.BlockSpec((tm, S5_WIDTH), lambda b, i: (i, b)),
    ]
    return pl.pallas_call(
        _in_proj_kernel,
        grid=(bsz, s // tm),
        in_specs=[pl.BlockSpec((1, tm, d), tok),
                  pl.BlockSpec((1, 1, d), per_b),
                  pl.BlockSpec((1, 1, d), per_b),
                  pl.BlockSpec((1, tm, LANES), tok),
                  pl.BlockSpec((1, tm, LANES), tok),
                  pl.BlockSpec((d, OFF_DA_V), const2),
                  pl.BlockSpec((DA_WIDTH, d), const2),
                  pl.BlockSpec((d, nrest), const2),
                  pl.BlockSpec((GATE_PAD, d), const2),
                  pl.BlockSpec((d, LANES), const2)],
        out_specs=out_specs,
        out_shape=out_shapes,
        compiler_params=_cparams(("arbitrary", "arbitrary")),
        name="in_proj",
    )(x, shift3, scale3, cos_t, sin_t, wqk, wvt, wrest, wgt, wgc)


def _diff_attn_kernel(lam_init, lamv_ref, gain_ref, q_ref, k_ref, vt_ref, o_ref, acc_s, m_s):
    qi = pl.program_id(2)
    tq = q_ref.shape[1]
    lane = lax.broadcasted_iota(jnp.int32, (1, LANES), 1)
    first = lane < DA_HEAD_DIM
    q = q_ref[0]
    zero = jnp.zeros_like(q)
    qm = (jnp.where(first, q, zero), jnp.where(first, zero, q))
    acc_s[...] = jnp.zeros_like(acc_s)
    m_s[...] = jnp.full(m_s.shape, NEG, F32)

    def step(j, nblk, masked):
        tk = nblk * tq
        kb = k_ref[0, pl.ds(pl.multiple_of(j * tq, tq), tk), :]
        vtb = vt_ref[0, 0, j] if nblk == 1 else jnp.concatenate([vt_ref[0, 0, j + b] for b in range(nblk)], axis=1)
        for c in range(2):
            st = lax.dot_general(kb, qm[c], (((1,), (1,)), ((), ())), preferred_element_type=F32)
            if masked:
                key_i = lax.broadcasted_iota(jnp.int32, (tk, tq), 0) - (tk - tq)
                qry_i = lax.broadcasted_iota(jnp.int32, (tk, tq), 1)
                st = jnp.where(key_i <= qry_i, st, NEG)
            m_prev = m_s[c]
            m_new = jnp.maximum(m_prev, jnp.max(st, axis=0, keepdims=True))
            alpha = jnp.exp2(m_prev - m_new)
            p = jnp.exp2(st - m_new).astype(BF16)
            acc_s[c] = alpha * acc_s[c] + jnp.dot(vtb, p, preferred_element_type=F32)
            m_s[c] = m_new

    def body(jj, carry):
        step(2 * jj, 2, False)
        return carry

    lax.fori_loop(0, qi // 2, body, 0)

    @pl.when(qi % 2 == 1)
    def _():
        step(qi - 1, 2, True)

    @pl.when(qi % 2 == 0)
    def _():
        step(qi, 1, True)

    outs = []
    for c in range(2):
        acc = acc_s[c]
        outs.append(acc[:DA_V_DIM] / acc[DA_V_DIM:DA_V_DIM + 1])

    lamv = lamv_ref[...]
    lam = (jnp.exp(jnp.sum(lamv[0:1] * lamv[1:2], axis=1, keepdims=True))
           - jnp.exp(jnp.sum(lamv[2:3] * lamv[3:4], axis=1, keepdims=True)) + lam_init)
    ot = outs[0] - lam * outs[1]
    ms = jnp.mean(ot * ot, axis=0, keepdims=True)
    ot = ot * (lax.rsqrt(ms + LN_EPS) * (1.0 - lam_init))
    o_ref[0] = (ot.T * gain_ref[...]).astype(o_ref.dtype)


def _diff_attn(q, k, vt, lamv, gain, lam_init):
    bsz, s, _ = q.shape
    tq = TQ
    nq = s // tq
    return pl.pallas_call(
        functools.partial(_diff_attn_kernel, lam_init),
        grid=(bsz, DA_HEADS, nq),
        in_specs=[pl.BlockSpec((4, DA_HEAD_DIM), lambda b, h, i: (0, 0)),
                  pl.BlockSpec((1, DA_V_DIM), lambda b, h, i: (0, 0)),
                  pl.BlockSpec((1, tq, DA_V_DIM), lambda b, h, i: (b, i, h)),
                  pl.BlockSpec((1, s, DA_V_DIM), lambda b, h, i: (b, 0, h)),
                  pl.BlockSpec((1, 1, nq, VT_ROWS, tq), lambda b, h, i: (b, h, 0, 0, 0))],
        out_specs=pl.BlockSpec((1, tq, DA_V_DIM), lambda b, h, i: (b, i, h)),
        out_shape=jax.ShapeDtypeStruct((bsz, s, DA_WIDTH), BF16),
        scratch_shapes=[pltpu.VMEM((2, VT_ROWS, tq), F32), pltpu.VMEM((2, 1, tq), F32)],
        compiler_params=_cparams(("arbitrary", "arbitrary", "arbitrary")),
        name="diff_attn",
    )(lamv, gain.reshape(1, DA_V_DIM), q, k, vt)


def _log_sigmoid(x):
    return jnp.minimum(x, 0.0) - jnp.log(1.0 + jnp.exp(-jnp.abs(x)))


def _mlstm_kernel(x_ref, v_ref, o_ref, gt_ref, gc_ref, cw_ref, cb_ref, wq_ref, wkt_ref, gbt_ref, gbc_ref,
                  ng_ref, hmean_ref, y_ref, xc_s, c_s, m_s):
    s = x_ref.shape[1]
    L = ML_CHUNK
    H, dh = ML_HEADS, ML_HEAD_DIM
    nc = s // L
    x = x_ref[0].astype(F32)
    cw = cw_ref[...]
    row = lax.broadcasted_iota(jnp.int32, (s, 1), 0)
    xc = x * cw[ML_CONV - 1:ML_CONV]
    for j in range(1, ML_CONV):
        xs = jnp.where(row >= j, pltpu.roll(x, j, 0), 0.0)
        xc = xc + xs * cw[ML_CONV - 1 - j:ML_CONV - j]
    xc = xc + cb_ref[...]
    xc_s[...] = (xc * _sigmoid(xc)).astype(BF16)

    c_s[...] = jnp.zeros_like(c_s)
    m_s[...] = jnp.full(m_s.shape, NEG, F32)

    ri = lax.broadcasted_iota(jnp.int32, (L, L), 0)
    ci = lax.broadcasted_iota(jnp.int32, (L, L), 1)
    causal = ci <= ri
    tril = causal.astype(F32)
    triu = (ri <= ci).astype(F32)
    lane = lax.broadcasted_iota(jnp.int32, (1, dh), 1)
    one_hot0 = jnp.broadcast_to((lane == 0).astype(BF16), (L, dh))

    def chunk(ci_, _):
        t0 = pl.multiple_of(ci_ * L, L)
        xcc = xc_s[pl.ds(t0, L), :]
        qc = jnp.dot(xcc, wq_ref[...], preferred_element_type=F32).astype(BF16)
        ktc = lax.dot_general(wkt_ref[...], xcc, (((1,), (1,)), ((), ())),
                              preferred_element_type=F32)
        g_rows = gt_ref[0, ci_] + gbt_ref[...]
        g_cols = gc_ref[0, pl.ds(t0, L), :] + gbc_ref[...]
        lf_rows = _log_sigmoid(g_rows)
        lf_cols = _log_sigmoid(g_cols)
        b_rows = jnp.dot(lf_rows, triu, preferred_element_type=F32, precision=HIGHEST)
        b_cols = jnp.dot(tril, lf_cols, preferred_element_type=F32, precision=HIGHEST)
        vch = v_ref[0, pl.ds(t0, L), :]
        och = o_ref[0, pl.ds(t0, L), :].astype(F32)
        hs = []
        for h in range(H):
            br = b_rows[H + h:H + h + 1, :]
            ir = g_rows[h:h + 1, :]
            bc = b_cols[:, H + h:H + h + 1]
            m_prev = m_s[h]
            log_d = jnp.where(causal, bc - br + ir, NEG)
            inter = bc + m_prev
            mx = jnp.maximum(inter, jnp.max(log_d, axis=1, keepdims=True))
            dmat = jnp.exp(log_d - mx)
            dec = jnp.exp(inter - mx)
            qh = qc[:, h * dh:(h + 1) * dh]
            kth = ktc[h * dh:(h + 1) * dh, :]
            vaug = jnp.concatenate([vch[:, h * dh:(h + 1) * dh], one_hot0], axis=1)
            sm = (jnp.dot(qh, kth.astype(BF16), preferred_element_type=F32) * dmat).astype(BF16)
            c_prev = c_s[h]
            na = (jnp.dot(sm, vaug, preferred_element_type=F32)
                  + dec * jnp.dot(qh, c_prev.astype(BF16), preferred_element_type=F32))
            den = na[:, dh:dh + 1]
            hs.append(na[:, :dh] / jnp.maximum(jnp.abs(den), jnp.exp(-mx)))
            g_tot = br[:, L - 1:L]
            a_row = g_tot - br + ir
            m_new = jnp.maximum(g_tot + m_prev, jnp.max(a_row, axis=1, keepdims=True))
            decay = jnp.exp(g_tot + m_prev - m_new)
            w_row = jnp.exp(a_row - m_new)
            kw = (kth * w_row).astype(BF16)
            c_s[h] = decay * c_prev + jnp.dot(kw, vaug, preferred_element_type=F32)
            m_s[h] = m_new
        hcat = jnp.concatenate(hs, axis=1)
        ms = jnp.dot(hcat * hcat, hmean_ref[...], preferred_element_type=F32, precision=HIGHEST)
        y = hcat * lax.rsqrt(ms + LN_EPS) * ng_ref[...] * _sigmoid(och)
        y_ref[0, pl.ds(t0, L), :] = y.astype(y_ref.dtype)
        return 0

    lax.fori_loop(0, nc, chunk, 0)


def _mlstm(mlx, mlv, mlo, g_t, g_c, conv_w, conv_b, w_q, w_k, gate_b, norm_g):
    bsz, s, _ = mlx.shape
    H, dh = ML_HEADS, ML_HEAD_DIM
    eye = jnp.eye(H, dtype=F32)
    wq_bd = jnp.einsum('hde,hg->hdge', w_q, eye).reshape(ML_WIDTH, ML_WIDTH).astype(BF16)
    wk_bd = jnp.einsum('hde,hg->hdge', w_k * (dh ** -0.5), eye).reshape(ML_WIDTH, ML_WIDTH)
    wkt_bd = wk_bd.T.astype(BF16)
    gbt = gate_b.reshape(GATE_PAD, 1)
    gbc = jnp.pad(gate_b.reshape(1, GATE_PAD), ((0, 0), (0, LANES - GATE_PAD)))
    hmean = jnp.kron(eye, jnp.full((dh, dh), 1.0 / dh, F32))
    nc = s // ML_CHUNK
    g_t4 = g_t.reshape(bsz, GATE_PAD, nc, ML_CHUNK).transpose(0, 2, 1, 3)
    tok = lambda b: (b, 0, 0)
    c2 = lambda b: (0, 0)
    return pl.pallas_call(
        _mlstm_kernel,
        grid=(bsz,),
        in_specs=[pl.BlockSpec((1, s, ML_WIDTH), tok),
                  pl.BlockSpec((1, s, ML_WIDTH), tok),
                  pl.BlockSpec((1, s, ML_WIDTH), tok),
                  pl.BlockSpec((1, nc, GATE_PAD, ML_CHUNK), lambda b: (b, 0, 0, 0)),
                  pl.BlockSpec((1, s, LANES), tok),
                  pl.BlockSpec((ML_CONV, ML_WIDTH), c2),
                  pl.BlockSpec((1, ML_WIDTH), c2),
                  pl.BlockSpec((ML_WIDTH, ML_WIDTH), c2),
                  pl.BlockSpec((ML_WIDTH, ML_WIDTH), c2),
                  pl.BlockSpec((GATE_PAD, 1), c2),
                  pl.BlockSpec((1, LANES), c2),
                  pl.BlockSpec((1, ML_WIDTH), c2),
                  pl.BlockSpec((ML_WIDTH, ML_WIDTH), c2)],
        out_specs=pl.BlockSpec((1, s, ML_WIDTH), tok),
        out_shape=jax.ShapeDtypeStruct((bsz, s, ML_WIDTH), BF16),
        scratch_shapes=[pltpu.VMEM((s, ML_WIDTH), BF16),
                        pltpu.VMEM((H, dh, LANES), F32),
                        pltpu.VMEM((H, 1, 1), F32)],
        compiler_params=_cparams(("arbitrary",)),
        name="mlstm",
    )(mlx, mlv, mlo, g_t4, g_c, conv_w, conv_b.reshape(1, ML_WIDTH), wq_bd, wkt_bd, gbt, gbc,
      norm_g.reshape(1, ML_WIDTH), hmean)


def _gelu_tanh(x):
    return 0.5 * x * (1.0 + jnp.tanh(math.sqrt(2.0 / math.pi) * (x + 0.044715 * (x * x * x))))


def _s5_kernel(u_ref, are_ref, aim_ref, bcat_ref, ccat_ref, d_ref, wglu_ref, y_ref, xs_s, st_s):
    tc, bsz, w = u_ref.shape
    n = S5_NSTATE

    @pl.when(pl.program_id(0) == 0)
    def _():
        st_s[...] = jnp.zeros_like(st_s)

    u = u_ref[...].reshape(tc * bsz, w)
    xs_s[...] = jnp.dot(u.astype(BF16), bcat_ref[...], preferred_element_type=F32).reshape(tc, bsz, 2 * n)
    a_re = jnp.broadcast_to(are_ref[...], (bsz, n))
    a_im = jnp.broadcast_to(aim_ref[...], (bsz, n))

    def step(t, carry):
        x_re, x_im = carry
        bu = xs_s[t]
        n_re = a_re * x_re - a_im * x_im + bu[:, :n]
        n_im = a_re * x_im + a_im * x_re + bu[:, n:]
        xs_s[t] = jnp.concatenate([n_re, n_im], axis=1)
        return n_re, n_im

    x_re, x_im = lax.fori_loop(0, tc, step, (st_s[0], st_s[1]), unroll=S5_UNROLL)
    st_s[0] = x_re
    st_s[1] = x_im

    xs = xs_s[...].reshape(tc * bsz, 2 * n).astype(BF16)
    y = jnp.dot(xs, ccat_ref[...], preferred_element_type=F32) + d_ref[...] * u
    z = jnp.dot(_gelu_tanh(y).astype(BF16), wglu_ref[...], preferred_element_type=F32)
    out = z[:, :w] * _sigmoid(z[:, w:])
    y_ref[...] = out.reshape(tc, bsz, w).astype(y_ref.dtype)


def _s5_params(a_re, a_im, log_dt, b_re, b_im, c_re, c_im, w_glu):
    G, P, Hc = S5_GROUPS, S5_STATE, S5_GROUP
    dt = jnp.exp(log_dt)[:, None]
    mag = jnp.exp(a_re * dt)
    ab_re = mag * jnp.cos(a_im * dt)
    ab_im = mag * jnp.sin(a_im * dt)
    nr, ni = ab_re - 1.0, ab_im
    den = a_re * a_re + a_im * a_im
    fr = (nr * a_re + ni * a_im) / den
    fi = (ni * a_re - nr * a_im) / den
    bb_re = fr[..., None] * b_re - fi[..., None] * b_im
    bb_im = fr[..., None] * b_im + fi[..., None] * b_re
    eye = jnp.eye(G, dtype=F32)
    bd = lambda t, sub: jnp.einsum(sub, t, eye)
    bre = bd(bb_re, 'gph,gk->ghkp').reshape(G * Hc, G * P)
    bim = bd(bb_im, 'gph,gk->ghkp').reshape(G * Hc, G * P)
    bcat = jnp.concatenate([bre, bim], axis=1).astype(BF16)
    cre = bd(c_re, 'ghp,gk->gpkh').reshape(G * P, G * Hc)
    cim = bd(c_im, 'ghp,gk->gpkh').reshape(G * P, G * Hc)
    ccat = jnp.concatenate([cre, -cim], axis=0).astype(BF16)
    wv = bd(w_glu[:, :, :Hc], 'ghj,gk->ghkj').reshape(G * Hc, G * Hc)
    wg = bd(w_glu[:, :, Hc:], 'ghj,gk->ghkj').reshape(G * Hc, G * Hc)
    wglu = jnp.concatenate([wv, wg], axis=1).astype(BF16)
    return ab_re.reshape(1, G * P), ab_im.reshape(1, G * P), bcat, ccat, wglu


def _s5(u_tm, bsz, params, d_skip):
    s = u_tm.shape[0]
    w = S5_WIDTH
    n = S5_NSTATE
    are, aim, bcat, ccat, wglu = params
    u3 = u_tm.reshape(s, bsz, w)
    tc = S5_TC
    c2 = lambda i: (0, 0)
    y = pl.pallas_call(
        _s5_kernel,
        grid=(s // tc,),
        in_specs=[pl.BlockSpec((tc, bsz, w), lambda i: (i, 0, 0)),
                  pl.BlockSpec((1, n), c2),
                  pl.BlockSpec((1, n), c2),
                  pl.BlockSpec((w, 2 * n), c2),
                  pl.BlockSpec((2 * n, w), c2),
                  pl.BlockSpec((1, w), c2),
                  pl.BlockSpec((w, 2 * w), c2)],
        out_specs=pl.BlockSpec((tc, bsz, w), lambda i: (i, 0, 0)),
        out_shape=jax.ShapeDtypeStruct((s, bsz, w), F32),
        scratch_shapes=[pltpu.VMEM((tc, bsz, 2 * n), F32),
                        pltpu.VMEM((2, bsz, n), F32)],
        compiler_params=_cparams(("arbitrary",)),
        name="s5",
    )(u3, are, aim, bcat, ccat, d_skip.reshape(1, w), wglu)
    return y.reshape(s, bsz * w)


def _layer_norm(z, g, b):
    mu = jnp.mean(z, axis=1, keepdims=True)
    zc = z - mu
    var = jnp.mean(zc * zc, axis=1, keepdims=True)
    return zc * lax.rsqrt(var + LN_EPS) * g + b


def _out_proj_kernel(yda_ref, yml_ref, ys5_ref, x_ref, gate_ref, lng_ref, lnb_ref, shift_ref, scale_ref,
                     wout_ref, wrt_ref, brt_ref, x1_ref, h2_ref, eid_ref, prob_ref):
    y = jnp.dot(yda_ref[0], wout_ref[0:DA_WIDTH, :], preferred_element_type=F32)
    y = y + jnp.dot(yml_ref[0], wout_ref[DA_WIDTH:DA_WIDTH + ML_WIDTH, :], preferred_element_type=F32)
    y = y + jnp.dot(ys5_ref[...].astype(BF16), wout_ref[DA_WIDTH + ML_WIDTH:, :], preferred_element_type=F32)
    x1 = _layer_norm(DN_ALPHA * x_ref[0] + (1.0 + gate_ref[0]) * y, lng_ref[...], lnb_ref[...])
    x1_ref[0] = x1
    h2 = x1 * (1.0 + scale_ref[0]) + shift_ref[0]
    h2_ref[...] = _pack_bf16_pairs(h2)
    h_hi = h2.astype(BF16)
    h_lo = (h2 - h_hi.astype(F32)).astype(BF16)
    nt_dot = lambda a, b: lax.dot_general(a, b, (((1,), (1,)), ((), ())), preferred_element_type=F32)
    by_hi = nt_dot(wrt_ref[...], h_hi)
    logits = (by_hi[:N_EXPERTS] + by_hi[N_EXPERTS:] + nt_dot(wrt_ref[0:N_EXPERTS, :], h_lo)
              + brt_ref[...])
    eidx = lax.broadcasted_iota(jnp.int32, logits.shape, 0)
    vals, ids = [], []
    for _ in range(TOP_K):
        mx = jnp.max(logits, axis=0, keepdims=True)
        sel = jnp.min(jnp.where(logits == mx, eidx, N_EXPERTS), axis=0, keepdims=True)
        vals.append(mx)
        ids.append(sel)
        logits = jnp.where(eidx == sel, -jnp.inf, logits)
    ex = [jnp.exp(v - vals[0]) for v in vals]
    tot = ex[0] + ex[1] + ex[2] + ex[3]
    zi = jnp.zeros_like(ids[0])
    eid_ref[...] = jnp.concatenate(ids + [zi] * (SUBLANES - TOP_K), axis=0)
    zf = jnp.zeros((LANES - TOP_K, tot.shape[1]), F32)
    prob_ref[...] = jnp.concatenate([e / tot for e in ex] + [zf], axis=0).T


def _out_proj(y_da, y_ml, y_s5, x, gate, ln_g, ln_b, shift2, scale2, w_out_l, w_router_l, b_router_l):
    bsz, s, d = x.shape
    tm = TM_PROJ
    nt = s // tm
    tok = lambda b, i: (b, i, 0)
    per_b = lambda b, i: (b, 0, 0)
    c2 = lambda b, i: (0, 0)
    r3 = lambda a: a.reshape(bsz, 1, d)
    flat = lambda b, i: (0, b * nt + i)
    wr_t = w_router_l.T
    wr_hi = wr_t.astype(BF16)
    wr_lo = (wr_t - wr_hi.astype(F32)).astype(BF16)
    return pl.pallas_call(
        _out_proj_kernel,
        grid=(bsz, nt),
        in_specs=[pl.BlockSpec((1, tm, DA_WIDTH), tok),
                  pl.BlockSpec((1, tm, ML_WIDTH), tok),
                  pl.BlockSpec((tm, S5_WIDTH), lambda b, i: (i, b)),
                  pl.BlockSpec((1, tm, d), tok),
                  pl.BlockSpec((1, 1, d), per_b),
                  pl.BlockSpec((1, d), c2),
                  pl.BlockSpec((1, d), c2),
                  pl.BlockSpec((1, 1, d), per_b),
                  pl.BlockSpec((1, 1, d), per_b),
                  pl.BlockSpec((d, d), c2),
                  pl.BlockSpec((2 * N_EXPERTS, d), c2),
                  pl.BlockSpec((N_EXPERTS, 1), c2)],
        out_specs=[pl.BlockSpec((1, tm, d), tok),
                   pl.BlockSpec((tm, d // 2), lambda b, i: (b * nt + i, 0)),
                   pl.BlockSpec((SUBLANES, tm), flat),
                   pl.BlockSpec((tm, LANES), lambda b, i: (b * nt + i, 0))],
        out_shape=[jax.ShapeDtypeStruct((bsz, s, d), F32),
                   jax.ShapeDtypeStruct((bsz * s, d // 2), jnp.int32),
                   jax.ShapeDtypeStruct((SUBLANES, bsz * s), jnp.int32),
                   jax.ShapeDtypeStruct((bsz * s, LANES), F32)],
        compiler_params=_cparams(("arbitrary", "arbitrary")),
        name="out_proj",
    )(y_da, y_ml, y_s5, x, r3(gate), ln_g.reshape(1, d), ln_b.reshape(1, d), r3(shift2), r3(scale2),
      w_out_l.astype(BF16), jnp.concatenate([wr_hi, wr_lo], axis=0), b_router_l.reshape(N_EXPERTS, 1))


META_END, META_PAD, META_CNT = 0, 1, 2


def _route_kernel(eid_ref, pos_ref, meta_ref, carry_s, start_s):
    phase = pl.program_id(0)
    i = pl.program_id(1)
    tb = eid_ref.shape[1]
    ntp = meta_ref.shape[1]
    tm = TM_MOE

    @pl.when(jnp.logical_and(phase == 0, i == 0))
    def _():
        carry_s[...] = jnp.zeros_like(carry_s)

    eid = eid_ref[...]
    eidx = lax.broadcasted_iota(jnp.int32, (N_EXPERTS, tb), 0)
    hot = [eidx == eid[k:k + 1, :] for k in range(TOP_K)]
    member = jnp.zeros((N_EXPERTS, tb), F32)
    for k in range(TOP_K):
        member = member + hot[k].astype(F32)
    total = jnp.sum(member, axis=1, keepdims=True)

    @pl.when(phase == 0)
    def _():
        carry_s[...] = carry_s[...] + total

    @pl.when(jnp.logical_and(phase == 1, i == 0))
    def _():
        cnt = carry_s[...]
        padded = jnp.floor((cnt + (tm - 1)) * (1.0 / tm)) * tm
        er = lax.broadcasted_iota(jnp.int32, (N_EXPERTS, N_EXPERTS), 0)
        ec = lax.broadcasted_iota(jnp.int32, (N_EXPERTS, N_EXPERTS), 1)
        ends = jnp.dot((ec <= er).astype(F32), padded, preferred_element_type=F32, precision=HIGHEST)
        start_s[...] = ends - padded
        carry_s[...] = jnp.zeros_like(carry_s)
        lane = lax.broadcasted_iota(jnp.int32, (N_EXPERTS, ntp), 1)
        sub = lax.broadcasted_iota(jnp.int32, (N_EXPERTS, ntp), 0)
        diag = lane == sub

        def as_row(col):
            return jnp.sum(jnp.where(diag, col, 0.0), axis=0, keepdims=True)

        zero = jnp.zeros((SUBLANES - 3, ntp), F32)
        meta_ref[...] = jnp.concatenate([as_row(ends[:, 0:1]), as_row(padded[:, 0:1]), as_row(cnt[:, 0:1]), zero],
                                        axis=0).astype(jnp.int32)

    @pl.when(phase == 1)
    def _():
        ri = lax.broadcasted_iota(jnp.int32, (tb, tb), 0)
        ci = lax.broadcasted_iota(jnp.int32, (tb, tb), 1)
        triu = (ri <= ci).astype(BF16)
        incl = jnp.dot(member.astype(BF16), triu, preferred_element_type=F32)
        slot = incl - member + carry_s[:, 0:1] + start_s[:, 0:1]
        rows = [jnp.sum(jnp.where(hot[k], slot, 0.0), axis=0, keepdims=True) for k in range(TOP_K)]
        zr = jnp.zeros_like(rows[0])
        pos_ref[...] = jnp.concatenate(rows + [zr] * (SUBLANES - TOP_K), axis=0).astype(jnp.int32)
        carry_s[...] = carry_s[...] + total


def _route(eid):
    t = eid.shape[1]
    tb = TB_RANK
    ntp = LANES
    pos8, meta = pl.pallas_call(
        _route_kernel,
        grid=(2, t // tb),
        in_specs=[pl.BlockSpec((SUBLANES, tb), lambda p, i: (0, i))],
        out_specs=[pl.BlockSpec((SUBLANES, tb), lambda p, i: (0, i * p)),
                   pl.BlockSpec((SUBLANES, ntp), lambda p, i: (0, 0))],
        out_shape=[jax.ShapeDtypeStruct((SUBLANES, t), jnp.int32),
                   jax.ShapeDtypeStruct((SUBLANES, ntp), jnp.int32)],
        scratch_shapes=[pltpu.VMEM((N_EXPERTS, LANES), F32), pltpu.VMEM((N_EXPERTS, LANES), F32)],
        compiler_params=_cparams(("arbitrary", "arbitrary")),
        name="route",
    )(eid)
    return pos8, meta


def _sc_workers():
    info = plsc.get_sparse_core_info()
    return info.num_cores, info.num_cores * info.num_subcores


def _dispatch(h2, pos8, n_rows):
    t, d = h2.shape
    n_cores, n_workers = _sc_workers()
    tpw = t // n_workers
    ch = SC_CHUNK
    mesh = plsc.VectorSubcoreMesh(core_axis_name="c", subcore_axis_name="s")

    @functools.partial(
        pl.kernel, mesh=mesh,
        out_type=jax.ShapeDtypeStruct((n_rows, d), h2.dtype),
        scratch_types=[pltpu.VMEM((ch,), jnp.int32)] * TOP_K + [pltpu.VMEM((ch, d), h2.dtype),
                                                                pltpu.SemaphoreType.DMA])
    def scatter_rows(h_hbm, pos_hbm, out_hbm, i0, i1, i2, i3, rows_v, sem):
        idx = (i0, i1, i2, i3)
        base = (lax.axis_index("s") * n_cores + lax.axis_index("c")) * tpw

        @pl.loop(0, tpw // ch)
        def _(i):
            off = base + i * ch
            pltpu.sync_copy(h_hbm.at[pl.ds(off, ch)], rows_v)
            for k in range(TOP_K):
                pltpu.sync_copy(pos_hbm.at[k, pl.ds(off, ch)], idx[k])
            copies = [pltpu.async_copy(rows_v, out_hbm.at[idx[k]], sem) for k in range(TOP_K)]
            for cp in copies:
                cp.wait()

    return scatter_rows(h2, pos8)


def _gather_expert_rows(ys, pos8):
    _, d = ys.shape
    t = pos8.shape[1]
    n_cores, n_workers = _sc_workers()
    tpw = t // n_workers
    ch = SC_CHUNK
    mesh = plsc.VectorSubcoreMesh(core_axis_name="c", subcore_axis_name="s")

    @functools.partial(
        pl.kernel, mesh=mesh,
        out_type=jax.ShapeDtypeStruct((TOP_K, t, d), ys.dtype),
        scratch_types=[pltpu.VMEM((ch,), jnp.int32), pltpu.VMEM((ch, d), ys.dtype), pltpu.SemaphoreType.DMA])
    def gather_rows(ys_hbm, pos_hbm, out_hbm, idx_v, rows_v, sem):
        base = (lax.axis_index("s") * n_cores + lax.axis_index("c")) * tpw

        @pl.loop(0, tpw // ch)
        def _(i):
            off = base + i * ch
            for k in range(TOP_K):
                pltpu.sync_copy(pos_hbm.at[k, pl.ds(off, ch)], idx_v)
                pltpu.async_copy(ys_hbm.at[idx_v], rows_v, sem).wait()
                pltpu.sync_copy(rows_v, out_hbm.at[k, pl.ds(off, ch)])

    return gather_rows(ys, pos8)


def _pack_bf16_pairs(a):
    n = a.shape[1] // 2
    lo = pltpu.bitcast(a[:, :n].astype(BF16).astype(F32), jnp.int32)
    hi = pltpu.bitcast(a[:, n:].astype(BF16).astype(F32), jnp.int32)
    return jnp.bitwise_or(jnp.bitwise_and(hi, -65536), jnp.bitwise_and(lax.shift_right_logical(lo, 16), 65535))


def _unpack_bf16_pairs(p):
    lo = pltpu.bitcast(lax.shift_left(p, 16), F32)
    hi = pltpu.bitcast(jnp.bitwise_and(p, -65536), F32)
    return lo, hi


def _expert_kernel(meta_ref, xs_ref, wup_ref, bup_ref, wdn_ref, bdn_ref, ys_ref,
                   wup_s, wdn_s, xbuf, obuf, in_sem, out_sem):
    e = pl.program_id(0)
    tm = TM_MOE
    half = D_MODEL // 2
    pad = meta_ref[META_PAD, e]
    n_t = pad // tm
    row0 = meta_ref[META_END, e] - pad
    cnt = meta_ref[META_CNT, e]

    def rows(i):
        return pl.ds(pl.multiple_of(row0 + i * tm, tm), tm)

    def x_copy(i, slot):
        return pltpu.make_async_copy(xs_ref.at[rows(i)], xbuf.at[slot], in_sem.at[slot])

    def y_copy(i, slot):
        return pltpu.make_async_copy(obuf.at[slot], ys_ref.at[rows(i)], out_sem.at[slot])

    @pl.when(n_t > 0)
    def _():
        x_copy(0, 0).start()
        wup_s[...] = wup_ref[0, 0].astype(BF16)
        wdn_s[...] = wdn_ref[0, 0].astype(BF16)

        def tile(i, carry):
            slot = lax.rem(i, 2)
            x_copy(i, slot).wait()

            @pl.when(i + 1 < n_t)
            def _():
                x_copy(i + 1, 1 - slot).start()

            @pl.when(i >= 2)
            def _():
                y_copy(i - 2, slot).wait()

            row = lax.broadcasted_iota(jnp.int32, (tm, 1), 0)
            lo, hi = _unpack_bf16_pairs(jnp.where(row < cnt - i * tm, xbuf[slot], 0))
            z = (jnp.dot(lo.astype(BF16), wup_s[0:half, :], preferred_element_type=F32)
                 + jnp.dot(hi.astype(BF16), wup_s[half:, :], preferred_element_type=F32) + bup_ref[0, 0])
            glu = jnp.minimum(z[:, :D_EXPERT], SWIGLU_LIMIT)
            lin = jnp.clip(z[:, D_EXPERT:], -SWIGLU_LIMIT, SWIGLU_LIMIT)
            act = (glu * _sigmoid(SWIGLU_ALPHA * glu) * (lin + 1.0)).astype(BF16)
            y = jnp.dot(act, wdn_s[...], preferred_element_type=F32) + bdn_ref[0, 0]
            obuf[slot] = _pack_bf16_pairs(y)
            y_copy(i, slot).start()
            return carry

        lax.fori_loop(0, n_t, tile, 0)

        @pl.when(n_t >= 2)
        def _():
            y_copy(n_t - 2, lax.rem(n_t, 2)).wait()

        y_copy(n_t - 1, lax.rem(n_t - 1, 2)).wait()

    @pl.when(e == N_EXPERTS - 1)
    def _():
        obuf[0] = jnp.zeros((tm, half), jnp.int32)

        def fill(i, carry):
            cp = pltpu.make_async_copy(obuf.at[0], ys_ref.at[pl.ds(pl.multiple_of(i * tm, tm), tm)], out_sem.at[0])
            cp.start()
            cp.wait()
            return carry

        lax.fori_loop(meta_ref[META_END, N_EXPERTS - 1] // tm, ys_ref.shape[0] // tm, fill, 0)


def _expert_mlp(xs, meta, layer, w_up, b_up, w_down, b_down):
    n_rows, half = xs.shape
    d = 2 * half
    tm = TM_MOE
    f = w_up.shape[-1]
    b_up4 = b_up.reshape(DEPTH, N_EXPERTS, 1, f)
    b_dn4 = b_down.reshape(DEPTH, N_EXPERTS, 1, d)
    wsel = lambda e, m: (layer, e, 0, 0)
    grid_spec = pltpu.PrefetchScalarGridSpec(
        num_scalar_prefetch=1,
        grid=(N_EXPERTS,),
        in_specs=[pl.BlockSpec(memory_space=pl.ANY),
                  pl.BlockSpec((1, 1, d, f), wsel),
                  pl.BlockSpec((1, 1, 1, f), wsel),
                  pl.BlockSpec((1, 1, f // 2, d), wsel),
                  pl.BlockSpec((1, 1, 1, d), wsel)],
        out_specs=pl.BlockSpec(memory_space=pl.ANY),
        scratch_shapes=[pltpu.VMEM((d, f), BF16), pltpu.VMEM((f // 2, d), BF16),
                        pltpu.VMEM((2, tm, half), jnp.int32), pltpu.VMEM((2, tm, half), jnp.int32),
                        pltpu.SemaphoreType.DMA((2,)), pltpu.SemaphoreType.DMA((2,))],
    )
    return pl.pallas_call(
        _expert_kernel,
        grid_spec=grid_spec,
        out_shape=jax.ShapeDtypeStruct((n_rows, half), jnp.int32),
        compiler_params=_cparams(("arbitrary",)),
        name="expert_mlp",
    )(meta, xs, w_up, b_up4, w_down, b_dn4)


def _combine_kernel(rows_ref, prob_ref, x_ref, gate_ref, lng_ref, lnb_ref, o_ref):
    p = prob_ref[...]
    y = None
    for k in range(TOP_K):
        yk = p[:, k:k + 1] * jnp.concatenate(_unpack_bf16_pairs(rows_ref[k]), axis=1)
        y = yk if y is None else y + yk
    o_ref[0] = _layer_norm(DN_ALPHA * x_ref[0] + (1.0 + gate_ref[0]) * y, lng_ref[...], lnb_ref[...])


def _combine(rows, prob_c, x1, gate, ln_g, ln_b):
    bsz, s, d = x1.shape
    tm = TM_DISP
    nt = s // tm
    return pl.pallas_call(
        _combine_kernel,
        grid=(bsz, nt),
        in_specs=[pl.BlockSpec((TOP_K, tm, d // 2), lambda b, i: (0, b * nt + i, 0)),
                  pl.BlockSpec((tm, LANES), lambda b, i: (b * nt + i, 0)),
                  pl.BlockSpec((1, tm, d), lambda b, i: (b, i, 0)),
                  pl.BlockSpec((1, 1, d), lambda b, i: (b, 0, 0)),
                  pl.BlockSpec((1, d), lambda b, i: (0, 0)),
                  pl.BlockSpec((1, d), lambda b, i: (0, 0))],
        out_specs=pl.BlockSpec((1, tm, d), lambda b, i: (b, i, 0)),
        out_shape=jax.ShapeDtypeStruct((bsz, s, d), F32),
        compiler_params=_cparams(("arbitrary", "arbitrary")),
        name="combine",
    )(rows, prob_c, x1, gate.reshape(bsz, 1, d), ln_g.reshape(1, d), ln_b.reshape(1, d))


def kernel(x, c, positions, ada_w, ada_b, w_in, lam_q1, lam_k1, lam_q2, lam_k2, da_norm_g, ml_conv_w, ml_conv_b,
           ml_w_q, ml_w_k, ml_gate_b, ml_norm_g, s5_a_re, s5_a_im, s5_log_dt, s5_b_re, s5_b_im, s5_c_re, s5_c_im,
           s5_d, s5_w_glu, w_out, ln_g, ln_b, w_router, b_router, w_up, b_up, w_down, b_down):
    bsz, s, d = x.shape
    t = bsz * s
    n_tiles_max = (t * TOP_K) // TM_MOE + N_EXPERTS
    n_rows = n_tiles_max * TM_MOE
    mod = _modulation(c, ada_w, ada_b)
    cos_t, sin_t = _rope_tables(positions)
    for l in range(DEPTH):
        shift, scale, gate = jnp.split(mod[2 * l], 3, axis=-1)
        q, k, v, mlx, mlv, mlo, g_t, g_c, s5u = _in_proj(x, shift, scale, cos_t, sin_t, w_in[l])
        lam_init = 0.8 - 0.6 * math.exp(-0.3 * l)
        lamv = jnp.stack([lam_q1[l], lam_k1[l], lam_q2[l], lam_k2[l]])
        y_da = _diff_attn(q, k, v, lamv, da_norm_g[l], lam_init)
        y_ml = _mlstm(mlx, mlv, mlo, g_t, g_c, ml_conv_w[l], ml_conv_b[l], ml_w_q[l], ml_w_k[l],
                      ml_gate_b[l], ml_norm_g[l])
        s5p = _s5_params(s5_a_re[l], s5_a_im[l], s5_log_dt[l], s5_b_re[l], s5_b_im[l], s5_c_re[l], s5_c_im[l],
                         s5_w_glu[l])
        y_s5 = _s5(s5u, bsz, s5p, s5_d[l])
        shift2, scale2, gate2 = jnp.split(mod[2 * l + 1], 3, axis=-1)
        x1, h2, eid, prob = _out_proj(y_da, y_ml, y_s5, x, gate, ln_g[l, 0], ln_b[l, 0], shift2, scale2,
                                      w_out[l], w_router[l], b_router[l])
        pos8, meta = _route(eid)
        xs = _dispatch(h2, pos8, n_rows)
        ys = _expert_mlp(xs, meta, l, w_up, b_up, w_down, b_down)
        rows = _gather_expert_rows(ys, pos8)
        x = _combine(rows, prob, x1, gate2, ln_g[l, 1], ln_b[l, 1])
    return x
```

```python
import functools
import math

import jax
import jax.numpy as jnp
from jax import lax
from jax.experimental import pallas as pl
from jax.experimental.pallas import tpu as pltpu
from jax.experimental.pallas import tpu_sc as plsc

F32 = jnp.float32
BF16 = jnp.bfloat16
HIGHEST = lax.Precision.HIGHEST

D_MODEL = 1024
DEPTH = 2
DA_HEADS = 4
DA_HEAD_DIM = 64
DA_V_DIM = 2 * DA_HEAD_DIM
DA_WIDTH = DA_HEADS * DA_V_DIM
DA_QK_WIDTH = DA_HEADS * 2 * DA_HEAD_DIM
ROPE_THETA = 10000.0
ML_HEADS = 4
ML_HEAD_DIM = 64
ML_WIDTH = ML_HEADS * ML_HEAD_DIM
ML_CONV = 4
S5_GROUP = 16
S5_STATE = 64
S5_WIDTH = D_MODEL - DA_WIDTH - ML_WIDTH
S5_GROUPS = S5_WIDTH // S5_GROUP
S5_NSTATE = S5_GROUPS * S5_STATE
N_EXPERTS = 32
TOP_K = 4
D_EXPERT = D_MODEL
SWIGLU_LIMIT = 7.0
SWIGLU_ALPHA = 1.702
DN_ALPHA = (2 * DEPTH) ** 0.25
LN_EPS = 1e-5
NEG = -1e30

OFF_DA_K = DA_QK_WIDTH
OFF_DA_V = 2 * DA_QK_WIDTH
OFF_ML_X = OFF_DA_V + DA_WIDTH
OFF_ML_V = OFF_ML_X + ML_WIDTH
OFF_ML_O = OFF_ML_V + ML_WIDTH
OFF_ML_I = OFF_ML_O + ML_WIDTH
OFF_ML_F = OFF_ML_I + ML_HEADS
OFF_S5_U = OFF_ML_F + ML_HEADS
N_IN = OFF_S5_U + S5_WIDTH

LANES = 128
SUBLANES = 8
VMEM_LIMIT_BYTES = 56 * 1024 * 1024

TM_PROJ = 512
TQ = 512
ML_CHUNK = 256
ML_NB = 2
S5_TC = 128
S5_UNROLL = 8
TB_RANK = 512
TM_MOE = 256
TM_DISP = 256
MOE_PARTS = 2
SC_CHUNK = 128
GATE_PAD = 8
VT_ROWS = DA_V_DIM + 16
Q_PRESCALE = DA_HEAD_DIM ** -0.5 * math.log2(math.e)


def _cparams(sem, vmem=VMEM_LIMIT_BYTES):
    return pltpu.CompilerParams(dimension_semantics=sem, vmem_limit_bytes=vmem)


def _sigmoid(x):
    return 1.0 / (1.0 + jnp.exp(-x))


def _mod_kernel(c_ref, w_ref, b_ref, o_ref):
    c = c_ref[...]
    ca = (c * _sigmoid(c)).astype(BF16)
    w = w_ref[0].astype(BF16)
    o_ref[0] = jnp.dot(ca, w, preferred_element_type=F32) + b_ref[0]


def _modulation(c, ada_w, ada_b):
    nmod = ada_w.shape[0] * ada_w.shape[1]
    bsz, d = c.shape
    e = ada_w.shape[-1]
    tn = 1024
    w = ada_w.reshape(nmod, d, e)
    b = ada_b.reshape(nmod, 1, e)
    return pl.pallas_call(
        _mod_kernel,
        grid=(nmod, e // tn),
        in_specs=[pl.BlockSpec((bsz, d), lambda n, j: (0, 0)),
                  pl.BlockSpec((1, d, tn), lambda n, j: (n, 0, j)),
                  pl.BlockSpec((1, 1, tn), lambda n, j: (n, 0, j))],
        out_specs=pl.BlockSpec((1, bsz, tn), lambda n, j: (n, 0, j)),
        out_shape=jax.ShapeDtypeStruct((nmod, bsz, e), F32),
        compiler_params=_cparams(("arbitrary", "arbitrary")),
        name="modulation",
    )(c, w, b)


def _rope_kernel(pos_ref, cos_ref, sin_ref):
    pos = pos_ref[0].astype(F32)
    lane = lax.broadcasted_iota(jnp.int32, (1, LANES), 1)
    fidx = (lane % (DA_HEAD_DIM // 2)).astype(F32)
    inv = jnp.exp(fidx * (-2.0 * math.log(ROPE_THETA) / DA_HEAD_DIM))
    ang = pos * inv
    sign = jnp.where((lane % DA_HEAD_DIM) < DA_HEAD_DIM // 2, -1.0, 1.0)
    cos_ref[0] = jnp.cos(ang)
    sin_ref[0] = jnp.sin(ang) * sign


def _rope_tables(positions):
    bsz, s = positions.shape
    ts = 512
    pos3 = positions.reshape(bsz, s, 1)
    return pl.pallas_call(
        _rope_kernel,
        grid=(bsz, s // ts),
        in_specs=[pl.BlockSpec((1, ts, 1), lambda b, i: (b, i, 0))],
        out_specs=[pl.BlockSpec((1, ts, LANES), lambda b, i: (b, i, 0))] * 2,
        out_shape=[jax.ShapeDtypeStruct((bsz, s, LANES), F32)] * 2,
        compiler_params=_cparams(("arbitrary", "arbitrary")),
        name="rope_tables",
    )(pos3)


def _in_proj_kernel(x_ref, shift_ref, scale_ref, cos_ref, sin_ref, wqk_ref, wvt_ref, wrest_ref, wgt_ref, wgc_ref,
                    q_ref, k_ref, vt_ref, mlx_ref, mlv_ref, mlo_ref, gt_ref, gc_ref, s5u_ref):
    h = (x_ref[0] * (1.0 + scale_ref[0]) + shift_ref[0]).astype(BF16)
    cos = cos_ref[0]
    sin = sin_ref[0]
    lane = lax.broadcasted_iota(jnp.int32, (1, LANES), 1)
    lo_half = (lane % DA_HEAD_DIM) < DA_HEAD_DIM // 2
    half = DA_HEAD_DIM // 2

    def rope(t):
        fwd = pltpu.roll(t, half, 1)
        bwd = pltpu.roll(t, LANES - half, 1)
        partner = jnp.where(lo_half, bwd, fwd)
        return t * cos + partner * sin

    qk = jnp.dot(h, wqk_ref[...], preferred_element_type=F32)
    nslab = DA_QK_WIDTH // LANES
    for c in range(nslab):
        q_ref[0, :, c * LANES:(c + 1) * LANES] = (
            rope(qk[:, c * LANES:(c + 1) * LANES]) * Q_PRESCALE).astype(BF16)
        k_ref[0, :, c * LANES:(c + 1) * LANES] = rope(
            qk[:, DA_QK_WIDTH + c * LANES:DA_QK_WIDTH + (c + 1) * LANES]).astype(BF16)

    vt = lax.dot_general(wvt_ref[...], h, (((1,), (1,)), ((), ())), preferred_element_type=F32)
    tm = h.shape[0]
    for hh in range(DA_HEADS):
        for jj in range(tm // TQ):
            vt_ref[0, hh, jj, 0:DA_V_DIM, :] = vt[hh * DA_V_DIM:(hh + 1) * DA_V_DIM,
                                                  jj * TQ:(jj + 1) * TQ].astype(BF16)
            vt_ref[0, hh, jj, DA_V_DIM:VT_ROWS, :] = jnp.ones((VT_ROWS - DA_V_DIM, TQ), BF16)

    r = jnp.dot(h, wrest_ref[...], preferred_element_type=F32)
    o = 0
    mlx_ref[0] = r[:, o:o + ML_WIDTH].astype(BF16); o += ML_WIDTH
    mlv_ref[0] = r[:, o:o + ML_WIDTH].astype(BF16); o += ML_WIDTH
    mlo_ref[0] = r[:, o:o + ML_WIDTH].astype(BF16); o += ML_WIDTH
    s5u_ref[...] = r[:, o:o + S5_WIDTH]
    gt_ref[0] = lax.dot_general(wgt_ref[...], h, (((1,), (1,)), ((), ())), preferred_element_type=F32)
    gc_ref[0] = jnp.dot(h, wgc_ref[...], preferred_element_type=F32)


def _in_proj(x, shift, scale, cos_t, sin_t, w_in_l):
    bsz, s, d = x.shape
    tm = TM_PROJ
    w = w_in_l.astype(BF16)
    wqk = w[:, :OFF_DA_V]
    wvt = w[:, OFF_DA_V:OFF_ML_X].T
    wrest = jnp.concatenate([w[:, OFF_ML_X:OFF_ML_I], w[:, OFF_S5_U:]], axis=1)
    wg = w[:, OFF_ML_I:OFF_S5_U]
    wgt = wg.T
    wgc = jnp.pad(wg, ((0, 0), (0, LANES - GATE_PAD)))
    nrest = wrest.shape[1]
    shift3 = shift.reshape(bsz, 1, d)
    scale3 = scale.reshape(bsz, 1, d)
    tok = lambda b, i: (b, i, 0)
    per_b = lambda b, i: (b, 0, 0)
    const2 = lambda b, i: (0, 0)
    out_shapes = [
        jax.ShapeDtypeStruct((bsz, s, DA_QK_WIDTH), BF16),
        jax.ShapeDtypeStruct((bsz, s, DA_QK_WIDTH), BF16),
        jax.ShapeDtypeStruct((bsz, DA_HEADS, s // TQ, VT_ROWS, TQ), BF16),
        jax.ShapeDtypeStruct((bsz, s, ML_WIDTH), BF16),
        jax.ShapeDtypeStruct((bsz, s, ML_WIDTH), BF16),
        jax.ShapeDtypeStruct((bsz, s, ML_WIDTH), BF16),
        jax.ShapeDtypeStruct((bsz, GATE_PAD, s), F32),
        jax.ShapeDtypeStruct((bsz, s, LANES), F32),
        jax.ShapeDtypeStruct((s, bsz * S5_WIDTH), F32),
    ]
    out_specs = [
        pl.BlockSpec((1, tm, DA_QK_WIDTH), tok),
        pl.BlockSpec((1, tm, DA_QK_WIDTH), tok),
        pl.BlockSpec((1, DA_HEADS, tm // TQ, VT_ROWS, TQ), lambda b, i: (b, 0, i, 0, 0)),
        pl.BlockSpec((1, tm, ML_WIDTH), tok),
        pl.BlockSpec((1, tm, ML_WIDTH), tok),
        pl.BlockSpec((1, tm, ML_WIDTH), tok),
        pl.BlockSpec((1, GATE_PAD, tm), lambda b, i: (b, 0, i)),
        pl.BlockSpec((1, tm, LANES), tok),
        pl.BlockSpec((tm, S5_WIDTH), lambda b, i: (i, b)),
    ]
    return pl.pallas_call(
        _in_proj_kernel,
        grid=(bsz, s // tm),
        in_specs=[pl.BlockSpec((1, tm, d), tok),
                  pl.BlockSpec((1, 1, d), per_b),
                  pl.BlockSpec((1, 1, d), per_b),
                  pl.BlockSpec((1, tm, LANES), tok),
                  pl.BlockSpec((1, tm, LANES), tok),
                  pl.BlockSpec((d, OFF_DA_V), const2),
                  pl.BlockSpec((DA_WIDTH, d), const2),
                  pl.BlockSpec((d, nrest), const2),
                  pl.BlockSpec((GATE_PAD, d), const2),
                  pl.BlockSpec((d, LANES), const2)],
        out_specs=out_specs,
        out_shape=out_shapes,
        compiler_params=_cparams(("arbitrary", "arbitrary")),
        name="in_proj",
    )(x, shift3, scale3, cos_t, sin_t, wqk, wvt, wrest, wgt, wgc)


def _diff_attn_kernel(lam_init, lamv_ref, gain_ref, q_ref, k_ref, vt_ref, o_ref, acc_s, m_s):
    qi = pl.program_id(2)
    tq = q_ref.shape[1]
    lane = lax.broadcasted_iota(jnp.int32, (1, LANES), 1)
    first = lane < DA_HEAD_DIM
    q = q_ref[0]
    zero = jnp.zeros_like(q)
    qm = (jnp.where(first, q, zero), jnp.where(first, zero, q))
    acc_s[...] = jnp.zeros_like(acc_s)
    m_s[...] = jnp.full(m_s.shape, NEG, F32)

    def step(j, nblk, masked):
        tk = nblk * tq
        kb = k_ref[0, pl.ds(pl.multiple_of(j * tq, tq), tk), :]
        vtb = vt_ref[0, 0, j] if nblk == 1 else jnp.concatenate([vt_ref[0, 0, j + b] for b in range(nblk)], axis=1)
        for c in range(2):
            st = lax.dot_general(kb, qm[c], (((1,), (1,)), ((), ())), preferred_element_type=F32)
            if masked:
                key_i = lax.broadcasted_iota(jnp.int32, (tk, tq), 0) - (tk - tq)
                qry_i = lax.broadcasted_iota(jnp.int32, (tk, tq), 1)
                st = jnp.where(key_i <= qry_i, st, NEG)
            m_prev = m_s[c]
            m_new = jnp.maximum(m_prev, jnp.max(st, axis=0, keepdims=True))
            alpha = jnp.exp2(m_prev - m_new)
            p = jnp.exp2(st - m_new).astype(BF16)
            acc_s[c] = alpha * acc_s[c] + jnp.dot(vtb, p, preferred_element_type=F32)
            m_s[c] = m_new

    def body(jj, carry):
        step(2 * jj, 2, False)
        return carry

    lax.fori_loop(0, qi // 2, body, 0)

    @pl.when(qi % 2 == 1)
    def _():
        step(qi - 1, 2, True)

    @pl.when(qi % 2 == 0)
    def _():
        step(qi, 1, True)

    outs = []
    for c in range(2):
        acc = acc_s[c]
        outs.append(acc[:DA_V_DIM] / acc[DA_V_DIM:DA_V_DIM + 1])

    lamv = lamv_ref[...]
    lam = (jnp.exp(jnp.sum(lamv[0:1] * lamv[1:2], axis=1, keepdims=True))
           - jnp.exp(jnp.sum(lamv[2:3] * lamv[3:4], axis=1, keepdims=True)) + lam_init)
    ot = outs[0] - lam * outs[1]
    ms = jnp.mean(ot * ot, axis=0, keepdims=True)
    ot = ot * (lax.rsqrt(ms + LN_EPS) * (1.0 - lam_init))
    o_ref[0] = (ot.T * gain_ref[...]).astype(o_ref.dtype)


def _diff_attn(q, k, vt, lamv, gain, lam_init):
    bsz, s, _ = q.shape
    tq = TQ
    nq = s // tq
    return pl.pallas_call(
        functools.partial(_diff_attn_kernel, lam_init),
        grid=(bsz, DA_HEADS, nq),
        in_specs=[pl.BlockSpec((4, DA_HEAD_DIM), lambda b, h, i: (0, 0)),
                  pl.BlockSpec((1, DA_V_DIM), lambda b, h, i: (0, 0)),
                  pl.BlockSpec((1, tq, DA_V_DIM), lambda b, h, i: (b, i, h)),
                  pl.BlockSpec((1, s, DA_V_DIM), lambda b, h, i: (b, 0, h)),
                  pl.BlockSpec((1, 1, nq, VT_ROWS, tq), lambda b, h, i: (b, h, 0, 0, 0))],
        out_specs=pl.BlockSpec((1, tq, DA_V_DIM), lambda b, h, i: (b, i, h)),
        out_shape=jax.ShapeDtypeStruct((bsz, s, DA_WIDTH), BF16),
        scratch_shapes=[pltpu.VMEM((2, VT_ROWS, tq), F32), pltpu.VMEM((2, 1, tq), F32)],
        compiler_params=_cparams(("arbitrary", "arbitrary", "arbitrary")),
        name="diff_attn",
    )(lamv, gain.reshape(1, DA_V_DIM), q, k, vt)


def _log_sigmoid(x):
    return jnp.minimum(x, 0.0) - jnp.log(1.0 + jnp.exp(-jnp.abs(x)))


def _mlstm_kernel(x_ref, v_ref, o_ref, gt_ref, gc_ref, cw_ref, cb_ref, wq_ref, wkt_ref, gbt_ref, gbc_ref,
                  ng_ref, hmean_ref, y_ref, xc_s, c_s, m_s):
    nb, s = x_ref.shape[0], x_ref.shape[1]
    L = ML_CHUNK
    H, dh = ML_HEADS, ML_HEAD_DIM
    nc = s // L
    cw = cw_ref[...]
    row = lax.broadcasted_iota(jnp.int32, (s, 1), 0)
    for bi in range(nb):
        x = x_ref[bi].astype(F32)
        xc = x * cw[ML_CONV - 1:ML_CONV]
        for j in range(1, ML_CONV):
            xs = jnp.where(row >= j, pltpu.roll(x, j, 0), 0.0)
            xc = xc + xs * cw[ML_CONV - 1 - j:ML_CONV - j]
        xc = xc + cb_ref[...]
        xc_s[bi] = (xc * _sigmoid(xc)).astype(BF16)

    c_s[...] = jnp.zeros_like(c_s)
    m_s[...] = jnp.full(m_s.shape, NEG, F32)

    ri = lax.broadcasted_iota(jnp.int32, (L, L), 0)
    ci = lax.broadcasted_iota(jnp.int32, (L, L), 1)
    causal = ci <= ri
    tril = causal.astype(F32)
    triu = (ri <= ci).astype(F32)
    lane = lax.broadcasted_iota(jnp.int32, (1, dh), 1)
    one_hot0 = jnp.broadcast_to((lane == 0).astype(BF16), (L, dh))

    def chunk_one(bi, ci_, t0):
        xcc = xc_s[bi, pl.ds(t0, L), :]
        qc = jnp.dot(xcc, wq_ref[...], preferred_element_type=F32).astype(BF16)
        ktc = lax.dot_general(wkt_ref[...], xcc, (((1,), (1,)), ((), ())),
                              preferred_element_type=F32)
        g_rows = gt_ref[bi, ci_] + gbt_ref[...]
        g_cols = gc_ref[bi, pl.ds(t0, L), :] + gbc_ref[...]
        lf_rows = _log_sigmoid(g_rows)
        lf_cols = _log_sigmoid(g_cols)
        b_rows = jnp.dot(lf_rows, triu, preferred_element_type=F32, precision=HIGHEST)
        b_cols = jnp.dot(tril, lf_cols, preferred_element_type=F32, precision=HIGHEST)
        vch = v_ref[bi, pl.ds(t0, L), :]
        och = o_ref[bi, pl.ds(t0, L), :].astype(F32)
        hs = []
        for h in range(H):
            br = b_rows[H + h:H + h + 1, :]
            ir = g_rows[h:h + 1, :]
            bc = b_cols[:, H + h:H + h + 1]
            m_prev = m_s[bi, h]
            log_d = jnp.where(causal, bc - br + ir, NEG)
            inter = bc + m_prev
            mx = jnp.maximum(inter, jnp.max(log_d, axis=1, keepdims=True))
            dmat = jnp.exp(log_d - mx)
            dec = jnp.exp(inter - mx)
            qh = qc[:, h * dh:(h + 1) * dh]
            kth = ktc[h * dh:(h + 1) * dh, :]
            vaug = jnp.concatenate([vch[:, h * dh:(h + 1) * dh], one_hot0], axis=1)
            sm = (jnp.dot(qh, kth.astype(BF16), preferred_element_type=F32) * dmat).astype(BF16)
            c_prev = c_s[bi, h]
            na = (jnp.dot(sm, vaug, preferred_element_type=F32)
                  + dec * jnp.dot(qh, c_prev.astype(BF16), preferred_element_type=F32))
            den = na[:, dh:dh + 1]
            hs.append(na[:, :dh] / jnp.maximum(jnp.abs(den), jnp.exp(-mx)))
            g_tot = br[:, L - 1:L]
            a_row = g_tot - br + ir
            m_new = jnp.maximum(g_tot + m_prev, jnp.max(a_row, axis=1, keepdims=True))
            decay = jnp.exp(g_tot + m_prev - m_new)
            w_row = jnp.exp(a_row - m_new)
            kw = (kth * w_row).astype(BF16)
            c_s[bi, h] = decay * c_prev + jnp.dot(kw, vaug, preferred_element_type=F32)
            m_s[bi, h] = m_new
        hcat = jnp.concatenate(hs, axis=1)
        ms = jnp.dot(hcat * hcat, hmean_ref[...], preferred_element_type=F32, precision=HIGHEST)
        y = hcat * lax.rsqrt(ms + LN_EPS) * ng_ref[...] * _sigmoid(och)
        y_ref[bi, pl.ds(t0, L), :] = y.astype(y_ref.dtype)

    def chunk(ci_, _):
        t0 = pl.multiple_of(ci_ * L, L)
        for bi in range(nb):
            chunk_one(bi, ci_, t0)
        return 0

    lax.fori_loop(0, nc, chunk, 0)


def _mlstm(mlx, mlv, mlo, g_t, g_c, conv_w, conv_b, w_q, w_k, gate_b, norm_g):
    bsz, s, _ = mlx.shape
    H, dh = ML_HEADS, ML_HEAD_DIM
    eye = jnp.eye(H, dtype=F32)
    wq_bd = jnp.einsum('hde,hg->hdge', w_q, eye).reshape(ML_WIDTH, ML_WIDTH).astype(BF16)
    wk_bd = jnp.einsum('hde,hg->hdge', w_k * (dh ** -0.5), eye).reshape(ML_WIDTH, ML_WIDTH)
    wkt_bd = wk_bd.T.astype(BF16)
    gbt = gate_b.reshape(GATE_PAD, 1)
    gbc = jnp.pad(gate_b.reshape(1, GATE_PAD), ((0, 0), (0, LANES - GATE_PAD)))
    hmean = jnp.kron(eye, jnp.full((dh, dh), 1.0 / dh, F32))
    nc = s // ML_CHUNK
    g_t4 = g_t.reshape(bsz, GATE_PAD, nc, ML_CHUNK).transpose(0, 2, 1, 3)
    tok = lambda b: (b, 0, 0)
    c2 = lambda b: (0, 0)
    nb = ML_NB
    return pl.pallas_call(
        _mlstm_kernel,
        grid=(bsz // nb,),
        in_specs=[pl.BlockSpec((nb, s, ML_WIDTH), tok),
                  pl.BlockSpec((nb, s, ML_WIDTH), tok),
                  pl.BlockSpec((nb, s, ML_WIDTH), tok),
                  pl.BlockSpec((nb, nc, GATE_PAD, ML_CHUNK), lambda b: (b, 0, 0, 0)),
                  pl.BlockSpec((nb, s, LANES), tok),
                  pl.BlockSpec((ML_CONV, ML_WIDTH), c2),
                  pl.BlockSpec((1, ML_WIDTH), c2),
                  pl.BlockSpec((ML_WIDTH, ML_WIDTH), c2),
                  pl.BlockSpec((ML_WIDTH, ML_WIDTH), c2),
                  pl.BlockSpec((GATE_PAD, 1), c2),
                  pl.BlockSpec((1, LANES), c2),
                  pl.BlockSpec((1, ML_WIDTH), c2),
                  pl.BlockSpec((ML_WIDTH, ML_WIDTH), c2)],
        out_specs=pl.BlockSpec((nb, s, ML_WIDTH), tok),
        out_shape=jax.ShapeDtypeStruct((bsz, s, ML_WIDTH), BF16),
        scratch_shapes=[pltpu.VMEM((nb, s, ML_WIDTH), BF16),
                        pltpu.VMEM((nb, H, dh, LANES), F32),
                        pltpu.VMEM((nb, H, 1, 1), F32)],
        compiler_params=_cparams(("arbitrary",)),
        name="mlstm",
    )(mlx, mlv, mlo, g_t4, g_c, conv_w, conv_b.reshape(1, ML_WIDTH), wq_bd, wkt_bd, gbt, gbc,
      norm_g.reshape(1, ML_WIDTH), hmean)


def _gelu_tanh(x):
    return 0.5 * x * (1.0 + jnp.tanh(math.sqrt(2.0 / math.pi) * (x + 0.044715 * (x * x * x))))


def _s5_kernel(u_ref, are_ref, aim_ref, bcat_ref, ccat_ref, d_ref, wglu_ref, y_ref, xs_s, st_s):
    tc, bsz, w = u_ref.shape
    n = S5_NSTATE

    @pl.when(pl.program_id(0) == 0)
    def _():
        st_s[...] = jnp.zeros_like(st_s)

    u = u_ref[...].reshape(tc * bsz, w)
    xs_s[...] = jnp.dot(u.astype(BF16), bcat_ref[...], preferred_element_type=F32).reshape(tc, bsz, 2 * n)
    a_re = jnp.broadcast_to(are_ref[...], (bsz, n))
    a_im = jnp.broadcast_to(aim_ref[...], (bsz, n))

    def step(t, carry):
        x_re, x_im = carry
        bu = xs_s[t]
        n_re = a_re * x_re - a_im * x_im + bu[:, :n]
        n_im = a_re * x_im + a_im * x_re + bu[:, n:]
        xs_s[t] = jnp.concatenate([n_re, n_im], axis=1)
        return n_re, n_im

    x_re, x_im = lax.fori_loop(0, tc, step, (st_s[0], st_s[1]), unroll=S5_UNROLL)
    st_s[0] = x_re
    st_s[1] = x_im

    xs = xs_s[...].reshape(tc * bsz, 2 * n).astype(BF16)
    y = jnp.dot(xs, ccat_ref[...], preferred_element_type=F32) + d_ref[...] * u
    z = jnp.dot(_gelu_tanh(y).astype(BF16), wglu_ref[...], preferred_element_type=F32)
    out = z[:, :w] * _sigmoid(z[:, w:])
    y_ref[...] = out.reshape(tc, bsz, w).astype(y_ref.dtype)


def _s5_params(a_re, a_im, log_dt, b_re, b_im, c_re, c_im, w_glu):
    G, P, Hc = S5_GROUPS, S5_STATE, S5_GROUP
    dt = jnp.exp(log_dt)[:, None]
    mag = jnp.exp(a_re * dt)
    ab_re = mag * jnp.cos(a_im * dt)
    ab_im = mag * jnp.sin(a_im * dt)
    nr, ni = ab_re - 1.0, ab_im
    den = a_re * a_re + a_im * a_im
    fr = (nr * a_re + ni * a_im) / den
    fi = (ni * a_re - nr * a_im) / den
    bb_re = fr[..., None] * b_re - fi[..., None] * b_im
    bb_im = fr[..., None] * b_im + fi[..., None] * b_re
    eye = jnp.eye(G, dtype=F32)
    bd = lambda t, sub: jnp.einsum(sub, t, eye)
    bre = bd(bb_re, 'gph,gk->ghkp').reshape(G * Hc, G * P)
    bim = bd(bb_im, 'gph,gk->ghkp').reshape(G * Hc, G * P)
    bcat = jnp.concatenate([bre, bim], axis=1).astype(BF16)
    cre = bd(c_re, 'ghp,gk->gpkh').reshape(G * P, G * Hc)
    cim = bd(c_im, 'ghp,gk->gpkh').reshape(G * P, G * Hc)
    ccat = jnp.concatenate([cre, -cim], axis=0).astype(BF16)
    wv = bd(w_glu[:, :, :Hc], 'ghj,gk->ghkj').reshape(G * Hc, G * Hc)
    wg = bd(w_glu[:, :, Hc:], 'ghj,gk->ghkj').reshape(G * Hc, G * Hc)
    wglu = jnp.concatenate([wv, wg], axis=1).astype(BF16)
    return ab_re.reshape(1, G * P), ab_im.reshape(1, G * P), bcat, ccat, wglu


def _s5(u_tm, bsz, params, d_skip):
    s = u_tm.shape[0]
    w = S5_WIDTH
    n = S5_NSTATE
    are, aim, bcat, ccat, wglu = params
    u3 = u_tm.reshape(s, bsz, w)
    tc = S5_TC
    c2 = lambda i: (0, 0)
    y = pl.pallas_call(
        _s5_kernel,
        grid=(s // tc,),
        in_specs=[pl.BlockSpec((tc, bsz, w), lambda i: (i, 0, 0)),
                  pl.BlockSpec((1, n), c2),
                  pl.BlockSpec((1, n), c2),
                  pl.BlockSpec((w, 2 * n), c2),
                  pl.BlockSpec((2 * n, w), c2),
                  pl.BlockSpec((1, w), c2),
                  pl.BlockSpec((w, 2 * w), c2)],
        out_specs=pl.BlockSpec((tc, bsz, w), lambda i: (i, 0, 0)),
        out_shape=jax.ShapeDtypeStruct((s, bsz, w), F32),
        scratch_shapes=[pltpu.VMEM((tc, bsz, 2 * n), F32),
                        pltpu.VMEM((2, bsz, n), F32)],
        compiler_params=_cparams(("arbitrary",)),
        name="s5",
    )(u3, are, aim, bcat, ccat, d_skip.reshape(1, w), wglu)
    return y.reshape(s, bsz * w)


def _layer_norm(z, g, b):
    mu = jnp.mean(z, axis=1, keepdims=True)
    zc = z - mu
    var = jnp.mean(zc * zc, axis=1, keepdims=True)
    return zc * lax.rsqrt(var + LN_EPS) * g + b


def _out_proj_kernel(yda_ref, yml_ref, ys5_ref, x_ref, gate_ref, lng_ref, lnb_ref, shift_ref, scale_ref,
                     wout_ref, wrt_ref, brt_ref, x1_ref, h2_ref, eid_ref, prob_ref):
    y = jnp.dot(yda_ref[0], wout_ref[0:DA_WIDTH, :], preferred_element_type=F32)
    y = y + jnp.dot(yml_ref[0], wout_ref[DA_WIDTH:DA_WIDTH + ML_WIDTH, :], preferred_element_type=F32)
    y = y + jnp.dot(ys5_ref[...].astype(BF16), wout_ref[DA_WIDTH + ML_WIDTH:, :], preferred_element_type=F32)
    x1 = _layer_norm(DN_ALPHA * x_ref[0] + (1.0 + gate_ref[0]) * y, lng_ref[...], lnb_ref[...])
    x1_ref[0] = x1
    h2 = x1 * (1.0 + scale_ref[0]) + shift_ref[0]
    h2_ref[...] = _pack_bf16_pairs(h2)
    h_hi = h2.astype(BF16)
    h_lo = (h2 - h_hi.astype(F32)).astype(BF16)
    nt_dot = lambda a, b: lax.dot_general(a, b, (((1,), (1,)), ((), ())), preferred_element_type=F32)
    by_hi = nt_dot(wrt_ref[...], h_hi)
    logits = (by_hi[:N_EXPERTS] + by_hi[N_EXPERTS:] + nt_dot(wrt_ref[0:N_EXPERTS, :], h_lo)
              + brt_ref[...])
    eidx = lax.broadcasted_iota(jnp.int32, logits.shape, 0)
    vals, ids = [], []
    for _ in range(TOP_K):
        mx = jnp.max(logits, axis=0, keepdims=True)
        sel = jnp.min(jnp.where(logits == mx, eidx, N_EXPERTS), axis=0, keepdims=True)
        vals.append(mx)
        ids.append(sel)
        logits = jnp.where(eidx == sel, -jnp.inf, logits)
    ex = [jnp.exp(v - vals[0]) for v in vals]
    tot = ex[0] + ex[1] + ex[2] + ex[3]
    zi = jnp.zeros_like(ids[0])
    eid_ref[...] = jnp.concatenate(ids + [zi] * (SUBLANES - TOP_K), axis=0)
    zf = jnp.zeros((LANES - TOP_K, tot.shape[1]), F32)
    prob_ref[...] = jnp.concatenate([e / tot for e in ex] + [zf], axis=0).T


def _out_proj(y_da, y_ml, y_s5, x, gate, ln_g, ln_b, shift2, scale2, w_out_l, w_router_l, b_router_l):
    bsz, s, d = x.shape
    tm = TM_PROJ
    nt = s // tm
    tok = lambda b, i: (b, i, 0)
    per_b = lambda b, i: (b, 0, 0)
    c2 = lambda b, i: (0, 0)
    r3 = lambda a: a.reshape(bsz, 1, d)
    flat = lambda b, i: (0, b * nt + i)
    wr_t = w_router_l.T
    wr_hi = wr_t.astype(BF16)
    wr_lo = (wr_t - wr_hi.astype(F32)).astype(BF16)
    return pl.pallas_call(
        _out_proj_kernel,
        grid=(bsz, nt),
        in_specs=[pl.BlockSpec((1, tm, DA_WIDTH), tok),
                  pl.BlockSpec((1, tm, ML_WIDTH), tok),
                  pl.BlockSpec((tm, S5_WIDTH), lambda b, i: (i, b)),
                  pl.BlockSpec((1, tm, d), tok),
                  pl.BlockSpec((1, 1, d), per_b),
                  pl.BlockSpec((1, d), c2),
                  pl.BlockSpec((1, d), c2),
                  pl.BlockSpec((1, 1, d), per_b),
                  pl.BlockSpec((1, 1, d), per_b),
                  pl.BlockSpec((d, d), c2),
                  pl.BlockSpec((2 * N_EXPERTS, d), c2),
                  pl.BlockSpec((N_EXPERTS, 1), c2)],
        out_specs=[pl.BlockSpec((1, tm, d), tok),
                   pl.BlockSpec((tm, d // 2), lambda b, i: (b * nt + i, 0)),
                   pl.BlockSpec((SUBLANES, tm), flat),
                   pl.BlockSpec((tm, LANES), lambda b, i: (b * nt + i, 0))],
        out_shape=[jax.ShapeDtypeStruct((bsz, s, d), F32),
                   jax.ShapeDtypeStruct((bsz * s, d // 2), jnp.int32),
                   jax.ShapeDtypeStruct((SUBLANES, bsz * s), jnp.int32),
                   jax.ShapeDtypeStruct((bsz * s, LANES), F32)],
        compiler_params=_cparams(("arbitrary", "arbitrary")),
        name="out_proj",
    )(y_da, y_ml, y_s5, x, r3(gate), ln_g.reshape(1, d), ln_b.reshape(1, d), r3(shift2), r3(scale2),
      w_out_l.astype(BF16), jnp.concatenate([wr_hi, wr_lo], axis=0), b_router_l.reshape(N_EXPERTS, 1))


META_END, META_PAD, META_CNT = 0, 1, 2


def _route_kernel(eid_ref, pos_ref, meta_ref, carry_s, start_s):
    phase = pl.program_id(0)
    i = pl.program_id(1)
    tb = eid_ref.shape[1]
    ntp = meta_ref.shape[1]
    tm = TM_MOE

    @pl.when(jnp.logical_and(phase == 0, i == 0))
    def _():
        carry_s[...] = jnp.zeros_like(carry_s)

    eid = eid_ref[...]
    eidx = lax.broadcasted_iota(jnp.int32, (N_EXPERTS, tb), 0)
    hot = [eidx == eid[k:k + 1, :] for k in range(TOP_K)]
    member = jnp.zeros((N_EXPERTS, tb), F32)
    for k in range(TOP_K):
        member = member + hot[k].astype(F32)
    total = jnp.sum(member, axis=1, keepdims=True)

    @pl.when(phase == 0)
    def _():
        carry_s[...] = carry_s[...] + total

    @pl.when(jnp.logical_and(phase == 1, i == 0))
    def _():
        cnt = carry_s[...]
        padded = jnp.floor((cnt + (tm - 1)) * (1.0 / tm)) * tm
        er = lax.broadcasted_iota(jnp.int32, (N_EXPERTS, N_EXPERTS), 0)
        ec = lax.broadcasted_iota(jnp.int32, (N_EXPERTS, N_EXPERTS), 1)
        ends = jnp.dot((ec <= er).astype(F32), padded, preferred_element_type=F32, precision=HIGHEST)
        start_s[...] = ends - padded
        carry_s[...] = jnp.zeros_like(carry_s)
        lane = lax.broadcasted_iota(jnp.int32, (N_EXPERTS, ntp), 1)
        sub = lax.broadcasted_iota(jnp.int32, (N_EXPERTS, ntp), 0)
        diag = lane == sub

        def as_row(col):
            return jnp.sum(jnp.where(diag, col, 0.0), axis=0, keepdims=True)

        zero = jnp.zeros((SUBLANES - 3, ntp), F32)
        meta_ref[...] = jnp.concatenate([as_row(ends[:, 0:1]), as_row(padded[:, 0:1]), as_row(cnt[:, 0:1]), zero],
                                        axis=0).astype(jnp.int32)

    @pl.when(phase == 1)
    def _():
        ri = lax.broadcasted_iota(jnp.int32, (tb, tb), 0)
        ci = lax.broadcasted_iota(jnp.int32, (tb, tb), 1)
        triu = (ri <= ci).astype(BF16)
        incl = jnp.dot(member.astype(BF16), triu, preferred_element_type=F32)
        slot = incl - member + carry_s[:, 0:1] + start_s[:, 0:1]
        rows = [jnp.sum(jnp.where(hot[k], slot, 0.0), axis=0, keepdims=True) for k in range(TOP_K)]
        zr = jnp.zeros_like(rows[0])
        pos_ref[...] = jnp.concatenate(rows + [zr] * (SUBLANES - TOP_K), axis=0).astype(jnp.int32)
        carry_s[...] = carry_s[...] + total


def _route(eid):
    t = eid.shape[1]
    tb = TB_RANK
    ntp = LANES
    pos8, meta = pl.pallas_call(
        _route_kernel,
        grid=(2, t // tb),
        in_specs=[pl.BlockSpec((SUBLANES, tb), lambda p, i: (0, i))],
        out_specs=[pl.BlockSpec((SUBLANES, tb), lambda p, i: (0, i * p)),
                   pl.BlockSpec((SUBLANES, ntp), lambda p, i: (0, 0))],
        out_shape=[jax.ShapeDtypeStruct((SUBLANES, t), jnp.int32),
                   jax.ShapeDtypeStruct((SUBLANES, ntp), jnp.int32)],
        scratch_shapes=[pltpu.VMEM((N_EXPERTS, LANES), F32), pltpu.VMEM((N_EXPERTS, LANES), F32)],
        compiler_params=_cparams(("arbitrary", "arbitrary")),
        name="route",
    )(eid)
    return pos8, meta


def _sc_workers():
    info = plsc.get_sparse_core_info()
    return info.num_cores, info.num_cores * info.num_subcores


def _dispatch(h2, pos8, n_rows):
    t, d = h2.shape
    n_cores, n_workers = _sc_workers()
    tpw = t // n_workers
    ch = SC_CHUNK
    mesh = plsc.VectorSubcoreMesh(core_axis_name="c", subcore_axis_name="s")

    @functools.partial(
        pl.kernel, mesh=mesh,
        out_type=jax.ShapeDtypeStruct((n_rows, d), h2.dtype),
        scratch_types=[pltpu.VMEM((ch,), jnp.int32)] * TOP_K + [pltpu.VMEM((ch, d), h2.dtype),
                                                                pltpu.SemaphoreType.DMA])
    def scatter_rows(h_hbm, pos_hbm, out_hbm, i0, i1, i2, i3, rows_v, sem):
        idx = (i0, i1, i2, i3)
        base = (lax.axis_index("s") * n_cores + lax.axis_index("c")) * tpw

        @pl.loop(0, tpw // ch)
        def _(i):
            off = base + i * ch
            pltpu.sync_copy(h_hbm.at[pl.ds(off, ch)], rows_v)
            for k in range(TOP_K):
                pltpu.sync_copy(pos_hbm.at[k, pl.ds(off, ch)], idx[k])
            copies = [pltpu.async_copy(rows_v, out_hbm.at[idx[k]], sem) for k in range(TOP_K)]
            for cp in copies:
                cp.wait()

    return scatter_rows(h2, pos8)


def _gather_expert_rows(ys, pos8):
    _, d = ys.shape
    t = pos8.shape[1]
    n_cores, n_workers = _sc_workers()
    tpw = t // n_workers
    ch = SC_CHUNK
    mesh = plsc.VectorSubcoreMesh(core_axis_name="c", subcore_axis_name="s")

    @functools.partial(
        pl.kernel, mesh=mesh,
        out_type=jax.ShapeDtypeStruct((TOP_K, t, d), ys.dtype),
        scratch_types=[pltpu.VMEM((ch,), jnp.int32), pltpu.VMEM((ch, d), ys.dtype), pltpu.SemaphoreType.DMA])
    def gather_rows(ys_hbm, pos_hbm, out_hbm, idx_v, rows_v, sem):
        base = (lax.axis_index("s") * n_cores + lax.axis_index("c")) * tpw

        @pl.loop(0, tpw // ch)
        def _(i):
            off = base + i * ch
            for k in range(TOP_K):
                pltpu.sync_copy(pos_hbm.at[k, pl.ds(off, ch)], idx_v)
                pltpu.async_copy(ys_hbm.at[idx_v], rows_v, sem).wait()
                pltpu.sync_copy(rows_v, out_hbm.at[k, pl.ds(off, ch)])

    return gather_rows(ys, pos8)


def _pack_bf16_pairs(a):
    n = a.shape[1] // 2
    lo = pltpu.bitcast(a[:, :n].astype(BF16).astype(F32), jnp.int32)
    hi = pltpu.bitcast(a[:, n:].astype(BF16).astype(F32), jnp.int32)
    return jnp.bitwise_or(jnp.bitwise_and(hi, -65536), jnp.bitwise_and(lax.shift_right_logical(lo, 16), 65535))


def _unpack_bf16_pairs(p):
    lo = pltpu.bitcast(lax.shift_left(p, 16), F32)
    hi = pltpu.bitcast(jnp.bitwise_and(p, -65536), F32)
    return lo, hi


def _expert_kernel(meta_ref, xs_ref, wup_ref, bup_ref, wdn_ref, bdn_ref, ys_ref,
                   wup_s, wdn_s, xbuf, obuf, in_sem, out_sem):
    e = pl.program_id(0)
    tm = TM_MOE
    half = D_MODEL // 2
    pad = meta_ref[META_PAD, e]
    n_t = pad // tm
    row0 = meta_ref[META_END, e] - pad
    cnt = meta_ref[META_CNT, e]

    def rows(i):
        return pl.ds(pl.multiple_of(row0 + i * tm, tm), tm)

    def x_copy(i, slot):
        return pltpu.make_async_copy(xs_ref.at[rows(i)], xbuf.at[slot], in_sem.at[slot])

    def y_copy(i, slot):
        return pltpu.make_async_copy(obuf.at[slot], ys_ref.at[rows(i)], out_sem.at[slot])

    @pl.when(n_t > 0)
    def _():
        x_copy(0, 0).start()
        wup_s[...] = wup_ref[0, 0].astype(BF16)
        wdn_s[...] = wdn_ref[0, 0].astype(BF16)

        def tile(i, carry):
            slot = lax.rem(i, 2)
            x_copy(i, slot).wait()

            @pl.when(i + 1 < n_t)
            def _():
                x_copy(i + 1, 1 - slot).start()

            @pl.when(i >= 2)
            def _():
                y_copy(i - 2, slot).wait()

            row = lax.broadcasted_iota(jnp.int32, (tm, 1), 0)
            lo, hi = _unpack_bf16_pairs(jnp.where(row < cnt - i * tm, xbuf[slot], 0))
            z = (jnp.dot(lo.astype(BF16), wup_s[0:half, :], preferred_element_type=F32)
                 + jnp.dot(hi.astype(BF16), wup_s[half:, :], preferred_element_type=F32) + bup_ref[0, 0])
            glu = jnp.minimum(z[:, :D_EXPERT], SWIGLU_LIMIT)
            lin = jnp.clip(z[:, D_EXPERT:], -SWIGLU_LIMIT, SWIGLU_LIMIT)
            act = (glu * _sigmoid(SWIGLU_ALPHA * glu) * (lin + 1.0)).astype(BF16)
            y = jnp.dot(act, wdn_s[...], preferred_element_type=F32) + bdn_ref[0, 0]
            obuf[slot] = _pack_bf16_pairs(y)
            y_copy(i, slot).start()
            return carry

        lax.fori_loop(0, n_t, tile, 0)

        @pl.when(n_t >= 2)
        def _():
            y_copy(n_t - 2, lax.rem(n_t, 2)).wait()

        y_copy(n_t - 1, lax.rem(n_t - 1, 2)).wait()

    @pl.when(e == N_EXPERTS - 1)
    def _():
        obuf[0] = jnp.zeros((tm, half), jnp.int32)

        def fill(i, carry):
            cp = pltpu.make_async_copy(obuf.at[0], ys_ref.at[pl.ds(pl.multiple_of(i * tm, tm), tm)], out_sem.at[0])
            cp.start()
            cp.wait()
            return carry

        lax.fori_loop(meta_ref[META_END, N_EXPERTS - 1] // tm, ys_ref.shape[0] // tm, fill, 0)


def _expert_mlp(xs, meta, layer, w_up, b_up, w_down, b_down):
    n_rows, half = xs.shape
    d = 2 * half
    tm = TM_MOE
    f = w_up.shape[-1]
    b_up4 = b_up.reshape(DEPTH, N_EXPERTS, 1, f)
    b_dn4 = b_down.reshape(DEPTH, N_EXPERTS, 1, d)
    wsel = lambda e, m: (layer, e, 0, 0)
    grid_spec = pltpu.PrefetchScalarGridSpec(
        num_scalar_prefetch=1,
        grid=(N_EXPERTS,),
        in_specs=[pl.BlockSpec(memory_space=pl.ANY),
                  pl.BlockSpec((1, 1, d, f), wsel),
                  pl.BlockSpec((1, 1, 1, f), wsel),
                  pl.BlockSpec((1, 1, f // 2, d), wsel),
                  pl.BlockSpec((1, 1, 1, d), wsel)],
        out_specs=pl.BlockSpec(memory_space=pl.ANY),
        scratch_shapes=[pltpu.VMEM((d, f), BF16), pltpu.VMEM((f // 2, d), BF16),
                        pltpu.VMEM((2, tm, half), jnp.int32), pltpu.VMEM((2, tm, half), jnp.int32),
                        pltpu.SemaphoreType.DMA((2,)), pltpu.SemaphoreType.DMA((2,))],
    )
    return pl.pallas_call(
        _expert_kernel,
        grid_spec=grid_spec,
        out_shape=jax.ShapeDtypeStruct((n_rows, half), jnp.int32),
        compiler_params=_cparams(("arbitrary",)),
        name="expert_mlp",
    )(meta, xs, w_up, b_up4, w_down, b_dn4)


def _combine_kernel(rows_ref, prob_ref, x_ref, gate_ref, lng_ref, lnb_ref, o_ref):
    p = prob_ref[...]
    y = None
    for k in range(TOP_K):
        yk = p[:, k:k + 1] * jnp.concatenate(_unpack_bf16_pairs(rows_ref[k]), axis=1)
        y = yk if y is None else y + yk
    o_ref[0] = _layer_norm(DN_ALPHA * x_ref[0] + (1.0 + gate_ref[0]) * y, lng_ref[...], lnb_ref[...])


def _combine(rows, prob_c, x1, gate, ln_g, ln_b, part):
    bsz, s, d = x1.shape
    tm = TM_DISP
    nt = s // tm
    pb = bsz // MOE_PARTS
    b0 = part * pb
    return pl.pallas_call(
        _combine_kernel,
        grid=(pb, nt),
        in_specs=[pl.BlockSpec((TOP_K, tm, d // 2), lambda b, i: (0, b * nt + i, 0)),
                  pl.BlockSpec((tm, LANES), lambda b, i: ((b0 + b) * nt + i, 0)),
                  pl.BlockSpec((1, tm, d), lambda b, i: (b0 + b, i, 0)),
                  pl.BlockSpec((1, 1, d), lambda b, i: (b0 + b, 0, 0)),
                  pl.BlockSpec((1, d), lambda b, i: (0, 0)),
                  pl.BlockSpec((1, d), lambda b, i: (0, 0))],
        out_specs=pl.BlockSpec((1, tm, d), lambda b, i: (b0 + b, i, 0)),
        out_shape=jax.ShapeDtypeStruct((bsz, s, d), F32),
        input_output_aliases={2: 0},
        compiler_params=_cparams(("arbitrary", "arbitrary")),
        name="combine",
    )(rows, prob_c, x1, gate.reshape(bsz, 1, d), ln_g.reshape(1, d), ln_b.reshape(1, d))


def kernel(x, c, positions, ada_w, ada_b, w_in, lam_q1, lam_k1, lam_q2, lam_k2, da_norm_g, ml_conv_w, ml_conv_b,
           ml_w_q, ml_w_k, ml_gate_b, ml_norm_g, s5_a_re, s5_a_im, s5_log_dt, s5_b_re, s5_b_im, s5_c_re, s5_c_im,
           s5_d, s5_w_glu, w_out, ln_g, ln_b, w_router, b_router, w_up, b_up, w_down, b_down):
    bsz, s, d = x.shape
    t = bsz * s
    n_tiles_max = (t * TOP_K) // TM_MOE + N_EXPERTS
    n_rows = n_tiles_max * TM_MOE
    mod = _modulation(c, ada_w, ada_b)
    cos_t, sin_t = _rope_tables(positions)
    for l in range(DEPTH):
        shift, scale, gate = jnp.split(mod[2 * l], 3, axis=-1)
        q, k, v, mlx, mlv, mlo, g_t, g_c, s5u = _in_proj(x, shift, scale, cos_t, sin_t, w_in[l])
        lam_init = 0.8 - 0.6 * math.exp(-0.3 * l)
        lamv = jnp.stack([lam_q1[l], lam_k1[l], lam_q2[l], lam_k2[l]])
        y_da = _diff_attn(q, k, v, lamv, da_norm_g[l], lam_init)
        y_ml = _mlstm(mlx, mlv, mlo, g_t, g_c, ml_conv_w[l], ml_conv_b[l], ml_w_q[l], ml_w_k[l],
                      ml_gate_b[l], ml_norm_g[l])
        s5p = _s5_params(s5_a_re[l], s5_a_im[l], s5_log_dt[l], s5_b_re[l], s5_b_im[l], s5_c_re[l], s5_c_im[l],
                         s5_w_glu[l])
        y_s5 = _s5(s5u, bsz, s5p, s5_d[l])
        shift2, scale2, gate2 = jnp.split(mod[2 * l + 1], 3, axis=-1)
        x1, h2, eid, prob = _out_proj(y_da, y_ml, y_s5, x, gate, ln_g[l, 0], ln_b[l, 0], shift2, scale2,
                                      w_out[l], w_router[l], b_router[l])
        pos8, meta = _route(eid)
        xs = _dispatch(h2, pos8, n_rows)
        ys = _expert_mlp(xs, meta, l, w_up, b_up, w_down, b_down)
        tp = t // MOE_PARTS
        x = x1
        for part in range(MOE_PARTS):
            rows = _gather_expert_rows(ys, pos8[:, part * tp:(part + 1) * tp])
            x = _combine(rows, prob, x, gate2, ln_g[l, 1], ln_b[l, 1], part)
    return x
```

```python
import functools
import math

import jax
import jax.numpy as jnp
from jax import lax
from jax.experimental import pallas as pl
from jax.experimental.pallas import tpu as pltpu
from jax.experimental.pallas import tpu_sc as plsc

F32 = jnp.float32
BF16 = jnp.bfloat16
HIGHEST = lax.Precision.HIGHEST

D_MODEL = 1024
DEPTH = 2
DA_HEADS = 4
DA_HEAD_DIM = 64
DA_V_DIM = 2 * DA_HEAD_DIM
DA_WIDTH = DA_HEADS * DA_V_DIM
DA_QK_WIDTH = DA_HEADS * 2 * DA_HEAD_DIM
ROPE_THETA = 10000.0
ML_HEADS = 4
ML_HEAD_DIM = 64
ML_WIDTH = ML_HEADS * ML_HEAD_DIM
ML_CONV = 4
S5_GROUP = 16
S5_STATE = 64
S5_WIDTH = D_MODEL - DA_WIDTH - ML_WIDTH
S5_GROUPS = S5_WIDTH // S5_GROUP
S5_NSTATE = S5_GROUPS * S5_STATE
N_EXPERTS = 32
TOP_K = 4
D_EXPERT = D_MODEL
SWIGLU_LIMIT = 7.0
SWIGLU_ALPHA = 1.702
DN_ALPHA = (2 * DEPTH) ** 0.25
LN_EPS = 1e-5
NEG = -1e30

OFF_DA_K = DA_QK_WIDTH
OFF_DA_V = 2 * DA_QK_WIDTH
OFF_ML_X = OFF_DA_V + DA_WIDTH
OFF_ML_V = OFF_ML_X + ML_WIDTH
OFF_ML_O = OFF_ML_V + ML_WIDTH
OFF_ML_I = OFF_ML_O + ML_WIDTH
OFF_ML_F = OFF_ML_I + ML_HEADS
OFF_S5_U = OFF_ML_F + ML_HEADS
N_IN = OFF_S5_U + S5_WIDTH

LANES = 128
SUBLANES = 8
VMEM_LIMIT_BYTES = 56 * 1024 * 1024

TM_PROJ = 1024
TQ = 512
ML_CHUNK = 256
ML_NB = 2
S5_TC = 256
S5_UNROLL = 8
TB_RANK = 512
TM_MOE = 256
TM_DISP = 256
MOE_PARTS = 2
SC_CHUNK = 128
GATE_PAD = 8
VT_ROWS = DA_V_DIM + 16
Q_PRESCALE = DA_HEAD_DIM ** -0.5 * math.log2(math.e)


def _cparams(sem, vmem=VMEM_LIMIT_BYTES):
    return pltpu.CompilerParams(dimension_semantics=sem, vmem_limit_bytes=vmem)


def _sigmoid(x):
    return 1.0 / (1.0 + jnp.exp(-x))


def _mod_kernel(c_ref, w_ref, b_ref, o_ref):
    c = c_ref[...]
    ca = (c * _sigmoid(c)).astype(BF16)
    w = w_ref[0].astype(BF16)
    o_ref[0] = jnp.dot(ca, w, preferred_element_type=F32) + b_ref[0]


def _modulation(c, ada_w, ada_b):
    nmod = ada_w.shape[0] * ada_w.shape[1]
    bsz, d = c.shape
    e = ada_w.shape[-1]
    tn = 1024
    w = ada_w.reshape(nmod, d, e)
    b = ada_b.reshape(nmod, 1, e)
    return pl.pallas_call(
        _mod_kernel,
        grid=(nmod, e // tn),
        in_specs=[pl.BlockSpec((bsz, d), lambda n, j: (0, 0)),
                  pl.BlockSpec((1, d, tn), lambda n, j: (n, 0, j)),
                  pl.BlockSpec((1, 1, tn), lambda n, j: (n, 0, j))],
        out_specs=pl.BlockSpec((1, bsz, tn), lambda n, j: (n, 0, j)),
        out_shape=jax.ShapeDtypeStruct((nmod, bsz, e), F32),
        compiler_params=_cparams(("arbitrary", "arbitrary")),
        name="modulation",
    )(c, w, b)


def _rope_kernel(pos_ref, cos_ref, sin_ref):
    pos = pos_ref[0].astype(F32)
    lane = lax.broadcasted_iota(jnp.int32, (1, LANES), 1)
    fidx = (lane % (DA_HEAD_DIM // 2)).astype(F32)
    inv = jnp.exp(fidx * (-2.0 * math.log(ROPE_THETA) / DA_HEAD_DIM))
    ang = pos * inv
    sign = jnp.where((lane % DA_HEAD_DIM) < DA_HEAD_DIM // 2, -1.0, 1.0)
    cos_ref[0] = jnp.cos(ang)
    sin_ref[0] = jnp.sin(ang) * sign


def _rope_tables(positions):
    bsz, s = positions.shape
    ts = 512
    pos3 = positions.reshape(bsz, s, 1)
    return pl.pallas_call(
        _rope_kernel,
        grid=(bsz, s // ts),
        in_specs=[pl.BlockSpec((1, ts, 1), lambda b, i: (b, i, 0))],
        out_specs=[pl.BlockSpec((1, ts, LANES), lambda b, i: (b, i, 0))] * 2,
        out_shape=[jax.ShapeDtypeStruct((bsz, s, LANES), F32)] * 2,
        compiler_params=_cparams(("arbitrary", "arbitrary")),
        name="rope_tables",
    )(pos3)


def _in_proj_kernel(x_ref, shift_ref, scale_ref, cos_ref, sin_ref, wqk_ref, wvt_ref, wrest_ref, wgt_ref, wgc_ref,
                    q_ref, k_ref, vt_ref, mlx_ref, mlv_ref, mlo_ref, gt_ref, gc_ref, s5u_ref):
    h = (x_ref[0] * (1.0 + scale_ref[0]) + shift_ref[0]).astype(BF16)
    cos = cos_ref[0]
    sin = sin_ref[0]
    lane = lax.broadcasted_iota(jnp.int32, (1, LANES), 1)
    lo_half = (lane % DA_HEAD_DIM) < DA_HEAD_DIM // 2
    half = DA_HEAD_DIM // 2

    def rope(t):
        fwd = pltpu.roll(t, half, 1)
        bwd = pltpu.roll(t, LANES - half, 1)
        partner = jnp.where(lo_half, bwd, fwd)
        return t * cos + partner * sin

    qk = jnp.dot(h, wqk_ref[...], preferred_element_type=F32)
    nslab = DA_QK_WIDTH // LANES
    for c in range(nslab):
        q_ref[0, :, c * LANES:(c + 1) * LANES] = (
            rope(qk[:, c * LANES:(c + 1) * LANES]) * Q_PRESCALE).astype(BF16)
        k_ref[0, :, c * LANES:(c + 1) * LANES] = rope(
            qk[:, DA_QK_WIDTH + c * LANES:DA_QK_WIDTH + (c + 1) * LANES]).astype(BF16)

    vt = lax.dot_general(wvt_ref[...], h, (((1,), (1,)), ((), ())), preferred_element_type=F32)
    tm = h.shape[0]
    for hh in range(DA_HEADS):
        for jj in range(tm // TQ):
            vt_ref[0, hh, jj, 0:DA_V_DIM, :] = vt[hh * DA_V_DIM:(hh + 1) * DA_V_DIM,
                                                  jj * TQ:(jj + 1) * TQ].astype(BF16)
            vt_ref[0, hh, jj, DA_V_DIM:VT_ROWS, :] = jnp.ones((VT_ROWS - DA_V_DIM, TQ), BF16)

    r = jnp.dot(h, wrest_ref[...], preferred_element_type=F32)
    o = 0
    mlx_ref[0] = r[:, o:o + ML_WIDTH].astype(BF16); o += ML_WIDTH
    mlv_ref[0] = r[:, o:o + ML_WIDTH].astype(BF16); o += ML_WIDTH
    mlo_ref[0] = r[:, o:o + ML_WIDTH].astype(BF16); o += ML_WIDTH
    s5u_ref[...] = r[:, o:o + S5_WIDTH]
    gt_ref[0] = lax.dot_general(wgt_ref[...], h, (((1,), (1,)), ((), ())), preferred_element_type=F32)
    gc_ref[0] = jnp.dot(h, wgc_ref[...], preferred_element_type=F32)


def _in_proj(x, shift, scale, cos_t, sin_t, w_in_l):
    bsz, s, d = x.shape
    tm = TM_PROJ
    w = w_in_l.astype(BF16)
    wqk = w[:, :OFF_DA_V]
    wvt = w[:, OFF_DA_V:OFF_ML_X].T
    wrest = jnp.concatenate([w[:, OFF_ML_X:OFF_ML_I], w[:, OFF_S5_U:]], axis=1)
    wg = w[:, OFF_ML_I:OFF_S5_U]
    wgt = wg.T
    wgc = jnp.pad(wg, ((0, 0), (0, LANES - GATE_PAD)))
    nrest = wrest.shape[1]
    shift3 = shift.reshape(bsz, 1, d)
    scale3 = scale.reshape(bsz, 1, d)
    tok = lambda b, i: (b, i, 0)
    per_b = lambda b, i: (b, 0, 0)
    const2 = lambda b, i: (0, 0)
    out_shapes = [
        jax.ShapeDtypeStruct((bsz, s, DA_QK_WIDTH), BF16),
        jax.ShapeDtypeStruct((bsz, s, DA_QK_WIDTH), BF16),
        jax.ShapeDtypeStruct((bsz, DA_HEADS, s // TQ, VT_ROWS, TQ), BF16),
        jax.ShapeDtypeStruct((bsz, s, ML_WIDTH), BF16),
        jax.ShapeDtypeStruct((bsz, s, ML_WIDTH), BF16),
        jax.ShapeDtypeStruct((bsz, s, ML_WIDTH), BF16),
        jax.ShapeDtypeStruct((bsz, GATE_PAD, s), F32),
        jax.ShapeDtypeStruct((bsz, s, LANES), F32),
        jax.ShapeDtypeStruct((s, bsz * S5_WIDTH), F32),
    ]
    out_specs = [
        pl.BlockSpec((1, tm, DA_QK_WIDTH), tok),
        pl.BlockSpec((1, tm, DA_QK_WIDTH), tok),
        pl.BlockSpec((1, DA_HEADS, tm // TQ, VT_ROWS, TQ), lambda b, i: (b, 0, i, 0, 0)),
        pl.BlockSpec((1, tm, ML_WIDTH), tok),
        pl.BlockSpec((1, tm, ML_WIDTH), tok),
        pl.BlockSpec((1, tm, ML_WIDTH), tok),
        pl.BlockSpec((1, GATE_PAD, tm), lambda b, i: (b, 0, i)),
        pl.BlockSpec((1, tm, LANES), tok),
        pl.BlockSpec((tm, S5_WIDTH), lambda b, i: (i, b)),
    ]
    return pl.pallas_call(
        _in_proj_kernel,
        grid=(bsz, s // tm),
        in_specs=[pl.BlockSpec((1, tm, d), tok),
                  pl.BlockSpec((1, 1, d), per_b),
                  pl.BlockSpec((1, 1, d), per_b),
                  pl.BlockSpec((1, tm, LANES), tok),
                  pl.BlockSpec((1, tm, LANES), tok),
                  pl.BlockSpec((d, OFF_DA_V), const2),
                  pl.BlockSpec((DA_WIDTH, d), const2),
                  pl.BlockSpec((d, nrest), const2),
                  pl.BlockSpec((GATE_PAD, d), const2),
                  pl.BlockSpec((d, LANES), const2)],
        out_specs=out_specs,
        out_shape=out_shapes,
        compiler_params=_cparams(("arbitrary", "arbitrary")),
        name="in_proj",
    )(x, shift3, scale3, cos_t, sin_t, wqk, wvt, wrest, wgt, wgc)


def _diff_attn_kernel(lam_init, lamv_ref, gain_ref, q_ref, k_ref, vt_ref, o_ref, acc_s, m_s):
    qi = pl.program_id(2)
    tq = q_ref.shape[1]
    lane = lax.broadcasted_iota(jnp.int32, (1, LANES), 1)
    first = lane < DA_HEAD_DIM
    q = q_ref[0]
    zero = jnp.zeros_like(q)
    qm = (jnp.where(first, q, zero), jnp.where(first, zero, q))
    acc_s[...] = jnp.zeros_like(acc_s)
    m_s[...] = jnp.full(m_s.shape, NEG, F32)

    def step(j, nblk, masked):
        tk = nblk * tq
        kb = k_ref[0, pl.ds(pl.multiple_of(j * tq, tq), tk), :]
        vtb = vt_ref[0, 0, j] if nblk == 1 else jnp.concatenate([vt_ref[0, 0, j + b] for b in range(nblk)], axis=1)
        for c in range(2):
            st = lax.dot_general(kb, qm[c], (((1,), (1,)), ((), ())), preferred_element_type=F32)
            if masked:
                key_i = lax.broadcasted_iota(jnp.int32, (tk, tq), 0) - (tk - tq)
                qry_i = lax.broadcasted_iota(jnp.int32, (tk, tq), 1)
                st = jnp.where(key_i <= qry_i, st, NEG)
            m_prev = m_s[c]
            m_new = jnp.maximum(m_prev, jnp.max(st, axis=0, keepdims=True))
            alpha = jnp.exp2(m_prev - m_new)
            p = jnp.exp2(st - m_new).astype(BF16)
            acc_s[c] = alpha * acc_s[c] + jnp.dot(vtb, p, preferred_element_type=F32)
            m_s[c] = m_new

    def body(jj, carry):
        step(2 * jj, 2, False)
        return carry

    lax.fori_loop(0, qi // 2, body, 0)

    @pl.when(qi % 2 == 1)
    def _():
        step(qi - 1, 2, True)

    @pl.when(qi % 2 == 0)
    def _():
        step(qi, 1, True)

    outs = []
    for c in range(2):
        acc = acc_s[c]
        outs.append(acc[:DA_V_DIM] / acc[DA_V_DIM:DA_V_DIM + 1])

    lamv = lamv_ref[...]
    lam = (jnp.exp(jnp.sum(lamv[0:1] * lamv[1:2], axis=1, keepdims=True))
           - jnp.exp(jnp.sum(lamv[2:3] * lamv[3:4], axis=1, keepdims=True)) + lam_init)
    ot = outs[0] - lam * outs[1]
    ms = jnp.mean(ot * ot, axis=0, keepdims=True)
    ot = ot * (lax.rsqrt(ms + LN_EPS) * (1.0 - lam_init))
    o_ref[0] = (ot.T * gain_ref[...]).astype(o_ref.dtype)


def _diff_attn(q, k, vt, lamv, gain, lam_init):
    bsz, s, _ = q.shape
    tq = TQ
    nq = s // tq
    return pl.pallas_call(
        functools.partial(_diff_attn_kernel, lam_init),
        grid=(bsz, DA_HEADS, nq),
        in_specs=[pl.BlockSpec((4, DA_HEAD_DIM), lambda b, h, i: (0, 0)),
                  pl.BlockSpec((1, DA_V_DIM), lambda b, h, i: (0, 0)),
                  pl.BlockSpec((1, tq, DA_V_DIM), lambda b, h, i: (b, i, h)),
                  pl.BlockSpec((1, s, DA_V_DIM), lambda b, h, i: (b, 0, h)),
                  pl.BlockSpec((1, 1, nq, VT_ROWS, tq), lambda b, h, i: (b, h, 0, 0, 0))],
        out_specs=pl.BlockSpec((1, tq, DA_V_DIM), lambda b, h, i: (b, i, h)),
        out_shape=jax.ShapeDtypeStruct((bsz, s, DA_WIDTH), BF16),
        scratch_shapes=[pltpu.VMEM((2, VT_ROWS, tq), F32), pltpu.VMEM((2, 1, tq), F32)],
        compiler_params=_cparams(("arbitrary", "arbitrary", "arbitrary")),
        name="diff_attn",
    )(lamv, gain.reshape(1, DA_V_DIM), q, k, vt)


def _log_sigmoid(x):
    return jnp.minimum(x, 0.0) - jnp.log(1.0 + jnp.exp(-jnp.abs(x)))


def _split3(a):
    hi = a.astype(BF16)
    r1 = a - hi.astype(F32)
    mid = r1.astype(BF16)
    lo = (r1 - mid.astype(F32)).astype(BF16)
    return hi, mid, lo


def _mlstm_kernel(x_ref, v_ref, o_ref, gt_ref, gc_ref, cw_ref, cb_ref, wq_ref, wkt_ref, gbt_ref, gbc_ref,
                  ng_ref, hmean_ref, y_ref, xc_s, c_s, m_s):
    nb, s = x_ref.shape[0], x_ref.shape[1]
    L = ML_CHUNK
    H, dh = ML_HEADS, ML_HEAD_DIM
    nc = s // L
    cw = cw_ref[...]
    row = lax.broadcasted_iota(jnp.int32, (s, 1), 0)
    for bi in range(nb):
        x = x_ref[bi].astype(F32)
        xc = x * cw[ML_CONV - 1:ML_CONV]
        for j in range(1, ML_CONV):
            xs = jnp.where(row >= j, pltpu.roll(x, j, 0), 0.0)
            xc = xc + xs * cw[ML_CONV - 1 - j:ML_CONV - j]
        xc = xc + cb_ref[...]
        xc_s[bi] = (xc * _sigmoid(xc)).astype(BF16)

    c_s[...] = jnp.zeros_like(c_s)
    m_s[...] = jnp.full(m_s.shape, NEG, F32)

    ri = lax.broadcasted_iota(jnp.int32, (L, L), 0)
    ci = lax.broadcasted_iota(jnp.int32, (L, L), 1)
    causal = ci <= ri
    tril = causal.astype(BF16)
    triu = (ri <= ci).astype(BF16)
    lane = lax.broadcasted_iota(jnp.int32, (1, dh), 1)
    one_hot0 = jnp.broadcast_to((lane == 0).astype(BF16), (L, dh))

    def chunk_one(bi, ci_, t0):
        xcc = xc_s[bi, pl.ds(t0, L), :]
        qc = jnp.dot(xcc, wq_ref[...], preferred_element_type=F32).astype(BF16)
        ktc = lax.dot_general(wkt_ref[...], xcc, (((1,), (1,)), ((), ())),
                              preferred_element_type=F32)
        g_rows = gt_ref[bi, ci_] + gbt_ref[...]
        g_cols = gc_ref[bi, pl.ds(t0, L), :] + gbc_ref[...]
        lf_rows = _log_sigmoid(g_rows)
        lf_cols = _log_sigmoid(g_cols)
        r3 = jnp.dot(jnp.concatenate(_split3(lf_rows), axis=0), triu, preferred_element_type=F32)
        b_rows = r3[0:GATE_PAD] + r3[GATE_PAD:2 * GATE_PAD] + r3[2 * GATE_PAD:]
        c3 = jnp.dot(tril, jnp.concatenate(_split3(lf_cols), axis=1), preferred_element_type=F32)
        b_cols = c3[:, 0:LANES] + c3[:, LANES:2 * LANES] + c3[:, 2 * LANES:]
        vch = v_ref[bi, pl.ds(t0, L), :]
        och = o_ref[bi, pl.ds(t0, L), :].astype(F32)
        hs = []
        for h in range(H):
            br = b_rows[H + h:H + h + 1, :]
            ir = g_rows[h:h + 1, :]
            bc = b_cols[:, H + h:H + h + 1]
            m_prev = m_s[bi, h]
            log_d = jnp.where(causal, bc - br + ir, NEG)
            inter = bc + m_prev
            mx = jnp.maximum(inter, jnp.max(log_d, axis=1, keepdims=True))
            dmat = jnp.exp(log_d - mx)
            dec = jnp.exp(inter - mx)
            qh = qc[:, h * dh:(h + 1) * dh]
            kth = ktc[h * dh:(h + 1) * dh, :]
            vaug = jnp.concatenate([vch[:, h * dh:(h + 1) * dh], one_hot0], axis=1)
            sm = (jnp.dot(qh, kth.astype(BF16), preferred_element_type=F32) * dmat).astype(BF16)
            c_prev = c_s[bi, h]
            na = (jnp.dot(sm, vaug, preferred_element_type=F32)
                  + dec * jnp.dot(qh, c_prev.astype(BF16), preferred_element_type=F32))
            den = na[:, dh:dh + 1]
            hs.append(na[:, :dh] / jnp.maximum(jnp.abs(den), jnp.exp(-mx)))
            g_tot = br[:, L - 1:L]
            a_row = g_tot - br + ir
            m_new = jnp.maximum(g_tot + m_prev, jnp.max(a_row, axis=1, keepdims=True))
            decay = jnp.exp(g_tot + m_prev - m_new)
            w_row = jnp.exp(a_row - m_new)
            kw = (kth * w_row).astype(BF16)
            c_s[bi, h] = decay * c_prev + jnp.dot(kw, vaug, preferred_element_type=F32)
            m_s[bi, h] = m_new
        hcat = jnp.concatenate(hs, axis=1)
        m3 = jnp.dot(jnp.concatenate(_split3(hcat * hcat), axis=0), hmean_ref[...],
                     preferred_element_type=F32)
        ms = m3[0:L] + m3[L:2 * L] + m3[2 * L:]
        y = hcat * lax.rsqrt(ms + LN_EPS) * ng_ref[...] * _sigmoid(och)
        y_ref[bi, pl.ds(t0, L), :] = y.astype(y_ref.dtype)

    def chunk(ci_, _):
        t0 = pl.multiple_of(ci_ * L, L)
        for bi in range(nb):
            chunk_one(bi, ci_, t0)
        return 0

    lax.fori_loop(0, nc, chunk, 0)


def _mlstm(mlx, mlv, mlo, g_t, g_c, conv_w, conv_b, w_q, w_k, gate_b, norm_g):
    bsz, s, _ = mlx.shape
    H, dh = ML_HEADS, ML_HEAD_DIM
    eye = jnp.eye(H, dtype=F32)
    wq_bd = jnp.einsum('hde,hg->hdge', w_q, eye).reshape(ML_WIDTH, ML_WIDTH).astype(BF16)
    wk_bd = jnp.einsum('hde,hg->hdge', w_k * (dh ** -0.5), eye).reshape(ML_WIDTH, ML_WIDTH)
    wkt_bd = wk_bd.T.astype(BF16)
    gbt = gate_b.reshape(GATE_PAD, 1)
    gbc = jnp.pad(gate_b.reshape(1, GATE_PAD), ((0, 0), (0, LANES - GATE_PAD)))
    hmean = jnp.kron(eye, jnp.full((dh, dh), 1.0 / dh, F32)).astype(BF16)
    nc = s // ML_CHUNK
    g_t4 = g_t.reshape(bsz, GATE_PAD, nc, ML_CHUNK).transpose(0, 2, 1, 3)
    tok = lambda b: (b, 0, 0)
    c2 = lambda b: (0, 0)
    nb = ML_NB
    return pl.pallas_call(
        _mlstm_kernel,
        grid=(bsz // nb,),
        in_specs=[pl.BlockSpec((nb, s, ML_WIDTH), tok),
                  pl.BlockSpec((nb, s, ML_WIDTH), tok),
                  pl.BlockSpec((nb, s, ML_WIDTH), tok),
                  pl.BlockSpec((nb, nc, GATE_PAD, ML_CHUNK), lambda b: (b, 0, 0, 0)),
                  pl.BlockSpec((nb, s, LANES), tok),
                  pl.BlockSpec((ML_CONV, ML_WIDTH), c2),
                  pl.BlockSpec((1, ML_WIDTH), c2),
                  pl.BlockSpec((ML_WIDTH, ML_WIDTH), c2),
                  pl.BlockSpec((ML_WIDTH, ML_WIDTH), c2),
                  pl.BlockSpec((GATE_PAD, 1), c2),
                  pl.BlockSpec((1, LANES), c2),
                  pl.BlockSpec((1, ML_WIDTH), c2),
                  pl.BlockSpec((ML_WIDTH, ML_WIDTH), c2)],
        out_specs=pl.BlockSpec((nb, s, ML_WIDTH), tok),
        out_shape=jax.ShapeDtypeStruct((bsz, s, ML_WIDTH), BF16),
        scratch_shapes=[pltpu.VMEM((nb, s, ML_WIDTH), BF16),
                        pltpu.VMEM((nb, H, dh, LANES), F32),
                        pltpu.VMEM((nb, H, 1, 1), F32)],
        compiler_params=_cparams(("arbitrary",)),
        name="mlstm",
    )(mlx, mlv, mlo, g_t4, g_c, conv_w, conv_b.reshape(1, ML_WIDTH), wq_bd, wkt_bd, gbt, gbc,
      norm_g.reshape(1, ML_WIDTH), hmean)


def _gelu_tanh(x):
    return 0.5 * x * (1.0 + jnp.tanh(math.sqrt(2.0 / math.pi) * (x + 0.044715 * (x * x * x))))


def _s5_kernel(u_ref, are_ref, aim_ref, bcat_ref, ccat_ref, d_ref, wglu_ref, y_ref, xs_s, st_s):
    tc, bsz, w = u_ref.shape
    n = S5_NSTATE

    @pl.when(pl.program_id(0) == 0)
    def _():
        st_s[...] = jnp.zeros_like(st_s)

    u = u_ref[...].reshape(tc * bsz, w)
    xs_s[...] = jnp.dot(u.astype(BF16), bcat_ref[...], preferred_element_type=F32).reshape(tc, bsz, 2 * n)
    a_re = jnp.broadcast_to(are_ref[...], (bsz, n))
    a_im = jnp.broadcast_to(aim_ref[...], (bsz, n))

    def step(t, carry):
        x_re, x_im = carry
        bu = xs_s[t]
        n_re = a_re * x_re - a_im * x_im + bu[:, :n]
        n_im = a_re * x_im + a_im * x_re + bu[:, n:]
        xs_s[t] = jnp.concatenate([n_re, n_im], axis=1)
        return n_re, n_im

    x_re, x_im = lax.fori_loop(0, tc, step, (st_s[0], st_s[1]), unroll=S5_UNROLL)
    st_s[0] = x_re
    st_s[1] = x_im

    xs = xs_s[...].reshape(tc * bsz, 2 * n).astype(BF16)
    y = jnp.dot(xs, ccat_ref[...], preferred_element_type=F32) + d_ref[...] * u
    z = jnp.dot(_gelu_tanh(y).astype(BF16), wglu_ref[...], preferred_element_type=F32)
    out = z[:, :w] * _sigmoid(z[:, w:])
    y_ref[...] = out.reshape(tc, bsz, w).astype(y_ref.dtype)


def _s5_params(a_re, a_im, log_dt, b_re, b_im, c_re, c_im, w_glu):
    G, P, Hc = S5_GROUPS, S5_STATE, S5_GROUP
    dt = jnp.exp(log_dt)[:, None]
    mag = jnp.exp(a_re * dt)
    ab_re = mag * jnp.cos(a_im * dt)
    ab_im = mag * jnp.sin(a_im * dt)
    nr, ni = ab_re - 1.0, ab_im
    den = a_re * a_re + a_im * a_im
    fr = (nr * a_re + ni * a_im) / den
    fi = (ni * a_re - nr * a_im) / den
    bb_re = fr[..., None] * b_re - fi[..., None] * b_im
    bb_im = fr[..., None] * b_im + fi[..., None] * b_re
    eye = jnp.eye(G, dtype=F32)
    bd = lambda t, sub: jnp.einsum(sub, t, eye)
    bre = bd(bb_re, 'gph,gk->ghkp').reshape(G * Hc, G * P)
    bim = bd(bb_im, 'gph,gk->ghkp').reshape(G * Hc, G * P)
    bcat = jnp.concatenate([bre, bim], axis=1).astype(BF16)
    cre = bd(c_re, 'ghp,gk->gpkh').reshape(G * P, G * Hc)
    cim = bd(c_im, 'ghp,gk->gpkh').reshape(G * P, G * Hc)
    ccat = jnp.concatenate([cre, -cim], axis=0).astype(BF16)
    wv = bd(w_glu[:, :, :Hc], 'ghj,gk->ghkj').reshape(G * Hc, G * Hc)
    wg = bd(w_glu[:, :, Hc:], 'ghj,gk->ghkj').reshape(G * Hc, G * Hc)
    wglu = jnp.concatenate([wv, wg], axis=1).astype(BF16)
    return ab_re.reshape(1, G * P), ab_im.reshape(1, G * P), bcat, ccat, wglu


def _s5(u_tm, bsz, params, d_skip):
    s = u_tm.shape[0]
    w = S5_WIDTH
    n = S5_NSTATE
    are, aim, bcat, ccat, wglu = params
    u3 = u_tm.reshape(s, bsz, w)
    tc = S5_TC
    c2 = lambda i: (0, 0)
    y = pl.pallas_call(
        _s5_kernel,
        grid=(s // tc,),
        in_specs=[pl.BlockSpec((tc, bsz, w), lambda i: (i, 0, 0)),
                  pl.BlockSpec((1, n), c2),
                  pl.BlockSpec((1, n), c2),
                  pl.BlockSpec((w, 2 * n), c2),
                  pl.BlockSpec((2 * n, w), c2),
                  pl.BlockSpec((1, w), c2),
                  pl.BlockSpec((w, 2 * w), c2)],
        out_specs=pl.BlockSpec((tc, bsz, w), lambda i: (i, 0, 0)),
        out_shape=jax.ShapeDtypeStruct((s, bsz, w), F32),
        scratch_shapes=[pltpu.VMEM((tc, bsz, 2 * n), F32),
                        pltpu.VMEM((2, bsz, n), F32)],
        compiler_params=_cparams(("arbitrary",)),
        name="s5",
    )(u3, are, aim, bcat, ccat, d_skip.reshape(1, w), wglu)
    return y.reshape(s, bsz * w)


def _layer_norm(z, g, b):
    mu = jnp.mean(z, axis=1, keepdims=True)
    zc = z - mu
    var = jnp.mean(zc * zc, axis=1, keepdims=True)
    return zc * lax.rsqrt(var + LN_EPS) * g + b


def _out_proj_kernel(yda_ref, yml_ref, ys5_ref, x_ref, gate_ref, lng_ref, lnb_ref, shift_ref, scale_ref,
                     wout_ref, wrt_ref, brt_ref, x1_ref, h2_ref, eid_ref, prob_ref):
    y = jnp.dot(yda_ref[0], wout_ref[0:DA_WIDTH, :], preferred_element_type=F32)
    y = y + jnp.dot(yml_ref[0], wout_ref[DA_WIDTH:DA_WIDTH + ML_WIDTH, :], preferred_element_type=F32)
    y = y + jnp.dot(ys5_ref[...].astype(BF16), wout_ref[DA_WIDTH + ML_WIDTH:, :], preferred_element_type=F32)
    x1 = _layer_norm(DN_ALPHA * x_ref[0] + (1.0 + gate_ref[0]) * y, lng_ref[...], lnb_ref[...])
    x1_ref[0] = x1
    h2 = x1 * (1.0 + scale_ref[0]) + shift_ref[0]
    h2_ref[...] = _pack_bf16_pairs(h2)
    h_hi = h2.astype(BF16)
    h_lo = (h2 - h_hi.astype(F32)).astype(BF16)
    nt_dot = lambda a, b: lax.dot_general(a, b, (((1,), (1,)), ((), ())), preferred_element_type=F32)
    by_hi = nt_dot(wrt_ref[...], h_hi)
    logits = (by_hi[:N_EXPERTS] + by_hi[N_EXPERTS:] + nt_dot(wrt_ref[0:N_EXPERTS, :], h_lo)
              + brt_ref[...])
    eidx = lax.broadcasted_iota(jnp.int32, logits.shape, 0)
    vals, ids = [], []
    for _ in range(TOP_K):
        mx = jnp.max(logits, axis=0, keepdims=True)
        sel = jnp.min(jnp.where(logits == mx, eidx, N_EXPERTS), axis=0, keepdims=True)
        vals.append(mx)
        ids.append(sel)
        logits = jnp.where(eidx == sel, -jnp.inf, logits)
    ex = [jnp.exp(v - vals[0]) for v in vals]
    tot = ex[0] + ex[1] + ex[2] + ex[3]
    zi = jnp.zeros_like(ids[0])
    eid_ref[...] = jnp.concatenate(ids + [zi] * (SUBLANES - TOP_K), axis=0)
    zf = jnp.zeros((LANES - TOP_K, tot.shape[1]), F32)
    prob_ref[...] = jnp.concatenate([e / tot for e in ex] + [zf], axis=0).T


def _out_proj(y_da, y_ml, y_s5, x, gate, ln_g, ln_b, shift2, scale2, w_out_l, w_router_l, b_router_l):
    bsz, s, d = x.shape
    tm = TM_PROJ
    nt = s // tm
    tok = lambda b, i: (b, i, 0)
    per_b = lambda b, i: (b, 0, 0)
    c2 = lambda b, i: (0, 0)
    r3 = lambda a: a.reshape(bsz, 1, d)
    flat = lambda b, i: (0, b * nt + i)
    wr_t = w_router_l.T
    wr_hi = wr_t.astype(BF16)
    wr_lo = (wr_t - wr_hi.astype(F32)).astype(BF16)
    return pl.pallas_call(
        _out_proj_kernel,
        grid=(bsz, nt),
        in_specs=[pl.BlockSpec((1, tm, DA_WIDTH), tok),
                  pl.BlockSpec((1, tm, ML_WIDTH), tok),
                  pl.BlockSpec((tm, S5_WIDTH), lambda b, i: (i, b)),
                  pl.BlockSpec((1, tm, d), tok),
                  pl.BlockSpec((1, 1, d), per_b),
                  pl.BlockSpec((1, d), c2),
                  pl.BlockSpec((1, d), c2),
                  pl.BlockSpec((1, 1, d), per_b),
                  pl.BlockSpec((1, 1, d), per_b),
                  pl.BlockSpec((d, d), c2),
                  pl.BlockSpec((2 * N_EXPERTS, d), c2),
                  pl.BlockSpec((N_EXPERTS, 1), c2)],
        out_specs=[pl.BlockSpec((1, tm, d), tok),
                   pl.BlockSpec((tm, d // 2), lambda b, i: (b * nt + i, 0)),
                   pl.BlockSpec((SUBLANES, tm), flat),
                   pl.BlockSpec((tm, LANES), lambda b, i: (b * nt + i, 0))],
        out_shape=[jax.ShapeDtypeStruct((bsz, s, d), F32),
                   jax.ShapeDtypeStruct((bsz * s, d // 2), jnp.int32),
                   jax.ShapeDtypeStruct((SUBLANES, bsz * s), jnp.int32),
                   jax.ShapeDtypeStruct((bsz * s, LANES), F32)],
        compiler_params=_cparams(("arbitrary", "arbitrary")),
        name="out_proj",
    )(y_da, y_ml, y_s5, x, r3(gate), ln_g.reshape(1, d), ln_b.reshape(1, d), r3(shift2), r3(scale2),
      w_out_l.astype(BF16), jnp.concatenate([wr_hi, wr_lo], axis=0), b_router_l.reshape(N_EXPERTS, 1))


META_END, META_PAD, META_CNT = 0, 1, 2


def _route_kernel(eid_ref, pos_ref, meta_ref, carry_s, start_s):
    phase = pl.program_id(0)
    i = pl.program_id(1)
    tb = eid_ref.shape[1]
    ntp = meta_ref.shape[1]
    tm = TM_MOE

    @pl.when(jnp.logical_and(phase == 0, i == 0))
    def _():
        carry_s[...] = jnp.zeros_like(carry_s)

    eid = eid_ref[...]
    eidx = lax.broadcasted_iota(jnp.int32, (N_EXPERTS, tb), 0)
    hot = [eidx == eid[k:k + 1, :] for k in range(TOP_K)]
    member = jnp.zeros((N_EXPERTS, tb), F32)
    for k in range(TOP_K):
        member = member + hot[k].astype(F32)
    total = jnp.sum(member, axis=1, keepdims=True)

    @pl.when(phase == 0)
    def _():
        carry_s[...] = carry_s[...] + total

    @pl.when(jnp.logical_and(phase == 1, i == 0))
    def _():
        cnt = carry_s[...]
        padded = jnp.floor((cnt + (tm - 1)) * (1.0 / tm)) * tm
        er = lax.broadcasted_iota(jnp.int32, (N_EXPERTS, N_EXPERTS), 0)
        ec = lax.broadcasted_iota(jnp.int32, (N_EXPERTS, N_EXPERTS), 1)
        ends = jnp.dot((ec <= er).astype(F32), padded, preferred_element_type=F32, precision=HIGHEST)
        start_s[...] = ends - padded
        carry_s[...] = jnp.zeros_like(carry_s)
        lane = lax.broadcasted_iota(jnp.int32, (N_EXPERTS, ntp), 1)
        sub = lax.broadcasted_iota(jnp.int32, (N_EXPERTS, ntp), 0)
        diag = lane == sub

        def as_row(col):
            return jnp.sum(jnp.where(diag, col, 0.0), axis=0, keepdims=True)

        zero = jnp.zeros((SUBLANES - 3, ntp), F32)
        meta_ref[...] = jnp.concatenate([as_row(ends[:, 0:1]), as_row(padded[:, 0:1]), as_row(cnt[:, 0:1]), zero],
                                        axis=0).astype(jnp.int32)

    @pl.when(phase == 1)
    def _():
        ri = lax.broadcasted_iota(jnp.int32, (tb, tb), 0)
        ci = lax.broadcasted_iota(jnp.int32, (tb, tb), 1)
        triu = (ri <= ci).astype(BF16)
        incl = jnp.dot(member.astype(BF16), triu, preferred_element_type=F32)
        slot = incl - member + carry_s[:, 0:1] + start_s[:, 0:1]
        rows = [jnp.sum(jnp.where(hot[k], slot, 0.0), axis=0, keepdims=True) for k in range(TOP_K)]
        zr = jnp.zeros_like(rows[0])
        pos_ref[...] = jnp.concatenate(rows + [zr] * (SUBLANES - TOP_K), axis=0).astype(jnp.int32)
        carry_s[...] = carry_s[...] + total


def _route(eid):
    t = eid.shape[1]
    tb = TB_RANK
    ntp = LANES
    pos8, meta = pl.pallas_call(
        _route_kernel,
        grid=(2, t // tb),
        in_specs=[pl.BlockSpec((SUBLANES, tb), lambda p, i: (0, i))],
        out_specs=[pl.BlockSpec((SUBLANES, tb), lambda p, i: (0, i * p)),
                   pl.BlockSpec((SUBLANES, ntp), lambda p, i: (0, 0))],
        out_shape=[jax.ShapeDtypeStruct((SUBLANES, t), jnp.int32),
                   jax.ShapeDtypeStruct((SUBLANES, ntp), jnp.int32)],
        scratch_shapes=[pltpu.VMEM((N_EXPERTS, LANES), F32), pltpu.VMEM((N_EXPERTS, LANES), F32)],
        compiler_params=_cparams(("arbitrary", "arbitrary")),
        name="route",
    )(eid)
    return pos8, meta


def _sc_workers():
    info = plsc.get_sparse_core_info()
    return info.num_cores, info.num_cores * info.num_subcores


def _dispatch(h2, pos8, n_rows):
    t, d = h2.shape
    n_cores, n_workers = _sc_workers()
    tpw = t // n_workers
    ch = SC_CHUNK
    mesh = plsc.VectorSubcoreMesh(core_axis_name="c", subcore_axis_name="s")

    @functools.partial(
        pl.kernel, mesh=mesh,
        out_type=jax.ShapeDtypeStruct((n_rows, d), h2.dtype),
        scratch_types=[pltpu.VMEM((ch,), jnp.int32)] * TOP_K + [pltpu.VMEM((ch, d), h2.dtype),
                                                                pltpu.SemaphoreType.DMA])
    def scatter_rows(h_hbm, pos_hbm, out_hbm, i0, i1, i2, i3, rows_v, sem):
        idx = (i0, i1, i2, i3)
        base = (lax.axis_index("s") * n_cores + lax.axis_index("c")) * tpw

        @pl.loop(0, tpw // ch)
        def _(i):
            off = base + i * ch
            pltpu.sync_copy(h_hbm.at[pl.ds(off, ch)], rows_v)
            for k in range(TOP_K):
                pltpu.sync_copy(pos_hbm.at[k, pl.ds(off, ch)], idx[k])
            copies = [pltpu.async_copy(rows_v, out_hbm.at[idx[k]], sem) for k in range(TOP_K)]
            for cp in copies:
                cp.wait()

    return scatter_rows(h2, pos8)


def _gather_expert_rows(ys, pos8):
    _, d = ys.shape
    t = pos8.shape[1]
    n_cores, n_workers = _sc_workers()
    tpw = t // n_workers
    ch = SC_CHUNK
    mesh = plsc.VectorSubcoreMesh(core_axis_name="c", subcore_axis_name="s")

    @functools.partial(
        pl.kernel, mesh=mesh,
        out_type=jax.ShapeDtypeStruct((TOP_K, t, d), ys.dtype),
        scratch_types=[pltpu.VMEM((ch,), jnp.int32), pltpu.VMEM((ch, d), ys.dtype), pltpu.SemaphoreType.DMA])
    def gather_rows(ys_hbm, pos_hbm, out_hbm, idx_v, rows_v, sem):
        base = (lax.axis_index("s") * n_cores + lax.axis_index("c")) * tpw

        @pl.loop(0, tpw // ch)
        def _(i):
            off = base + i * ch
            for k in range(TOP_K):
                pltpu.sync_copy(pos_hbm.at[k, pl.ds(off, ch)], idx_v)
                pltpu.async_copy(ys_hbm.at[idx_v], rows_v, sem).wait()
                pltpu.sync_copy(rows_v, out_hbm.at[k, pl.ds(off, ch)])

    return gather_rows(ys, pos8)


def _pack_bf16_pairs(a):
    n = a.shape[1] // 2
    lo = pltpu.bitcast(a[:, :n].astype(BF16).astype(F32), jnp.int32)
    hi = pltpu.bitcast(a[:, n:].astype(BF16).astype(F32), jnp.int32)
    return jnp.bitwise_or(jnp.bitwise_and(hi, -65536), jnp.bitwise_and(lax.shift_right_logical(lo, 16), 65535))


def _unpack_bf16_pairs(p):
    lo = pltpu.bitcast(lax.shift_left(p, 16), F32)
    hi = pltpu.bitcast(jnp.bitwise_and(p, -65536), F32)
    return lo, hi


def _expert_kernel(meta_ref, xs_ref, wup_ref, bup_ref, wdn_ref, bdn_ref, ys_ref,
                   wup_s, wdn_s, xbuf, obuf, in_sem, out_sem):
    e = pl.program_id(0)
    tm = TM_MOE
    half = D_MODEL // 2
    pad = meta_ref[META_PAD, e]
    n_t = pad // tm
    row0 = meta_ref[META_END, e] - pad
    cnt = meta_ref[META_CNT, e]

    def rows(i):
        return pl.ds(pl.multiple_of(row0 + i * tm, tm), tm)

    def x_copy(i, slot):
        return pltpu.make_async_copy(xs_ref.at[rows(i)], xbuf.at[slot], in_sem.at[slot])

    def y_copy(i, slot):
        return pltpu.make_async_copy(obuf.at[slot], ys_ref.at[rows(i)], out_sem.at[slot])

    @pl.when(n_t > 0)
    def _():
        x_copy(0, 0).start()
        wup_s[...] = wup_ref[0, 0].astype(BF16)
        wdn_s[...] = wdn_ref[0, 0].astype(BF16)

        def tile(i, carry):
            slot = lax.rem(i, 2)
            x_copy(i, slot).wait()

            @pl.when(i + 1 < n_t)
            def _():
                x_copy(i + 1, 1 - slot).start()

            @pl.when(i >= 2)
            def _():
                y_copy(i - 2, slot).wait()

            row = lax.broadcasted_iota(jnp.int32, (tm, 1), 0)
            lo, hi = _unpack_bf16_pairs(jnp.where(row < cnt - i * tm, xbuf[slot], 0))
            z = (jnp.dot(lo.astype(BF16), wup_s[0:half, :], preferred_element_type=F32)
                 + jnp.dot(hi.astype(BF16), wup_s[half:, :], preferred_element_type=F32) + bup_ref[0, 0])
            glu = jnp.minimum(z[:, :D_EXPERT], SWIGLU_LIMIT)
            lin = jnp.clip(z[:, D_EXPERT:], -SWIGLU_LIMIT, SWIGLU_LIMIT)
            act = (glu * _sigmoid(SWIGLU_ALPHA * glu) * (lin + 1.0)).astype(BF16)
            y = jnp.dot(act, wdn_s[...], preferred_element_type=F32) + bdn_ref[0, 0]
            obuf[slot] = _pack_bf16_pairs(y)
            y_copy(i, slot).start()
            return carry

        lax.fori_loop(0, n_t, tile, 0)

        @pl.when(n_t >= 2)
        def _():
            y_copy(n_t - 2, lax.rem(n_t, 2)).wait()

        y_copy(n_t - 1, lax.rem(n_t - 1, 2)).wait()

    @pl.when(e == N_EXPERTS - 1)
    def _():
        obuf[0] = jnp.zeros((tm, half), jnp.int32)

        def fill(i, carry):
            cp = pltpu.make_async_copy(obuf.at[0], ys_ref.at[pl.ds(pl.multiple_of(i * tm, tm), tm)], out_sem.at[0])
            cp.start()
            cp.wait()
            return carry

        lax.fori_loop(meta_ref[META_END, N_EXPERTS - 1] // tm, ys_ref.shape[0] // tm, fill, 0)


def _expert_mlp(xs, meta, layer, w_up, b_up, w_down, b_down):
    n_rows, half = xs.shape
    d = 2 * half
    tm = TM_MOE
    f = w_up.shape[-1]
    b_up4 = b_up.reshape(DEPTH, N_EXPERTS, 1, f)
    b_dn4 = b_down.reshape(DEPTH, N_EXPERTS, 1, d)
    wsel = lambda e, m: (layer, e, 0, 0)
    grid_spec = pltpu.PrefetchScalarGridSpec(
        num_scalar_prefetch=1,
        grid=(N_EXPERTS,),
        in_specs=[pl.BlockSpec(memory_space=pl.ANY),
                  pl.BlockSpec((1, 1, d, f), wsel),
                  pl.BlockSpec((1, 1, 1, f), wsel),
                  pl.BlockSpec((1, 1, f // 2, d), wsel),
                  pl.BlockSpec((1, 1, 1, d), wsel)],
        out_specs=pl.BlockSpec(memory_space=pl.ANY),
        scratch_shapes=[pltpu.VMEM((d, f), BF16), pltpu.VMEM((f // 2, d), BF16),
                        pltpu.VMEM((2, tm, half), jnp.int32), pltpu.VMEM((2, tm, half), jnp.int32),
                        pltpu.SemaphoreType.DMA((2,)), pltpu.SemaphoreType.DMA((2,))],
    )
    return pl.pallas_call(
        _expert_kernel,
        grid_spec=grid_spec,
        out_shape=jax.ShapeDtypeStruct((n_rows, half), jnp.int32),
        compiler_params=_cparams(("arbitrary",)),
        name="expert_mlp",
    )(meta, xs, w_up, b_up4, w_down, b_dn4)


def _combine_kernel(rows_ref, prob_ref, x_ref, gate_ref, lng_ref, lnb_ref, o_ref):
    p = prob_ref[...]
    y = None
    for k in range(TOP_K):
        yk = p[:, k:k + 1] * jnp.concatenate(_unpack_bf16_pairs(rows_ref[k]), axis=1)
        y = yk if y is None else y + yk
    o_ref[0] = _layer_norm(DN_ALPHA * x_ref[0] + (1.0 + gate_ref[0]) * y, lng_ref[...], lnb_ref[...])


def _combine(rows, prob_c, x1, gate, ln_g, ln_b, part):
    bsz, s, d = x1.shape
    tm = TM_DISP
    nt = s // tm
    pb = bsz // MOE_PARTS
    b0 = part * pb
    return pl.pallas_call(
        _combine_kernel,
        grid=(pb, nt),
        in_specs=[pl.BlockSpec((TOP_K, tm, d // 2), lambda b, i: (0, b * nt + i, 0)),
                  pl.BlockSpec((tm, LANES), lambda b, i: ((b0 + b) * nt + i, 0)),
                  pl.BlockSpec((1, tm, d), lambda b, i: (b0 + b, i, 0)),
                  pl.BlockSpec((1, 1, d), lambda b, i: (b0 + b, 0, 0)),
                  pl.BlockSpec((1, d), lambda b, i: (0, 0)),
                  pl.BlockSpec((1, d), lambda b, i: (0, 0))],
        out_specs=pl.BlockSpec((1, tm, d), lambda b, i: (b0 + b, i, 0)),
        out_shape=jax.ShapeDtypeStruct((bsz, s, d), F32),
        input_output_aliases={2: 0},
        compiler_params=_cparams(("arbitrary", "arbitrary")),
        name="combine",
    )(rows, prob_c, x1, gate.reshape(bsz, 1, d), ln_g.reshape(1, d), ln_b.reshape(1, d))


def kernel(x, c, positions, ada_w, ada_b, w_in, lam_q1, lam_k1, lam_q2, lam_k2, da_norm_g, ml_conv_w, ml_conv_b,
           ml_w_q, ml_w_k, ml_gate_b, ml_norm_g, s5_a_re, s5_a_im, s5_log_dt, s5_b_re, s5_b_im, s5_c_re, s5_c_im,
           s5_d, s5_w_glu, w_out, ln_g, ln_b, w_router, b_router, w_up, b_up, w_down, b_down):
    bsz, s, d = x.shape
    t = bsz * s
    n_tiles_max = (t * TOP_K) // TM_MOE + N_EXPERTS
    n_rows = n_tiles_max * TM_MOE
    mod = _modulation(c, ada_w, ada_b)
    cos_t, sin_t = _rope_tables(positions)
    for l in range(DEPTH):
        shift, scale, gate = jnp.split(mod[2 * l], 3, axis=-1)
        q, k, v, mlx, mlv, mlo, g_t, g_c, s5u = _in_proj(x, shift, scale, cos_t, sin_t, w_in[l])
        lam_init = 0.8 - 0.6 * math.exp(-0.3 * l)
        lamv = jnp.stack([lam_q1[l], lam_k1[l], lam_q2[l], lam_k2[l]])
        y_da = _diff_attn(q, k, v, lamv, da_norm_g[l], lam_init)
        y_ml = _mlstm(mlx, mlv, mlo, g_t, g_c, ml_conv_w[l], ml_conv_b[l], ml_w_q[l], ml_w_k[l],
                      ml_gate_b[l], ml_norm_g[l])
        s5p = _s5_params(s5_a_re[l], s5_a_im[l], s5_log_dt[l], s5_b_re[l], s5_b_im[l], s5_c_re[l], s5_c_im[l],
                         s5_w_glu[l])
        y_s5 = _s5(s5u, bsz, s5p, s5_d[l])
        shift2, scale2, gate2 = jnp.split(mod[2 * l + 1], 3, axis=-1)
        x1, h2, eid, prob = _out_proj(y_da, y_ml, y_s5, x, gate, ln_g[l, 0], ln_b[l, 0], shift2, scale2,
                                      w_out[l], w_router[l], b_router[l])
        pos8, meta = _route(eid)
        xs = _dispatch(h2, pos8, n_rows)
        ys = _expert_mlp(xs, meta, l, w_up, b_up, w_down, b_down)
        tp = t // MOE_PARTS
        x = x1
        for part in range(MOE_PARTS):
            rows = _gather_expert_rows(ys, pos8[:, part * tp:(part + 1) * tp])
            x = _combine(rows, prob, x, gate2, ln_g[l, 1], ln_b[l, 1], part)
    return x
```

```python
import functools
import math

import jax
import jax.numpy as jnp
from jax import lax
from jax.experimental import pallas as pl
from jax.experimental.pallas import tpu as pltpu
from jax.experimental.pallas import tpu_sc as plsc

F32 = jnp.float32
BF16 = jnp.bfloat16
HIGHEST = lax.Precision.HIGHEST

D_MODEL = 1024
DEPTH = 2
DA_HEADS = 4
DA_HEAD_DIM = 64
DA_V_DIM = 2 * DA_HEAD_DIM
DA_WIDTH = DA_HEADS * DA_V_DIM
DA_QK_WIDTH = DA_HEADS * 2 * DA_HEAD_DIM
ROPE_THETA = 10000.0
ML_HEADS = 4
ML_HEAD_DIM = 64
ML_WIDTH = ML_HEADS * ML_HEAD_DIM
ML_CONV = 4
S5_GROUP = 16
S5_STATE = 64
S5_WIDTH = D_MODEL - DA_WIDTH - ML_WIDTH
S5_GROUPS = S5_WIDTH // S5_GROUP
S5_NSTATE = S5_GROUPS * S5_STATE
N_EXPERTS = 32
TOP_K = 4
D_EXPERT = D_MODEL
SWIGLU_LIMIT = 7.0
SWIGLU_ALPHA = 1.702
DN_ALPHA = (2 * DEPTH) ** 0.25
LN_EPS = 1e-5
NEG = -1e30

OFF_DA_K = DA_QK_WIDTH
OFF_DA_V = 2 * DA_QK_WIDTH
OFF_ML_X = OFF_DA_V + DA_WIDTH
OFF_ML_V = OFF_ML_X + ML_WIDTH
OFF_ML_O = OFF_ML_V + ML_WIDTH
OFF_ML_I = OFF_ML_O + ML_WIDTH
OFF_ML_F = OFF_ML_I + ML_HEADS
OFF_S5_U = OFF_ML_F + ML_HEADS
N_IN = OFF_S5_U + S5_WIDTH

LANES = 128
SUBLANES = 8
VMEM_LIMIT_BYTES = 56 * 1024 * 1024

TM_PROJ = 1024
TQ = 512
ML_CHUNK = 256
ML_NB = 2
S5_TC = 256
S5_UNROLL = 8
TB_RANK = 512
TM_MOE = 256
TM_DISP = 256
MOE_PARTS = 4
SC_CHUNK = 128
GATE_PAD = 8
VT_ROWS = DA_V_DIM + 16
Q_PRESCALE = DA_HEAD_DIM ** -0.5 * math.log2(math.e)


def _cparams(sem, vmem=VMEM_LIMIT_BYTES):
    return pltpu.CompilerParams(dimension_semantics=sem, vmem_limit_bytes=vmem)


def _sigmoid(x):
    return 1.0 / (1.0 + jnp.exp(-x))


def _mod_kernel(c_ref, w_ref, b_ref, o_ref):
    c = c_ref[...]
    ca = (c * _sigmoid(c)).astype(BF16)
    w = w_ref[0, 0].astype(BF16)
    o_ref[0] = jnp.dot(ca, w, preferred_element_type=F32) + b_ref[0]


def _modulation(c, ada_w, ada_b):
    nsub = ada_w.shape[1]
    nmod = ada_w.shape[0] * nsub
    bsz, d = c.shape
    e = ada_w.shape[-1]
    tn = 1024
    b = ada_b.reshape(nmod, 1, e)
    return pl.pallas_call(
        _mod_kernel,
        grid=(nmod, e // tn),
        in_specs=[pl.BlockSpec((bsz, d), lambda n, j: (0, 0)),
                  pl.BlockSpec((1, 1, d, tn), lambda n, j: (n // nsub, n % nsub, 0, j)),
                  pl.BlockSpec((1, 1, tn), lambda n, j: (n, 0, j))],
        out_specs=pl.BlockSpec((1, bsz, tn), lambda n, j: (n, 0, j)),
        out_shape=jax.ShapeDtypeStruct((nmod, bsz, e), F32),
        compiler_params=_cparams(("arbitrary", "arbitrary")),
        name="modulation",
    )(c, ada_w, b)


def _rope_kernel(pos_ref, cos_ref, sin_ref):
    nfreq = DA_HEAD_DIM // 2
    pos = pos_ref[0].astype(F32)
    fidx = lax.broadcasted_iota(jnp.int32, (nfreq, 1), 0).astype(F32)
    inv = jnp.exp(fidx * (-2.0 * math.log(ROPE_THETA) / DA_HEAD_DIM))
    ang = inv * pos
    reps = LANES // nfreq
    cos_t = jnp.concatenate([jnp.cos(ang)] * reps, axis=0).T
    sin_t = jnp.concatenate([jnp.sin(ang)] * reps, axis=0).T
    lane = lax.broadcasted_iota(jnp.int32, (1, LANES), 1)
    sign = jnp.where((lane % DA_HEAD_DIM) < nfreq, -1.0, 1.0)
    cos_ref[0] = cos_t
    sin_ref[0] = sin_t * sign


def _rope_tables(positions):
    bsz, s = positions.shape
    ts = 512
    pos3 = positions.reshape(bsz, 1, s)
    return pl.pallas_call(
        _rope_kernel,
        grid=(bsz, s // ts),
        in_specs=[pl.BlockSpec((1, 1, ts), lambda b, i: (b, 0, i))],
        out_specs=[pl.BlockSpec((1, ts, LANES), lambda b, i: (b, i, 0))] * 2,
        out_shape=[jax.ShapeDtypeStruct((bsz, s, LANES), F32)] * 2,
        compiler_params=_cparams(("arbitrary", "arbitrary")),
        name="rope_tables",
    )(pos3)


def _in_proj_kernel(x_ref, shift_ref, scale_ref, cos_ref, sin_ref, wqk_ref, wvt_ref, wrest_ref, wgt_ref, wgc_ref,
                    q_ref, k_ref, vt_ref, mlx_ref, mlv_ref, mlo_ref, gt_ref, gc_ref, s5u_ref):
    h = (x_ref[0] * (1.0 + scale_ref[0]) + shift_ref[0]).astype(BF16)
    cos = cos_ref[0]
    sin = sin_ref[0]
    lane = lax.broadcasted_iota(jnp.int32, (1, LANES), 1)
    lo_half = (lane % DA_HEAD_DIM) < DA_HEAD_DIM // 2
    half = DA_HEAD_DIM // 2

    def rope(t):
        fwd = pltpu.roll(t, half, 1)
        bwd = pltpu.roll(t, LANES - half, 1)
        partner = jnp.where(lo_half, bwd, fwd)
        return t * cos + partner * sin

    qk = jnp.dot(h, wqk_ref[...], preferred_element_type=F32)
    nslab = DA_QK_WIDTH // LANES
    for c in range(nslab):
        q_ref[0, :, c * LANES:(c + 1) * LANES] = (
            rope(qk[:, c * LANES:(c + 1) * LANES]) * Q_PRESCALE).astype(BF16)
        k_ref[0, :, c * LANES:(c + 1) * LANES] = rope(
            qk[:, DA_QK_WIDTH + c * LANES:DA_QK_WIDTH + (c + 1) * LANES]).astype(BF16)

    vt = lax.dot_general(wvt_ref[...], h, (((1,), (1,)), ((), ())), preferred_element_type=F32)
    tm = h.shape[0]
    for hh in range(DA_HEADS):
        for jj in range(tm // TQ):
            vt_ref[0, hh, jj, 0:DA_V_DIM, :] = vt[hh * DA_V_DIM:(hh + 1) * DA_V_DIM,
                                                  jj * TQ:(jj + 1) * TQ].astype(BF16)
            vt_ref[0, hh, jj, DA_V_DIM:VT_ROWS, :] = jnp.ones((VT_ROWS - DA_V_DIM, TQ), BF16)

    r = jnp.dot(h, wrest_ref[...], preferred_element_type=F32)
    o = 0
    mlx_ref[0] = r[:, o:o + ML_WIDTH].astype(BF16); o += ML_WIDTH
    mlv_ref[0] = r[:, o:o + ML_WIDTH].astype(BF16); o += ML_WIDTH
    mlo_ref[0] = r[:, o:o + ML_WIDTH].astype(BF16); o += ML_WIDTH
    s5u_ref[...] = r[:, o:o + S5_WIDTH]
    gt_ref[0] = lax.dot_general(wgt_ref[...], h, (((1,), (1,)), ((), ())), preferred_element_type=F32)
    gc_ref[0] = jnp.dot(h, wgc_ref[...], preferred_element_type=F32)


def _in_proj(x, shift, scale, cos_t, sin_t, w_in_l):
    bsz, s, d = x.shape
    tm = TM_PROJ
    w = w_in_l.astype(BF16)
    wqk = w[:, :OFF_DA_V]
    wvt = w[:, OFF_DA_V:OFF_ML_X].T
    wrest = jnp.concatenate([w[:, OFF_ML_X:OFF_ML_I], w[:, OFF_S5_U:]], axis=1)
    wg = w[:, OFF_ML_I:OFF_S5_U]
    wgt = wg.T
    wgc = jnp.pad(wg, ((0, 0), (0, LANES - GATE_PAD)))
    nrest = wrest.shape[1]
    shift3 = shift.reshape(bsz, 1, d)
    scale3 = scale.reshape(bsz, 1, d)
    tok = lambda b, i: (b, i, 0)
    per_b = lambda b, i: (b, 0, 0)
    const2 = lambda b, i: (0, 0)
    out_shapes = [
        jax.ShapeDtypeStruct((bsz, s, DA_QK_WIDTH), BF16),
        jax.ShapeDtypeStruct((bsz, s, DA_QK_WIDTH), BF16),
        jax.ShapeDtypeStruct((bsz, DA_HEADS, s // TQ, VT_ROWS, TQ), BF16),
        jax.ShapeDtypeStruct((bsz, s, ML_WIDTH), BF16),
        jax.ShapeDtypeStruct((bsz, s, ML_WIDTH), BF16),
        jax.ShapeDtypeStruct((bsz, s, ML_WIDTH), BF16),
        jax.ShapeDtypeStruct((bsz, GATE_PAD, s), F32),
        jax.ShapeDtypeStruct((bsz, s, LANES), F32),
        jax.ShapeDtypeStruct((s, bsz * S5_WIDTH), F32),
    ]
    out_specs = [
        pl.BlockSpec((1, tm, DA_QK_WIDTH), tok),
        pl.BlockSpec((1, tm, DA_QK_WIDTH), tok),
        pl.BlockSpec((1, DA_HEADS, tm // TQ, VT_ROWS, TQ), lambda b, i: (b, 0, i, 0, 0)),
        pl.BlockSpec((1, tm, ML_WIDTH), tok),
        pl.BlockSpec((1, tm, ML_WIDTH), tok),
        pl.BlockSpec((1, tm, ML_WIDTH), tok),
        pl.BlockSpec((1, GATE_PAD, tm), lambda b, i: (b, 0, i)),
        pl.BlockSpec((1, tm, LANES), tok),
        pl.BlockSpec((tm, S5_WIDTH), lambda b, i: (i, b)),
    ]
    return pl.pallas_call(
        _in_proj_kernel,
        grid=(bsz, s // tm),
        in_specs=[pl.BlockSpec((1, tm, d), tok),
                  pl.BlockSpec((1, 1, d), per_b),
                  pl.BlockSpec((1, 1, d), per_b),
                  pl.BlockSpec((1, tm, LANES), tok),
                  pl.BlockSpec((1, tm, LANES), tok),
                  pl.BlockSpec((d, OFF_DA_V), const2),
                  pl.BlockSpec((DA_WIDTH, d), const2),
                  pl.BlockSpec((d, nrest), const2),
                  pl.BlockSpec((GATE_PAD, d), const2),
                  pl.BlockSpec((d, LANES), const2)],
        out_specs=out_specs,
        out_shape=out_shapes,
        compiler_params=_cparams(("arbitrary", "arbitrary")),
        name="in_proj",
    )(x, shift3, scale3, cos_t, sin_t, wqk, wvt, wrest, wgt, wgc)


def _diff_attn_kernel(lam_init, lamv_ref, gain_ref, q_ref, k_ref, vt_ref, o_ref, acc_s, m_s):
    qi = pl.program_id(2)
    tq = q_ref.shape[1]
    lane = lax.broadcasted_iota(jnp.int32, (1, LANES), 1)
    first = lane < DA_HEAD_DIM
    q = q_ref[0]
    zero = jnp.zeros_like(q)
    qm = (jnp.where(first, q, zero), jnp.where(first, zero, q))
    acc_s[...] = jnp.zeros_like(acc_s)
    m_s[...] = jnp.full(m_s.shape, NEG, F32)

    def step(j, nblk, masked):
        tk = nblk * tq
        kb = k_ref[0, pl.ds(pl.multiple_of(j * tq, tq), tk), :]
        vtb = vt_ref[0, 0, j] if nblk == 1 else jnp.concatenate([vt_ref[0, 0, j + b] for b in range(nblk)], axis=1)
        for c in range(2):
            st = lax.dot_general(kb, qm[c], (((1,), (1,)), ((), ())), preferred_element_type=F32)
            if masked:
                key_i = lax.broadcasted_iota(jnp.int32, (tk, tq), 0) - (tk - tq)
                qry_i = lax.broadcasted_iota(jnp.int32, (tk, tq), 1)
                st = jnp.where(key_i <= qry_i, st, NEG)
            m_prev = m_s[c]
            m_new = jnp.maximum(m_prev, jnp.max(st, axis=0, keepdims=True))
            alpha = jnp.exp2(m_prev - m_new)
            p = jnp.exp2(st - m_new).astype(BF16)
            acc_s[c] = alpha * acc_s[c] + jnp.dot(vtb, p, preferred_element_type=F32)
            m_s[c] = m_new

    def body(jj, carry):
        step(2 * jj, 2, False)
        return carry

    lax.fori_loop(0, qi // 2, body, 0)

    @pl.when(qi % 2 == 1)
    def _():
        step(qi - 1, 2, True)

    @pl.when(qi % 2 == 0)
    def _():
        step(qi, 1, True)

    outs = []
    for c in range(2):
        acc = acc_s[c]
        outs.append(acc[:DA_V_DIM] / acc[DA_V_DIM:DA_V_DIM + 1])

    lamv = lamv_ref[...]
    lam = (jnp.exp(jnp.sum(lamv[0:1] * lamv[1:2], axis=1, keepdims=True))
           - jnp.exp(jnp.sum(lamv[2:3] * lamv[3:4], axis=1, keepdims=True)) + lam_init)
    ot = outs[0] - lam * outs[1]
    ms = jnp.mean(ot * ot, axis=0, keepdims=True)
    ot = ot * (lax.rsqrt(ms + LN_EPS) * (1.0 - lam_init))
    o_ref[0] = (ot.T * gain_ref[...]).astype(o_ref.dtype)


def _diff_attn(q, k, vt, lamv, gain, lam_init):
    bsz, s, _ = q.shape
    tq = TQ
    nq = s // tq
    return pl.pallas_call(
        functools.partial(_diff_attn_kernel, lam_init),
        grid=(bsz, DA_HEADS, nq),
        in_specs=[pl.BlockSpec((4, DA_HEAD_DIM), lambda b, h, i: (0, 0)),
                  pl.BlockSpec((1, DA_V_DIM), lambda b, h, i: (0, 0)),
                  pl.BlockSpec((1, tq, DA_V_DIM), lambda b, h, i: (b, i, h)),
                  pl.BlockSpec((1, s, DA_V_DIM), lambda b, h, i: (b, 0, h)),
                  pl.BlockSpec((1, 1, nq, VT_ROWS, tq), lambda b, h, i: (b, h, 0, 0, 0))],
        out_specs=pl.BlockSpec((1, tq, DA_V_DIM), lambda b, h, i: (b, i, h)),
        out_shape=jax.ShapeDtypeStruct((bsz, s, DA_WIDTH), BF16),
        scratch_shapes=[pltpu.VMEM((2, VT_ROWS, tq), F32), pltpu.VMEM((2, 1, tq), F32)],
        compiler_params=_cparams(("arbitrary", "arbitrary", "arbitrary")),
        name="diff_attn",
    )(lamv, gain.reshape(1, DA_V_DIM), q, k, vt)


def _log_sigmoid(x):
    return jnp.minimum(x, 0.0) - jnp.log(1.0 + jnp.exp(-jnp.abs(x)))


def _split3(a):
    hi = a.astype(BF16)
    r1 = a - hi.astype(F32)
    mid = r1.astype(BF16)
    lo = (r1 - mid.astype(F32)).astype(BF16)
    return hi, mid, lo


def _mlstm_kernel(x_ref, v_ref, o_ref, gt_ref, gc_ref, cw_ref, cb_ref, wq_ref, wkt_ref, gbt_ref, gbc_ref,
                  ng_ref, hmean_ref, y_ref, xc_s, c_s, m_s):
    nb, s = x_ref.shape[0], x_ref.shape[1]
    L = ML_CHUNK
    H, dh = ML_HEADS, ML_HEAD_DIM
    nc = s // L
    cw = cw_ref[...]
    row = lax.broadcasted_iota(jnp.int32, (s, 1), 0)
    for bi in range(nb):
        x = x_ref[bi].astype(F32)
        xc = x * cw[ML_CONV - 1:ML_CONV]
        for j in range(1, ML_CONV):
            xs = jnp.where(row >= j, pltpu.roll(x, j, 0), 0.0)
            xc = xc + xs * cw[ML_CONV - 1 - j:ML_CONV - j]
        xc = xc + cb_ref[...]
        xc_s[bi] = (xc * _sigmoid(xc)).astype(BF16)

    c_s[...] = jnp.zeros_like(c_s)
    m_s[...] = jnp.full(m_s.shape, NEG, F32)

    ri = lax.broadcasted_iota(jnp.int32, (L, L), 0)
    ci = lax.broadcasted_iota(jnp.int32, (L, L), 1)
    causal = ci <= ri
    tril = causal.astype(BF16)
    triu = (ri <= ci).astype(BF16)
    lane = lax.broadcasted_iota(jnp.int32, (1, dh), 1)
    one_hot0 = jnp.broadcast_to((lane == 0).astype(BF16), (L, dh))

    def chunk_one(bi, ci_, t0):
        xcc = xc_s[bi, pl.ds(t0, L), :]
        qc = jnp.dot(xcc, wq_ref[...], preferred_element_type=F32).astype(BF16)
        ktc = lax.dot_general(wkt_ref[...], xcc, (((1,), (1,)), ((), ())),
                              preferred_element_type=F32)
        g_rows = gt_ref[bi, ci_] + gbt_ref[...]
        g_cols = gc_ref[bi, pl.ds(t0, L), :] + gbc_ref[...]
        lf_rows = _log_sigmoid(g_rows)
        lf_cols = _log_sigmoid(g_cols)
        r3 = jnp.dot(jnp.concatenate(_split3(lf_rows), axis=0), triu, preferred_element_type=F32)
        b_rows = r3[0:GATE_PAD] + r3[GATE_PAD:2 * GATE_PAD] + r3[2 * GATE_PAD:]
        c3 = jnp.dot(tril, jnp.concatenate(_split3(lf_cols), axis=1), preferred_element_type=F32)
        b_cols = c3[:, 0:LANES] + c3[:, LANES:2 * LANES] + c3[:, 2 * LANES:]
        vch = v_ref[bi, pl.ds(t0, L), :]
        och = o_ref[bi, pl.ds(t0, L), :].astype(F32)
        hs = []
        for h in range(H):
            br = b_rows[H + h:H + h + 1, :]
            ir = g_rows[h:h + 1, :]
            bc = b_cols[:, H + h:H + h + 1]
            m_prev = m_s[bi, h]
            log_d = jnp.where(causal, bc - br + ir, NEG)
            inter = bc + m_prev
            mx = jnp.maximum(inter, jnp.max(log_d, axis=1, keepdims=True))
            dmat = jnp.exp(log_d - mx)
            dec = jnp.exp(inter - mx)
            qh = qc[:, h * dh:(h + 1) * dh]
            kth = ktc[h * dh:(h + 1) * dh, :]
            vaug = jnp.concatenate([vch[:, h * dh:(h + 1) * dh], one_hot0], axis=1)
            sm = (jnp.dot(qh, kth.astype(BF16), preferred_element_type=F32) * dmat).astype(BF16)
            c_prev = c_s[bi, h]
            na = (jnp.dot(sm, vaug, preferred_element_type=F32)
                  + dec * jnp.dot(qh, c_prev.astype(BF16), preferred_element_type=F32))
            den = na[:, dh:dh + 1]
            hs.append(na[:, :dh] / jnp.maximum(jnp.abs(den), jnp.exp(-mx)))
            g_tot = br[:, L - 1:L]
            a_row = g_tot - br + ir
            m_new = jnp.maximum(g_tot + m_prev, jnp.max(a_row, axis=1, keepdims=True))
            decay = jnp.exp(g_tot + m_prev - m_new)
            w_row = jnp.exp(a_row - m_new)
            kw = (kth * w_row).astype(BF16)
            c_s[bi, h] = decay * c_prev + jnp.dot(kw, vaug, preferred_element_type=F32)
            m_s[bi, h] = m_new
        hcat = jnp.concatenate(hs, axis=1)
        m3 = jnp.dot(jnp.concatenate(_split3(hcat * hcat), axis=0), hmean_ref[...],
                     preferred_element_type=F32)
        ms = m3[0:L] + m3[L:2 * L] + m3[2 * L:]
        y = hcat * lax.rsqrt(ms + LN_EPS) * ng_ref[...] * _sigmoid(och)
        y_ref[bi, pl.ds(t0, L), :] = y.astype(y_ref.dtype)

    def chunk(ci_, _):
        t0 = pl.multiple_of(ci_ * L, L)
        for bi in range(nb):
            chunk_one(bi, ci_, t0)
        return 0

    lax.fori_loop(0, nc, chunk, 0)


def _mlstm(mlx, mlv, mlo, g_t, g_c, conv_w, conv_b, w_q, w_k, gate_b, norm_g):
    bsz, s, _ = mlx.shape
    H, dh = ML_HEADS, ML_HEAD_DIM
    eye = jnp.eye(H, dtype=F32)
    wq_bd = jnp.einsum('hde,hg->hdge', w_q, eye).reshape(ML_WIDTH, ML_WIDTH).astype(BF16)
    wk_bd = jnp.einsum('hde,hg->hdge', w_k * (dh ** -0.5), eye).reshape(ML_WIDTH, ML_WIDTH)
    wkt_bd = wk_bd.T.astype(BF16)
    gbt = gate_b.reshape(GATE_PAD, 1)
    gbc = jnp.pad(gate_b.reshape(1, GATE_PAD), ((0, 0), (0, LANES - GATE_PAD)))
    hmean = jnp.kron(eye, jnp.full((dh, dh), 1.0 / dh, F32)).astype(BF16)
    nc = s // ML_CHUNK
    g_t4 = g_t.reshape(bsz, GATE_PAD, nc, ML_CHUNK).transpose(0, 2, 1, 3)
    tok = lambda b: (b, 0, 0)
    c2 = lambda b: (0, 0)
    nb = ML_NB
    return pl.pallas_call(
        _mlstm_kernel,
        grid=(bsz // nb,),
        in_specs=[pl.BlockSpec((nb, s, ML_WIDTH), tok),
                  pl.BlockSpec((nb, s, ML_WIDTH), tok),
                  pl.BlockSpec((nb, s, ML_WIDTH), tok),
                  pl.BlockSpec((nb, nc, GATE_PAD, ML_CHUNK), lambda b: (b, 0, 0, 0)),
                  pl.BlockSpec((nb, s, LANES), tok),
                  pl.BlockSpec((ML_CONV, ML_WIDTH), c2),
                  pl.BlockSpec((1, ML_WIDTH), c2),
                  pl.BlockSpec((ML_WIDTH, ML_WIDTH), c2),
                  pl.BlockSpec((ML_WIDTH, ML_WIDTH), c2),
                  pl.BlockSpec((GATE_PAD, 1), c2),
                  pl.BlockSpec((1, LANES), c2),
                  pl.BlockSpec((1, ML_WIDTH), c2),
                  pl.BlockSpec((ML_WIDTH, ML_WIDTH), c2)],
        out_specs=pl.BlockSpec((nb, s, ML_WIDTH), tok),
        out_shape=jax.ShapeDtypeStruct((bsz, s, ML_WIDTH), BF16),
        scratch_shapes=[pltpu.VMEM((nb, s, ML_WIDTH), BF16),
                        pltpu.VMEM((nb, H, dh, LANES), F32),
                        pltpu.VMEM((nb, H, 1, 1), F32)],
        compiler_params=_cparams(("arbitrary",)),
        name="mlstm",
    )(mlx, mlv, mlo, g_t4, g_c, conv_w, conv_b.reshape(1, ML_WIDTH), wq_bd, wkt_bd, gbt, gbc,
      norm_g.reshape(1, ML_WIDTH), hmean)


def _gelu_tanh(x):
    return 0.5 * x * (1.0 + jnp.tanh(math.sqrt(2.0 / math.pi) * (x + 0.044715 * (x * x * x))))


def _s5_kernel(u_ref, are_ref, aim_ref, bcat_ref, ccat_ref, d_ref, wglu_ref, y_ref, xs_s, st_s):
    tc, bsz, w = u_ref.shape
    n = S5_NSTATE

    @pl.when(pl.program_id(0) == 0)
    def _():
        st_s[...] = jnp.zeros_like(st_s)

    u = u_ref[...].reshape(tc * bsz, w)
    xs_s[...] = jnp.dot(u.astype(BF16), bcat_ref[...], preferred_element_type=F32).reshape(tc, bsz, 2 * n)
    a_re = jnp.broadcast_to(are_ref[...], (bsz, n))
    a_im = jnp.broadcast_to(aim_ref[...], (bsz, n))

    def step(t, carry):
        x_re, x_im = carry
        bu = xs_s[t]
        n_re = a_re * x_re - a_im * x_im + bu[:, :n]
        n_im = a_re * x_im + a_im * x_re + bu[:, n:]
        xs_s[t] = jnp.concatenate([n_re, n_im], axis=1)
        return n_re, n_im

    x_re, x_im = lax.fori_loop(0, tc, step, (st_s[0], st_s[1]), unroll=S5_UNROLL)
    st_s[0] = x_re
    st_s[1] = x_im

    xs = xs_s[...].reshape(tc * bsz, 2 * n).astype(BF16)
    y = jnp.dot(xs, ccat_ref[...], preferred_element_type=F32) + d_ref[...] * u
    z = jnp.dot(_gelu_tanh(y).astype(BF16), wglu_ref[...], preferred_element_type=F32)
    out = z[:, :w] * _sigmoid(z[:, w:])
    y_ref[...] = out.reshape(tc, bsz, w).astype(y_ref.dtype)


def _s5_params(a_re, a_im, log_dt, b_re, b_im, c_re, c_im, w_glu):
    G, P, Hc = S5_GROUPS, S5_STATE, S5_GROUP
    dt = jnp.exp(log_dt)[:, None]
    mag = jnp.exp(a_re * dt)
    ab_re = mag * jnp.cos(a_im * dt)
    ab_im = mag * jnp.sin(a_im * dt)
    nr, ni = ab_re - 1.0, ab_im
    den = a_re * a_re + a_im * a_im
    fr = (nr * a_re + ni * a_im) / den
    fi = (ni * a_re - nr * a_im) / den
    bb_re = fr[..., None] * b_re - fi[..., None] * b_im
    bb_im = fr[..., None] * b_im + fi[..., None] * b_re
    eye = jnp.eye(G, dtype=F32)
    bd = lambda t, sub: jnp.einsum(sub, t, eye)
    bre = bd(bb_re, 'gph,gk->ghkp').reshape(G * Hc, G * P)
    bim = bd(bb_im, 'gph,gk->ghkp').reshape(G * Hc, G * P)
    bcat = jnp.concatenate([bre, bim], axis=1).astype(BF16)
    cre = bd(c_re, 'ghp,gk->gpkh').reshape(G * P, G * Hc)
    cim = bd(c_im, 'ghp,gk->gpkh').reshape(G * P, G * Hc)
    ccat = jnp.concatenate([cre, -cim], axis=0).astype(BF16)
    wv = bd(w_glu[:, :, :Hc], 'ghj,gk->ghkj').reshape(G * Hc, G * Hc)
    wg = bd(w_glu[:, :, Hc:], 'ghj,gk->ghkj').reshape(G * Hc, G * Hc)
    wglu = jnp.concatenate([wv, wg], axis=1).astype(BF16)
    return ab_re.reshape(1, G * P), ab_im.reshape(1, G * P), bcat, ccat, wglu


def _s5(u_tm, bsz, params, d_skip):
    s = u_tm.shape[0]
    w = S5_WIDTH
    n = S5_NSTATE
    are, aim, bcat, ccat, wglu = params
    u3 = u_tm.reshape(s, bsz, w)
    tc = S5_TC
    c2 = lambda i: (0, 0)
    y = pl.pallas_call(
        _s5_kernel,
        grid=(s // tc,),
        in_specs=[pl.BlockSpec((tc, bsz, w), lambda i: (i, 0, 0)),
                  pl.BlockSpec((1, n), c2),
                  pl.BlockSpec((1, n), c2),
                  pl.BlockSpec((w, 2 * n), c2),
                  pl.BlockSpec((2 * n, w), c2),
                  pl.BlockSpec((1, w), c2),
                  pl.BlockSpec((w, 2 * w), c2)],
        out_specs=pl.BlockSpec((tc, bsz, w), lambda i: (i, 0, 0)),
        out_shape=jax.ShapeDtypeStruct((s, bsz, w), F32),
        scratch_shapes=[pltpu.VMEM((tc, bsz, 2 * n), F32),
                        pltpu.VMEM((2, bsz, n), F32)],
        compiler_params=_cparams(("arbitrary",)),
        name="s5",
    )(u3, are, aim, bcat, ccat, d_skip.reshape(1, w), wglu)
    return y.reshape(s, bsz * w)


def _layer_norm(z, g, b):
    mu = jnp.mean(z, axis=1, keepdims=True)
    zc = z - mu
    var = jnp.mean(zc * zc, axis=1, keepdims=True)
    return zc * lax.rsqrt(var + LN_EPS) * g + b


def _out_proj_kernel(yda_ref, yml_ref, ys5_ref, x_ref, gate_ref, lng_ref, lnb_ref, shift_ref, scale_ref,
                     wout_ref, wrt_ref, brt_ref, x1_ref, h2_ref, eid_ref, prob_ref, cnt_ref):
    y = jnp.dot(yda_ref[0], wout_ref[0:DA_WIDTH, :], preferred_element_type=F32)
    y = y + jnp.dot(yml_ref[0], wout_ref[DA_WIDTH:DA_WIDTH + ML_WIDTH, :], preferred_element_type=F32)
    y = y + jnp.dot(ys5_ref[...].astype(BF16), wout_ref[DA_WIDTH + ML_WIDTH:, :], preferred_element_type=F32)
    x1 = _layer_norm(DN_ALPHA * x_ref[0] + (1.0 + gate_ref[0]) * y, lng_ref[...], lnb_ref[...])
    x1_ref[0] = x1
    h2 = x1 * (1.0 + scale_ref[0]) + shift_ref[0]
    h2_ref[...] = _pack_bf16_pairs(h2)
    h_hi = h2.astype(BF16)
    h_lo = (h2 - h_hi.astype(F32)).astype(BF16)
    nt_dot = lambda a, b: lax.dot_general(a, b, (((1,), (1,)), ((), ())), preferred_element_type=F32)
    by_hi = nt_dot(wrt_ref[...], h_hi)
    logits = (by_hi[:N_EXPERTS] + by_hi[N_EXPERTS:] + nt_dot(wrt_ref[0:N_EXPERTS, :], h_lo)
              + brt_ref[...])
    eidx = lax.broadcasted_iota(jnp.int32, logits.shape, 0)
    vals, ids = [], []
    for _ in range(TOP_K):
        mx = jnp.max(logits, axis=0, keepdims=True)
        sel = jnp.min(jnp.where(logits == mx, eidx, N_EXPERTS), axis=0, keepdims=True)
        vals.append(mx)
        ids.append(sel)
        logits = jnp.where(eidx == sel, -jnp.inf, logits)
    ex = [jnp.exp(v - vals[0]) for v in vals]
    tot = ex[0] + ex[1] + ex[2] + ex[3]
    zi = jnp.zeros_like(ids[0])
    eid_ref[...] = jnp.concatenate(ids + [zi] * (SUBLANES - TOP_K), axis=0)

    @pl.when(jnp.logical_and(pl.program_id(0) == 0, pl.program_id(1) == 0))
    def _():
        cnt_ref[...] = jnp.zeros_like(cnt_ref)

    member = jnp.zeros(logits.shape, F32)
    for sel in ids:
        member = member + (eidx == sel).astype(F32)
    cnt_ref[...] = cnt_ref[...] + jnp.sum(member, axis=1, keepdims=True)
    zf = jnp.zeros((LANES - TOP_K, tot.shape[1]), F32)
    prob_ref[...] = jnp.concatenate([e / tot for e in ex] + [zf], axis=0).T


def _out_proj(y_da, y_ml, y_s5, x, gate, ln_g, ln_b, shift2, scale2, w_out_l, w_router_l, b_router_l):
    bsz, s, d = x.shape
    tm = TM_PROJ
    nt = s // tm
    tok = lambda b, i: (b, i, 0)
    per_b = lambda b, i: (b, 0, 0)
    c2 = lambda b, i: (0, 0)
    r3 = lambda a: a.reshape(bsz, 1, d)
    flat = lambda b, i: (0, b * nt + i)
    wr_t = w_router_l.T
    wr_hi = wr_t.astype(BF16)
    wr_lo = (wr_t - wr_hi.astype(F32)).astype(BF16)
    return pl.pallas_call(
        _out_proj_kernel,
        grid=(bsz, nt),
        in_specs=[pl.BlockSpec((1, tm, DA_WIDTH), tok),
                  pl.BlockSpec((1, tm, ML_WIDTH), tok),
                  pl.BlockSpec((tm, S5_WIDTH), lambda b, i: (i, b)),
                  pl.BlockSpec((1, tm, d), tok),
                  pl.BlockSpec((1, 1, d), per_b),
                  pl.BlockSpec((1, d), c2),
                  pl.BlockSpec((1, d), c2),
                  pl.BlockSpec((1, 1, d), per_b),
                  pl.BlockSpec((1, 1, d), per_b),
                  pl.BlockSpec((d, d), c2),
                  pl.BlockSpec((2 * N_EXPERTS, d), c2),
                  pl.BlockSpec((N_EXPERTS, 1), c2)],
        out_specs=[pl.BlockSpec((1, tm, d), tok),
                   pl.BlockSpec((tm, d // 2), lambda b, i: (b * nt + i, 0)),
                   pl.BlockSpec((SUBLANES, tm), flat),
                   pl.BlockSpec((tm, LANES), lambda b, i: (b * nt + i, 0)),
                   pl.BlockSpec((N_EXPERTS, LANES), c2)],
        out_shape=[jax.ShapeDtypeStruct((bsz, s, d), F32),
                   jax.ShapeDtypeStruct((bsz * s, d // 2), jnp.int32),
                   jax.ShapeDtypeStruct((SUBLANES, bsz * s), jnp.int32),
                   jax.ShapeDtypeStruct((bsz * s, LANES), F32),
                   jax.ShapeDtypeStruct((N_EXPERTS, LANES), F32)],
        compiler_params=_cparams(("arbitrary", "arbitrary")),
        name="out_proj",
    )(y_da, y_ml, y_s5, x, r3(gate), ln_g.reshape(1, d), ln_b.reshape(1, d), r3(shift2), r3(scale2),
      w_out_l.astype(BF16), jnp.concatenate([wr_hi, wr_lo], axis=0), b_router_l.reshape(N_EXPERTS, 1))


META_END, META_PAD, META_CNT = 0, 1, 2


def _route_kernel(eid_ref, cnt_ref, pos_ref, meta_ref, carry_s, start_s):
    i = pl.program_id(0)
    tb = eid_ref.shape[1]
    ntp = meta_ref.shape[1]
    tm = TM_MOE

    @pl.when(i == 0)
    def _():
        cnt = cnt_ref[...]
        padded = jnp.floor((cnt + (tm - 1)) * (1.0 / tm)) * tm
        er = lax.broadcasted_iota(jnp.int32, (N_EXPERTS, N_EXPERTS), 0)
        ec = lax.broadcasted_iota(jnp.int32, (N_EXPERTS, N_EXPERTS), 1)
        ends = jnp.dot((ec <= er).astype(F32), padded, preferred_element_type=F32, precision=HIGHEST)
        start_s[...] = ends - padded
        carry_s[...] = jnp.zeros_like(carry_s)
        lane = lax.broadcasted_iota(jnp.int32, (N_EXPERTS, ntp), 1)
        sub = lax.broadcasted_iota(jnp.int32, (N_EXPERTS, ntp), 0)
        diag = lane == sub

        def as_row(col):
            return jnp.sum(jnp.where(diag, col, 0.0), axis=0, keepdims=True)

        zero = jnp.zeros((SUBLANES - 3, ntp), F32)
        meta_ref[...] = jnp.concatenate([as_row(ends[:, 0:1]), as_row(padded[:, 0:1]), as_row(cnt[:, 0:1]), zero],
                                        axis=0).astype(jnp.int32)

    eid = eid_ref[...]
    eidx = lax.broadcasted_iota(jnp.int32, (N_EXPERTS, tb), 0)
    hot = [eidx == eid[k:k + 1, :] for k in range(TOP_K)]
    member = jnp.zeros((N_EXPERTS, tb), F32)
    for k in range(TOP_K):
        member = member + hot[k].astype(F32)
    ri = lax.broadcasted_iota(jnp.int32, (tb, tb), 0)
    ci = lax.broadcasted_iota(jnp.int32, (tb, tb), 1)
    triu = (ri <= ci).astype(BF16)
    incl = jnp.dot(member.astype(BF16), triu, preferred_element_type=F32)
    slot = incl - member + carry_s[:, 0:1] + start_s[:, 0:1]
    rows = [jnp.sum(jnp.where(hot[k], slot, 0.0), axis=0, keepdims=True) for k in range(TOP_K)]
    zr = jnp.zeros_like(rows[0])
    pos_ref[...] = jnp.concatenate(rows + [zr] * (SUBLANES - TOP_K), axis=0).astype(jnp.int32)
    carry_s[...] = carry_s[...] + jnp.sum(member, axis=1, keepdims=True)


def _route(eid, counts):
    t = eid.shape[1]
    tb = TB_RANK
    ntp = LANES
    pos8, meta = pl.pallas_call(
        _route_kernel,
        grid=(t // tb,),
        in_specs=[pl.BlockSpec((SUBLANES, tb), lambda i: (0, i)),
                  pl.BlockSpec((N_EXPERTS, LANES), lambda i: (0, 0))],
        out_specs=[pl.BlockSpec((SUBLANES, tb), lambda i: (0, i)),
                   pl.BlockSpec((SUBLANES, ntp), lambda i: (0, 0))],
        out_shape=[jax.ShapeDtypeStruct((SUBLANES, t), jnp.int32),
                   jax.ShapeDtypeStruct((SUBLANES, ntp), jnp.int32)],
        scratch_shapes=[pltpu.VMEM((N_EXPERTS, LANES), F32), pltpu.VMEM((N_EXPERTS, LANES), F32)],
        compiler_params=_cparams(("arbitrary",)),
        name="route",
    )(eid, counts)
    return pos8, meta


def _sc_workers():
    info = plsc.get_sparse_core_info()
    return info.num_cores, info.num_cores * info.num_subcores


def _dispatch(h2, pos8, n_rows):
    t, d = h2.shape
    n_cores, n_workers = _sc_workers()
    tpw = t // n_workers
    ch = SC_CHUNK
    mesh = plsc.VectorSubcoreMesh(core_axis_name="c", subcore_axis_name="s")

    @functools.partial(
        pl.kernel, mesh=mesh,
        out_type=jax.ShapeDtypeStruct((n_rows, d), h2.dtype),
        scratch_types=[pltpu.VMEM((ch,), jnp.int32)] * TOP_K + [pltpu.VMEM((ch, d), h2.dtype),
                                                                pltpu.SemaphoreType.DMA])
    def scatter_rows(h_hbm, pos_hbm, out_hbm, i0, i1, i2, i3, rows_v, sem):
        idx = (i0, i1, i2, i3)
        base = (lax.axis_index("s") * n_cores + lax.axis_index("c")) * tpw

        @pl.loop(0, tpw // ch)
        def _(i):
            off = base + i * ch
            pltpu.sync_copy(h_hbm.at[pl.ds(off, ch)], rows_v)
            for k in range(TOP_K):
                pltpu.sync_copy(pos_hbm.at[k, pl.ds(off, ch)], idx[k])
            copies = [pltpu.async_copy(rows_v, out_hbm.at[idx[k]], sem) for k in range(TOP_K)]
            for cp in copies:
                cp.wait()

    return scatter_rows(h2, pos8)


def _gather_expert_rows(ys, pos8):
    _, d = ys.shape
    t = pos8.shape[1]
    n_cores, n_workers = _sc_workers()
    tpw = t // n_workers
    ch = SC_CHUNK
    mesh = plsc.VectorSubcoreMesh(core_axis_name="c", subcore_axis_name="s")

    @functools.partial(
        pl.kernel, mesh=mesh,
        out_type=jax.ShapeDtypeStruct((TOP_K, t, d), ys.dtype),
        scratch_types=[pltpu.VMEM((ch,), jnp.int32), pltpu.VMEM((ch, d), ys.dtype), pltpu.SemaphoreType.DMA])
    def gather_rows(ys_hbm, pos_hbm, out_hbm, idx_v, rows_v, sem):
        base = (lax.axis_index("s") * n_cores + lax.axis_index("c")) * tpw

        @pl.loop(0, tpw // ch)
        def _(i):
            off = base + i * ch
            for k in range(TOP_K):
                pltpu.sync_copy(pos_hbm.at[k, pl.ds(off, ch)], idx_v)
                pltpu.async_copy(ys_hbm.at[idx_v], rows_v, sem).wait()
                pltpu.sync_copy(rows_v, out_hbm.at[k, pl.ds(off, ch)])

    return gather_rows(ys, pos8)


def _pack_bf16_pairs(a):
    n = a.shape[1] // 2
    lo = pltpu.bitcast(a[:, :n].astype(BF16).astype(F32), jnp.int32)
    hi = pltpu.bitcast(a[:, n:].astype(BF16).astype(F32), jnp.int32)
    return jnp.bitwise_or(jnp.bitwise_and(hi, -65536), jnp.bitwise_and(lax.shift_right_logical(lo, 16), 65535))


def _unpack_bf16_pairs(p):
    lo = pltpu.bitcast(lax.shift_left(p, 16), F32)
    hi = pltpu.bitcast(jnp.bitwise_and(p, -65536), F32)
    return lo, hi


def _expert_kernel(meta_ref, xs_ref, wup_ref, bup_ref, wdn_ref, bdn_ref, ys_ref,
                   wup_s, wdn_s, xbuf, obuf, in_sem, out_sem):
    e = pl.program_id(0)
    tm = TM_MOE
    half = D_MODEL // 2
    pad = meta_ref[META_PAD, e]
    n_t = pad // tm
    row0 = meta_ref[META_END, e] - pad
    cnt = meta_ref[META_CNT, e]

    def rows(i):
        return pl.ds(pl.multiple_of(row0 + i * tm, tm), tm)

    def x_copy(i, slot):
        return pltpu.make_async_copy(xs_ref.at[rows(i)], xbuf.at[slot], in_sem.at[slot])

    def y_copy(i, slot):
        return pltpu.make_async_copy(obuf.at[slot], ys_ref.at[rows(i)], out_sem.at[slot])

    @pl.when(n_t > 0)
    def _():
        x_copy(0, 0).start()
        wup_s[...] = wup_ref[0, 0].astype(BF16)
        wdn_s[...] = wdn_ref[0, 0].astype(BF16)

        def tile(i, carry):
            slot = lax.rem(i, 2)
            x_copy(i, slot).wait()

            @pl.when(i + 1 < n_t)
            def _():
                x_copy(i + 1, 1 - slot).start()

            @pl.when(i >= 2)
            def _():
                y_copy(i - 2, slot).wait()

            row = lax.broadcasted_iota(jnp.int32, (tm, 1), 0)
            lo, hi = _unpack_bf16_pairs(jnp.where(row < cnt - i * tm, xbuf[slot], 0))
            z = (jnp.dot(lo.astype(BF16), wup_s[0:half, :], preferred_element_type=F32)
                 + jnp.dot(hi.astype(BF16), wup_s[half:, :], preferred_element_type=F32) + bup_ref[0, 0])
            glu = jnp.minimum(z[:, :D_EXPERT], SWIGLU_LIMIT)
            lin = jnp.clip(z[:, D_EXPERT:], -SWIGLU_LIMIT, SWIGLU_LIMIT)
            act = (glu * _sigmoid(SWIGLU_ALPHA * glu) * (lin + 1.0)).astype(BF16)
            y = jnp.dot(act, wdn_s[...], preferred_element_type=F32) + bdn_ref[0, 0]
            obuf[slot] = _pack_bf16_pairs(y)
            y_copy(i, slot).start()
            return carry

        lax.fori_loop(0, n_t, tile, 0)

        @pl.when(n_t >= 2)
        def _():
            y_copy(n_t - 2, lax.rem(n_t, 2)).wait()

        y_copy(n_t - 1, lax.rem(n_t - 1, 2)).wait()

    @pl.when(e == N_EXPERTS - 1)
    def _():
        obuf[0] = jnp.zeros((tm, half), jnp.int32)

        def fill(i, carry):
            cp = pltpu.make_async_copy(obuf.at[0], ys_ref.at[pl.ds(pl.multiple_of(i * tm, tm), tm)], out_sem.at[0])
            cp.start()
            cp.wait()
            return carry

        lax.fori_loop(meta_ref[META_END, N_EXPERTS - 1] // tm, ys_ref.shape[0] // tm, fill, 0)


def _expert_mlp(xs, meta, layer, w_up, b_up, w_down, b_down):
    n_rows, half = xs.shape
    d = 2 * half
    tm = TM_MOE
    f = w_up.shape[-1]
    b_up4 = b_up.reshape(DEPTH, N_EXPERTS, 1, f)
    b_dn4 = b_down.reshape(DEPTH, N_EXPERTS, 1, d)
    wsel = lambda e, m: (layer, e, 0, 0)
    grid_spec = pltpu.PrefetchScalarGridSpec(
        num_scalar_prefetch=1,
        grid=(N_EXPERTS,),
        in_specs=[pl.BlockSpec(memory_space=pl.ANY),
                  pl.BlockSpec((1, 1, d, f), wsel),
                  pl.BlockSpec((1, 1, 1, f), wsel),
                  pl.BlockSpec((1, 1, f // 2, d), wsel),
                  pl.BlockSpec((1, 1, 1, d), wsel)],
        out_specs=pl.BlockSpec(memory_space=pl.ANY),
        scratch_shapes=[pltpu.VMEM((d, f), BF16), pltpu.VMEM((f // 2, d), BF16),
                        pltpu.VMEM((2, tm, half), jnp.int32), pltpu.VMEM((2, tm, half), jnp.int32),
                        pltpu.SemaphoreType.DMA((2,)), pltpu.SemaphoreType.DMA((2,))],
    )
    return pl.pallas_call(
        _expert_kernel,
        grid_spec=grid_spec,
        out_shape=jax.ShapeDtypeStruct((n_rows, half), jnp.int32),
        compiler_params=_cparams(("arbitrary",)),
        name="expert_mlp",
    )(meta, xs, w_up, b_up4, w_down, b_dn4)


def _combine_kernel(rows_ref, prob_ref, x_ref, gate_ref, lng_ref, lnb_ref, o_ref):
    p = prob_ref[...]
    y = None
    for k in range(TOP_K):
        yk = p[:, k:k + 1] * jnp.concatenate(_unpack_bf16_pairs(rows_ref[k]), axis=1)
        y = yk if y is None else y + yk
    o_ref[0] = _layer_norm(DN_ALPHA * x_ref[0] + (1.0 + gate_ref[0]) * y, lng_ref[...], lnb_ref[...])


def _combine(rows, prob_c, x1, gate, ln_g, ln_b, part):
    bsz, s, d = x1.shape
    tm = TM_DISP
    nt = s // tm
    pb = bsz // MOE_PARTS
    b0 = part * pb
    return pl.pallas_call(
        _combine_kernel,
        grid=(pb, nt),
        in_specs=[pl.BlockSpec((TOP_K, tm, d // 2), lambda b, i: (0, b * nt + i, 0)),
                  pl.BlockSpec((tm, LANES), lambda b, i: ((b0 + b) * nt + i, 0)),
                  pl.BlockSpec((1, tm, d), lambda b, i: (b0 + b, i, 0)),
                  pl.BlockSpec((1, 1, d), lambda b, i: (b0 + b, 0, 0)),
                  pl.BlockSpec((1, d), lambda b, i: (0, 0)),
                  pl.BlockSpec((1, d), lambda b, i: (0, 0))],
        out_specs=pl.BlockSpec((1, tm, d), lambda b, i: (b0 + b, i, 0)),
        out_shape=jax.ShapeDtypeStruct((bsz, s, d), F32),
        input_output_aliases={2: 0},
        compiler_params=_cparams(("arbitrary", "arbitrary")),
        name="combine",
    )(rows, prob_c, x1, gate.reshape(bsz, 1, d), ln_g.reshape(1, d), ln_b.reshape(1, d))


def kernel(x, c, positions, ada_w, ada_b, w_in, lam_q1, lam_k1, lam_q2, lam_k2, da_norm_g, ml_conv_w, ml_conv_b,
           ml_w_q, ml_w_k, ml_gate_b, ml_norm_g, s5_a_re, s5_a_im, s5_log_dt, s5_b_re, s5_b_im, s5_c_re, s5_c_im,
           s5_d, s5_w_glu, w_out, ln_g, ln_b, w_router, b_router, w_up, b_up, w_down, b_down):
    bsz, s, d = x.shape
    t = bsz * s
    n_tiles_max = (t * TOP_K) // TM_MOE + N_EXPERTS
    n_rows = n_tiles_max * TM_MOE
    mod = _modulation(c, ada_w, ada_b)
    cos_t, sin_t = _rope_tables(positions)
    for l in range(DEPTH):
        shift, scale, gate = jnp.split(mod[2 * l], 3, axis=-1)
        q, k, v, mlx, mlv, mlo, g_t, g_c, s5u = _in_proj(x, shift, scale, cos_t, sin_t, w_in[l])
        lam_init = 0.8 - 0.6 * math.exp(-0.3 * l)
        lamv = jnp.stack([lam_q1[l], lam_k1[l], lam_q2[l], lam_k2[l]])
        y_da = _diff_attn(q, k, v, lamv, da_norm_g[l], lam_init)
        y_ml = _mlstm(mlx, mlv, mlo, g_t, g_c, ml_conv_w[l], ml_conv_b[l], ml_w_q[l], ml_w_k[l],
                      ml_gate_b[l], ml_norm_g[l])
        s5p = _s5_params(s5_a_re[l], s5_a_im[l], s5_log_dt[l], s5_b_re[l], s5_b_im[l], s5_c_re[l], s5_c_im[l],
                         s5_w_glu[l])
        y_s5 = _s5(s5u, bsz, s5p, s5_d[l])
        shift2, scale2, gate2 = jnp.split(mod[2 * l + 1], 3, axis=-1)
        x1, h2, eid, prob, counts = _out_proj(y_da, y_ml, y_s5, x, gate, ln_g[l, 0], ln_b[l, 0], shift2, scale2,
                                      w_out[l], w_router[l], b_router[l])
        pos8, meta = _route(eid, counts)
        xs = _dispatch(h2, pos8, n_rows)
        ys = _expert_mlp(xs, meta, l, w_up, b_up, w_down, b_down)
        tp = t // MOE_PARTS
        x = x1
        for part in range(MOE_PARTS):
            rows = _gather_expert_rows(ys, pos8[:, part * tp:(part + 1) * tp])
            x = _combine(rows, prob, x, gate2, ln_g[l, 1], ln_b[l, 1], part)
    return x
```

```python
import functools
import math

import jax
import jax.numpy as jnp
from jax import lax
from jax.experimental import pallas as pl
from jax.experimental.pallas import tpu as pltpu
from jax.experimental.pallas import tpu_sc as plsc

F32 = jnp.float32
BF16 = jnp.bfloat16
HIGHEST = lax.Precision.HIGHEST

D_MODEL = 1024
DEPTH = 2
DA_HEADS = 4
DA_HEAD_DIM = 64
DA_V_DIM = 2 * DA_HEAD_DIM
DA_WIDTH = DA_HEADS * DA_V_DIM
DA_QK_WIDTH = DA_HEADS * 2 * DA_HEAD_DIM
ROPE_THETA = 10000.0
ML_HEADS = 4
ML_HEAD_DIM = 64
ML_WIDTH = ML_HEADS * ML_HEAD_DIM
ML_CONV = 4
S5_GROUP = 16
S5_STATE = 64
S5_WIDTH = D_MODEL - DA_WIDTH - ML_WIDTH
S5_GROUPS = S5_WIDTH // S5_GROUP
S5_NSTATE = S5_GROUPS * S5_STATE
N_EXPERTS = 32
TOP_K = 4
D_EXPERT = D_MODEL
SWIGLU_LIMIT = 7.0
SWIGLU_ALPHA = 1.702
DN_ALPHA = (2 * DEPTH) ** 0.25
LN_EPS = 1e-5
NEG = -1e30

OFF_DA_K = DA_QK_WIDTH
OFF_DA_V = 2 * DA_QK_WIDTH
OFF_ML_X = OFF_DA_V + DA_WIDTH
OFF_ML_V = OFF_ML_X + ML_WIDTH
OFF_ML_O = OFF_ML_V + ML_WIDTH
OFF_ML_I = OFF_ML_O + ML_WIDTH
OFF_ML_F = OFF_ML_I + ML_HEADS
OFF_S5_U = OFF_ML_F + ML_HEADS
N_IN = OFF_S5_U + S5_WIDTH

LANES = 128
SUBLANES = 8
VMEM_LIMIT_BYTES = 56 * 1024 * 1024

TM_PROJ = 1024
TQ = 512
ML_CHUNK = 256
ML_NB = 2
S5_TC = 256
S5_UNROLL = 8
TB_RANK = 512
TM_MOE = 256
TM_DISP = 256
MOE_PARTS = 2
SC_CHUNK = 64
GATE_PAD = 8
VT_ROWS = DA_V_DIM + 16
Q_PRESCALE = DA_HEAD_DIM ** -0.5 * math.log2(math.e)


def _cparams(sem, vmem=VMEM_LIMIT_BYTES):
    return pltpu.CompilerParams(dimension_semantics=sem, vmem_limit_bytes=vmem)


def _sigmoid(x):
    return 1.0 / (1.0 + jnp.exp(-x))


def _mod_kernel(c_ref, w_ref, b_ref, o_ref):
    c = c_ref[...]
    ca = (c * _sigmoid(c)).astype(BF16)
    w = w_ref[0, 0].astype(BF16)
    o_ref[0] = jnp.dot(ca, w, preferred_element_type=F32) + b_ref[0]


def _modulation(c, ada_w, ada_b):
    nsub = ada_w.shape[1]
    nmod = ada_w.shape[0] * nsub
    bsz, d = c.shape
    e = ada_w.shape[-1]
    tn = 1024
    b = ada_b.reshape(nmod, 1, e)
    return pl.pallas_call(
        _mod_kernel,
        grid=(nmod, e // tn),
        in_specs=[pl.BlockSpec((bsz, d), lambda n, j: (0, 0)),
                  pl.BlockSpec((1, 1, d, tn), lambda n, j: (n // nsub, n % nsub, 0, j)),
                  pl.BlockSpec((1, 1, tn), lambda n, j: (n, 0, j))],
        out_specs=pl.BlockSpec((1, bsz, tn), lambda n, j: (n, 0, j)),
        out_shape=jax.ShapeDtypeStruct((nmod, bsz, e), F32),
        compiler_params=_cparams(("arbitrary", "arbitrary")),
        name="modulation",
    )(c, ada_w, b)


def _rope_kernel(pos_ref, cos_ref, sin_ref):
    nfreq = DA_HEAD_DIM // 2
    pos = pos_ref[0].astype(F32)
    fidx = lax.broadcasted_iota(jnp.int32, (nfreq, 1), 0).astype(F32)
    inv = jnp.exp(fidx * (-2.0 * math.log(ROPE_THETA) / DA_HEAD_DIM))
    ang = inv * pos
    reps = LANES // nfreq
    cos_t = jnp.concatenate([jnp.cos(ang)] * reps, axis=0).T
    sin_t = jnp.concatenate([jnp.sin(ang)] * reps, axis=0).T
    lane = lax.broadcasted_iota(jnp.int32, (1, LANES), 1)
    sign = jnp.where((lane % DA_HEAD_DIM) < nfreq, -1.0, 1.0)
    cos_ref[0] = cos_t
    sin_ref[0] = sin_t * sign


def _rope_tables(positions):
    bsz, s = positions.shape
    ts = 512
    pos3 = positions.reshape(bsz, 1, s)
    return pl.pallas_call(
        _rope_kernel,
        grid=(bsz, s // ts),
        in_specs=[pl.BlockSpec((1, 1, ts), lambda b, i: (b, 0, i))],
        out_specs=[pl.BlockSpec((1, ts, LANES), lambda b, i: (b, i, 0))] * 2,
        out_shape=[jax.ShapeDtypeStruct((bsz, s, LANES), F32)] * 2,
        compiler_params=_cparams(("arbitrary", "arbitrary")),
        name="rope_tables",
    )(pos3)


def _in_proj_kernel(x_ref, shift_ref, scale_ref, cos_ref, sin_ref, wqk_ref, wvt_ref, wrest_ref, wgt_ref, wgc_ref,
                    q_ref, k_ref, vt_ref, mlx_ref, mlv_ref, mlo_ref, gt_ref, gc_ref, s5u_ref):
    h = (x_ref[0] * (1.0 + scale_ref[0]) + shift_ref[0]).astype(BF16)
    cos = cos_ref[0]
    sin = sin_ref[0]
    lane = lax.broadcasted_iota(jnp.int32, (1, LANES), 1)
    lo_half = (lane % DA_HEAD_DIM) < DA_HEAD_DIM // 2
    half = DA_HEAD_DIM // 2

    def rope(t):
        fwd = pltpu.roll(t, half, 1)
        bwd = pltpu.roll(t, LANES - half, 1)
        partner = jnp.where(lo_half, bwd, fwd)
        return t * cos + partner * sin

    qk = jnp.dot(h, wqk_ref[...], preferred_element_type=F32)
    nslab = DA_QK_WIDTH // LANES
    for c in range(nslab):
        q_ref[0, :, c * LANES:(c + 1) * LANES] = (
            rope(qk[:, c * LANES:(c + 1) * LANES]) * Q_PRESCALE).astype(BF16)
        k_ref[0, :, c * LANES:(c + 1) * LANES] = rope(
            qk[:, DA_QK_WIDTH + c * LANES:DA_QK_WIDTH + (c + 1) * LANES]).astype(BF16)

    vt = lax.dot_general(wvt_ref[...], h, (((1,), (1,)), ((), ())), preferred_element_type=F32)
    tm = h.shape[0]
    for hh in range(DA_HEADS):
        for jj in range(tm // TQ):
            vt_ref[0, hh, jj, 0:DA_V_DIM, :] = vt[hh * DA_V_DIM:(hh + 1) * DA_V_DIM,
                                                  jj * TQ:(jj + 1) * TQ].astype(BF16)
            vt_ref[0, hh, jj, DA_V_DIM:VT_ROWS, :] = jnp.ones((VT_ROWS - DA_V_DIM, TQ), BF16)

    r = jnp.dot(h, wrest_ref[...], preferred_element_type=F32)
    o = 0
    mlx_ref[0] = r[:, o:o + ML_WIDTH].astype(BF16); o += ML_WIDTH
    mlv_ref[0] = r[:, o:o + ML_WIDTH].astype(BF16); o += ML_WIDTH
    mlo_ref[0] = r[:, o:o + ML_WIDTH].astype(BF16); o += ML_WIDTH
    s5u_ref[...] = r[:, o:o + S5_WIDTH]
    gt_ref[0] = lax.dot_general(wgt_ref[...], h, (((1,), (1,)), ((), ())), preferred_element_type=F32)
    gc_ref[0] = jnp.dot(h, wgc_ref[...], preferred_element_type=F32)


def _in_proj(x, shift, scale, cos_t, sin_t, w_in_l):
    bsz, s, d = x.shape
    tm = TM_PROJ
    w = w_in_l.astype(BF16)
    wqk = w[:, :OFF_DA_V]
    wvt = w[:, OFF_DA_V:OFF_ML_X].T
    wrest = jnp.concatenate([w[:, OFF_ML_X:OFF_ML_I], w[:, OFF_S5_U:]], axis=1)
    wg = w[:, OFF_ML_I:OFF_S5_U]
    wgt = wg.T
    wgc = jnp.pad(wg, ((0, 0), (0, LANES - GATE_PAD)))
    nrest = wrest.shape[1]
    shift3 = shift.reshape(bsz, 1, d)
    scale3 = scale.reshape(bsz, 1, d)
    tok = lambda b, i: (b, i, 0)
    per_b = lambda b, i: (b, 0, 0)
    const2 = lambda b, i: (0, 0)
    out_shapes = [
        jax.ShapeDtypeStruct((bsz, s, DA_QK_WIDTH), BF16),
        jax.ShapeDtypeStruct((bsz, s, DA_QK_WIDTH), BF16),
        jax.ShapeDtypeStruct((bsz, DA_HEADS, s // TQ, VT_ROWS, TQ), BF16),
        jax.ShapeDtypeStruct((bsz, s, ML_WIDTH), BF16),
        jax.ShapeDtypeStruct((bsz, s, ML_WIDTH), BF16),
        jax.ShapeDtypeStruct((bsz, s, ML_WIDTH), BF16),
        jax.ShapeDtypeStruct((bsz, GATE_PAD, s), F32),
        jax.ShapeDtypeStruct((bsz, s, LANES), F32),
        jax.ShapeDtypeStruct((s, bsz * S5_WIDTH), F32),
    ]
    out_specs = [
        pl.BlockSpec((1, tm, DA_QK_WIDTH), tok),
        pl.BlockSpec((1, tm, DA_QK_WIDTH), tok),
        pl.BlockSpec((1, DA_HEADS, tm // TQ, VT_ROWS, TQ), lambda b, i: (b, 0, i, 0, 0)),
        pl.BlockSpec((1, tm, ML_WIDTH), tok),
        pl.BlockSpec((1, tm, ML_WIDTH), tok),
        pl.BlockSpec((1, tm, ML_WIDTH), tok),
        pl.BlockSpec((1, GATE_PAD, tm), lambda b, i: (b, 0, i)),
        pl.BlockSpec((1, tm, LANES), tok),
        pl.BlockSpec((tm, S5_WIDTH), lambda b, i: (i, b)),
    ]
    return pl.pallas_call(
        _in_proj_kernel,
        grid=(bsz, s // tm),
        in_specs=[pl.BlockSpec((1, tm, d), tok),
                  pl.BlockSpec((1, 1, d), per_b),
                  pl.BlockSpec((1, 1, d), per_b),
                  pl.BlockSpec((1, tm, LANES), tok),
                  pl.BlockSpec((1, tm, LANES), tok),
                  pl.BlockSpec((d, OFF_DA_V), const2),
                  pl.BlockSpec((DA_WIDTH, d), const2),
                  pl.BlockSpec((d, nrest), const2),
                  pl.BlockSpec((GATE_PAD, d), const2),
                  pl.BlockSpec((d, LANES), const2)],
        out_specs=out_specs,
        out_shape=out_shapes,
        compiler_params=_cparams(("arbitrary", "arbitrary")),
        name="in_proj",
    )(x, shift3, scale3, cos_t, sin_t, wqk, wvt, wrest, wgt, wgc)


def _diff_attn_kernel(lam_init, lamv_ref, gain_ref, q_ref, k_ref, vt_ref, o_ref, acc_s, m_s):
    qi = pl.program_id(2)
    tq = q_ref.shape[1]
    lane = lax.broadcasted_iota(jnp.int32, (1, LANES), 1)
    first = lane < DA_HEAD_DIM
    q = q_ref[0]
    zero = jnp.zeros_like(q)
    qm = (jnp.where(first, q, zero), jnp.where(first, zero, q))
    acc_s[...] = jnp.zeros_like(acc_s)
    m_s[...] = jnp.full(m_s.shape, NEG, F32)

    def step(j, nblk, masked):
        tk = nblk * tq
        kb = k_ref[0, pl.ds(pl.multiple_of(j * tq, tq), tk), :]
        vtb = vt_ref[0, 0, j] if nblk == 1 else jnp.concatenate([vt_ref[0, 0, j + b] for b in range(nblk)], axis=1)
        for c in range(2):
            st = lax.dot_general(kb, qm[c], (((1,), (1,)), ((), ())), preferred_element_type=F32)
            if masked:
                key_i = lax.broadcasted_iota(jnp.int32, (tk, tq), 0) - (tk - tq)
                qry_i = lax.broadcasted_iota(jnp.int32, (tk, tq), 1)
                st = jnp.where(key_i <= qry_i, st, NEG)
            m_prev = m_s[c]
            m_new = jnp.maximum(m_prev, jnp.max(st, axis=0, keepdims=True))
            alpha = jnp.exp2(m_prev - m_new)
            p = jnp.exp2(st - m_new).astype(BF16)
            acc_s[c] = alpha * acc_s[c] + jnp.dot(vtb, p, preferred_element_type=F32)
            m_s[c] = m_new

    def body(jj, carry):
        step(2 * jj, 2, False)
        return carry

    lax.fori_loop(0, qi // 2, body, 0)

    @pl.when(qi % 2 == 1)
    def _():
        step(qi - 1, 2, True)

    @pl.when(qi % 2 == 0)
    def _():
        step(qi, 1, True)

    outs = []
    for c in range(2):
        acc = acc_s[c]
        outs.append(acc[:DA_V_DIM] / acc[DA_V_DIM:DA_V_DIM + 1])

    lamv = lamv_ref[...]
    lam = (jnp.exp(jnp.sum(lamv[0:1] * lamv[1:2], axis=1, keepdims=True))
           - jnp.exp(jnp.sum(lamv[2:3] * lamv[3:4], axis=1, keepdims=True)) + lam_init)
    ot = outs[0] - lam * outs[1]
    ms = jnp.mean(ot * ot, axis=0, keepdims=True)
    ot = ot * (lax.rsqrt(ms + LN_EPS) * (1.0 - lam_init))
    o_ref[0] = (ot.T * gain_ref[...]).astype(o_ref.dtype)


def _diff_attn(q, k, vt, lamv, gain, lam_init):
    bsz, s, _ = q.shape
    tq = TQ
    nq = s // tq
    return pl.pallas_call(
        functools.partial(_diff_attn_kernel, lam_init),
        grid=(bsz, DA_HEADS, nq),
        in_specs=[pl.BlockSpec((4, DA_HEAD_DIM), lambda b, h, i: (0, 0)),
                  pl.BlockSpec((1, DA_V_DIM), lambda b, h, i: (0, 0)),
                  pl.BlockSpec((1, tq, DA_V_DIM), lambda b, h, i: (b, i, h)),
                  pl.BlockSpec((1, s, DA_V_DIM), lambda b, h, i: (b, 0, h)),
                  pl.BlockSpec((1, 1, nq, VT_ROWS, tq), lambda b, h, i: (b, h, 0, 0, 0))],
        out_specs=pl.BlockSpec((1, tq, DA_V_DIM), lambda b, h, i: (b, i, h)),
        out_shape=jax.ShapeDtypeStruct((bsz, s, DA_WIDTH), BF16),
        scratch_shapes=[pltpu.VMEM((2, VT_ROWS, tq), F32), pltpu.VMEM((2, 1, tq), F32)],
        compiler_params=_cparams(("arbitrary", "arbitrary", "arbitrary")),
        name="diff_attn",
    )(lamv, gain.reshape(1, DA_V_DIM), q, k, vt)


def _log_sigmoid(x):
    return jnp.minimum(x, 0.0) - jnp.log(1.0 + jnp.exp(-jnp.abs(x)))


def _split3(a):
    hi = a.astype(BF16)
    r1 = a - hi.astype(F32)
    mid = r1.astype(BF16)
    lo = (r1 - mid.astype(F32)).astype(BF16)
    return hi, mid, lo


def _mlstm_kernel(x_ref, v_ref, o_ref, gt_ref, gc_ref, cw_ref, cb_ref, wq_ref, wkt_ref, gbt_ref, gbc_ref,
                  ng_ref, hmean_ref, y_ref, xc_s, c_s, m_s):
    nb, s = x_ref.shape[0], x_ref.shape[1]
    L = ML_CHUNK
    H, dh = ML_HEADS, ML_HEAD_DIM
    nc = s // L
    cw = cw_ref[...]
    row = lax.broadcasted_iota(jnp.int32, (s, 1), 0)
    for bi in range(nb):
        x = x_ref[bi].astype(F32)
        xc = x * cw[ML_CONV - 1:ML_CONV]
        for j in range(1, ML_CONV):
            xs = jnp.where(row >= j, pltpu.roll(x, j, 0), 0.0)
            xc = xc + xs * cw[ML_CONV - 1 - j:ML_CONV - j]
        xc = xc + cb_ref[...]
        xc_s[bi] = (xc * _sigmoid(xc)).astype(BF16)

    c_s[...] = jnp.zeros_like(c_s)
    m_s[...] = jnp.full(m_s.shape, NEG, F32)

    ri = lax.broadcasted_iota(jnp.int32, (L, L), 0)
    ci = lax.broadcasted_iota(jnp.int32, (L, L), 1)
    causal = ci <= ri
    tril = causal.astype(BF16)
    triu = (ri <= ci).astype(BF16)
    lane = lax.broadcasted_iota(jnp.int32, (1, dh), 1)
    one_hot0 = jnp.broadcast_to((lane == 0).astype(BF16), (L, dh))

    def chunk_one(bi, ci_, t0):
        xcc = xc_s[bi, pl.ds(t0, L), :]
        qc = jnp.dot(xcc, wq_ref[...], preferred_element_type=F32).astype(BF16)
        ktc = lax.dot_general(wkt_ref[...], xcc, (((1,), (1,)), ((), ())),
                              preferred_element_type=F32)
        g_rows = gt_ref[bi, ci_] + gbt_ref[...]
        g_cols = gc_ref[bi, pl.ds(t0, L), :] + gbc_ref[...]
        lf_rows = _log_sigmoid(g_rows)
        lf_cols = _log_sigmoid(g_cols)
        r3 = jnp.dot(jnp.concatenate(_split3(lf_rows), axis=0), triu, preferred_element_type=F32)
        b_rows = r3[0:GATE_PAD] + r3[GATE_PAD:2 * GATE_PAD] + r3[2 * GATE_PAD:]
        c3 = jnp.dot(tril, jnp.concatenate(_split3(lf_cols), axis=1), preferred_element_type=F32)
        b_cols = c3[:, 0:LANES] + c3[:, LANES:2 * LANES] + c3[:, 2 * LANES:]
        vch = v_ref[bi, pl.ds(t0, L), :]
        och = o_ref[bi, pl.ds(t0, L), :].astype(F32)
        hs = []
        for h in range(H):
            br = b_rows[H + h:H + h + 1, :]
            ir = g_rows[h:h + 1, :]
            bc = b_cols[:, H + h:H + h + 1]
            m_prev = m_s[bi, h]
            log_d = jnp.where(causal, bc - br + ir, NEG)
            inter = bc + m_prev
            mx = jnp.maximum(inter, jnp.max(log_d, axis=1, keepdims=True))
            dmat = jnp.exp(log_d - mx)
            dec = jnp.exp(inter - mx)
            qh = qc[:, h * dh:(h + 1) * dh]
            kth = ktc[h * dh:(h + 1) * dh, :]
            vaug = jnp.concatenate([vch[:, h * dh:(h + 1) * dh], one_hot0], axis=1)
            sm = (jnp.dot(qh, kth.astype(BF16), preferred_element_type=F32) * dmat).astype(BF16)
            c_prev = c_s[bi, h]
            na = (jnp.dot(sm, vaug, preferred_element_type=F32)
                  + dec * jnp.dot(qh, c_prev.astype(BF16), preferred_element_type=F32))
            den = na[:, dh:dh + 1]
            hs.append(na[:, :dh] / jnp.maximum(jnp.abs(den), jnp.exp(-mx)))
            g_tot = br[:, L - 1:L]
            a_row = g_tot - br + ir
            m_new = jnp.maximum(g_tot + m_prev, jnp.max(a_row, axis=1, keepdims=True))
            decay = jnp.exp(g_tot + m_prev - m_new)
            w_row = jnp.exp(a_row - m_new)
            kw = (kth * w_row).astype(BF16)
            c_s[bi, h] = decay * c_prev + jnp.dot(kw, vaug, preferred_element_type=F32)
            m_s[bi, h] = m_new
        hcat = jnp.concatenate(hs, axis=1)
        m3 = jnp.dot(jnp.concatenate(_split3(hcat * hcat), axis=0), hmean_ref[...],
                     preferred_element_type=F32)
        ms = m3[0:L] + m3[L:2 * L] + m3[2 * L:]
        y = hcat * lax.rsqrt(ms + LN_EPS) * ng_ref[...] * _sigmoid(och)
        y_ref[bi, pl.ds(t0, L), :] = y.astype(y_ref.dtype)

    def chunk(ci_, _):
        t0 = pl.multiple_of(ci_ * L, L)
        for bi in range(nb):
            chunk_one(bi, ci_, t0)
        return 0

    lax.fori_loop(0, nc, chunk, 0)


def _mlstm(mlx, mlv, mlo, g_t, g_c, conv_w, conv_b, w_q, w_k, gate_b, norm_g):
    bsz, s, _ = mlx.shape
    H, dh = ML_HEADS, ML_HEAD_DIM
    eye = jnp.eye(H, dtype=F32)
    wq_bd = jnp.einsum('hde,hg->hdge', w_q, eye).reshape(ML_WIDTH, ML_WIDTH).astype(BF16)
    wk_bd = jnp.einsum('hde,hg->hdge', w_k * (dh ** -0.5), eye).reshape(ML_WIDTH, ML_WIDTH)
    wkt_bd = wk_bd.T.astype(BF16)
    gbt = gate_b.reshape(GATE_PAD, 1)
    gbc = jnp.pad(gate_b.reshape(1, GATE_PAD), ((0, 0), (0, LANES - GATE_PAD)))
    hmean = jnp.kron(eye, jnp.full((dh, dh), 1.0 / dh, F32)).astype(BF16)
    nc = s // ML_CHUNK
    g_t4 = g_t.reshape(bsz, GATE_PAD, nc, ML_CHUNK).transpose(0, 2, 1, 3)
    tok = lambda b: (b, 0, 0)
    c2 = lambda b: (0, 0)
    nb = ML_NB
    return pl.pallas_call(
        _mlstm_kernel,
        grid=(bsz // nb,),
        in_specs=[pl.BlockSpec((nb, s, ML_WIDTH), tok),
                  pl.BlockSpec((nb, s, ML_WIDTH), tok),
                  pl.BlockSpec((nb, s, ML_WIDTH), tok),
                  pl.BlockSpec((nb, nc, GATE_PAD, ML_CHUNK), lambda b: (b, 0, 0, 0)),
                  pl.BlockSpec((nb, s, LANES), tok),
                  pl.BlockSpec((ML_CONV, ML_WIDTH), c2),
                  pl.BlockSpec((1, ML_WIDTH), c2),
                  pl.BlockSpec((ML_WIDTH, ML_WIDTH), c2),
                  pl.BlockSpec((ML_WIDTH, ML_WIDTH), c2),
                  pl.BlockSpec((GATE_PAD, 1), c2),
                  pl.BlockSpec((1, LANES), c2),
                  pl.BlockSpec((1, ML_WIDTH), c2),
                  pl.BlockSpec((ML_WIDTH, ML_WIDTH), c2)],
        out_specs=pl.BlockSpec((nb, s, ML_WIDTH), tok),
        out_shape=jax.ShapeDtypeStruct((bsz, s, ML_WIDTH), BF16),
        scratch_shapes=[pltpu.VMEM((nb, s, ML_WIDTH), BF16),
                        pltpu.VMEM((nb, H, dh, LANES), F32),
                        pltpu.VMEM((nb, H, 1, 1), F32)],
        compiler_params=_cparams(("arbitrary",)),
        name="mlstm",
    )(mlx, mlv, mlo, g_t4, g_c, conv_w, conv_b.reshape(1, ML_WIDTH), wq_bd, wkt_bd, gbt, gbc,
      norm_g.reshape(1, ML_WIDTH), hmean)


def _gelu_tanh(x):
    return 0.5 * x * (1.0 + jnp.tanh(math.sqrt(2.0 / math.pi) * (x + 0.044715 * (x * x * x))))


def _s5_kernel(u_ref, are_ref, aim_ref, bcat_ref, ccat_ref, d_ref, wglu_ref, y_ref, xs_s, st_s):
    tc, bsz, w = u_ref.shape
    n = S5_NSTATE

    @pl.when(pl.program_id(0) == 0)
    def _():
        st_s[...] = jnp.zeros_like(st_s)

    u = u_ref[...].reshape(tc * bsz, w)
    xs_s[...] = jnp.dot(u.astype(BF16), bcat_ref[...], preferred_element_type=F32).reshape(tc, bsz, 2 * n)
    a_re = jnp.broadcast_to(are_ref[...], (bsz, n))
    a_im = jnp.broadcast_to(aim_ref[...], (bsz, n))

    def step(t, carry):
        x_re, x_im = carry
        bu = xs_s[t]
        n_re = a_re * x_re - a_im * x_im + bu[:, :n]
        n_im = a_re * x_im + a_im * x_re + bu[:, n:]
        xs_s[t] = jnp.concatenate([n_re, n_im], axis=1)
        return n_re, n_im

    x_re, x_im = lax.fori_loop(0, tc, step, (st_s[0], st_s[1]), unroll=S5_UNROLL)
    st_s[0] = x_re
    st_s[1] = x_im

    xs = xs_s[...].reshape(tc * bsz, 2 * n).astype(BF16)
    y = jnp.dot(xs, ccat_ref[...], preferred_element_type=F32) + d_ref[...] * u
    z = jnp.dot(_gelu_tanh(y).astype(BF16), wglu_ref[...], preferred_element_type=F32)
    out = z[:, :w] * _sigmoid(z[:, w:])
    y_ref[...] = out.reshape(tc, bsz, w).astype(y_ref.dtype)


def _s5_params(a_re, a_im, log_dt, b_re, b_im, c_re, c_im, w_glu):
    G, P, Hc = S5_GROUPS, S5_STATE, S5_GROUP
    dt = jnp.exp(log_dt)[:, None]
    mag = jnp.exp(a_re * dt)
    ab_re = mag * jnp.cos(a_im * dt)
    ab_im = mag * jnp.sin(a_im * dt)
    nr, ni = ab_re - 1.0, ab_im
    den = a_re * a_re + a_im * a_im
    fr = (nr * a_re + ni * a_im) / den
    fi = (ni * a_re - nr * a_im) / den
    bb_re = fr[..., None] * b_re - fi[..., None] * b_im
    bb_im = fr[..., None] * b_im + fi[..., None] * b_re
    eye = jnp.eye(G, dtype=F32)
    bd = lambda t, sub: jnp.einsum(sub, t, eye)
    bre = bd(bb_re, 'gph,gk->ghkp').reshape(G * Hc, G * P)
    bim = bd(bb_im, 'gph,gk->ghkp').reshape(G * Hc, G * P)
    bcat = jnp.concatenate([bre, bim], axis=1).astype(BF16)
    cre = bd(c_re, 'ghp,gk->gpkh').reshape(G * P, G * Hc)
    cim = bd(c_im, 'ghp,gk->gpkh').reshape(G * P, G * Hc)
    ccat = jnp.concatenate([cre, -cim], axis=0).astype(BF16)
    wv = bd(w_glu[:, :, :Hc], 'ghj,gk->ghkj').reshape(G * Hc, G * Hc)
    wg = bd(w_glu[:, :, Hc:], 'ghj,gk->ghkj').reshape(G * Hc, G * Hc)
    wglu = jnp.concatenate([wv, wg], axis=1).astype(BF16)
    return ab_re.reshape(1, G * P), ab_im.reshape(1, G * P), bcat, ccat, wglu


def _s5(u_tm, bsz, params, d_skip):
    s = u_tm.shape[0]
    w = S5_WIDTH
    n = S5_NSTATE
    are, aim, bcat, ccat, wglu = params
    u3 = u_tm.reshape(s, bsz, w)
    tc = S5_TC
    c2 = lambda i: (0, 0)
    y = pl.pallas_call(
        _s5_kernel,
        grid=(s // tc,),
        in_specs=[pl.BlockSpec((tc, bsz, w), lambda i: (i, 0, 0)),
                  pl.BlockSpec((1, n), c2),
                  pl.BlockSpec((1, n), c2),
                  pl.BlockSpec((w, 2 * n), c2),
                  pl.BlockSpec((2 * n, w), c2),
                  pl.BlockSpec((1, w), c2),
                  pl.BlockSpec((w, 2 * w), c2)],
        out_specs=pl.BlockSpec((tc, bsz, w), lambda i: (i, 0, 0)),
        out_shape=jax.ShapeDtypeStruct((s, bsz, w), F32),
        scratch_shapes=[pltpu.VMEM((tc, bsz, 2 * n), F32),
                        pltpu.VMEM((2, bsz, n), F32)],
        compiler_params=_cparams(("arbitrary",)),
        name="s5",
    )(u3, are, aim, bcat, ccat, d_skip.reshape(1, w), wglu)
    return y.reshape(s, bsz * w)


def _layer_norm(z, g, b):
    mu = jnp.mean(z, axis=1, keepdims=True)
    zc = z - mu
    var = jnp.mean(zc * zc, axis=1, keepdims=True)
    return zc * lax.rsqrt(var + LN_EPS) * g + b


def _out_proj_kernel(yda_ref, yml_ref, ys5_ref, x_ref, gate_ref, lng_ref, lnb_ref, shift_ref, scale_ref,
                     wout_ref, wrt_ref, brt_ref, x1_ref, h2_ref, eid_ref, prob_ref, cnt_ref):
    y = jnp.dot(yda_ref[0], wout_ref[0:DA_WIDTH, :], preferred_element_type=F32)
    y = y + jnp.dot(yml_ref[0], wout_ref[DA_WIDTH:DA_WIDTH + ML_WIDTH, :], preferred_element_type=F32)
    y = y + jnp.dot(ys5_ref[...].astype(BF16), wout_ref[DA_WIDTH + ML_WIDTH:, :], preferred_element_type=F32)
    x1 = _layer_norm(DN_ALPHA * x_ref[0] + (1.0 + gate_ref[0]) * y, lng_ref[...], lnb_ref[...])
    x1_ref[0] = x1
    h2 = x1 * (1.0 + scale_ref[0]) + shift_ref[0]
    h2_ref[...] = _pack_bf16_pairs(h2)
    h_hi = h2.astype(BF16)
    h_lo = (h2 - h_hi.astype(F32)).astype(BF16)
    nt_dot = lambda a, b: lax.dot_general(a, b, (((1,), (1,)), ((), ())), preferred_element_type=F32)
    by_hi = nt_dot(wrt_ref[...], h_hi)
    logits = (by_hi[:N_EXPERTS] + by_hi[N_EXPERTS:] + nt_dot(wrt_ref[0:N_EXPERTS, :], h_lo)
              + brt_ref[...])
    eidx = lax.broadcasted_iota(jnp.int32, logits.shape, 0)
    vals, ids = [], []
    for _ in range(TOP_K):
        mx = jnp.max(logits, axis=0, keepdims=True)
        sel = jnp.min(jnp.where(logits == mx, eidx, N_EXPERTS), axis=0, keepdims=True)
        vals.append(mx)
        ids.append(sel)
        logits = jnp.where(eidx == sel, -jnp.inf, logits)
    ex = [jnp.exp(v - vals[0]) for v in vals]
    tot = ex[0] + ex[1] + ex[2] + ex[3]
    zi = jnp.zeros_like(ids[0])
    eid_ref[...] = jnp.concatenate(ids + [zi] * (SUBLANES - TOP_K), axis=0)

    @pl.when(jnp.logical_and(pl.program_id(0) == 0, pl.program_id(1) == 0))
    def _():
        cnt_ref[...] = jnp.zeros_like(cnt_ref)

    member = jnp.zeros(logits.shape, F32)
    for sel in ids:
        member = member + (eidx == sel).astype(F32)
    cnt_ref[...] = cnt_ref[...] + jnp.sum(member, axis=1, keepdims=True)
    zf = jnp.zeros((LANES - TOP_K, tot.shape[1]), F32)
    prob_ref[...] = jnp.concatenate([e / tot for e in ex] + [zf], axis=0).T


def _out_proj(y_da, y_ml, y_s5, x, gate, ln_g, ln_b, shift2, scale2, w_out_l, w_router_l, b_router_l):
    bsz, s, d = x.shape
    tm = TM_PROJ
    nt = s // tm
    tok = lambda b, i: (b, i, 0)
    per_b = lambda b, i: (b, 0, 0)
    c2 = lambda b, i: (0, 0)
    r3 = lambda a: a.reshape(bsz, 1, d)
    flat = lambda b, i: (0, b * nt + i)
    wr_t = w_router_l.T
    wr_hi = wr_t.astype(BF16)
    wr_lo = (wr_t - wr_hi.astype(F32)).astype(BF16)
    return pl.pallas_call(
        _out_proj_kernel,
        grid=(bsz, nt),
        in_specs=[pl.BlockSpec((1, tm, DA_WIDTH), tok),
                  pl.BlockSpec((1, tm, ML_WIDTH), tok),
                  pl.BlockSpec((tm, S5_WIDTH), lambda b, i: (i, b)),
                  pl.BlockSpec((1, tm, d), tok),
                  pl.BlockSpec((1, 1, d), per_b),
                  pl.BlockSpec((1, d), c2),
                  pl.BlockSpec((1, d), c2),
                  pl.BlockSpec((1, 1, d), per_b),
                  pl.BlockSpec((1, 1, d), per_b),
                  pl.BlockSpec((d, d), c2),
                  pl.BlockSpec((2 * N_EXPERTS, d), c2),
                  pl.BlockSpec((N_EXPERTS, 1), c2)],
        out_specs=[pl.BlockSpec((1, tm, d), tok),
                   pl.BlockSpec((tm, d // 2), lambda b, i: (b * nt + i, 0)),
                   pl.BlockSpec((SUBLANES, tm), flat),
                   pl.BlockSpec((tm, LANES), lambda b, i: (b * nt + i, 0)),
                   pl.BlockSpec((N_EXPERTS, LANES), c2)],
        out_shape=[jax.ShapeDtypeStruct((bsz, s, d), F32),
                   jax.ShapeDtypeStruct((bsz * s, d // 2), jnp.int32),
                   jax.ShapeDtypeStruct((SUBLANES, bsz * s), jnp.int32),
                   jax.ShapeDtypeStruct((bsz * s, LANES), F32),
                   jax.ShapeDtypeStruct((N_EXPERTS, LANES), F32)],
        compiler_params=_cparams(("arbitrary", "arbitrary")),
        name="out_proj",
    )(y_da, y_ml, y_s5, x, r3(gate), ln_g.reshape(1, d), ln_b.reshape(1, d), r3(shift2), r3(scale2),
      w_out_l.astype(BF16), jnp.concatenate([wr_hi, wr_lo], axis=0), b_router_l.reshape(N_EXPERTS, 1))


META_END, META_PAD, META_CNT = 0, 1, 2


def _route_kernel(eid_ref, cnt_ref, pos_ref, meta_ref, carry_s, start_s):
    i = pl.program_id(0)
    tb = eid_ref.shape[1]
    ntp = meta_ref.shape[1]
    tm = TM_MOE

    @pl.when(i == 0)
    def _():
        cnt = cnt_ref[...]
        padded = jnp.floor((cnt + (tm - 1)) * (1.0 / tm)) * tm
        er = lax.broadcasted_iota(jnp.int32, (N_EXPERTS, N_EXPERTS), 0)
        ec = lax.broadcasted_iota(jnp.int32, (N_EXPERTS, N_EXPERTS), 1)
        ends = jnp.dot((ec <= er).astype(F32), padded, preferred_element_type=F32, precision=HIGHEST)
        start_s[...] = ends - padded
        carry_s[...] = jnp.zeros_like(carry_s)
        lane = lax.broadcasted_iota(jnp.int32, (N_EXPERTS, ntp), 1)
        sub = lax.broadcasted_iota(jnp.int32, (N_EXPERTS, ntp), 0)
        diag = lane == sub

        def as_row(col):
            return jnp.sum(jnp.where(diag, col, 0.0), axis=0, keepdims=True)

        zero = jnp.zeros((SUBLANES - 3, ntp), F32)
        meta_ref[...] = jnp.concatenate([as_row(ends[:, 0:1]), as_row(padded[:, 0:1]), as_row(cnt[:, 0:1]), zero],
                                        axis=0).astype(jnp.int32)

    eid = eid_ref[...]
    eidx = lax.broadcasted_iota(jnp.int32, (N_EXPERTS, tb), 0)
    hot = [eidx == eid[k:k + 1, :] for k in range(TOP_K)]
    member = jnp.zeros((N_EXPERTS, tb), F32)
    for k in range(TOP_K):
        member = member + hot[k].astype(F32)
    ri = lax.broadcasted_iota(jnp.int32, (tb, tb), 0)
    ci = lax.broadcasted_iota(jnp.int32, (tb, tb), 1)
    triu = (ri <= ci).astype(BF16)
    incl = jnp.dot(member.astype(BF16), triu, preferred_element_type=F32)
    slot = incl - member + carry_s[:, 0:1] + start_s[:, 0:1]
    rows = [jnp.sum(jnp.where(hot[k], slot, 0.0), axis=0, keepdims=True) for k in range(TOP_K)]
    zr = jnp.zeros_like(rows[0])
    pos_ref[...] = jnp.concatenate(rows + [zr] * (SUBLANES - TOP_K), axis=0).astype(jnp.int32)
    carry_s[...] = carry_s[...] + jnp.sum(member, axis=1, keepdims=True)


def _route(eid, counts):
    t = eid.shape[1]
    tb = TB_RANK
    ntp = LANES
    pos8, meta = pl.pallas_call(
        _route_kernel,
        grid=(t // tb,),
        in_specs=[pl.BlockSpec((SUBLANES, tb), lambda i: (0, i)),
                  pl.BlockSpec((N_EXPERTS, LANES), lambda i: (0, 0))],
        out_specs=[pl.BlockSpec((SUBLANES, tb), lambda i: (0, i)),
                   pl.BlockSpec((SUBLANES, ntp), lambda i: (0, 0))],
        out_shape=[jax.ShapeDtypeStruct((SUBLANES, t), jnp.int32),
                   jax.ShapeDtypeStruct((SUBLANES, ntp), jnp.int32)],
        scratch_shapes=[pltpu.VMEM((N_EXPERTS, LANES), F32), pltpu.VMEM((N_EXPERTS, LANES), F32)],
        compiler_params=_cparams(("arbitrary",)),
        name="route",
    )(eid, counts)
    return pos8, meta


def _sc_workers():
    info = plsc.get_sparse_core_info()
    return info.num_cores, info.num_cores * info.num_subcores


def _dispatch(h2, pos8, n_rows):
    t, d = h2.shape
    n_cores, n_workers = _sc_workers()
    tpw = t // n_workers
    ch = SC_CHUNK
    mesh = plsc.VectorSubcoreMesh(core_axis_name="c", subcore_axis_name="s")

    @functools.partial(
        pl.kernel, mesh=mesh,
        out_type=jax.ShapeDtypeStruct((n_rows, d), h2.dtype),
        scratch_types=([pltpu.VMEM((ch,), jnp.int32)] * (2 * TOP_K) + [pltpu.VMEM((ch, d), h2.dtype)] * 2
                       + [pltpu.SemaphoreType.DMA]))
    def scatter_rows(h_hbm, pos_hbm, out_hbm, *scratch):
        idx = (scratch[0:TOP_K], scratch[TOP_K:2 * TOP_K])
        rows = scratch[2 * TOP_K:2 * TOP_K + 2]
        sem = scratch[2 * TOP_K + 2]
        base = (lax.axis_index("s") * n_cores + lax.axis_index("c")) * tpw

        def load(i, b):
            off = base + i * ch
            pltpu.sync_copy(h_hbm.at[pl.ds(off, ch)], rows[b])
            for k in range(TOP_K):
                pltpu.sync_copy(pos_hbm.at[k, pl.ds(off, ch)], idx[b][k])

        load(0, 0)
        for i in range(tpw // ch):
            b = i % 2
            copies = [pltpu.async_copy(rows[b], out_hbm.at[idx[b][k]], sem) for k in range(TOP_K)]
            if i + 1 < tpw // ch:
                load(i + 1, 1 - b)
            for cp in copies:
                cp.wait()

    return scatter_rows(h2, pos8)


def _gather_expert_rows(ys, pos8):
    _, d = ys.shape
    t = pos8.shape[1]
    n_cores, n_workers = _sc_workers()
    tpw = t // n_workers
    ch = SC_CHUNK
    mesh = plsc.VectorSubcoreMesh(core_axis_name="c", subcore_axis_name="s")

    @functools.partial(
        pl.kernel, mesh=mesh,
        out_type=jax.ShapeDtypeStruct((TOP_K, t, d), ys.dtype),
        scratch_types=([pltpu.VMEM((ch,), jnp.int32)] * 2 + [pltpu.VMEM((ch, d), ys.dtype)] * 2
                       + [pltpu.SemaphoreType.DMA] * 4))
    def gather_rows(ys_hbm, pos_hbm, out_hbm, idx0, idx1, rows0, rows1, g0, g1, w0, w1):
        idx, rows, gsem, wsem = (idx0, idx1), (rows0, rows1), (g0, g1), (w0, w1)
        base = (lax.axis_index("s") * n_cores + lax.axis_index("c")) * tpw
        items = [(i, k) for i in range(tpw // ch) for k in range(TOP_K)]

        def gather(n):
            i, k = items[n]
            pltpu.sync_copy(pos_hbm.at[k, pl.ds(base + i * ch, ch)], idx[n % 2])
            return pltpu.async_copy(ys_hbm.at[idx[n % 2]], rows[n % 2], gsem[n % 2])

        def write(n):
            i, k = items[n]
            return pltpu.async_copy(rows[n % 2], out_hbm.at[k, pl.ds(base + i * ch, ch)], wsem[n % 2])

        gathers, writes = {}, {}
        for n in range(len(items)):
            if n >= 2:
                writes[n - 2].wait()
            gathers[n] = gather(n)
            if n >= 1:
                gathers[n - 1].wait()
                writes[n - 1] = write(n - 1)
        last = len(items) - 1
        gathers[last].wait()
        writes[last] = write(last)
        if last >= 1:
            writes[last - 1].wait()
        writes[last].wait()

    return gather_rows(ys, pos8)


def _pack_bf16_pairs(a):
    n = a.shape[1] // 2
    lo = pltpu.bitcast(a[:, :n].astype(BF16).astype(F32), jnp.int32)
    hi = pltpu.bitcast(a[:, n:].astype(BF16).astype(F32), jnp.int32)
    return jnp.bitwise_or(jnp.bitwise_and(hi, -65536), jnp.bitwise_and(lax.shift_right_logical(lo, 16), 65535))


def _unpack_bf16_pairs(p):
    lo = pltpu.bitcast(lax.shift_left(p, 16), F32)
    hi = pltpu.bitcast(jnp.bitwise_and(p, -65536), F32)
    return lo, hi


def _expert_kernel(meta_ref, xs_ref, wup_ref, bup_ref, wdn_ref, bdn_ref, ys_ref,
                   wup_s, wdn_s, xbuf, obuf, in_sem, out_sem):
    e = pl.program_id(0)
    tm = TM_MOE
    half = D_MODEL // 2
    pad = meta_ref[META_PAD, e]
    n_t = pad // tm
    row0 = meta_ref[META_END, e] - pad
    cnt = meta_ref[META_CNT, e]

    def rows(i):
        return pl.ds(pl.multiple_of(row0 + i * tm, tm), tm)

    def x_copy(i, slot):
        return pltpu.make_async_copy(xs_ref.at[rows(i)], xbuf.at[slot], in_sem.at[slot])

    def y_copy(i, slot):
        return pltpu.make_async_copy(obuf.at[slot], ys_ref.at[rows(i)], out_sem.at[slot])

    @pl.when(n_t > 0)
    def _():
        x_copy(0, 0).start()
        wup_s[...] = wup_ref[0, 0].astype(BF16)
        wdn_s[...] = wdn_ref[0, 0].astype(BF16)

        def tile(i, carry):
            slot = lax.rem(i, 2)
            x_copy(i, slot).wait()

            @pl.when(i + 1 < n_t)
            def _():
                x_copy(i + 1, 1 - slot).start()

            @pl.when(i >= 2)
            def _():
                y_copy(i - 2, slot).wait()

            row = lax.broadcasted_iota(jnp.int32, (tm, 1), 0)
            lo, hi = _unpack_bf16_pairs(jnp.where(row < cnt - i * tm, xbuf[slot], 0))
            z = (jnp.dot(lo.astype(BF16), wup_s[0:half, :], preferred_element_type=F32)
                 + jnp.dot(hi.astype(BF16), wup_s[half:, :], preferred_element_type=F32) + bup_ref[0, 0])
            glu = jnp.minimum(z[:, :D_EXPERT], SWIGLU_LIMIT)
            lin = jnp.clip(z[:, D_EXPERT:], -SWIGLU_LIMIT, SWIGLU_LIMIT)
            act = (glu * _sigmoid(SWIGLU_ALPHA * glu) * (lin + 1.0)).astype(BF16)
            y = jnp.dot(act, wdn_s[...], preferred_element_type=F32) + bdn_ref[0, 0]
            obuf[slot] = _pack_bf16_pairs(y)
            y_copy(i, slot).start()
            return carry

        lax.fori_loop(0, n_t, tile, 0)

        @pl.when(n_t >= 2)
        def _():
            y_copy(n_t - 2, lax.rem(n_t, 2)).wait()

        y_copy(n_t - 1, lax.rem(n_t - 1, 2)).wait()

    @pl.when(e == N_EXPERTS - 1)
    def _():
        obuf[0] = jnp.zeros((tm, half), jnp.int32)

        def fill(i, carry):
            cp = pltpu.make_async_copy(obuf.at[0], ys_ref.at[pl.ds(pl.multiple_of(i * tm, tm), tm)], out_sem.at[0])
            cp.start()
            cp.wait()
            return carry

        lax.fori_loop(meta_ref[META_END, N_EXPERTS - 1] // tm, ys_ref.shape[0] // tm, fill, 0)


def _expert_mlp(xs, meta, layer, w_up, b_up, w_down, b_down):
    n_rows, half = xs.shape
    d = 2 * half
    tm = TM_MOE
    f = w_up.shape[-1]
    b_up4 = b_up.reshape(DEPTH, N_EXPERTS, 1, f)
    b_dn4 = b_down.reshape(DEPTH, N_EXPERTS, 1, d)
    wsel = lambda e, m: (layer, e, 0, 0)
    grid_spec = pltpu.PrefetchScalarGridSpec(
        num_scalar_prefetch=1,
        grid=(N_EXPERTS,),
        in_specs=[pl.BlockSpec(memory_space=pl.ANY),
                  pl.BlockSpec((1, 1, d, f), wsel),
                  pl.BlockSpec((1, 1, 1, f), wsel),
                  pl.BlockSpec((1, 1, f // 2, d), wsel),
                  pl.BlockSpec((1, 1, 1, d), wsel)],
        out_specs=pl.BlockSpec(memory_space=pl.ANY),
        scratch_shapes=[pltpu.VMEM((d, f), BF16), pltpu.VMEM((f // 2, d), BF16),
                        pltpu.VMEM((2, tm, half), jnp.int32), pltpu.VMEM((2, tm, half), jnp.int32),
                        pltpu.SemaphoreType.DMA((2,)), pltpu.SemaphoreType.DMA((2,))],
    )
    return pl.pallas_call(
        _expert_kernel,
        grid_spec=grid_spec,
        out_shape=jax.ShapeDtypeStruct((n_rows, half), jnp.int32),
        compiler_params=_cparams(("arbitrary",)),
        name="expert_mlp",
    )(meta, xs, w_up, b_up4, w_down, b_dn4)


def _combine_kernel(rows_ref, prob_ref, x_ref, gate_ref, lng_ref, lnb_ref, o_ref):
    p = prob_ref[...]
    y = None
    for k in range(TOP_K):
        yk = p[:, k:k + 1] * jnp.concatenate(_unpack_bf16_pairs(rows_ref[k]), axis=1)
        y = yk if y is None else y + yk
    o_ref[0] = _layer_norm(DN_ALPHA * x_ref[0] + (1.0 + gate_ref[0]) * y, lng_ref[...], lnb_ref[...])


def _combine(rows, prob_c, x1, gate, ln_g, ln_b, part):
    bsz, s, d = x1.shape
    tm = TM_DISP
    nt = s // tm
    pb = bsz // MOE_PARTS
    b0 = part * pb
    return pl.pallas_call(
        _combine_kernel,
        grid=(pb, nt),
        in_specs=[pl.BlockSpec((TOP_K, tm, d // 2), lambda b, i: (0, b * nt + i, 0)),
                  pl.BlockSpec((tm, LANES), lambda b, i: ((b0 + b) * nt + i, 0)),
                  pl.BlockSpec((1, tm, d), lambda b, i: (b0 + b, i, 0)),
                  pl.BlockSpec((1, 1, d), lambda b, i: (b0 + b, 0, 0)),
                  pl.BlockSpec((1, d), lambda b, i: (0, 0)),
                  pl.BlockSpec((1, d), lambda b, i: (0, 0))],
        out_specs=pl.BlockSpec((1, tm, d), lambda b, i: (b0 + b, i, 0)),
        out_shape=jax.ShapeDtypeStruct((bsz, s, d), F32),
        input_output_aliases={2: 0},
        compiler_params=_cparams(("arbitrary", "arbitrary")),
        name="combine",
    )(rows, prob_c, x1, gate.reshape(bsz, 1, d), ln_g.reshape(1, d), ln_b.reshape(1, d))


def kernel(x, c, positions, ada_w, ada_b, w_in, lam_q1, lam_k1, lam_q2, lam_k2, da_norm_g, ml_conv_w, ml_conv_b,
           ml_w_q, ml_w_k, ml_gate_b, ml_norm_g, s5_a_re, s5_a_im, s5_log_dt, s5_b_re, s5_b_im, s5_c_re, s5_c_im,
           s5_d, s5_w_glu, w_out, ln_g, ln_b, w_router, b_router, w_up, b_up, w_down, b_down):
    bsz, s, d = x.shape
    t = bsz * s
    n_tiles_max = (t * TOP_K) // TM_MOE + N_EXPERTS
    n_rows = n_tiles_max * TM_MOE
    mod = _modulation(c, ada_w, ada_b)
    cos_t, sin_t = _rope_tables(positions)
    for l in range(DEPTH):
        shift, scale, gate = jnp.split(mod[2 * l], 3, axis=-1)
        q, k, v, mlx, mlv, mlo, g_t, g_c, s5u = _in_proj(x, shift, scale, cos_t, sin_t, w_in[l])
        lam_init = 0.8 - 0.6 * math.exp(-0.3 * l)
        lamv = jnp.stack([lam_q1[l], lam_k1[l], lam_q2[l], lam_k2[l]])
        y_da = _diff_attn(q, k, v, lamv, da_norm_g[l], lam_init)
        y_ml = _mlstm(mlx, mlv, mlo, g_t, g_c, ml_conv_w[l], ml_conv_b[l], ml_w_q[l], ml_w_k[l],
                      ml_gate_b[l], ml_norm_g[l])
        s5p = _s5_params(s5_a_re[l], s5_a_im[l], s5_log_dt[l], s5_b_re[l], s5_b_im[l], s5_c_re[l], s5_c_im[l],
                         s5_w_glu[l])
        y_s5 = _s5(s5u, bsz, s5p, s5_d[l])
        shift2, scale2, gate2 = jnp.split(mod[2 * l + 1], 3, axis=-1)
        x1, h2, eid, prob, counts = _out_proj(y_da, y_ml, y_s5, x, gate, ln_g[l, 0], ln_b[l, 0], shift2, scale2,
                                      w_out[l], w_router[l], b_router[l])
        pos8, meta = _route(eid, counts)
        xs = _dispatch(h2, pos8, n_rows)
        ys = _expert_mlp(xs, meta, l, w_up, b_up, w_down, b_down)
        tp = t // MOE_PARTS
        x = x1
        for part in range(MOE_PARTS):
            rows = _gather_expert_rows(ys, pos8[:, part * tp:(part + 1) * tp])
            x = _combine(rows, prob, x, gate2, ln_g[l, 1], ln_b[l, 1], part)
    return x
```

```python
import functools
import math

import jax
import jax.numpy as jnp
from jax import lax
from jax.experimental import pallas as pl
from jax.experimental.pallas import tpu as pltpu
from jax.experimental.pallas import tpu_sc as plsc

F32 = jnp.float32
BF16 = jnp.bfloat16
HIGHEST = lax.Precision.HIGHEST

D_MODEL = 1024
DEPTH = 2
DA_HEADS = 4
DA_HEAD_DIM = 64
DA_V_DIM = 2 * DA_HEAD_DIM
DA_WIDTH = DA_HEADS * DA_V_DIM
DA_QK_WIDTH = DA_HEADS * 2 * DA_HEAD_DIM
ROPE_THETA = 10000.0
ML_HEADS = 4
ML_HEAD_DIM = 64
ML_WIDTH = ML_HEADS * ML_HEAD_DIM
ML_CONV = 4
S5_GROUP = 16
S5_STATE = 64
S5_WIDTH = D_MODEL - DA_WIDTH - ML_WIDTH
S5_GROUPS = S5_WIDTH // S5_GROUP
S5_NSTATE = S5_GROUPS * S5_STATE
N_EXPERTS = 32
TOP_K = 4
D_EXPERT = D_MODEL
SWIGLU_LIMIT = 7.0
SWIGLU_ALPHA = 1.702
DN_ALPHA = (2 * DEPTH) ** 0.25
LN_EPS = 1e-5
NEG = -1e30

OFF_DA_K = DA_QK_WIDTH
OFF_DA_V = 2 * DA_QK_WIDTH
OFF_ML_X = OFF_DA_V + DA_WIDTH
OFF_ML_V = OFF_ML_X + ML_WIDTH
OFF_ML_O = OFF_ML_V + ML_WIDTH
OFF_ML_I = OFF_ML_O + ML_WIDTH
OFF_ML_F = OFF_ML_I + ML_HEADS
OFF_S5_U = OFF_ML_F + ML_HEADS
N_IN = OFF_S5_U + S5_WIDTH

LANES = 128
SUBLANES = 8
VMEM_LIMIT_BYTES = 56 * 1024 * 1024

TM_PROJ = 1024
TQ = 512
ML_CHUNK = 256
ML_NB = 2
S5_TC = 256
S5_UNROLL = 8
TB_RANK = 512
TM_MOE = 256
TM_DISP = 256
MOE_PARTS = 2
SC_SCATTER_CHUNK = 128
SC_GATHER_CHUNK = 64
GATE_PAD = 8
VT_ROWS = DA_V_DIM + 16
Q_PRESCALE = DA_HEAD_DIM ** -0.5 * math.log2(math.e)


def _cparams(sem, vmem=VMEM_LIMIT_BYTES):
    return pltpu.CompilerParams(dimension_semantics=sem, vmem_limit_bytes=vmem)


def _sigmoid(x):
    return 1.0 / (1.0 + jnp.exp(-x))


def _mod_kernel(c_ref, w_ref, b_ref, o_ref):
    c = c_ref[...]
    ca = (c * _sigmoid(c)).astype(BF16)
    w = w_ref[0, 0].astype(BF16)
    o_ref[0] = jnp.dot(ca, w, preferred_element_type=F32) + b_ref[0]


def _modulation(c, ada_w, ada_b):
    nsub = ada_w.shape[1]
    nmod = ada_w.shape[0] * nsub
    bsz, d = c.shape
    e = ada_w.shape[-1]
    tn = 1024
    b = ada_b.reshape(nmod, 1, e)
    return pl.pallas_call(
        _mod_kernel,
        grid=(nmod, e // tn),
        in_specs=[pl.BlockSpec((bsz, d), lambda n, j: (0, 0)),
                  pl.BlockSpec((1, 1, d, tn), lambda n, j: (n // nsub, n % nsub, 0, j)),
                  pl.BlockSpec((1, 1, tn), lambda n, j: (n, 0, j))],
        out_specs=pl.BlockSpec((1, bsz, tn), lambda n, j: (n, 0, j)),
        out_shape=jax.ShapeDtypeStruct((nmod, bsz, e), F32),
        compiler_params=_cparams(("arbitrary", "arbitrary")),
        name="modulation",
    )(c, ada_w, b)


def _rope_kernel(pos_ref, cos_ref, sin_ref):
    nfreq = DA_HEAD_DIM // 2
    pos = pos_ref[0].astype(F32)
    fidx = lax.broadcasted_iota(jnp.int32, (nfreq, 1), 0).astype(F32)
    inv = jnp.exp(fidx * (-2.0 * math.log(ROPE_THETA) / DA_HEAD_DIM))
    ang = inv * pos
    reps = LANES // nfreq
    cos_t = jnp.concatenate([jnp.cos(ang)] * reps, axis=0).T
    sin_t = jnp.concatenate([jnp.sin(ang)] * reps, axis=0).T
    lane = lax.broadcasted_iota(jnp.int32, (1, LANES), 1)
    sign = jnp.where((lane % DA_HEAD_DIM) < nfreq, -1.0, 1.0)
    cos_ref[0] = cos_t
    sin_ref[0] = sin_t * sign


def _rope_tables(positions):
    bsz, s = positions.shape
    ts = 512
    pos3 = positions.reshape(bsz, 1, s)
    return pl.pallas_call(
        _rope_kernel,
        grid=(bsz, s // ts),
        in_specs=[pl.BlockSpec((1, 1, ts), lambda b, i: (b, 0, i))],
        out_specs=[pl.BlockSpec((1, ts, LANES), lambda b, i: (b, i, 0))] * 2,
        out_shape=[jax.ShapeDtypeStruct((bsz, s, LANES), F32)] * 2,
        compiler_params=_cparams(("arbitrary", "arbitrary")),
        name="rope_tables",
    )(pos3)


def _in_proj_kernel(x_ref, shift_ref, scale_ref, cos_ref, sin_ref, wqk_ref, wvt_ref, wrest_ref, wgt_ref, wgc_ref,
                    q_ref, k_ref, vt_ref, mlx_ref, mlv_ref, mlo_ref, gt_ref, gc_ref, s5u_ref):
    h = (x_ref[0] * (1.0 + scale_ref[0]) + shift_ref[0]).astype(BF16)
    cos = cos_ref[0]
    sin = sin_ref[0]
    lane = lax.broadcasted_iota(jnp.int32, (1, LANES), 1)
    lo_half = (lane % DA_HEAD_DIM) < DA_HEAD_DIM // 2
    half = DA_HEAD_DIM // 2

    def rope(t):
        fwd = pltpu.roll(t, half, 1)
        bwd = pltpu.roll(t, LANES - half, 1)
        partner = jnp.where(lo_half, bwd, fwd)
        return t * cos + partner * sin

    qk = jnp.dot(h, wqk_ref[...], preferred_element_type=F32)
    nslab = DA_QK_WIDTH // LANES
    for c in range(nslab):
        q_ref[0, :, c * LANES:(c + 1) * LANES] = (
            rope(qk[:, c * LANES:(c + 1) * LANES]) * Q_PRESCALE).astype(BF16)
        k_ref[0, :, c * LANES:(c + 1) * LANES] = rope(
            qk[:, DA_QK_WIDTH + c * LANES:DA_QK_WIDTH + (c + 1) * LANES]).astype(BF16)

    vt = lax.dot_general(wvt_ref[...], h, (((1,), (1,)), ((), ())), preferred_element_type=F32)
    tm = h.shape[0]
    for hh in range(DA_HEADS):
        for jj in range(tm // TQ):
            vt_ref[0, hh, jj, 0:DA_V_DIM, :] = vt[hh * DA_V_DIM:(hh + 1) * DA_V_DIM,
                                                  jj * TQ:(jj + 1) * TQ].astype(BF16)
            vt_ref[0, hh, jj, DA_V_DIM:VT_ROWS, :] = jnp.ones((VT_ROWS - DA_V_DIM, TQ), BF16)

    r = jnp.dot(h, wrest_ref[...], preferred_element_type=F32)
    o = 0
    mlx_ref[0] = r[:, o:o + ML_WIDTH].astype(BF16); o += ML_WIDTH
    mlv_ref[0] = r[:, o:o + ML_WIDTH].astype(BF16); o += ML_WIDTH
    mlo_ref[0] = r[:, o:o + ML_WIDTH].astype(BF16); o += ML_WIDTH
    s5u_ref[...] = r[:, o:o + S5_WIDTH]
    gt_ref[0] = lax.dot_general(wgt_ref[...], h, (((1,), (1,)), ((), ())), preferred_element_type=F32)
    gc_ref[0] = jnp.dot(h, wgc_ref[...], preferred_element_type=F32)


def _in_proj(x, shift, scale, cos_t, sin_t, w_in_l):
    bsz, s, d = x.shape
    tm = TM_PROJ
    w = w_in_l.astype(BF16)
    wqk = w[:, :OFF_DA_V]
    wvt = w[:, OFF_DA_V:OFF_ML_X].T
    wrest = jnp.concatenate([w[:, OFF_ML_X:OFF_ML_I], w[:, OFF_S5_U:]], axis=1)
    wg = w[:, OFF_ML_I:OFF_S5_U]
    wgt = wg.T
    wgc = jnp.pad(wg, ((0, 0), (0, LANES - GATE_PAD)))
    nrest = wrest.shape[1]
    shift3 = shift.reshape(bsz, 1, d)
    scale3 = scale.reshape(bsz, 1, d)
    tok = lambda b, i: (b, i, 0)
    per_b = lambda b, i: (b, 0, 0)
    const2 = lambda b, i: (0, 0)
    out_shapes = [
        jax.ShapeDtypeStruct((bsz, s, DA_QK_WIDTH), BF16),
        jax.ShapeDtypeStruct((bsz, s, DA_QK_WIDTH), BF16),
        jax.ShapeDtypeStruct((bsz, DA_HEADS, s // TQ, VT_ROWS, TQ), BF16),
        jax.ShapeDtypeStruct((bsz, s, ML_WIDTH), BF16),
        jax.ShapeDtypeStruct((bsz, s, ML_WIDTH), BF16),
        jax.ShapeDtypeStruct((bsz, s, ML_WIDTH), BF16),
        jax.ShapeDtypeStruct((bsz, GATE_PAD, s), F32),
        jax.ShapeDtypeStruct((bsz, s, LANES), F32),
        jax.ShapeDtypeStruct((s, bsz * S5_WIDTH), F32),
    ]
    out_specs = [
        pl.BlockSpec((1, tm, DA_QK_WIDTH), tok),
        pl.BlockSpec((1, tm, DA_QK_WIDTH), tok),
        pl.BlockSpec((1, DA_HEADS, tm // TQ, VT_ROWS, TQ), lambda b, i: (b, 0, i, 0, 0)),
        pl.BlockSpec((1, tm, ML_WIDTH), tok),
        pl.BlockSpec((1, tm, ML_WIDTH), tok),
        pl.BlockSpec((1, tm, ML_WIDTH), tok),
        pl.BlockSpec((1, GATE_PAD, tm), lambda b, i: (b, 0, i)),
        pl.BlockSpec((1, tm, LANES), tok),
        pl.BlockSpec((tm, S5_WIDTH), lambda b, i: (i, b)),
    ]
    return pl.pallas_call(
        _in_proj_kernel,
        grid=(bsz, s // tm),
        in_specs=[pl.BlockSpec((1, tm, d), tok),
                  pl.BlockSpec((1, 1, d), per_b),
                  pl.BlockSpec((1, 1, d), per_b),
                  pl.BlockSpec((1, tm, LANES), tok),
                  pl.BlockSpec((1, tm, LANES), tok),
                  pl.BlockSpec((d, OFF_DA_V), const2),
                  pl.BlockSpec((DA_WIDTH, d), const2),
                  pl.BlockSpec((d, nrest), const2),
                  pl.BlockSpec((GATE_PAD, d), const2),
                  pl.BlockSpec((d, LANES), const2)],
        out_specs=out_specs,
        out_shape=out_shapes,
        compiler_params=_cparams(("arbitrary", "arbitrary")),
        name="in_proj",
    )(x, shift3, scale3, cos_t, sin_t, wqk, wvt, wrest, wgt, wgc)


def _diff_attn_kernel(lam_init, lamv_ref, gain_ref, q_ref, k_ref, vt_ref, o_ref, acc_s, m_s):
    qi = pl.program_id(2)
    tq = q_ref.shape[1]
    lane = lax.broadcasted_iota(jnp.int32, (1, LANES), 1)
    first = lane < DA_HEAD_DIM
    q = q_ref[0]
    zero = jnp.zeros_like(q)
    qm = (jnp.where(first, q, zero), jnp.where(first, zero, q))
    acc_s[...] = jnp.zeros_like(acc_s)
    m_s[...] = jnp.full(m_s.shape, NEG, F32)

    def step(j, nblk, masked):
        tk = nblk * tq
        kb = k_ref[0, pl.ds(pl.multiple_of(j * tq, tq), tk), :]
        vtb = vt_ref[0, 0, j] if nblk == 1 else jnp.concatenate([vt_ref[0, 0, j + b] for b in range(nblk)], axis=1)
        for c in range(2):
            st = lax.dot_general(kb, qm[c], (((1,), (1,)), ((), ())), preferred_element_type=F32)
            if masked:
                key_i = lax.broadcasted_iota(jnp.int32, (tk, tq), 0) - (tk - tq)
                qry_i = lax.broadcasted_iota(jnp.int32, (tk, tq), 1)
                st = jnp.where(key_i <= qry_i, st, NEG)
            m_prev = m_s[c]
            m_new = jnp.maximum(m_prev, jnp.max(st, axis=0, keepdims=True))
            alpha = jnp.exp2(m_prev - m_new)
            p = jnp.exp2(st - m_new).astype(BF16)
            acc_s[c] = alpha * acc_s[c] + jnp.dot(vtb, p, preferred_element_type=F32)
            m_s[c] = m_new

    def body(jj, carry):
        step(2 * jj, 2, False)
        return carry

    lax.fori_loop(0, qi // 2, body, 0)

    @pl.when(qi % 2 == 1)
    def _():
        step(qi - 1, 2, True)

    @pl.when(qi % 2 == 0)
    def _():
        step(qi, 1, True)

    outs = []
    for c in range(2):
        acc = acc_s[c]
        outs.append(acc[:DA_V_DIM] / acc[DA_V_DIM:DA_V_DIM + 1])

    lamv = lamv_ref[...]
    lam = (jnp.exp(jnp.sum(lamv[0:1] * lamv[1:2], axis=1, keepdims=True))
           - jnp.exp(jnp.sum(lamv[2:3] * lamv[3:4], axis=1, keepdims=True)) + lam_init)
    ot = outs[0] - lam * outs[1]
    ms = jnp.mean(ot * ot, axis=0, keepdims=True)
    ot = ot * (lax.rsqrt(ms + LN_EPS) * (1.0 - lam_init))
    o_ref[0] = (ot.T * gain_ref[...]).astype(o_ref.dtype)


def _diff_attn(q, k, vt, lamv, gain, lam_init):
    bsz, s, _ = q.shape
    tq = TQ
    nq = s // tq
    return pl.pallas_call(
        functools.partial(_diff_attn_kernel, lam_init),
        grid=(bsz, DA_HEADS, nq),
        in_specs=[pl.BlockSpec((4, DA_HEAD_DIM), lambda b, h, i: (0, 0)),
                  pl.BlockSpec((1, DA_V_DIM), lambda b, h, i: (0, 0)),
                  pl.BlockSpec((1, tq, DA_V_DIM), lambda b, h, i: (b, i, h)),
                  pl.BlockSpec((1, s, DA_V_DIM), lambda b, h, i: (b, 0, h)),
                  pl.BlockSpec((1, 1, nq, VT_ROWS, tq), lambda b, h, i: (b, h, 0, 0, 0))],
        out_specs=pl.BlockSpec((1, tq, DA_V_DIM), lambda b, h, i: (b, i, h)),
        out_shape=jax.ShapeDtypeStruct((bsz, s, DA_WIDTH), BF16),
        scratch_shapes=[pltpu.VMEM((2, VT_ROWS, tq), F32), pltpu.VMEM((2, 1, tq), F32)],
        compiler_params=_cparams(("arbitrary", "arbitrary", "arbitrary")),
        name="diff_attn",
    )(lamv, gain.reshape(1, DA_V_DIM), q, k, vt)


def _log_sigmoid(x):
    return jnp.minimum(x, 0.0) - jnp.log(1.0 + jnp.exp(-jnp.abs(x)))


def _split3(a):
    hi = a.astype(BF16)
    r1 = a - hi.astype(F32)
    mid = r1.astype(BF16)
    lo = (r1 - mid.astype(F32)).astype(BF16)
    return hi, mid, lo


def _mlstm_kernel(x_ref, v_ref, o_ref, gt_ref, gc_ref, cw_ref, cb_ref, wq_ref, wkt_ref, gbt_ref, gbc_ref,
                  ng_ref, hmean_ref, y_ref, xc_s, c_s, m_s):
    nb, s = x_ref.shape[0], x_ref.shape[1]
    L = ML_CHUNK
    H, dh = ML_HEADS, ML_HEAD_DIM
    nc = s // L
    cw = cw_ref[...]
    row = lax.broadcasted_iota(jnp.int32, (s, 1), 0)
    for bi in range(nb):
        x = x_ref[bi].astype(F32)
        xc = x * cw[ML_CONV - 1:ML_CONV]
        for j in range(1, ML_CONV):
            xs = jnp.where(row >= j, pltpu.roll(x, j, 0), 0.0)
            xc = xc + xs * cw[ML_CONV - 1 - j:ML_CONV - j]
        xc = xc + cb_ref[...]
        xc_s[bi] = (xc * _sigmoid(xc)).astype(BF16)

    c_s[...] = jnp.zeros_like(c_s)
    m_s[...] = jnp.full(m_s.shape, NEG, F32)

    ri = lax.broadcasted_iota(jnp.int32, (L, L), 0)
    ci = lax.broadcasted_iota(jnp.int32, (L, L), 1)
    causal = ci <= ri
    tril = causal.astype(BF16)
    triu = (ri <= ci).astype(BF16)
    lane = lax.broadcasted_iota(jnp.int32, (1, dh), 1)
    one_hot0 = jnp.broadcast_to((lane == 0).astype(BF16), (L, dh))

    def chunk_one(bi, ci_, t0):
        xcc = xc_s[bi, pl.ds(t0, L), :]
        qc = jnp.dot(xcc, wq_ref[...], preferred_element_type=F32).astype(BF16)
        ktc = lax.dot_general(wkt_ref[...], xcc, (((1,), (1,)), ((), ())),
                              preferred_element_type=F32)
        g_rows = gt_ref[bi, ci_] + gbt_ref[...]
        g_cols = gc_ref[bi, pl.ds(t0, L), :] + gbc_ref[...]
        lf_rows = _log_sigmoid(g_rows)
        lf_cols = _log_sigmoid(g_cols)
        r3 = jnp.dot(jnp.concatenate(_split3(lf_rows), axis=0), triu, preferred_element_type=F32)
        b_rows = r3[0:GATE_PAD] + r3[GATE_PAD:2 * GATE_PAD] + r3[2 * GATE_PAD:]
        c3 = jnp.dot(tril, jnp.concatenate(_split3(lf_cols), axis=1), preferred_element_type=F32)
        b_cols = c3[:, 0:LANES] + c3[:, LANES:2 * LANES] + c3[:, 2 * LANES:]
        vch = v_ref[bi, pl.ds(t0, L), :]
        och = o_ref[bi, pl.ds(t0, L), :].astype(F32)
        hs = []
        for h in range(H):
            br = b_rows[H + h:H + h + 1, :]
            ir = g_rows[h:h + 1, :]
            bc = b_cols[:, H + h:H + h + 1]
            m_prev = m_s[bi, h]
            log_d = jnp.where(causal, bc - br + ir, NEG)
            inter = bc + m_prev
            mx = jnp.maximum(inter, jnp.max(log_d, axis=1, keepdims=True))
            dmat = jnp.exp(log_d - mx)
            dec = jnp.exp(inter - mx)
            qh = qc[:, h * dh:(h + 1) * dh]
            kth = ktc[h * dh:(h + 1) * dh, :]
            vaug = jnp.concatenate([vch[:, h * dh:(h + 1) * dh], one_hot0], axis=1)
            sm = (jnp.dot(qh, kth.astype(BF16), preferred_element_type=F32) * dmat).astype(BF16)
            c_prev = c_s[bi, h]
            na = (jnp.dot(sm, vaug, preferred_element_type=F32)
                  + dec * jnp.dot(qh, c_prev.astype(BF16), preferred_element_type=F32))
            den = na[:, dh:dh + 1]
            hs.append(na[:, :dh] / jnp.maximum(jnp.abs(den), jnp.exp(-mx)))
            g_tot = br[:, L - 1:L]
            a_row = g_tot - br + ir
            m_new = jnp.maximum(g_tot + m_prev, jnp.max(a_row, axis=1, keepdims=True))
            decay = jnp.exp(g_tot + m_prev - m_new)
            w_row = jnp.exp(a_row - m_new)
            kw = (kth * w_row).astype(BF16)
            c_s[bi, h] = decay * c_prev + jnp.dot(kw, vaug, preferred_element_type=F32)
            m_s[bi, h] = m_new
        hcat = jnp.concatenate(hs, axis=1)
        m3 = jnp.dot(jnp.concatenate(_split3(hcat * hcat), axis=0), hmean_ref[...],
                     preferred_element_type=F32)
        ms = m3[0:L] + m3[L:2 * L] + m3[2 * L:]
        y = hcat * lax.rsqrt(ms + LN_EPS) * ng_ref[...] * _sigmoid(och)
        y_ref[bi, pl.ds(t0, L), :] = y.astype(y_ref.dtype)

    def chunk(ci_, _):
        t0 = pl.multiple_of(ci_ * L, L)
        for bi in range(nb):
            chunk_one(bi, ci_, t0)
        return 0

    lax.fori_loop(0, nc, chunk, 0)


def _mlstm(mlx, mlv, mlo, g_t, g_c, conv_w, conv_b, w_q, w_k, gate_b, norm_g):
    bsz, s, _ = mlx.shape
    H, dh = ML_HEADS, ML_HEAD_DIM
    eye = jnp.eye(H, dtype=F32)
    wq_bd = jnp.einsum('hde,hg->hdge', w_q, eye).reshape(ML_WIDTH, ML_WIDTH).astype(BF16)
    wk_bd = jnp.einsum('hde,hg->hdge', w_k * (dh ** -0.5), eye).reshape(ML_WIDTH, ML_WIDTH)
    wkt_bd = wk_bd.T.astype(BF16)
    gbt = gate_b.reshape(GATE_PAD, 1)
    gbc = jnp.pad(gate_b.reshape(1, GATE_PAD), ((0, 0), (0, LANES - GATE_PAD)))
    hmean = jnp.kron(eye, jnp.full((dh, dh), 1.0 / dh, F32)).astype(BF16)
    nc = s // ML_CHUNK
    g_t4 = g_t.reshape(bsz, GATE_PAD, nc, ML_CHUNK).transpose(0, 2, 1, 3)
    tok = lambda b: (b, 0, 0)
    c2 = lambda b: (0, 0)
    nb = ML_NB
    return pl.pallas_call(
        _mlstm_kernel,
        grid=(bsz // nb,),
        in_specs=[pl.BlockSpec((nb, s, ML_WIDTH), tok),
                  pl.BlockSpec((nb, s, ML_WIDTH), tok),
                  pl.BlockSpec((nb, s, ML_WIDTH), tok),
                  pl.BlockSpec((nb, nc, GATE_PAD, ML_CHUNK), lambda b: (b, 0, 0, 0)),
                  pl.BlockSpec((nb, s, LANES), tok),
                  pl.BlockSpec((ML_CONV, ML_WIDTH), c2),
                  pl.BlockSpec((1, ML_WIDTH), c2),
                  pl.BlockSpec((ML_WIDTH, ML_WIDTH), c2),
                  pl.BlockSpec((ML_WIDTH, ML_WIDTH), c2),
                  pl.BlockSpec((GATE_PAD, 1), c2),
                  pl.BlockSpec((1, LANES), c2),
                  pl.BlockSpec((1, ML_WIDTH), c2),
                  pl.BlockSpec((ML_WIDTH, ML_WIDTH), c2)],
        out_specs=pl.BlockSpec((nb, s, ML_WIDTH), tok),
        out_shape=jax.ShapeDtypeStruct((bsz, s, ML_WIDTH), BF16),
        scratch_shapes=[pltpu.VMEM((nb, s, ML_WIDTH), BF16),
                        pltpu.VMEM((nb, H, dh, LANES), F32),
                        pltpu.VMEM((nb, H, 1, 1), F32)],
        compiler_params=_cparams(("arbitrary",)),
        name="mlstm",
    )(mlx, mlv, mlo, g_t4, g_c, conv_w, conv_b.reshape(1, ML_WIDTH), wq_bd, wkt_bd, gbt, gbc,
      norm_g.reshape(1, ML_WIDTH), hmean)


def _gelu_tanh(x):
    return 0.5 * x * (1.0 + jnp.tanh(math.sqrt(2.0 / math.pi) * (x + 0.044715 * (x * x * x))))


def _s5_kernel(u_ref, are_ref, aim_ref, bcat_ref, ccat_ref, d_ref, wglu_ref, y_ref, xs_s, st_s):
    tc, bsz, w = u_ref.shape
    n = S5_NSTATE

    @pl.when(pl.program_id(0) == 0)
    def _():
        st_s[...] = jnp.zeros_like(st_s)

    u = u_ref[...].reshape(tc * bsz, w)
    xs_s[...] = jnp.dot(u.astype(BF16), bcat_ref[...], preferred_element_type=F32).reshape(tc, bsz, 2 * n)
    a_re = jnp.broadcast_to(are_ref[...], (bsz, n))
    a_im = jnp.broadcast_to(aim_ref[...], (bsz, n))

    def step(t, carry):
        x_re, x_im = carry
        bu = xs_s[t]
        n_re = a_re * x_re - a_im * x_im + bu[:, :n]
        n_im = a_re * x_im + a_im * x_re + bu[:, n:]
        xs_s[t] = jnp.concatenate([n_re, n_im], axis=1)
        return n_re, n_im

    x_re, x_im = lax.fori_loop(0, tc, step, (st_s[0], st_s[1]), unroll=S5_UNROLL)
    st_s[0] = x_re
    st_s[1] = x_im

    xs = xs_s[...].reshape(tc * bsz, 2 * n).astype(BF16)
    y = jnp.dot(xs, ccat_ref[...], preferred_element_type=F32) + d_ref[...] * u
    z = jnp.dot(_gelu_tanh(y).astype(BF16), wglu_ref[...], preferred_element_type=F32)
    out = z[:, :w] * _sigmoid(z[:, w:])
    y_ref[...] = out.reshape(tc, bsz, w).astype(y_ref.dtype)


def _s5_params(a_re, a_im, log_dt, b_re, b_im, c_re, c_im, w_glu):
    G, P, Hc = S5_GROUPS, S5_STATE, S5_GROUP
    dt = jnp.exp(log_dt)[:, None]
    mag = jnp.exp(a_re * dt)
    ab_re = mag * jnp.cos(a_im * dt)
    ab_im = mag * jnp.sin(a_im * dt)
    nr, ni = ab_re - 1.0, ab_im
    den = a_re * a_re + a_im * a_im
    fr = (nr * a_re + ni * a_im) / den
    fi = (ni * a_re - nr * a_im) / den
    bb_re = fr[..., None] * b_re - fi[..., None] * b_im
    bb_im = fr[..., None] * b_im + fi[..., None] * b_re
    eye = jnp.eye(G, dtype=F32)
    bd = lambda t, sub: jnp.einsum(sub, t, eye)
    bre = bd(bb_re, 'gph,gk->ghkp').reshape(G * Hc, G * P)
    bim = bd(bb_im, 'gph,gk->ghkp').reshape(G * Hc, G * P)
    bcat = jnp.concatenate([bre, bim], axis=1).astype(BF16)
    cre = bd(c_re, 'ghp,gk->gpkh').reshape(G * P, G * Hc)
    cim = bd(c_im, 'ghp,gk->gpkh').reshape(G * P, G * Hc)
    ccat = jnp.concatenate([cre, -cim], axis=0).astype(BF16)
    wv = bd(w_glu[:, :, :Hc], 'ghj,gk->ghkj').reshape(G * Hc, G * Hc)
    wg = bd(w_glu[:, :, Hc:], 'ghj,gk->ghkj').reshape(G * Hc, G * Hc)
    wglu = jnp.concatenate([wv, wg], axis=1).astype(BF16)
    return ab_re.reshape(1, G * P), ab_im.reshape(1, G * P), bcat, ccat, wglu


def _s5(u_tm, bsz, params, d_skip):
    s = u_tm.shape[0]
    w = S5_WIDTH
    n = S5_NSTATE
    are, aim, bcat, ccat, wglu = params
    u3 = u_tm.reshape(s, bsz, w)
    tc = S5_TC
    c2 = lambda i: (0, 0)
    y = pl.pallas_call(
        _s5_kernel,
        grid=(s // tc,),
        in_specs=[pl.BlockSpec((tc, bsz, w), lambda i: (i, 0, 0)),
                  pl.BlockSpec((1, n), c2),
                  pl.BlockSpec((1, n), c2),
                  pl.BlockSpec((w, 2 * n), c2),
                  pl.BlockSpec((2 * n, w), c2),
                  pl.BlockSpec((1, w), c2),
                  pl.BlockSpec((w, 2 * w), c2)],
        out_specs=pl.BlockSpec((tc, bsz, w), lambda i: (i, 0, 0)),
        out_shape=jax.ShapeDtypeStruct((s, bsz, w), F32),
        scratch_shapes=[pltpu.VMEM((tc, bsz, 2 * n), F32),
                        pltpu.VMEM((2, bsz, n), F32)],
        compiler_params=_cparams(("arbitrary",)),
        name="s5",
    )(u3, are, aim, bcat, ccat, d_skip.reshape(1, w), wglu)
    return y.reshape(s, bsz * w)


def _layer_norm(z, g, b):
    mu = jnp.mean(z, axis=1, keepdims=True)
    zc = z - mu
    var = jnp.mean(zc * zc, axis=1, keepdims=True)
    return zc * lax.rsqrt(var + LN_EPS) * g + b


def _out_proj_kernel(yda_ref, yml_ref, ys5_ref, x_ref, gate_ref, lng_ref, lnb_ref, shift_ref, scale_ref,
                     wout_ref, wrt_ref, brt_ref, x1_ref, h2_ref, eid_ref, prob_ref, cnt_ref):
    y = jnp.dot(yda_ref[0], wout_ref[0:DA_WIDTH, :], preferred_element_type=F32)
    y = y + jnp.dot(yml_ref[0], wout_ref[DA_WIDTH:DA_WIDTH + ML_WIDTH, :], preferred_element_type=F32)
    y = y + jnp.dot(ys5_ref[...].astype(BF16), wout_ref[DA_WIDTH + ML_WIDTH:, :], preferred_element_type=F32)
    x1 = _layer_norm(DN_ALPHA * x_ref[0] + (1.0 + gate_ref[0]) * y, lng_ref[...], lnb_ref[...])
    x1_ref[0] = x1
    h2 = x1 * (1.0 + scale_ref[0]) + shift_ref[0]
    h_hi = h2.astype(BF16)
    h_hi32 = h_hi.astype(F32)
    h2_ref[...] = _pack_rounded(h_hi32)
    h_lo = (h2 - h_hi32).astype(BF16)
    nt_dot = lambda a, b: lax.dot_general(a, b, (((1,), (1,)), ((), ())), preferred_element_type=F32)
    by_hi = nt_dot(wrt_ref[...], h_hi)
    logits = (by_hi[:N_EXPERTS] + by_hi[N_EXPERTS:] + nt_dot(wrt_ref[0:N_EXPERTS, :], h_lo)
              + brt_ref[...])
    eidx = lax.broadcasted_iota(jnp.int32, logits.shape, 0)
    vals, ids = [], []
    for _ in range(TOP_K):
        mx = jnp.max(logits, axis=0, keepdims=True)
        sel = jnp.min(jnp.where(logits == mx, eidx, N_EXPERTS), axis=0, keepdims=True)
        vals.append(mx)
        ids.append(sel)
        logits = jnp.where(eidx == sel, -jnp.inf, logits)
    ex = [jnp.exp(v - vals[0]) for v in vals]
    tot = ex[0] + ex[1] + ex[2] + ex[3]
    zi = jnp.zeros_like(ids[0])
    eid_ref[...] = jnp.concatenate(ids + [zi] * (SUBLANES - TOP_K), axis=0)

    @pl.when(jnp.logical_and(pl.program_id(0) == 0, pl.program_id(1) == 0))
    def _():
        cnt_ref[...] = jnp.zeros_like(cnt_ref)

    member = jnp.zeros(logits.shape, F32)
    for sel in ids:
        member = member + (eidx == sel).astype(F32)
    cnt_ref[...] = cnt_ref[...] + jnp.sum(member, axis=1, keepdims=True)
    zf = jnp.zeros((LANES - TOP_K, tot.shape[1]), F32)
    prob_ref[...] = jnp.concatenate([e / tot for e in ex] + [zf], axis=0).T


def _out_proj(y_da, y_ml, y_s5, x, gate, ln_g, ln_b, shift2, scale2, w_out_l, w_router_l, b_router_l):
    bsz, s, d = x.shape
    tm = TM_PROJ
    nt = s // tm
    tok = lambda b, i: (b, i, 0)
    per_b = lambda b, i: (b, 0, 0)
    c2 = lambda b, i: (0, 0)
    r3 = lambda a: a.reshape(bsz, 1, d)
    flat = lambda b, i: (0, b * nt + i)
    wr_t = w_router_l.T
    wr_hi = wr_t.astype(BF16)
    wr_lo = (wr_t - wr_hi.astype(F32)).astype(BF16)
    return pl.pallas_call(
        _out_proj_kernel,
        grid=(bsz, nt),
        in_specs=[pl.BlockSpec((1, tm, DA_WIDTH), tok),
                  pl.BlockSpec((1, tm, ML_WIDTH), tok),
                  pl.BlockSpec((tm, S5_WIDTH), lambda b, i: (i, b)),
                  pl.BlockSpec((1, tm, d), tok),
                  pl.BlockSpec((1, 1, d), per_b),
                  pl.BlockSpec((1, d), c2),
                  pl.BlockSpec((1, d), c2),
                  pl.BlockSpec((1, 1, d), per_b),
                  pl.BlockSpec((1, 1, d), per_b),
                  pl.BlockSpec((d, d), c2),
                  pl.BlockSpec((2 * N_EXPERTS, d), c2),
                  pl.BlockSpec((N_EXPERTS, 1), c2)],
        out_specs=[pl.BlockSpec((1, tm, d), tok),
                   pl.BlockSpec((tm, d // 2), lambda b, i: (b * nt + i, 0)),
                   pl.BlockSpec((SUBLANES, tm), flat),
                   pl.BlockSpec((tm, LANES), lambda b, i: (b * nt + i, 0)),
                   pl.BlockSpec((N_EXPERTS, LANES), c2)],
        out_shape=[jax.ShapeDtypeStruct((bsz, s, d), F32),
                   jax.ShapeDtypeStruct((bsz * s, d // 2), jnp.int32),
                   jax.ShapeDtypeStruct((SUBLANES, bsz * s), jnp.int32),
                   jax.ShapeDtypeStruct((bsz * s, LANES), F32),
                   jax.ShapeDtypeStruct((N_EXPERTS, LANES), F32)],
        compiler_params=_cparams(("arbitrary", "arbitrary")),
        name="out_proj",
    )(y_da, y_ml, y_s5, x, r3(gate), ln_g.reshape(1, d), ln_b.reshape(1, d), r3(shift2), r3(scale2),
      w_out_l.astype(BF16), jnp.concatenate([wr_hi, wr_lo], axis=0), b_router_l.reshape(N_EXPERTS, 1))


META_END, META_PAD, META_CNT = 0, 1, 2


def _route_kernel(eid_ref, cnt_ref, pos_ref, meta_ref, carry_s, start_s):
    i = pl.program_id(0)
    tb = eid_ref.shape[1]
    ntp = meta_ref.shape[1]
    tm = TM_MOE

    @pl.when(i == 0)
    def _():
        cnt = cnt_ref[...]
        padded = jnp.floor((cnt + (tm - 1)) * (1.0 / tm)) * tm
        er = lax.broadcasted_iota(jnp.int32, (N_EXPERTS, N_EXPERTS), 0)
        ec = lax.broadcasted_iota(jnp.int32, (N_EXPERTS, N_EXPERTS), 1)
        ends = jnp.dot((ec <= er).astype(F32), padded, preferred_element_type=F32, precision=HIGHEST)
        start_s[...] = ends - padded
        carry_s[...] = jnp.zeros_like(carry_s)
        lane = lax.broadcasted_iota(jnp.int32, (N_EXPERTS, ntp), 1)
        sub = lax.broadcasted_iota(jnp.int32, (N_EXPERTS, ntp), 0)
        diag = lane == sub

        def as_row(col):
            return jnp.sum(jnp.where(diag, col, 0.0), axis=0, keepdims=True)

        zero = jnp.zeros((SUBLANES - 3, ntp), F32)
        meta_ref[...] = jnp.concatenate([as_row(ends[:, 0:1]), as_row(padded[:, 0:1]), as_row(cnt[:, 0:1]), zero],
                                        axis=0).astype(jnp.int32)

    eid = eid_ref[...]
    eidx = lax.broadcasted_iota(jnp.int32, (N_EXPERTS, tb), 0)
    hot = [eidx == eid[k:k + 1, :] for k in range(TOP_K)]
    member = jnp.zeros((N_EXPERTS, tb), F32)
    for k in range(TOP_K):
        member = member + hot[k].astype(F32)
    ri = lax.broadcasted_iota(jnp.int32, (tb, tb), 0)
    ci = lax.broadcasted_iota(jnp.int32, (tb, tb), 1)
    triu = (ri <= ci).astype(BF16)
    incl = jnp.dot(member.astype(BF16), triu, preferred_element_type=F32)
    slot = incl - member + carry_s[:, 0:1] + start_s[:, 0:1]
    rows = [jnp.sum(jnp.where(hot[k], slot, 0.0), axis=0, keepdims=True) for k in range(TOP_K)]
    zr = jnp.zeros_like(rows[0])
    pos_ref[...] = jnp.concatenate(rows + [zr] * (SUBLANES - TOP_K), axis=0).astype(jnp.int32)
    carry_s[...] = carry_s[...] + jnp.sum(member, axis=1, keepdims=True)


def _route(eid, counts):
    t = eid.shape[1]
    tb = TB_RANK
    ntp = LANES
    pos8, meta = pl.pallas_call(
        _route_kernel,
        grid=(t // tb,),
        in_specs=[pl.BlockSpec((SUBLANES, tb), lambda i: (0, i)),
                  pl.BlockSpec((N_EXPERTS, LANES), lambda i: (0, 0))],
        out_specs=[pl.BlockSpec((SUBLANES, tb), lambda i: (0, i)),
                   pl.BlockSpec((SUBLANES, ntp), lambda i: (0, 0))],
        out_shape=[jax.ShapeDtypeStruct((SUBLANES, t), jnp.int32),
                   jax.ShapeDtypeStruct((SUBLANES, ntp), jnp.int32)],
        scratch_shapes=[pltpu.VMEM((N_EXPERTS, LANES), F32), pltpu.VMEM((N_EXPERTS, LANES), F32)],
        compiler_params=_cparams(("arbitrary",)),
        name="route",
    )(eid, counts)
    return pos8, meta


def _sc_workers():
    info = plsc.get_sparse_core_info()
    return info.num_cores, info.num_cores * info.num_subcores


def _dispatch(h2, pos8, n_rows):
    t, d = h2.shape
    n_cores, n_workers = _sc_workers()
    tpw = t // n_workers
    ch = SC_SCATTER_CHUNK
    mesh = plsc.VectorSubcoreMesh(core_axis_name="c", subcore_axis_name="s")

    @functools.partial(
        pl.kernel, mesh=mesh,
        out_type=jax.ShapeDtypeStruct((n_rows, d), h2.dtype),
        scratch_types=[pltpu.VMEM((ch,), jnp.int32)] * TOP_K + [pltpu.VMEM((ch, d), h2.dtype),
                                                                pltpu.SemaphoreType.DMA])
    def scatter_rows(h_hbm, pos_hbm, out_hbm, i0, i1, i2, i3, rows_v, sem):
        idx = (i0, i1, i2, i3)
        base = (lax.axis_index("s") * n_cores + lax.axis_index("c")) * tpw

        @pl.loop(0, tpw // ch)
        def _(i):
            off = base + i * ch
            pltpu.sync_copy(h_hbm.at[pl.ds(off, ch)], rows_v)
            for k in range(TOP_K):
                pltpu.sync_copy(pos_hbm.at[k, pl.ds(off, ch)], idx[k])
            copies = [pltpu.async_copy(rows_v, out_hbm.at[idx[k]], sem) for k in range(TOP_K)]
            for cp in copies:
                cp.wait()

    return scatter_rows(h2, pos8)


def _gather_expert_rows(ys, pos8):
    _, d = ys.shape
    t = pos8.shape[1]
    n_cores, n_workers = _sc_workers()
    tpw = t // n_workers
    ch = SC_GATHER_CHUNK
    mesh = plsc.VectorSubcoreMesh(core_axis_name="c", subcore_axis_name="s")

    @functools.partial(
        pl.kernel, mesh=mesh,
        out_type=jax.ShapeDtypeStruct((TOP_K, t, d), ys.dtype),
        scratch_types=([pltpu.VMEM((ch,), jnp.int32)] * 2 + [pltpu.VMEM((ch, d), ys.dtype)] * 2
                       + [pltpu.SemaphoreType.DMA] * 4))
    def gather_rows(ys_hbm, pos_hbm, out_hbm, idx0, idx1, rows0, rows1, g0, g1, w0, w1):
        idx, rows, gsem, wsem = (idx0, idx1), (rows0, rows1), (g0, g1), (w0, w1)
        base = (lax.axis_index("s") * n_cores + lax.axis_index("c")) * tpw
        items = [(i, k) for i in range(tpw // ch) for k in range(TOP_K)]

        def gather(n):
            i, k = items[n]
            pltpu.sync_copy(pos_hbm.at[k, pl.ds(base + i * ch, ch)], idx[n % 2])
            return pltpu.async_copy(ys_hbm.at[idx[n % 2]], rows[n % 2], gsem[n % 2])

        def write(n):
            i, k = items[n]
            return pltpu.async_copy(rows[n % 2], out_hbm.at[k, pl.ds(base + i * ch, ch)], wsem[n % 2])

        gathers, writes = {}, {}
        for n in range(len(items)):
            if n >= 2:
                writes[n - 2].wait()
            gathers[n] = gather(n)
            if n >= 1:
                gathers[n - 1].wait()
                writes[n - 1] = write(n - 1)
        last = len(items) - 1
        gathers[last].wait()
        writes[last] = write(last)
        if last >= 1:
            writes[last - 1].wait()
        writes[last].wait()

    return gather_rows(ys, pos8)


def _pack_bf16_pairs(a):
    return _pack_rounded(a.astype(BF16).astype(F32))


def _pack_rounded(r):
    n = r.shape[1] // 2
    lo = pltpu.bitcast(r[:, :n], jnp.int32)
    hi = pltpu.bitcast(r[:, n:], jnp.int32)
    return jnp.bitwise_or(jnp.bitwise_and(hi, -65536), jnp.bitwise_and(lax.shift_right_logical(lo, 16), 65535))


def _unpack_bf16_pairs(p):
    lo = pltpu.bitcast(lax.shift_left(p, 16), F32)
    hi = pltpu.bitcast(jnp.bitwise_and(p, -65536), F32)
    return lo, hi


def _expert_kernel(meta_ref, xs_ref, wup_ref, bup_ref, wdn_ref, bdn_ref, ys_ref,
                   wup_s, wdn_s, xbuf, obuf, in_sem, out_sem):
    e = pl.program_id(0)
    tm = TM_MOE
    half = D_MODEL // 2
    pad = meta_ref[META_PAD, e]
    n_t = pad // tm
    row0 = meta_ref[META_END, e] - pad
    cnt = meta_ref[META_CNT, e]

    def rows(i):
        return pl.ds(pl.multiple_of(row0 + i * tm, tm), tm)

    def x_copy(i, slot):
        return pltpu.make_async_copy(xs_ref.at[rows(i)], xbuf.at[slot], in_sem.at[slot])

    def y_copy(i, slot):
        return pltpu.make_async_copy(obuf.at[slot], ys_ref.at[rows(i)], out_sem.at[slot])

    @pl.when(n_t > 0)
    def _():
        x_copy(0, 0).start()
        wup_s[...] = wup_ref[0, 0].astype(BF16)
        wdn_s[...] = wdn_ref[0, 0].astype(BF16)

        def tile(i, carry):
            slot = lax.rem(i, 2)
            x_copy(i, slot).wait()

            @pl.when(i + 1 < n_t)
            def _():
                x_copy(i + 1, 1 - slot).start()

            @pl.when(i >= 2)
            def _():
                y_copy(i - 2, slot).wait()

            row = lax.broadcasted_iota(jnp.int32, (tm, 1), 0)
            lo, hi = _unpack_bf16_pairs(jnp.where(row < cnt - i * tm, xbuf[slot], 0))
            z = (jnp.dot(lo.astype(BF16), wup_s[0:half, :], preferred_element_type=F32)
                 + jnp.dot(hi.astype(BF16), wup_s[half:, :], preferred_element_type=F32) + bup_ref[0, 0])
            glu = jnp.minimum(z[:, :D_EXPERT], SWIGLU_LIMIT)
            lin = jnp.clip(z[:, D_EXPERT:], -SWIGLU_LIMIT, SWIGLU_LIMIT)
            act = (glu * _sigmoid(SWIGLU_ALPHA * glu) * (lin + 1.0)).astype(BF16)
            y = jnp.dot(act, wdn_s[...], preferred_element_type=F32) + bdn_ref[0, 0]
            obuf[slot] = _pack_bf16_pairs(y)
            y_copy(i, slot).start()
            return carry

        lax.fori_loop(0, n_t, tile, 0)

        @pl.when(n_t >= 2)
        def _():
            y_copy(n_t - 2, lax.rem(n_t, 2)).wait()

        y_copy(n_t - 1, lax.rem(n_t - 1, 2)).wait()

    @pl.when(e == N_EXPERTS - 1)
    def _():
        obuf[0] = jnp.zeros((tm, half), jnp.int32)

        def fill(i, carry):
            cp = pltpu.make_async_copy(obuf.at[0], ys_ref.at[pl.ds(pl.multiple_of(i * tm, tm), tm)], out_sem.at[0])
            cp.start()
            cp.wait()
            return carry

        lax.fori_loop(meta_ref[META_END, N_EXPERTS - 1] // tm, ys_ref.shape[0] // tm, fill, 0)


def _expert_mlp(xs, meta, layer, w_up, b_up, w_down, b_down):
    n_rows, half = xs.shape
    d = 2 * half
    tm = TM_MOE
    f = w_up.shape[-1]
    b_up4 = b_up.reshape(DEPTH, N_EXPERTS, 1, f)
    b_dn4 = b_down.reshape(DEPTH, N_EXPERTS, 1, d)
    wsel = lambda e, m: (layer, e, 0, 0)
    grid_spec = pltpu.PrefetchScalarGridSpec(
        num_scalar_prefetch=1,
        grid=(N_EXPERTS,),
        in_specs=[pl.BlockSpec(memory_space=pl.ANY),
                  pl.BlockSpec((1, 1, d, f), wsel),
                  pl.BlockSpec((1, 1, 1, f), wsel),
                  pl.BlockSpec((1, 1, f // 2, d), wsel),
                  pl.BlockSpec((1, 1, 1, d), wsel)],
        out_specs=pl.BlockSpec(memory_space=pl.ANY),
        scratch_shapes=[pltpu.VMEM((d, f), BF16), pltpu.VMEM((f // 2, d), BF16),
                        pltpu.VMEM((2, tm, half), jnp.int32), pltpu.VMEM((2, tm, half), jnp.int32),
                        pltpu.SemaphoreType.DMA((2,)), pltpu.SemaphoreType.DMA((2,))],
    )
    return pl.pallas_call(
        _expert_kernel,
        grid_spec=grid_spec,
        out_shape=jax.ShapeDtypeStruct((n_rows, half), jnp.int32),
        compiler_params=_cparams(("arbitrary",)),
        name="expert_mlp",
    )(meta, xs, w_up, b_up4, w_down, b_dn4)


def _combine_kernel(rows_ref, prob_ref, x_ref, gate_ref, lng_ref, lnb_ref, o_ref):
    p = prob_ref[...]
    y = None
    for k in range(TOP_K):
        yk = p[:, k:k + 1] * jnp.concatenate(_unpack_bf16_pairs(rows_ref[k]), axis=1)
        y = yk if y is None else y + yk
    o_ref[0] = _layer_norm(DN_ALPHA * x_ref[0] + (1.0 + gate_ref[0]) * y, lng_ref[...], lnb_ref[...])


def _combine(rows, prob_c, x1, gate, ln_g, ln_b, part):
    bsz, s, d = x1.shape
    tm = TM_DISP
    nt = s // tm
    pb = bsz // MOE_PARTS
    b0 = part * pb
    return pl.pallas_call(
        _combine_kernel,
        grid=(pb, nt),
        in_specs=[pl.BlockSpec((TOP_K, tm, d // 2), lambda b, i: (0, b * nt + i, 0)),
                  pl.BlockSpec((tm, LANES), lambda b, i: ((b0 + b) * nt + i, 0)),
                  pl.BlockSpec((1, tm, d), lambda b, i: (b0 + b, i, 0)),
                  pl.BlockSpec((1, 1, d), lambda b, i: (b0 + b, 0, 0)),
                  pl.BlockSpec((1, d), lambda b, i: (0, 0)),
                  pl.BlockSpec((1, d), lambda b, i: (0, 0))],
        out_specs=pl.BlockSpec((1, tm, d), lambda b, i: (b0 + b, i, 0)),
        out_shape=jax.ShapeDtypeStruct((bsz, s, d), F32),
        input_output_aliases={2: 0},
        compiler_params=_cparams(("arbitrary", "arbitrary")),
        name="combine",
    )(rows, prob_c, x1, gate.reshape(bsz, 1, d), ln_g.reshape(1, d), ln_b.reshape(1, d))


def kernel(x, c, positions, ada_w, ada_b, w_in, lam_q1, lam_k1, lam_q2, lam_k2, da_norm_g, ml_conv_w, ml_conv_b,
           ml_w_q, ml_w_k, ml_gate_b, ml_norm_g, s5_a_re, s5_a_im, s5_log_dt, s5_b_re, s5_b_im, s5_c_re, s5_c_im,
           s5_d, s5_w_glu, w_out, ln_g, ln_b, w_router, b_router, w_up, b_up, w_down, b_down):
    bsz, s, d = x.shape
    t = bsz * s
    n_tiles_max = (t * TOP_K) // TM_MOE + N_EXPERTS
    n_rows = n_tiles_max * TM_MOE
    mod = _modulation(c, ada_w, ada_b)
    cos_t, sin_t = _rope_tables(positions)
    for l in range(DEPTH):
        shift, scale, gate = jnp.split(mod[2 * l], 3, axis=-1)
        q, k, v, mlx, mlv, mlo, g_t, g_c, s5u = _in_proj(x, shift, scale, cos_t, sin_t, w_in[l])
        lam_init = 0.8 - 0.6 * math.exp(-0.3 * l)
        lamv = jnp.stack([lam_q1[l], lam_k1[l], lam_q2[l], lam_k2[l]])
        y_da = _diff_attn(q, k, v, lamv, da_norm_g[l], lam_init)
        y_ml = _mlstm(mlx, mlv, mlo, g_t, g_c, ml_conv_w[l], ml_conv_b[l], ml_w_q[l], ml_w_k[l],
                      ml_gate_b[l], ml_norm_g[l])
        s5p = _s5_params(s5_a_re[l], s5_a_im[l], s5_log_dt[l], s5_b_re[l], s5_b_im[l], s5_c_re[l], s5_c_im[l],
                         s5_w_glu[l])
        y_s5 = _s5(s5u, bsz, s5p, s5_d[l])
        shift2, scale2, gate2 = jnp.split(mod[2 * l + 1], 3, axis=-1)
        x1, h2, eid, prob, counts = _out_proj(y_da, y_ml, y_s5, x, gate, ln_g[l, 0], ln_b[l, 0], shift2, scale2,
                                      w_out[l], w_router[l], b_router[l])
        pos8, meta = _route(eid, counts)
        xs = _dispatch(h2, pos8, n_rows)
        ys = _expert_mlp(xs, meta, l, w_up, b_up, w_down, b_down)
        tp = t // MOE_PARTS
        x = x1
        for part in range(MOE_PARTS):
            rows = _gather_expert_rows(ys, pos8[:, part * tp:(part + 1) * tp])
            x = _combine(rows, prob, x, gate2, ln_g[l, 1], ln_b[l, 1], part)
    return x
```

```python
import functools
import math

import jax
import jax.numpy as jnp
from jax import lax
from jax.experimental import pallas as pl
from jax.experimental.pallas import tpu as pltpu
from jax.experimental.pallas import tpu_sc as plsc

F32 = jnp.float32
BF16 = jnp.bfloat16
HIGHEST = lax.Precision.HIGHEST

D_MODEL = 1024
DEPTH = 2
DA_HEADS = 4
DA_HEAD_DIM = 64
DA_V_DIM = 2 * DA_HEAD_DIM
DA_WIDTH = DA_HEADS * DA_V_DIM
DA_QK_WIDTH = DA_HEADS * 2 * DA_HEAD_DIM
ROPE_THETA = 10000.0
ML_HEADS = 4
ML_HEAD_DIM = 64
ML_WIDTH = ML_HEADS * ML_HEAD_DIM
ML_CONV = 4
S5_GROUP = 16
S5_STATE = 64
S5_WIDTH = D_MODEL - DA_WIDTH - ML_WIDTH
S5_GROUPS = S5_WIDTH // S5_GROUP
S5_NSTATE = S5_GROUPS * S5_STATE
N_EXPERTS = 32
TOP_K = 4
D_EXPERT = D_MODEL
SWIGLU_LIMIT = 7.0
SWIGLU_ALPHA = 1.702
DN_ALPHA = (2 * DEPTH) ** 0.25
LN_EPS = 1e-5
NEG = -1e30

OFF_DA_K = DA_QK_WIDTH
OFF_DA_V = 2 * DA_QK_WIDTH
OFF_ML_X = OFF_DA_V + DA_WIDTH
OFF_ML_V = OFF_ML_X + ML_WIDTH
OFF_ML_O = OFF_ML_V + ML_WIDTH
OFF_ML_I = OFF_ML_O + ML_WIDTH
OFF_ML_F = OFF_ML_I + ML_HEADS
OFF_S5_U = OFF_ML_F + ML_HEADS
N_IN = OFF_S5_U + S5_WIDTH

LANES = 128
SUBLANES = 8
VMEM_LIMIT_BYTES = 56 * 1024 * 1024

TM_PROJ = 1024
TQ = 512
ML_CHUNK = 256
ML_NB = 4
S5_TC = 256
S5_UNROLL = 8
TB_RANK = 1024
TM_MOE = 256
TM_DISP = 512
MOE_PARTS = 2
SC_SCATTER_CHUNK = 128
SC_GATHER_CHUNK = 64
GATE_PAD = 8
VT_ROWS = DA_V_DIM + 16
Q_PRESCALE = DA_HEAD_DIM ** -0.5 * math.log2(math.e)


def _cparams(sem, vmem=VMEM_LIMIT_BYTES):
    return pltpu.CompilerParams(dimension_semantics=sem, vmem_limit_bytes=vmem)


def _sigmoid(x):
    return 1.0 / (1.0 + jnp.exp(-x))


def _mod_kernel(c_ref, w_ref, b_ref, o_ref):
    c = c_ref[...]
    ca = (c * _sigmoid(c)).astype(BF16)
    w = w_ref[0, 0].astype(BF16)
    o_ref[0] = jnp.dot(ca, w, preferred_element_type=F32) + b_ref[0]


def _modulation(c, ada_w, ada_b):
    nsub = ada_w.shape[1]
    nmod = ada_w.shape[0] * nsub
    bsz, d = c.shape
    e = ada_w.shape[-1]
    tn = 1024
    b = ada_b.reshape(nmod, 1, e)
    return pl.pallas_call(
        _mod_kernel,
        grid=(nmod, e // tn),
        in_specs=[pl.BlockSpec((bsz, d), lambda n, j: (0, 0)),
                  pl.BlockSpec((1, 1, d, tn), lambda n, j: (n // nsub, n % nsub, 0, j)),
                  pl.BlockSpec((1, 1, tn), lambda n, j: (n, 0, j))],
        out_specs=pl.BlockSpec((1, bsz, tn), lambda n, j: (n, 0, j)),
        out_shape=jax.ShapeDtypeStruct((nmod, bsz, e), F32),
        compiler_params=_cparams(("arbitrary", "arbitrary")),
        name="modulation",
    )(c, ada_w, b)


def _rope_kernel(pos_ref, cos_ref, sin_ref):
    nfreq = DA_HEAD_DIM // 2
    pos = pos_ref[0].astype(F32)
    fidx = lax.broadcasted_iota(jnp.int32, (nfreq, 1), 0).astype(F32)
    inv = jnp.exp(fidx * (-2.0 * math.log(ROPE_THETA) / DA_HEAD_DIM))
    ang = inv * pos
    reps = LANES // nfreq
    cos_t = jnp.concatenate([jnp.cos(ang)] * reps, axis=0).T
    sin_t = jnp.concatenate([jnp.sin(ang)] * reps, axis=0).T
    lane = lax.broadcasted_iota(jnp.int32, (1, LANES), 1)
    sign = jnp.where((lane % DA_HEAD_DIM) < nfreq, -1.0, 1.0)
    cos_ref[0] = cos_t
    sin_ref[0] = sin_t * sign


def _rope_tables(positions):
    bsz, s = positions.shape
    ts = 512
    pos3 = positions.reshape(bsz, 1, s)
    return pl.pallas_call(
        _rope_kernel,
        grid=(bsz, s // ts),
        in_specs=[pl.BlockSpec((1, 1, ts), lambda b, i: (b, 0, i))],
        out_specs=[pl.BlockSpec((1, ts, LANES), lambda b, i: (b, i, 0))] * 2,
        out_shape=[jax.ShapeDtypeStruct((bsz, s, LANES), F32)] * 2,
        compiler_params=_cparams(("arbitrary", "arbitrary")),
        name="rope_tables",
    )(pos3)


def _in_proj_kernel(x_ref, shift_ref, scale_ref, cos_ref, sin_ref, wqk_ref, wvt_ref, wrest_ref, wgt_ref, wgc_ref,
                    q_ref, k_ref, vt_ref, mlx_ref, mlv_ref, mlo_ref, gt_ref, gc_ref, s5u_ref):
    h = (x_ref[0] * (1.0 + scale_ref[0]) + shift_ref[0]).astype(BF16)
    cos = cos_ref[0]
    sin = sin_ref[0]
    lane = lax.broadcasted_iota(jnp.int32, (1, LANES), 1)
    lo_half = (lane % DA_HEAD_DIM) < DA_HEAD_DIM // 2
    half = DA_HEAD_DIM // 2

    def rope(t):
        fwd = pltpu.roll(t, half, 1)
        bwd = pltpu.roll(t, LANES - half, 1)
        partner = jnp.where(lo_half, bwd, fwd)
        return t * cos + partner * sin

    qk = jnp.dot(h, wqk_ref[...], preferred_element_type=F32)
    nslab = DA_QK_WIDTH // LANES
    for c in range(nslab):
        q_ref[0, :, c * LANES:(c + 1) * LANES] = (
            rope(qk[:, c * LANES:(c + 1) * LANES]) * Q_PRESCALE).astype(BF16)
        k_ref[0, :, c * LANES:(c + 1) * LANES] = rope(
            qk[:, DA_QK_WIDTH + c * LANES:DA_QK_WIDTH + (c + 1) * LANES]).astype(BF16)

    vt = lax.dot_general(wvt_ref[...], h, (((1,), (1,)), ((), ())), preferred_element_type=F32)
    tm = h.shape[0]
    for hh in range(DA_HEADS):
        for jj in range(tm // TQ):
            vt_ref[0, hh, jj, 0:DA_V_DIM, :] = vt[hh * DA_V_DIM:(hh + 1) * DA_V_DIM,
                                                  jj * TQ:(jj + 1) * TQ].astype(BF16)
            vt_ref[0, hh, jj, DA_V_DIM:VT_ROWS, :] = jnp.ones((VT_ROWS - DA_V_DIM, TQ), BF16)

    r = jnp.dot(h, wrest_ref[...], preferred_element_type=F32)
    o = 0
    mlx_ref[0] = r[:, o:o + ML_WIDTH].astype(BF16); o += ML_WIDTH
    mlv_ref[0] = r[:, o:o + ML_WIDTH].astype(BF16); o += ML_WIDTH
    mlo_ref[0] = r[:, o:o + ML_WIDTH].astype(BF16); o += ML_WIDTH
    s5u_ref[...] = r[:, o:o + S5_WIDTH]
    gt_ref[0] = lax.dot_general(wgt_ref[...], h, (((1,), (1,)), ((), ())), preferred_element_type=F32)
    gc_ref[0] = jnp.dot(h, wgc_ref[...], preferred_element_type=F32)


def _in_proj(x, shift, scale, cos_t, sin_t, w_in_l):
    bsz, s, d = x.shape
    tm = TM_PROJ
    w = w_in_l.astype(BF16)
    wqk = w[:, :OFF_DA_V]
    wvt = w[:, OFF_DA_V:OFF_ML_X].T
    wrest = jnp.concatenate([w[:, OFF_ML_X:OFF_ML_I], w[:, OFF_S5_U:]], axis=1)
    wg = w[:, OFF_ML_I:OFF_S5_U]
    wgt = wg.T
    wgc = jnp.pad(wg, ((0, 0), (0, LANES - GATE_PAD)))
    nrest = wrest.shape[1]
    shift3 = shift.reshape(bsz, 1, d)
    scale3 = scale.reshape(bsz, 1, d)
    tok = lambda b, i: (b, i, 0)
    per_b = lambda b, i: (b, 0, 0)
    const2 = lambda b, i: (0, 0)
    out_shapes = [
        jax.ShapeDtypeStruct((bsz, s, DA_QK_WIDTH), BF16),
        jax.ShapeDtypeStruct((bsz, s, DA_QK_WIDTH), BF16),
        jax.ShapeDtypeStruct((bsz, DA_HEADS, s // TQ, VT_ROWS, TQ), BF16),
        jax.ShapeDtypeStruct((bsz, s, ML_WIDTH), BF16),
        jax.ShapeDtypeStruct((bsz, s, ML_WIDTH), BF16),
        jax.ShapeDtypeStruct((bsz, s, ML_WIDTH), BF16),
        jax.ShapeDtypeStruct((bsz, GATE_PAD, s), F32),
        jax.ShapeDtypeStruct((bsz, s, LANES), F32),
        jax.ShapeDtypeStruct((s, bsz * S5_WIDTH), F32),
    ]
    out_specs = [
        pl.BlockSpec((1, tm, DA_QK_WIDTH), tok),
        pl.BlockSpec((1, tm, DA_QK_WIDTH), tok),
        pl.BlockSpec((1, DA_HEADS, tm // TQ, VT_ROWS, TQ), lambda b, i: (b, 0, i, 0, 0)),
        pl.BlockSpec((1, tm, ML_WIDTH), tok),
        pl.BlockSpec((1, tm, ML_WIDTH), tok),
        pl.BlockSpec((1, tm, ML_WIDTH), tok),
        pl.BlockSpec((1, GATE_PAD, tm), lambda b, i: (b, 0, i)),
        pl.BlockSpec((1, tm, LANES), tok),
        pl.BlockSpec((tm, S5_WIDTH), lambda b, i: (i, b)),
    ]
    return pl.pallas_call(
        _in_proj_kernel,
        grid=(bsz, s // tm),
        in_specs=[pl.BlockSpec((1, tm, d), tok),
                  pl.BlockSpec((1, 1, d), per_b),
                  pl.BlockSpec((1, 1, d), per_b),
                  pl.BlockSpec((1, tm, LANES), tok),
                  pl.BlockSpec((1, tm, LANES), tok),
                  pl.BlockSpec((d, OFF_DA_V), const2),
                  pl.BlockSpec((DA_WIDTH, d), const2),
                  pl.BlockSpec((d, nrest), const2),
                  pl.BlockSpec((GATE_PAD, d), const2),
                  pl.BlockSpec((d, LANES), const2)],
        out_specs=out_specs,
        out_shape=out_shapes,
        compiler_params=_cparams(("arbitrary", "arbitrary")),
        name="in_proj",
    )(x, shift3, scale3, cos_t, sin_t, wqk, wvt, wrest, wgt, wgc)


def _diff_attn_kernel(lam_init, lamv_ref, gain_ref, q_ref, k_ref, vt_ref, o_ref, acc_s, m_s):
    qi = pl.program_id(2)
    tq = q_ref.shape[1]
    lane = lax.broadcasted_iota(jnp.int32, (1, LANES), 1)
    first = lane < DA_HEAD_DIM
    q = q_ref[0]
    zero = jnp.zeros_like(q)
    qm = (jnp.where(first, q, zero), jnp.where(first, zero, q))
    acc_s[...] = jnp.zeros_like(acc_s)
    m_s[...] = jnp.full(m_s.shape, NEG, F32)

    def step(j, nblk, masked):
        tk = nblk * tq
        kb = k_ref[0, pl.ds(pl.multiple_of(j * tq, tq), tk), :]
        vtb = vt_ref[0, 0, j] if nblk == 1 else jnp.concatenate([vt_ref[0, 0, j + b] for b in range(nblk)], axis=1)
        for c in range(2):
            st = lax.dot_general(kb, qm[c], (((1,), (1,)), ((), ())), preferred_element_type=F32)
            if masked:
                key_i = lax.broadcasted_iota(jnp.int32, (tk, tq), 0) - (tk - tq)
                qry_i = lax.broadcasted_iota(jnp.int32, (tk, tq), 1)
                st = jnp.where(key_i <= qry_i, st, NEG)
            m_prev = m_s[c]
            m_new = jnp.maximum(m_prev, jnp.max(st, axis=0, keepdims=True))
            alpha = jnp.exp2(m_prev - m_new)
            p = jnp.exp2(st - m_new).astype(BF16)
            acc_s[c] = alpha * acc_s[c] + jnp.dot(vtb, p, preferred_element_type=F32)
            m_s[c] = m_new

    def body(jj, carry):
        step(2 * jj, 2, False)
        return carry

    lax.fori_loop(0, qi // 2, body, 0)

    @pl.when(qi % 2 == 1)
    def _():
        step(qi - 1, 2, True)

    @pl.when(qi % 2 == 0)
    def _():
        step(qi, 1, True)

    outs = []
    for c in range(2):
        acc = acc_s[c]
        outs.append(acc[:DA_V_DIM] / acc[DA_V_DIM:DA_V_DIM + 1])

    lamv = lamv_ref[...]
    lam = (jnp.exp(jnp.sum(lamv[0:1] * lamv[1:2], axis=1, keepdims=True))
           - jnp.exp(jnp.sum(lamv[2:3] * lamv[3:4], axis=1, keepdims=True)) + lam_init)
    ot = outs[0] - lam * outs[1]
    ms = jnp.mean(ot * ot, axis=0, keepdims=True)
    ot = ot * (lax.rsqrt(ms + LN_EPS) * (1.0 - lam_init))
    o_ref[0] = (ot.T * gain_ref[...]).astype(o_ref.dtype)


def _diff_attn(q, k, vt, lamv, gain, lam_init):
    bsz, s, _ = q.shape
    tq = TQ
    nq = s // tq
    return pl.pallas_call(
        functools.partial(_diff_attn_kernel, lam_init),
        grid=(bsz, DA_HEADS, nq),
        in_specs=[pl.BlockSpec((4, DA_HEAD_DIM), lambda b, h, i: (0, 0)),
                  pl.BlockSpec((1, DA_V_DIM), lambda b, h, i: (0, 0)),
                  pl.BlockSpec((1, tq, DA_V_DIM), lambda b, h, i: (b, i, h)),
                  pl.BlockSpec((1, s, DA_V_DIM), lambda b, h, i: (b, 0, h)),
                  pl.BlockSpec((1, 1, nq, VT_ROWS, tq), lambda b, h, i: (b, h, 0, 0, 0))],
        out_specs=pl.BlockSpec((1, tq, DA_V_DIM), lambda b, h, i: (b, i, h)),
        out_shape=jax.ShapeDtypeStruct((bsz, s, DA_WIDTH), BF16),
        scratch_shapes=[pltpu.VMEM((2, VT_ROWS, tq), F32), pltpu.VMEM((2, 1, tq), F32)],
        compiler_params=_cparams(("arbitrary", "arbitrary", "arbitrary")),
        name="diff_attn",
    )(lamv, gain.reshape(1, DA_V_DIM), q, k, vt)


def _log_sigmoid(x):
    return jnp.minimum(x, 0.0) - jnp.log(1.0 + jnp.exp(-jnp.abs(x)))


def _split3(a):
    hi = a.astype(BF16)
    r1 = a - hi.astype(F32)
    mid = r1.astype(BF16)
    lo = (r1 - mid.astype(F32)).astype(BF16)
    return hi, mid, lo


def _mlstm_kernel(x_ref, v_ref, o_ref, gt_ref, gc_ref, cw_ref, cb_ref, wq_ref, wkt_ref, gbt_ref, gbc_ref,
                  ng_ref, hmean_ref, y_ref, xc_s, c_s, m_s):
    nb, s = x_ref.shape[0], x_ref.shape[1]
    L = ML_CHUNK
    H, dh = ML_HEADS, ML_HEAD_DIM
    nc = s // L
    cw = cw_ref[...]
    row = lax.broadcasted_iota(jnp.int32, (s, 1), 0)
    for bi in range(nb):
        x = x_ref[bi].astype(F32)
        xc = x * cw[ML_CONV - 1:ML_CONV]
        for j in range(1, ML_CONV):
            xs = jnp.where(row >= j, pltpu.roll(x, j, 0), 0.0)
            xc = xc + xs * cw[ML_CONV - 1 - j:ML_CONV - j]
        xc = xc + cb_ref[...]
        xc_s[bi] = (xc * _sigmoid(xc)).astype(BF16)

    c_s[...] = jnp.zeros_like(c_s)
    m_s[...] = jnp.full(m_s.shape, NEG, F32)

    ri = lax.broadcasted_iota(jnp.int32, (L, L), 0)
    ci = lax.broadcasted_iota(jnp.int32, (L, L), 1)
    causal = ci <= ri
    tril = causal.astype(BF16)
    triu = (ri <= ci).astype(BF16)
    lane = lax.broadcasted_iota(jnp.int32, (1, dh), 1)
    one_hot0 = jnp.broadcast_to((lane == 0).astype(BF16), (L, dh))

    def chunk_one(bi, ci_, t0):
        xcc = xc_s[bi, pl.ds(t0, L), :]
        qc = jnp.dot(xcc, wq_ref[...], preferred_element_type=F32).astype(BF16)
        ktc = lax.dot_general(wkt_ref[...], xcc, (((1,), (1,)), ((), ())),
                              preferred_element_type=F32)
        g_rows = gt_ref[bi, ci_] + gbt_ref[...]
        g_cols = gc_ref[bi, pl.ds(t0, L), :] + gbc_ref[...]
        lf_rows = _log_sigmoid(g_rows)
        lf_cols = _log_sigmoid(g_cols)
        r3 = jnp.dot(jnp.concatenate(_split3(lf_rows), axis=0), triu, preferred_element_type=F32)
        b_rows = r3[0:GATE_PAD] + r3[GATE_PAD:2 * GATE_PAD] + r3[2 * GATE_PAD:]
        c3 = jnp.dot(tril, jnp.concatenate(_split3(lf_cols), axis=1), preferred_element_type=F32)
        b_cols = c3[:, 0:LANES] + c3[:, LANES:2 * LANES] + c3[:, 2 * LANES:]
        vch = v_ref[bi, pl.ds(t0, L), :]
        och = o_ref[bi, pl.ds(t0, L), :].astype(F32)
        hs = []
        for h in range(H):
            br = b_rows[H + h:H + h + 1, :]
            ir = g_rows[h:h + 1, :]
            bc = b_cols[:, H + h:H + h + 1]
            m_prev = m_s[bi, h]
            log_d = jnp.where(causal, bc - br + ir, NEG)
            inter = bc + m_prev
            mx = jnp.maximum(inter, jnp.max(log_d, axis=1, keepdims=True))
            dmat = jnp.exp(log_d - mx)
            dec = jnp.exp(inter - mx)
            qh = qc[:, h * dh:(h + 1) * dh]
            kth = ktc[h * dh:(h + 1) * dh, :]
            vaug = jnp.concatenate([vch[:, h * dh:(h + 1) * dh], one_hot0], axis=1)
            sm = (jnp.dot(qh, kth.astype(BF16), preferred_element_type=F32) * dmat).astype(BF16)
            c_prev = c_s[bi, h]
            na = (jnp.dot(sm, vaug, preferred_element_type=F32)
                  + dec * jnp.dot(qh, c_prev.astype(BF16), preferred_element_type=F32))
            den = na[:, dh:dh + 1]
            hs.append(na[:, :dh] / jnp.maximum(jnp.abs(den), jnp.exp(-mx)))
            g_tot = br[:, L - 1:L]
            a_row = g_tot - br + ir
            m_new = jnp.maximum(g_tot + m_prev, jnp.max(a_row, axis=1, keepdims=True))
            decay = jnp.exp(g_tot + m_prev - m_new)
            w_row = jnp.exp(a_row - m_new)
            kw = (kth * w_row).astype(BF16)
            c_s[bi, h] = decay * c_prev + jnp.dot(kw, vaug, preferred_element_type=F32)
            m_s[bi, h] = m_new
        hcat = jnp.concatenate(hs, axis=1)
        m3 = jnp.dot(jnp.concatenate(_split3(hcat * hcat), axis=0), hmean_ref[...],
                     preferred_element_type=F32)
        ms = m3[0:L] + m3[L:2 * L] + m3[2 * L:]
        y = hcat * lax.rsqrt(ms + LN_EPS) * ng_ref[...] * _sigmoid(och)
        y_ref[bi, pl.ds(t0, L), :] = y.astype(y_ref.dtype)

    def chunk(ci_, _):
        t0 = pl.multiple_of(ci_ * L, L)
        for bi in range(nb):
            chunk_one(bi, ci_, t0)
        return 0

    lax.fori_loop(0, nc, chunk, 0)


def _mlstm(mlx, mlv, mlo, g_t, g_c, conv_w, conv_b, w_q, w_k, gate_b, norm_g):
    bsz, s, _ = mlx.shape
    H, dh = ML_HEADS, ML_HEAD_DIM
    eye = jnp.eye(H, dtype=F32)
    wq_bd = jnp.einsum('hde,hg->hdge', w_q, eye).reshape(ML_WIDTH, ML_WIDTH).astype(BF16)
    wk_bd = jnp.einsum('hde,hg->hdge', w_k * (dh ** -0.5), eye).reshape(ML_WIDTH, ML_WIDTH)
    wkt_bd = wk_bd.T.astype(BF16)
    gbt = gate_b.reshape(GATE_PAD, 1)
    gbc = jnp.pad(gate_b.reshape(1, GATE_PAD), ((0, 0), (0, LANES - GATE_PAD)))
    hmean = jnp.kron(eye, jnp.full((dh, dh), 1.0 / dh, F32)).astype(BF16)
    nc = s // ML_CHUNK
    g_t4 = g_t.reshape(bsz, GATE_PAD, nc, ML_CHUNK).transpose(0, 2, 1, 3)
    tok = lambda b: (b, 0, 0)
    c2 = lambda b: (0, 0)
    nb = ML_NB
    return pl.pallas_call(
        _mlstm_kernel,
        grid=(bsz // nb,),
        in_specs=[pl.BlockSpec((nb, s, ML_WIDTH), tok),
                  pl.BlockSpec((nb, s, ML_WIDTH), tok),
                  pl.BlockSpec((nb, s, ML_WIDTH), tok),
                  pl.BlockSpec((nb, nc, GATE_PAD, ML_CHUNK), lambda b: (b, 0, 0, 0)),
                  pl.BlockSpec((nb, s, LANES), tok),
                  pl.BlockSpec((ML_CONV, ML_WIDTH), c2),
                  pl.BlockSpec((1, ML_WIDTH), c2),
                  pl.BlockSpec((ML_WIDTH, ML_WIDTH), c2),
                  pl.BlockSpec((ML_WIDTH, ML_WIDTH), c2),
                  pl.BlockSpec((GATE_PAD, 1), c2),
                  pl.BlockSpec((1, LANES), c2),
                  pl.BlockSpec((1, ML_WIDTH), c2),
                  pl.BlockSpec((ML_WIDTH, ML_WIDTH), c2)],
        out_specs=pl.BlockSpec((nb, s, ML_WIDTH), tok),
        out_shape=jax.ShapeDtypeStruct((bsz, s, ML_WIDTH), BF16),
        scratch_shapes=[pltpu.VMEM((nb, s, ML_WIDTH), BF16),
                        pltpu.VMEM((nb, H, dh, LANES), F32),
                        pltpu.VMEM((nb, H, 1, 1), F32)],
        compiler_params=_cparams(("arbitrary",)),
        name="mlstm",
    )(mlx, mlv, mlo, g_t4, g_c, conv_w, conv_b.reshape(1, ML_WIDTH), wq_bd, wkt_bd, gbt, gbc,
      norm_g.reshape(1, ML_WIDTH), hmean)


def _gelu_tanh(x):
    return 0.5 * x * (1.0 + jnp.tanh(math.sqrt(2.0 / math.pi) * (x + 0.044715 * (x * x * x))))


def _s5_kernel(u_ref, are_ref, aim_ref, bcat_ref, ccat_ref, d_ref, wglu_ref, y_ref, xs_s, st_s):
    tc, bsz, w = u_ref.shape
    n = S5_NSTATE

    @pl.when(pl.program_id(0) == 0)
    def _():
        st_s[...] = jnp.zeros_like(st_s)

    u = u_ref[...].reshape(tc * bsz, w)
    xs_s[...] = jnp.dot(u.astype(BF16), bcat_ref[...], preferred_element_type=F32).reshape(tc, bsz, 2 * n)
    a_re = jnp.broadcast_to(are_ref[...], (bsz, n))
    a_im = jnp.broadcast_to(aim_ref[...], (bsz, n))

    def step(t, carry):
        x_re, x_im = carry
        bu = xs_s[t]
        n_re = a_re * x_re - a_im * x_im + bu[:, :n]
        n_im = a_re * x_im + a_im * x_re + bu[:, n:]
        xs_s[t] = jnp.concatenate([n_re, n_im], axis=1)
        return n_re, n_im

    x_re, x_im = lax.fori_loop(0, tc, step, (st_s[0], st_s[1]), unroll=S5_UNROLL)
    st_s[0] = x_re
    st_s[1] = x_im

    xs = xs_s[...].reshape(tc * bsz, 2 * n).astype(BF16)
    y = jnp.dot(xs, ccat_ref[...], preferred_element_type=F32) + d_ref[...] * u
    z = jnp.dot(_gelu_tanh(y).astype(BF16), wglu_ref[...], preferred_element_type=F32)
    out = z[:, :w] * _sigmoid(z[:, w:])
    y_ref[...] = out.reshape(tc, bsz, w).astype(y_ref.dtype)


def _s5_params(a_re, a_im, log_dt, b_re, b_im, c_re, c_im, w_glu):
    G, P, Hc = S5_GROUPS, S5_STATE, S5_GROUP
    dt = jnp.exp(log_dt)[:, None]
    mag = jnp.exp(a_re * dt)
    ab_re = mag * jnp.cos(a_im * dt)
    ab_im = mag * jnp.sin(a_im * dt)
    nr, ni = ab_re - 1.0, ab_im
    den = a_re * a_re + a_im * a_im
    fr = (nr * a_re + ni * a_im) / den
    fi = (ni * a_re - nr * a_im) / den
    bb_re = fr[..., None] * b_re - fi[..., None] * b_im
    bb_im = fr[..., None] * b_im + fi[..., None] * b_re
    eye = jnp.eye(G, dtype=F32)
    bd = lambda t, sub: jnp.einsum(sub, t, eye)
    bre = bd(bb_re, 'gph,gk->ghkp').reshape(G * Hc, G * P)
    bim = bd(bb_im, 'gph,gk->ghkp').reshape(G * Hc, G * P)
    bcat = jnp.concatenate([bre, bim], axis=1).astype(BF16)
    cre = bd(c_re, 'ghp,gk->gpkh').reshape(G * P, G * Hc)
    cim = bd(c_im, 'ghp,gk->gpkh').reshape(G * P, G * Hc)
    ccat = jnp.concatenate([cre, -cim], axis=0).astype(BF16)
    wv = bd(w_glu[:, :, :Hc], 'ghj,gk->ghkj').reshape(G * Hc, G * Hc)
    wg = bd(w_glu[:, :, Hc:], 'ghj,gk->ghkj').reshape(G * Hc, G * Hc)
    wglu = jnp.concatenate([wv, wg], axis=1).astype(BF16)
    return ab_re.reshape(1, G * P), ab_im.reshape(1, G * P), bcat, ccat, wglu


def _s5(u_tm, bsz, params, d_skip):
    s = u_tm.shape[0]
    w = S5_WIDTH
    n = S5_NSTATE
    are, aim, bcat, ccat, wglu = params
    u3 = u_tm.reshape(s, bsz, w)
    tc = S5_TC
    c2 = lambda i: (0, 0)
    y = pl.pallas_call(
        _s5_kernel,
        grid=(s // tc,),
        in_specs=[pl.BlockSpec((tc, bsz, w), lambda i: (i, 0, 0)),
                  pl.BlockSpec((1, n), c2),
                  pl.BlockSpec((1, n), c2),
                  pl.BlockSpec((w, 2 * n), c2),
                  pl.BlockSpec((2 * n, w), c2),
                  pl.BlockSpec((1, w), c2),
                  pl.BlockSpec((w, 2 * w), c2)],
        out_specs=pl.BlockSpec((tc, bsz, w), lambda i: (i, 0, 0)),
        out_shape=jax.ShapeDtypeStruct((s, bsz, w), F32),
        scratch_shapes=[pltpu.VMEM((tc, bsz, 2 * n), F32),
                        pltpu.VMEM((2, bsz, n), F32)],
        compiler_params=_cparams(("arbitrary",)),
        name="s5",
    )(u3, are, aim, bcat, ccat, d_skip.reshape(1, w), wglu)
    return y.reshape(s, bsz * w)


def _layer_norm(z, g, b):
    mu = jnp.mean(z, axis=1, keepdims=True)
    zc = z - mu
    var = jnp.mean(zc * zc, axis=1, keepdims=True)
    return zc * lax.rsqrt(var + LN_EPS) * g + b


def _out_proj_kernel(yda_ref, yml_ref, ys5_ref, x_ref, gate_ref, lng_ref, lnb_ref, shift_ref, scale_ref,
                     wout_ref, wrt_ref, brt_ref, x1_ref, h2_ref, eid_ref, prob_ref, cnt_ref):
    y = jnp.dot(yda_ref[0], wout_ref[0:DA_WIDTH, :], preferred_element_type=F32)
    y = y + jnp.dot(yml_ref[0], wout_ref[DA_WIDTH:DA_WIDTH + ML_WIDTH, :], preferred_element_type=F32)
    y = y + jnp.dot(ys5_ref[...].astype(BF16), wout_ref[DA_WIDTH + ML_WIDTH:, :], preferred_element_type=F32)
    x1 = _layer_norm(DN_ALPHA * x_ref[0] + (1.0 + gate_ref[0]) * y, lng_ref[...], lnb_ref[...])
    x1_ref[0] = x1
    h2 = x1 * (1.0 + scale_ref[0]) + shift_ref[0]
    h_hi = h2.astype(BF16)
    h_hi32 = h_hi.astype(F32)
    h2_ref[...] = _pack_rounded(h_hi32)
    h_lo = (h2 - h_hi32).astype(BF16)
    nt_dot = lambda a, b: lax.dot_general(a, b, (((1,), (1,)), ((), ())), preferred_element_type=F32)
    by_hi = nt_dot(wrt_ref[...], h_hi)
    logits = (by_hi[:N_EXPERTS] + by_hi[N_EXPERTS:] + nt_dot(wrt_ref[0:N_EXPERTS, :], h_lo)
              + brt_ref[...])
    eidx = lax.broadcasted_iota(jnp.int32, logits.shape, 0)
    vals, ids = [], []
    for _ in range(TOP_K):
        mx = jnp.max(logits, axis=0, keepdims=True)
        sel = jnp.min(jnp.where(logits == mx, eidx, N_EXPERTS), axis=0, keepdims=True)
        vals.append(mx)
        ids.append(sel)
        logits = jnp.where(eidx == sel, -jnp.inf, logits)
    ex = [jnp.exp(v - vals[0]) for v in vals]
    tot = ex[0] + ex[1] + ex[2] + ex[3]
    zi = jnp.zeros_like(ids[0])
    eid_ref[...] = jnp.concatenate(ids + [zi] * (SUBLANES - TOP_K), axis=0)

    @pl.when(jnp.logical_and(pl.program_id(0) == 0, pl.program_id(1) == 0))
    def _():
        cnt_ref[...] = jnp.zeros_like(cnt_ref)

    member = jnp.zeros(logits.shape, F32)
    for sel in ids:
        member = member + (eidx == sel).astype(F32)
    cnt_ref[...] = cnt_ref[...] + jnp.sum(member, axis=1, keepdims=True)
    zf = jnp.zeros((LANES - TOP_K, tot.shape[1]), F32)
    prob_ref[...] = jnp.concatenate([e / tot for e in ex] + [zf], axis=0).T


def _out_proj(y_da, y_ml, y_s5, x, gate, ln_g, ln_b, shift2, scale2, w_out_l, w_router_l, b_router_l):
    bsz, s, d = x.shape
    tm = TM_PROJ
    nt = s // tm
    tok = lambda b, i: (b, i, 0)
    per_b = lambda b, i: (b, 0, 0)
    c2 = lambda b, i: (0, 0)
    r3 = lambda a: a.reshape(bsz, 1, d)
    flat = lambda b, i: (0, b * nt + i)
    wr_t = w_router_l.T
    wr_hi = wr_t.astype(BF16)
    wr_lo = (wr_t - wr_hi.astype(F32)).astype(BF16)
    return pl.pallas_call(
        _out_proj_kernel,
        grid=(bsz, nt),
        in_specs=[pl.BlockSpec((1, tm, DA_WIDTH), tok),
                  pl.BlockSpec((1, tm, ML_WIDTH), tok),
                  pl.BlockSpec((tm, S5_WIDTH), lambda b, i: (i, b)),
                  pl.BlockSpec((1, tm, d), tok),
                  pl.BlockSpec((1, 1, d), per_b),
                  pl.BlockSpec((1, d), c2),
                  pl.BlockSpec((1, d), c2),
                  pl.BlockSpec((1, 1, d), per_b),
                  pl.BlockSpec((1, 1, d), per_b),
                  pl.BlockSpec((d, d), c2),
                  pl.BlockSpec((2 * N_EXPERTS, d), c2),
                  pl.BlockSpec((N_EXPERTS, 1), c2)],
        out_specs=[pl.BlockSpec((1, tm, d), tok),
                   pl.BlockSpec((tm, d // 2), lambda b, i: (b * nt + i, 0)),
                   pl.BlockSpec((SUBLANES, tm), flat),
                   pl.BlockSpec((tm, LANES), lambda b, i: (b * nt + i, 0)),
                   pl.BlockSpec((N_EXPERTS, LANES), c2)],
        out_shape=[jax.ShapeDtypeStruct((bsz, s, d), F32),
                   jax.ShapeDtypeStruct((bsz * s, d // 2), jnp.int32),
                   jax.ShapeDtypeStruct((SUBLANES, bsz * s), jnp.int32),
                   jax.ShapeDtypeStruct((bsz * s, LANES), F32),
                   jax.ShapeDtypeStruct((N_EXPERTS, LANES), F32)],
        compiler_params=_cparams(("arbitrary", "arbitrary")),
        name="out_proj",
    )(y_da, y_ml, y_s5, x, r3(gate), ln_g.reshape(1, d), ln_b.reshape(1, d), r3(shift2), r3(scale2),
      w_out_l.astype(BF16), jnp.concatenate([wr_hi, wr_lo], axis=0), b_router_l.reshape(N_EXPERTS, 1))


META_END, META_PAD, META_CNT = 0, 1, 2


def _route_kernel(eid_ref, cnt_ref, pos_ref, meta_ref, carry_s, start_s):
    i = pl.program_id(0)
    tb = eid_ref.shape[1]
    ntp = meta_ref.shape[1]
    tm = TM_MOE

    @pl.when(i == 0)
    def _():
        cnt = cnt_ref[...]
        padded = jnp.floor((cnt + (tm - 1)) * (1.0 / tm)) * tm
        er = lax.broadcasted_iota(jnp.int32, (N_EXPERTS, N_EXPERTS), 0)
        ec = lax.broadcasted_iota(jnp.int32, (N_EXPERTS, N_EXPERTS), 1)
        ends = jnp.dot((ec <= er).astype(F32), padded, preferred_element_type=F32, precision=HIGHEST)
        start_s[...] = ends - padded
        carry_s[...] = jnp.zeros_like(carry_s)
        lane = lax.broadcasted_iota(jnp.int32, (N_EXPERTS, ntp), 1)
        sub = lax.broadcasted_iota(jnp.int32, (N_EXPERTS, ntp), 0)
        diag = lane == sub

        def as_row(col):
            return jnp.sum(jnp.where(diag, col, 0.0), axis=0, keepdims=True)

        zero = jnp.zeros((SUBLANES - 3, ntp), F32)
        meta_ref[...] = jnp.concatenate([as_row(ends[:, 0:1]), as_row(padded[:, 0:1]), as_row(cnt[:, 0:1]), zero],
                                        axis=0).astype(jnp.int32)

    eid = eid_ref[...]
    eidx = lax.broadcasted_iota(jnp.int32, (N_EXPERTS, tb), 0)
    hot = [eidx == eid[k:k + 1, :] for k in range(TOP_K)]
    member = jnp.zeros((N_EXPERTS, tb), F32)
    for k in range(TOP_K):
        member = member + hot[k].astype(F32)
    ri = lax.broadcasted_iota(jnp.int32, (tb, tb), 0)
    ci = lax.broadcasted_iota(jnp.int32, (tb, tb), 1)
    triu = (ri <= ci).astype(BF16)
    incl = jnp.dot(member.astype(BF16), triu, preferred_element_type=F32)
    slot = incl - member + carry_s[:, 0:1] + start_s[:, 0:1]
    rows = [jnp.sum(jnp.where(hot[k], slot, 0.0), axis=0, keepdims=True) for k in range(TOP_K)]
    zr = jnp.zeros_like(rows[0])
    pos_ref[...] = jnp.concatenate(rows + [zr] * (SUBLANES - TOP_K), axis=0).astype(jnp.int32)
    carry_s[...] = carry_s[...] + jnp.sum(member, axis=1, keepdims=True)


def _route(eid, counts):
    t = eid.shape[1]
    tb = TB_RANK
    ntp = LANES
    pos8, meta = pl.pallas_call(
        _route_kernel,
        grid=(t // tb,),
        in_specs=[pl.BlockSpec((SUBLANES, tb), lambda i: (0, i)),
                  pl.BlockSpec((N_EXPERTS, LANES), lambda i: (0, 0))],
        out_specs=[pl.BlockSpec((SUBLANES, tb), lambda i: (0, i)),
                   pl.BlockSpec((SUBLANES, ntp), lambda i: (0, 0))],
        out_shape=[jax.ShapeDtypeStruct((SUBLANES, t), jnp.int32),
                   jax.ShapeDtypeStruct((SUBLANES, ntp), jnp.int32)],
        scratch_shapes=[pltpu.VMEM((N_EXPERTS, LANES), F32), pltpu.VMEM((N_EXPERTS, LANES), F32)],
        compiler_params=_cparams(("arbitrary",)),
        name="route",
    )(eid, counts)
    return pos8, meta


def _sc_workers():
    info = plsc.get_sparse_core_info()
    return info.num_cores, info.num_cores * info.num_subcores


def _dispatch(h2, pos8, n_rows):
    t, d = h2.shape
    n_cores, n_workers = _sc_workers()
    tpw = t // n_workers
    ch = SC_SCATTER_CHUNK
    mesh = plsc.VectorSubcoreMesh(core_axis_name="c", subcore_axis_name="s")

    @functools.partial(
        pl.kernel, mesh=mesh,
        out_type=jax.ShapeDtypeStruct((n_rows, d), h2.dtype),
        scratch_types=[pltpu.VMEM((ch,), jnp.int32)] * TOP_K + [pltpu.VMEM((ch, d), h2.dtype),
                                                                pltpu.SemaphoreType.DMA])
    def scatter_rows(h_hbm, pos_hbm, out_hbm, i0, i1, i2, i3, rows_v, sem):
        idx = (i0, i1, i2, i3)
        base = (lax.axis_index("s") * n_cores + lax.axis_index("c")) * tpw

        @pl.loop(0, tpw // ch)
        def _(i):
            off = base + i * ch
            pltpu.sync_copy(h_hbm.at[pl.ds(off, ch)], rows_v)
            for k in range(TOP_K):
                pltpu.sync_copy(pos_hbm.at[k, pl.ds(off, ch)], idx[k])
            copies = [pltpu.async_copy(rows_v, out_hbm.at[idx[k]], sem) for k in range(TOP_K)]
            for cp in copies:
                cp.wait()

    return scatter_rows(h2, pos8)


def _gather_expert_rows(ys, pos8):
    _, d = ys.shape
    t = pos8.shape[1]
    n_cores, n_workers = _sc_workers()
    tpw = t // n_workers
    ch = SC_GATHER_CHUNK
    mesh = plsc.VectorSubcoreMesh(core_axis_name="c", subcore_axis_name="s")

    @functools.partial(
        pl.kernel, mesh=mesh,
        out_type=jax.ShapeDtypeStruct((TOP_K, t, d), ys.dtype),
        scratch_types=([pltpu.VMEM((ch,), jnp.int32)] * 2 + [pltpu.VMEM((ch, d), ys.dtype)] * 2
                       + [pltpu.SemaphoreType.DMA] * 4))
    def gather_rows(ys_hbm, pos_hbm, out_hbm, idx0, idx1, rows0, rows1, g0, g1, w0, w1):
        idx, rows, gsem, wsem = (idx0, idx1), (rows0, rows1), (g0, g1), (w0, w1)
        base = (lax.axis_index("s") * n_cores + lax.axis_index("c")) * tpw
        items = [(i, k) for i in range(tpw // ch) for k in range(TOP_K)]

        def gather(n):
            i, k = items[n]
            pltpu.sync_copy(pos_hbm.at[k, pl.ds(base + i * ch, ch)], idx[n % 2])
            return pltpu.async_copy(ys_hbm.at[idx[n % 2]], rows[n % 2], gsem[n % 2])

        def write(n):
            i, k = items[n]
            return pltpu.async_copy(rows[n % 2], out_hbm.at[k, pl.ds(base + i * ch, ch)], wsem[n % 2])

        gathers, writes = {}, {}
        for n in range(len(items)):
            if n >= 2:
                writes[n - 2].wait()
            gathers[n] = gather(n)
            if n >= 1:
                gathers[n - 1].wait()
                writes[n - 1] = write(n - 1)
        last = len(items) - 1
        gathers[last].wait()
        writes[last] = write(last)
        if last >= 1:
            writes[last - 1].wait()
        writes[last].wait()

    return gather_rows(ys, pos8)


def _pack_bf16_pairs(a):
    return _pack_rounded(a.astype(BF16).astype(F32))


def _pack_rounded(r):
    n = r.shape[1] // 2
    lo = pltpu.bitcast(r[:, :n], jnp.int32)
    hi = pltpu.bitcast(r[:, n:], jnp.int32)
    return jnp.bitwise_or(jnp.bitwise_and(hi, -65536), jnp.bitwise_and(lax.shift_right_logical(lo, 16), 65535))


def _unpack_bf16_pairs(p):
    lo = pltpu.bitcast(lax.shift_left(p, 16), F32)
    hi = pltpu.bitcast(jnp.bitwise_and(p, -65536), F32)
    return lo, hi


def _expert_kernel(meta_ref, xs_ref, wup_ref, bup_ref, wdn_ref, bdn_ref, ys_ref,
                   wup_s, wdn_s, xbuf, obuf, in_sem, out_sem):
    e = pl.program_id(0)
    tm = TM_MOE
    half = D_MODEL // 2
    pad = meta_ref[META_PAD, e]
    n_t = pad // tm
    row0 = meta_ref[META_END, e] - pad
    cnt = meta_ref[META_CNT, e]

    def rows(i):
        return pl.ds(pl.multiple_of(row0 + i * tm, tm), tm)

    def x_copy(i, slot):
        return pltpu.make_async_copy(xs_ref.at[rows(i)], xbuf.at[slot], in_sem.at[slot])

    def y_copy(i, slot):
        return pltpu.make_async_copy(obuf.at[slot], ys_ref.at[rows(i)], out_sem.at[slot])

    @pl.when(n_t > 0)
    def _():
        x_copy(0, 0).start()
        wup_s[...] = wup_ref[0, 0].astype(BF16)
        wdn_s[...] = wdn_ref[0, 0].astype(BF16)

        def tile(i, carry):
            slot = lax.rem(i, 2)
            x_copy(i, slot).wait()

            @pl.when(i + 1 < n_t)
            def _():
                x_copy(i + 1, 1 - slot).start()

            @pl.when(i >= 2)
            def _():
                y_copy(i - 2, slot).wait()

            row = lax.broadcasted_iota(jnp.int32, (tm, 1), 0)
            lo, hi = _unpack_bf16_pairs(jnp.where(row < cnt - i * tm, xbuf[slot], 0))
            z = (jnp.dot(lo.astype(BF16), wup_s[0:half, :], preferred_element_type=F32)
                 + jnp.dot(hi.astype(BF16), wup_s[half:, :], preferred_element_type=F32) + bup_ref[0, 0])
            glu = jnp.minimum(z[:, :D_EXPERT], SWIGLU_LIMIT)
            lin = jnp.clip(z[:, D_EXPERT:], -SWIGLU_LIMIT, SWIGLU_LIMIT)
            act = (glu * _sigmoid(SWIGLU_ALPHA * glu) * (lin + 1.0)).astype(BF16)
            y = jnp.dot(act, wdn_s[...], preferred_element_type=F32) + bdn_ref[0, 0]
            obuf[slot] = _pack_bf16_pairs(y)
            y_copy(i, slot).start()
            return carry

        lax.fori_loop(0, n_t, tile, 0)

        @pl.when(n_t >= 2)
        def _():
            y_copy(n_t - 2, lax.rem(n_t, 2)).wait()

        y_copy(n_t - 1, lax.rem(n_t - 1, 2)).wait()

    @pl.when(e == N_EXPERTS - 1)
    def _():
        obuf[0] = jnp.zeros((tm, half), jnp.int32)

        def fill(i, carry):
            cp = pltpu.make_async_copy(obuf.at[0], ys_ref.at[pl.ds(pl.multiple_of(i * tm, tm), tm)], out_sem.at[0])
            cp.start()
            cp.wait()
            return carry

        lax.fori_loop(meta_ref[META_END, N_EXPERTS - 1] // tm, ys_ref.shape[0] // tm, fill, 0)


def _expert_mlp(xs, meta, layer, w_up, b_up, w_down, b_down):
    n_rows, half = xs.shape
    d = 2 * half
    tm = TM_MOE
    f = w_up.shape[-1]
    b_up4 = b_up.reshape(DEPTH, N_EXPERTS, 1, f)
    b_dn4 = b_down.reshape(DEPTH, N_EXPERTS, 1, d)
    wsel = lambda e, m: (layer, e, 0, 0)
    grid_spec = pltpu.PrefetchScalarGridSpec(
        num_scalar_prefetch=1,
        grid=(N_EXPERTS,),
        in_specs=[pl.BlockSpec(memory_space=pl.ANY),
                  pl.BlockSpec((1, 1, d, f), wsel),
                  pl.BlockSpec((1, 1, 1, f), wsel),
                  pl.BlockSpec((1, 1, f // 2, d), wsel),
                  pl.BlockSpec((1, 1, 1, d), wsel)],
        out_specs=pl.BlockSpec(memory_space=pl.ANY),
        scratch_shapes=[pltpu.VMEM((d, f), BF16), pltpu.VMEM((f // 2, d), BF16),
                        pltpu.VMEM((2, tm, half), jnp.int32), pltpu.VMEM((2, tm, half), jnp.int32),
                        pltpu.SemaphoreType.DMA((2,)), pltpu.SemaphoreType.DMA((2,))],
    )
    return pl.pallas_call(
        _expert_kernel,
        grid_spec=grid_spec,
        out_shape=jax.ShapeDtypeStruct((n_rows, half), jnp.int32),
        compiler_params=_cparams(("arbitrary",)),
        name="expert_mlp",
    )(meta, xs, w_up, b_up4, w_down, b_dn4)


def _combine_kernel(rows_ref, prob_ref, x_ref, gate_ref, lng_ref, lnb_ref, o_ref):
    p = prob_ref[...]
    y = None
    for k in range(TOP_K):
        yk = p[:, k:k + 1] * jnp.concatenate(_unpack_bf16_pairs(rows_ref[k]), axis=1)
        y = yk if y is None else y + yk
    o_ref[0] = _layer_norm(DN_ALPHA * x_ref[0] + (1.0 + gate_ref[0]) * y, lng_ref[...], lnb_ref[...])


def _combine(rows, prob_c, x1, gate, ln_g, ln_b, part):
    bsz, s, d = x1.shape
    tm = TM_DISP
    nt = s // tm
    pb = bsz // MOE_PARTS
    b0 = part * pb
    return pl.pallas_call(
        _combine_kernel,
        grid=(pb, nt),
        in_specs=[pl.BlockSpec((TOP_K, tm, d // 2), lambda b, i: (0, b * nt + i, 0)),
                  pl.BlockSpec((tm, LANES), lambda b, i: ((b0 + b) * nt + i, 0)),
                  pl.BlockSpec((1, tm, d), lambda b, i: (b0 + b, i, 0)),
                  pl.BlockSpec((1, 1, d), lambda b, i: (b0 + b, 0, 0)),
                  pl.BlockSpec((1, d), lambda b, i: (0, 0)),
                  pl.BlockSpec((1, d), lambda b, i: (0, 0))],
        out_specs=pl.BlockSpec((1, tm, d), lambda b, i: (b0 + b, i, 0)),
        out_shape=jax.ShapeDtypeStruct((bsz, s, d), F32),
        input_output_aliases={2: 0},
        compiler_params=_cparams(("arbitrary", "arbitrary")),
        name="combine",
    )(rows, prob_c, x1, gate.reshape(bsz, 1, d), ln_g.reshape(1, d), ln_b.reshape(1, d))


def kernel(x, c, positions, ada_w, ada_b, w_in, lam_q1, lam_k1, lam_q2, lam_k2, da_norm_g, ml_conv_w, ml_conv_b,
           ml_w_q, ml_w_k, ml_gate_b, ml_norm_g, s5_a_re, s5_a_im, s5_log_dt, s5_b_re, s5_b_im, s5_c_re, s5_c_im,
           s5_d, s5_w_glu, w_out, ln_g, ln_b, w_router, b_router, w_up, b_up, w_down, b_down):
    bsz, s, d = x.shape
    t = bsz * s
    n_tiles_max = (t * TOP_K) // TM_MOE + N_EXPERTS
    n_rows = n_tiles_max * TM_MOE
    mod = _modulation(c, ada_w, ada_b)
    cos_t, sin_t = _rope_tables(positions)
    for l in range(DEPTH):
        shift, scale, gate = jnp.split(mod[2 * l], 3, axis=-1)
        q, k, v, mlx, mlv, mlo, g_t, g_c, s5u = _in_proj(x, shift, scale, cos_t, sin_t, w_in[l])
        lam_init = 0.8 - 0.6 * math.exp(-0.3 * l)
        lamv = jnp.stack([lam_q1[l], lam_k1[l], lam_q2[l], lam_k2[l]])
        y_da = _diff_attn(q, k, v, lamv, da_norm_g[l], lam_init)
        y_ml = _mlstm(mlx, mlv, mlo, g_t, g_c, ml_conv_w[l], ml_conv_b[l], ml_w_q[l], ml_w_k[l],
                      ml_gate_b[l], ml_norm_g[l])
        s5p = _s5_params(s5_a_re[l], s5_a_im[l], s5_log_dt[l], s5_b_re[l], s5_b_im[l], s5_c_re[l], s5_c_im[l],
                         s5_w_glu[l])
        y_s5 = _s5(s5u, bsz, s5p, s5_d[l])
        shift2, scale2, gate2 = jnp.split(mod[2 * l + 1], 3, axis=-1)
        x1, h2, eid, prob, counts = _out_proj(y_da, y_ml, y_s5, x, gate, ln_g[l, 0], ln_b[l, 0], shift2, scale2,
                                      w_out[l], w_router[l], b_router[l])
        pos8, meta = _route(eid, counts)
        xs = _dispatch(h2, pos8, n_rows)
        ys = _expert_mlp(xs, meta, l, w_up, b_up, w_down, b_down)
        tp = t // MOE_PARTS
        x = x1
        for part in range(MOE_PARTS):
            rows = _gather_expert_rows(ys, pos8[:, part * tp:(part + 1) * tp])
            x = _combine(rows, prob, x, gate2, ln_g[l, 1], ln_b[l, 1], part)
    return x
```

```python
import functools
import math

import jax
import jax.numpy as jnp
from jax import lax
from jax.experimental import pallas as pl
from jax.experimental.pallas import tpu as pltpu
from jax.experimental.pallas import tpu_sc as plsc

F32 = jnp.float32
BF16 = jnp.bfloat16
HIGHEST = lax.Precision.HIGHEST

D_MODEL = 1024
DEPTH = 2
DA_HEADS = 4
DA_HEAD_DIM = 64
DA_V_DIM = 2 * DA_HEAD_DIM
DA_WIDTH = DA_HEADS * DA_V_DIM
DA_QK_WIDTH = DA_HEADS * 2 * DA_HEAD_DIM
ROPE_THETA = 10000.0
ML_HEADS = 4
ML_HEAD_DIM = 64
ML_WIDTH = ML_HEADS * ML_HEAD_DIM
ML_CONV = 4
S5_GROUP = 16
S5_STATE = 64
S5_WIDTH = D_MODEL - DA_WIDTH - ML_WIDTH
S5_GROUPS = S5_WIDTH // S5_GROUP
S5_NSTATE = S5_GROUPS * S5_STATE
N_EXPERTS = 32
TOP_K = 4
D_EXPERT = D_MODEL
SWIGLU_LIMIT = 7.0
SWIGLU_ALPHA = 1.702
DN_ALPHA = (2 * DEPTH) ** 0.25
LN_EPS = 1e-5
NEG = -1e30

OFF_DA_K = DA_QK_WIDTH
OFF_DA_V = 2 * DA_QK_WIDTH
OFF_ML_X = OFF_DA_V + DA_WIDTH
OFF_ML_V = OFF_ML_X + ML_WIDTH
OFF_ML_O = OFF_ML_V + ML_WIDTH
OFF_ML_I = OFF_ML_O + ML_WIDTH
OFF_ML_F = OFF_ML_I + ML_HEADS
OFF_S5_U = OFF_ML_F + ML_HEADS
N_IN = OFF_S5_U + S5_WIDTH

LANES = 128
SUBLANES = 8
VMEM_LIMIT_BYTES = 56 * 1024 * 1024

TM_PROJ = 1024
TQ = 512
ML_CHUNK = 256
ML_NB = 4
S5_TC = 256
S5_UNROLL = 8
TB_RANK = 1024
TM_MOE = 256
TM_DISP = 512
MOE_PARTS = 2
SC_SCATTER_CHUNK = 128
SC_GATHER_CHUNK = 64
GATE_PAD = 8
VT_ROWS = DA_V_DIM + 16
Q_PRESCALE = DA_HEAD_DIM ** -0.5 * math.log2(math.e)


def _cparams(sem, vmem=VMEM_LIMIT_BYTES):
    return pltpu.CompilerParams(dimension_semantics=sem, vmem_limit_bytes=vmem)


def _sigmoid(x):
    return 1.0 / (1.0 + jnp.exp(-x))


def _mod_kernel(c_ref, w_ref, b_ref, o_ref):
    c = c_ref[...]
    ca = (c * _sigmoid(c)).astype(BF16)
    w = w_ref[0, 0].astype(BF16)
    o_ref[0] = jnp.dot(ca, w, preferred_element_type=F32) + b_ref[0]


def _modulation(c, ada_w, ada_b):
    nsub = ada_w.shape[1]
    nmod = ada_w.shape[0] * nsub
    bsz, d = c.shape
    e = ada_w.shape[-1]
    tn = 1024
    b = ada_b.reshape(nmod, 1, e)
    return pl.pallas_call(
        _mod_kernel,
        grid=(nmod, e // tn),
        in_specs=[pl.BlockSpec((bsz, d), lambda n, j: (0, 0)),
                  pl.BlockSpec((1, 1, d, tn), lambda n, j: (n // nsub, n % nsub, 0, j)),
                  pl.BlockSpec((1, 1, tn), lambda n, j: (n, 0, j))],
        out_specs=pl.BlockSpec((1, bsz, tn), lambda n, j: (n, 0, j)),
        out_shape=jax.ShapeDtypeStruct((nmod, bsz, e), F32),
        compiler_params=_cparams(("arbitrary", "arbitrary")),
        name="modulation",
    )(c, ada_w, b)


def _rope_kernel(pos_ref, cos_ref, sin_ref):
    nfreq = DA_HEAD_DIM // 2
    pos = pos_ref[0].astype(F32)
    fidx = lax.broadcasted_iota(jnp.int32, (nfreq, 1), 0).astype(F32)
    inv = jnp.exp(fidx * (-2.0 * math.log(ROPE_THETA) / DA_HEAD_DIM))
    ang = inv * pos
    reps = LANES // nfreq
    cos_t = jnp.concatenate([jnp.cos(ang)] * reps, axis=0).T
    sin_t = jnp.concatenate([jnp.sin(ang)] * reps, axis=0).T
    lane = lax.broadcasted_iota(jnp.int32, (1, LANES), 1)
    sign = jnp.where((lane % DA_HEAD_DIM) < nfreq, -1.0, 1.0)
    cos_ref[0] = cos_t
    sin_ref[0] = sin_t * sign


def _rope_tables(positions):
    bsz, s = positions.shape
    ts = 512
    pos3 = positions.reshape(bsz, 1, s)
    return pl.pallas_call(
        _rope_kernel,
        grid=(bsz, s // ts),
        in_specs=[pl.BlockSpec((1, 1, ts), lambda b, i: (b, 0, i))],
        out_specs=[pl.BlockSpec((1, ts, LANES), lambda b, i: (b, i, 0))] * 2,
        out_shape=[jax.ShapeDtypeStruct((bsz, s, LANES), F32)] * 2,
        compiler_params=_cparams(("arbitrary", "arbitrary")),
        name="rope_tables",
    )(pos3)


def _in_proj_kernel(x_ref, shift_ref, scale_ref, cos_ref, sin_ref, win_ref,
                    q_ref, k_ref, vt_ref, mlx_ref, mlv_ref, mlo_ref, gt_ref, gc_ref, s5u_ref,
                    wqk_ref, wvt_ref, wrest_ref, wgt_ref, wgc_ref):
    @pl.when(jnp.logical_and(pl.program_id(0) == 0, pl.program_id(1) == 0))
    def _():
        wqk_ref[...] = win_ref[0, 0:OFF_DA_V, :].T.astype(BF16)
        wvt_ref[...] = win_ref[0, OFF_DA_V:OFF_ML_X, :].astype(BF16)
        wrest_ref[:, 0:3 * ML_WIDTH] = win_ref[0, OFF_ML_X:OFF_ML_I, :].T.astype(BF16)
        wrest_ref[:, 3 * ML_WIDTH:] = win_ref[0, OFF_S5_U:N_IN, :].T.astype(BF16)
        wgt_ref[...] = win_ref[0, OFF_ML_I:OFF_S5_U, :].astype(BF16)
        gslab = win_ref[0, OFF_ML_I:OFF_ML_I + LANES, :].T
        glane = lax.broadcasted_iota(jnp.int32, (1, LANES), 1)
        wgc_ref[...] = jnp.where(glane < GATE_PAD, gslab, 0.0).astype(BF16)

    h = (x_ref[0] * (1.0 + scale_ref[0]) + shift_ref[0]).astype(BF16)
    cos = cos_ref[0]
    sin = sin_ref[0]
    lane = lax.broadcasted_iota(jnp.int32, (1, LANES), 1)
    lo_half = (lane % DA_HEAD_DIM) < DA_HEAD_DIM // 2
    half = DA_HEAD_DIM // 2

    def rope(t):
        fwd = pltpu.roll(t, half, 1)
        bwd = pltpu.roll(t, LANES - half, 1)
        partner = jnp.where(lo_half, bwd, fwd)
        return t * cos + partner * sin

    qk = jnp.dot(h, wqk_ref[...], preferred_element_type=F32)
    nslab = DA_QK_WIDTH // LANES
    for c in range(nslab):
        q_ref[0, :, c * LANES:(c + 1) * LANES] = (
            rope(qk[:, c * LANES:(c + 1) * LANES]) * Q_PRESCALE).astype(BF16)
        k_ref[0, :, c * LANES:(c + 1) * LANES] = rope(
            qk[:, DA_QK_WIDTH + c * LANES:DA_QK_WIDTH + (c + 1) * LANES]).astype(BF16)

    vt = lax.dot_general(wvt_ref[...], h, (((1,), (1,)), ((), ())), preferred_element_type=F32)
    tm = h.shape[0]
    for hh in range(DA_HEADS):
        for jj in range(tm // TQ):
            vt_ref[0, hh, jj, 0:DA_V_DIM, :] = vt[hh * DA_V_DIM:(hh + 1) * DA_V_DIM,
                                                  jj * TQ:(jj + 1) * TQ].astype(BF16)
            vt_ref[0, hh, jj, DA_V_DIM:VT_ROWS, :] = jnp.ones((VT_ROWS - DA_V_DIM, TQ), BF16)

    r = jnp.dot(h, wrest_ref[...], preferred_element_type=F32)
    o = 0
    mlx_ref[0] = r[:, o:o + ML_WIDTH].astype(BF16); o += ML_WIDTH
    mlv_ref[0] = r[:, o:o + ML_WIDTH].astype(BF16); o += ML_WIDTH
    mlo_ref[0] = r[:, o:o + ML_WIDTH].astype(BF16); o += ML_WIDTH
    s5u_ref[...] = r[:, o:o + S5_WIDTH]
    gt = lax.dot_general(wgt_ref[...], h, (((1,), (1,)), ((), ())), preferred_element_type=F32)
    for c in range(h.shape[0] // ML_CHUNK):
        gt_ref[0, c] = gt[:, c * ML_CHUNK:(c + 1) * ML_CHUNK]
    gc_ref[0] = jnp.dot(h, wgc_ref[...], preferred_element_type=F32)


def _in_proj(x, shift, scale, cos_t, sin_t, w_in, layer):
    bsz, s, d = x.shape
    tm = TM_PROJ
    nrest = 3 * ML_WIDTH + S5_WIDTH
    shift3 = shift.reshape(bsz, 1, d)
    scale3 = scale.reshape(bsz, 1, d)
    tok = lambda b, i: (b, i, 0)
    per_b = lambda b, i: (b, 0, 0)
    out_shapes = [
        jax.ShapeDtypeStruct((bsz, s, DA_QK_WIDTH), BF16),
        jax.ShapeDtypeStruct((bsz, s, DA_QK_WIDTH), BF16),
        jax.ShapeDtypeStruct((bsz, DA_HEADS, s // TQ, VT_ROWS, TQ), BF16),
        jax.ShapeDtypeStruct((bsz, s, ML_WIDTH), BF16),
        jax.ShapeDtypeStruct((bsz, s, ML_WIDTH), BF16),
        jax.ShapeDtypeStruct((bsz, s, ML_WIDTH), BF16),
        jax.ShapeDtypeStruct((bsz, s // ML_CHUNK, GATE_PAD, ML_CHUNK), F32),
        jax.ShapeDtypeStruct((bsz, s, LANES), F32),
        jax.ShapeDtypeStruct((s, bsz * S5_WIDTH), F32),
    ]
    out_specs = [
        pl.BlockSpec((1, tm, DA_QK_WIDTH), tok),
        pl.BlockSpec((1, tm, DA_QK_WIDTH), tok),
        pl.BlockSpec((1, DA_HEADS, tm // TQ, VT_ROWS, TQ), lambda b, i: (b, 0, i, 0, 0)),
        pl.BlockSpec((1, tm, ML_WIDTH), tok),
        pl.BlockSpec((1, tm, ML_WIDTH), tok),
        pl.BlockSpec((1, tm, ML_WIDTH), tok),
        pl.BlockSpec((1, tm // ML_CHUNK, GATE_PAD, ML_CHUNK), lambda b, i: (b, i, 0, 0)),
        pl.BlockSpec((1, tm, LANES), tok),
        pl.BlockSpec((tm, S5_WIDTH), lambda b, i: (i, b)),
    ]
    return pl.pallas_call(
        _in_proj_kernel,
        grid=(bsz, s // tm),
        in_specs=[pl.BlockSpec((1, tm, d), tok),
                  pl.BlockSpec((1, 1, d), per_b),
                  pl.BlockSpec((1, 1, d), per_b),
                  pl.BlockSpec((1, tm, LANES), tok),
                  pl.BlockSpec((1, tm, LANES), tok),
                  pl.BlockSpec((1, N_IN, d), lambda b, i: (layer, 0, 0))],
        out_specs=out_specs,
        out_shape=out_shapes,
        scratch_shapes=[pltpu.VMEM((d, OFF_DA_V), BF16), pltpu.VMEM((DA_WIDTH, d), BF16),
                        pltpu.VMEM((d, nrest), BF16), pltpu.VMEM((GATE_PAD, d), BF16),
                        pltpu.VMEM((d, LANES), BF16)],
        compiler_params=_cparams(("arbitrary", "arbitrary")),
        name="in_proj",
    )(x, shift3, scale3, cos_t, sin_t, jnp.swapaxes(w_in, 1, 2))


def _diff_attn_kernel(lam_init, lamv_ref, gain_ref, q_ref, k_ref, vt_ref, o_ref, acc_s, m_s):
    qi = pl.program_id(2)
    tq = q_ref.shape[1]
    lane = lax.broadcasted_iota(jnp.int32, (1, LANES), 1)
    first = lane < DA_HEAD_DIM
    q = q_ref[0]
    zero = jnp.zeros_like(q)
    qm = (jnp.where(first, q, zero), jnp.where(first, zero, q))
    acc_s[...] = jnp.zeros_like(acc_s)
    m_s[...] = jnp.full(m_s.shape, NEG, F32)

    def step(j, nblk, masked):
        tk = nblk * tq
        kb = k_ref[0, pl.ds(pl.multiple_of(j * tq, tq), tk), :]
        vtb = vt_ref[0, 0, j] if nblk == 1 else jnp.concatenate([vt_ref[0, 0, j + b] for b in range(nblk)], axis=1)
        for c in range(2):
            st = lax.dot_general(kb, qm[c], (((1,), (1,)), ((), ())), preferred_element_type=F32)
            if masked:
                key_i = lax.broadcasted_iota(jnp.int32, (tk, tq), 0) - (tk - tq)
                qry_i = lax.broadcasted_iota(jnp.int32, (tk, tq), 1)
                st = jnp.where(key_i <= qry_i, st, NEG)
            m_prev = m_s[c]
            m_new = jnp.maximum(m_prev, jnp.max(st, axis=0, keepdims=True))
            alpha = jnp.exp2(m_prev - m_new)
            p = jnp.exp2(st - m_new).astype(BF16)
            acc_s[c] = alpha * acc_s[c] + jnp.dot(vtb, p, preferred_element_type=F32)
            m_s[c] = m_new

    def body(jj, carry):
        step(2 * jj, 2, False)
        return carry

    lax.fori_loop(0, qi // 2, body, 0)

    @pl.when(qi % 2 == 1)
    def _():
        step(qi - 1, 2, True)

    @pl.when(qi % 2 == 0)
    def _():
        step(qi, 1, True)

    outs = []
    for c in range(2):
        acc = acc_s[c]
        outs.append(acc[:DA_V_DIM] / acc[DA_V_DIM:DA_V_DIM + 1])

    lamv = lamv_ref[...]
    lam = (jnp.exp(jnp.sum(lamv[0:1] * lamv[1:2], axis=1, keepdims=True))
           - jnp.exp(jnp.sum(lamv[2:3] * lamv[3:4], axis=1, keepdims=True)) + lam_init)
    ot = outs[0] - lam * outs[1]
    ms = jnp.mean(ot * ot, axis=0, keepdims=True)
    ot = ot * (lax.rsqrt(ms + LN_EPS) * (1.0 - lam_init))
    o_ref[0] = (ot.T * gain_ref[...]).astype(o_ref.dtype)


def _diff_attn(q, k, vt, lamv, gain, lam_init):
    bsz, s, _ = q.shape
    tq = TQ
    nq = s // tq
    return pl.pallas_call(
        functools.partial(_diff_attn_kernel, lam_init),
        grid=(bsz, DA_HEADS, nq),
        in_specs=[pl.BlockSpec((4, DA_HEAD_DIM), lambda b, h, i: (0, 0)),
                  pl.BlockSpec((1, DA_V_DIM), lambda b, h, i: (0, 0)),
                  pl.BlockSpec((1, tq, DA_V_DIM), lambda b, h, i: (b, i, h)),
                  pl.BlockSpec((1, s, DA_V_DIM), lambda b, h, i: (b, 0, h)),
                  pl.BlockSpec((1, 1, nq, VT_ROWS, tq), lambda b, h, i: (b, h, 0, 0, 0))],
        out_specs=pl.BlockSpec((1, tq, DA_V_DIM), lambda b, h, i: (b, i, h)),
        out_shape=jax.ShapeDtypeStruct((bsz, s, DA_WIDTH), BF16),
        scratch_shapes=[pltpu.VMEM((2, VT_ROWS, tq), F32), pltpu.VMEM((2, 1, tq), F32)],
        compiler_params=_cparams(("arbitrary", "arbitrary", "arbitrary")),
        name="diff_attn",
    )(lamv, gain.reshape(1, DA_V_DIM), q, k, vt)


def _log_sigmoid(x):
    return jnp.minimum(x, 0.0) - jnp.log(1.0 + jnp.exp(-jnp.abs(x)))


def _split3(a):
    hi = a.astype(BF16)
    r1 = a - hi.astype(F32)
    mid = r1.astype(BF16)
    lo = (r1 - mid.astype(F32)).astype(BF16)
    return hi, mid, lo


def _mlstm_kernel(x_ref, v_ref, o_ref, gt_ref, gc_ref, cw_ref, cb_ref, wq_ref, wkt_ref, gbt_ref, gbc_ref,
                  ng_ref, hmean_ref, y_ref, xc_s, c_s, m_s):
    nb, s = x_ref.shape[0], x_ref.shape[1]
    L = ML_CHUNK
    H, dh = ML_HEADS, ML_HEAD_DIM
    nc = s // L
    cw = cw_ref[...]
    row = lax.broadcasted_iota(jnp.int32, (s, 1), 0)
    for bi in range(nb):
        x = x_ref[bi].astype(F32)
        xc = x * cw[ML_CONV - 1:ML_CONV]
        for j in range(1, ML_CONV):
            xs = jnp.where(row >= j, pltpu.roll(x, j, 0), 0.0)
            xc = xc + xs * cw[ML_CONV - 1 - j:ML_CONV - j]
        xc = xc + cb_ref[...]
        xc_s[bi] = (xc * _sigmoid(xc)).astype(BF16)

    c_s[...] = jnp.zeros_like(c_s)
    m_s[...] = jnp.full(m_s.shape, NEG, F32)

    ri = lax.broadcasted_iota(jnp.int32, (L, L), 0)
    ci = lax.broadcasted_iota(jnp.int32, (L, L), 1)
    causal = ci <= ri
    tril = causal.astype(BF16)
    triu = (ri <= ci).astype(BF16)
    lane = lax.broadcasted_iota(jnp.int32, (1, dh), 1)
    one_hot0 = jnp.broadcast_to((lane == 0).astype(BF16), (L, dh))

    def chunk_one(bi, ci_, t0):
        xcc = xc_s[bi, pl.ds(t0, L), :]
        qc = jnp.dot(xcc, wq_ref[...], preferred_element_type=F32).astype(BF16)
        ktc = lax.dot_general(wkt_ref[...], xcc, (((1,), (1,)), ((), ())),
                              preferred_element_type=F32)
        g_rows = gt_ref[bi, ci_] + gbt_ref[...]
        g_cols = gc_ref[bi, pl.ds(t0, L), :] + gbc_ref[...]
        lf_rows = _log_sigmoid(g_rows)
        lf_cols = _log_sigmoid(g_cols)
        r3 = jnp.dot(jnp.concatenate(_split3(lf_rows), axis=0), triu, preferred_element_type=F32)
        b_rows = r3[0:GATE_PAD] + r3[GATE_PAD:2 * GATE_PAD] + r3[2 * GATE_PAD:]
        c3 = jnp.dot(tril, jnp.concatenate(_split3(lf_cols), axis=1), preferred_element_type=F32)
        b_cols = c3[:, 0:LANES] + c3[:, LANES:2 * LANES] + c3[:, 2 * LANES:]
        vch = v_ref[bi, pl.ds(t0, L), :]
        och = o_ref[bi, pl.ds(t0, L), :].astype(F32)
        hs = []
        for h in range(H):
            br = b_rows[H + h:H + h + 1, :]
            ir = g_rows[h:h + 1, :]
            bc = b_cols[:, H + h:H + h + 1]
            m_prev = m_s[bi, h]
            log_d = jnp.where(causal, bc - br + ir, NEG)
            inter = bc + m_prev
            mx = jnp.maximum(inter, jnp.max(log_d, axis=1, keepdims=True))
            dmat = jnp.exp(log_d - mx)
            dec = jnp.exp(inter - mx)
            qh = qc[:, h * dh:(h + 1) * dh]
            kth = ktc[h * dh:(h + 1) * dh, :]
            vaug = jnp.concatenate([vch[:, h * dh:(h + 1) * dh], one_hot0], axis=1)
            sm = (jnp.dot(qh, kth.astype(BF16), preferred_element_type=F32) * dmat).astype(BF16)
            c_prev = c_s[bi, h]
            na = (jnp.dot(sm, vaug, preferred_element_type=F32)
                  + dec * jnp.dot(qh, c_prev.astype(BF16), preferred_element_type=F32))
            den = na[:, dh:dh + 1]
            hs.append(na[:, :dh] / jnp.maximum(jnp.abs(den), jnp.exp(-mx)))
            g_tot = br[:, L - 1:L]
            a_row = g_tot - br + ir
            m_new = jnp.maximum(g_tot + m_prev, jnp.max(a_row, axis=1, keepdims=True))
            decay = jnp.exp(g_tot + m_prev - m_new)
            w_row = jnp.exp(a_row - m_new)
            kw = (kth * w_row).astype(BF16)
            c_s[bi, h] = decay * c_prev + jnp.dot(kw, vaug, preferred_element_type=F32)
            m_s[bi, h] = m_new
        hcat = jnp.concatenate(hs, axis=1)
        m3 = jnp.dot(jnp.concatenate(_split3(hcat * hcat), axis=0), hmean_ref[...],
                     preferred_element_type=F32)
        ms = m3[0:L] + m3[L:2 * L] + m3[2 * L:]
        y = hcat * lax.rsqrt(ms + LN_EPS) * ng_ref[...] * _sigmoid(och)
        y_ref[bi, pl.ds(t0, L), :] = y.astype(y_ref.dtype)

    def chunk(ci_, _):
        t0 = pl.multiple_of(ci_ * L, L)
        for bi in range(nb):
            chunk_one(bi, ci_, t0)
        return 0

    lax.fori_loop(0, nc, chunk, 0)


def _mlstm(mlx, mlv, mlo, g_t, g_c, conv_w, conv_b, w_q, w_k, gate_b, norm_g):
    bsz, s, _ = mlx.shape
    H, dh = ML_HEADS, ML_HEAD_DIM
    eye = jnp.eye(H, dtype=F32)
    wq_bd = jnp.einsum('hde,hg->hdge', w_q, eye).reshape(ML_WIDTH, ML_WIDTH).astype(BF16)
    wk_bd = jnp.einsum('hde,hg->hdge', w_k * (dh ** -0.5), eye).reshape(ML_WIDTH, ML_WIDTH)
    wkt_bd = wk_bd.T.astype(BF16)
    gbt = gate_b.reshape(GATE_PAD, 1)
    gbc = jnp.pad(gate_b.reshape(1, GATE_PAD), ((0, 0), (0, LANES - GATE_PAD)))
    hmean = jnp.kron(eye, jnp.full((dh, dh), 1.0 / dh, F32)).astype(BF16)
    nc = s // ML_CHUNK
    tok = lambda b: (b, 0, 0)
    c2 = lambda b: (0, 0)
    nb = ML_NB
    return pl.pallas_call(
        _mlstm_kernel,
        grid=(bsz // nb,),
        in_specs=[pl.BlockSpec((nb, s, ML_WIDTH), tok),
                  pl.BlockSpec((nb, s, ML_WIDTH), tok),
                  pl.BlockSpec((nb, s, ML_WIDTH), tok),
                  pl.BlockSpec((nb, nc, GATE_PAD, ML_CHUNK), lambda b: (b, 0, 0, 0)),
                  pl.BlockSpec((nb, s, LANES), tok),
                  pl.BlockSpec((ML_CONV, ML_WIDTH), c2),
                  pl.BlockSpec((1, ML_WIDTH), c2),
                  pl.BlockSpec((ML_WIDTH, ML_WIDTH), c2),
                  pl.BlockSpec((ML_WIDTH, ML_WIDTH), c2),
                  pl.BlockSpec((GATE_PAD, 1), c2),
                  pl.BlockSpec((1, LANES), c2),
                  pl.BlockSpec((1, ML_WIDTH), c2),
                  pl.BlockSpec((ML_WIDTH, ML_WIDTH), c2)],
        out_specs=pl.BlockSpec((nb, s, ML_WIDTH), tok),
        out_shape=jax.ShapeDtypeStruct((bsz, s, ML_WIDTH), BF16),
        scratch_shapes=[pltpu.VMEM((nb, s, ML_WIDTH), BF16),
                        pltpu.VMEM((nb, H, dh, LANES), F32),
                        pltpu.VMEM((nb, H, 1, 1), F32)],
        compiler_params=_cparams(("arbitrary",)),
        name="mlstm",
    )(mlx, mlv, mlo, g_t, g_c, conv_w, conv_b.reshape(1, ML_WIDTH), wq_bd, wkt_bd, gbt, gbc,
      norm_g.reshape(1, ML_WIDTH), hmean)


def _gelu_tanh(x):
    return 0.5 * x * (1.0 + jnp.tanh(math.sqrt(2.0 / math.pi) * (x + 0.044715 * (x * x * x))))


def _s5_kernel(u_ref, are_ref, aim_ref, bcat_ref, ccat_ref, d_ref, wglu_ref, y_ref, xs_s, st_s):
    tc, bsz, w = u_ref.shape
    n = S5_NSTATE

    @pl.when(pl.program_id(0) == 0)
    def _():
        st_s[...] = jnp.zeros_like(st_s)

    u = u_ref[...].reshape(tc * bsz, w)
    xs_s[...] = jnp.dot(u.astype(BF16), bcat_ref[...], preferred_element_type=F32).reshape(tc, bsz, 2 * n)
    a_re = jnp.broadcast_to(are_ref[...], (bsz, n))
    a_im = jnp.broadcast_to(aim_ref[...], (bsz, n))

    def step(t, carry):
        x_re, x_im = carry
        bu = xs_s[t]
        n_re = a_re * x_re - a_im * x_im + bu[:, :n]
        n_im = a_re * x_im + a_im * x_re + bu[:, n:]
        xs_s[t] = jnp.concatenate([n_re, n_im], axis=1)
        return n_re, n_im

    x_re, x_im = lax.fori_loop(0, tc, step, (st_s[0], st_s[1]), unroll=S5_UNROLL)
    st_s[0] = x_re
    st_s[1] = x_im

    xs = xs_s[...].reshape(tc * bsz, 2 * n).astype(BF16)
    y = jnp.dot(xs, ccat_ref[...], preferred_element_type=F32) + d_ref[...] * u
    z = jnp.dot(_gelu_tanh(y).astype(BF16), wglu_ref[...], preferred_element_type=F32)
    out = z[:, :w] * _sigmoid(z[:, w:])
    y_ref[...] = out.reshape(tc, bsz, w).astype(y_ref.dtype)


def _s5_params(a_re, a_im, log_dt, b_re, b_im, c_re, c_im, w_glu):
    G, P, Hc = S5_GROUPS, S5_STATE, S5_GROUP
    dt = jnp.exp(log_dt)[:, None]
    mag = jnp.exp(a_re * dt)
    ab_re = mag * jnp.cos(a_im * dt)
    ab_im = mag * jnp.sin(a_im * dt)
    nr, ni = ab_re - 1.0, ab_im
    den = a_re * a_re + a_im * a_im
    fr = (nr * a_re + ni * a_im) / den
    fi = (ni * a_re - nr * a_im) / den
    bb_re = fr[..., None] * b_re - fi[..., None] * b_im
    bb_im = fr[..., None] * b_im + fi[..., None] * b_re
    eye = jnp.eye(G, dtype=F32)
    bd = lambda t, sub: jnp.einsum(sub, t, eye)
    bre = bd(bb_re, 'gph,gk->ghkp').reshape(G * Hc, G * P)
    bim = bd(bb_im, 'gph,gk->ghkp').reshape(G * Hc, G * P)
    bcat = jnp.concatenate([bre, bim], axis=1).astype(BF16)
    cre = bd(c_re, 'ghp,gk->gpkh').reshape(G * P, G * Hc)
    cim = bd(c_im, 'ghp,gk->gpkh').reshape(G * P, G * Hc)
    ccat = jnp.concatenate([cre, -cim], axis=0).astype(BF16)
    wv = bd(w_glu[:, :, :Hc], 'ghj,gk->ghkj').reshape(G * Hc, G * Hc)
    wg = bd(w_glu[:, :, Hc:], 'ghj,gk->ghkj').reshape(G * Hc, G * Hc)
    wglu = jnp.concatenate([wv, wg], axis=1).astype(BF16)
    return ab_re.reshape(1, G * P), ab_im.reshape(1, G * P), bcat, ccat, wglu


def _s5(u_tm, bsz, params, d_skip):
    s = u_tm.shape[0]
    w = S5_WIDTH
    n = S5_NSTATE
    are, aim, bcat, ccat, wglu = params
    u3 = u_tm.reshape(s, bsz, w)
    tc = S5_TC
    c2 = lambda i: (0, 0)
    y = pl.pallas_call(
        _s5_kernel,
        grid=(s // tc,),
        in_specs=[pl.BlockSpec((tc, bsz, w), lambda i: (i, 0, 0)),
                  pl.BlockSpec((1, n), c2),
                  pl.BlockSpec((1, n), c2),
                  pl.BlockSpec((w, 2 * n), c2),
                  pl.BlockSpec((2 * n, w), c2),
                  pl.BlockSpec((1, w), c2),
                  pl.BlockSpec((w, 2 * w), c2)],
        out_specs=pl.BlockSpec((tc, bsz, w), lambda i: (i, 0, 0)),
        out_shape=jax.ShapeDtypeStruct((s, bsz, w), F32),
        scratch_shapes=[pltpu.VMEM((tc, bsz, 2 * n), F32),
                        pltpu.VMEM((2, bsz, n), F32)],
        compiler_params=_cparams(("arbitrary",)),
        name="s5",
    )(u3, are, aim, bcat, ccat, d_skip.reshape(1, w), wglu)
    return y.reshape(s, bsz * w)


def _layer_norm(z, g, b):
    mu = jnp.mean(z, axis=1, keepdims=True)
    zc = z - mu
    var = jnp.mean(zc * zc, axis=1, keepdims=True)
    return zc * lax.rsqrt(var + LN_EPS) * g + b


def _out_proj_kernel(yda_ref, yml_ref, ys5_ref, x_ref, gate_ref, lng_ref, lnb_ref, shift_ref, scale_ref,
                     wout32_ref, wrt_ref, brt_ref, x1_ref, h2_ref, eid_ref, prob_ref, cnt_ref, wout_ref):
    first_step = jnp.logical_and(pl.program_id(0) == 0, pl.program_id(1) == 0)

    @pl.when(first_step)
    def _():
        wout_ref[...] = wout32_ref[0].astype(BF16)

    y = jnp.dot(yda_ref[0], wout_ref[0:DA_WIDTH, :], preferred_element_type=F32)
    y = y + jnp.dot(yml_ref[0], wout_ref[DA_WIDTH:DA_WIDTH + ML_WIDTH, :], preferred_element_type=F32)
    y = y + jnp.dot(ys5_ref[...].astype(BF16), wout_ref[DA_WIDTH + ML_WIDTH:, :], preferred_element_type=F32)
    x1 = _layer_norm(DN_ALPHA * x_ref[0] + (1.0 + gate_ref[0]) * y, lng_ref[...], lnb_ref[...])
    x1_ref[0] = x1
    h2 = x1 * (1.0 + scale_ref[0]) + shift_ref[0]
    h_hi = h2.astype(BF16)
    h_hi32 = h_hi.astype(F32)
    h2_ref[...] = _pack_rounded(h_hi32)
    h_lo = (h2 - h_hi32).astype(BF16)
    nt_dot = lambda a, b: lax.dot_general(a, b, (((1,), (1,)), ((), ())), preferred_element_type=F32)
    by_hi = nt_dot(wrt_ref[...], h_hi)
    logits = (by_hi[:N_EXPERTS] + by_hi[N_EXPERTS:] + nt_dot(wrt_ref[0:N_EXPERTS, :], h_lo)
              + brt_ref[...])
    eidx = lax.broadcasted_iota(jnp.int32, logits.shape, 0)
    vals, ids = [], []
    for _ in range(TOP_K):
        mx = jnp.max(logits, axis=0, keepdims=True)
        sel = jnp.min(jnp.where(logits == mx, eidx, N_EXPERTS), axis=0, keepdims=True)
        vals.append(mx)
        ids.append(sel)
        logits = jnp.where(eidx == sel, -jnp.inf, logits)
    ex = [jnp.exp(v - vals[0]) for v in vals]
    tot = ex[0] + ex[1] + ex[2] + ex[3]
    zi = jnp.zeros_like(ids[0])
    eid_ref[...] = jnp.concatenate(ids + [zi] * (SUBLANES - TOP_K), axis=0)

    @pl.when(first_step)
    def _():
        cnt_ref[...] = jnp.zeros_like(cnt_ref)

    member = jnp.zeros(logits.shape, F32)
    for sel in ids:
        member = member + (eidx == sel).astype(F32)
    cnt_ref[...] = cnt_ref[...] + jnp.sum(member, axis=1, keepdims=True)
    zf = jnp.zeros((LANES - TOP_K, tot.shape[1]), F32)
    prob_ref[...] = jnp.concatenate([e / tot for e in ex] + [zf], axis=0).T


def _out_proj(y_da, y_ml, y_s5, x, gate, ln_g, ln_b, shift2, scale2, w_out, layer, w_router_l, b_router_l):
    bsz, s, d = x.shape
    tm = TM_PROJ
    nt = s // tm
    tok = lambda b, i: (b, i, 0)
    per_b = lambda b, i: (b, 0, 0)
    c2 = lambda b, i: (0, 0)
    r3 = lambda a: a.reshape(bsz, 1, d)
    flat = lambda b, i: (0, b * nt + i)
    wr_t = w_router_l.T
    wr_hi = wr_t.astype(BF16)
    wr_lo = (wr_t - wr_hi.astype(F32)).astype(BF16)
    return pl.pallas_call(
        _out_proj_kernel,
        grid=(bsz, nt),
        in_specs=[pl.BlockSpec((1, tm, DA_WIDTH), tok),
                  pl.BlockSpec((1, tm, ML_WIDTH), tok),
                  pl.BlockSpec((tm, S5_WIDTH), lambda b, i: (i, b)),
                  pl.BlockSpec((1, tm, d), tok),
                  pl.BlockSpec((1, 1, d), per_b),
                  pl.BlockSpec((1, d), c2),
                  pl.BlockSpec((1, d), c2),
                  pl.BlockSpec((1, 1, d), per_b),
                  pl.BlockSpec((1, 1, d), per_b),
                  pl.BlockSpec((1, d, d), lambda b, i: (layer, 0, 0)),
                  pl.BlockSpec((2 * N_EXPERTS, d), c2),
                  pl.BlockSpec((N_EXPERTS, 1), c2)],
        out_specs=[pl.BlockSpec((1, tm, d), tok),
                   pl.BlockSpec((tm, d // 2), lambda b, i: (b * nt + i, 0)),
                   pl.BlockSpec((SUBLANES, tm), flat),
                   pl.BlockSpec((tm, LANES), lambda b, i: (b * nt + i, 0)),
                   pl.BlockSpec((N_EXPERTS, LANES), c2)],
        out_shape=[jax.ShapeDtypeStruct((bsz, s, d), F32),
                   jax.ShapeDtypeStruct((bsz * s, d // 2), jnp.int32),
                   jax.ShapeDtypeStruct((SUBLANES, bsz * s), jnp.int32),
                   jax.ShapeDtypeStruct((bsz * s, LANES), F32),
                   jax.ShapeDtypeStruct((N_EXPERTS, LANES), F32)],
        scratch_shapes=[pltpu.VMEM((d, d), BF16)],
        compiler_params=_cparams(("arbitrary", "arbitrary")),
        name="out_proj",
    )(y_da, y_ml, y_s5, x, r3(gate), ln_g.reshape(1, d), ln_b.reshape(1, d), r3(shift2), r3(scale2),
      w_out, jnp.concatenate([wr_hi, wr_lo], axis=0), b_router_l.reshape(N_EXPERTS, 1))


META_END, META_PAD, META_CNT = 0, 1, 2


def _route_kernel(eid_ref, cnt_ref, pos_ref, meta_ref, carry_s, start_s):
    i = pl.program_id(0)
    tb = eid_ref.shape[1]
    ntp = meta_ref.shape[1]
    tm = TM_MOE

    @pl.when(i == 0)
    def _():
        cnt = cnt_ref[...]
        padded = jnp.floor((cnt + (tm - 1)) * (1.0 / tm)) * tm
        er = lax.broadcasted_iota(jnp.int32, (N_EXPERTS, N_EXPERTS), 0)
        ec = lax.broadcasted_iota(jnp.int32, (N_EXPERTS, N_EXPERTS), 1)
        ends = jnp.dot((ec <= er).astype(F32), padded, preferred_element_type=F32, precision=HIGHEST)
        start_s[...] = ends - padded
        carry_s[...] = jnp.zeros_like(carry_s)
        lane = lax.broadcasted_iota(jnp.int32, (N_EXPERTS, ntp), 1)
        sub = lax.broadcasted_iota(jnp.int32, (N_EXPERTS, ntp), 0)
        diag = lane == sub

        def as_row(col):
            return jnp.sum(jnp.where(diag, col, 0.0), axis=0, keepdims=True)

        zero = jnp.zeros((SUBLANES - 3, ntp), F32)
        meta_ref[...] = jnp.concatenate([as_row(ends[:, 0:1]), as_row(padded[:, 0:1]), as_row(cnt[:, 0:1]), zero],
                                        axis=0).astype(jnp.int32)

    eid = eid_ref[...]
    eidx = lax.broadcasted_iota(jnp.int32, (N_EXPERTS, tb), 0)
    hot = [eidx == eid[k:k + 1, :] for k in range(TOP_K)]
    member = jnp.zeros((N_EXPERTS, tb), F32)
    for k in range(TOP_K):
        member = member + hot[k].astype(F32)
    ri = lax.broadcasted_iota(jnp.int32, (tb, tb), 0)
    ci = lax.broadcasted_iota(jnp.int32, (tb, tb), 1)
    triu = (ri <= ci).astype(BF16)
    incl = jnp.dot(member.astype(BF16), triu, preferred_element_type=F32)
    slot = incl - member + carry_s[:, 0:1] + start_s[:, 0:1]
    rows = [jnp.sum(jnp.where(hot[k], slot, 0.0), axis=0, keepdims=True) for k in range(TOP_K)]
    zr = jnp.zeros_like(rows[0])
    pos_ref[...] = jnp.concatenate(rows + [zr] * (SUBLANES - TOP_K), axis=0).astype(jnp.int32)
    carry_s[...] = carry_s[...] + jnp.sum(member, axis=1, keepdims=True)


def _route(eid, counts):
    t = eid.shape[1]
    tb = TB_RANK
    ntp = LANES
    pos8, meta = pl.pallas_call(
        _route_kernel,
        grid=(t // tb,),
        in_specs=[pl.BlockSpec((SUBLANES, tb), lambda i: (0, i)),
                  pl.BlockSpec((N_EXPERTS, LANES), lambda i: (0, 0))],
        out_specs=[pl.BlockSpec((SUBLANES, tb), lambda i: (0, i)),
                   pl.BlockSpec((SUBLANES, ntp), lambda i: (0, 0))],
        out_shape=[jax.ShapeDtypeStruct((SUBLANES, t), jnp.int32),
                   jax.ShapeDtypeStruct((SUBLANES, ntp), jnp.int32)],
        scratch_shapes=[pltpu.VMEM((N_EXPERTS, LANES), F32), pltpu.VMEM((N_EXPERTS, LANES), F32)],
        compiler_params=_cparams(("arbitrary",)),
        name="route",
    )(eid, counts)
    return pos8, meta


def _sc_workers():
    info = plsc.get_sparse_core_info()
    return info.num_cores, info.num_cores * info.num_subcores


def _dispatch(h2, pos8, n_rows):
    t, d = h2.shape
    n_cores, n_workers = _sc_workers()
    tpw = t // n_workers
    ch = SC_SCATTER_CHUNK
    mesh = plsc.VectorSubcoreMesh(core_axis_name="c", subcore_axis_name="s")

    @functools.partial(
        pl.kernel, mesh=mesh,
        out_type=jax.ShapeDtypeStruct((n_rows, d), h2.dtype),
        scratch_types=[pltpu.VMEM((ch,), jnp.int32)] * TOP_K + [pltpu.VMEM((ch, d), h2.dtype),
                                                                pltpu.SemaphoreType.DMA])
    def scatter_rows(h_hbm, pos_hbm, out_hbm, i0, i1, i2, i3, rows_v, sem):
        idx = (i0, i1, i2, i3)
        base = (lax.axis_index("s") * n_cores + lax.axis_index("c")) * tpw

        @pl.loop(0, tpw // ch)
        def _(i):
            off = base + i * ch
            pltpu.sync_copy(h_hbm.at[pl.ds(off, ch)], rows_v)
            for k in range(TOP_K):
                pltpu.sync_copy(pos_hbm.at[k, pl.ds(off, ch)], idx[k])
            copies = [pltpu.async_copy(rows_v, out_hbm.at[idx[k]], sem) for k in range(TOP_K)]
            for cp in copies:
                cp.wait()

    return scatter_rows(h2, pos8)


def _gather_expert_rows(ys, pos8):
    _, d = ys.shape
    t = pos8.shape[1]
    n_cores, n_workers = _sc_workers()
    tpw = t // n_workers
    ch = SC_GATHER_CHUNK
    mesh = plsc.VectorSubcoreMesh(core_axis_name="c", subcore_axis_name="s")

    @functools.partial(
        pl.kernel, mesh=mesh,
        out_type=jax.ShapeDtypeStruct((TOP_K, t, d), ys.dtype),
        scratch_types=([pltpu.VMEM((ch,), jnp.int32)] * 2 + [pltpu.VMEM((ch, d), ys.dtype)] * 2
                       + [pltpu.SemaphoreType.DMA] * 4))
    def gather_rows(ys_hbm, pos_hbm, out_hbm, idx0, idx1, rows0, rows1, g0, g1, w0, w1):
        idx, rows, gsem, wsem = (idx0, idx1), (rows0, rows1), (g0, g1), (w0, w1)
        base = (lax.axis_index("s") * n_cores + lax.axis_index("c")) * tpw
        items = [(i, k) for i in range(tpw // ch) for k in range(TOP_K)]

        def gather(n):
            i, k = items[n]
            pltpu.sync_copy(pos_hbm.at[k, pl.ds(base + i * ch, ch)], idx[n % 2])
            return pltpu.async_copy(ys_hbm.at[idx[n % 2]], rows[n % 2], gsem[n % 2])

        def write(n):
            i, k = items[n]
            return pltpu.async_copy(rows[n % 2], out_hbm.at[k, pl.ds(base + i * ch, ch)], wsem[n % 2])

        gathers, writes = {}, {}
        for n in range(len(items)):
            if n >= 2:
                writes[n - 2].wait()
            gathers[n] = gather(n)
            if n >= 1:
                gathers[n - 1].wait()
                writes[n - 1] = write(n - 1)
        last = len(items) - 1
        gathers[last].wait()
        writes[last] = write(last)
        if last >= 1:
            writes[last - 1].wait()
        writes[last].wait()

    return gather_rows(ys, pos8)


def _pack_bf16_pairs(a):
    return _pack_rounded(a.astype(BF16).astype(F32))


def _pack_rounded(r):
    n = r.shape[1] // 2
    lo = pltpu.bitcast(r[:, :n], jnp.int32)
    hi = pltpu.bitcast(r[:, n:], jnp.int32)
    return jnp.bitwise_or(jnp.bitwise_and(hi, -65536), jnp.bitwise_and(lax.shift_right_logical(lo, 16), 65535))


def _unpack_bf16_pairs(p):
    lo = pltpu.bitcast(lax.shift_left(p, 16), F32)
    hi = pltpu.bitcast(jnp.bitwise_and(p, -65536), F32)
    return lo, hi


def _expert_kernel(meta_ref, xs_ref, wup_ref, bup_ref, wdn_ref, bdn_ref, ys_ref,
                   wup_s, wdn_s, xbuf, obuf, in_sem, out_sem):
    e = pl.program_id(0)
    tm = TM_MOE
    half = D_MODEL // 2
    pad = meta_ref[META_PAD, e]
    n_t = pad // tm
    row0 = meta_ref[META_END, e] - pad
    cnt = meta_ref[META_CNT, e]

    def rows(i):
        return pl.ds(pl.multiple_of(row0 + i * tm, tm), tm)

    def x_copy(i, slot):
        return pltpu.make_async_copy(xs_ref.at[rows(i)], xbuf.at[slot], in_sem.at[slot])

    def y_copy(i, slot):
        return pltpu.make_async_copy(obuf.at[slot], ys_ref.at[rows(i)], out_sem.at[slot])

    @pl.when(n_t > 0)
    def _():
        x_copy(0, 0).start()
        wup_s[...] = wup_ref[0, 0].astype(BF16)
        wdn_s[...] = wdn_ref[0, 0].astype(BF16)

        def tile(i, carry):
            slot = lax.rem(i, 2)
            x_copy(i, slot).wait()

            @pl.when(i + 1 < n_t)
            def _():
                x_copy(i + 1, 1 - slot).start()

            @pl.when(i >= 2)
            def _():
                y_copy(i - 2, slot).wait()

            row = lax.broadcasted_iota(jnp.int32, (tm, 1), 0)
            lo, hi = _unpack_bf16_pairs(jnp.where(row < cnt - i * tm, xbuf[slot], 0))
            z = (jnp.dot(lo.astype(BF16), wup_s[0:half, :], preferred_element_type=F32)
                 + jnp.dot(hi.astype(BF16), wup_s[half:, :], preferred_element_type=F32) + bup_ref[0, 0])
            glu = jnp.minimum(z[:, :D_EXPERT], SWIGLU_LIMIT)
            lin = jnp.clip(z[:, D_EXPERT:], -SWIGLU_LIMIT, SWIGLU_LIMIT)
            act = (glu * _sigmoid(SWIGLU_ALPHA * glu) * (lin + 1.0)).astype(BF16)
            y = jnp.dot(act, wdn_s[...], preferred_element_type=F32) + bdn_ref[0, 0]
            obuf[slot] = _pack_bf16_pairs(y)
            y_copy(i, slot).start()
            return carry

        lax.fori_loop(0, n_t, tile, 0)

        @pl.when(n_t >= 2)
        def _():
            y_copy(n_t - 2, lax.rem(n_t, 2)).wait()

        y_copy(n_t - 1, lax.rem(n_t - 1, 2)).wait()

    @pl.when(e == N_EXPERTS - 1)
    def _():
        obuf[0] = jnp.zeros((tm, half), jnp.int32)

        def fill(i, carry):
            cp = pltpu.make_async_copy(obuf.at[0], ys_ref.at[pl.ds(pl.multiple_of(i * tm, tm), tm)], out_sem.at[0])
            cp.start()
            cp.wait()
            return carry

        lax.fori_loop(meta_ref[META_END, N_EXPERTS - 1] // tm, ys_ref.shape[0] // tm, fill, 0)


def _expert_mlp(xs, meta, layer, w_up, b_up, w_down, b_down):
    n_rows, half = xs.shape
    d = 2 * half
    tm = TM_MOE
    f = w_up.shape[-1]
    b_up4 = b_up.reshape(DEPTH, N_EXPERTS, 1, f)
    b_dn4 = b_down.reshape(DEPTH, N_EXPERTS, 1, d)
    wsel = lambda e, m: (layer, e, 0, 0)
    grid_spec = pltpu.PrefetchScalarGridSpec(
        num_scalar_prefetch=1,
        grid=(N_EXPERTS,),
        in_specs=[pl.BlockSpec(memory_space=pl.ANY),
                  pl.BlockSpec((1, 1, d, f), wsel),
                  pl.BlockSpec((1, 1, 1, f), wsel),
                  pl.BlockSpec((1, 1, f // 2, d), wsel),
                  pl.BlockSpec((1, 1, 1, d), wsel)],
        out_specs=pl.BlockSpec(memory_space=pl.ANY),
        scratch_shapes=[pltpu.VMEM((d, f), BF16), pltpu.VMEM((f // 2, d), BF16),
                        pltpu.VMEM((2, tm, half), jnp.int32), pltpu.VMEM((2, tm, half), jnp.int32),
                        pltpu.SemaphoreType.DMA((2,)), pltpu.SemaphoreType.DMA((2,))],
    )
    return pl.pallas_call(
        _expert_kernel,
        grid_spec=grid_spec,
        out_shape=jax.ShapeDtypeStruct((n_rows, half), jnp.int32),
        compiler_params=_cparams(("arbitrary",)),
        name="expert_mlp",
    )(meta, xs, w_up, b_up4, w_down, b_dn4)


def _combine_kernel(rows_ref, prob_ref, x_ref, gate_ref, lng_ref, lnb_ref, o_ref):
    p = prob_ref[...]
    y = None
    for k in range(TOP_K):
        yk = p[:, k:k + 1] * jnp.concatenate(_unpack_bf16_pairs(rows_ref[k]), axis=1)
        y = yk if y is None else y + yk
    o_ref[0] = _layer_norm(DN_ALPHA * x_ref[0] + (1.0 + gate_ref[0]) * y, lng_ref[...], lnb_ref[...])


def _combine(rows, prob_c, x1, gate, ln_g, ln_b, part):
    bsz, s, d = x1.shape
    tm = TM_DISP
    nt = s // tm
    pb = bsz // MOE_PARTS
    b0 = part * pb
    return pl.pallas_call(
        _combine_kernel,
        grid=(pb, nt),
        in_specs=[pl.BlockSpec((TOP_K, tm, d // 2), lambda b, i: (0, b * nt + i, 0)),
                  pl.BlockSpec((tm, LANES), lambda b, i: ((b0 + b) * nt + i, 0)),
                  pl.BlockSpec((1, tm, d), lambda b, i: (b0 + b, i, 0)),
                  pl.BlockSpec((1, 1, d), lambda b, i: (b0 + b, 0, 0)),
                  pl.BlockSpec((1, d), lambda b, i: (0, 0)),
                  pl.BlockSpec((1, d), lambda b, i: (0, 0))],
        out_specs=pl.BlockSpec((1, tm, d), lambda b, i: (b0 + b, i, 0)),
        out_shape=jax.ShapeDtypeStruct((bsz, s, d), F32),
        input_output_aliases={2: 0},
        compiler_params=_cparams(("arbitrary", "arbitrary")),
        name="combine",
    )(rows, prob_c, x1, gate.reshape(bsz, 1, d), ln_g.reshape(1, d), ln_b.reshape(1, d))


def kernel(x, c, positions, ada_w, ada_b, w_in, lam_q1, lam_k1, lam_q2, lam_k2, da_norm_g, ml_conv_w, ml_conv_b,
           ml_w_q, ml_w_k, ml_gate_b, ml_norm_g, s5_a_re, s5_a_im, s5_log_dt, s5_b_re, s5_b_im, s5_c_re, s5_c_im,
           s5_d, s5_w_glu, w_out, ln_g, ln_b, w_router, b_router, w_up, b_up, w_down, b_down):
    bsz, s, d = x.shape
    t = bsz * s
    n_tiles_max = (t * TOP_K) // TM_MOE + N_EXPERTS
    n_rows = n_tiles_max * TM_MOE
    mod = _modulation(c, ada_w, ada_b)
    cos_t, sin_t = _rope_tables(positions)
    for l in range(DEPTH):
        shift, scale, gate = jnp.split(mod[2 * l], 3, axis=-1)
        q, k, v, mlx, mlv, mlo, g_t, g_c, s5u = _in_proj(x, shift, scale, cos_t, sin_t, w_in, l)
        lam_init = 0.8 - 0.6 * math.exp(-0.3 * l)
        lamv = jnp.stack([lam_q1[l], lam_k1[l], lam_q2[l], lam_k2[l]])
        y_da = _diff_attn(q, k, v, lamv, da_norm_g[l], lam_init)
        y_ml = _mlstm(mlx, mlv, mlo, g_t, g_c, ml_conv_w[l], ml_conv_b[l], ml_w_q[l], ml_w_k[l],
                      ml_gate_b[l], ml_norm_g[l])
        s5p = _s5_params(s5_a_re[l], s5_a_im[l], s5_log_dt[l], s5_b_re[l], s5_b_im[l], s5_c_re[l], s5_c_im[l],
                         s5_w_glu[l])
        y_s5 = _s5(s5u, bsz, s5p, s5_d[l])
        shift2, scale2, gate2 = jnp.split(mod[2 * l + 1], 3, axis=-1)
        x1, h2, eid, prob, counts = _out_proj(y_da, y_ml, y_s5, x, gate, ln_g[l, 0], ln_b[l, 0], shift2, scale2,
                                      w_out, l, w_router[l], b_router[l])
        pos8, meta = _route(eid, counts)
        xs = _dispatch(h2, pos8, n_rows)
        ys = _expert_mlp(xs, meta, l, w_up, b_up, w_down, b_down)
        tp = t // MOE_PARTS
        x = x1
        for part in range(MOE_PARTS):
            rows = _gather_expert_rows(ys, pos8[:, part * tp:(part + 1) * tp])
            x = _combine(rows, prob, x, gate2, ln_g[l, 1], ln_b[l, 1], part)
    return x
```

```python
import functools
import math

import jax
import jax.numpy as jnp
from jax import lax
from jax.experimental import pallas as pl
from jax.experimental.pallas import tpu as pltpu
from jax.experimental.pallas import tpu_sc as plsc

F32 = jnp.float32
BF16 = jnp.bfloat16
HIGHEST = lax.Precision.HIGHEST

D_MODEL = 1024
DEPTH = 2
DA_HEADS = 4
DA_HEAD_DIM = 64
DA_V_DIM = 2 * DA_HEAD_DIM
DA_WIDTH = DA_HEADS * DA_V_DIM
DA_QK_WIDTH = DA_HEADS * 2 * DA_HEAD_DIM
ROPE_THETA = 10000.0
ML_HEADS = 4
ML_HEAD_DIM = 64
ML_WIDTH = ML_HEADS * ML_HEAD_DIM
ML_CONV = 4
S5_GROUP = 16
S5_STATE = 64
S5_WIDTH = D_MODEL - DA_WIDTH - ML_WIDTH
S5_GROUPS = S5_WIDTH // S5_GROUP
S5_NSTATE = S5_GROUPS * S5_STATE
N_EXPERTS = 32
TOP_K = 4
D_EXPERT = D_MODEL
SWIGLU_LIMIT = 7.0
SWIGLU_ALPHA = 1.702
DN_ALPHA = (2 * DEPTH) ** 0.25
LN_EPS = 1e-5
NEG = -1e30

OFF_DA_K = DA_QK_WIDTH
OFF_DA_V = 2 * DA_QK_WIDTH
OFF_ML_X = OFF_DA_V + DA_WIDTH
OFF_ML_V = OFF_ML_X + ML_WIDTH
OFF_ML_O = OFF_ML_V + ML_WIDTH
OFF_ML_I = OFF_ML_O + ML_WIDTH
OFF_ML_F = OFF_ML_I + ML_HEADS
OFF_S5_U = OFF_ML_F + ML_HEADS
N_IN = OFF_S5_U + S5_WIDTH

LANES = 128
SUBLANES = 8
VMEM_LIMIT_BYTES = 56 * 1024 * 1024

TM_PROJ = 1024
TQ = 512
ML_CHUNK = 256
ML_NB = 4
S5_TC = 256
S5_UNROLL = 8
TB_RANK = 1024
TM_MOE = 256
TM_DISP = 512
MOE_PARTS = 1
SC_SCATTER_CHUNK = 128
SC_GATHER_CHUNK = 64
GATE_PAD = 8
VT_ROWS = DA_V_DIM + 16
Q_PRESCALE = DA_HEAD_DIM ** -0.5 * math.log2(math.e)


def _cparams(sem, vmem=VMEM_LIMIT_BYTES):
    return pltpu.CompilerParams(dimension_semantics=sem, vmem_limit_bytes=vmem)


def _sigmoid(x):
    return 1.0 / (1.0 + jnp.exp(-x))


def _mod_kernel(c_ref, w_ref, b_ref, o_ref):
    c = c_ref[...]
    ca = (c * _sigmoid(c)).astype(BF16)
    w = w_ref[0, 0].astype(BF16)
    o_ref[0] = jnp.dot(ca, w, preferred_element_type=F32) + b_ref[0]


def _modulation(c, ada_w, ada_b):
    nsub = ada_w.shape[1]
    nmod = ada_w.shape[0] * nsub
    bsz, d = c.shape
    e = ada_w.shape[-1]
    tn = 1024
    b = ada_b.reshape(nmod, 1, e)
    return pl.pallas_call(
        _mod_kernel,
        grid=(nmod, e // tn),
        in_specs=[pl.BlockSpec((bsz, d), lambda n, j: (0, 0)),
                  pl.BlockSpec((1, 1, d, tn), lambda n, j: (n // nsub, n % nsub, 0, j)),
                  pl.BlockSpec((1, 1, tn), lambda n, j: (n, 0, j))],
        out_specs=pl.BlockSpec((1, bsz, tn), lambda n, j: (n, 0, j)),
        out_shape=jax.ShapeDtypeStruct((nmod, bsz, e), F32),
        compiler_params=_cparams(("arbitrary", "arbitrary")),
        name="modulation",
    )(c, ada_w, b)


def _rope_kernel(pos_ref, cos_ref, sin_ref):
    nfreq = DA_HEAD_DIM // 2
    pos = pos_ref[0].astype(F32)
    fidx = lax.broadcasted_iota(jnp.int32, (nfreq, 1), 0).astype(F32)
    inv = jnp.exp(fidx * (-2.0 * math.log(ROPE_THETA) / DA_HEAD_DIM))
    ang = inv * pos
    reps = LANES // nfreq
    cos_t = jnp.concatenate([jnp.cos(ang)] * reps, axis=0).T
    sin_t = jnp.concatenate([jnp.sin(ang)] * reps, axis=0).T
    lane = lax.broadcasted_iota(jnp.int32, (1, LANES), 1)
    sign = jnp.where((lane % DA_HEAD_DIM) < nfreq, -1.0, 1.0)
    cos_ref[0] = cos_t
    sin_ref[0] = sin_t * sign


def _rope_tables(positions):
    bsz, s = positions.shape
    ts = 512
    pos3 = positions.reshape(bsz, 1, s)
    return pl.pallas_call(
        _rope_kernel,
        grid=(bsz, s // ts),
        in_specs=[pl.BlockSpec((1, 1, ts), lambda b, i: (b, 0, i))],
        out_specs=[pl.BlockSpec((1, ts, LANES), lambda b, i: (b, i, 0))] * 2,
        out_shape=[jax.ShapeDtypeStruct((bsz, s, LANES), F32)] * 2,
        compiler_params=_cparams(("arbitrary", "arbitrary")),
        name="rope_tables",
    )(pos3)


def _in_proj_kernel(x_ref, shift_ref, scale_ref, cos_ref, sin_ref, win_ref,
                    q_ref, k_ref, vt_ref, mlx_ref, mlv_ref, mlo_ref, gt_ref, gc_ref, s5u_ref,
                    wqk_ref, wvt_ref, wrest_ref, wgt_ref, wgc_ref):
    @pl.when(jnp.logical_and(pl.program_id(0) == 0, pl.program_id(1) == 0))
    def _():
        wqk_ref[...] = win_ref[0, 0:OFF_DA_V, :].T.astype(BF16)
        wvt_ref[...] = win_ref[0, OFF_DA_V:OFF_ML_X, :].astype(BF16)
        wrest_ref[:, 0:3 * ML_WIDTH] = win_ref[0, OFF_ML_X:OFF_ML_I, :].T.astype(BF16)
        wrest_ref[:, 3 * ML_WIDTH:] = win_ref[0, OFF_S5_U:N_IN, :].T.astype(BF16)
        wgt_ref[...] = win_ref[0, OFF_ML_I:OFF_S5_U, :].astype(BF16)
        gslab = win_ref[0, OFF_ML_I:OFF_ML_I + LANES, :].T
        glane = lax.broadcasted_iota(jnp.int32, (1, LANES), 1)
        wgc_ref[...] = jnp.where(glane < GATE_PAD, gslab, 0.0).astype(BF16)

    h = (x_ref[0] * (1.0 + scale_ref[0]) + shift_ref[0]).astype(BF16)
    cos = cos_ref[0]
    sin = sin_ref[0]
    lane = lax.broadcasted_iota(jnp.int32, (1, LANES), 1)
    lo_half = (lane % DA_HEAD_DIM) < DA_HEAD_DIM // 2
    half = DA_HEAD_DIM // 2

    def rope(t):
        fwd = pltpu.roll(t, half, 1)
        bwd = pltpu.roll(t, LANES - half, 1)
        partner = jnp.where(lo_half, bwd, fwd)
        return t * cos + partner * sin

    qk = jnp.dot(h, wqk_ref[...], preferred_element_type=F32)
    nslab = DA_QK_WIDTH // LANES
    for c in range(nslab):
        q_ref[0, :, c * LANES:(c + 1) * LANES] = (
            rope(qk[:, c * LANES:(c + 1) * LANES]) * Q_PRESCALE).astype(BF16)
        k_ref[0, :, c * LANES:(c + 1) * LANES] = rope(
            qk[:, DA_QK_WIDTH + c * LANES:DA_QK_WIDTH + (c + 1) * LANES]).astype(BF16)

    vt = lax.dot_general(wvt_ref[...], h, (((1,), (1,)), ((), ())), preferred_element_type=F32)
    tm = h.shape[0]
    for hh in range(DA_HEADS):
        for jj in range(tm // TQ):
            vt_ref[0, hh, jj, 0:DA_V_DIM, :] = vt[hh * DA_V_DIM:(hh + 1) * DA_V_DIM,
                                                  jj * TQ:(jj + 1) * TQ].astype(BF16)
            vt_ref[0, hh, jj, DA_V_DIM:VT_ROWS, :] = jnp.ones((VT_ROWS - DA_V_DIM, TQ), BF16)

    r = jnp.dot(h, wrest_ref[...], preferred_element_type=F32)
    o = 0
    mlx_ref[0] = r[:, o:o + ML_WIDTH].astype(BF16); o += ML_WIDTH
    mlv_ref[0] = r[:, o:o + ML_WIDTH].astype(BF16); o += ML_WIDTH
    mlo_ref[0] = r[:, o:o + ML_WIDTH].astype(BF16); o += ML_WIDTH
    s5u_ref[...] = r[:, o:o + S5_WIDTH]
    gt = lax.dot_general(wgt_ref[...], h, (((1,), (1,)), ((), ())), preferred_element_type=F32)
    for c in range(h.shape[0] // ML_CHUNK):
        gt_ref[0, c] = gt[:, c * ML_CHUNK:(c + 1) * ML_CHUNK]
    gc_ref[0] = jnp.dot(h, wgc_ref[...], preferred_element_type=F32)


def _in_proj(x, shift, scale, cos_t, sin_t, w_in, layer):
    bsz, s, d = x.shape
    tm = TM_PROJ
    nrest = 3 * ML_WIDTH + S5_WIDTH
    shift3 = shift.reshape(bsz, 1, d)
    scale3 = scale.reshape(bsz, 1, d)
    tok = lambda b, i: (b, i, 0)
    per_b = lambda b, i: (b, 0, 0)
    out_shapes = [
        jax.ShapeDtypeStruct((bsz, s, DA_QK_WIDTH), BF16),
        jax.ShapeDtypeStruct((bsz, s, DA_QK_WIDTH), BF16),
        jax.ShapeDtypeStruct((bsz, DA_HEADS, s // TQ, VT_ROWS, TQ), BF16),
        jax.ShapeDtypeStruct((bsz, s, ML_WIDTH), BF16),
        jax.ShapeDtypeStruct((bsz, s, ML_WIDTH), BF16),
        jax.ShapeDtypeStruct((bsz, s, ML_WIDTH), BF16),
        jax.ShapeDtypeStruct((bsz, s // ML_CHUNK, GATE_PAD, ML_CHUNK), F32),
        jax.ShapeDtypeStruct((bsz, s, LANES), F32),
        jax.ShapeDtypeStruct((s, bsz * S5_WIDTH), F32),
    ]
    out_specs = [
        pl.BlockSpec((1, tm, DA_QK_WIDTH), tok),
        pl.BlockSpec((1, tm, DA_QK_WIDTH), tok),
        pl.BlockSpec((1, DA_HEADS, tm // TQ, VT_ROWS, TQ), lambda b, i: (b, 0, i, 0, 0)),
        pl.BlockSpec((1, tm, ML_WIDTH), tok),
        pl.BlockSpec((1, tm, ML_WIDTH), tok),
        pl.BlockSpec((1, tm, ML_WIDTH), tok),
        pl.BlockSpec((1, tm // ML_CHUNK, GATE_PAD, ML_CHUNK), lambda b, i: (b, i, 0, 0)),
        pl.BlockSpec((1, tm, LANES), tok),
        pl.BlockSpec((tm, S5_WIDTH), lambda b, i: (i, b)),
    ]
    return pl.pallas_call(
        _in_proj_kernel,
        grid=(bsz, s // tm),
        in_specs=[pl.BlockSpec((1, tm, d), tok),
                  pl.BlockSpec((1, 1, d), per_b),
                  pl.BlockSpec((1, 1, d), per_b),
                  pl.BlockSpec((1, tm, LANES), tok),
                  pl.BlockSpec((1, tm, LANES), tok),
                  pl.BlockSpec((1, N_IN, d), lambda b, i: (layer, 0, 0))],
        out_specs=out_specs,
        out_shape=out_shapes,
        scratch_shapes=[pltpu.VMEM((d, OFF_DA_V), BF16), pltpu.VMEM((DA_WIDTH, d), BF16),
                        pltpu.VMEM((d, nrest), BF16), pltpu.VMEM((GATE_PAD, d), BF16),
                        pltpu.VMEM((d, LANES), BF16)],
        compiler_params=_cparams(("arbitrary", "arbitrary")),
        name="in_proj",
    )(x, shift3, scale3, cos_t, sin_t, jnp.swapaxes(w_in, 1, 2))


def _diff_attn_kernel(lam_init, lamv_ref, gain_ref, q_ref, k_ref, vt_ref, o_ref, acc_s, m_s):
    qi = pl.program_id(2)
    tq = q_ref.shape[1]
    lane = lax.broadcasted_iota(jnp.int32, (1, LANES), 1)
    first = lane < DA_HEAD_DIM
    q = q_ref[0]
    zero = jnp.zeros_like(q)
    qm = (jnp.where(first, q, zero), jnp.where(first, zero, q))
    acc_s[...] = jnp.zeros_like(acc_s)
    m_s[...] = jnp.full(m_s.shape, NEG, F32)

    def step(j, nblk, masked):
        tk = nblk * tq
        kb = k_ref[0, pl.ds(pl.multiple_of(j * tq, tq), tk), :]
        vtb = vt_ref[0, 0, j] if nblk == 1 else jnp.concatenate([vt_ref[0, 0, j + b] for b in range(nblk)], axis=1)
        for c in range(2):
            st = lax.dot_general(kb, qm[c], (((1,), (1,)), ((), ())), preferred_element_type=F32)
            if masked:
                key_i = lax.broadcasted_iota(jnp.int32, (tk, tq), 0) - (tk - tq)
                qry_i = lax.broadcasted_iota(jnp.int32, (tk, tq), 1)
                st = jnp.where(key_i <= qry_i, st, NEG)
            m_prev = m_s[c]
            m_new = jnp.maximum(m_prev, jnp.max(st, axis=0, keepdims=True))
            alpha = jnp.exp2(m_prev - m_new)
            p = jnp.exp2(st - m_new).astype(BF16)
            acc_s[c] = alpha * acc_s[c] + jnp.dot(vtb, p, preferred_element_type=F32)
            m_s[c] = m_new

    def body(jj, carry):
        step(2 * jj, 2, False)
        return carry

    lax.fori_loop(0, qi // 2, body, 0)

    @pl.when(qi % 2 == 1)
    def _():
        step(qi - 1, 2, True)

    @pl.when(qi % 2 == 0)
    def _():
        step(qi, 1, True)

    outs = []
    for c in range(2):
        acc = acc_s[c]
        outs.append(acc[:DA_V_DIM] / acc[DA_V_DIM:DA_V_DIM + 1])

    lamv = lamv_ref[...]
    lam = (jnp.exp(jnp.sum(lamv[0:1] * lamv[1:2], axis=1, keepdims=True))
           - jnp.exp(jnp.sum(lamv[2:3] * lamv[3:4], axis=1, keepdims=True)) + lam_init)
    ot = outs[0] - lam * outs[1]
    ms = jnp.mean(ot * ot, axis=0, keepdims=True)
    ot = ot * (lax.rsqrt(ms + LN_EPS) * (1.0 - lam_init))
    o_ref[0] = (ot.T * gain_ref[...]).astype(o_ref.dtype)


def _diff_attn(q, k, vt, lamv, gain, lam_init):
    bsz, s, _ = q.shape
    tq = TQ
    nq = s // tq
    return pl.pallas_call(
        functools.partial(_diff_attn_kernel, lam_init),
        grid=(bsz, DA_HEADS, nq),
        in_specs=[pl.BlockSpec((4, DA_HEAD_DIM), lambda b, h, i: (0, 0)),
                  pl.BlockSpec((1, DA_V_DIM), lambda b, h, i: (0, 0)),
                  pl.BlockSpec((1, tq, DA_V_DIM), lambda b, h, i: (b, i, h)),
                  pl.BlockSpec((1, s, DA_V_DIM), lambda b, h, i: (b, 0, h)),
                  pl.BlockSpec((1, 1, nq, VT_ROWS, tq), lambda b, h, i: (b, h, 0, 0, 0))],
        out_specs=pl.BlockSpec((1, tq, DA_V_DIM), lambda b, h, i: (b, i, h)),
        out_shape=jax.ShapeDtypeStruct((bsz, s, DA_WIDTH), BF16),
        scratch_shapes=[pltpu.VMEM((2, VT_ROWS, tq), F32), pltpu.VMEM((2, 1, tq), F32)],
        compiler_params=_cparams(("arbitrary", "arbitrary", "arbitrary")),
        name="diff_attn",
    )(lamv, gain.reshape(1, DA_V_DIM), q, k, vt)


def _log_sigmoid(x):
    return jnp.minimum(x, 0.0) - jnp.log(1.0 + jnp.exp(-jnp.abs(x)))


def _split3(a):
    hi = a.astype(BF16)
    r1 = a - hi.astype(F32)
    mid = r1.astype(BF16)
    lo = (r1 - mid.astype(F32)).astype(BF16)
    return hi, mid, lo


def _mlstm_kernel(x_ref, v_ref, o_ref, gt_ref, gc_ref, cw_ref, cb_ref, wq_ref, wkt_ref, gbt_ref, gbc_ref,
                  ng_ref, hmean_ref, y_ref, xc_s, c_s, m_s):
    nb, s = x_ref.shape[0], x_ref.shape[1]
    L = ML_CHUNK
    H, dh = ML_HEADS, ML_HEAD_DIM
    nc = s // L
    cw = cw_ref[...]
    row = lax.broadcasted_iota(jnp.int32, (s, 1), 0)
    for bi in range(nb):
        x = x_ref[bi].astype(F32)
        xc = x * cw[ML_CONV - 1:ML_CONV]
        for j in range(1, ML_CONV):
            xs = jnp.where(row >= j, pltpu.roll(x, j, 0), 0.0)
            xc = xc + xs * cw[ML_CONV - 1 - j:ML_CONV - j]
        xc = xc + cb_ref[...]
        xc_s[bi] = (xc * _sigmoid(xc)).astype(BF16)

    c_s[...] = jnp.zeros_like(c_s)
    m_s[...] = jnp.full(m_s.shape, NEG, F32)

    ri = lax.broadcasted_iota(jnp.int32, (L, L), 0)
    ci = lax.broadcasted_iota(jnp.int32, (L, L), 1)
    causal = ci <= ri
    tril = causal.astype(BF16)
    triu = (ri <= ci).astype(BF16)
    lane = lax.broadcasted_iota(jnp.int32, (1, dh), 1)
    one_hot0 = jnp.broadcast_to((lane == 0).astype(BF16), (L, dh))

    def chunk_one(bi, ci_, t0):
        xcc = xc_s[bi, pl.ds(t0, L), :]
        qc = jnp.dot(xcc, wq_ref[...], preferred_element_type=F32).astype(BF16)
        ktc = lax.dot_general(wkt_ref[...], xcc, (((1,), (1,)), ((), ())),
                              preferred_element_type=F32)
        g_rows = gt_ref[bi, ci_] + gbt_ref[...]
        g_cols = gc_ref[bi, pl.ds(t0, L), :] + gbc_ref[...]
        lf_rows = _log_sigmoid(g_rows)
        lf_cols = _log_sigmoid(g_cols)
        r3 = jnp.dot(jnp.concatenate(_split3(lf_rows), axis=0), triu, preferred_element_type=F32)
        b_rows = r3[0:GATE_PAD] + r3[GATE_PAD:2 * GATE_PAD] + r3[2 * GATE_PAD:]
        c3 = jnp.dot(tril, jnp.concatenate(_split3(lf_cols), axis=1), preferred_element_type=F32)
        b_cols = c3[:, 0:LANES] + c3[:, LANES:2 * LANES] + c3[:, 2 * LANES:]
        vch = v_ref[bi, pl.ds(t0, L), :]
        och = o_ref[bi, pl.ds(t0, L), :].astype(F32)
        hs = []
        for h in range(H):
            br = b_rows[H + h:H + h + 1, :]
            ir = g_rows[h:h + 1, :]
            bc = b_cols[:, H + h:H + h + 1]
            m_prev = m_s[bi, h]
            log_d = jnp.where(causal, bc - br + ir, NEG)
            inter = bc + m_prev
            mx = jnp.maximum(inter, jnp.max(log_d, axis=1, keepdims=True))
            dmat = jnp.exp(log_d - mx)
            dec = jnp.exp(inter - mx)
            qh = qc[:, h * dh:(h + 1) * dh]
            kth = ktc[h * dh:(h + 1) * dh, :]
            vaug = jnp.concatenate([vch[:, h * dh:(h + 1) * dh], one_hot0], axis=1)
            sm = (jnp.dot(qh, kth.astype(BF16), preferred_element_type=F32) * dmat).astype(BF16)
            c_prev = c_s[bi, h]
            na = (jnp.dot(sm, vaug, preferred_element_type=F32)
                  + dec * jnp.dot(qh, c_prev.astype(BF16), preferred_element_type=F32))
            den = na[:, dh:dh + 1]
            hs.append(na[:, :dh] / jnp.maximum(jnp.abs(den), jnp.exp(-mx)))
            g_tot = br[:, L - 1:L]
            a_row = g_tot - br + ir
            m_new = jnp.maximum(g_tot + m_prev, jnp.max(a_row, axis=1, keepdims=True))
            decay = jnp.exp(g_tot + m_prev - m_new)
            w_row = jnp.exp(a_row - m_new)
            kw = (kth * w_row).astype(BF16)
            c_s[bi, h] = decay * c_prev + jnp.dot(kw, vaug, preferred_element_type=F32)
            m_s[bi, h] = m_new
        hcat = jnp.concatenate(hs, axis=1)
        m3 = jnp.dot(jnp.concatenate(_split3(hcat * hcat), axis=0), hmean_ref[...],
                     preferred_element_type=F32)
        ms = m3[0:L] + m3[L:2 * L] + m3[2 * L:]
        y = hcat * lax.rsqrt(ms + LN_EPS) * ng_ref[...] * _sigmoid(och)
        y_ref[bi, pl.ds(t0, L), :] = y.astype(y_ref.dtype)

    def chunk(ci_, _):
        t0 = pl.multiple_of(ci_ * L, L)
        for bi in range(nb):
            chunk_one(bi, ci_, t0)
        return 0

    lax.fori_loop(0, nc, chunk, 0)


def _mlstm(mlx, mlv, mlo, g_t, g_c, conv_w, conv_b, w_q, w_k, gate_b, norm_g):
    bsz, s, _ = mlx.shape
    H, dh = ML_HEADS, ML_HEAD_DIM
    eye = jnp.eye(H, dtype=F32)
    wq_bd = jnp.einsum('hde,hg->hdge', w_q, eye).reshape(ML_WIDTH, ML_WIDTH).astype(BF16)
    wk_bd = jnp.einsum('hde,hg->hdge', w_k * (dh ** -0.5), eye).reshape(ML_WIDTH, ML_WIDTH)
    wkt_bd = wk_bd.T.astype(BF16)
    gbt = gate_b.reshape(GATE_PAD, 1)
    gbc = jnp.pad(gate_b.reshape(1, GATE_PAD), ((0, 0), (0, LANES - GATE_PAD)))
    hmean = jnp.kron(eye, jnp.full((dh, dh), 1.0 / dh, F32)).astype(BF16)
    nc = s // ML_CHUNK
    tok = lambda b: (b, 0, 0)
    c2 = lambda b: (0, 0)
    nb = ML_NB
    return pl.pallas_call(
        _mlstm_kernel,
        grid=(bsz // nb,),
        in_specs=[pl.BlockSpec((nb, s, ML_WIDTH), tok),
                  pl.BlockSpec((nb, s, ML_WIDTH), tok),
                  pl.BlockSpec((nb, s, ML_WIDTH), tok),
                  pl.BlockSpec((nb, nc, GATE_PAD, ML_CHUNK), lambda b: (b, 0, 0, 0)),
                  pl.BlockSpec((nb, s, LANES), tok),
                  pl.BlockSpec((ML_CONV, ML_WIDTH), c2),
                  pl.BlockSpec((1, ML_WIDTH), c2),
                  pl.BlockSpec((ML_WIDTH, ML_WIDTH), c2),
                  pl.BlockSpec((ML_WIDTH, ML_WIDTH), c2),
                  pl.BlockSpec((GATE_PAD, 1), c2),
                  pl.BlockSpec((1, LANES), c2),
                  pl.BlockSpec((1, ML_WIDTH), c2),
                  pl.BlockSpec((ML_WIDTH, ML_WIDTH), c2)],
        out_specs=pl.BlockSpec((nb, s, ML_WIDTH), tok),
        out_shape=jax.ShapeDtypeStruct((bsz, s, ML_WIDTH), BF16),
        scratch_shapes=[pltpu.VMEM((nb, s, ML_WIDTH), BF16),
                        pltpu.VMEM((nb, H, dh, LANES), F32),
                        pltpu.VMEM((nb, H, 1, 1), F32)],
        compiler_params=_cparams(("arbitrary",)),
        name="mlstm",
    )(mlx, mlv, mlo, g_t, g_c, conv_w, conv_b.reshape(1, ML_WIDTH), wq_bd, wkt_bd, gbt, gbc,
      norm_g.reshape(1, ML_WIDTH), hmean)


def _gelu_tanh(x):
    return 0.5 * x * (1.0 + jnp.tanh(math.sqrt(2.0 / math.pi) * (x + 0.044715 * (x * x * x))))


def _s5_kernel(u_ref, are_ref, aim_ref, bcat_ref, ccat_ref, d_ref, wglu_ref, y_ref, xs_s, st_s):
    tc, bsz, w = u_ref.shape
    n = S5_NSTATE

    @pl.when(pl.program_id(0) == 0)
    def _():
        st_s[...] = jnp.zeros_like(st_s)

    u = u_ref[...].reshape(tc * bsz, w)
    xs_s[...] = jnp.dot(u.astype(BF16), bcat_ref[...], preferred_element_type=F32).reshape(tc, bsz, 2 * n)
    a_re = jnp.broadcast_to(are_ref[...], (bsz, n))
    a_im = jnp.broadcast_to(aim_ref[...], (bsz, n))

    def step(t, carry):
        x_re, x_im = carry
        bu = xs_s[t]
        n_re = a_re * x_re - a_im * x_im + bu[:, :n]
        n_im = a_re * x_im + a_im * x_re + bu[:, n:]
        xs_s[t] = jnp.concatenate([n_re, n_im], axis=1)
        return n_re, n_im

    x_re, x_im = lax.fori_loop(0, tc, step, (st_s[0], st_s[1]), unroll=S5_UNROLL)
    st_s[0] = x_re
    st_s[1] = x_im

    xs = xs_s[...].reshape(tc * bsz, 2 * n).astype(BF16)
    y = jnp.dot(xs, ccat_ref[...], preferred_element_type=F32) + d_ref[...] * u
    z = jnp.dot(_gelu_tanh(y).astype(BF16), wglu_ref[...], preferred_element_type=F32)
    out = z[:, :w] * _sigmoid(z[:, w:])
    y_ref[...] = out.reshape(tc, bsz, w).astype(y_ref.dtype)


def _s5_params(a_re, a_im, log_dt, b_re, b_im, c_re, c_im, w_glu):
    G, P, Hc = S5_GROUPS, S5_STATE, S5_GROUP
    dt = jnp.exp(log_dt)[:, None]
    mag = jnp.exp(a_re * dt)
    ab_re = mag * jnp.cos(a_im * dt)
    ab_im = mag * jnp.sin(a_im * dt)
    nr, ni = ab_re - 1.0, ab_im
    den = a_re * a_re + a_im * a_im
    fr = (nr * a_re + ni * a_im) / den
    fi = (ni * a_re - nr * a_im) / den
    bb_re = fr[..., None] * b_re - fi[..., None] * b_im
    bb_im = fr[..., None] * b_im + fi[..., None] * b_re
    eye = jnp.eye(G, dtype=F32)
    bd = lambda t, sub: jnp.einsum(sub, t, eye)
    bre = bd(bb_re, 'gph,gk->ghkp').reshape(G * Hc, G * P)
    bim = bd(bb_im, 'gph,gk->ghkp').reshape(G * Hc, G * P)
    bcat = jnp.concatenate([bre, bim], axis=1).astype(BF16)
    cre = bd(c_re, 'ghp,gk->gpkh').reshape(G * P, G * Hc)
    cim = bd(c_im, 'ghp,gk->gpkh').reshape(G * P, G * Hc)
    ccat = jnp.concatenate([cre, -cim], axis=0).astype(BF16)
    wv = bd(w_glu[:, :, :Hc], 'ghj,gk->ghkj').reshape(G * Hc, G * Hc)
    wg = bd(w_glu[:, :, Hc:], 'ghj,gk->ghkj').reshape(G * Hc, G * Hc)
    wglu = jnp.concatenate([wv, wg], axis=1).astype(BF16)
    return ab_re.reshape(1, G * P), ab_im.reshape(1, G * P), bcat, ccat, wglu


def _s5(u_tm, bsz, params, d_skip):
    s = u_tm.shape[0]
    w = S5_WIDTH
    n = S5_NSTATE
    are, aim, bcat, ccat, wglu = params
    u3 = u_tm.reshape(s, bsz, w)
    tc = S5_TC
    c2 = lambda i: (0, 0)
    y = pl.pallas_call(
        _s5_kernel,
        grid=(s // tc,),
        in_specs=[pl.BlockSpec((tc, bsz, w), lambda i: (i, 0, 0)),
                  pl.BlockSpec((1, n), c2),
                  pl.BlockSpec((1, n), c2),
                  pl.BlockSpec((w, 2 * n), c2),
                  pl.BlockSpec((2 * n, w), c2),
                  pl.BlockSpec((1, w), c2),
                  pl.BlockSpec((w, 2 * w), c2)],
        out_specs=pl.BlockSpec((tc, bsz, w), lambda i: (i, 0, 0)),
        out_shape=jax.ShapeDtypeStruct((s, bsz, w), F32),
        scratch_shapes=[pltpu.VMEM((tc, bsz, 2 * n), F32),
                        pltpu.VMEM((2, bsz, n), F32)],
        compiler_params=_cparams(("arbitrary",)),
        name="s5",
    )(u3, are, aim, bcat, ccat, d_skip.reshape(1, w), wglu)
    return y.reshape(s, bsz * w)


def _layer_norm(z, g, b):
    mu = jnp.mean(z, axis=1, keepdims=True)
    zc = z - mu
    var = jnp.mean(zc * zc, axis=1, keepdims=True)
    return zc * lax.rsqrt(var + LN_EPS) * g + b


def _out_proj_kernel(yda_ref, yml_ref, ys5_ref, x_ref, gate_ref, lng_ref, lnb_ref, shift_ref, scale_ref,
                     wout32_ref, wrt_ref, brt_ref, x1_ref, h2_ref, eid_ref, prob_ref, cnt_ref, wout_ref):
    first_step = jnp.logical_and(pl.program_id(0) == 0, pl.program_id(1) == 0)

    @pl.when(first_step)
    def _():
        wout_ref[...] = wout32_ref[0].astype(BF16)

    y = jnp.dot(yda_ref[0], wout_ref[0:DA_WIDTH, :], preferred_element_type=F32)
    y = y + jnp.dot(yml_ref[0], wout_ref[DA_WIDTH:DA_WIDTH + ML_WIDTH, :], preferred_element_type=F32)
    y = y + jnp.dot(ys5_ref[...].astype(BF16), wout_ref[DA_WIDTH + ML_WIDTH:, :], preferred_element_type=F32)
    x1 = _layer_norm(DN_ALPHA * x_ref[0] + (1.0 + gate_ref[0]) * y, lng_ref[...], lnb_ref[...])
    x1_ref[0] = x1
    h2 = x1 * (1.0 + scale_ref[0]) + shift_ref[0]
    h_hi = h2.astype(BF16)
    h_hi32 = h_hi.astype(F32)
    h2_ref[...] = _pack_rounded(h_hi32)
    h_lo = (h2 - h_hi32).astype(BF16)
    nt_dot = lambda a, b: lax.dot_general(a, b, (((1,), (1,)), ((), ())), preferred_element_type=F32)
    by_hi = nt_dot(wrt_ref[...], h_hi)
    logits = (by_hi[:N_EXPERTS] + by_hi[N_EXPERTS:] + nt_dot(wrt_ref[0:N_EXPERTS, :], h_lo)
              + brt_ref[...])
    eidx = lax.broadcasted_iota(jnp.int32, logits.shape, 0)
    vals, ids = [], []
    for _ in range(TOP_K):
        mx = jnp.max(logits, axis=0, keepdims=True)
        sel = jnp.min(jnp.where(logits == mx, eidx, N_EXPERTS), axis=0, keepdims=True)
        vals.append(mx)
        ids.append(sel)
        logits = jnp.where(eidx == sel, -jnp.inf, logits)
    ex = [jnp.exp(v - vals[0]) for v in vals]
    tot = ex[0] + ex[1] + ex[2] + ex[3]
    zi = jnp.zeros_like(ids[0])
    eid_ref[...] = jnp.concatenate(ids + [zi] * (SUBLANES - TOP_K), axis=0)

    @pl.when(first_step)
    def _():
        cnt_ref[...] = jnp.zeros_like(cnt_ref)

    member = jnp.zeros(logits.shape, F32)
    for sel in ids:
        member = member + (eidx == sel).astype(F32)
    cnt_ref[...] = cnt_ref[...] + jnp.sum(member, axis=1, keepdims=True)
    zf = jnp.zeros((LANES - TOP_K, tot.shape[1]), F32)
    prob_ref[...] = jnp.concatenate([e / tot for e in ex] + [zf], axis=0).T


def _out_proj(y_da, y_ml, y_s5, x, gate, ln_g, ln_b, shift2, scale2, w_out, layer, w_router_l, b_router_l):
    bsz, s, d = x.shape
    tm = TM_PROJ
    nt = s // tm
    tok = lambda b, i: (b, i, 0)
    per_b = lambda b, i: (b, 0, 0)
    c2 = lambda b, i: (0, 0)
    r3 = lambda a: a.reshape(bsz, 1, d)
    flat = lambda b, i: (0, b * nt + i)
    wr_t = w_router_l.T
    wr_hi = wr_t.astype(BF16)
    wr_lo = (wr_t - wr_hi.astype(F32)).astype(BF16)
    return pl.pallas_call(
        _out_proj_kernel,
        grid=(bsz, nt),
        in_specs=[pl.BlockSpec((1, tm, DA_WIDTH), tok),
                  pl.BlockSpec((1, tm, ML_WIDTH), tok),
                  pl.BlockSpec((tm, S5_WIDTH), lambda b, i: (i, b)),
                  pl.BlockSpec((1, tm, d), tok),
                  pl.BlockSpec((1, 1, d), per_b),
                  pl.BlockSpec((1, d), c2),
                  pl.BlockSpec((1, d), c2),
                  pl.BlockSpec((1, 1, d), per_b),
                  pl.BlockSpec((1, 1, d), per_b),
                  pl.BlockSpec((1, d, d), lambda b, i: (layer, 0, 0)),
                  pl.BlockSpec((2 * N_EXPERTS, d), c2),
                  pl.BlockSpec((N_EXPERTS, 1), c2)],
        out_specs=[pl.BlockSpec((1, tm, d), tok),
                   pl.BlockSpec((tm, d // 2), lambda b, i: (b * nt + i, 0)),
                   pl.BlockSpec((SUBLANES, tm), flat),
                   pl.BlockSpec((tm, LANES), lambda b, i: (b * nt + i, 0)),
                   pl.BlockSpec((N_EXPERTS, LANES), c2)],
        out_shape=[jax.ShapeDtypeStruct((bsz, s, d), F32),
                   jax.ShapeDtypeStruct((bsz * s, d // 2), jnp.int32),
                   jax.ShapeDtypeStruct((SUBLANES, bsz * s), jnp.int32),
                   jax.ShapeDtypeStruct((bsz * s, LANES), F32),
                   jax.ShapeDtypeStruct((N_EXPERTS, LANES), F32)],
        scratch_shapes=[pltpu.VMEM((d, d), BF16)],
        compiler_params=_cparams(("arbitrary", "arbitrary")),
        name="out_proj",
    )(y_da, y_ml, y_s5, x, r3(gate), ln_g.reshape(1, d), ln_b.reshape(1, d), r3(shift2), r3(scale2),
      w_out, jnp.concatenate([wr_hi, wr_lo], axis=0), b_router_l.reshape(N_EXPERTS, 1))


META_END, META_PAD, META_CNT = 0, 1, 2


def _route_kernel(eid_ref, cnt_ref, pos_ref, meta_ref, carry_s, start_s):
    i = pl.program_id(0)
    tb = eid_ref.shape[1]
    ntp = meta_ref.shape[1]
    tm = TM_MOE

    @pl.when(i == 0)
    def _():
        cnt = cnt_ref[...]
        padded = jnp.floor((cnt + (tm - 1)) * (1.0 / tm)) * tm
        er = lax.broadcasted_iota(jnp.int32, (N_EXPERTS, N_EXPERTS), 0)
        ec = lax.broadcasted_iota(jnp.int32, (N_EXPERTS, N_EXPERTS), 1)
        ends = jnp.dot((ec <= er).astype(F32), padded, preferred_element_type=F32, precision=HIGHEST)
        start_s[...] = ends - padded
        carry_s[...] = jnp.zeros_like(carry_s)
        lane = lax.broadcasted_iota(jnp.int32, (N_EXPERTS, ntp), 1)
        sub = lax.broadcasted_iota(jnp.int32, (N_EXPERTS, ntp), 0)
        diag = lane == sub

        def as_row(col):
            return jnp.sum(jnp.where(diag, col, 0.0), axis=0, keepdims=True)

        zero = jnp.zeros((SUBLANES - 3, ntp), F32)
        meta_ref[...] = jnp.concatenate([as_row(ends[:, 0:1]), as_row(padded[:, 0:1]), as_row(cnt[:, 0:1]), zero],
                                        axis=0).astype(jnp.int32)

    eid = eid_ref[...]
    eidx = lax.broadcasted_iota(jnp.int32, (N_EXPERTS, tb), 0)
    hot = [eidx == eid[k:k + 1, :] for k in range(TOP_K)]
    member = jnp.zeros((N_EXPERTS, tb), F32)
    for k in range(TOP_K):
        member = member + hot[k].astype(F32)
    ri = lax.broadcasted_iota(jnp.int32, (tb, tb), 0)
    ci = lax.broadcasted_iota(jnp.int32, (tb, tb), 1)
    triu = (ri <= ci).astype(BF16)
    incl = jnp.dot(member.astype(BF16), triu, preferred_element_type=F32)
    slot = incl - member + carry_s[:, 0:1] + start_s[:, 0:1]
    rows = [jnp.sum(jnp.where(hot[k], slot, 0.0), axis=0, keepdims=True) for k in range(TOP_K)]
    zr = jnp.zeros_like(rows[0])
    pos_ref[...] = jnp.concatenate(rows + [zr] * (SUBLANES - TOP_K), axis=0).astype(jnp.int32)
    carry_s[...] = carry_s[...] + jnp.sum(member, axis=1, keepdims=True)


def _route(eid, counts):
    t = eid.shape[1]
    tb = TB_RANK
    ntp = LANES
    pos8, meta = pl.pallas_call(
        _route_kernel,
        grid=(t // tb,),
        in_specs=[pl.BlockSpec((SUBLANES, tb), lambda i: (0, i)),
                  pl.BlockSpec((N_EXPERTS, LANES), lambda i: (0, 0))],
        out_specs=[pl.BlockSpec((SUBLANES, tb), lambda i: (0, i)),
                   pl.BlockSpec((SUBLANES, ntp), lambda i: (0, 0))],
        out_shape=[jax.ShapeDtypeStruct((SUBLANES, t), jnp.int32),
                   jax.ShapeDtypeStruct((SUBLANES, ntp), jnp.int32)],
        scratch_shapes=[pltpu.VMEM((N_EXPERTS, LANES), F32), pltpu.VMEM((N_EXPERTS, LANES), F32)],
        compiler_params=_cparams(("arbitrary",)),
        name="route",
    )(eid, counts)
    return pos8, meta


def _sc_workers():
    info = plsc.get_sparse_core_info()
    return info.num_cores, info.num_cores * info.num_subcores


def _dispatch(h2, pos8, n_rows):
    t, d = h2.shape
    n_cores, n_workers = _sc_workers()
    tpw = t // n_workers
    ch = SC_SCATTER_CHUNK
    mesh = plsc.VectorSubcoreMesh(core_axis_name="c", subcore_axis_name="s")

    @functools.partial(
        pl.kernel, mesh=mesh,
        out_type=jax.ShapeDtypeStruct((n_rows, d), h2.dtype),
        scratch_types=[pltpu.VMEM((ch,), jnp.int32)] * TOP_K + [pltpu.VMEM((ch, d), h2.dtype),
                                                                pltpu.SemaphoreType.DMA])
    def scatter_rows(h_hbm, pos_hbm, out_hbm, i0, i1, i2, i3, rows_v, sem):
        idx = (i0, i1, i2, i3)
        base = (lax.axis_index("s") * n_cores + lax.axis_index("c")) * tpw

        @pl.loop(0, tpw // ch)
        def _(i):
            off = base + i * ch
            pltpu.sync_copy(h_hbm.at[pl.ds(off, ch)], rows_v)
            for k in range(TOP_K):
                pltpu.sync_copy(pos_hbm.at[k, pl.ds(off, ch)], idx[k])
            copies = [pltpu.async_copy(rows_v, out_hbm.at[idx[k]], sem) for k in range(TOP_K)]
            for cp in copies:
                cp.wait()

    return scatter_rows(h2, pos8)


def _gather_expert_rows(ys, pos8):
    _, d = ys.shape
    t = pos8.shape[1]
    n_cores, n_workers = _sc_workers()
    tpw = t // n_workers
    ch = SC_GATHER_CHUNK
    mesh = plsc.VectorSubcoreMesh(core_axis_name="c", subcore_axis_name="s")

    @functools.partial(
        pl.kernel, mesh=mesh,
        out_type=jax.ShapeDtypeStruct((TOP_K, t, d), ys.dtype),
        scratch_types=([pltpu.VMEM((ch,), jnp.int32)] * 2 + [pltpu.VMEM((ch, d), ys.dtype)] * 2
                       + [pltpu.SemaphoreType.DMA] * 4))
    def gather_rows(ys_hbm, pos_hbm, out_hbm, idx0, idx1, rows0, rows1, g0, g1, w0, w1):
        idx, rows, gsem, wsem = (idx0, idx1), (rows0, rows1), (g0, g1), (w0, w1)
        base = (lax.axis_index("s") * n_cores + lax.axis_index("c")) * tpw
        items = [(i, k) for i in range(tpw // ch) for k in range(TOP_K)]

        def gather(n):
            i, k = items[n]
            pltpu.sync_copy(pos_hbm.at[k, pl.ds(base + i * ch, ch)], idx[n % 2])
            return pltpu.async_copy(ys_hbm.at[idx[n % 2]], rows[n % 2], gsem[n % 2])

        def write(n):
            i, k = items[n]
            return pltpu.async_copy(rows[n % 2], out_hbm.at[k, pl.ds(base + i * ch, ch)], wsem[n % 2])

        gathers, writes = {}, {}
        for n in range(len(items)):
            if n >= 2:
                writes[n - 2].wait()
            gathers[n] = gather(n)
            if n >= 1:
                gathers[n - 1].wait()
                writes[n - 1] = write(n - 1)
        last = len(items) - 1
        gathers[last].wait()
        writes[last] = write(last)
        if last >= 1:
            writes[last - 1].wait()
        writes[last].wait()

    return gather_rows(ys, pos8)


def _pack_bf16_pairs(a):
    return _pack_rounded(a.astype(BF16).astype(F32))


def _pack_rounded(r):
    n = r.shape[1] // 2
    lo = pltpu.bitcast(r[:, :n], jnp.int32)
    hi = pltpu.bitcast(r[:, n:], jnp.int32)
    return jnp.bitwise_or(jnp.bitwise_and(hi, -65536), jnp.bitwise_and(lax.shift_right_logical(lo, 16), 65535))


def _unpack_bf16_pairs(p):
    lo = pltpu.bitcast(lax.shift_left(p, 16), F32)
    hi = pltpu.bitcast(jnp.bitwise_and(p, -65536), F32)
    return lo, hi


def _expert_kernel(meta_ref, xs_ref, wup_ref, bup_ref, wdn_ref, bdn_ref, ys_ref,
                   wup_s, wdn_s, xbuf, obuf, in_sem, out_sem):
    e = pl.program_id(0)
    tm = TM_MOE
    half = D_MODEL // 2
    pad = meta_ref[META_PAD, e]
    n_t = pad // tm
    row0 = meta_ref[META_END, e] - pad
    cnt = meta_ref[META_CNT, e]

    def rows(i):
        return pl.ds(pl.multiple_of(row0 + i * tm, tm), tm)

    def x_copy(i, slot):
        return pltpu.make_async_copy(xs_ref.at[rows(i)], xbuf.at[slot], in_sem.at[slot])

    def y_copy(i, slot):
        return pltpu.make_async_copy(obuf.at[slot], ys_ref.at[rows(i)], out_sem.at[slot])

    @pl.when(n_t > 0)
    def _():
        x_copy(0, 0).start()
        wup_s[...] = wup_ref[0, 0].astype(BF16)
        wdn_s[...] = wdn_ref[0, 0].astype(BF16)

        def tile(i, carry):
            slot = lax.rem(i, 2)
            x_copy(i, slot).wait()

            @pl.when(i + 1 < n_t)
            def _():
                x_copy(i + 1, 1 - slot).start()

            @pl.when(i >= 2)
            def _():
                y_copy(i - 2, slot).wait()

            row = lax.broadcasted_iota(jnp.int32, (tm, 1), 0)
            lo, hi = _unpack_bf16_pairs(jnp.where(row < cnt - i * tm, xbuf[slot], 0))
            z = (jnp.dot(lo.astype(BF16), wup_s[0:half, :], preferred_element_type=F32)
                 + jnp.dot(hi.astype(BF16), wup_s[half:, :], preferred_element_type=F32) + bup_ref[0, 0])
            glu = jnp.minimum(z[:, :D_EXPERT], SWIGLU_LIMIT)
            lin = jnp.clip(z[:, D_EXPERT:], -SWIGLU_LIMIT, SWIGLU_LIMIT)
            act = (glu * _sigmoid(SWIGLU_ALPHA * glu) * (lin + 1.0)).astype(BF16)
            y = jnp.dot(act, wdn_s[...], preferred_element_type=F32) + bdn_ref[0, 0]
            obuf[slot] = _pack_bf16_pairs(y)
            y_copy(i, slot).start()
            return carry

        lax.fori_loop(0, n_t, tile, 0)

        @pl.when(n_t >= 2)
        def _():
            y_copy(n_t - 2, lax.rem(n_t, 2)).wait()

        y_copy(n_t - 1, lax.rem(n_t - 1, 2)).wait()

    @pl.when(e == N_EXPERTS - 1)
    def _():
        obuf[0] = jnp.zeros((tm, half), jnp.int32)

        def fill(i, carry):
            cp = pltpu.make_async_copy(obuf.at[0], ys_ref.at[pl.ds(pl.multiple_of(i * tm, tm), tm)], out_sem.at[0])
            cp.start()
            cp.wait()
            return carry

        lax.fori_loop(meta_ref[META_END, N_EXPERTS - 1] // tm, ys_ref.shape[0] // tm, fill, 0)


def _expert_mlp(xs, meta, layer, w_up, b_up, w_down, b_down):
    n_rows, half = xs.shape
    d = 2 * half
    tm = TM_MOE
    f = w_up.shape[-1]
    b_up4 = b_up.reshape(DEPTH, N_EXPERTS, 1, f)
    b_dn4 = b_down.reshape(DEPTH, N_EXPERTS, 1, d)
    wsel = lambda e, m: (layer, e, 0, 0)
    grid_spec = pltpu.PrefetchScalarGridSpec(
        num_scalar_prefetch=1,
        grid=(N_EXPERTS,),
        in_specs=[pl.BlockSpec(memory_space=pl.ANY),
                  pl.BlockSpec((1, 1, d, f), wsel),
                  pl.BlockSpec((1, 1, 1, f), wsel),
                  pl.BlockSpec((1, 1, f // 2, d), wsel),
                  pl.BlockSpec((1, 1, 1, d), wsel)],
        out_specs=pl.BlockSpec(memory_space=pl.ANY),
        scratch_shapes=[pltpu.VMEM((d, f), BF16), pltpu.VMEM((f // 2, d), BF16),
                        pltpu.VMEM((2, tm, half), jnp.int32), pltpu.VMEM((2, tm, half), jnp.int32),
                        pltpu.SemaphoreType.DMA((2,)), pltpu.SemaphoreType.DMA((2,))],
    )
    return pl.pallas_call(
        _expert_kernel,
        grid_spec=grid_spec,
        out_shape=jax.ShapeDtypeStruct((n_rows, half), jnp.int32),
        compiler_params=_cparams(("arbitrary",)),
        name="expert_mlp",
    )(meta, xs, w_up, b_up4, w_down, b_dn4)


def _combine_kernel(rows_ref, prob_ref, x_ref, gate_ref, lng_ref, lnb_ref, o_ref):
    p = prob_ref[...]
    y = None
    for k in range(TOP_K):
        yk = p[:, k:k + 1] * jnp.concatenate(_unpack_bf16_pairs(rows_ref[k]), axis=1)
        y = yk if y is None else y + yk
    o_ref[0] = _layer_norm(DN_ALPHA * x_ref[0] + (1.0 + gate_ref[0]) * y, lng_ref[...], lnb_ref[...])


def _combine(rows, prob_c, x1, gate, ln_g, ln_b, part):
    bsz, s, d = x1.shape
    tm = TM_DISP
    nt = s // tm
    pb = bsz // MOE_PARTS
    b0 = part * pb
    return pl.pallas_call(
        _combine_kernel,
        grid=(pb, nt),
        in_specs=[pl.BlockSpec((TOP_K, tm, d // 2), lambda b, i: (0, b * nt + i, 0)),
                  pl.BlockSpec((tm, LANES), lambda b, i: ((b0 + b) * nt + i, 0)),
                  pl.BlockSpec((1, tm, d), lambda b, i: (b0 + b, i, 0)),
                  pl.BlockSpec((1, 1, d), lambda b, i: (b0 + b, 0, 0)),
                  pl.BlockSpec((1, d), lambda b, i: (0, 0)),
                  pl.BlockSpec((1, d), lambda b, i: (0, 0))],
        out_specs=pl.BlockSpec((1, tm, d), lambda b, i: (b0 + b, i, 0)),
        out_shape=jax.ShapeDtypeStruct((bsz, s, d), F32),
        input_output_aliases={2: 0},
        compiler_params=_cparams(("arbitrary", "arbitrary")),
        name="combine",
    )(rows, prob_c, x1, gate.reshape(bsz, 1, d), ln_g.reshape(1, d), ln_b.reshape(1, d))


def kernel(x, c, positions, ada_w, ada_b, w_in, lam_q1, lam_k1, lam_q2, lam_k2, da_norm_g, ml_conv_w, ml_conv_b,
           ml_w_q, ml_w_k, ml_gate_b, ml_norm_g, s5_a_re, s5_a_im, s5_log_dt, s5_b_re, s5_b_im, s5_c_re, s5_c_im,
           s5_d, s5_w_glu, w_out, ln_g, ln_b, w_router, b_router, w_up, b_up, w_down, b_down):
    bsz, s, d = x.shape
    t = bsz * s
    n_tiles_max = (t * TOP_K) // TM_MOE + N_EXPERTS
    n_rows = n_tiles_max * TM_MOE
    mod = _modulation(c, ada_w, ada_b)
    cos_t, sin_t = _rope_tables(positions)
    for l in range(DEPTH):
        shift, scale, gate = jnp.split(mod[2 * l], 3, axis=-1)
        q, k, v, mlx, mlv, mlo, g_t, g_c, s5u = _in_proj(x, shift, scale, cos_t, sin_t, w_in, l)
        lam_init = 0.8 - 0.6 * math.exp(-0.3 * l)
        lamv = jnp.stack([lam_q1[l], lam_k1[l], lam_q2[l], lam_k2[l]])
        y_da = _diff_attn(q, k, v, lamv, da_norm_g[l], lam_init)
        y_ml = _mlstm(mlx, mlv, mlo, g_t, g_c, ml_conv_w[l], ml_conv_b[l], ml_w_q[l], ml_w_k[l],
                      ml_gate_b[l], ml_norm_g[l])
        s5p = _s5_params(s5_a_re[l], s5_a_im[l], s5_log_dt[l], s5_b_re[l], s5_b_im[l], s5_c_re[l], s5_c_im[l],
                         s5_w_glu[l])
        y_s5 = _s5(s5u, bsz, s5p, s5_d[l])
        shift2, scale2, gate2 = jnp.split(mod[2 * l + 1], 3, axis=-1)
        x1, h2, eid, prob, counts = _out_proj(y_da, y_ml, y_s5, x, gate, ln_g[l, 0], ln_b[l, 0], shift2, scale2,
                                      w_out, l, w_router[l], b_router[l])
        pos8, meta = _route(eid, counts)
        xs = _dispatch(h2, pos8, n_rows)
        ys = _expert_mlp(xs, meta, l, w_up, b_up, w_down, b_down)
        tp = t // MOE_PARTS
        x = x1
        for part in range(MOE_PARTS):
            rows = _gather_expert_rows(ys, pos8[:, part * tp:(part + 1) * tp])
            x = _combine(rows, prob, x, gate2, ln_g[l, 1], ln_b[l, 1], part)
    return x
```

```python
import functools
import math

import jax
import jax.numpy as jnp
from jax import lax
from jax.experimental import pallas as pl
from jax.experimental.pallas import tpu as pltpu
from jax.experimental.pallas import tpu_sc as plsc

F32 = jnp.float32
BF16 = jnp.bfloat16
HIGHEST = lax.Precision.HIGHEST

D_MODEL = 1024
DEPTH = 2
DA_HEADS = 4
DA_HEAD_DIM = 64
DA_V_DIM = 2 * DA_HEAD_DIM
DA_WIDTH = DA_HEADS * DA_V_DIM
DA_QK_WIDTH = DA_HEADS * 2 * DA_HEAD_DIM
ROPE_THETA = 10000.0
ML_HEADS = 4
ML_HEAD_DIM = 64
ML_WIDTH = ML_HEADS * ML_HEAD_DIM
ML_CONV = 4
S5_GROUP = 16
S5_STATE = 64
S5_WIDTH = D_MODEL - DA_WIDTH - ML_WIDTH
S5_GROUPS = S5_WIDTH // S5_GROUP
S5_NSTATE = S5_GROUPS * S5_STATE
N_EXPERTS = 32
TOP_K = 4
D_EXPERT = D_MODEL
SWIGLU_LIMIT = 7.0
SWIGLU_ALPHA = 1.702
DN_ALPHA = (2 * DEPTH) ** 0.25
LN_EPS = 1e-5
NEG = -1e30

OFF_DA_K = DA_QK_WIDTH
OFF_DA_V = 2 * DA_QK_WIDTH
OFF_ML_X = OFF_DA_V + DA_WIDTH
OFF_ML_V = OFF_ML_X + ML_WIDTH
OFF_ML_O = OFF_ML_V + ML_WIDTH
OFF_ML_I = OFF_ML_O + ML_WIDTH
OFF_ML_F = OFF_ML_I + ML_HEADS
OFF_S5_U = OFF_ML_F + ML_HEADS
N_IN = OFF_S5_U + S5_WIDTH

LANES = 128
SUBLANES = 8
VMEM_LIMIT_BYTES = 56 * 1024 * 1024

TM_PROJ = 1024
TQ = 512
ML_CHUNK = 256
ML_NB = 4
S5_TC = 256
S5_UNROLL = 8
TB_RANK = 1024
TM_MOE = 256
TM_DISP = 512
SC_SCATTER_CHUNK = 128
SC_GATHER_CHUNK = 64
GATE_PAD = 8
VT_ROWS = DA_V_DIM + 16
Q_PRESCALE = DA_HEAD_DIM ** -0.5 * math.log2(math.e)


def _cparams(sem, vmem=VMEM_LIMIT_BYTES):
    return pltpu.CompilerParams(dimension_semantics=sem, vmem_limit_bytes=vmem)


def _sigmoid(x):
    return 1.0 / (1.0 + jnp.exp(-x))


def _mod_kernel(c_ref, w_ref, b_ref, o_ref):
    c = c_ref[...]
    ca = (c * _sigmoid(c)).astype(BF16)
    w = w_ref[0, 0].astype(BF16)
    o_ref[0] = jnp.dot(ca, w, preferred_element_type=F32) + b_ref[0]


def _modulation(c, ada_w, ada_b):
    nsub = ada_w.shape[1]
    nmod = ada_w.shape[0] * nsub
    bsz, d = c.shape
    e = ada_w.shape[-1]
    tn = 1024
    b = ada_b.reshape(nmod, 1, e)
    return pl.pallas_call(
        _mod_kernel,
        grid=(nmod, e // tn),
        in_specs=[pl.BlockSpec((bsz, d), lambda n, j: (0, 0)),
                  pl.BlockSpec((1, 1, d, tn), lambda n, j: (n // nsub, n % nsub, 0, j)),
                  pl.BlockSpec((1, 1, tn), lambda n, j: (n, 0, j))],
        out_specs=pl.BlockSpec((1, bsz, tn), lambda n, j: (n, 0, j)),
        out_shape=jax.ShapeDtypeStruct((nmod, bsz, e), F32),
        compiler_params=_cparams(("arbitrary", "arbitrary")),
        name="modulation",
    )(c, ada_w, b)


def _rope_kernel(pos_ref, cos_ref, sin_ref):
    nfreq = DA_HEAD_DIM // 2
    pos = pos_ref[0].astype(F32)
    fidx = lax.broadcasted_iota(jnp.int32, (nfreq, 1), 0).astype(F32)
    inv = jnp.exp(fidx * (-2.0 * math.log(ROPE_THETA) / DA_HEAD_DIM))
    ang = inv * pos
    reps = LANES // nfreq
    cos_t = jnp.concatenate([jnp.cos(ang)] * reps, axis=0).T
    sin_t = jnp.concatenate([jnp.sin(ang)] * reps, axis=0).T
    lane = lax.broadcasted_iota(jnp.int32, (1, LANES), 1)
    sign = jnp.where((lane % DA_HEAD_DIM) < nfreq, -1.0, 1.0)
    cos_ref[0] = cos_t
    sin_ref[0] = sin_t * sign


def _rope_tables(positions):
    bsz, s = positions.shape
    ts = 512
    pos3 = positions.reshape(bsz, 1, s)
    return pl.pallas_call(
        _rope_kernel,
        grid=(bsz, s // ts),
        in_specs=[pl.BlockSpec((1, 1, ts), lambda b, i: (b, 0, i))],
        out_specs=[pl.BlockSpec((1, ts, LANES), lambda b, i: (b, i, 0))] * 2,
        out_shape=[jax.ShapeDtypeStruct((bsz, s, LANES), F32)] * 2,
        compiler_params=_cparams(("arbitrary", "arbitrary")),
        name="rope_tables",
    )(pos3)


def _in_proj_kernel(x_ref, shift_ref, scale_ref, cos_ref, sin_ref, win_ref,
                    q_ref, k_ref, vt_ref, mlx_ref, mlv_ref, mlo_ref, gt_ref, gc_ref, s5u_ref,
                    wqk_ref, wvt_ref, wrest_ref, wgt_ref, wgc_ref):
    @pl.when(jnp.logical_and(pl.program_id(0) == 0, pl.program_id(1) == 0))
    def _():
        wqk_ref[...] = win_ref[0, 0:OFF_DA_V, :].T.astype(BF16)
        wvt_ref[...] = win_ref[0, OFF_DA_V:OFF_ML_X, :].astype(BF16)
        wrest_ref[:, 0:3 * ML_WIDTH] = win_ref[0, OFF_ML_X:OFF_ML_I, :].T.astype(BF16)
        wrest_ref[:, 3 * ML_WIDTH:] = win_ref[0, OFF_S5_U:N_IN, :].T.astype(BF16)
        wgt_ref[...] = win_ref[0, OFF_ML_I:OFF_S5_U, :].astype(BF16)
        gslab = win_ref[0, OFF_ML_I:OFF_ML_I + LANES, :].T
        glane = lax.broadcasted_iota(jnp.int32, (1, LANES), 1)
        wgc_ref[...] = jnp.where(glane < GATE_PAD, gslab, 0.0).astype(BF16)

    h = (x_ref[0] * (1.0 + scale_ref[0]) + shift_ref[0]).astype(BF16)
    cos = cos_ref[0]
    sin = sin_ref[0]
    lane = lax.broadcasted_iota(jnp.int32, (1, LANES), 1)
    lo_half = (lane % DA_HEAD_DIM) < DA_HEAD_DIM // 2
    half = DA_HEAD_DIM // 2

    def rope(t):
        fwd = pltpu.roll(t, half, 1)
        bwd = pltpu.roll(t, LANES - half, 1)
        partner = jnp.where(lo_half, bwd, fwd)
        return t * cos + partner * sin

    qk = jnp.dot(h, wqk_ref[...], preferred_element_type=F32)
    nslab = DA_QK_WIDTH // LANES
    for c in range(nslab):
        q_ref[0, :, c * LANES:(c + 1) * LANES] = (
            rope(qk[:, c * LANES:(c + 1) * LANES]) * Q_PRESCALE).astype(BF16)
        k_ref[0, :, c * LANES:(c + 1) * LANES] = rope(
            qk[:, DA_QK_WIDTH + c * LANES:DA_QK_WIDTH + (c + 1) * LANES]).astype(BF16)

    vt = lax.dot_general(wvt_ref[...], h, (((1,), (1,)), ((), ())), preferred_element_type=F32)
    tm = h.shape[0]
    for hh in range(DA_HEADS):
        for jj in range(tm // TQ):
            vt_ref[0, hh, jj, 0:DA_V_DIM, :] = vt[hh * DA_V_DIM:(hh + 1) * DA_V_DIM,
                                                  jj * TQ:(jj + 1) * TQ].astype(BF16)
            vt_ref[0, hh, jj, DA_V_DIM:VT_ROWS, :] = jnp.ones((VT_ROWS - DA_V_DIM, TQ), BF16)

    r = jnp.dot(h, wrest_ref[...], preferred_element_type=F32)
    o = 0
    mlx_ref[0] = r[:, o:o + ML_WIDTH].astype(BF16); o += ML_WIDTH
    mlv_ref[0] = r[:, o:o + ML_WIDTH].astype(BF16); o += ML_WIDTH
    mlo_ref[0] = r[:, o:o + ML_WIDTH].astype(BF16); o += ML_WIDTH
    s5u_ref[...] = r[:, o:o + S5_WIDTH]
    gt = lax.dot_general(wgt_ref[...], h, (((1,), (1,)), ((), ())), preferred_element_type=F32)
    for c in range(h.shape[0] // ML_CHUNK):
        gt_ref[0, c] = gt[:, c * ML_CHUNK:(c + 1) * ML_CHUNK]
    gc_ref[0] = jnp.dot(h, wgc_ref[...], preferred_element_type=F32)


def _in_proj(x, shift, scale, cos_t, sin_t, w_in, layer):
    bsz, s, d = x.shape
    tm = TM_PROJ
    nrest = 3 * ML_WIDTH + S5_WIDTH
    shift3 = shift.reshape(bsz, 1, d)
    scale3 = scale.reshape(bsz, 1, d)
    tok = lambda b, i: (b, i, 0)
    per_b = lambda b, i: (b, 0, 0)
    out_shapes = [
        jax.ShapeDtypeStruct((bsz, s, DA_QK_WIDTH), BF16),
        jax.ShapeDtypeStruct((bsz, s, DA_QK_WIDTH), BF16),
        jax.ShapeDtypeStruct((bsz, DA_HEADS, s // TQ, VT_ROWS, TQ), BF16),
        jax.ShapeDtypeStruct((bsz, s, ML_WIDTH), BF16),
        jax.ShapeDtypeStruct((bsz, s, ML_WIDTH), BF16),
        jax.ShapeDtypeStruct((bsz, s, ML_WIDTH), BF16),
        jax.ShapeDtypeStruct((bsz, s // ML_CHUNK, GATE_PAD, ML_CHUNK), F32),
        jax.ShapeDtypeStruct((bsz, s, LANES), F32),
        jax.ShapeDtypeStruct((s, bsz * S5_WIDTH), F32),
    ]
    out_specs = [
        pl.BlockSpec((1, tm, DA_QK_WIDTH), tok),
        pl.BlockSpec((1, tm, DA_QK_WIDTH), tok),
        pl.BlockSpec((1, DA_HEADS, tm // TQ, VT_ROWS, TQ), lambda b, i: (b, 0, i, 0, 0)),
        pl.BlockSpec((1, tm, ML_WIDTH), tok),
        pl.BlockSpec((1, tm, ML_WIDTH), tok),
        pl.BlockSpec((1, tm, ML_WIDTH), tok),
        pl.BlockSpec((1, tm // ML_CHUNK, GATE_PAD, ML_CHUNK), lambda b, i: (b, i, 0, 0)),
        pl.BlockSpec((1, tm, LANES), tok),
        pl.BlockSpec((tm, S5_WIDTH), lambda b, i: (i, b)),
    ]
    return pl.pallas_call(
        _in_proj_kernel,
        grid=(bsz, s // tm),
        in_specs=[pl.BlockSpec((1, tm, d), tok),
                  pl.BlockSpec((1, 1, d), per_b),
                  pl.BlockSpec((1, 1, d), per_b),
                  pl.BlockSpec((1, tm, LANES), tok),
                  pl.BlockSpec((1, tm, LANES), tok),
                  pl.BlockSpec((1, N_IN, d), lambda b, i: (layer, 0, 0))],
        out_specs=out_specs,
        out_shape=out_shapes,
        scratch_shapes=[pltpu.VMEM((d, OFF_DA_V), BF16), pltpu.VMEM((DA_WIDTH, d), BF16),
                        pltpu.VMEM((d, nrest), BF16), pltpu.VMEM((GATE_PAD, d), BF16),
                        pltpu.VMEM((d, LANES), BF16)],
        compiler_params=_cparams(("arbitrary", "arbitrary")),
        name="in_proj",
    )(x, shift3, scale3, cos_t, sin_t, jnp.swapaxes(w_in, 1, 2))


def _diff_attn_kernel(lam_init, lamv_ref, gain_ref, q_ref, k_ref, vt_ref, o_ref, acc_s, m_s):
    qi = pl.program_id(2)
    tq = q_ref.shape[1]
    lane = lax.broadcasted_iota(jnp.int32, (1, LANES), 1)
    first = lane < DA_HEAD_DIM
    q = q_ref[0]
    zero = jnp.zeros_like(q)
    qm = (jnp.where(first, q, zero), jnp.where(first, zero, q))
    acc_s[...] = jnp.zeros_like(acc_s)
    m_s[...] = jnp.full(m_s.shape, NEG, F32)

    def step(j, nblk, masked):
        tk = nblk * tq
        kb = k_ref[0, pl.ds(pl.multiple_of(j * tq, tq), tk), :]
        vtb = vt_ref[0, 0, j] if nblk == 1 else jnp.concatenate([vt_ref[0, 0, j + b] for b in range(nblk)], axis=1)
        for c in range(2):
            st = lax.dot_general(kb, qm[c], (((1,), (1,)), ((), ())), preferred_element_type=F32)
            if masked:
                key_i = lax.broadcasted_iota(jnp.int32, (tk, tq), 0) - (tk - tq)
                qry_i = lax.broadcasted_iota(jnp.int32, (tk, tq), 1)
                st = jnp.where(key_i <= qry_i, st, NEG)
            m_prev = m_s[c]
            m_new = jnp.maximum(m_prev, jnp.max(st, axis=0, keepdims=True))
            alpha = jnp.exp2(m_prev - m_new)
            p = jnp.exp2(st - m_new).astype(BF16)
            acc_s[c] = alpha * acc_s[c] + jnp.dot(vtb, p, preferred_element_type=F32)
            m_s[c] = m_new

    def body(jj, carry):
        step(2 * jj, 2, False)
        return carry

    lax.fori_loop(0, qi // 2, body, 0)

    @pl.when(qi % 2 == 1)
    def _():
        step(qi - 1, 2, True)

    @pl.when(qi % 2 == 0)
    def _():
        step(qi, 1, True)

    outs = []
    for c in range(2):
        acc = acc_s[c]
        outs.append(acc[:DA_V_DIM] / acc[DA_V_DIM:DA_V_DIM + 1])

    lamv = lamv_ref[...]
    lam = (jnp.exp(jnp.sum(lamv[0:1] * lamv[1:2], axis=1, keepdims=True))
           - jnp.exp(jnp.sum(lamv[2:3] * lamv[3:4], axis=1, keepdims=True)) + lam_init)
    ot = outs[0] - lam * outs[1]
    ms = jnp.mean(ot * ot, axis=0, keepdims=True)
    ot = ot * (lax.rsqrt(ms + LN_EPS) * (1.0 - lam_init))
    o_ref[0] = (ot.T * gain_ref[...]).astype(o_ref.dtype)


def _diff_attn(q, k, vt, lamv, gain, lam_init):
    bsz, s, _ = q.shape
    tq = TQ
    nq = s // tq
    return pl.pallas_call(
        functools.partial(_diff_attn_kernel, lam_init),
        grid=(bsz, DA_HEADS, nq),
        in_specs=[pl.BlockSpec((4, DA_HEAD_DIM), lambda b, h, i: (0, 0)),
                  pl.BlockSpec((1, DA_V_DIM), lambda b, h, i: (0, 0)),
                  pl.BlockSpec((1, tq, DA_V_DIM), lambda b, h, i: (b, i, h)),
                  pl.BlockSpec((1, s, DA_V_DIM), lambda b, h, i: (b, 0, h)),
                  pl.BlockSpec((1, 1, nq, VT_ROWS, tq), lambda b, h, i: (b, h, 0, 0, 0))],
        out_specs=pl.BlockSpec((1, tq, DA_V_DIM), lambda b, h, i: (b, i, h)),
        out_shape=jax.ShapeDtypeStruct((bsz, s, DA_WIDTH), BF16),
        scratch_shapes=[pltpu.VMEM((2, VT_ROWS, tq), F32), pltpu.VMEM((2, 1, tq), F32)],
        compiler_params=_cparams(("arbitrary", "arbitrary", "arbitrary")),
        name="diff_attn",
    )(lamv, gain.reshape(1, DA_V_DIM), q, k, vt)


def _log_sigmoid(x):
    return jnp.minimum(x, 0.0) - jnp.log(1.0 + jnp.exp(-jnp.abs(x)))


def _split3(a):
    hi = a.astype(BF16)
    r1 = a - hi.astype(F32)
    mid = r1.astype(BF16)
    lo = (r1 - mid.astype(F32)).astype(BF16)
    return hi, mid, lo


def _mlstm_kernel(x_ref, v_ref, o_ref, gt_ref, gc_ref, cw_ref, cb_ref, wq_ref, wkt_ref, gbt_ref, gbc_ref,
                  ng_ref, hmean_ref, y_ref, xc_s, c_s, m_s):
    nb, s = x_ref.shape[0], x_ref.shape[1]
    L = ML_CHUNK
    H, dh = ML_HEADS, ML_HEAD_DIM
    nc = s // L
    cw = cw_ref[...]
    row = lax.broadcasted_iota(jnp.int32, (s, 1), 0)
    for bi in range(nb):
        x = x_ref[bi].astype(F32)
        xc = x * cw[ML_CONV - 1:ML_CONV]
        for j in range(1, ML_CONV):
            xs = jnp.where(row >= j, pltpu.roll(x, j, 0), 0.0)
            xc = xc + xs * cw[ML_CONV - 1 - j:ML_CONV - j]
        xc = xc + cb_ref[...]
        xc_s[bi] = (xc * _sigmoid(xc)).astype(BF16)

    c_s[...] = jnp.zeros_like(c_s)
    m_s[...] = jnp.full(m_s.shape, NEG, F32)

    ri = lax.broadcasted_iota(jnp.int32, (L, L), 0)
    ci = lax.broadcasted_iota(jnp.int32, (L, L), 1)
    causal = ci <= ri
    tril = causal.astype(BF16)
    triu = (ri <= ci).astype(BF16)
    lane = lax.broadcasted_iota(jnp.int32, (1, dh), 1)
    one_hot0 = jnp.broadcast_to((lane == 0).astype(BF16), (L, dh))

    def chunk_one(bi, ci_, t0):
        xcc = xc_s[bi, pl.ds(t0, L), :]
        qc = jnp.dot(xcc, wq_ref[...], preferred_element_type=F32).astype(BF16)
        ktc = lax.dot_general(wkt_ref[...], xcc, (((1,), (1,)), ((), ())),
                              preferred_element_type=F32)
        g_rows = gt_ref[bi, ci_] + gbt_ref[...]
        g_cols = gc_ref[bi, pl.ds(t0, L), :] + gbc_ref[...]
        lf_rows = _log_sigmoid(g_rows)
        lf_cols = _log_sigmoid(g_cols)
        r3 = jnp.dot(jnp.concatenate(_split3(lf_rows), axis=0), triu, preferred_element_type=F32)
        b_rows = r3[0:GATE_PAD] + r3[GATE_PAD:2 * GATE_PAD] + r3[2 * GATE_PAD:]
        c3 = jnp.dot(tril, jnp.concatenate(_split3(lf_cols), axis=1), preferred_element_type=F32)
        b_cols = c3[:, 0:LANES] + c3[:, LANES:2 * LANES] + c3[:, 2 * LANES:]
        vch = v_ref[bi, pl.ds(t0, L), :]
        och = o_ref[bi, pl.ds(t0, L), :].astype(F32)
        hs = []
        for h in range(H):
            br = b_rows[H + h:H + h + 1, :]
            ir = g_rows[h:h + 1, :]
            bc = b_cols[:, H + h:H + h + 1]
            m_prev = m_s[bi, h]
            log_d = jnp.where(causal, bc - br + ir, NEG)
            inter = bc + m_prev
            mx = jnp.maximum(inter, jnp.max(log_d, axis=1, keepdims=True))
            dmat = jnp.exp(log_d - mx)
            dec = jnp.exp(inter - mx)
            qh = qc[:, h * dh:(h + 1) * dh]
            kth = ktc[h * dh:(h + 1) * dh, :]
            vaug = jnp.concatenate([vch[:, h * dh:(h + 1) * dh], one_hot0], axis=1)
            sm = (jnp.dot(qh, kth.astype(BF16), preferred_element_type=F32) * dmat).astype(BF16)
            c_prev = c_s[bi, h]
            na = (jnp.dot(sm, vaug, preferred_element_type=F32)
                  + dec * jnp.dot(qh, c_prev.astype(BF16), preferred_element_type=F32))
            den = na[:, dh:dh + 1]
            hs.append(na[:, :dh] / jnp.maximum(jnp.abs(den), jnp.exp(-mx)))
            g_tot = br[:, L - 1:L]
            a_row = g_tot - br + ir
            m_new = jnp.maximum(g_tot + m_prev, jnp.max(a_row, axis=1, keepdims=True))
            decay = jnp.exp(g_tot + m_prev - m_new)
            w_row = jnp.exp(a_row - m_new)
            kw = (kth * w_row).astype(BF16)
            c_s[bi, h] = decay * c_prev + jnp.dot(kw, vaug, preferred_element_type=F32)
            m_s[bi, h] = m_new
        hcat = jnp.concatenate(hs, axis=1)
        m3 = jnp.dot(jnp.concatenate(_split3(hcat * hcat), axis=0), hmean_ref[...],
                     preferred_element_type=F32)
        ms = m3[0:L] + m3[L:2 * L] + m3[2 * L:]
        y = hcat * lax.rsqrt(ms + LN_EPS) * ng_ref[...] * _sigmoid(och)
        y_ref[bi, pl.ds(t0, L), :] = y.astype(y_ref.dtype)

    def chunk(ci_, _):
        t0 = pl.multiple_of(ci_ * L, L)
        for bi in range(nb):
            chunk_one(bi, ci_, t0)
        return 0

    lax.fori_loop(0, nc, chunk, 0)


def _mlstm(mlx, mlv, mlo, g_t, g_c, conv_w, conv_b, w_q, w_k, gate_b, norm_g):
    bsz, s, _ = mlx.shape
    H, dh = ML_HEADS, ML_HEAD_DIM
    eye = jnp.eye(H, dtype=F32)
    wq_bd = jnp.einsum('hde,hg->hdge', w_q, eye).reshape(ML_WIDTH, ML_WIDTH).astype(BF16)
    wk_bd = jnp.einsum('hde,hg->hdge', w_k * (dh ** -0.5), eye).reshape(ML_WIDTH, ML_WIDTH)
    wkt_bd = wk_bd.T.astype(BF16)
    gbt = gate_b.reshape(GATE_PAD, 1)
    gbc = jnp.pad(gate_b.reshape(1, GATE_PAD), ((0, 0), (0, LANES - GATE_PAD)))
    hmean = jnp.kron(eye, jnp.full((dh, dh), 1.0 / dh, F32)).astype(BF16)
    nc = s // ML_CHUNK
    tok = lambda b: (b, 0, 0)
    c2 = lambda b: (0, 0)
    nb = ML_NB
    return pl.pallas_call(
        _mlstm_kernel,
        grid=(bsz // nb,),
        in_specs=[pl.BlockSpec((nb, s, ML_WIDTH), tok),
                  pl.BlockSpec((nb, s, ML_WIDTH), tok),
                  pl.BlockSpec((nb, s, ML_WIDTH), tok),
                  pl.BlockSpec((nb, nc, GATE_PAD, ML_CHUNK), lambda b: (b, 0, 0, 0)),
                  pl.BlockSpec((nb, s, LANES), tok),
                  pl.BlockSpec((ML_CONV, ML_WIDTH), c2),
                  pl.BlockSpec((1, ML_WIDTH), c2),
                  pl.BlockSpec((ML_WIDTH, ML_WIDTH), c2),
                  pl.BlockSpec((ML_WIDTH, ML_WIDTH), c2),
                  pl.BlockSpec((GATE_PAD, 1), c2),
                  pl.BlockSpec((1, LANES), c2),
                  pl.BlockSpec((1, ML_WIDTH), c2),
                  pl.BlockSpec((ML_WIDTH, ML_WIDTH), c2)],
        out_specs=pl.BlockSpec((nb, s, ML_WIDTH), tok),
        out_shape=jax.ShapeDtypeStruct((bsz, s, ML_WIDTH), BF16),
        scratch_shapes=[pltpu.VMEM((nb, s, ML_WIDTH), BF16),
                        pltpu.VMEM((nb, H, dh, LANES), F32),
                        pltpu.VMEM((nb, H, 1, 1), F32)],
        compiler_params=_cparams(("arbitrary",)),
        name="mlstm",
    )(mlx, mlv, mlo, g_t, g_c, conv_w, conv_b.reshape(1, ML_WIDTH), wq_bd, wkt_bd, gbt, gbc,
      norm_g.reshape(1, ML_WIDTH), hmean)


def _gelu_tanh(x):
    return 0.5 * x * (1.0 + jnp.tanh(math.sqrt(2.0 / math.pi) * (x + 0.044715 * (x * x * x))))


def _s5_kernel(u_ref, are_ref, aim_ref, bcat_ref, ccat_ref, d_ref, wglu_ref, y_ref, xs_s, st_s):
    tc, bsz, w = u_ref.shape
    n = S5_NSTATE

    @pl.when(pl.program_id(0) == 0)
    def _():
        st_s[...] = jnp.zeros_like(st_s)

    u = u_ref[...].reshape(tc * bsz, w)
    xs_s[...] = jnp.dot(u.astype(BF16), bcat_ref[...], preferred_element_type=F32).reshape(tc, bsz, 2 * n)
    a_re = jnp.broadcast_to(are_ref[...], (bsz, n))
    a_im = jnp.broadcast_to(aim_ref[...], (bsz, n))

    def step(t, carry):
        x_re, x_im = carry
        bu = xs_s[t]
        n_re = a_re * x_re - a_im * x_im + bu[:, :n]
        n_im = a_re * x_im + a_im * x_re + bu[:, n:]
        xs_s[t] = jnp.concatenate([n_re, n_im], axis=1)
        return n_re, n_im

    x_re, x_im = lax.fori_loop(0, tc, step, (st_s[0], st_s[1]), unroll=S5_UNROLL)
    st_s[0] = x_re
    st_s[1] = x_im

    xs = xs_s[...].reshape(tc * bsz, 2 * n).astype(BF16)
    y = jnp.dot(xs, ccat_ref[...], preferred_element_type=F32) + d_ref[...] * u
    z = jnp.dot(_gelu_tanh(y).astype(BF16), wglu_ref[...], preferred_element_type=F32)
    out = z[:, :w] * _sigmoid(z[:, w:])
    y_ref[...] = out.reshape(tc, bsz, w).astype(y_ref.dtype)


def _s5_params(a_re, a_im, log_dt, b_re, b_im, c_re, c_im, w_glu):
    G, P, Hc = S5_GROUPS, S5_STATE, S5_GROUP
    dt = jnp.exp(log_dt)[:, None]
    mag = jnp.exp(a_re * dt)
    ab_re = mag * jnp.cos(a_im * dt)
    ab_im = mag * jnp.sin(a_im * dt)
    nr, ni = ab_re - 1.0, ab_im
    den = a_re * a_re + a_im * a_im
    fr = (nr * a_re + ni * a_im) / den
    fi = (ni * a_re - nr * a_im) / den
    bb_re = fr[..., None] * b_re - fi[..., None] * b_im
    bb_im = fr[..., None] * b_im + fi[..., None] * b_re
    eye = jnp.eye(G, dtype=F32)
    bd = lambda t, sub: jnp.einsum(sub, t, eye)
    bre = bd(bb_re, 'gph,gk->ghkp').reshape(G * Hc, G * P)
    bim = bd(bb_im, 'gph,gk->ghkp').reshape(G * Hc, G * P)
    bcat = jnp.concatenate([bre, bim], axis=1).astype(BF16)
    cre = bd(c_re, 'ghp,gk->gpkh').reshape(G * P, G * Hc)
    cim = bd(c_im, 'ghp,gk->gpkh').reshape(G * P, G * Hc)
    ccat = jnp.concatenate([cre, -cim], axis=0).astype(BF16)
    wv = bd(w_glu[:, :, :Hc], 'ghj,gk->ghkj').reshape(G * Hc, G * Hc)
    wg = bd(w_glu[:, :, Hc:], 'ghj,gk->ghkj').reshape(G * Hc, G * Hc)
    wglu = jnp.concatenate([wv, wg], axis=1).astype(BF16)
    return ab_re.reshape(1, G * P), ab_im.reshape(1, G * P), bcat, ccat, wglu


def _s5(u_tm, bsz, params, d_skip):
    s = u_tm.shape[0]
    w = S5_WIDTH
    n = S5_NSTATE
    are, aim, bcat, ccat, wglu = params
    u3 = u_tm.reshape(s, bsz, w)
    tc = S5_TC
    c2 = lambda i: (0, 0)
    y = pl.pallas_call(
        _s5_kernel,
        grid=(s // tc,),
        in_specs=[pl.BlockSpec((tc, bsz, w), lambda i: (i, 0, 0)),
                  pl.BlockSpec((1, n), c2),
                  pl.BlockSpec((1, n), c2),
                  pl.BlockSpec((w, 2 * n), c2),
                  pl.BlockSpec((2 * n, w), c2),
                  pl.BlockSpec((1, w), c2),
                  pl.BlockSpec((w, 2 * w), c2)],
        out_specs=pl.BlockSpec((tc, bsz, w), lambda i: (i, 0, 0)),
        out_shape=jax.ShapeDtypeStruct((s, bsz, w), F32),
        scratch_shapes=[pltpu.VMEM((tc, bsz, 2 * n), F32),
                        pltpu.VMEM((2, bsz, n), F32)],
        compiler_params=_cparams(("arbitrary",)),
        name="s5",
    )(u3, are, aim, bcat, ccat, d_skip.reshape(1, w), wglu)
    return y.reshape(s, bsz * w)


def _layer_norm(z, g, b):
    mu = jnp.mean(z, axis=1, keepdims=True)
    zc = z - mu
    var = jnp.mean(zc * zc, axis=1, keepdims=True)
    return zc * lax.rsqrt(var + LN_EPS) * g + b


def _out_proj_kernel(yda_ref, yml_ref, ys5_ref, x_ref, gate_ref, lng_ref, lnb_ref, shift_ref, scale_ref,
                     wout32_ref, wrt_ref, brt_ref, x1_ref, h2_ref, eid_ref, prob_ref, cnt_ref, wout_ref):
    first_step = jnp.logical_and(pl.program_id(0) == 0, pl.program_id(1) == 0)

    @pl.when(first_step)
    def _():
        wout_ref[...] = wout32_ref[0].astype(BF16)

    y = jnp.dot(yda_ref[0], wout_ref[0:DA_WIDTH, :], preferred_element_type=F32)
    y = y + jnp.dot(yml_ref[0], wout_ref[DA_WIDTH:DA_WIDTH + ML_WIDTH, :], preferred_element_type=F32)
    y = y + jnp.dot(ys5_ref[...].astype(BF16), wout_ref[DA_WIDTH + ML_WIDTH:, :], preferred_element_type=F32)
    x1 = _layer_norm(DN_ALPHA * x_ref[0] + (1.0 + gate_ref[0]) * y, lng_ref[...], lnb_ref[...])
    x1_ref[0] = x1
    h2 = x1 * (1.0 + scale_ref[0]) + shift_ref[0]
    h_hi = h2.astype(BF16)
    h_hi32 = h_hi.astype(F32)
    h2_ref[...] = _pack_rounded(h_hi32)
    h_lo = (h2 - h_hi32).astype(BF16)
    nt_dot = lambda a, b: lax.dot_general(a, b, (((1,), (1,)), ((), ())), preferred_element_type=F32)
    by_hi = nt_dot(wrt_ref[...], h_hi)
    logits = (by_hi[:N_EXPERTS] + by_hi[N_EXPERTS:] + nt_dot(wrt_ref[0:N_EXPERTS, :], h_lo)
              + brt_ref[...])
    eidx = lax.broadcasted_iota(jnp.int32, logits.shape, 0)
    vals, ids = [], []
    for _ in range(TOP_K):
        mx = jnp.max(logits, axis=0, keepdims=True)
        sel = jnp.min(jnp.where(logits == mx, eidx, N_EXPERTS), axis=0, keepdims=True)
        vals.append(mx)
        ids.append(sel)
        logits = jnp.where(eidx == sel, -jnp.inf, logits)
    ex = [jnp.exp(v - vals[0]) for v in vals]
    tot = ex[0] + ex[1] + ex[2] + ex[3]
    zi = jnp.zeros_like(ids[0])
    eid_ref[...] = jnp.concatenate(ids + [zi] * (SUBLANES - TOP_K), axis=0)

    @pl.when(first_step)
    def _():
        cnt_ref[...] = jnp.zeros_like(cnt_ref)

    member = jnp.zeros(logits.shape, F32)
    for sel in ids:
        member = member + (eidx == sel).astype(F32)
    cnt_ref[...] = cnt_ref[...] + jnp.sum(member, axis=1, keepdims=True)
    zf = jnp.zeros((LANES - TOP_K, tot.shape[1]), F32)
    prob_ref[...] = jnp.concatenate([e / tot for e in ex] + [zf], axis=0).T


def _out_proj(y_da, y_ml, y_s5, x, gate, ln_g, ln_b, shift2, scale2, w_out, layer, w_router_l, b_router_l):
    bsz, s, d = x.shape
    tm = TM_PROJ
    nt = s // tm
    tok = lambda b, i: (b, i, 0)
    per_b = lambda b, i: (b, 0, 0)
    c2 = lambda b, i: (0, 0)
    r3 = lambda a: a.reshape(bsz, 1, d)
    flat = lambda b, i: (0, b * nt + i)
    wr_t = w_router_l.T
    wr_hi = wr_t.astype(BF16)
    wr_lo = (wr_t - wr_hi.astype(F32)).astype(BF16)
    return pl.pallas_call(
        _out_proj_kernel,
        grid=(bsz, nt),
        in_specs=[pl.BlockSpec((1, tm, DA_WIDTH), tok),
                  pl.BlockSpec((1, tm, ML_WIDTH), tok),
                  pl.BlockSpec((tm, S5_WIDTH), lambda b, i: (i, b)),
                  pl.BlockSpec((1, tm, d), tok),
                  pl.BlockSpec((1, 1, d), per_b),
                  pl.BlockSpec((1, d), c2),
                  pl.BlockSpec((1, d), c2),
                  pl.BlockSpec((1, 1, d), per_b),
                  pl.BlockSpec((1, 1, d), per_b),
                  pl.BlockSpec((1, d, d), lambda b, i: (layer, 0, 0)),
                  pl.BlockSpec((2 * N_EXPERTS, d), c2),
                  pl.BlockSpec((N_EXPERTS, 1), c2)],
        out_specs=[pl.BlockSpec((1, tm, d), tok),
                   pl.BlockSpec((tm, d // 2), lambda b, i: (b * nt + i, 0)),
                   pl.BlockSpec((SUBLANES, tm), flat),
                   pl.BlockSpec((tm, LANES), lambda b, i: (b * nt + i, 0)),
                   pl.BlockSpec((N_EXPERTS, LANES), c2)],
        out_shape=[jax.ShapeDtypeStruct((bsz, s, d), F32),
                   jax.ShapeDtypeStruct((bsz * s, d // 2), jnp.int32),
                   jax.ShapeDtypeStruct((SUBLANES, bsz * s), jnp.int32),
                   jax.ShapeDtypeStruct((bsz * s, LANES), F32),
                   jax.ShapeDtypeStruct((N_EXPERTS, LANES), F32)],
        scratch_shapes=[pltpu.VMEM((d, d), BF16)],
        compiler_params=_cparams(("arbitrary", "arbitrary")),
        name="out_proj",
    )(y_da, y_ml, y_s5, x, r3(gate), ln_g.reshape(1, d), ln_b.reshape(1, d), r3(shift2), r3(scale2),
      w_out, jnp.concatenate([wr_hi, wr_lo], axis=0), b_router_l.reshape(N_EXPERTS, 1))


META_END, META_PAD, META_CNT = 0, 1, 2


def _route_kernel(eid_ref, cnt_ref, pos_ref, meta_ref, carry_s, start_s):
    i = pl.program_id(0)
    tb = eid_ref.shape[1]
    ntp = meta_ref.shape[1]
    tm = TM_MOE

    @pl.when(i == 0)
    def _():
        cnt = cnt_ref[...]
        padded = jnp.floor((cnt + (tm - 1)) * (1.0 / tm)) * tm
        er = lax.broadcasted_iota(jnp.int32, (N_EXPERTS, N_EXPERTS), 0)
        ec = lax.broadcasted_iota(jnp.int32, (N_EXPERTS, N_EXPERTS), 1)
        ends = jnp.dot((ec <= er).astype(F32), padded, preferred_element_type=F32, precision=HIGHEST)
        start_s[...] = ends - padded
        carry_s[...] = jnp.zeros_like(carry_s)
        lane = lax.broadcasted_iota(jnp.int32, (N_EXPERTS, ntp), 1)
        sub = lax.broadcasted_iota(jnp.int32, (N_EXPERTS, ntp), 0)
        diag = lane == sub

        def as_row(col):
            return jnp.sum(jnp.where(diag, col, 0.0), axis=0, keepdims=True)

        zero = jnp.zeros((SUBLANES - 3, ntp), F32)
        meta_ref[...] = jnp.concatenate([as_row(ends[:, 0:1]), as_row(padded[:, 0:1]), as_row(cnt[:, 0:1]), zero],
                                        axis=0).astype(jnp.int32)

    eid = eid_ref[...]
    eidx = lax.broadcasted_iota(jnp.int32, (N_EXPERTS, tb), 0)
    hot = [eidx == eid[k:k + 1, :] for k in range(TOP_K)]
    member = jnp.zeros((N_EXPERTS, tb), F32)
    for k in range(TOP_K):
        member = member + hot[k].astype(F32)
    ri = lax.broadcasted_iota(jnp.int32, (tb, tb), 0)
    ci = lax.broadcasted_iota(jnp.int32, (tb, tb), 1)
    triu = (ri <= ci).astype(BF16)
    incl = jnp.dot(member.astype(BF16), triu, preferred_element_type=F32)
    slot = incl - member + carry_s[:, 0:1] + start_s[:, 0:1]
    rows = [jnp.sum(jnp.where(hot[k], slot, 0.0), axis=0, keepdims=True) for k in range(TOP_K)]
    zr = jnp.zeros_like(rows[0])
    pos_ref[...] = jnp.concatenate(rows + [zr] * (SUBLANES - TOP_K), axis=0).astype(jnp.int32)
    carry_s[...] = carry_s[...] + jnp.sum(member, axis=1, keepdims=True)


def _route(eid, counts):
    t = eid.shape[1]
    tb = TB_RANK
    ntp = LANES
    pos8, meta = pl.pallas_call(
        _route_kernel,
        grid=(t // tb,),
        in_specs=[pl.BlockSpec((SUBLANES, tb), lambda i: (0, i)),
                  pl.BlockSpec((N_EXPERTS, LANES), lambda i: (0, 0))],
        out_specs=[pl.BlockSpec((SUBLANES, tb), lambda i: (0, i)),
                   pl.BlockSpec((SUBLANES, ntp), lambda i: (0, 0))],
        out_shape=[jax.ShapeDtypeStruct((SUBLANES, t), jnp.int32),
                   jax.ShapeDtypeStruct((SUBLANES, ntp), jnp.int32)],
        scratch_shapes=[pltpu.VMEM((N_EXPERTS, LANES), F32), pltpu.VMEM((N_EXPERTS, LANES), F32)],
        compiler_params=_cparams(("arbitrary",)),
        name="route",
    )(eid, counts)
    return pos8, meta


def _sc_workers():
    info = plsc.get_sparse_core_info()
    return info.num_cores, info.num_cores * info.num_subcores


def _dispatch(h2, pos8, n_rows):
    t, d = h2.shape
    n_cores, n_workers = _sc_workers()
    tpw = t // n_workers
    ch = SC_SCATTER_CHUNK
    mesh = plsc.VectorSubcoreMesh(core_axis_name="c", subcore_axis_name="s")

    @functools.partial(
        pl.kernel, mesh=mesh,
        out_type=jax.ShapeDtypeStruct((n_rows, d), h2.dtype),
        scratch_types=[pltpu.VMEM((ch,), jnp.int32)] * TOP_K + [pltpu.VMEM((ch, d), h2.dtype),
                                                                pltpu.SemaphoreType.DMA])
    def scatter_rows(h_hbm, pos_hbm, out_hbm, i0, i1, i2, i3, rows_v, sem):
        idx = (i0, i1, i2, i3)
        base = (lax.axis_index("s") * n_cores + lax.axis_index("c")) * tpw

        @pl.loop(0, tpw // ch)
        def _(i):
            off = base + i * ch
            pltpu.sync_copy(h_hbm.at[pl.ds(off, ch)], rows_v)
            for k in range(TOP_K):
                pltpu.sync_copy(pos_hbm.at[k, pl.ds(off, ch)], idx[k])
            copies = [pltpu.async_copy(rows_v, out_hbm.at[idx[k]], sem) for k in range(TOP_K)]
            for cp in copies:
                cp.wait()

    return scatter_rows(h2, pos8)


def _gather_expert_rows(ys, pos8):
    _, d = ys.shape
    t = pos8.shape[1]
    n_cores, n_workers = _sc_workers()
    tpw = t // n_workers
    ch = SC_GATHER_CHUNK
    mesh = plsc.VectorSubcoreMesh(core_axis_name="c", subcore_axis_name="s")

    @functools.partial(
        pl.kernel, mesh=mesh,
        out_type=jax.ShapeDtypeStruct((TOP_K, t, d), ys.dtype),
        scratch_types=([pltpu.VMEM((ch,), jnp.int32)] * 2 + [pltpu.VMEM((ch, d), ys.dtype)] * 2
                       + [pltpu.SemaphoreType.DMA] * 4))
    def gather_rows(ys_hbm, pos_hbm, out_hbm, idx0, idx1, rows0, rows1, g0, g1, w0, w1):
        idx, rows, gsem, wsem = (idx0, idx1), (rows0, rows1), (g0, g1), (w0, w1)
        base = (lax.axis_index("s") * n_cores + lax.axis_index("c")) * tpw
        items = [(i, k) for i in range(tpw // ch) for k in range(TOP_K)]

        def gather(n):
            i, k = items[n]
            pltpu.sync_copy(pos_hbm.at[k, pl.ds(base + i * ch, ch)], idx[n % 2])
            return pltpu.async_copy(ys_hbm.at[idx[n % 2]], rows[n % 2], gsem[n % 2])

        def write(n):
            i, k = items[n]
            return pltpu.async_copy(rows[n % 2], out_hbm.at[k, pl.ds(base + i * ch, ch)], wsem[n % 2])

        gathers, writes = {}, {}
        for n in range(len(items)):
            if n >= 2:
                writes[n - 2].wait()
            gathers[n] = gather(n)
            if n >= 1:
                gathers[n - 1].wait()
                writes[n - 1] = write(n - 1)
        last = len(items) - 1
        gathers[last].wait()
        writes[last] = write(last)
        if last >= 1:
            writes[last - 1].wait()
        writes[last].wait()

    return gather_rows(ys, pos8)


def _pack_bf16_pairs(a):
    return _pack_rounded(a.astype(BF16).astype(F32))


def _pack_rounded(r):
    n = r.shape[1] // 2
    lo = pltpu.bitcast(r[:, :n], jnp.int32)
    hi = pltpu.bitcast(r[:, n:], jnp.int32)
    return jnp.bitwise_or(jnp.bitwise_and(hi, -65536), jnp.bitwise_and(lax.shift_right_logical(lo, 16), 65535))


def _unpack_bf16_pairs(p):
    lo = pltpu.bitcast(lax.shift_left(p, 16), F32)
    hi = pltpu.bitcast(jnp.bitwise_and(p, -65536), F32)
    return lo, hi


def _expert_kernel(meta_ref, xs_ref, wup_ref, bup_ref, wdn_ref, bdn_ref, ys_ref,
                   wup_s, wdn_s, xbuf, obuf, in_sem, out_sem):
    e = pl.program_id(0)
    tm = TM_MOE
    half = D_MODEL // 2
    pad = meta_ref[META_PAD, e]
    n_t = pad // tm
    row0 = meta_ref[META_END, e] - pad
    cnt = meta_ref[META_CNT, e]

    def rows(i):
        return pl.ds(pl.multiple_of(row0 + i * tm, tm), tm)

    def x_copy(i, slot):
        return pltpu.make_async_copy(xs_ref.at[rows(i)], xbuf.at[slot], in_sem.at[slot])

    def y_copy(i, slot):
        return pltpu.make_async_copy(obuf.at[slot], ys_ref.at[rows(i)], out_sem.at[slot])

    @pl.when(n_t > 0)
    def _():
        x_copy(0, 0).start()
        wup_s[...] = wup_ref[0, 0].astype(BF16)
        wdn_s[...] = wdn_ref[0, 0].astype(BF16)

        def tile(i, carry):
            slot = lax.rem(i, 2)
            x_copy(i, slot).wait()

            @pl.when(i + 1 < n_t)
            def _():
                x_copy(i + 1, 1 - slot).start()

            @pl.when(i >= 2)
            def _():
                y_copy(i - 2, slot).wait()

            row = lax.broadcasted_iota(jnp.int32, (tm, 1), 0)
            lo, hi = _unpack_bf16_pairs(jnp.where(row < cnt - i * tm, xbuf[slot], 0))
            z = (jnp.dot(lo.astype(BF16), wup_s[0:half, :], preferred_element_type=F32)
                 + jnp.dot(hi.astype(BF16), wup_s[half:, :], preferred_element_type=F32) + bup_ref[0, 0])
            glu = jnp.minimum(z[:, :D_EXPERT], SWIGLU_LIMIT)
            lin = jnp.clip(z[:, D_EXPERT:], -SWIGLU_LIMIT, SWIGLU_LIMIT)
            act = (glu * _sigmoid(SWIGLU_ALPHA * glu) * (lin + 1.0)).astype(BF16)
            y = jnp.dot(act, wdn_s[...], preferred_element_type=F32) + bdn_ref[0, 0]
            obuf[slot] = _pack_bf16_pairs(y)
            y_copy(i, slot).start()
            return carry

        lax.fori_loop(0, n_t, tile, 0)

        @pl.when(n_t >= 2)
        def _():
            y_copy(n_t - 2, lax.rem(n_t, 2)).wait()

        y_copy(n_t - 1, lax.rem(n_t - 1, 2)).wait()

    @pl.when(e == N_EXPERTS - 1)
    def _():
        obuf[0] = jnp.zeros((tm, half), jnp.int32)

        def fill(i, carry):
            cp = pltpu.make_async_copy(obuf.at[0], ys_ref.at[pl.ds(pl.multiple_of(i * tm, tm), tm)], out_sem.at[0])
            cp.start()
            cp.wait()
            return carry

        lax.fori_loop(meta_ref[META_END, N_EXPERTS - 1] // tm, ys_ref.shape[0] // tm, fill, 0)


def _expert_mlp(xs, meta, layer, w_up, b_up, w_down, b_down):
    n_rows, half = xs.shape
    d = 2 * half
    tm = TM_MOE
    f = w_up.shape[-1]
    b_up4 = b_up.reshape(DEPTH, N_EXPERTS, 1, f)
    b_dn4 = b_down.reshape(DEPTH, N_EXPERTS, 1, d)
    wsel = lambda e, m: (layer, e, 0, 0)
    grid_spec = pltpu.PrefetchScalarGridSpec(
        num_scalar_prefetch=1,
        grid=(N_EXPERTS,),
        in_specs=[pl.BlockSpec(memory_space=pl.ANY),
                  pl.BlockSpec((1, 1, d, f), wsel),
                  pl.BlockSpec((1, 1, 1, f), wsel),
                  pl.BlockSpec((1, 1, f // 2, d), wsel),
                  pl.BlockSpec((1, 1, 1, d), wsel)],
        out_specs=pl.BlockSpec(memory_space=pl.ANY),
        scratch_shapes=[pltpu.VMEM((d, f), BF16), pltpu.VMEM((f // 2, d), BF16),
                        pltpu.VMEM((2, tm, half), jnp.int32), pltpu.VMEM((2, tm, half), jnp.int32),
                        pltpu.SemaphoreType.DMA((2,)), pltpu.SemaphoreType.DMA((2,))],
    )
    return pl.pallas_call(
        _expert_kernel,
        grid_spec=grid_spec,
        out_shape=jax.ShapeDtypeStruct((n_rows, half), jnp.int32),
        compiler_params=_cparams(("arbitrary",)),
        name="expert_mlp",
    )(meta, xs, w_up, b_up4, w_down, b_dn4)


def _combine_kernel(rows_ref, prob_ref, x_ref, gate_ref, lng_ref, lnb_ref, o_ref):
    p = prob_ref[...]
    y = None
    for k in range(TOP_K):
        yk = p[:, k:k + 1] * jnp.concatenate(_unpack_bf16_pairs(rows_ref[k]), axis=1)
        y = yk if y is None else y + yk
    o_ref[0] = _layer_norm(DN_ALPHA * x_ref[0] + (1.0 + gate_ref[0]) * y, lng_ref[...], lnb_ref[...])


def _combine(rows, prob_c, x1, gate, ln_g, ln_b):
    bsz, s, d = x1.shape
    tm = TM_DISP
    nt = s // tm
    return pl.pallas_call(
        _combine_kernel,
        grid=(bsz, nt),
        in_specs=[pl.BlockSpec((TOP_K, tm, d // 2), lambda b, i: (0, b * nt + i, 0)),
                  pl.BlockSpec((tm, LANES), lambda b, i: (b * nt + i, 0)),
                  pl.BlockSpec((1, tm, d), lambda b, i: (b, i, 0)),
                  pl.BlockSpec((1, 1, d), lambda b, i: (b, 0, 0)),
                  pl.BlockSpec((1, d), lambda b, i: (0, 0)),
                  pl.BlockSpec((1, d), lambda b, i: (0, 0))],
        out_specs=pl.BlockSpec((1, tm, d), lambda b, i: (b, i, 0)),
        out_shape=jax.ShapeDtypeStruct((bsz, s, d), F32),
        compiler_params=_cparams(("arbitrary", "arbitrary")),
        name="combine",
    )(rows, prob_c, x1, gate.reshape(bsz, 1, d), ln_g.reshape(1, d), ln_b.reshape(1, d))


def kernel(x, c, positions, ada_w, ada_b, w_in, lam_q1, lam_k1, lam_q2, lam_k2, da_norm_g, ml_conv_w, ml_conv_b,
           ml_w_q, ml_w_k, ml_gate_b, ml_norm_g, s5_a_re, s5_a_im, s5_log_dt, s5_b_re, s5_b_im, s5_c_re, s5_c_im,
           s5_d, s5_w_glu, w_out, ln_g, ln_b, w_router, b_router, w_up, b_up, w_down, b_down):
    bsz, s, d = x.shape
    t = bsz * s
    n_tiles_max = (t * TOP_K) // TM_MOE + N_EXPERTS
    n_rows = n_tiles_max * TM_MOE
    mod = _modulation(c, ada_w, ada_b)
    cos_t, sin_t = _rope_tables(positions)
    for l in range(DEPTH):
        shift, scale, gate = jnp.split(mod[2 * l], 3, axis=-1)
        q, k, v, mlx, mlv, mlo, g_t, g_c, s5u = _in_proj(x, shift, scale, cos_t, sin_t, w_in, l)
        lam_init = 0.8 - 0.6 * math.exp(-0.3 * l)
        lamv = jnp.stack([lam_q1[l], lam_k1[l], lam_q2[l], lam_k2[l]])
        y_da = _diff_attn(q, k, v, lamv, da_norm_g[l], lam_init)
        y_ml = _mlstm(mlx, mlv, mlo, g_t, g_c, ml_conv_w[l], ml_conv_b[l], ml_w_q[l], ml_w_k[l],
                      ml_gate_b[l], ml_norm_g[l])
        s5p = _s5_params(s5_a_re[l], s5_a_im[l], s5_log_dt[l], s5_b_re[l], s5_b_im[l], s5_c_re[l], s5_c_im[l],
                         s5_w_glu[l])
        y_s5 = _s5(s5u, bsz, s5p, s5_d[l])
        shift2, scale2, gate2 = jnp.split(mod[2 * l + 1], 3, axis=-1)
        x1, h2, eid, prob, counts = _out_proj(y_da, y_ml, y_s5, x, gate, ln_g[l, 0], ln_b[l, 0], shift2, scale2,
                                      w_out, l, w_router[l], b_router[l])
        pos8, meta = _route(eid, counts)
        xs = _dispatch(h2, pos8, n_rows)
        ys = _expert_mlp(xs, meta, l, w_up, b_up, w_down, b_down)
        rows = _gather_expert_rows(ys, pos8)
        x = _combine(rows, prob, x1, gate2, ln_g[l, 1], ln_b[l, 1])
    return x
```

```python
import functools
import math

import jax
import jax.numpy as jnp
from jax import lax
from jax.experimental import pallas as pl
from jax.experimental.pallas import tpu as pltpu
from jax.experimental.pallas import tpu_sc as plsc

F32 = jnp.float32
BF16 = jnp.bfloat16
HIGHEST = lax.Precision.HIGHEST

D_MODEL = 1024
DEPTH = 2
DA_HEADS = 4
DA_HEAD_DIM = 64
DA_V_DIM = 2 * DA_HEAD_DIM
DA_WIDTH = DA_HEADS * DA_V_DIM
DA_QK_WIDTH = DA_HEADS * 2 * DA_HEAD_DIM
ROPE_THETA = 10000.0
ML_HEADS = 4
ML_HEAD_DIM = 64
ML_WIDTH = ML_HEADS * ML_HEAD_DIM
ML_CONV = 4
S5_GROUP = 16
S5_STATE = 64
S5_WIDTH = D_MODEL - DA_WIDTH - ML_WIDTH
S5_GROUPS = S5_WIDTH // S5_GROUP
S5_NSTATE = S5_GROUPS * S5_STATE
N_EXPERTS = 32
TOP_K = 4
D_EXPERT = D_MODEL
SWIGLU_LIMIT = 7.0
SWIGLU_ALPHA = 1.702
DN_ALPHA = (2 * DEPTH) ** 0.25
LN_EPS = 1e-5
NEG = -1e30

OFF_DA_K = DA_QK_WIDTH
OFF_DA_V = 2 * DA_QK_WIDTH
OFF_ML_X = OFF_DA_V + DA_WIDTH
OFF_ML_V = OFF_ML_X + ML_WIDTH
OFF_ML_O = OFF_ML_V + ML_WIDTH
OFF_ML_I = OFF_ML_O + ML_WIDTH
OFF_ML_F = OFF_ML_I + ML_HEADS
OFF_S5_U = OFF_ML_F + ML_HEADS
N_IN = OFF_S5_U + S5_WIDTH

LANES = 128
SUBLANES = 8
BF16_SUBLANES = 16
VMEM_LIMIT_BYTES = 56 * 1024 * 1024

TM_PROJ = 1024
TQ = 512
ML_CHUNK = 256
ML_NB = 4
S5_TC = 256
S5_UNROLL = 8
TB_RANK = 1024
TM_MOE = 256
TM_DISP = 512
SC_SCATTER_CHUNK = 128
SC_GATHER_CHUNK = 64
GATE_PAD = 8
VT_ROWS = DA_V_DIM + BF16_SUBLANES
TN_MOD = 1024
TS_ROPE = 512
HALF_BITS = 16
LOW_HALF = (1 << HALF_BITS) - 1
HIGH_HALF = ~LOW_HALF
Q_PRESCALE = DA_HEAD_DIM ** -0.5 * math.log2(math.e)


def _cparams(sem, vmem=VMEM_LIMIT_BYTES):
    return pltpu.CompilerParams(dimension_semantics=sem, vmem_limit_bytes=vmem)


def _sigmoid(x):
    return 1.0 / (1.0 + jnp.exp(-x))


def _mod_kernel(c_ref, w_ref, b_ref, o_ref):
    c = c_ref[...]
    ca = (c * _sigmoid(c)).astype(BF16)
    w = w_ref[0, 0].astype(BF16)
    o_ref[0] = jnp.dot(ca, w, preferred_element_type=F32) + b_ref[0]


def _modulation(c, ada_w, ada_b):
    nsub = ada_w.shape[1]
    nmod = ada_w.shape[0] * nsub
    bsz, d = c.shape
    e = ada_w.shape[-1]
    tn = TN_MOD
    b = ada_b.reshape(nmod, 1, e)
    return pl.pallas_call(
        _mod_kernel,
        grid=(nmod, e // tn),
        in_specs=[pl.BlockSpec((bsz, d), lambda n, j: (0, 0)),
                  pl.BlockSpec((1, 1, d, tn), lambda n, j: (n // nsub, n % nsub, 0, j)),
                  pl.BlockSpec((1, 1, tn), lambda n, j: (n, 0, j))],
        out_specs=pl.BlockSpec((1, bsz, tn), lambda n, j: (n, 0, j)),
        out_shape=jax.ShapeDtypeStruct((nmod, bsz, e), F32),
        compiler_params=_cparams(("arbitrary", "arbitrary")),
        name="modulation",
    )(c, ada_w, b)


def _rope_kernel(pos_ref, cos_ref, sin_ref):
    nfreq = DA_HEAD_DIM // 2
    pos = pos_ref[0].astype(F32)
    fidx = lax.broadcasted_iota(jnp.int32, (nfreq, 1), 0).astype(F32)
    inv = jnp.exp(fidx * (-2.0 * math.log(ROPE_THETA) / DA_HEAD_DIM))
    ang = inv * pos
    reps = LANES // nfreq
    cos_t = jnp.concatenate([jnp.cos(ang)] * reps, axis=0).T
    sin_t = jnp.concatenate([jnp.sin(ang)] * reps, axis=0).T
    lane = lax.broadcasted_iota(jnp.int32, (1, LANES), 1)
    sign = jnp.where((lane % DA_HEAD_DIM) < nfreq, -1.0, 1.0)
    cos_ref[0] = cos_t
    sin_ref[0] = sin_t * sign


def _rope_tables(positions):
    bsz, s = positions.shape
    ts = TS_ROPE
    pos3 = positions.reshape(bsz, 1, s)
    return pl.pallas_call(
        _rope_kernel,
        grid=(bsz, s // ts),
        in_specs=[pl.BlockSpec((1, 1, ts), lambda b, i: (b, 0, i))],
        out_specs=[pl.BlockSpec((1, ts, LANES), lambda b, i: (b, i, 0))] * 2,
        out_shape=[jax.ShapeDtypeStruct((bsz, s, LANES), F32)] * 2,
        compiler_params=_cparams(("arbitrary", "arbitrary")),
        name="rope_tables",
    )(pos3)


def _in_proj_kernel(x_ref, shift_ref, scale_ref, cos_ref, sin_ref, win_ref,
                    q_ref, k_ref, vt_ref, mlx_ref, mlv_ref, mlo_ref, gt_ref, gc_ref, s5u_ref,
                    wqk_ref, wvt_ref, wrest_ref, wgt_ref, wgc_ref):
    @pl.when(jnp.logical_and(pl.program_id(0) == 0, pl.program_id(1) == 0))
    def _():
        wqk_ref[...] = win_ref[0, 0:OFF_DA_V, :].T.astype(BF16)
        wvt_ref[...] = win_ref[0, OFF_DA_V:OFF_ML_X, :].astype(BF16)
        wrest_ref[:, 0:3 * ML_WIDTH] = win_ref[0, OFF_ML_X:OFF_ML_I, :].T.astype(BF16)
        wrest_ref[:, 3 * ML_WIDTH:] = win_ref[0, OFF_S5_U:N_IN, :].T.astype(BF16)
        wgt_ref[...] = win_ref[0, OFF_ML_I:OFF_S5_U, :].astype(BF16)
        gslab = win_ref[0, OFF_ML_I:OFF_ML_I + LANES, :].T
        glane = lax.broadcasted_iota(jnp.int32, (1, LANES), 1)
        wgc_ref[...] = jnp.where(glane < GATE_PAD, gslab, 0.0).astype(BF16)

    h = (x_ref[0] * (1.0 + scale_ref[0]) + shift_ref[0]).astype(BF16)
    cos = cos_ref[0]
    sin = sin_ref[0]
    lane = lax.broadcasted_iota(jnp.int32, (1, LANES), 1)
    lo_half = (lane % DA_HEAD_DIM) < DA_HEAD_DIM // 2
    half = DA_HEAD_DIM // 2

    def rope(t):
        fwd = pltpu.roll(t, half, 1)
        bwd = pltpu.roll(t, LANES - half, 1)
        partner = jnp.where(lo_half, bwd, fwd)
        return t * cos + partner * sin

    qk = jnp.dot(h, wqk_ref[...], preferred_element_type=F32)
    nslab = DA_QK_WIDTH // LANES
    for c in range(nslab):
        q_ref[0, :, c * LANES:(c + 1) * LANES] = (
            rope(qk[:, c * LANES:(c + 1) * LANES]) * Q_PRESCALE).astype(BF16)
        k_ref[0, :, c * LANES:(c + 1) * LANES] = rope(
            qk[:, DA_QK_WIDTH + c * LANES:DA_QK_WIDTH + (c + 1) * LANES]).astype(BF16)

    vt = lax.dot_general(wvt_ref[...], h, (((1,), (1,)), ((), ())), preferred_element_type=F32)
    tm = h.shape[0]
    for hh in range(DA_HEADS):
        for jj in range(tm // TQ):
            vt_ref[0, hh, jj, 0:DA_V_DIM, :] = vt[hh * DA_V_DIM:(hh + 1) * DA_V_DIM,
                                                  jj * TQ:(jj + 1) * TQ].astype(BF16)
            vt_ref[0, hh, jj, DA_V_DIM:VT_ROWS, :] = jnp.ones((VT_ROWS - DA_V_DIM, TQ), BF16)

    r = jnp.dot(h, wrest_ref[...], preferred_element_type=F32)
    o = 0
    mlx_ref[0] = r[:, o:o + ML_WIDTH].astype(BF16); o += ML_WIDTH
    mlv_ref[0] = r[:, o:o + ML_WIDTH].astype(BF16); o += ML_WIDTH
    mlo_ref[0] = r[:, o:o + ML_WIDTH].astype(BF16); o += ML_WIDTH
    s5u_ref[...] = r[:, o:o + S5_WIDTH]
    gt = lax.dot_general(wgt_ref[...], h, (((1,), (1,)), ((), ())), preferred_element_type=F32)
    for c in range(h.shape[0] // ML_CHUNK):
        gt_ref[0, c] = gt[:, c * ML_CHUNK:(c + 1) * ML_CHUNK]
    gc_ref[0] = jnp.dot(h, wgc_ref[...], preferred_element_type=F32)


def _in_proj(x, shift, scale, cos_t, sin_t, w_in, layer):
    bsz, s, d = x.shape
    tm = TM_PROJ
    nrest = 3 * ML_WIDTH + S5_WIDTH
    shift3 = shift.reshape(bsz, 1, d)
    scale3 = scale.reshape(bsz, 1, d)
    tok = lambda b, i: (b, i, 0)
    per_b = lambda b, i: (b, 0, 0)
    out_shapes = [
        jax.ShapeDtypeStruct((bsz, s, DA_QK_WIDTH), BF16),
        jax.ShapeDtypeStruct((bsz, s, DA_QK_WIDTH), BF16),
        jax.ShapeDtypeStruct((bsz, DA_HEADS, s // TQ, VT_ROWS, TQ), BF16),
        jax.ShapeDtypeStruct((bsz, s, ML_WIDTH), BF16),
        jax.ShapeDtypeStruct((bsz, s, ML_WIDTH), BF16),
        jax.ShapeDtypeStruct((bsz, s, ML_WIDTH), BF16),
        jax.ShapeDtypeStruct((bsz, s // ML_CHUNK, GATE_PAD, ML_CHUNK), F32),
        jax.ShapeDtypeStruct((bsz, s, LANES), F32),
        jax.ShapeDtypeStruct((s, bsz * S5_WIDTH), F32),
    ]
    out_specs = [
        pl.BlockSpec((1, tm, DA_QK_WIDTH), tok),
        pl.BlockSpec((1, tm, DA_QK_WIDTH), tok),
        pl.BlockSpec((1, DA_HEADS, tm // TQ, VT_ROWS, TQ), lambda b, i: (b, 0, i, 0, 0)),
        pl.BlockSpec((1, tm, ML_WIDTH), tok),
        pl.BlockSpec((1, tm, ML_WIDTH), tok),
        pl.BlockSpec((1, tm, ML_WIDTH), tok),
        pl.BlockSpec((1, tm // ML_CHUNK, GATE_PAD, ML_CHUNK), lambda b, i: (b, i, 0, 0)),
        pl.BlockSpec((1, tm, LANES), tok),
        pl.BlockSpec((tm, S5_WIDTH), lambda b, i: (i, b)),
    ]
    return pl.pallas_call(
        _in_proj_kernel,
        grid=(bsz, s // tm),
        in_specs=[pl.BlockSpec((1, tm, d), tok),
                  pl.BlockSpec((1, 1, d), per_b),
                  pl.BlockSpec((1, 1, d), per_b),
                  pl.BlockSpec((1, tm, LANES), tok),
                  pl.BlockSpec((1, tm, LANES), tok),
                  pl.BlockSpec((1, N_IN, d), lambda b, i: (layer, 0, 0))],
        out_specs=out_specs,
        out_shape=out_shapes,
        scratch_shapes=[pltpu.VMEM((d, OFF_DA_V), BF16), pltpu.VMEM((DA_WIDTH, d), BF16),
                        pltpu.VMEM((d, nrest), BF16), pltpu.VMEM((GATE_PAD, d), BF16),
                        pltpu.VMEM((d, LANES), BF16)],
        compiler_params=_cparams(("arbitrary", "arbitrary")),
        name="in_proj",
    )(x, shift3, scale3, cos_t, sin_t, jnp.swapaxes(w_in, 1, 2))


def _diff_attn_kernel(lam_init, lamv_ref, gain_ref, q_ref, k_ref, vt_ref, o_ref, acc_s, m_s):
    qi = pl.program_id(2)
    tq = q_ref.shape[1]
    lane = lax.broadcasted_iota(jnp.int32, (1, LANES), 1)
    first = lane < DA_HEAD_DIM
    q = q_ref[0]
    zero = jnp.zeros_like(q)
    qm = (jnp.where(first, q, zero), jnp.where(first, zero, q))
    acc_s[...] = jnp.zeros_like(acc_s)
    m_s[...] = jnp.full(m_s.shape, NEG, F32)

    def step(j, nblk, masked):
        tk = nblk * tq
        kb = k_ref[0, pl.ds(pl.multiple_of(j * tq, tq), tk), :]
        vtb = vt_ref[0, 0, j] if nblk == 1 else jnp.concatenate([vt_ref[0, 0, j + b] for b in range(nblk)], axis=1)
        for c in range(2):
            st = lax.dot_general(kb, qm[c], (((1,), (1,)), ((), ())), preferred_element_type=F32)
            if masked:
                key_i = lax.broadcasted_iota(jnp.int32, (tk, tq), 0) - (tk - tq)
                qry_i = lax.broadcasted_iota(jnp.int32, (tk, tq), 1)
                st = jnp.where(key_i <= qry_i, st, NEG)
            m_prev = m_s[c]
            m_new = jnp.maximum(m_prev, jnp.max(st, axis=0, keepdims=True))
            alpha = jnp.exp2(m_prev - m_new)
            p = jnp.exp2(st - m_new).astype(BF16)
            acc_s[c] = alpha * acc_s[c] + jnp.dot(vtb, p, preferred_element_type=F32)
            m_s[c] = m_new

    def body(jj, carry):
        step(2 * jj, 2, False)
        return carry

    lax.fori_loop(0, qi // 2, body, 0)

    @pl.when(qi % 2 == 1)
    def _():
        step(qi - 1, 2, True)

    @pl.when(qi % 2 == 0)
    def _():
        step(qi, 1, True)

    outs = []
    for c in range(2):
        acc = acc_s[c]
        outs.append(acc[:DA_V_DIM] / acc[DA_V_DIM:DA_V_DIM + 1])

    lamv = lamv_ref[...]
    lam = (jnp.exp(jnp.sum(lamv[0:1] * lamv[1:2], axis=1, keepdims=True))
           - jnp.exp(jnp.sum(lamv[2:3] * lamv[3:4], axis=1, keepdims=True)) + lam_init)
    ot = outs[0] - lam * outs[1]
    ms = jnp.mean(ot * ot, axis=0, keepdims=True)
    ot = ot * (lax.rsqrt(ms + LN_EPS) * (1.0 - lam_init))
    o_ref[0] = (ot.T * gain_ref[...]).astype(o_ref.dtype)


def _diff_attn(q, k, vt, lamv, gain, lam_init):
    bsz, s, _ = q.shape
    tq = TQ
    nq = s // tq
    return pl.pallas_call(
        functools.partial(_diff_attn_kernel, lam_init),
        grid=(bsz, DA_HEADS, nq),
        in_specs=[pl.BlockSpec((4, DA_HEAD_DIM), lambda b, h, i: (0, 0)),
                  pl.BlockSpec((1, DA_V_DIM), lambda b, h, i: (0, 0)),
                  pl.BlockSpec((1, tq, DA_V_DIM), lambda b, h, i: (b, i, h)),
                  pl.BlockSpec((1, s, DA_V_DIM), lambda b, h, i: (b, 0, h)),
                  pl.BlockSpec((1, 1, nq, VT_ROWS, tq), lambda b, h, i: (b, h, 0, 0, 0))],
        out_specs=pl.BlockSpec((1, tq, DA_V_DIM), lambda b, h, i: (b, i, h)),
        out_shape=jax.ShapeDtypeStruct((bsz, s, DA_WIDTH), BF16),
        scratch_shapes=[pltpu.VMEM((2, VT_ROWS, tq), F32), pltpu.VMEM((2, 1, tq), F32)],
        compiler_params=_cparams(("arbitrary", "arbitrary", "arbitrary")),
        name="diff_attn",
    )(lamv, gain.reshape(1, DA_V_DIM), q, k, vt)


def _log_sigmoid(x):
    return jnp.minimum(x, 0.0) - jnp.log(1.0 + jnp.exp(-jnp.abs(x)))


def _split3(a):
    hi = a.astype(BF16)
    r1 = a - hi.astype(F32)
    mid = r1.astype(BF16)
    lo = (r1 - mid.astype(F32)).astype(BF16)
    return hi, mid, lo


def _mlstm_kernel(x_ref, v_ref, o_ref, gt_ref, gc_ref, cw_ref, cb_ref, wq_ref, wkt_ref, gbt_ref, gbc_ref,
                  ng_ref, hmean_ref, y_ref, xc_s, c_s, m_s):
    nb, s = x_ref.shape[0], x_ref.shape[1]
    L = ML_CHUNK
    H, dh = ML_HEADS, ML_HEAD_DIM
    nc = s // L
    cw = cw_ref[...]
    row = lax.broadcasted_iota(jnp.int32, (s, 1), 0)
    for bi in range(nb):
        x = x_ref[bi].astype(F32)
        xc = x * cw[ML_CONV - 1:ML_CONV]
        for j in range(1, ML_CONV):
            xs = jnp.where(row >= j, pltpu.roll(x, j, 0), 0.0)
            xc = xc + xs * cw[ML_CONV - 1 - j:ML_CONV - j]
        xc = xc + cb_ref[...]
        xc_s[bi] = (xc * _sigmoid(xc)).astype(BF16)

    c_s[...] = jnp.zeros_like(c_s)
    m_s[...] = jnp.full(m_s.shape, NEG, F32)

    ri = lax.broadcasted_iota(jnp.int32, (L, L), 0)
    ci = lax.broadcasted_iota(jnp.int32, (L, L), 1)
    causal = ci <= ri
    tril = causal.astype(BF16)
    triu = (ri <= ci).astype(BF16)
    lane = lax.broadcasted_iota(jnp.int32, (1, dh), 1)
    one_hot0 = jnp.broadcast_to((lane == 0).astype(BF16), (L, dh))

    def chunk_one(bi, ci_, t0):
        xcc = xc_s[bi, pl.ds(t0, L), :]
        qc = jnp.dot(xcc, wq_ref[...], preferred_element_type=F32).astype(BF16)
        ktc = lax.dot_general(wkt_ref[...], xcc, (((1,), (1,)), ((), ())),
                              preferred_element_type=F32)
        g_rows = gt_ref[bi, ci_] + gbt_ref[...]
        g_cols = gc_ref[bi, pl.ds(t0, L), :] + gbc_ref[...]
        lf_rows = _log_sigmoid(g_rows)
        lf_cols = _log_sigmoid(g_cols)
        r3 = jnp.dot(jnp.concatenate(_split3(lf_rows), axis=0), triu, preferred_element_type=F32)
        b_rows = r3[0:GATE_PAD] + r3[GATE_PAD:2 * GATE_PAD] + r3[2 * GATE_PAD:]
        c3 = jnp.dot(tril, jnp.concatenate(_split3(lf_cols), axis=1), preferred_element_type=F32)
        b_cols = c3[:, 0:LANES] + c3[:, LANES:2 * LANES] + c3[:, 2 * LANES:]
        vch = v_ref[bi, pl.ds(t0, L), :]
        och = o_ref[bi, pl.ds(t0, L), :].astype(F32)
        hs = []
        for h in range(H):
            br = b_rows[H + h:H + h + 1, :]
            ir = g_rows[h:h + 1, :]
            bc = b_cols[:, H + h:H + h + 1]
            m_prev = m_s[bi, h]
            log_d = jnp.where(causal, bc - br + ir, NEG)
            inter = bc + m_prev
            mx = jnp.maximum(inter, jnp.max(log_d, axis=1, keepdims=True))
            dmat = jnp.exp(log_d - mx)
            dec = jnp.exp(inter - mx)
            qh = qc[:, h * dh:(h + 1) * dh]
            kth = ktc[h * dh:(h + 1) * dh, :]
            vaug = jnp.concatenate([vch[:, h * dh:(h + 1) * dh], one_hot0], axis=1)
            sm = (jnp.dot(qh, kth.astype(BF16), preferred_element_type=F32) * dmat).astype(BF16)
            c_prev = c_s[bi, h]
            na = (jnp.dot(sm, vaug, preferred_element_type=F32)
                  + dec * jnp.dot(qh, c_prev.astype(BF16), preferred_element_type=F32))
            den = na[:, dh:dh + 1]
            hs.append(na[:, :dh] / jnp.maximum(jnp.abs(den), jnp.exp(-mx)))
            g_tot = br[:, L - 1:L]
            a_row = g_tot - br + ir
            m_new = jnp.maximum(g_tot + m_prev, jnp.max(a_row, axis=1, keepdims=True))
            decay = jnp.exp(g_tot + m_prev - m_new)
            w_row = jnp.exp(a_row - m_new)
            kw = (kth * w_row).astype(BF16)
            c_s[bi, h] = decay * c_prev + jnp.dot(kw, vaug, preferred_element_type=F32)
            m_s[bi, h] = m_new
        hcat = jnp.concatenate(hs, axis=1)
        m3 = jnp.dot(jnp.concatenate(_split3(hcat * hcat), axis=0), hmean_ref[...],
                     preferred_element_type=F32)
        ms = m3[0:L] + m3[L:2 * L] + m3[2 * L:]
        y = hcat * lax.rsqrt(ms + LN_EPS) * ng_ref[...] * _sigmoid(och)
        y_ref[bi, pl.ds(t0, L), :] = y.astype(y_ref.dtype)

    def chunk(ci_, _):
        t0 = pl.multiple_of(ci_ * L, L)
        for bi in range(nb):
            chunk_one(bi, ci_, t0)
        return 0

    lax.fori_loop(0, nc, chunk, 0)


def _mlstm(mlx, mlv, mlo, g_t, g_c, conv_w, conv_b, w_q, w_k, gate_b, norm_g):
    bsz, s, _ = mlx.shape
    H, dh = ML_HEADS, ML_HEAD_DIM
    eye = jnp.eye(H, dtype=F32)
    wq_bd = jnp.einsum('hde,hg->hdge', w_q, eye).reshape(ML_WIDTH, ML_WIDTH).astype(BF16)
    wk_bd = jnp.einsum('hde,hg->hdge', w_k * (dh ** -0.5), eye).reshape(ML_WIDTH, ML_WIDTH)
    wkt_bd = wk_bd.T.astype(BF16)
    gbt = gate_b.reshape(GATE_PAD, 1)
    gbc = jnp.pad(gate_b.reshape(1, GATE_PAD), ((0, 0), (0, LANES - GATE_PAD)))
    hmean = jnp.kron(eye, jnp.full((dh, dh), 1.0 / dh, F32)).astype(BF16)
    nc = s // ML_CHUNK
    tok = lambda b: (b, 0, 0)
    c2 = lambda b: (0, 0)
    nb = ML_NB
    return pl.pallas_call(
        _mlstm_kernel,
        grid=(bsz // nb,),
        in_specs=[pl.BlockSpec((nb, s, ML_WIDTH), tok),
                  pl.BlockSpec((nb, s, ML_WIDTH), tok),
                  pl.BlockSpec((nb, s, ML_WIDTH), tok),
                  pl.BlockSpec((nb, nc, GATE_PAD, ML_CHUNK), lambda b: (b, 0, 0, 0)),
                  pl.BlockSpec((nb, s, LANES), tok),
                  pl.BlockSpec((ML_CONV, ML_WIDTH), c2),
                  pl.BlockSpec((1, ML_WIDTH), c2),
                  pl.BlockSpec((ML_WIDTH, ML_WIDTH), c2),
                  pl.BlockSpec((ML_WIDTH, ML_WIDTH), c2),
                  pl.BlockSpec((GATE_PAD, 1), c2),
                  pl.BlockSpec((1, LANES), c2),
                  pl.BlockSpec((1, ML_WIDTH), c2),
                  pl.BlockSpec((ML_WIDTH, ML_WIDTH), c2)],
        out_specs=pl.BlockSpec((nb, s, ML_WIDTH), tok),
        out_shape=jax.ShapeDtypeStruct((bsz, s, ML_WIDTH), BF16),
        scratch_shapes=[pltpu.VMEM((nb, s, ML_WIDTH), BF16),
                        pltpu.VMEM((nb, H, dh, LANES), F32),
                        pltpu.VMEM((nb, H, 1, 1), F32)],
        compiler_params=_cparams(("arbitrary",)),
        name="mlstm",
    )(mlx, mlv, mlo, g_t, g_c, conv_w, conv_b.reshape(1, ML_WIDTH), wq_bd, wkt_bd, gbt, gbc,
      norm_g.reshape(1, ML_WIDTH), hmean)


def _gelu_tanh(x):
    return 0.5 * x * (1.0 + jnp.tanh(math.sqrt(2.0 / math.pi) * (x + 0.044715 * (x * x * x))))


def _s5_kernel(u_ref, are_ref, aim_ref, bcat_ref, ccat_ref, d_ref, wglu_ref, y_ref, xs_s, st_s):
    tc, bsz, w = u_ref.shape
    n = S5_NSTATE

    @pl.when(pl.program_id(0) == 0)
    def _():
        st_s[...] = jnp.zeros_like(st_s)

    u = u_ref[...].reshape(tc * bsz, w)
    xs_s[...] = jnp.dot(u.astype(BF16), bcat_ref[...], preferred_element_type=F32).reshape(tc, bsz, 2 * n)
    a_re = jnp.broadcast_to(are_ref[...], (bsz, n))
    a_im = jnp.broadcast_to(aim_ref[...], (bsz, n))

    def step(t, carry):
        x_re, x_im = carry
        bu = xs_s[t]
        n_re = a_re * x_re - a_im * x_im + bu[:, :n]
        n_im = a_re * x_im + a_im * x_re + bu[:, n:]
        xs_s[t] = jnp.concatenate([n_re, n_im], axis=1)
        return n_re, n_im

    x_re, x_im = lax.fori_loop(0, tc, step, (st_s[0], st_s[1]), unroll=S5_UNROLL)
    st_s[0] = x_re
    st_s[1] = x_im

    xs = xs_s[...].reshape(tc * bsz, 2 * n).astype(BF16)
    y = jnp.dot(xs, ccat_ref[...], preferred_element_type=F32) + d_ref[...] * u
    z = jnp.dot(_gelu_tanh(y).astype(BF16), wglu_ref[...], preferred_element_type=F32)
    out = z[:, :w] * _sigmoid(z[:, w:])
    y_ref[...] = out.reshape(tc, bsz, w).astype(y_ref.dtype)


def _s5_params(a_re, a_im, log_dt, b_re, b_im, c_re, c_im, w_glu):
    G, P, Hc = S5_GROUPS, S5_STATE, S5_GROUP
    dt = jnp.exp(log_dt)[:, None]
    mag = jnp.exp(a_re * dt)
    ab_re = mag * jnp.cos(a_im * dt)
    ab_im = mag * jnp.sin(a_im * dt)
    nr, ni = ab_re - 1.0, ab_im
    den = a_re * a_re + a_im * a_im
    fr = (nr * a_re + ni * a_im) / den
    fi = (ni * a_re - nr * a_im) / den
    bb_re = fr[..., None] * b_re - fi[..., None] * b_im
    bb_im = fr[..., None] * b_im + fi[..., None] * b_re
    eye = jnp.eye(G, dtype=F32)
    bd = lambda t, sub: jnp.einsum(sub, t, eye)
    bre = bd(bb_re, 'gph,gk->ghkp').reshape(G * Hc, G * P)
    bim = bd(bb_im, 'gph,gk->ghkp').reshape(G * Hc, G * P)
    bcat = jnp.concatenate([bre, bim], axis=1).astype(BF16)
    cre = bd(c_re, 'ghp,gk->gpkh').reshape(G * P, G * Hc)
    cim = bd(c_im, 'ghp,gk->gpkh').reshape(G * P, G * Hc)
    ccat = jnp.concatenate([cre, -cim], axis=0).astype(BF16)
    wv = bd(w_glu[:, :, :Hc], 'ghj,gk->ghkj').reshape(G * Hc, G * Hc)
    wg = bd(w_glu[:, :, Hc:], 'ghj,gk->ghkj').reshape(G * Hc, G * Hc)
    wglu = jnp.concatenate([wv, wg], axis=1).astype(BF16)
    return ab_re.reshape(1, G * P), ab_im.reshape(1, G * P), bcat, ccat, wglu


def _s5(u_tm, bsz, params, d_skip):
    s = u_tm.shape[0]
    w = S5_WIDTH
    n = S5_NSTATE
    are, aim, bcat, ccat, wglu = params
    u3 = u_tm.reshape(s, bsz, w)
    tc = S5_TC
    c2 = lambda i: (0, 0)
    y = pl.pallas_call(
        _s5_kernel,
        grid=(s // tc,),
        in_specs=[pl.BlockSpec((tc, bsz, w), lambda i: (i, 0, 0)),
                  pl.BlockSpec((1, n), c2),
                  pl.BlockSpec((1, n), c2),
                  pl.BlockSpec((w, 2 * n), c2),
                  pl.BlockSpec((2 * n, w), c2),
                  pl.BlockSpec((1, w), c2),
                  pl.BlockSpec((w, 2 * w), c2)],
        out_specs=pl.BlockSpec((tc, bsz, w), lambda i: (i, 0, 0)),
        out_shape=jax.ShapeDtypeStruct((s, bsz, w), F32),
        scratch_shapes=[pltpu.VMEM((tc, bsz, 2 * n), F32),
                        pltpu.VMEM((2, bsz, n), F32)],
        compiler_params=_cparams(("arbitrary",)),
        name="s5",
    )(u3, are, aim, bcat, ccat, d_skip.reshape(1, w), wglu)
    return y.reshape(s, bsz * w)


def _layer_norm(z, g, b):
    mu = jnp.mean(z, axis=1, keepdims=True)
    zc = z - mu
    var = jnp.mean(zc * zc, axis=1, keepdims=True)
    return zc * lax.rsqrt(var + LN_EPS) * g + b


def _out_proj_kernel(yda_ref, yml_ref, ys5_ref, x_ref, gate_ref, lng_ref, lnb_ref, shift_ref, scale_ref,
                     wout32_ref, wrt_ref, brt_ref, x1_ref, h2_ref, eid_ref, prob_ref, cnt_ref, wout_ref):
    first_step = jnp.logical_and(pl.program_id(0) == 0, pl.program_id(1) == 0)

    @pl.when(first_step)
    def _():
        wout_ref[...] = wout32_ref[0].astype(BF16)

    y = jnp.dot(yda_ref[0], wout_ref[0:DA_WIDTH, :], preferred_element_type=F32)
    y = y + jnp.dot(yml_ref[0], wout_ref[DA_WIDTH:DA_WIDTH + ML_WIDTH, :], preferred_element_type=F32)
    y = y + jnp.dot(ys5_ref[...].astype(BF16), wout_ref[DA_WIDTH + ML_WIDTH:, :], preferred_element_type=F32)
    x1 = _layer_norm(DN_ALPHA * x_ref[0] + (1.0 + gate_ref[0]) * y, lng_ref[...], lnb_ref[...])
    x1_ref[0] = x1
    h2 = x1 * (1.0 + scale_ref[0]) + shift_ref[0]
    h_hi = h2.astype(BF16)
    h_hi32 = h_hi.astype(F32)
    h2_ref[...] = _pack_rounded(h_hi32)
    h_lo = (h2 - h_hi32).astype(BF16)
    nt_dot = lambda a, b: lax.dot_general(a, b, (((1,), (1,)), ((), ())), preferred_element_type=F32)
    by_hi = nt_dot(wrt_ref[...], h_hi)
    logits = (by_hi[:N_EXPERTS] + by_hi[N_EXPERTS:] + nt_dot(wrt_ref[0:N_EXPERTS, :], h_lo)
              + brt_ref[...])
    eidx = lax.broadcasted_iota(jnp.int32, logits.shape, 0)
    vals, ids = [], []
    for _ in range(TOP_K):
        mx = jnp.max(logits, axis=0, keepdims=True)
        sel = jnp.min(jnp.where(logits == mx, eidx, N_EXPERTS), axis=0, keepdims=True)
        vals.append(mx)
        ids.append(sel)
        logits = jnp.where(eidx == sel, -jnp.inf, logits)
    ex = [jnp.exp(v - vals[0]) for v in vals]
    tot = ex[0] + ex[1] + ex[2] + ex[3]
    zi = jnp.zeros_like(ids[0])
    eid_ref[...] = jnp.concatenate(ids + [zi] * (SUBLANES - TOP_K), axis=0)

    @pl.when(first_step)
    def _():
        cnt_ref[...] = jnp.zeros_like(cnt_ref)

    member = jnp.zeros(logits.shape, F32)
    for sel in ids:
        member = member + (eidx == sel).astype(F32)
    cnt_ref[...] = cnt_ref[...] + jnp.sum(member, axis=1, keepdims=True)
    zf = jnp.zeros((LANES - TOP_K, tot.shape[1]), F32)
    prob_ref[...] = jnp.concatenate([e / tot for e in ex] + [zf], axis=0).T


def _out_proj(y_da, y_ml, y_s5, x, gate, ln_g, ln_b, shift2, scale2, w_out, layer, w_router_l, b_router_l):
    bsz, s, d = x.shape
    tm = TM_PROJ
    nt = s // tm
    tok = lambda b, i: (b, i, 0)
    per_b = lambda b, i: (b, 0, 0)
    c2 = lambda b, i: (0, 0)
    r3 = lambda a: a.reshape(bsz, 1, d)
    flat = lambda b, i: (0, b * nt + i)
    wr_t = w_router_l.T
    wr_hi = wr_t.astype(BF16)
    wr_lo = (wr_t - wr_hi.astype(F32)).astype(BF16)
    return pl.pallas_call(
        _out_proj_kernel,
        grid=(bsz, nt),
        in_specs=[pl.BlockSpec((1, tm, DA_WIDTH), tok),
                  pl.BlockSpec((1, tm, ML_WIDTH), tok),
                  pl.BlockSpec((tm, S5_WIDTH), lambda b, i: (i, b)),
                  pl.BlockSpec((1, tm, d), tok),
                  pl.BlockSpec((1, 1, d), per_b),
                  pl.BlockSpec((1, d), c2),
                  pl.BlockSpec((1, d), c2),
                  pl.BlockSpec((1, 1, d), per_b),
                  pl.BlockSpec((1, 1, d), per_b),
                  pl.BlockSpec((1, d, d), lambda b, i: (layer, 0, 0)),
                  pl.BlockSpec((2 * N_EXPERTS, d), c2),
                  pl.BlockSpec((N_EXPERTS, 1), c2)],
        out_specs=[pl.BlockSpec((1, tm, d), tok),
                   pl.BlockSpec((tm, d // 2), lambda b, i: (b * nt + i, 0)),
                   pl.BlockSpec((SUBLANES, tm), flat),
                   pl.BlockSpec((tm, LANES), lambda b, i: (b * nt + i, 0)),
                   pl.BlockSpec((N_EXPERTS, LANES), c2)],
        out_shape=[jax.ShapeDtypeStruct((bsz, s, d), F32),
                   jax.ShapeDtypeStruct((bsz * s, d // 2), jnp.int32),
                   jax.ShapeDtypeStruct((SUBLANES, bsz * s), jnp.int32),
                   jax.ShapeDtypeStruct((bsz * s, LANES), F32),
                   jax.ShapeDtypeStruct((N_EXPERTS, LANES), F32)],
        scratch_shapes=[pltpu.VMEM((d, d), BF16)],
        compiler_params=_cparams(("arbitrary", "arbitrary")),
        name="out_proj",
    )(y_da, y_ml, y_s5, x, r3(gate), ln_g.reshape(1, d), ln_b.reshape(1, d), r3(shift2), r3(scale2),
      w_out, jnp.concatenate([wr_hi, wr_lo], axis=0), b_router_l.reshape(N_EXPERTS, 1))


META_END, META_PAD, META_CNT = 0, 1, 2


def _route_kernel(eid_ref, cnt_ref, pos_ref, meta_ref, carry_s, start_s):
    i = pl.program_id(0)
    tb = eid_ref.shape[1]
    ntp = meta_ref.shape[1]
    tm = TM_MOE

    @pl.when(i == 0)
    def _():
        cnt = cnt_ref[...]
        padded = jnp.floor((cnt + (tm - 1)) * (1.0 / tm)) * tm
        er = lax.broadcasted_iota(jnp.int32, (N_EXPERTS, N_EXPERTS), 0)
        ec = lax.broadcasted_iota(jnp.int32, (N_EXPERTS, N_EXPERTS), 1)
        ends = jnp.dot((ec <= er).astype(F32), padded, preferred_element_type=F32, precision=HIGHEST)
        start_s[...] = ends - padded
        carry_s[...] = jnp.zeros_like(carry_s)
        lane = lax.broadcasted_iota(jnp.int32, (N_EXPERTS, ntp), 1)
        sub = lax.broadcasted_iota(jnp.int32, (N_EXPERTS, ntp), 0)
        diag = lane == sub

        def as_row(col):
            return jnp.sum(jnp.where(diag, col, 0.0), axis=0, keepdims=True)

        zero = jnp.zeros((SUBLANES - 3, ntp), F32)
        meta_ref[...] = jnp.concatenate([as_row(ends[:, 0:1]), as_row(padded[:, 0:1]), as_row(cnt[:, 0:1]), zero],
                                        axis=0).astype(jnp.int32)

    eid = eid_ref[...]
    eidx = lax.broadcasted_iota(jnp.int32, (N_EXPERTS, tb), 0)
    hot = [eidx == eid[k:k + 1, :] for k in range(TOP_K)]
    member = jnp.zeros((N_EXPERTS, tb), F32)
    for k in range(TOP_K):
        member = member + hot[k].astype(F32)
    ri = lax.broadcasted_iota(jnp.int32, (tb, tb), 0)
    ci = lax.broadcasted_iota(jnp.int32, (tb, tb), 1)
    triu = (ri <= ci).astype(BF16)
    incl = jnp.dot(member.astype(BF16), triu, preferred_element_type=F32)
    slot = incl - member + carry_s[:, 0:1] + start_s[:, 0:1]
    rows = [jnp.sum(jnp.where(hot[k], slot, 0.0), axis=0, keepdims=True) for k in range(TOP_K)]
    zr = jnp.zeros_like(rows[0])
    pos_ref[...] = jnp.concatenate(rows + [zr] * (SUBLANES - TOP_K), axis=0).astype(jnp.int32)
    carry_s[...] = carry_s[...] + jnp.sum(member, axis=1, keepdims=True)


def _route(eid, counts):
    t = eid.shape[1]
    tb = TB_RANK
    ntp = LANES
    pos8, meta = pl.pallas_call(
        _route_kernel,
        grid=(t // tb,),
        in_specs=[pl.BlockSpec((SUBLANES, tb), lambda i: (0, i)),
                  pl.BlockSpec((N_EXPERTS, LANES), lambda i: (0, 0))],
        out_specs=[pl.BlockSpec((SUBLANES, tb), lambda i: (0, i)),
                   pl.BlockSpec((SUBLANES, ntp), lambda i: (0, 0))],
        out_shape=[jax.ShapeDtypeStruct((SUBLANES, t), jnp.int32),
                   jax.ShapeDtypeStruct((SUBLANES, ntp), jnp.int32)],
        scratch_shapes=[pltpu.VMEM((N_EXPERTS, LANES), F32), pltpu.VMEM((N_EXPERTS, LANES), F32)],
        compiler_params=_cparams(("arbitrary",)),
        name="route",
    )(eid, counts)
    return pos8, meta


def _sc_workers():
    info = plsc.get_sparse_core_info()
    return info.num_cores, info.num_cores * info.num_subcores


def _dispatch(h2, pos8, n_rows):
    t, d = h2.shape
    n_cores, n_workers = _sc_workers()
    tpw = t // n_workers
    ch = SC_SCATTER_CHUNK
    mesh = plsc.VectorSubcoreMesh(core_axis_name="c", subcore_axis_name="s")

    @functools.partial(
        pl.kernel, mesh=mesh,
        out_type=jax.ShapeDtypeStruct((n_rows, d), h2.dtype),
        scratch_types=[pltpu.VMEM((ch,), jnp.int32)] * TOP_K + [pltpu.VMEM((ch, d), h2.dtype),
                                                                pltpu.SemaphoreType.DMA])
    def scatter_rows(h_hbm, pos_hbm, out_hbm, i0, i1, i2, i3, rows_v, sem):
        idx = (i0, i1, i2, i3)
        base = (lax.axis_index("s") * n_cores + lax.axis_index("c")) * tpw

        @pl.loop(0, tpw // ch)
        def _(i):
            off = base + i * ch
            pltpu.sync_copy(h_hbm.at[pl.ds(off, ch)], rows_v)
            for k in range(TOP_K):
                pltpu.sync_copy(pos_hbm.at[k, pl.ds(off, ch)], idx[k])
            copies = [pltpu.async_copy(rows_v, out_hbm.at[idx[k]], sem) for k in range(TOP_K)]
            for cp in copies:
                cp.wait()

    return scatter_rows(h2, pos8)


def _gather_expert_rows(ys, pos8):
    _, d = ys.shape
    t = pos8.shape[1]
    n_cores, n_workers = _sc_workers()
    tpw = t // n_workers
    ch = SC_GATHER_CHUNK
    mesh = plsc.VectorSubcoreMesh(core_axis_name="c", subcore_axis_name="s")

    @functools.partial(
        pl.kernel, mesh=mesh,
        out_type=jax.ShapeDtypeStruct((TOP_K, t, d), ys.dtype),
        scratch_types=([pltpu.VMEM((ch,), jnp.int32)] * 2 + [pltpu.VMEM((ch, d), ys.dtype)] * 2
                       + [pltpu.SemaphoreType.DMA] * 4))
    def gather_rows(ys_hbm, pos_hbm, out_hbm, idx0, idx1, rows0, rows1, g0, g1, w0, w1):
        idx, rows, gsem, wsem = (idx0, idx1), (rows0, rows1), (g0, g1), (w0, w1)
        base = (lax.axis_index("s") * n_cores + lax.axis_index("c")) * tpw
        items = [(i, k) for i in range(tpw // ch) for k in range(TOP_K)]

        def gather(n):
            i, k = items[n]
            pltpu.sync_copy(pos_hbm.at[k, pl.ds(base + i * ch, ch)], idx[n % 2])
            return pltpu.async_copy(ys_hbm.at[idx[n % 2]], rows[n % 2], gsem[n % 2])

        def write(n):
            i, k = items[n]
            return pltpu.async_copy(rows[n % 2], out_hbm.at[k, pl.ds(base + i * ch, ch)], wsem[n % 2])

        gathers, writes = {}, {}
        for n in range(len(items)):
            if n >= 2:
                writes[n - 2].wait()
            gathers[n] = gather(n)
            if n >= 1:
                gathers[n - 1].wait()
                writes[n - 1] = write(n - 1)
        last = len(items) - 1
        gathers[last].wait()
        writes[last] = write(last)
        if last >= 1:
            writes[last - 1].wait()
        writes[last].wait()

    return gather_rows(ys, pos8)


def _pack_bf16_pairs(a):
    return _pack_rounded(a.astype(BF16).astype(F32))


def _pack_rounded(r):
    n = r.shape[1] // 2
    lo = pltpu.bitcast(r[:, :n], jnp.int32)
    hi = pltpu.bitcast(r[:, n:], jnp.int32)
    return jnp.bitwise_or(jnp.bitwise_and(hi, HIGH_HALF),
                          jnp.bitwise_and(lax.shift_right_logical(lo, HALF_BITS), LOW_HALF))


def _unpack_bf16_pairs(p):
    lo = pltpu.bitcast(lax.shift_left(p, HALF_BITS), F32)
    hi = pltpu.bitcast(jnp.bitwise_and(p, HIGH_HALF), F32)
    return lo, hi


def _expert_kernel(meta_ref, xs_ref, wup_ref, bup_ref, wdn_ref, bdn_ref, ys_ref,
                   wup_s, wdn_s, xbuf, obuf, in_sem, out_sem):
    e = pl.program_id(0)
    tm = TM_MOE
    half = D_MODEL // 2
    pad = meta_ref[META_PAD, e]
    n_t = pad // tm
    row0 = meta_ref[META_END, e] - pad
    cnt = meta_ref[META_CNT, e]

    def rows(i):
        return pl.ds(pl.multiple_of(row0 + i * tm, tm), tm)

    def x_copy(i, slot):
        return pltpu.make_async_copy(xs_ref.at[rows(i)], xbuf.at[slot], in_sem.at[slot])

    def y_copy(i, slot):
        return pltpu.make_async_copy(obuf.at[slot], ys_ref.at[rows(i)], out_sem.at[slot])

    @pl.when(n_t > 0)
    def _():
        x_copy(0, 0).start()
        wup_s[...] = wup_ref[0, 0].astype(BF16)
        wdn_s[...] = wdn_ref[0, 0].astype(BF16)

        def tile(i, carry):
            slot = lax.rem(i, 2)
            x_copy(i, slot).wait()

            @pl.when(i + 1 < n_t)
            def _():
                x_copy(i + 1, 1 - slot).start()

            @pl.when(i >= 2)
            def _():
                y_copy(i - 2, slot).wait()

            row = lax.broadcasted_iota(jnp.int32, (tm, 1), 0)
            lo, hi = _unpack_bf16_pairs(jnp.where(row < cnt - i * tm, xbuf[slot], 0))
            z = (jnp.dot(lo.astype(BF16), wup_s[0:half, :], preferred_element_type=F32)
                 + jnp.dot(hi.astype(BF16), wup_s[half:, :], preferred_element_type=F32) + bup_ref[0, 0])
            glu = jnp.minimum(z[:, :D_EXPERT], SWIGLU_LIMIT)
            lin = jnp.clip(z[:, D_EXPERT:], -SWIGLU_LIMIT, SWIGLU_LIMIT)
            act = (glu * _sigmoid(SWIGLU_ALPHA * glu) * (lin + 1.0)).astype(BF16)
            y = jnp.dot(act, wdn_s[...], preferred_element_type=F32) + bdn_ref[0, 0]
            obuf[slot] = _pack_bf16_pairs(y)
            y_copy(i, slot).start()
            return carry

        lax.fori_loop(0, n_t, tile, 0)

        @pl.when(n_t >= 2)
        def _():
            y_copy(n_t - 2, lax.rem(n_t, 2)).wait()

        y_copy(n_t - 1, lax.rem(n_t - 1, 2)).wait()

    @pl.when(e == N_EXPERTS - 1)
    def _():
        obuf[0] = jnp.zeros((tm, half), jnp.int32)

        def fill(i, carry):
            cp = pltpu.make_async_copy(obuf.at[0], ys_ref.at[pl.ds(pl.multiple_of(i * tm, tm), tm)], out_sem.at[0])
            cp.start()
            cp.wait()
            return carry

        lax.fori_loop(meta_ref[META_END, N_EXPERTS - 1] // tm, ys_ref.shape[0] // tm, fill, 0)


def _expert_mlp(xs, meta, layer, w_up, b_up, w_down, b_down):
    n_rows, half = xs.shape
    d = 2 * half
    tm = TM_MOE
    f = w_up.shape[-1]
    b_up4 = b_up.reshape(DEPTH, N_EXPERTS, 1, f)
    b_dn4 = b_down.reshape(DEPTH, N_EXPERTS, 1, d)
    wsel = lambda e, m: (layer, e, 0, 0)
    grid_spec = pltpu.PrefetchScalarGridSpec(
        num_scalar_prefetch=1,
        grid=(N_EXPERTS,),
        in_specs=[pl.BlockSpec(memory_space=pl.ANY),
                  pl.BlockSpec((1, 1, d, f), wsel),
                  pl.BlockSpec((1, 1, 1, f), wsel),
                  pl.BlockSpec((1, 1, f // 2, d), wsel),
                  pl.BlockSpec((1, 1, 1, d), wsel)],
        out_specs=pl.BlockSpec(memory_space=pl.ANY),
        scratch_shapes=[pltpu.VMEM((d, f), BF16), pltpu.VMEM((f // 2, d), BF16),
                        pltpu.VMEM((2, tm, half), jnp.int32), pltpu.VMEM((2, tm, half), jnp.int32),
                        pltpu.SemaphoreType.DMA((2,)), pltpu.SemaphoreType.DMA((2,))],
    )
    return pl.pallas_call(
        _expert_kernel,
        grid_spec=grid_spec,
        out_shape=jax.ShapeDtypeStruct((n_rows, half), jnp.int32),
        compiler_params=_cparams(("arbitrary",)),
        name="expert_mlp",
    )(meta, xs, w_up, b_up4, w_down, b_dn4)


def _combine_kernel(rows_ref, prob_ref, x_ref, gate_ref, lng_ref, lnb_ref, o_ref):
    p = prob_ref[...]
    y = None
    for k in range(TOP_K):
        yk = p[:, k:k + 1] * jnp.concatenate(_unpack_bf16_pairs(rows_ref[k]), axis=1)
        y = yk if y is None else y + yk
    o_ref[0] = _layer_norm(DN_ALPHA * x_ref[0] + (1.0 + gate_ref[0]) * y, lng_ref[...], lnb_ref[...])


def _combine(rows, prob_c, x1, gate, ln_g, ln_b):
    bsz, s, d = x1.shape
    tm = TM_DISP
    nt = s // tm
    return pl.pallas_call(
        _combine_kernel,
        grid=(bsz, nt),
        in_specs=[pl.BlockSpec((TOP_K, tm, d // 2), lambda b, i: (0, b * nt + i, 0)),
                  pl.BlockSpec((tm, LANES), lambda b, i: (b * nt + i, 0)),
                  pl.BlockSpec((1, tm, d), lambda b, i: (b, i, 0)),
                  pl.BlockSpec((1, 1, d), lambda b, i: (b, 0, 0)),
                  pl.BlockSpec((1, d), lambda b, i: (0, 0)),
                  pl.BlockSpec((1, d), lambda b, i: (0, 0))],
        out_specs=pl.BlockSpec((1, tm, d), lambda b, i: (b, i, 0)),
        out_shape=jax.ShapeDtypeStruct((bsz, s, d), F32),
        compiler_params=_cparams(("arbitrary", "arbitrary")),
        name="combine",
    )(rows, prob_c, x1, gate.reshape(bsz, 1, d), ln_g.reshape(1, d), ln_b.reshape(1, d))


def kernel(x, c, positions, ada_w, ada_b, w_in, lam_q1, lam_k1, lam_q2, lam_k2, da_norm_g, ml_conv_w, ml_conv_b,
           ml_w_q, ml_w_k, ml_gate_b, ml_norm_g, s5_a_re, s5_a_im, s5_log_dt, s5_b_re, s5_b_im, s5_c_re, s5_c_im,
           s5_d, s5_w_glu, w_out, ln_g, ln_b, w_router, b_router, w_up, b_up, w_down, b_down):
    bsz, s, d = x.shape
    t = bsz * s
    n_tiles_max = (t * TOP_K) // TM_MOE + N_EXPERTS
    n_rows = n_tiles_max * TM_MOE
    mod = _modulation(c, ada_w, ada_b)
    cos_t, sin_t = _rope_tables(positions)
    for l in range(DEPTH):
        shift, scale, gate = jnp.split(mod[2 * l], 3, axis=-1)
        q, k, v, mlx, mlv, mlo, g_t, g_c, s5u = _in_proj(x, shift, scale, cos_t, sin_t, w_in, l)
        lam_init = 0.8 - 0.6 * math.exp(-0.3 * l)
        lamv = jnp.stack([lam_q1[l], lam_k1[l], lam_q2[l], lam_k2[l]])
        y_da = _diff_attn(q, k, v, lamv, da_norm_g[l], lam_init)
        y_ml = _mlstm(mlx, mlv, mlo, g_t, g_c, ml_conv_w[l], ml_conv_b[l], ml_w_q[l], ml_w_k[l],
                      ml_gate_b[l], ml_norm_g[l])
        s5p = _s5_params(s5_a_re[l], s5_a_im[l], s5_log_dt[l], s5_b_re[l], s5_b_im[l], s5_c_re[l], s5_c_im[l],
                         s5_w_glu[l])
        y_s5 = _s5(s5u, bsz, s5p, s5_d[l])
        shift2, scale2, gate2 = jnp.split(mod[2 * l + 1], 3, axis=-1)
        x1, h2, eid, prob, counts = _out_proj(y_da, y_ml, y_s5, x, gate, ln_g[l, 0], ln_b[l, 0], shift2, scale2,
                                      w_out, l, w_router[l], b_router[l])
        pos8, meta = _route(eid, counts)
        xs = _dispatch(h2, pos8, n_rows)
        ys = _expert_mlp(xs, meta, l, w_up, b_up, w_down, b_down)
        rows = _gather_expert_rows(ys, pos8)
        x = _combine(rows, prob, x1, gate2, ln_g[l, 1], ln_b[l, 1])
    return x
```

```python
import functools
import math

import jax
import jax.numpy as jnp
from jax import lax
from jax.experimental import pallas as pl
from jax.experimental.pallas import tpu as pltpu
from jax.experimental.pallas import tpu_sc as plsc

F32 = jnp.float32
BF16 = jnp.bfloat16
HIGHEST = lax.Precision.HIGHEST

D_MODEL = 1024
DEPTH = 2
DA_HEADS = 4
DA_HEAD_DIM = 64
DA_V_DIM = 2 * DA_HEAD_DIM
DA_WIDTH = DA_HEADS * DA_V_DIM
DA_QK_WIDTH = DA_HEADS * 2 * DA_HEAD_DIM
ROPE_THETA = 10000.0
ML_HEADS = 4
ML_HEAD_DIM = 64
ML_WIDTH = ML_HEADS * ML_HEAD_DIM
ML_CONV = 4
S5_GROUP = 16
S5_STATE = 64
S5_WIDTH = D_MODEL - DA_WIDTH - ML_WIDTH
S5_GROUPS = S5_WIDTH // S5_GROUP
S5_NSTATE = S5_GROUPS * S5_STATE
N_EXPERTS = 32
TOP_K = 4
D_EXPERT = D_MODEL
SWIGLU_LIMIT = 7.0
SWIGLU_ALPHA = 1.702
DN_ALPHA = (2 * DEPTH) ** 0.25
LN_EPS = 1e-5
NEG = -1e30

OFF_DA_K = DA_QK_WIDTH
OFF_DA_V = 2 * DA_QK_WIDTH
OFF_ML_X = OFF_DA_V + DA_WIDTH
OFF_ML_V = OFF_ML_X + ML_WIDTH
OFF_ML_O = OFF_ML_V + ML_WIDTH
OFF_ML_I = OFF_ML_O + ML_WIDTH
OFF_ML_F = OFF_ML_I + ML_HEADS
OFF_S5_U = OFF_ML_F + ML_HEADS
N_IN = OFF_S5_U + S5_WIDTH

LANES = 128
SUBLANES = 8
VMEM_LIMIT_BYTES = 56 * 1024 * 1024

TM_PROJ = 1024
TQ = 512
ML_CHUNK = 256
ML_NB = 4
S5_TC = 256
S5_UNROLL = 8
TB_RANK = 1024
TM_MOE = 256
TM_DISP = 512
SC_SCATTER_CHUNK = 128
SC_GATHER_CHUNK = 64
GATE_PAD = 8
VT_ROWS = DA_V_DIM + 16
Q_PRESCALE = DA_HEAD_DIM ** -0.5 * math.log2(math.e)


def _cparams(sem, vmem=VMEM_LIMIT_BYTES):
    return pltpu.CompilerParams(dimension_semantics=sem, vmem_limit_bytes=vmem)


def _sigmoid(x):
    return 1.0 / (1.0 + jnp.exp(-x))


def _mod_kernel(c_ref, w_ref, b_ref, o_ref):
    c = c_ref[...]
    ca = (c * _sigmoid(c)).astype(BF16)
    w = w_ref[0, 0].astype(BF16)
    o_ref[0] = jnp.dot(ca, w, preferred_element_type=F32) + b_ref[0]


def _modulation(c, ada_w, ada_b):
    nsub = ada_w.shape[1]
    nmod = ada_w.shape[0] * nsub
    bsz, d = c.shape
    e = ada_w.shape[-1]
    tn = 1024
    b = ada_b.reshape(nmod, 1, e)
    return pl.pallas_call(
        _mod_kernel,
        grid=(nmod, e // tn),
        in_specs=[pl.BlockSpec((bsz, d), lambda n, j: (0, 0)),
                  pl.BlockSpec((1, 1, d, tn), lambda n, j: (n // nsub, n % nsub, 0, j)),
                  pl.BlockSpec((1, 1, tn), lambda n, j: (n, 0, j))],
        out_specs=pl.BlockSpec((1, bsz, tn), lambda n, j: (n, 0, j)),
        out_shape=jax.ShapeDtypeStruct((nmod, bsz, e), F32),
        compiler_params=_cparams(("arbitrary", "arbitrary")),
        name="modulation",
    )(c, ada_w, b)


def _rope_kernel(pos_ref, cos_ref, sin_ref):
    nfreq = DA_HEAD_DIM // 2
    pos = pos_ref[0].astype(F32)
    fidx = lax.broadcasted_iota(jnp.int32, (nfreq, 1), 0).astype(F32)
    inv = jnp.exp(fidx * (-2.0 * math.log(ROPE_THETA) / DA_HEAD_DIM))
    ang = inv * pos
    reps = LANES // nfreq
    cos_t = jnp.concatenate([jnp.cos(ang)] * reps, axis=0).T
    sin_t = jnp.concatenate([jnp.sin(ang)] * reps, axis=0).T
    lane = lax.broadcasted_iota(jnp.int32, (1, LANES), 1)
    sign = jnp.where((lane % DA_HEAD_DIM) < nfreq, -1.0, 1.0)
    cos_ref[0] = cos_t
    sin_ref[0] = sin_t * sign


def _rope_tables(positions):
    bsz, s = positions.shape
    ts = 512
    pos3 = positions.reshape(bsz, 1, s)
    return pl.pallas_call(
        _rope_kernel,
        grid=(bsz, s // ts),
        in_specs=[pl.BlockSpec((1, 1, ts), lambda b, i: (b, 0, i))],
        out_specs=[pl.BlockSpec((1, ts, LANES), lambda b, i: (b, i, 0))] * 2,
        out_shape=[jax.ShapeDtypeStruct((bsz, s, LANES), F32)] * 2,
        compiler_params=_cparams(("arbitrary", "arbitrary")),
        name="rope_tables",
    )(pos3)


def _in_proj_kernel(x_ref, shift_ref, scale_ref, cos_ref, sin_ref, win_ref,
                    q_ref, k_ref, vt_ref, mlx_ref, mlv_ref, mlo_ref, gt_ref, gc_ref, s5u_ref,
                    wqk_ref, wvt_ref, wrest_ref, wgt_ref, wgc_ref):
    @pl.when(jnp.logical_and(pl.program_id(0) == 0, pl.program_id(1) == 0))
    def _():
        wqk_ref[...] = win_ref[0, 0:OFF_DA_V, :].T.astype(BF16)
        wvt_ref[...] = win_ref[0, OFF_DA_V:OFF_ML_X, :].astype(BF16)
        wrest_ref[:, 0:3 * ML_WIDTH] = win_ref[0, OFF_ML_X:OFF_ML_I, :].T.astype(BF16)
        wrest_ref[:, 3 * ML_WIDTH:] = win_ref[0, OFF_S5_U:N_IN, :].T.astype(BF16)
        wgt_ref[...] = win_ref[0, OFF_ML_I:OFF_S5_U, :].astype(BF16)
        gslab = win_ref[0, OFF_ML_I:OFF_ML_I + LANES, :].T
        glane = lax.broadcasted_iota(jnp.int32, (1, LANES), 1)
        wgc_ref[...] = jnp.where(glane < GATE_PAD, gslab, 0.0).astype(BF16)

    h = (x_ref[0] * (1.0 + scale_ref[0]) + shift_ref[0]).astype(BF16)
    cos = cos_ref[0]
    sin = sin_ref[0]
    lane = lax.broadcasted_iota(jnp.int32, (1, LANES), 1)
    lo_half = (lane % DA_HEAD_DIM) < DA_HEAD_DIM // 2
    half = DA_HEAD_DIM // 2

    def rope(t):
        fwd = pltpu.roll(t, half, 1)
        bwd = pltpu.roll(t, LANES - half, 1)
        partner = jnp.where(lo_half, bwd, fwd)
        return t * cos + partner * sin

    qk = jnp.dot(h, wqk_ref[...], preferred_element_type=F32)
    nslab = DA_QK_WIDTH // LANES
    for c in range(nslab):
        q_ref[0, :, c * LANES:(c + 1) * LANES] = (
            rope(qk[:, c * LANES:(c + 1) * LANES]) * Q_PRESCALE).astype(BF16)
        k_ref[0, :, c * LANES:(c + 1) * LANES] = rope(
            qk[:, DA_QK_WIDTH + c * LANES:DA_QK_WIDTH + (c + 1) * LANES]).astype(BF16)

    vt = lax.dot_general(wvt_ref[...], h, (((1,), (1,)), ((), ())), preferred_element_type=F32)
    tm = h.shape[0]
    for hh in range(DA_HEADS):
        for jj in range(tm // TQ):
            vt_ref[0, hh, jj, 0:DA_V_DIM, :] = vt[hh * DA_V_DIM:(hh + 1) * DA_V_DIM,
                                                  jj * TQ:(jj + 1) * TQ].astype(BF16)
            vt_ref[0, hh, jj, DA_V_DIM:VT_ROWS, :] = jnp.ones((VT_ROWS - DA_V_DIM, TQ), BF16)

    r = jnp.dot(h, wrest_ref[...], preferred_element_type=F32)
    o = 0
    mlx_ref[0] = r[:, o:o + ML_WIDTH].astype(BF16); o += ML_WIDTH
    mlv_ref[0] = r[:, o:o + ML_WIDTH].astype(BF16); o += ML_WIDTH
    mlo_ref[0] = r[:, o:o + ML_WIDTH].astype(BF16); o += ML_WIDTH
    s5u_ref[...] = r[:, o:o + S5_WIDTH]
    gt = lax.dot_general(wgt_ref[...], h, (((1,), (1,)), ((), ())), preferred_element_type=F32)
    for c in range(h.shape[0] // ML_CHUNK):
        gt_ref[0, c] = gt[:, c * ML_CHUNK:(c + 1) * ML_CHUNK]
    gc_ref[0] = jnp.dot(h, wgc_ref[...], preferred_element_type=F32)


def _in_proj(x, shift, scale, cos_t, sin_t, w_in, layer):
    bsz, s, d = x.shape
    tm = TM_PROJ
    nrest = 3 * ML_WIDTH + S5_WIDTH
    shift3 = shift.reshape(bsz, 1, d)
    scale3 = scale.reshape(bsz, 1, d)
    tok = lambda b, i: (b, i, 0)
    per_b = lambda b, i: (b, 0, 0)
    out_shapes = [
        jax.ShapeDtypeStruct((bsz, s, DA_QK_WIDTH), BF16),
        jax.ShapeDtypeStruct((bsz, s, DA_QK_WIDTH), BF16),
        jax.ShapeDtypeStruct((bsz, DA_HEADS, s // TQ, VT_ROWS, TQ), BF16),
        jax.ShapeDtypeStruct((bsz, s, ML_WIDTH), BF16),
        jax.ShapeDtypeStruct((bsz, s, ML_WIDTH), BF16),
        jax.ShapeDtypeStruct((bsz, s, ML_WIDTH), BF16),
        jax.ShapeDtypeStruct((bsz, s // ML_CHUNK, GATE_PAD, ML_CHUNK), F32),
        jax.ShapeDtypeStruct((bsz, s, LANES), F32),
        jax.ShapeDtypeStruct((s, bsz * S5_WIDTH), F32),
    ]
    out_specs = [
        pl.BlockSpec((1, tm, DA_QK_WIDTH), tok),
        pl.BlockSpec((1, tm, DA_QK_WIDTH), tok),
        pl.BlockSpec((1, DA_HEADS, tm // TQ, VT_ROWS, TQ), lambda b, i: (b, 0, i, 0, 0)),
        pl.BlockSpec((1, tm, ML_WIDTH), tok),
        pl.BlockSpec((1, tm, ML_WIDTH), tok),
        pl.BlockSpec((1, tm, ML_WIDTH), tok),
        pl.BlockSpec((1, tm // ML_CHUNK, GATE_PAD, ML_CHUNK), lambda b, i: (b, i, 0, 0)),
        pl.BlockSpec((1, tm, LANES), tok),
        pl.BlockSpec((tm, S5_WIDTH), lambda b, i: (i, b)),
    ]
    return pl.pallas_call(
        _in_proj_kernel,
        grid=(bsz, s // tm),
        in_specs=[pl.BlockSpec((1, tm, d), tok),
                  pl.BlockSpec((1, 1, d), per_b),
                  pl.BlockSpec((1, 1, d), per_b),
                  pl.BlockSpec((1, tm, LANES), tok),
                  pl.BlockSpec((1, tm, LANES), tok),
                  pl.BlockSpec((1, N_IN, d), lambda b, i: (layer, 0, 0))],
        out_specs=out_specs,
        out_shape=out_shapes,
        scratch_shapes=[pltpu.VMEM((d, OFF_DA_V), BF16), pltpu.VMEM((DA_WIDTH, d), BF16),
                        pltpu.VMEM((d, nrest), BF16), pltpu.VMEM((GATE_PAD, d), BF16),
                        pltpu.VMEM((d, LANES), BF16)],
        compiler_params=_cparams(("arbitrary", "arbitrary")),
        name="in_proj",
    )(x, shift3, scale3, cos_t, sin_t, jnp.swapaxes(w_in, 1, 2))


def _diff_attn_kernel(lam_init, lamv_ref, gain_ref, q_ref, k_ref, vt_ref, o_ref, acc_s, m_s):
    qi = pl.program_id(2)
    tq = q_ref.shape[1]
    lane = lax.broadcasted_iota(jnp.int32, (1, LANES), 1)
    first = lane < DA_HEAD_DIM
    q = q_ref[0]
    zero = jnp.zeros_like(q)
    qm = (jnp.where(first, q, zero), jnp.where(first, zero, q))
    acc_s[...] = jnp.zeros_like(acc_s)
    m_s[...] = jnp.full(m_s.shape, NEG, F32)

    def step(j, nblk, masked):
        tk = nblk * tq
        kb = k_ref[0, pl.ds(pl.multiple_of(j * tq, tq), tk), :]
        vtb = vt_ref[0, 0, j] if nblk == 1 else jnp.concatenate([vt_ref[0, 0, j + b] for b in range(nblk)], axis=1)
        for c in range(2):
            st = lax.dot_general(kb, qm[c], (((1,), (1,)), ((), ())), preferred_element_type=F32)
            if masked:
                key_i = lax.broadcasted_iota(jnp.int32, (tk, tq), 0) - (tk - tq)
                qry_i = lax.broadcasted_iota(jnp.int32, (tk, tq), 1)
                st = jnp.where(key_i <= qry_i, st, NEG)
            m_prev = m_s[c]
            m_new = jnp.maximum(m_prev, jnp.max(st, axis=0, keepdims=True))
            alpha = jnp.exp2(m_prev - m_new)
            p = jnp.exp2(st - m_new).astype(BF16)
            acc_s[c] = alpha * acc_s[c] + jnp.dot(vtb, p, preferred_element_type=F32)
            m_s[c] = m_new

    def body(jj, carry):
        step(2 * jj, 2, False)
        return carry

    lax.fori_loop(0, qi // 2, body, 0)

    @pl.when(qi % 2 == 1)
    def _():
        step(qi - 1, 2, True)

    @pl.when(qi % 2 == 0)
    def _():
        step(qi, 1, True)

    outs = []
    for c in range(2):
        acc = acc_s[c]
        outs.append(acc[:DA_V_DIM] / acc[DA_V_DIM:DA_V_DIM + 1])

    lamv = lamv_ref[...]
    lam = (jnp.exp(jnp.sum(lamv[0:1] * lamv[1:2], axis=1, keepdims=True))
           - jnp.exp(jnp.sum(lamv[2:3] * lamv[3:4], axis=1, keepdims=True)) + lam_init)
    ot = outs[0] - lam * outs[1]
    ms = jnp.mean(ot * ot, axis=0, keepdims=True)
    ot = ot * (lax.rsqrt(ms + LN_EPS) * (1.0 - lam_init))
    o_ref[0] = (ot.T * gain_ref[...]).astype(o_ref.dtype)


def _diff_attn(q, k, vt, lamv, gain, lam_init):
    bsz, s, _ = q.shape
    tq = TQ
    nq = s // tq
    return pl.pallas_call(
        functools.partial(_diff_attn_kernel, lam_init),
        grid=(bsz, DA_HEADS, nq),
        in_specs=[pl.BlockSpec((4, DA_HEAD_DIM), lambda b, h, i: (0, 0)),
                  pl.BlockSpec((1, DA_V_DIM), lambda b, h, i: (0, 0)),
                  pl.BlockSpec((1, tq, DA_V_DIM), lambda b, h, i: (b, i, h)),
                  pl.BlockSpec((1, s, DA_V_DIM), lambda b, h, i: (b, 0, h)),
                  pl.BlockSpec((1, 1, nq, VT_ROWS, tq), lambda b, h, i: (b, h, 0, 0, 0))],
        out_specs=pl.BlockSpec((1, tq, DA_V_DIM), lambda b, h, i: (b, i, h)),
        out_shape=jax.ShapeDtypeStruct((bsz, s, DA_WIDTH), BF16),
        scratch_shapes=[pltpu.VMEM((2, VT_ROWS, tq), F32), pltpu.VMEM((2, 1, tq), F32)],
        compiler_params=_cparams(("arbitrary", "arbitrary", "arbitrary")),
        name="diff_attn",
    )(lamv, gain.reshape(1, DA_V_DIM), q, k, vt)


def _log_sigmoid(x):
    return jnp.minimum(x, 0.0) - jnp.log(1.0 + jnp.exp(-jnp.abs(x)))


def _split3(a):
    hi = a.astype(BF16)
    r1 = a - hi.astype(F32)
    mid = r1.astype(BF16)
    lo = (r1 - mid.astype(F32)).astype(BF16)
    return hi, mid, lo


def _mlstm_kernel(x_ref, v_ref, o_ref, gt_ref, gc_ref, cw_ref, cb_ref, wq_ref, wkt_ref, gbt_ref, gbc_ref,
                  ng_ref, hmean_ref, y_ref, xc_s, c_s, m_s):
    nb, s = x_ref.shape[0], x_ref.shape[1]
    L = ML_CHUNK
    H, dh = ML_HEADS, ML_HEAD_DIM
    nc = s // L
    cw = cw_ref[...]
    row = lax.broadcasted_iota(jnp.int32, (s, 1), 0)
    for bi in range(nb):
        x = x_ref[bi].astype(F32)
        xc = x * cw[ML_CONV - 1:ML_CONV]
        for j in range(1, ML_CONV):
            xs = jnp.where(row >= j, pltpu.roll(x, j, 0), 0.0)
            xc = xc + xs * cw[ML_CONV - 1 - j:ML_CONV - j]
        xc = xc + cb_ref[...]
        xc_s[bi] = (xc * _sigmoid(xc)).astype(BF16)

    c_s[...] = jnp.zeros_like(c_s)
    m_s[...] = jnp.full(m_s.shape, NEG, F32)

    ri = lax.broadcasted_iota(jnp.int32, (L, L), 0)
    ci = lax.broadcasted_iota(jnp.int32, (L, L), 1)
    causal = ci <= ri
    tril = causal.astype(BF16)
    triu = (ri <= ci).astype(BF16)
    lane = lax.broadcasted_iota(jnp.int32, (1, dh), 1)
    one_hot0 = jnp.broadcast_to((lane == 0).astype(BF16), (L, dh))

    def chunk_one(bi, ci_, t0):
        xcc = xc_s[bi, pl.ds(t0, L), :]
        qc = jnp.dot(xcc, wq_ref[...], preferred_element_type=F32).astype(BF16)
        ktc = lax.dot_general(wkt_ref[...], xcc, (((1,), (1,)), ((), ())),
                              preferred_element_type=F32)
        g_rows = gt_ref[bi, ci_] + gbt_ref[...]
        g_cols = gc_ref[bi, pl.ds(t0, L), :] + gbc_ref[...]
        lf_rows = _log_sigmoid(g_rows)
        lf_cols = _log_sigmoid(g_cols)
        r3 = jnp.dot(jnp.concatenate(_split3(lf_rows), axis=0), triu, preferred_element_type=F32)
        b_rows = r3[0:GATE_PAD] + r3[GATE_PAD:2 * GATE_PAD] + r3[2 * GATE_PAD:]
        c3 = jnp.dot(tril, jnp.concatenate(_split3(lf_cols), axis=1), preferred_element_type=F32)
        b_cols = c3[:, 0:LANES] + c3[:, LANES:2 * LANES] + c3[:, 2 * LANES:]
        vch = v_ref[bi, pl.ds(t0, L), :]
        och = o_ref[bi, pl.ds(t0, L), :].astype(F32)
        hs = []
        for h in range(H):
            br = b_rows[H + h:H + h + 1, :]
            ir = g_rows[h:h + 1, :]
            bc = b_cols[:, H + h:H + h + 1]
            m_prev = m_s[bi, h]
            log_d = jnp.where(causal, bc - br + ir, NEG)
            inter = bc + m_prev
            mx = jnp.maximum(inter, jnp.max(log_d, axis=1, keepdims=True))
            dmat = jnp.exp(log_d - mx)
            dec = jnp.exp(inter - mx)
            qh = qc[:, h * dh:(h + 1) * dh]
            kth = ktc[h * dh:(h + 1) * dh, :]
            vaug = jnp.concatenate([vch[:, h * dh:(h + 1) * dh], one_hot0], axis=1)
            sm = (jnp.dot(qh, kth.astype(BF16), preferred_element_type=F32) * dmat).astype(BF16)
            c_prev = c_s[bi, h]
            na = (jnp.dot(sm, vaug, preferred_element_type=F32)
                  + dec * jnp.dot(qh, c_prev.astype(BF16), preferred_element_type=F32))
            den = na[:, dh:dh + 1]
            hs.append(na[:, :dh] / jnp.maximum(jnp.abs(den), jnp.exp(-mx)))
            g_tot = br[:, L - 1:L]
            a_row = g_tot - br + ir
            m_new = jnp.maximum(g_tot + m_prev, jnp.max(a_row, axis=1, keepdims=True))
            decay = jnp.exp(g_tot + m_prev - m_new)
            w_row = jnp.exp(a_row - m_new)
            kw = (kth * w_row).astype(BF16)
            c_s[bi, h] = decay * c_prev + jnp.dot(kw, vaug, preferred_element_type=F32)
            m_s[bi, h] = m_new
        hcat = jnp.concatenate(hs, axis=1)
        m3 = jnp.dot(jnp.concatenate(_split3(hcat * hcat), axis=0), hmean_ref[...],
                     preferred_element_type=F32)
        ms = m3[0:L] + m3[L:2 * L] + m3[2 * L:]
        y = hcat * lax.rsqrt(ms + LN_EPS) * ng_ref[...] * _sigmoid(och)
        y_ref[bi, pl.ds(t0, L), :] = y.astype(y_ref.dtype)

    def chunk(ci_, _):
        t0 = pl.multiple_of(ci_ * L, L)
        for bi in range(nb):
            chunk_one(bi, ci_, t0)
        return 0

    lax.fori_loop(0, nc, chunk, 0)


def _mlstm(mlx, mlv, mlo, g_t, g_c, conv_w, conv_b, w_q, w_k, gate_b, norm_g):
    bsz, s, _ = mlx.shape
    H, dh = ML_HEADS, ML_HEAD_DIM
    eye = jnp.eye(H, dtype=F32)
    wq_bd = jnp.einsum('hde,hg->hdge', w_q, eye).reshape(ML_WIDTH, ML_WIDTH).astype(BF16)
    wk_bd = jnp.einsum('hde,hg->hdge', w_k * (dh ** -0.5), eye).reshape(ML_WIDTH, ML_WIDTH)
    wkt_bd = wk_bd.T.astype(BF16)
    gbt = gate_b.reshape(GATE_PAD, 1)
    gbc = jnp.pad(gate_b.reshape(1, GATE_PAD), ((0, 0), (0, LANES - GATE_PAD)))
    hmean = jnp.kron(eye, jnp.full((dh, dh), 1.0 / dh, F32)).astype(BF16)
    nc = s // ML_CHUNK
    tok = lambda b: (b, 0, 0)
    c2 = lambda b: (0, 0)
    nb = ML_NB
    return pl.pallas_call(
        _mlstm_kernel,
        grid=(bsz // nb,),
        in_specs=[pl.BlockSpec((nb, s, ML_WIDTH), tok),
                  pl.BlockSpec((nb, s, ML_WIDTH), tok),
                  pl.BlockSpec((nb, s, ML_WIDTH), tok),
                  pl.BlockSpec((nb, nc, GATE_PAD, ML_CHUNK), lambda b: (b, 0, 0, 0)),
                  pl.BlockSpec((nb, s, LANES), tok),
                  pl.BlockSpec((ML_CONV, ML_WIDTH), c2),
                  pl.BlockSpec((1, ML_WIDTH), c2),
                  pl.BlockSpec((ML_WIDTH, ML_WIDTH), c2),
                  pl.BlockSpec((ML_WIDTH, ML_WIDTH), c2),
                  pl.BlockSpec((GATE_PAD, 1), c2),
                  pl.BlockSpec((1, LANES), c2),
                  pl.BlockSpec((1, ML_WIDTH), c2),
                  pl.BlockSpec((ML_WIDTH, ML_WIDTH), c2)],
        out_specs=pl.BlockSpec((nb, s, ML_WIDTH), tok),
        out_shape=jax.ShapeDtypeStruct((bsz, s, ML_WIDTH), BF16),
        scratch_shapes=[pltpu.VMEM((nb, s, ML_WIDTH), BF16),
                        pltpu.VMEM((nb, H, dh, LANES), F32),
                        pltpu.VMEM((nb, H, 1, 1), F32)],
        compiler_params=_cparams(("arbitrary",)),
        name="mlstm",
    )(mlx, mlv, mlo, g_t, g_c, conv_w, conv_b.reshape(1, ML_WIDTH), wq_bd, wkt_bd, gbt, gbc,
      norm_g.reshape(1, ML_WIDTH), hmean)


def _gelu_tanh(x):
    return 0.5 * x * (1.0 + jnp.tanh(math.sqrt(2.0 / math.pi) * (x + 0.044715 * (x * x * x))))


def _s5_kernel(u_ref, are_ref, aim_ref, bcat_ref, ccat_ref, d_ref, wglu_ref, y_ref, xs_s, st_s):
    tc, bsz, w = u_ref.shape
    n = S5_NSTATE

    @pl.when(pl.program_id(0) == 0)
    def _():
        st_s[...] = jnp.zeros_like(st_s)

    u = u_ref[...].reshape(tc * bsz, w)
    xs_s[...] = jnp.dot(u.astype(BF16), bcat_ref[...], preferred_element_type=F32).reshape(tc, bsz, 2 * n)
    a_re = jnp.broadcast_to(are_ref[...], (bsz, n))
    a_im = jnp.broadcast_to(aim_ref[...], (bsz, n))

    def step(t, carry):
        x_re, x_im = carry
        bu = xs_s[t]
        n_re = a_re * x_re - a_im * x_im + bu[:, :n]
        n_im = a_re * x_im + a_im * x_re + bu[:, n:]
        xs_s[t] = jnp.concatenate([n_re, n_im], axis=1)
        return n_re, n_im

    x_re, x_im = lax.fori_loop(0, tc, step, (st_s[0], st_s[1]), unroll=S5_UNROLL)
    st_s[0] = x_re
    st_s[1] = x_im

    xs = xs_s[...].reshape(tc * bsz, 2 * n).astype(BF16)
    y = jnp.dot(xs, ccat_ref[...], preferred_element_type=F32) + d_ref[...] * u
    z = jnp.dot(_gelu_tanh(y).astype(BF16), wglu_ref[...], preferred_element_type=F32)
    out = z[:, :w] * _sigmoid(z[:, w:])
    y_ref[...] = out.reshape(tc, bsz, w).astype(y_ref.dtype)


def _s5_params(a_re, a_im, log_dt, b_re, b_im, c_re, c_im, w_glu):
    G, P, Hc = S5_GROUPS, S5_STATE, S5_GROUP
    dt = jnp.exp(log_dt)[:, None]
    mag = jnp.exp(a_re * dt)
    ab_re = mag * jnp.cos(a_im * dt)
    ab_im = mag * jnp.sin(a_im * dt)
    nr, ni = ab_re - 1.0, ab_im
    den = a_re * a_re + a_im * a_im
    fr = (nr * a_re + ni * a_im) / den
    fi = (ni * a_re - nr * a_im) / den
    bb_re = fr[..., None] * b_re - fi[..., None] * b_im
    bb_im = fr[..., None] * b_im + fi[..., None] * b_re
    eye = jnp.eye(G, dtype=F32)
    bd = lambda t, sub: jnp.einsum(sub, t, eye)
    bre = bd(bb_re, 'gph,gk->ghkp').reshape(G * Hc, G * P)
    bim = bd(bb_im, 'gph,gk->ghkp').reshape(G * Hc, G * P)
    bcat = jnp.concatenate([bre, bim], axis=1).astype(BF16)
    cre = bd(c_re, 'ghp,gk->gpkh').reshape(G * P, G * Hc)
    cim = bd(c_im, 'ghp,gk->gpkh').reshape(G * P, G * Hc)
    ccat = jnp.concatenate([cre, -cim], axis=0).astype(BF16)
    wv = bd(w_glu[:, :, :Hc], 'ghj,gk->ghkj').reshape(G * Hc, G * Hc)
    wg = bd(w_glu[:, :, Hc:], 'ghj,gk->ghkj').reshape(G * Hc, G * Hc)
    wglu = jnp.concatenate([wv, wg], axis=1).astype(BF16)
    return ab_re.reshape(1, G * P), ab_im.reshape(1, G * P), bcat, ccat, wglu


def _s5(u_tm, bsz, params, d_skip):
    s = u_tm.shape[0]
    w = S5_WIDTH
    n = S5_NSTATE
    are, aim, bcat, ccat, wglu = params
    u3 = u_tm.reshape(s, bsz, w)
    tc = S5_TC
    c2 = lambda i: (0, 0)
    y = pl.pallas_call(
        _s5_kernel,
        grid=(s // tc,),
        in_specs=[pl.BlockSpec((tc, bsz, w), lambda i: (i, 0, 0)),
                  pl.BlockSpec((1, n), c2),
                  pl.BlockSpec((1, n), c2),
                  pl.BlockSpec((w, 2 * n), c2),
                  pl.BlockSpec((2 * n, w), c2),
                  pl.BlockSpec((1, w), c2),
                  pl.BlockSpec((w, 2 * w), c2)],
        out_specs=pl.BlockSpec((tc, bsz, w), lambda i: (i, 0, 0)),
        out_shape=jax.ShapeDtypeStruct((s, bsz, w), F32),
        scratch_shapes=[pltpu.VMEM((tc, bsz, 2 * n), F32),
                        pltpu.VMEM((2, bsz, n), F32)],
        compiler_params=_cparams(("arbitrary",)),
        name="s5",
    )(u3, are, aim, bcat, ccat, d_skip.reshape(1, w), wglu)
    return y.reshape(s, bsz * w)


def _layer_norm(z, g, b):
    mu = jnp.mean(z, axis=1, keepdims=True)
    zc = z - mu
    var = jnp.mean(zc * zc, axis=1, keepdims=True)
    return zc * lax.rsqrt(var + LN_EPS) * g + b


def _out_proj_kernel(yda_ref, yml_ref, ys5_ref, x_ref, gate_ref, lng_ref, lnb_ref, shift_ref, scale_ref,
                     wout32_ref, wrt_ref, brt_ref, x1_ref, h2_ref, eid_ref, prob_ref, cnt_ref, wout_ref):
    first_step = jnp.logical_and(pl.program_id(0) == 0, pl.program_id(1) == 0)

    @pl.when(first_step)
    def _():
        wout_ref[...] = wout32_ref[0].astype(BF16)

    y = jnp.dot(yda_ref[0], wout_ref[0:DA_WIDTH, :], preferred_element_type=F32)
    y = y + jnp.dot(yml_ref[0], wout_ref[DA_WIDTH:DA_WIDTH + ML_WIDTH, :], preferred_element_type=F32)
    y = y + jnp.dot(ys5_ref[...].astype(BF16), wout_ref[DA_WIDTH + ML_WIDTH:, :], preferred_element_type=F32)
    x1 = _layer_norm(DN_ALPHA * x_ref[0] + (1.0 + gate_ref[0]) * y, lng_ref[...], lnb_ref[...])
    x1_ref[0] = x1
    h2 = x1 * (1.0 + scale_ref[0]) + shift_ref[0]
    h_hi = h2.astype(BF16)
    h_hi32 = h_hi.astype(F32)
    h2_ref[...] = _pack_rounded(h_hi32)
    h_lo = (h2 - h_hi32).astype(BF16)
    nt_dot = lambda a, b: lax.dot_general(a, b, (((1,), (1,)), ((), ())), preferred_element_type=F32)
    by_hi = nt_dot(wrt_ref[...], h_hi)
    logits = (by_hi[:N_EXPERTS] + by_hi[N_EXPERTS:] + nt_dot(wrt_ref[0:N_EXPERTS, :], h_lo)
              + brt_ref[...])
    eidx = lax.broadcasted_iota(jnp.int32, logits.shape, 0)
    vals, ids = [], []
    for _ in range(TOP_K):
        mx = jnp.max(logits, axis=0, keepdims=True)
        sel = jnp.min(jnp.where(logits == mx, eidx, N_EXPERTS), axis=0, keepdims=True)
        vals.append(mx)
        ids.append(sel)
        logits = jnp.where(eidx == sel, -jnp.inf, logits)
    ex = [jnp.exp(v - vals[0]) for v in vals]
    tot = ex[0] + ex[1] + ex[2] + ex[3]
    zi = jnp.zeros_like(ids[0])
    eid_ref[...] = jnp.concatenate(ids + [zi] * (SUBLANES - TOP_K), axis=0)

    @pl.when(first_step)
    def _():
        cnt_ref[...] = jnp.zeros_like(cnt_ref)

    member = jnp.zeros(logits.shape, F32)
    for sel in ids:
        member = member + (eidx == sel).astype(F32)
    cnt_ref[...] = cnt_ref[...] + jnp.sum(member, axis=1, keepdims=True)
    zf = jnp.zeros((LANES - TOP_K, tot.shape[1]), F32)
    prob_ref[...] = jnp.concatenate([e / tot for e in ex] + [zf], axis=0).T


def _out_proj(y_da, y_ml, y_s5, x, gate, ln_g, ln_b, shift2, scale2, w_out, layer, w_router_l, b_router_l):
    bsz, s, d = x.shape
    tm = TM_PROJ
    nt = s // tm
    tok = lambda b, i: (b, i, 0)
    per_b = lambda b, i: (b, 0, 0)
    c2 = lambda b, i: (0, 0)
    r3 = lambda a: a.reshape(bsz, 1, d)
    flat = lambda b, i: (0, b * nt + i)
    wr_t = w_router_l.T
    wr_hi = wr_t.astype(BF16)
    wr_lo = (wr_t - wr_hi.astype(F32)).astype(BF16)
    return pl.pallas_call(
        _out_proj_kernel,
        grid=(bsz, nt),
        in_specs=[pl.BlockSpec((1, tm, DA_WIDTH), tok),
                  pl.BlockSpec((1, tm, ML_WIDTH), tok),
                  pl.BlockSpec((tm, S5_WIDTH), lambda b, i: (i, b)),
                  pl.BlockSpec((1, tm, d), tok),
                  pl.BlockSpec((1, 1, d), per_b),
                  pl.BlockSpec((1, d), c2),
                  pl.BlockSpec((1, d), c2),
                  pl.BlockSpec((1, 1, d), per_b),
                  pl.BlockSpec((1, 1, d), per_b),
                  pl.BlockSpec((1, d, d), lambda b, i: (layer, 0, 0)),
                  pl.BlockSpec((2 * N_EXPERTS, d), c2),
                  pl.BlockSpec((N_EXPERTS, 1), c2)],
        out_specs=[pl.BlockSpec((1, tm, d), tok),
                   pl.BlockSpec((tm, d // 2), lambda b, i: (b * nt + i, 0)),
                   pl.BlockSpec((SUBLANES, tm), flat),
                   pl.BlockSpec((tm, LANES), lambda b, i: (b * nt + i, 0)),
                   pl.BlockSpec((N_EXPERTS, LANES), c2)],
        out_shape=[jax.ShapeDtypeStruct((bsz, s, d), F32),
                   jax.ShapeDtypeStruct((bsz * s, d // 2), jnp.int32),
                   jax.ShapeDtypeStruct((SUBLANES, bsz * s), jnp.int32),
                   jax.ShapeDtypeStruct((bsz * s, LANES), F32),
                   jax.ShapeDtypeStruct((N_EXPERTS, LANES), F32)],
        scratch_shapes=[pltpu.VMEM((d, d), BF16)],
        compiler_params=_cparams(("arbitrary", "arbitrary")),
        name="out_proj",
    )(y_da, y_ml, y_s5, x, r3(gate), ln_g.reshape(1, d), ln_b.reshape(1, d), r3(shift2), r3(scale2),
      w_out, jnp.concatenate([wr_hi, wr_lo], axis=0), b_router_l.reshape(N_EXPERTS, 1))


META_END, META_PAD, META_CNT = 0, 1, 2


def _route_kernel(eid_ref, cnt_ref, pos_ref, meta_ref, carry_s, start_s):
    i = pl.program_id(0)
    tb = eid_ref.shape[1]
    ntp = meta_ref.shape[1]
    tm = TM_MOE

    @pl.when(i == 0)
    def _():
        cnt = cnt_ref[...]
        padded = jnp.floor((cnt + (tm - 1)) * (1.0 / tm)) * tm
        er = lax.broadcasted_iota(jnp.int32, (N_EXPERTS, N_EXPERTS), 0)
        ec = lax.broadcasted_iota(jnp.int32, (N_EXPERTS, N_EXPERTS), 1)
        ends = jnp.dot((ec <= er).astype(F32), padded, preferred_element_type=F32, precision=HIGHEST)
        start_s[...] = ends - padded
        carry_s[...] = jnp.zeros_like(carry_s)
        lane = lax.broadcasted_iota(jnp.int32, (N_EXPERTS, ntp), 1)
        sub = lax.broadcasted_iota(jnp.int32, (N_EXPERTS, ntp), 0)
        diag = lane == sub

        def as_row(col):
            return jnp.sum(jnp.where(diag, col, 0.0), axis=0, keepdims=True)

        zero = jnp.zeros((SUBLANES - 3, ntp), F32)
        meta_ref[...] = jnp.concatenate([as_row(ends[:, 0:1]), as_row(padded[:, 0:1]), as_row(cnt[:, 0:1]), zero],
                                        axis=0).astype(jnp.int32)

    eid = eid_ref[...]
    eidx = lax.broadcasted_iota(jnp.int32, (N_EXPERTS, tb), 0)
    hot = [eidx == eid[k:k + 1, :] for k in range(TOP_K)]
    member = jnp.zeros((N_EXPERTS, tb), F32)
    for k in range(TOP_K):
        member = member + hot[k].astype(F32)
    ri = lax.broadcasted_iota(jnp.int32, (tb, tb), 0)
    ci = lax.broadcasted_iota(jnp.int32, (tb, tb), 1)
    triu = (ri <= ci).astype(BF16)
    incl = jnp.dot(member.astype(BF16), triu, preferred_element_type=F32)
    slot = incl - member + carry_s[:, 0:1] + start_s[:, 0:1]
    rows = [jnp.sum(jnp.where(hot[k], slot, 0.0), axis=0, keepdims=True) for k in range(TOP_K)]
    zr = jnp.zeros_like(rows[0])
    pos_ref[...] = jnp.concatenate(rows + [zr] * (SUBLANES - TOP_K), axis=0).astype(jnp.int32)
    carry_s[...] = carry_s[...] + jnp.sum(member, axis=1, keepdims=True)


def _route(eid, counts):
    t = eid.shape[1]
    tb = TB_RANK
    ntp = LANES
    pos8, meta = pl.pallas_call(
        _route_kernel,
        grid=(t // tb,),
        in_specs=[pl.BlockSpec((SUBLANES, tb), lambda i: (0, i)),
                  pl.BlockSpec((N_EXPERTS, LANES), lambda i: (0, 0))],
        out_specs=[pl.BlockSpec((SUBLANES, tb), lambda i: (0, i)),
                   pl.BlockSpec((SUBLANES, ntp), lambda i: (0, 0))],
        out_shape=[jax.ShapeDtypeStruct((SUBLANES, t), jnp.int32),
                   jax.ShapeDtypeStruct((SUBLANES, ntp), jnp.int32)],
        scratch_shapes=[pltpu.VMEM((N_EXPERTS, LANES), F32), pltpu.VMEM((N_EXPERTS, LANES), F32)],
        compiler_params=_cparams(("arbitrary",)),
        name="route",
    )(eid, counts)
    return pos8, meta


def _sc_workers():
    info = plsc.get_sparse_core_info()
    return info.num_cores, info.num_cores * info.num_subcores


def _dispatch(h2, pos8, n_rows):
    t, d = h2.shape
    n_cores, n_workers = _sc_workers()
    tpw = t // n_workers
    ch = SC_SCATTER_CHUNK
    mesh = plsc.VectorSubcoreMesh(core_axis_name="c", subcore_axis_name="s")

    @functools.partial(
        pl.kernel, mesh=mesh,
        out_type=jax.ShapeDtypeStruct((n_rows, d), h2.dtype),
        scratch_types=[pltpu.VMEM((ch,), jnp.int32)] * TOP_K + [pltpu.VMEM((ch, d), h2.dtype),
                                                                pltpu.SemaphoreType.DMA])
    def scatter_rows(h_hbm, pos_hbm, out_hbm, i0, i1, i2, i3, rows_v, sem):
        idx = (i0, i1, i2, i3)
        base = (lax.axis_index("s") * n_cores + lax.axis_index("c")) * tpw

        @pl.loop(0, tpw // ch)
        def _(i):
            off = base + i * ch
            pltpu.sync_copy(h_hbm.at[pl.ds(off, ch)], rows_v)
            for k in range(TOP_K):
                pltpu.sync_copy(pos_hbm.at[k, pl.ds(off, ch)], idx[k])
            copies = [pltpu.async_copy(rows_v, out_hbm.at[idx[k]], sem) for k in range(TOP_K)]
            for cp in copies:
                cp.wait()

    return scatter_rows(h2, pos8)


def _gather_expert_rows(ys, pos8):
    _, d = ys.shape
    t = pos8.shape[1]
    n_cores, n_workers = _sc_workers()
    tpw = t // n_workers
    ch = SC_GATHER_CHUNK
    mesh = plsc.VectorSubcoreMesh(core_axis_name="c", subcore_axis_name="s")

    @functools.partial(
        pl.kernel, mesh=mesh,
        out_type=jax.ShapeDtypeStruct((TOP_K, t, d), ys.dtype),
        scratch_types=([pltpu.VMEM((ch,), jnp.int32)] * 2 + [pltpu.VMEM((ch, d), ys.dtype)] * 2
                       + [pltpu.SemaphoreType.DMA] * 4))
    def gather_rows(ys_hbm, pos_hbm, out_hbm, idx0, idx1, rows0, rows1, g0, g1, w0, w1):
        idx, rows, gsem, wsem = (idx0, idx1), (rows0, rows1), (g0, g1), (w0, w1)
        base = (lax.axis_index("s") * n_cores + lax.axis_index("c")) * tpw
        items = [(i, k) for i in range(tpw // ch) for k in range(TOP_K)]

        def gather(n):
            i, k = items[n]
            pltpu.sync_copy(pos_hbm.at[k, pl.ds(base + i * ch, ch)], idx[n % 2])
            return pltpu.async_copy(ys_hbm.at[idx[n % 2]], rows[n % 2], gsem[n % 2])

        def write(n):
            i, k = items[n]
            return pltpu.async_copy(rows[n % 2], out_hbm.at[k, pl.ds(base + i * ch, ch)], wsem[n % 2])

        gathers, writes = {}, {}
        for n in range(len(items)):
            if n >= 2:
                writes[n - 2].wait()
            gathers[n] = gather(n)
            if n >= 1:
                gathers[n - 1].wait()
                writes[n - 1] = write(n - 1)
        last = len(items) - 1
        gathers[last].wait()
        writes[last] = write(last)
        if last >= 1:
            writes[last - 1].wait()
        writes[last].wait()

    return gather_rows(ys, pos8)


def _pack_bf16_pairs(a):
    return _pack_rounded(a.astype(BF16).astype(F32))


def _pack_rounded(r):
    n = r.shape[1] // 2
    lo = pltpu.bitcast(r[:, :n], jnp.int32)
    hi = pltpu.bitcast(r[:, n:], jnp.int32)
    return jnp.bitwise_or(jnp.bitwise_and(hi, -65536), jnp.bitwise_and(lax.shift_right_logical(lo, 16), 65535))


def _unpack_bf16_pairs(p):
    lo = pltpu.bitcast(lax.shift_left(p, 16), F32)
    hi = pltpu.bitcast(jnp.bitwise_and(p, -65536), F32)
    return lo, hi


def _expert_kernel(meta_ref, xs_ref, wup_ref, bup_ref, wdn_ref, bdn_ref, ys_ref,
                   wup_s, wdn_s, xbuf, obuf, in_sem, out_sem, g_s):
    e = pl.program_id(0)
    tm = TM_MOE
    half = D_MODEL // 2
    n_t = meta_ref[META_PAD, e] // tm
    cnt = meta_ref[META_CNT, e]
    n_total = meta_ref[META_END, N_EXPERTS - 1] // tm

    def rows(g):
        return pl.ds(pl.multiple_of(g * tm, tm), tm)

    def x_copy(g, slot):
        return pltpu.make_async_copy(xs_ref.at[rows(g)], xbuf.at[slot], in_sem.at[slot])

    def y_copy(g, slot):
        return pltpu.make_async_copy(obuf.at[slot], ys_ref.at[rows(g)], out_sem.at[slot])

    @pl.when(e == 0)
    def _():
        g_s[0] = 0
        x_copy(0, 0).start()

    @pl.when(n_t > 0)
    def _():
        g0 = g_s[0]
        wup_s[...] = wup_ref[0, 0].astype(BF16)
        wdn_s[...] = wdn_ref[0, 0].astype(BF16)

        def tile(i, carry):
            g = g0 + i
            slot = lax.rem(g, 2)
            x_copy(g, slot).wait()

            @pl.when(g + 1 < n_total)
            def _():
                x_copy(g + 1, 1 - slot).start()

            @pl.when(g >= 2)
            def _():
                y_copy(g - 2, slot).wait()

            row = lax.broadcasted_iota(jnp.int32, (tm, 1), 0)
            lo, hi = _unpack_bf16_pairs(jnp.where(row < cnt - i * tm, xbuf[slot], 0))
            z = (jnp.dot(lo.astype(BF16), wup_s[0:half, :], preferred_element_type=F32)
                 + jnp.dot(hi.astype(BF16), wup_s[half:, :], preferred_element_type=F32) + bup_ref[0, 0])
            glu = jnp.minimum(z[:, :D_EXPERT], SWIGLU_LIMIT)
            lin = jnp.clip(z[:, D_EXPERT:], -SWIGLU_LIMIT, SWIGLU_LIMIT)
            act = (glu * _sigmoid(SWIGLU_ALPHA * glu) * (lin + 1.0)).astype(BF16)
            y = jnp.dot(act, wdn_s[...], preferred_element_type=F32) + bdn_ref[0, 0]
            obuf[slot] = _pack_bf16_pairs(y)
            y_copy(g, slot).start()
            return carry

        lax.fori_loop(0, n_t, tile, 0)
        g_s[0] = g0 + n_t

    @pl.when(e == N_EXPERTS - 1)
    def _():
        y_copy(n_total - 2, lax.rem(n_total, 2)).wait()
        y_copy(n_total - 1, lax.rem(n_total - 1, 2)).wait()
        obuf[0] = jnp.zeros((tm, half), jnp.int32)

        def fill(i, carry):
            cp = pltpu.make_async_copy(obuf.at[0], ys_ref.at[pl.ds(pl.multiple_of(i * tm, tm), tm)], out_sem.at[0])
            cp.start()
            cp.wait()
            return carry

        lax.fori_loop(meta_ref[META_END, N_EXPERTS - 1] // tm, ys_ref.shape[0] // tm, fill, 0)


def _expert_mlp(xs, meta, layer, w_up, b_up, w_down, b_down):
    n_rows, half = xs.shape
    d = 2 * half
    tm = TM_MOE
    f = w_up.shape[-1]
    b_up4 = b_up.reshape(DEPTH, N_EXPERTS, 1, f)
    b_dn4 = b_down.reshape(DEPTH, N_EXPERTS, 1, d)
    wsel = lambda e, m: (layer, e, 0, 0)
    grid_spec = pltpu.PrefetchScalarGridSpec(
        num_scalar_prefetch=1,
        grid=(N_EXPERTS,),
        in_specs=[pl.BlockSpec(memory_space=pl.ANY),
                  pl.BlockSpec((1, 1, d, f), wsel),
                  pl.BlockSpec((1, 1, 1, f), wsel),
                  pl.BlockSpec((1, 1, f // 2, d), wsel),
                  pl.BlockSpec((1, 1, 1, d), wsel)],
        out_specs=pl.BlockSpec(memory_space=pl.ANY),
        scratch_shapes=[pltpu.VMEM((d, f), BF16), pltpu.VMEM((f // 2, d), BF16),
                        pltpu.VMEM((2, tm, half), jnp.int32), pltpu.VMEM((2, tm, half), jnp.int32),
                        pltpu.SemaphoreType.DMA((2,)), pltpu.SemaphoreType.DMA((2,)),
                        pltpu.SMEM((1,), jnp.int32)],
    )
    return pl.pallas_call(
        _expert_kernel,
        grid_spec=grid_spec,
        out_shape=jax.ShapeDtypeStruct((n_rows, half), jnp.int32),
        compiler_params=_cparams(("arbitrary",)),
        name="expert_mlp",
    )(meta, xs, w_up, b_up4, w_down, b_dn4)


def _combine_kernel(rows_ref, prob_ref, x_ref, gate_ref, lng_ref, lnb_ref, o_ref):
    p = prob_ref[...]
    y = None
    for k in range(TOP_K):
        yk = p[:, k:k + 1] * jnp.concatenate(_unpack_bf16_pairs(rows_ref[k]), axis=1)
        y = yk if y is None else y + yk
    o_ref[0] = _layer_norm(DN_ALPHA * x_ref[0] + (1.0 + gate_ref[0]) * y, lng_ref[...], lnb_ref[...])


def _combine(rows, prob_c, x1, gate, ln_g, ln_b):
    bsz, s, d = x1.shape
    tm = TM_DISP
    nt = s // tm
    return pl.pallas_call(
        _combine_kernel,
        grid=(bsz, nt),
        in_specs=[pl.BlockSpec((TOP_K, tm, d // 2), lambda b, i: (0, b * nt + i, 0)),
                  pl.BlockSpec((tm, LANES), lambda b, i: (b * nt + i, 0)),
                  pl.BlockSpec((1, tm, d), lambda b, i: (b, i, 0)),
                  pl.BlockSpec((1, 1, d), lambda b, i: (b, 0, 0)),
                  pl.BlockSpec((1, d), lambda b, i: (0, 0)),
                  pl.BlockSpec((1, d), lambda b, i: (0, 0))],
        out_specs=pl.BlockSpec((1, tm, d), lambda b, i: (b, i, 0)),
        out_shape=jax.ShapeDtypeStruct((bsz, s, d), F32),
        compiler_params=_cparams(("arbitrary", "arbitrary")),
        name="combine",
    )(rows, prob_c, x1, gate.reshape(bsz, 1, d), ln_g.reshape(1, d), ln_b.reshape(1, d))


def kernel(x, c, positions, ada_w, ada_b, w_in, lam_q1, lam_k1, lam_q2, lam_k2, da_norm_g, ml_conv_w, ml_conv_b,
           ml_w_q, ml_w_k, ml_gate_b, ml_norm_g, s5_a_re, s5_a_im, s5_log_dt, s5_b_re, s5_b_im, s5_c_re, s5_c_im,
           s5_d, s5_w_glu, w_out, ln_g, ln_b, w_router, b_router, w_up, b_up, w_down, b_down):
    bsz, s, d = x.shape
    t = bsz * s
    n_tiles_max = (t * TOP_K) // TM_MOE + N_EXPERTS
    n_rows = n_tiles_max * TM_MOE
    mod = _modulation(c, ada_w, ada_b)
    cos_t, sin_t = _rope_tables(positions)
    for l in range(DEPTH):
        shift, scale, gate = jnp.split(mod[2 * l], 3, axis=-1)
        q, k, v, mlx, mlv, mlo, g_t, g_c, s5u = _in_proj(x, shift, scale, cos_t, sin_t, w_in, l)
        lam_init = 0.8 - 0.6 * math.exp(-0.3 * l)
        lamv = jnp.stack([lam_q1[l], lam_k1[l], lam_q2[l], lam_k2[l]])
        y_da = _diff_attn(q, k, v, lamv, da_norm_g[l], lam_init)
        y_ml = _mlstm(mlx, mlv, mlo, g_t, g_c, ml_conv_w[l], ml_conv_b[l], ml_w_q[l], ml_w_k[l],
                      ml_gate_b[l], ml_norm_g[l])
        s5p = _s5_params(s5_a_re[l], s5_a_im[l], s5_log_dt[l], s5_b_re[l], s5_b_im[l], s5_c_re[l], s5_c_im[l],
                         s5_w_glu[l])
        y_s5 = _s5(s5u, bsz, s5p, s5_d[l])
        shift2, scale2, gate2 = jnp.split(mod[2 * l + 1], 3, axis=-1)
        x1, h2, eid, prob, counts = _out_proj(y_da, y_ml, y_s5, x, gate, ln_g[l, 0], ln_b[l, 0], shift2, scale2,
                                      w_out, l, w_router[l], b_router[l])
        pos8, meta = _route(eid, counts)
        xs = _dispatch(h2, pos8, n_rows)
        ys = _expert_mlp(xs, meta, l, w_up, b_up, w_down, b_down)
        rows = _gather_expert_rows(ys, pos8)
        x = _combine(rows, prob, x1, gate2, ln_g[l, 1], ln_b[l, 1])
    return x
```

```python
import functools
import math

import jax
import jax.numpy as jnp
from jax import lax
from jax.experimental import pallas as pl
from jax.experimental.pallas import tpu as pltpu
from jax.experimental.pallas import tpu_sc as plsc

F32 = jnp.float32
BF16 = jnp.bfloat16
HIGHEST = lax.Precision.HIGHEST

D_MODEL = 1024
DEPTH = 2
DA_HEADS = 4
DA_HEAD_DIM = 64
DA_V_DIM = 2 * DA_HEAD_DIM
DA_WIDTH = DA_HEADS * DA_V_DIM
DA_QK_WIDTH = DA_HEADS * 2 * DA_HEAD_DIM
ROPE_THETA = 10000.0
ML_HEADS = 4
ML_HEAD_DIM = 64
ML_WIDTH = ML_HEADS * ML_HEAD_DIM
ML_CONV = 4
S5_GROUP = 16
S5_STATE = 64
S5_WIDTH = D_MODEL - DA_WIDTH - ML_WIDTH
S5_GROUPS = S5_WIDTH // S5_GROUP
S5_NSTATE = S5_GROUPS * S5_STATE
N_EXPERTS = 32
TOP_K = 4
D_EXPERT = D_MODEL
SWIGLU_LIMIT = 7.0
SWIGLU_ALPHA = 1.702
DN_ALPHA = (2 * DEPTH) ** 0.25
LN_EPS = 1e-5
NEG = -1e30

OFF_DA_K = DA_QK_WIDTH
OFF_DA_V = 2 * DA_QK_WIDTH
OFF_ML_X = OFF_DA_V + DA_WIDTH
OFF_ML_V = OFF_ML_X + ML_WIDTH
OFF_ML_O = OFF_ML_V + ML_WIDTH
OFF_ML_I = OFF_ML_O + ML_WIDTH
OFF_ML_F = OFF_ML_I + ML_HEADS
OFF_S5_U = OFF_ML_F + ML_HEADS
N_IN = OFF_S5_U + S5_WIDTH

LANES = 128
SUBLANES = 8
VMEM_LIMIT_BYTES = 56 * 1024 * 1024

TM_PROJ = 1024
TQ = 512
ML_CHUNK = 256
ML_NB = 4
S5_TC = 256
S5_UNROLL = 8
TB_RANK = 1024
TM_MOE = 256
EXPERT_RING = 4
TM_DISP = 512
SC_SCATTER_CHUNK = 128
SC_GATHER_CHUNK = 64
GATE_PAD = 8
VT_ROWS = DA_V_DIM + 16
Q_PRESCALE = DA_HEAD_DIM ** -0.5 * math.log2(math.e)


def _cparams(sem, vmem=VMEM_LIMIT_BYTES):
    return pltpu.CompilerParams(dimension_semantics=sem, vmem_limit_bytes=vmem)


def _sigmoid(x):
    return 1.0 / (1.0 + jnp.exp(-x))


def _mod_kernel(c_ref, w_ref, b_ref, o_ref):
    c = c_ref[...]
    ca = (c * _sigmoid(c)).astype(BF16)
    w = w_ref[0, 0].astype(BF16)
    o_ref[0] = jnp.dot(ca, w, preferred_element_type=F32) + b_ref[0]


def _modulation(c, ada_w, ada_b):
    nsub = ada_w.shape[1]
    nmod = ada_w.shape[0] * nsub
    bsz, d = c.shape
    e = ada_w.shape[-1]
    tn = 1024
    b = ada_b.reshape(nmod, 1, e)
    return pl.pallas_call(
        _mod_kernel,
        grid=(nmod, e // tn),
        in_specs=[pl.BlockSpec((bsz, d), lambda n, j: (0, 0)),
                  pl.BlockSpec((1, 1, d, tn), lambda n, j: (n // nsub, n % nsub, 0, j)),
                  pl.BlockSpec((1, 1, tn), lambda n, j: (n, 0, j))],
        out_specs=pl.BlockSpec((1, bsz, tn), lambda n, j: (n, 0, j)),
        out_shape=jax.ShapeDtypeStruct((nmod, bsz, e), F32),
        compiler_params=_cparams(("arbitrary", "arbitrary")),
        name="modulation",
    )(c, ada_w, b)


def _rope_kernel(pos_ref, cos_ref, sin_ref):
    nfreq = DA_HEAD_DIM // 2
    pos = pos_ref[0].astype(F32)
    fidx = lax.broadcasted_iota(jnp.int32, (nfreq, 1), 0).astype(F32)
    inv = jnp.exp(fidx * (-2.0 * math.log(ROPE_THETA) / DA_HEAD_DIM))
    ang = inv * pos
    reps = LANES // nfreq
    cos_t = jnp.concatenate([jnp.cos(ang)] * reps, axis=0).T
    sin_t = jnp.concatenate([jnp.sin(ang)] * reps, axis=0).T
    lane = lax.broadcasted_iota(jnp.int32, (1, LANES), 1)
    sign = jnp.where((lane % DA_HEAD_DIM) < nfreq, -1.0, 1.0)
    cos_ref[0] = cos_t
    sin_ref[0] = sin_t * sign


def _rope_tables(positions):
    bsz, s = positions.shape
    ts = 512
    pos3 = positions.reshape(bsz, 1, s)
    return pl.pallas_call(
        _rope_kernel,
        grid=(bsz, s // ts),
        in_specs=[pl.BlockSpec((1, 1, ts), lambda b, i: (b, 0, i))],
        out_specs=[pl.BlockSpec((1, ts, LANES), lambda b, i: (b, i, 0))] * 2,
        out_shape=[jax.ShapeDtypeStruct((bsz, s, LANES), F32)] * 2,
        compiler_params=_cparams(("arbitrary", "arbitrary")),
        name="rope_tables",
    )(pos3)


def _in_proj_kernel(x_ref, shift_ref, scale_ref, cos_ref, sin_ref, win_ref,
                    q_ref, k_ref, vt_ref, mlx_ref, mlv_ref, mlo_ref, gt_ref, gc_ref, s5u_ref,
                    wqk_ref, wvt_ref, wrest_ref, wgt_ref, wgc_ref):
    @pl.when(jnp.logical_and(pl.program_id(0) == 0, pl.program_id(1) == 0))
    def _():
        wqk_ref[...] = win_ref[0, 0:OFF_DA_V, :].T.astype(BF16)
        wvt_ref[...] = win_ref[0, OFF_DA_V:OFF_ML_X, :].astype(BF16)
        wrest_ref[:, 0:3 * ML_WIDTH] = win_ref[0, OFF_ML_X:OFF_ML_I, :].T.astype(BF16)
        wrest_ref[:, 3 * ML_WIDTH:] = win_ref[0, OFF_S5_U:N_IN, :].T.astype(BF16)
        wgt_ref[...] = win_ref[0, OFF_ML_I:OFF_S5_U, :].astype(BF16)
        gslab = win_ref[0, OFF_ML_I:OFF_ML_I + LANES, :].T
        glane = lax.broadcasted_iota(jnp.int32, (1, LANES), 1)
        wgc_ref[...] = jnp.where(glane < GATE_PAD, gslab, 0.0).astype(BF16)

    h = (x_ref[0] * (1.0 + scale_ref[0]) + shift_ref[0]).astype(BF16)
    cos = cos_ref[0]
    sin = sin_ref[0]
    lane = lax.broadcasted_iota(jnp.int32, (1, LANES), 1)
    lo_half = (lane % DA_HEAD_DIM) < DA_HEAD_DIM // 2
    half = DA_HEAD_DIM // 2

    def rope(t):
        fwd = pltpu.roll(t, half, 1)
        bwd = pltpu.roll(t, LANES - half, 1)
        partner = jnp.where(lo_half, bwd, fwd)
        return t * cos + partner * sin

    qk = jnp.dot(h, wqk_ref[...], preferred_element_type=F32)
    nslab = DA_QK_WIDTH // LANES
    for c in range(nslab):
        q_ref[0, :, c * LANES:(c + 1) * LANES] = (
            rope(qk[:, c * LANES:(c + 1) * LANES]) * Q_PRESCALE).astype(BF16)
        k_ref[0, :, c * LANES:(c + 1) * LANES] = rope(
            qk[:, DA_QK_WIDTH + c * LANES:DA_QK_WIDTH + (c + 1) * LANES]).astype(BF16)

    vt = lax.dot_general(wvt_ref[...], h, (((1,), (1,)), ((), ())), preferred_element_type=F32)
    tm = h.shape[0]
    for hh in range(DA_HEADS):
        for jj in range(tm // TQ):
            vt_ref[0, hh, jj, 0:DA_V_DIM, :] = vt[hh * DA_V_DIM:(hh + 1) * DA_V_DIM,
                                                  jj * TQ:(jj + 1) * TQ].astype(BF16)
            vt_ref[0, hh, jj, DA_V_DIM:VT_ROWS, :] = jnp.ones((VT_ROWS - DA_V_DIM, TQ), BF16)

    r = jnp.dot(h, wrest_ref[...], preferred_element_type=F32)
    o = 0
    mlx_ref[0] = r[:, o:o + ML_WIDTH].astype(BF16); o += ML_WIDTH
    mlv_ref[0] = r[:, o:o + ML_WIDTH].astype(BF16); o += ML_WIDTH
    mlo_ref[0] = r[:, o:o + ML_WIDTH].astype(BF16); o += ML_WIDTH
    s5u_ref[...] = r[:, o:o + S5_WIDTH]
    gt = lax.dot_general(wgt_ref[...], h, (((1,), (1,)), ((), ())), preferred_element_type=F32)
    for c in range(h.shape[0] // ML_CHUNK):
        gt_ref[0, c] = gt[:, c * ML_CHUNK:(c + 1) * ML_CHUNK]
    gc_ref[0] = jnp.dot(h, wgc_ref[...], preferred_element_type=F32)


def _in_proj(x, shift, scale, cos_t, sin_t, w_in, layer):
    bsz, s, d = x.shape
    tm = TM_PROJ
    nrest = 3 * ML_WIDTH + S5_WIDTH
    shift3 = shift.reshape(bsz, 1, d)
    scale3 = scale.reshape(bsz, 1, d)
    tok = lambda b, i: (b, i, 0)
    per_b = lambda b, i: (b, 0, 0)
    out_shapes = [
        jax.ShapeDtypeStruct((bsz, s, DA_QK_WIDTH), BF16),
        jax.ShapeDtypeStruct((bsz, s, DA_QK_WIDTH), BF16),
        jax.ShapeDtypeStruct((bsz, DA_HEADS, s // TQ, VT_ROWS, TQ), BF16),
        jax.ShapeDtypeStruct((bsz, s, ML_WIDTH), BF16),
        jax.ShapeDtypeStruct((bsz, s, ML_WIDTH), BF16),
        jax.ShapeDtypeStruct((bsz, s, ML_WIDTH), BF16),
        jax.ShapeDtypeStruct((bsz, s // ML_CHUNK, GATE_PAD, ML_CHUNK), F32),
        jax.ShapeDtypeStruct((bsz, s, LANES), F32),
        jax.ShapeDtypeStruct((s, bsz * S5_WIDTH), F32),
    ]
    out_specs = [
        pl.BlockSpec((1, tm, DA_QK_WIDTH), tok),
        pl.BlockSpec((1, tm, DA_QK_WIDTH), tok),
        pl.BlockSpec((1, DA_HEADS, tm // TQ, VT_ROWS, TQ), lambda b, i: (b, 0, i, 0, 0)),
        pl.BlockSpec((1, tm, ML_WIDTH), tok),
        pl.BlockSpec((1, tm, ML_WIDTH), tok),
        pl.BlockSpec((1, tm, ML_WIDTH), tok),
        pl.BlockSpec((1, tm // ML_CHUNK, GATE_PAD, ML_CHUNK), lambda b, i: (b, i, 0, 0)),
        pl.BlockSpec((1, tm, LANES), tok),
        pl.BlockSpec((tm, S5_WIDTH), lambda b, i: (i, b)),
    ]
    return pl.pallas_call(
        _in_proj_kernel,
        grid=(bsz, s // tm),
        in_specs=[pl.BlockSpec((1, tm, d), tok),
                  pl.BlockSpec((1, 1, d), per_b),
                  pl.BlockSpec((1, 1, d), per_b),
                  pl.BlockSpec((1, tm, LANES), tok),
                  pl.BlockSpec((1, tm, LANES), tok),
                  pl.BlockSpec((1, N_IN, d), lambda b, i: (layer, 0, 0))],
        out_specs=out_specs,
        out_shape=out_shapes,
        scratch_shapes=[pltpu.VMEM((d, OFF_DA_V), BF16), pltpu.VMEM((DA_WIDTH, d), BF16),
                        pltpu.VMEM((d, nrest), BF16), pltpu.VMEM((GATE_PAD, d), BF16),
                        pltpu.VMEM((d, LANES), BF16)],
        compiler_params=_cparams(("arbitrary", "arbitrary")),
        name="in_proj",
    )(x, shift3, scale3, cos_t, sin_t, jnp.swapaxes(w_in, 1, 2))


def _diff_attn_kernel(lam_init, lamv_ref, gain_ref, q_ref, k_ref, vt_ref, o_ref, acc_s, m_s):
    qi = pl.program_id(2)
    tq = q_ref.shape[1]
    lane = lax.broadcasted_iota(jnp.int32, (1, LANES), 1)
    first = lane < DA_HEAD_DIM
    q = q_ref[0]
    zero = jnp.zeros_like(q)
    qm = (jnp.where(first, q, zero), jnp.where(first, zero, q))
    acc_s[...] = jnp.zeros_like(acc_s)
    m_s[...] = jnp.full(m_s.shape, NEG, F32)

    def step(j, nblk, masked):
        tk = nblk * tq
        kb = k_ref[0, pl.ds(pl.multiple_of(j * tq, tq), tk), :]
        vtb = vt_ref[0, 0, j] if nblk == 1 else jnp.concatenate([vt_ref[0, 0, j + b] for b in range(nblk)], axis=1)
        for c in range(2):
            st = lax.dot_general(kb, qm[c], (((1,), (1,)), ((), ())), preferred_element_type=F32)
            if masked:
                key_i = lax.broadcasted_iota(jnp.int32, (tk, tq), 0) - (tk - tq)
                qry_i = lax.broadcasted_iota(jnp.int32, (tk, tq), 1)
                st = jnp.where(key_i <= qry_i, st, NEG)
            m_prev = m_s[c]
            m_new = jnp.maximum(m_prev, jnp.max(st, axis=0, keepdims=True))
            alpha = jnp.exp2(m_prev - m_new)
            p = jnp.exp2(st - m_new).astype(BF16)
            acc_s[c] = alpha * acc_s[c] + jnp.dot(vtb, p, preferred_element_type=F32)
            m_s[c] = m_new

    def body(jj, carry):
        step(2 * jj, 2, False)
        return carry

    lax.fori_loop(0, qi // 2, body, 0)

    @pl.when(qi % 2 == 1)
    def _():
        step(qi - 1, 2, True)

    @pl.when(qi % 2 == 0)
    def _():
        step(qi, 1, True)

    outs = []
    for c in range(2):
        acc = acc_s[c]
        outs.append(acc[:DA_V_DIM] / acc[DA_V_DIM:DA_V_DIM + 1])

    lamv = lamv_ref[...]
    lam = (jnp.exp(jnp.sum(lamv[0:1] * lamv[1:2], axis=1, keepdims=True))
           - jnp.exp(jnp.sum(lamv[2:3] * lamv[3:4], axis=1, keepdims=True)) + lam_init)
    ot = outs[0] - lam * outs[1]
    ms = jnp.mean(ot * ot, axis=0, keepdims=True)
    ot = ot * (lax.rsqrt(ms + LN_EPS) * (1.0 - lam_init))
    o_ref[0] = (ot.T * gain_ref[...]).astype(o_ref.dtype)


def _diff_attn(q, k, vt, lamv, gain, lam_init):
    bsz, s, _ = q.shape
    tq = TQ
    nq = s // tq
    return pl.pallas_call(
        functools.partial(_diff_attn_kernel, lam_init),
        grid=(bsz, DA_HEADS, nq),
        in_specs=[pl.BlockSpec((4, DA_HEAD_DIM), lambda b, h, i: (0, 0)),
                  pl.BlockSpec((1, DA_V_DIM), lambda b, h, i: (0, 0)),
                  pl.BlockSpec((1, tq, DA_V_DIM), lambda b, h, i: (b, i, h)),
                  pl.BlockSpec((1, s, DA_V_DIM), lambda b, h, i: (b, 0, h)),
                  pl.BlockSpec((1, 1, nq, VT_ROWS, tq), lambda b, h, i: (b, h, 0, 0, 0))],
        out_specs=pl.BlockSpec((1, tq, DA_V_DIM), lambda b, h, i: (b, i, h)),
        out_shape=jax.ShapeDtypeStruct((bsz, s, DA_WIDTH), BF16),
        scratch_shapes=[pltpu.VMEM((2, VT_ROWS, tq), F32), pltpu.VMEM((2, 1, tq), F32)],
        compiler_params=_cparams(("arbitrary", "arbitrary", "arbitrary")),
        name="diff_attn",
    )(lamv, gain.reshape(1, DA_V_DIM), q, k, vt)


def _log_sigmoid(x):
    return jnp.minimum(x, 0.0) - jnp.log(1.0 + jnp.exp(-jnp.abs(x)))


def _split3(a):
    hi = a.astype(BF16)
    r1 = a - hi.astype(F32)
    mid = r1.astype(BF16)
    lo = (r1 - mid.astype(F32)).astype(BF16)
    return hi, mid, lo


def _mlstm_kernel(x_ref, v_ref, o_ref, gt_ref, gc_ref, cw_ref, cb_ref, wq_ref, wkt_ref, gbt_ref, gbc_ref,
                  ng_ref, hmean_ref, y_ref, xc_s, c_s, m_s):
    nb, s = x_ref.shape[0], x_ref.shape[1]
    L = ML_CHUNK
    H, dh = ML_HEADS, ML_HEAD_DIM
    nc = s // L
    cw = cw_ref[...]
    row = lax.broadcasted_iota(jnp.int32, (s, 1), 0)
    for bi in range(nb):
        x = x_ref[bi].astype(F32)
        xc = x * cw[ML_CONV - 1:ML_CONV]
        for j in range(1, ML_CONV):
            xs = jnp.where(row >= j, pltpu.roll(x, j, 0), 0.0)
            xc = xc + xs * cw[ML_CONV - 1 - j:ML_CONV - j]
        xc = xc + cb_ref[...]
        xc_s[bi] = (xc * _sigmoid(xc)).astype(BF16)

    c_s[...] = jnp.zeros_like(c_s)
    m_s[...] = jnp.full(m_s.shape, NEG, F32)

    ri = lax.broadcasted_iota(jnp.int32, (L, L), 0)
    ci = lax.broadcasted_iota(jnp.int32, (L, L), 1)
    causal = ci <= ri
    tril = causal.astype(BF16)
    triu = (ri <= ci).astype(BF16)
    lane = lax.broadcasted_iota(jnp.int32, (1, dh), 1)
    one_hot0 = jnp.broadcast_to((lane == 0).astype(BF16), (L, dh))

    def chunk_one(bi, ci_, t0):
        xcc = xc_s[bi, pl.ds(t0, L), :]
        qc = jnp.dot(xcc, wq_ref[...], preferred_element_type=F32).astype(BF16)
        ktc = lax.dot_general(wkt_ref[...], xcc, (((1,), (1,)), ((), ())),
                              preferred_element_type=F32)
        g_rows = gt_ref[bi, ci_] + gbt_ref[...]
        g_cols = gc_ref[bi, pl.ds(t0, L), :] + gbc_ref[...]
        lf_rows = _log_sigmoid(g_rows)
        lf_cols = _log_sigmoid(g_cols)
        r3 = jnp.dot(jnp.concatenate(_split3(lf_rows), axis=0), triu, preferred_element_type=F32)
        b_rows = r3[0:GATE_PAD] + r3[GATE_PAD:2 * GATE_PAD] + r3[2 * GATE_PAD:]
        c3 = jnp.dot(tril, jnp.concatenate(_split3(lf_cols), axis=1), preferred_element_type=F32)
        b_cols = c3[:, 0:LANES] + c3[:, LANES:2 * LANES] + c3[:, 2 * LANES:]
        vch = v_ref[bi, pl.ds(t0, L), :]
        och = o_ref[bi, pl.ds(t0, L), :].astype(F32)
        hs = []
        for h in range(H):
            br = b_rows[H + h:H + h + 1, :]
            ir = g_rows[h:h + 1, :]
            bc = b_cols[:, H + h:H + h + 1]
            m_prev = m_s[bi, h]
            log_d = jnp.where(causal, bc - br + ir, NEG)
            inter = bc + m_prev
            mx = jnp.maximum(inter, jnp.max(log_d, axis=1, keepdims=True))
            dmat = jnp.exp(log_d - mx)
            dec = jnp.exp(inter - mx)
            qh = qc[:, h * dh:(h + 1) * dh]
            kth = ktc[h * dh:(h + 1) * dh, :]
            vaug = jnp.concatenate([vch[:, h * dh:(h + 1) * dh], one_hot0], axis=1)
            sm = (jnp.dot(qh, kth.astype(BF16), preferred_element_type=F32) * dmat).astype(BF16)
            c_prev = c_s[bi, h]
            na = (jnp.dot(sm, vaug, preferred_element_type=F32)
                  + dec * jnp.dot(qh, c_prev.astype(BF16), preferred_element_type=F32))
            den = na[:, dh:dh + 1]
            hs.append(na[:, :dh] / jnp.maximum(jnp.abs(den), jnp.exp(-mx)))
            g_tot = br[:, L - 1:L]
            a_row = g_tot - br + ir
            m_new = jnp.maximum(g_tot + m_prev, jnp.max(a_row, axis=1, keepdims=True))
            decay = jnp.exp(g_tot + m_prev - m_new)
            w_row = jnp.exp(a_row - m_new)
            kw = (kth * w_row).astype(BF16)
            c_s[bi, h] = decay * c_prev + jnp.dot(kw, vaug, preferred_element_type=F32)
            m_s[bi, h] = m_new
        hcat = jnp.concatenate(hs, axis=1)
        m3 = jnp.dot(jnp.concatenate(_split3(hcat * hcat), axis=0), hmean_ref[...],
                     preferred_element_type=F32)
        ms = m3[0:L] + m3[L:2 * L] + m3[2 * L:]
        y = hcat * lax.rsqrt(ms + LN_EPS) * ng_ref[...] * _sigmoid(och)
        y_ref[bi, pl.ds(t0, L), :] = y.astype(y_ref.dtype)

    def chunk(ci_, _):
        t0 = pl.multiple_of(ci_ * L, L)
        for bi in range(nb):
            chunk_one(bi, ci_, t0)
        return 0

    lax.fori_loop(0, nc, chunk, 0)


def _mlstm(mlx, mlv, mlo, g_t, g_c, conv_w, conv_b, w_q, w_k, gate_b, norm_g):
    bsz, s, _ = mlx.shape
    H, dh = ML_HEADS, ML_HEAD_DIM
    eye = jnp.eye(H, dtype=F32)
    wq_bd = jnp.einsum('hde,hg->hdge', w_q, eye).reshape(ML_WIDTH, ML_WIDTH).astype(BF16)
    wk_bd = jnp.einsum('hde,hg->hdge', w_k * (dh ** -0.5), eye).reshape(ML_WIDTH, ML_WIDTH)
    wkt_bd = wk_bd.T.astype(BF16)
    gbt = gate_b.reshape(GATE_PAD, 1)
    gbc = jnp.pad(gate_b.reshape(1, GATE_PAD), ((0, 0), (0, LANES - GATE_PAD)))
    hmean = jnp.kron(eye, jnp.full((dh, dh), 1.0 / dh, F32)).astype(BF16)
    nc = s // ML_CHUNK
    tok = lambda b: (b, 0, 0)
    c2 = lambda b: (0, 0)
    nb = ML_NB
    return pl.pallas_call(
        _mlstm_kernel,
        grid=(bsz // nb,),
        in_specs=[pl.BlockSpec((nb, s, ML_WIDTH), tok),
                  pl.BlockSpec((nb, s, ML_WIDTH), tok),
                  pl.BlockSpec((nb, s, ML_WIDTH), tok),
                  pl.BlockSpec((nb, nc, GATE_PAD, ML_CHUNK), lambda b: (b, 0, 0, 0)),
                  pl.BlockSpec((nb, s, LANES), tok),
                  pl.BlockSpec((ML_CONV, ML_WIDTH), c2),
                  pl.BlockSpec((1, ML_WIDTH), c2),
                  pl.BlockSpec((ML_WIDTH, ML_WIDTH), c2),
                  pl.BlockSpec((ML_WIDTH, ML_WIDTH), c2),
                  pl.BlockSpec((GATE_PAD, 1), c2),
                  pl.BlockSpec((1, LANES), c2),
                  pl.BlockSpec((1, ML_WIDTH), c2),
                  pl.BlockSpec((ML_WIDTH, ML_WIDTH), c2)],
        out_specs=pl.BlockSpec((nb, s, ML_WIDTH), tok),
        out_shape=jax.ShapeDtypeStruct((bsz, s, ML_WIDTH), BF16),
        scratch_shapes=[pltpu.VMEM((nb, s, ML_WIDTH), BF16),
                        pltpu.VMEM((nb, H, dh, LANES), F32),
                        pltpu.VMEM((nb, H, 1, 1), F32)],
        compiler_params=_cparams(("arbitrary",)),
        name="mlstm",
    )(mlx, mlv, mlo, g_t, g_c, conv_w, conv_b.reshape(1, ML_WIDTH), wq_bd, wkt_bd, gbt, gbc,
      norm_g.reshape(1, ML_WIDTH), hmean)


def _gelu_tanh(x):
    return 0.5 * x * (1.0 + jnp.tanh(math.sqrt(2.0 / math.pi) * (x + 0.044715 * (x * x * x))))


def _s5_kernel(u_ref, are_ref, aim_ref, bcat_ref, ccat_ref, d_ref, wglu_ref, y_ref, xs_s, st_s):
    tc, bsz, w = u_ref.shape
    n = S5_NSTATE

    @pl.when(pl.program_id(0) == 0)
    def _():
        st_s[...] = jnp.zeros_like(st_s)

    u = u_ref[...].reshape(tc * bsz, w)
    xs_s[...] = jnp.dot(u.astype(BF16), bcat_ref[...], preferred_element_type=F32).reshape(tc, bsz, 2 * n)
    a_re = jnp.broadcast_to(are_ref[...], (bsz, n))
    a_im = jnp.broadcast_to(aim_ref[...], (bsz, n))

    def step(t, carry):
        x_re, x_im = carry
        bu = xs_s[t]
        n_re = a_re * x_re - a_im * x_im + bu[:, :n]
        n_im = a_re * x_im + a_im * x_re + bu[:, n:]
        xs_s[t] = jnp.concatenate([n_re, n_im], axis=1)
        return n_re, n_im

    x_re, x_im = lax.fori_loop(0, tc, step, (st_s[0], st_s[1]), unroll=S5_UNROLL)
    st_s[0] = x_re
    st_s[1] = x_im

    xs = xs_s[...].reshape(tc * bsz, 2 * n).astype(BF16)
    y = jnp.dot(xs, ccat_ref[...], preferred_element_type=F32) + d_ref[...] * u
    z = jnp.dot(_gelu_tanh(y).astype(BF16), wglu_ref[...], preferred_element_type=F32)
    out = z[:, :w] * _sigmoid(z[:, w:])
    y_ref[...] = out.reshape(tc, bsz, w).astype(y_ref.dtype)


def _s5_params(a_re, a_im, log_dt, b_re, b_im, c_re, c_im, w_glu):
    G, P, Hc = S5_GROUPS, S5_STATE, S5_GROUP
    dt = jnp.exp(log_dt)[:, None]
    mag = jnp.exp(a_re * dt)
    ab_re = mag * jnp.cos(a_im * dt)
    ab_im = mag * jnp.sin(a_im * dt)
    nr, ni = ab_re - 1.0, ab_im
    den = a_re * a_re + a_im * a_im
    fr = (nr * a_re + ni * a_im) / den
    fi = (ni * a_re - nr * a_im) / den
    bb_re = fr[..., None] * b_re - fi[..., None] * b_im
    bb_im = fr[..., None] * b_im + fi[..., None] * b_re
    eye = jnp.eye(G, dtype=F32)
    bd = lambda t, sub: jnp.einsum(sub, t, eye)
    bre = bd(bb_re, 'gph,gk->ghkp').reshape(G * Hc, G * P)
    bim = bd(bb_im, 'gph,gk->ghkp').reshape(G * Hc, G * P)
    bcat = jnp.concatenate([bre, bim], axis=1).astype(BF16)
    cre = bd(c_re, 'ghp,gk->gpkh').reshape(G * P, G * Hc)
    cim = bd(c_im, 'ghp,gk->gpkh').reshape(G * P, G * Hc)
    ccat = jnp.concatenate([cre, -cim], axis=0).astype(BF16)
    wv = bd(w_glu[:, :, :Hc], 'ghj,gk->ghkj').reshape(G * Hc, G * Hc)
    wg = bd(w_glu[:, :, Hc:], 'ghj,gk->ghkj').reshape(G * Hc, G * Hc)
    wglu = jnp.concatenate([wv, wg], axis=1).astype(BF16)
    return ab_re.reshape(1, G * P), ab_im.reshape(1, G * P), bcat, ccat, wglu


def _s5(u_tm, bsz, params, d_skip):
    s = u_tm.shape[0]
    w = S5_WIDTH
    n = S5_NSTATE
    are, aim, bcat, ccat, wglu = params
    u3 = u_tm.reshape(s, bsz, w)
    tc = S5_TC
    c2 = lambda i: (0, 0)
    y = pl.pallas_call(
        _s5_kernel,
        grid=(s // tc,),
        in_specs=[pl.BlockSpec((tc, bsz, w), lambda i: (i, 0, 0)),
                  pl.BlockSpec((1, n), c2),
                  pl.BlockSpec((1, n), c2),
                  pl.BlockSpec((w, 2 * n), c2),
                  pl.BlockSpec((2 * n, w), c2),
                  pl.BlockSpec((1, w), c2),
                  pl.BlockSpec((w, 2 * w), c2)],
        out_specs=pl.BlockSpec((tc, bsz, w), lambda i: (i, 0, 0)),
        out_shape=jax.ShapeDtypeStruct((s, bsz, w), F32),
        scratch_shapes=[pltpu.VMEM((tc, bsz, 2 * n), F32),
                        pltpu.VMEM((2, bsz, n), F32)],
        compiler_params=_cparams(("arbitrary",)),
        name="s5",
    )(u3, are, aim, bcat, ccat, d_skip.reshape(1, w), wglu)
    return y.reshape(s, bsz * w)


def _layer_norm(z, g, b):
    mu = jnp.mean(z, axis=1, keepdims=True)
    zc = z - mu
    var = jnp.mean(zc * zc, axis=1, keepdims=True)
    return zc * lax.rsqrt(var + LN_EPS) * g + b


def _out_proj_kernel(yda_ref, yml_ref, ys5_ref, x_ref, gate_ref, lng_ref, lnb_ref, shift_ref, scale_ref,
                     wout32_ref, wrt_ref, brt_ref, x1_ref, h2_ref, eid_ref, prob_ref, cnt_ref, wout_ref):
    first_step = jnp.logical_and(pl.program_id(0) == 0, pl.program_id(1) == 0)

    @pl.when(first_step)
    def _():
        wout_ref[...] = wout32_ref[0].astype(BF16)

    y = jnp.dot(yda_ref[0], wout_ref[0:DA_WIDTH, :], preferred_element_type=F32)
    y = y + jnp.dot(yml_ref[0], wout_ref[DA_WIDTH:DA_WIDTH + ML_WIDTH, :], preferred_element_type=F32)
    y = y + jnp.dot(ys5_ref[...].astype(BF16), wout_ref[DA_WIDTH + ML_WIDTH:, :], preferred_element_type=F32)
    x1 = _layer_norm(DN_ALPHA * x_ref[0] + (1.0 + gate_ref[0]) * y, lng_ref[...], lnb_ref[...])
    x1_ref[0] = x1
    h2 = x1 * (1.0 + scale_ref[0]) + shift_ref[0]
    h_hi = h2.astype(BF16)
    h_hi32 = h_hi.astype(F32)
    h2_ref[...] = _pack_rounded(h_hi32)
    h_lo = (h2 - h_hi32).astype(BF16)
    nt_dot = lambda a, b: lax.dot_general(a, b, (((1,), (1,)), ((), ())), preferred_element_type=F32)
    by_hi = nt_dot(wrt_ref[...], h_hi)
    logits = (by_hi[:N_EXPERTS] + by_hi[N_EXPERTS:] + nt_dot(wrt_ref[0:N_EXPERTS, :], h_lo)
              + brt_ref[...])
    eidx = lax.broadcasted_iota(jnp.int32, logits.shape, 0)
    vals, ids = [], []
    for _ in range(TOP_K):
        mx = jnp.max(logits, axis=0, keepdims=True)
        sel = jnp.min(jnp.where(logits == mx, eidx, N_EXPERTS), axis=0, keepdims=True)
        vals.append(mx)
        ids.append(sel)
        logits = jnp.where(eidx == sel, -jnp.inf, logits)
    ex = [jnp.exp(v - vals[0]) for v in vals]
    tot = ex[0] + ex[1] + ex[2] + ex[3]
    zi = jnp.zeros_like(ids[0])
    eid_ref[...] = jnp.concatenate(ids + [zi] * (SUBLANES - TOP_K), axis=0)

    @pl.when(first_step)
    def _():
        cnt_ref[...] = jnp.zeros_like(cnt_ref)

    member = jnp.zeros(logits.shape, F32)
    for sel in ids:
        member = member + (eidx == sel).astype(F32)
    cnt_ref[...] = cnt_ref[...] + jnp.sum(member, axis=1, keepdims=True)
    zf = jnp.zeros((LANES - TOP_K, tot.shape[1]), F32)
    prob_ref[...] = jnp.concatenate([e / tot for e in ex] + [zf], axis=0).T


def _out_proj(y_da, y_ml, y_s5, x, gate, ln_g, ln_b, shift2, scale2, w_out, layer, w_router_l, b_router_l):
    bsz, s, d = x.shape
    tm = TM_PROJ
    nt = s // tm
    tok = lambda b, i: (b, i, 0)
    per_b = lambda b, i: (b, 0, 0)
    c2 = lambda b, i: (0, 0)
    r3 = lambda a: a.reshape(bsz, 1, d)
    flat = lambda b, i: (0, b * nt + i)
    wr_t = w_router_l.T
    wr_hi = wr_t.astype(BF16)
    wr_lo = (wr_t - wr_hi.astype(F32)).astype(BF16)
    return pl.pallas_call(
        _out_proj_kernel,
        grid=(bsz, nt),
        in_specs=[pl.BlockSpec((1, tm, DA_WIDTH), tok),
                  pl.BlockSpec((1, tm, ML_WIDTH), tok),
                  pl.BlockSpec((tm, S5_WIDTH), lambda b, i: (i, b)),
                  pl.BlockSpec((1, tm, d), tok),
                  pl.BlockSpec((1, 1, d), per_b),
                  pl.BlockSpec((1, d), c2),
                  pl.BlockSpec((1, d), c2),
                  pl.BlockSpec((1, 1, d), per_b),
                  pl.BlockSpec((1, 1, d), per_b),
                  pl.BlockSpec((1, d, d), lambda b, i: (layer, 0, 0)),
                  pl.BlockSpec((2 * N_EXPERTS, d), c2),
                  pl.BlockSpec((N_EXPERTS, 1), c2)],
        out_specs=[pl.BlockSpec((1, tm, d), tok),
                   pl.BlockSpec((tm, d // 2), lambda b, i: (b * nt + i, 0)),
                   pl.BlockSpec((SUBLANES, tm), flat),
                   pl.BlockSpec((tm, LANES), lambda b, i: (b * nt + i, 0)),
                   pl.BlockSpec((N_EXPERTS, LANES), c2)],
        out_shape=[jax.ShapeDtypeStruct((bsz, s, d), F32),
                   jax.ShapeDtypeStruct((bsz * s, d // 2), jnp.int32),
                   jax.ShapeDtypeStruct((SUBLANES, bsz * s), jnp.int32),
                   jax.ShapeDtypeStruct((bsz * s, LANES), F32),
                   jax.ShapeDtypeStruct((N_EXPERTS, LANES), F32)],
        scratch_shapes=[pltpu.VMEM((d, d), BF16)],
        compiler_params=_cparams(("arbitrary", "arbitrary")),
        name="out_proj",
    )(y_da, y_ml, y_s5, x, r3(gate), ln_g.reshape(1, d), ln_b.reshape(1, d), r3(shift2), r3(scale2),
      w_out, jnp.concatenate([wr_hi, wr_lo], axis=0), b_router_l.reshape(N_EXPERTS, 1))


META_END, META_PAD, META_CNT = 0, 1, 2


def _route_kernel(eid_ref, cnt_ref, pos_ref, meta_ref, carry_s, start_s):
    i = pl.program_id(0)
    tb = eid_ref.shape[1]
    ntp = meta_ref.shape[1]
    tm = TM_MOE

    @pl.when(i == 0)
    def _():
        cnt = cnt_ref[...]
        padded = jnp.floor((cnt + (tm - 1)) * (1.0 / tm)) * tm
        er = lax.broadcasted_iota(jnp.int32, (N_EXPERTS, N_EXPERTS), 0)
        ec = lax.broadcasted_iota(jnp.int32, (N_EXPERTS, N_EXPERTS), 1)
        ends = jnp.dot((ec <= er).astype(F32), padded, preferred_element_type=F32, precision=HIGHEST)
        start_s[...] = ends - padded
        carry_s[...] = jnp.zeros_like(carry_s)
        lane = lax.broadcasted_iota(jnp.int32, (N_EXPERTS, ntp), 1)
        sub = lax.broadcasted_iota(jnp.int32, (N_EXPERTS, ntp), 0)
        diag = lane == sub

        def as_row(col):
            return jnp.sum(jnp.where(diag, col, 0.0), axis=0, keepdims=True)

        zero = jnp.zeros((SUBLANES - 3, ntp), F32)
        meta_ref[...] = jnp.concatenate([as_row(ends[:, 0:1]), as_row(padded[:, 0:1]), as_row(cnt[:, 0:1]), zero],
                                        axis=0).astype(jnp.int32)

    eid = eid_ref[...]
    eidx = lax.broadcasted_iota(jnp.int32, (N_EXPERTS, tb), 0)
    hot = [eidx == eid[k:k + 1, :] for k in range(TOP_K)]
    member = jnp.zeros((N_EXPERTS, tb), F32)
    for k in range(TOP_K):
        member = member + hot[k].astype(F32)
    ri = lax.broadcasted_iota(jnp.int32, (tb, tb), 0)
    ci = lax.broadcasted_iota(jnp.int32, (tb, tb), 1)
    triu = (ri <= ci).astype(BF16)
    incl = jnp.dot(member.astype(BF16), triu, preferred_element_type=F32)
    slot = incl - member + carry_s[:, 0:1] + start_s[:, 0:1]
    rows = [jnp.sum(jnp.where(hot[k], slot, 0.0), axis=0, keepdims=True) for k in range(TOP_K)]
    zr = jnp.zeros_like(rows[0])
    pos_ref[...] = jnp.concatenate(rows + [zr] * (SUBLANES - TOP_K), axis=0).astype(jnp.int32)
    carry_s[...] = carry_s[...] + jnp.sum(member, axis=1, keepdims=True)


def _route(eid, counts):
    t = eid.shape[1]
    tb = TB_RANK
    ntp = LANES
    pos8, meta = pl.pallas_call(
        _route_kernel,
        grid=(t // tb,),
        in_specs=[pl.BlockSpec((SUBLANES, tb), lambda i: (0, i)),
                  pl.BlockSpec((N_EXPERTS, LANES), lambda i: (0, 0))],
        out_specs=[pl.BlockSpec((SUBLANES, tb), lambda i: (0, i)),
                   pl.BlockSpec((SUBLANES, ntp), lambda i: (0, 0))],
        out_shape=[jax.ShapeDtypeStruct((SUBLANES, t), jnp.int32),
                   jax.ShapeDtypeStruct((SUBLANES, ntp), jnp.int32)],
        scratch_shapes=[pltpu.VMEM((N_EXPERTS, LANES), F32), pltpu.VMEM((N_EXPERTS, LANES), F32)],
        compiler_params=_cparams(("arbitrary",)),
        name="route",
    )(eid, counts)
    return pos8, meta


def _sc_workers():
    info = plsc.get_sparse_core_info()
    return info.num_cores, info.num_cores * info.num_subcores


def _dispatch(h2, pos8, n_rows):
    t, d = h2.shape
    n_cores, n_workers = _sc_workers()
    tpw = t // n_workers
    ch = SC_SCATTER_CHUNK
    mesh = plsc.VectorSubcoreMesh(core_axis_name="c", subcore_axis_name="s")

    @functools.partial(
        pl.kernel, mesh=mesh,
        out_type=jax.ShapeDtypeStruct((n_rows, d), h2.dtype),
        scratch_types=[pltpu.VMEM((ch,), jnp.int32)] * TOP_K + [pltpu.VMEM((ch, d), h2.dtype),
                                                                pltpu.SemaphoreType.DMA])
    def scatter_rows(h_hbm, pos_hbm, out_hbm, i0, i1, i2, i3, rows_v, sem):
        idx = (i0, i1, i2, i3)
        base = (lax.axis_index("s") * n_cores + lax.axis_index("c")) * tpw

        @pl.loop(0, tpw // ch)
        def _(i):
            off = base + i * ch
            pltpu.sync_copy(h_hbm.at[pl.ds(off, ch)], rows_v)
            for k in range(TOP_K):
                pltpu.sync_copy(pos_hbm.at[k, pl.ds(off, ch)], idx[k])
            copies = [pltpu.async_copy(rows_v, out_hbm.at[idx[k]], sem) for k in range(TOP_K)]
            for cp in copies:
                cp.wait()

    return scatter_rows(h2, pos8)


def _gather_expert_rows(ys, pos8):
    _, d = ys.shape
    t = pos8.shape[1]
    n_cores, n_workers = _sc_workers()
    tpw = t // n_workers
    ch = SC_GATHER_CHUNK
    mesh = plsc.VectorSubcoreMesh(core_axis_name="c", subcore_axis_name="s")

    @functools.partial(
        pl.kernel, mesh=mesh,
        out_type=jax.ShapeDtypeStruct((TOP_K, t, d), ys.dtype),
        scratch_types=([pltpu.VMEM((ch,), jnp.int32)] * 2 + [pltpu.VMEM((ch, d), ys.dtype)] * 2
                       + [pltpu.SemaphoreType.DMA] * 4))
    def gather_rows(ys_hbm, pos_hbm, out_hbm, idx0, idx1, rows0, rows1, g0, g1, w0, w1):
        idx, rows, gsem, wsem = (idx0, idx1), (rows0, rows1), (g0, g1), (w0, w1)
        base = (lax.axis_index("s") * n_cores + lax.axis_index("c")) * tpw
        items = [(i, k) for i in range(tpw // ch) for k in range(TOP_K)]

        def gather(n):
            i, k = items[n]
            pltpu.sync_copy(pos_hbm.at[k, pl.ds(base + i * ch, ch)], idx[n % 2])
            return pltpu.async_copy(ys_hbm.at[idx[n % 2]], rows[n % 2], gsem[n % 2])

        def write(n):
            i, k = items[n]
            return pltpu.async_copy(rows[n % 2], out_hbm.at[k, pl.ds(base + i * ch, ch)], wsem[n % 2])

        gathers, writes = {}, {}
        for n in range(len(items)):
            if n >= 2:
                writes[n - 2].wait()
            gathers[n] = gather(n)
            if n >= 1:
                gathers[n - 1].wait()
                writes[n - 1] = write(n - 1)
        last = len(items) - 1
        gathers[last].wait()
        writes[last] = write(last)
        if last >= 1:
            writes[last - 1].wait()
        writes[last].wait()

    return gather_rows(ys, pos8)


def _pack_bf16_pairs(a):
    return _pack_rounded(a.astype(BF16).astype(F32))


def _pack_rounded(r):
    n = r.shape[1] // 2
    lo = pltpu.bitcast(r[:, :n], jnp.int32)
    hi = pltpu.bitcast(r[:, n:], jnp.int32)
    return jnp.bitwise_or(jnp.bitwise_and(hi, -65536), jnp.bitwise_and(lax.shift_right_logical(lo, 16), 65535))


def _unpack_bf16_pairs(p):
    lo = pltpu.bitcast(lax.shift_left(p, 16), F32)
    hi = pltpu.bitcast(jnp.bitwise_and(p, -65536), F32)
    return lo, hi


def _expert_kernel(meta_ref, xs_ref, wup_ref, bup_ref, wdn_ref, bdn_ref, ys_ref,
                   wup_s, wdn_s, xbuf, obuf, in_sem, out_sem, g_s):
    e = pl.program_id(0)
    tm = TM_MOE
    half = D_MODEL // 2
    n_t = meta_ref[META_PAD, e] // tm
    cnt = meta_ref[META_CNT, e]
    n_total = meta_ref[META_END, N_EXPERTS - 1] // tm
    ring = EXPERT_RING

    def rows(g):
        return pl.ds(pl.multiple_of(g * tm, tm), tm)

    def x_copy(g, slot):
        return pltpu.make_async_copy(xs_ref.at[rows(g)], xbuf.at[slot], in_sem.at[slot])

    def y_copy(g, slot):
        return pltpu.make_async_copy(obuf.at[slot], ys_ref.at[rows(g)], out_sem.at[slot])

    @pl.when(e == 0)
    def _():
        g_s[0] = 0
        for g in range(ring - 1):
            x_copy(g, g).start()

    @pl.when(n_t > 0)
    def _():
        g0 = g_s[0]
        wup_s[...] = wup_ref[0, 0].astype(BF16)
        wdn_s[...] = wdn_ref[0, 0].astype(BF16)

        def tile(i, carry):
            g = g0 + i
            slot = lax.rem(g, ring)
            x_copy(g, slot).wait()

            @pl.when(g + ring - 1 < n_total)
            def _():
                x_copy(g + ring - 1, lax.rem(g + ring - 1, ring)).start()

            @pl.when(g >= ring)
            def _():
                y_copy(g - ring, slot).wait()

            row = lax.broadcasted_iota(jnp.int32, (tm, 1), 0)
            lo, hi = _unpack_bf16_pairs(jnp.where(row < cnt - i * tm, xbuf[slot], 0))
            z = (jnp.dot(lo.astype(BF16), wup_s[0:half, :], preferred_element_type=F32)
                 + jnp.dot(hi.astype(BF16), wup_s[half:, :], preferred_element_type=F32) + bup_ref[0, 0])
            glu = jnp.minimum(z[:, :D_EXPERT], SWIGLU_LIMIT)
            lin = jnp.clip(z[:, D_EXPERT:], -SWIGLU_LIMIT, SWIGLU_LIMIT)
            act = (glu * _sigmoid(SWIGLU_ALPHA * glu) * (lin + 1.0)).astype(BF16)
            y = jnp.dot(act, wdn_s[...], preferred_element_type=F32) + bdn_ref[0, 0]
            obuf[slot] = _pack_bf16_pairs(y)
            y_copy(g, slot).start()
            return carry

        lax.fori_loop(0, n_t, tile, 0)
        g_s[0] = g0 + n_t

    @pl.when(e == N_EXPERTS - 1)
    def _():
        for back in range(ring, 0, -1):
            y_copy(n_total - back, lax.rem(n_total - back, ring)).wait()
        obuf[0] = jnp.zeros((tm, half), jnp.int32)

        def fill(i, carry):
            cp = pltpu.make_async_copy(obuf.at[0], ys_ref.at[pl.ds(pl.multiple_of(i * tm, tm), tm)], out_sem.at[0])
            cp.start()
            cp.wait()
            return carry

        lax.fori_loop(meta_ref[META_END, N_EXPERTS - 1] // tm, ys_ref.shape[0] // tm, fill, 0)


def _expert_mlp(xs, meta, layer, w_up, b_up, w_down, b_down):
    n_rows, half = xs.shape
    d = 2 * half
    tm = TM_MOE
    f = w_up.shape[-1]
    b_up4 = b_up.reshape(DEPTH, N_EXPERTS, 1, f)
    b_dn4 = b_down.reshape(DEPTH, N_EXPERTS, 1, d)
    wsel = lambda e, m: (layer, e, 0, 0)
    grid_spec = pltpu.PrefetchScalarGridSpec(
        num_scalar_prefetch=1,
        grid=(N_EXPERTS,),
        in_specs=[pl.BlockSpec(memory_space=pl.ANY),
                  pl.BlockSpec((1, 1, d, f), wsel),
                  pl.BlockSpec((1, 1, 1, f), wsel),
                  pl.BlockSpec((1, 1, f // 2, d), wsel),
                  pl.BlockSpec((1, 1, 1, d), wsel)],
        out_specs=pl.BlockSpec(memory_space=pl.ANY),
        scratch_shapes=[pltpu.VMEM((d, f), BF16), pltpu.VMEM((f // 2, d), BF16),
                        pltpu.VMEM((EXPERT_RING, tm, half), jnp.int32), pltpu.VMEM((EXPERT_RING, tm, half), jnp.int32),
                        pltpu.SemaphoreType.DMA((EXPERT_RING,)), pltpu.SemaphoreType.DMA((EXPERT_RING,)),
                        pltpu.SMEM((1,), jnp.int32)],
    )
    return pl.pallas_call(
        _expert_kernel,
        grid_spec=grid_spec,
        out_shape=jax.ShapeDtypeStruct((n_rows, half), jnp.int32),
        compiler_params=_cparams(("arbitrary",)),
        name="expert_mlp",
    )(meta, xs, w_up, b_up4, w_down, b_dn4)


def _combine_kernel(rows_ref, prob_ref, x_ref, gate_ref, lng_ref, lnb_ref, o_ref):
    p = prob_ref[...]
    y = None
    for k in range(TOP_K):
        yk = p[:, k:k + 1] * jnp.concatenate(_unpack_bf16_pairs(rows_ref[k]), axis=1)
        y = yk if y is None else y + yk
    o_ref[0] = _layer_norm(DN_ALPHA * x_ref[0] + (1.0 + gate_ref[0]) * y, lng_ref[...], lnb_ref[...])


def _combine(rows, prob_c, x1, gate, ln_g, ln_b):
    bsz, s, d = x1.shape
    tm = TM_DISP
    nt = s // tm
    return pl.pallas_call(
        _combine_kernel,
        grid=(bsz, nt),
        in_specs=[pl.BlockSpec((TOP_K, tm, d // 2), lambda b, i: (0, b * nt + i, 0)),
                  pl.BlockSpec((tm, LANES), lambda b, i: (b * nt + i, 0)),
                  pl.BlockSpec((1, tm, d), lambda b, i: (b, i, 0)),
                  pl.BlockSpec((1, 1, d), lambda b, i: (b, 0, 0)),
                  pl.BlockSpec((1, d), lambda b, i: (0, 0)),
                  pl.BlockSpec((1, d), lambda b, i: (0, 0))],
        out_specs=pl.BlockSpec((1, tm, d), lambda b, i: (b, i, 0)),
        out_shape=jax.ShapeDtypeStruct((bsz, s, d), F32),
        compiler_params=_cparams(("arbitrary", "arbitrary")),
        name="combine",
    )(rows, prob_c, x1, gate.reshape(bsz, 1, d), ln_g.reshape(1, d), ln_b.reshape(1, d))


def kernel(x, c, positions, ada_w, ada_b, w_in, lam_q1, lam_k1, lam_q2, lam_k2, da_norm_g, ml_conv_w, ml_conv_b,
           ml_w_q, ml_w_k, ml_gate_b, ml_norm_g, s5_a_re, s5_a_im, s5_log_dt, s5_b_re, s5_b_im, s5_c_re, s5_c_im,
           s5_d, s5_w_glu, w_out, ln_g, ln_b, w_router, b_router, w_up, b_up, w_down, b_down):
    bsz, s, d = x.shape
    t = bsz * s
    n_tiles_max = (t * TOP_K) // TM_MOE + N_EXPERTS
    n_rows = n_tiles_max * TM_MOE
    mod = _modulation(c, ada_w, ada_b)
    cos_t, sin_t = _rope_tables(positions)
    for l in range(DEPTH):
        shift, scale, gate = jnp.split(mod[2 * l], 3, axis=-1)
        q, k, v, mlx, mlv, mlo, g_t, g_c, s5u = _in_proj(x, shift, scale, cos_t, sin_t, w_in, l)
        lam_init = 0.8 - 0.6 * math.exp(-0.3 * l)
        lamv = jnp.stack([lam_q1[l], lam_k1[l], lam_q2[l], lam_k2[l]])
        y_da = _diff_attn(q, k, v, lamv, da_norm_g[l], lam_init)
        y_ml = _mlstm(mlx, mlv, mlo, g_t, g_c, ml_conv_w[l], ml_conv_b[l], ml_w_q[l], ml_w_k[l],
                      ml_gate_b[l], ml_norm_g[l])
        s5p = _s5_params(s5_a_re[l], s5_a_im[l], s5_log_dt[l], s5_b_re[l], s5_b_im[l], s5_c_re[l], s5_c_im[l],
                         s5_w_glu[l])
        y_s5 = _s5(s5u, bsz, s5p, s5_d[l])
        shift2, scale2, gate2 = jnp.split(mod[2 * l + 1], 3, axis=-1)
        x1, h2, eid, prob, counts = _out_proj(y_da, y_ml, y_s5, x, gate, ln_g[l, 0], ln_b[l, 0], shift2, scale2,
                                      w_out, l, w_router[l], b_router[l])
        pos8, meta = _route(eid, counts)
        xs = _dispatch(h2, pos8, n_rows)
        ys = _expert_mlp(xs, meta, l, w_up, b_up, w_down, b_down)
        rows = _gather_expert_rows(ys, pos8)
        x = _combine(rows, prob, x1, gate2, ln_g[l, 1], ln_b[l, 1])
    return x
```
